```python
import jax, jax.numpy as jnp
from jax import lax
import numpy as np

D_MODEL = 1024
BATCH = 8
SEQ = 4096
DEPTH = 2

HEAD_DIM = 64
N_HEADS = D_MODEL // HEAD_DIM
N_SB_HEADS = N_HEADS // 2
N_CA_HEADS = N_HEADS - N_SB_HEADS
D_SB = N_SB_HEADS * HEAD_DIM
D_CA = N_CA_HEADS * HEAD_DIM
D_IN = 3 * D_SB + 3 * D_CA
D_FF = 4 * D_MODEL
CHUNK = 64
LEFT_CHUNKS = 8
BAND = (LEFT_CHUNKS + 1) * CHUNK
REL_CLIP = 128
N_REL = 2 * REL_CLIP + 1
Q_BLOCK = 128
EPS = 1e-6
NEG_INF = -1e30

kernel_name = "hybrid_stickbreak_chunkrel_adaln_encoder"


def rmsnorm(x, g):
    xf = x.astype(jnp.float32)
    y = xf * lax.rsqrt(jnp.mean(xf * xf, axis=-1, keepdims=True) + EPS)
    return (y * g.astype(jnp.float32)).astype(x.dtype)


def stick_breaking_attention(q, k, v):
    B, S, H, d = q.shape
    scale = d ** -0.5
    outs = []
    for start in range(0, S, Q_BLOCK):
        end = start + Q_BLOCK
        qb = q[:, start:end]
        kb = k[:, :end]
        vb = v[:, :end]
        z = jnp.einsum('bqhd,bkhd->bhqk', qb, kb).astype(jnp.float32) * scale
        t_idx = start + jnp.arange(Q_BLOCK)[:, None]
        s_idx = jnp.arange(end)[None, :]
        strict = s_idx < t_idx
        log_beta = jax.nn.log_sigmoid(z)
        log_1m_beta = jnp.where(strict, jax.nn.log_sigmoid(-z), 0.0)
        suffix = lax.cumsum(log_1m_beta, axis=3, reverse=True) - log_1m_beta
        w = jnp.where(strict, jnp.exp(log_beta + suffix), 0.0)
        outs.append(jnp.einsum('bhqk,bkhd->bqhd', w.astype(v.dtype), vb))
    return jnp.concatenate(outs, axis=1)


def chunked_relpos_attention(q, k, v, rel_bias):
    B, S, H, d = q.shape
    nc = S // CHUNK
    pad = LEFT_CHUNKS * CHUNK
    kp = jnp.pad(k, ((0, 0), (pad, 0), (0, 0), (0, 0))).reshape(B, nc + LEFT_CHUNKS, CHUNK, H, d)
    vp = jnp.pad(v, ((0, 0), (pad, 0), (0, 0), (0, 0))).reshape(B, nc + LEFT_CHUNKS, CHUNK, H, d)
    k_band = jnp.concatenate([kp[:, i:i + nc] for i in range(LEFT_CHUNKS + 1)], axis=2)
    v_band = jnp.concatenate([vp[:, i:i + nc] for i in range(LEFT_CHUNKS + 1)], axis=2)
    qc = q.reshape(B, nc, CHUNK, H, d)
    s = jnp.einsum('bnqhd,bnkhd->bnhqk', qc, k_band).astype(jnp.float32) * (d ** -0.5)
    qi = jnp.arange(CHUNK)[:, None]
    kj = jnp.arange(BAND)[None, :]
    rel = qi + pad - kj
    rel_idx = jnp.clip(rel, -REL_CLIP, REL_CLIP) + REL_CLIP
    bias = rel_bias[:, rel_idx].astype(jnp.float32)
    s = s + bias[None, None]
    key_pos = jnp.arange(nc)[:, None] * CHUNK + kj - pad
    valid = key_pos >= 0
    s = jnp.where(valid[None, :, None, None, :], s, NEG_INF)
    p = jax.nn.softmax(s, axis=-1)
    out = jnp.einsum('bnhqk,bnkhd->bnqhd', p.astype(v.dtype), v_band)
    return out.reshape(B, S, H, d)


def _fwd_setup_inputs(seed: int = 0) -> dict:
    key = jax.random.key(seed)
    ks = jax.random.split(key, 16)
    f32 = jnp.float32
    x = jax.random.normal(ks[0], (BATCH, SEQ, D_MODEL), f32)
    c = jax.random.normal(ks[1], (BATCH, D_MODEL), f32)
    g_norm1 = 1.0 + 0.01 * jax.random.normal(ks[2], (DEPTH, D_MODEL), f32)
    w_in = jax.random.normal(ks[3], (DEPTH, D_MODEL, D_IN), f32) * D_MODEL ** -0.5
    g_q = 1.0 + 0.01 * jax.random.normal(ks[4], (DEPTH, HEAD_DIM), f32)
    g_k = 1.0 + 0.01 * jax.random.normal(ks[5], (DEPTH, HEAD_DIM), f32)
    rel_bias = 0.1 * jax.random.normal(ks[6], (DEPTH, N_CA_HEADS, N_REL), f32)
    w_o = jax.random.normal(ks[7], (DEPTH, D_MODEL, D_MODEL), f32) * D_MODEL ** -0.5
    g_norm2 = 1.0 + 0.01 * jax.random.normal(ks[8], (DEPTH, D_MODEL), f32)
    w1 = jax.random.normal(ks[9], (DEPTH, D_MODEL, D_FF), f32) * D_MODEL ** -0.5
    w2 = jax.random.normal(ks[10], (DEPTH, D_FF, D_MODEL), f32) * D_FF ** -0.5
    w_ada = jax.random.normal(ks[11], (DEPTH, D_MODEL, 6 * D_MODEL), f32) * (0.5 * D_MODEL ** -0.5)
    b_ada = 0.01 * jax.random.normal(ks[12], (DEPTH, 6 * D_MODEL), f32)
    return {"x": x, "c": c, "g_norm1": g_norm1, "w_in": w_in, "g_q": g_q, "g_k": g_k,
            "rel_bias": rel_bias, "w_o": w_o, "g_norm2": g_norm2, "w1": w1, "w2": w2,
            "w_ada": w_ada, "b_ada": b_ada}


def _fwd_reference(x, c, g_norm1, w_in, g_q, g_k, rel_bias, w_o, g_norm2, w1, w2, w_ada, b_ada):
    B, S, D = x.shape
    split_pts = [D_SB, 2 * D_SB, 3 * D_SB, 3 * D_SB + D_CA, 3 * D_SB + 2 * D_CA]
    c_act = jax.nn.silu(c)
    for l in range(DEPTH):
        mod = c_act @ w_ada[l] + b_ada[l]
        sh1, sc1, gt1, sh2, sc2, gt2 = [m[:, None, :] for m in jnp.split(mod, 6, axis=-1)]
        h = rmsnorm(x, g_norm1[l]) * (1.0 + sc1) + sh1
        proj = h @ w_in[l]
        q_sb, k_sb, v_sb, q_ca, k_ca, v_ca = jnp.split(proj, split_pts, axis=-1)
        hs = lambda t, n: t.reshape(B, S, n, HEAD_DIM)
        o_sb = stick_breaking_attention(hs(q_sb, N_SB_HEADS), hs(k_sb, N_SB_HEADS), hs(v_sb, N_SB_HEADS))
        q_ca = rmsnorm(hs(q_ca, N_CA_HEADS), g_q[l])
        k_ca = rmsnorm(hs(k_ca, N_CA_HEADS), g_k[l])
        o_ca = chunked_relpos_attention(q_ca, k_ca, hs(v_ca, N_CA_HEADS), rel_bias[l])
        mixed = jnp.concatenate([o_sb.reshape(B, S, D_SB), o_ca.reshape(B, S, D_CA)], axis=-1)
        x = x + gt1 * (mixed @ w_o[l])
        h = rmsnorm(x, g_norm2[l]) * (1.0 + sc2) + sh2
        x = x + gt2 * (jnp.square(jax.nn.relu(h @ w1[l])) @ w2[l])
    return x


import jax as _jax
import jax.numpy as _jnp

TWIN_FORMAT = 'train_step'
FWD_PARAMS = ['x', 'c', 'g_norm1', 'w_in', 'g_q', 'g_k', 'rel_bias', 'w_o', 'g_norm2', 'w1', 'w2', 'w_ada', 'b_ada']
TWIN_WEIGHTS = ['g_norm1', 'w_in', 'g_q', 'g_k', 'rel_bias', 'w_o', 'g_norm2', 'w1', 'w2', 'w_ada', 'b_ada']
TWIN_DIFF_INPUT = 'x'
TWIN_INPUTS = ['x', 'c', 'g_norm1', 'w_in', 'g_q', 'g_k', 'rel_bias', 'w_o', 'g_norm2', 'w1', 'w2', 'w_ada', 'b_ada', 'loss_target', 'm_g_norm1', 'm_w_in', 'm_g_q', 'm_g_k', 'm_rel_bias', 'm_w_o', 'm_g_norm2', 'm_w1', 'm_w2', 'm_w_ada', 'm_b_ada', 'v_g_norm1', 'v_w_in', 'v_g_q', 'v_g_k', 'v_rel_bias', 'v_w_o', 'v_g_norm2', 'v_w1', 'v_w2', 'v_w_ada', 'v_b_ada']
TWIN_OUTPUTS = ['loss', 'grad_x', 'grad_g_norm1', 'grad_w_in', 'grad_g_q', 'grad_g_k', 'grad_rel_bias', 'grad_w_o', 'grad_g_norm2', 'grad_w1', 'grad_w2', 'grad_w_ada', 'grad_b_ada', 'delta_g_norm1', 'delta_w_in', 'delta_g_q', 'delta_g_k', 'delta_rel_bias', 'delta_w_o', 'delta_g_norm2', 'delta_w1', 'delta_w2', 'delta_w_ada', 'delta_b_ada', 'new_m_g_norm1', 'new_m_w_in', 'new_m_g_q', 'new_m_g_k', 'new_m_rel_bias', 'new_m_w_o', 'new_m_g_norm2', 'new_m_w1', 'new_m_w2', 'new_m_w_ada', 'new_m_b_ada', 'new_v_g_norm1', 'new_v_w_in', 'new_v_g_q', 'new_v_g_k', 'new_v_rel_bias', 'new_v_w_o', 'new_v_g_norm2', 'new_v_w1', 'new_v_w2', 'new_v_w_ada', 'new_v_b_ada']
TWIN_LEAF_KINDS = {'loss': 'loss', 'grad_x': 'grad_x', 'grad_g_norm1': 'grad_w', 'grad_w_in': 'grad_w', 'grad_g_q': 'grad_w', 'grad_g_k': 'grad_w', 'grad_rel_bias': 'grad_w', 'grad_w_o': 'grad_w', 'grad_g_norm2': 'grad_w', 'grad_w1': 'grad_w', 'grad_w2': 'grad_w', 'grad_w_ada': 'grad_w', 'grad_b_ada': 'grad_w', 'delta_g_norm1': 'delta_w', 'delta_w_in': 'delta_w', 'delta_g_q': 'delta_w', 'delta_g_k': 'delta_w', 'delta_rel_bias': 'delta_w', 'delta_w_o': 'delta_w', 'delta_g_norm2': 'delta_w', 'delta_w1': 'delta_w', 'delta_w2': 'delta_w', 'delta_w_ada': 'delta_w', 'delta_b_ada': 'delta_w', 'new_m_g_norm1': 'new_m', 'new_m_w_in': 'new_m', 'new_m_g_q': 'new_m', 'new_m_g_k': 'new_m', 'new_m_rel_bias': 'new_m', 'new_m_w_o': 'new_m', 'new_m_g_norm2': 'new_m', 'new_m_w1': 'new_m', 'new_m_w2': 'new_m', 'new_m_w_ada': 'new_m', 'new_m_b_ada': 'new_m', 'new_v_g_norm1': 'new_v', 'new_v_w_in': 'new_v', 'new_v_g_q': 'new_v', 'new_v_g_k': 'new_v', 'new_v_rel_bias': 'new_v', 'new_v_w_o': 'new_v', 'new_v_g_norm2': 'new_v', 'new_v_w1': 'new_v', 'new_v_w2': 'new_v', 'new_v_w_ada': 'new_v', 'new_v_b_ada': 'new_v'}


def _forward(args):
    return _fwd_reference(*[args[k] for k in FWD_PARAMS])


def _output_shape():
    def fwd():
        inp = _fwd_setup_inputs(0)
        return _fwd_reference(*[inp[k] for k in FWD_PARAMS])
    out = _jax.eval_shape(fwd)
    return out.shape, out.dtype

N_MICROBATCH = 1
ADAM_LR = 0.001
ADAM_B1 = 0.9
ADAM_B2 = 0.999
ADAM_EPS = 1e-08
ADAM_WD = 0.01
ADAM_STEP = 10
PER_EXAMPLE_BATCH_AXIS = {'x': 0, 'c': 0, 'loss_target': 0}
SHARED_INPUTS = []
_WEIGHT_DTYPES = {'g_norm1': _jnp.float32, 'w_in': _jnp.float32, 'g_q': _jnp.float32, 'g_k': _jnp.float32, 'rel_bias': _jnp.float32, 'w_o': _jnp.float32, 'g_norm2': _jnp.float32, 'w1': _jnp.float32, 'w2': _jnp.float32, 'w_ada': _jnp.float32, 'b_ada': _jnp.float32}
MOMENT_SCALE = {'g_norm1': 1.034321e+00, 'w_in': 2.898403e-01, 'g_q': 1.104153e-01, 'g_k': 1.104077e-01, 'rel_bias': 7.759282e-03, 'w_o': 4.807143e-01, 'g_norm2': 1.139299e+01, 'w1': 3.654553e-01, 'w2': 1.423881e+00, 'w_ada': 2.516917e+00, 'b_ada': 6.260582e+00}


def _to_microbatches(a, axis):
    t = _jnp.moveaxis(a, axis, 0)
    t = t.reshape((N_MICROBATCH, t.shape[0] // N_MICROBATCH) + t.shape[1:])
    return _jnp.moveaxis(t, 1, axis + 1)


def setup_inputs(seed: int = 0) -> dict:
    inp = _fwd_setup_inputs(seed)
    key = _jax.random.fold_in(_jax.random.key(seed), 7919)
    shape, _ = _output_shape()
    out = dict(inp)
    out["loss_target"] = _jax.random.normal(_jax.random.fold_in(key, 0), shape, _jnp.float32)
    for i, name in enumerate(TWIN_WEIGHTS):
        w = inp[name].astype(_jnp.float32)
        if MOMENT_SCALE is None:
            s = _jnp.sqrt(_jnp.mean(_jnp.square(w)) + 1e-30)
        else:
            s = MOMENT_SCALE[name]
        km, kv = _jax.random.split(_jax.random.fold_in(key, i + 1))
        out[name] = w
        out["m_" + name] = s * _jax.random.normal(km, w.shape, _jnp.float32)
        out["v_" + name] = (s * s) * _jax.random.uniform(kv, w.shape, _jnp.float32, 0.5, 1.5)
    if N_MICROBATCH > 1:
        for name, axis in PER_EXAMPLE_BATCH_AXIS.items():
            out[name] = _to_microbatches(out[name], axis)
    return {'x': out['x'], 'c': out['c'], 'g_norm1': out['g_norm1'], 'w_in': out['w_in'], 'g_q': out['g_q'], 'g_k': out['g_k'], 'rel_bias': out['rel_bias'], 'w_o': out['w_o'], 'g_norm2': out['g_norm2'], 'w1': out['w1'], 'w2': out['w2'], 'w_ada': out['w_ada'], 'b_ada': out['b_ada'], 'loss_target': out['loss_target'], 'm_g_norm1': out['m_g_norm1'], 'm_w_in': out['m_w_in'], 'm_g_q': out['m_g_q'], 'm_g_k': out['m_g_k'], 'm_rel_bias': out['m_rel_bias'], 'm_w_o': out['m_w_o'], 'm_g_norm2': out['m_g_norm2'], 'm_w1': out['m_w1'], 'm_w2': out['m_w2'], 'm_w_ada': out['m_w_ada'], 'm_b_ada': out['m_b_ada'], 'v_g_norm1': out['v_g_norm1'], 'v_w_in': out['v_w_in'], 'v_g_q': out['v_g_q'], 'v_g_k': out['v_g_k'], 'v_rel_bias': out['v_rel_bias'], 'v_w_o': out['v_w_o'], 'v_g_norm2': out['v_g_norm2'], 'v_w1': out['v_w1'], 'v_w2': out['v_w2'], 'v_w_ada': out['v_w_ada'], 'v_b_ada': out['v_b_ada']}


def _loss(weights, diff, rest, loss_target):
    with _jax.named_scope("forward"):
        args = {**rest, TWIN_DIFF_INPUT: diff, **{k: w.astype(_WEIGHT_DTYPES[k]) for k, w in weights.items()}}
        y = _forward(args)
    with _jax.named_scope("loss_head"):
        err = _jnp.square(y.astype(_jnp.float32) - loss_target)
        return 0.5 * _jnp.sum(_jnp.mean(err, axis=-1)) if err.ndim else 0.5 * err


def _adamw(w, g, m, v):
    m = ADAM_B1 * m + (1.0 - ADAM_B1) * g
    v = ADAM_B2 * v + (1.0 - ADAM_B2) * _jnp.square(g)
    m_hat = m / (1.0 - ADAM_B1 ** ADAM_STEP)
    v_hat = v / (1.0 - ADAM_B2 ** ADAM_STEP)
    delta = -ADAM_LR * (m_hat / (_jnp.sqrt(v_hat) + ADAM_EPS) + ADAM_WD * w)
    return delta, m, v


def reference(x, c, g_norm1, w_in, g_q, g_k, rel_bias, w_o, g_norm2, w1, w2, w_ada, b_ada, loss_target, m_g_norm1, m_w_in, m_g_q, m_g_k, m_rel_bias, m_w_o, m_g_norm2, m_w1, m_w2, m_w_ada, m_b_ada, v_g_norm1, v_w_in, v_g_q, v_g_k, v_rel_bias, v_w_o, v_g_norm2, v_w1, v_w2, v_w_ada, v_b_ada):
    given = dict(x=x, c=c, g_norm1=g_norm1, w_in=w_in, g_q=g_q, g_k=g_k, rel_bias=rel_bias, w_o=w_o, g_norm2=g_norm2, w1=w1, w2=w2, w_ada=w_ada, b_ada=b_ada, loss_target=loss_target, m_g_norm1=m_g_norm1, m_w_in=m_w_in, m_g_q=m_g_q, m_g_k=m_g_k, m_rel_bias=m_rel_bias, m_w_o=m_w_o, m_g_norm2=m_g_norm2, m_w1=m_w1, m_w2=m_w2, m_w_ada=m_w_ada, m_b_ada=m_b_ada, v_g_norm1=v_g_norm1, v_w_in=v_w_in, v_g_q=v_g_q, v_g_k=v_g_k, v_rel_bias=v_rel_bias, v_w_o=v_w_o, v_g_norm2=v_g_norm2, v_w1=v_w1, v_w2=v_w2, v_w_ada=v_w_ada, v_b_ada=v_b_ada)
    weights = {n: given[n] for n in TWIN_WEIGHTS}
    shared = {n: given[n] for n in SHARED_INPUTS}
    per_example = {n: given[n] for n in ['x', 'c']}
    grad_fn = _jax.value_and_grad(_loss, argnums=(0, 1))

    def one_microbatch(ex, loss_target):
        ex = dict(ex)
        diff = ex.pop(TWIN_DIFF_INPUT)
        return grad_fn(weights, diff, {**shared, **ex}, loss_target)

    if N_MICROBATCH == 1:
        loss, (grad_w, grad_x) = one_microbatch(per_example, given["loss_target"])
    else:
        def body(carry, xs):
            loss_sum, grad_sum = carry
            l_k, (gw_k, gx_k) = one_microbatch(xs[0], xs[1])
            with _jax.named_scope("update"):
                return (loss_sum + l_k, _jax.tree.map(_jnp.add, grad_sum, gw_k)), gx_k

        init = (_jnp.zeros((), _jnp.float32), _jax.tree.map(_jnp.zeros_like, weights))
        (loss, grad_w), grad_x = _jax.lax.scan(body, init, (per_example, given["loss_target"]))
    with _jax.named_scope("update"):
        delta_w, new_m, new_v = {}, {}, {}
        for n in TWIN_WEIGHTS:
            delta_w[n], new_m[n], new_v[n] = _adamw(weights[n], grad_w[n], given["m_" + n], given["v_" + n])
    return (loss, grad_x, *[grad_w[n] for n in TWIN_WEIGHTS], *[delta_w[n] for n in TWIN_WEIGHTS],
            *[new_m[n] for n in TWIN_WEIGHTS], *[new_v[n] for n in TWIN_WEIGHTS])
```

```python
import functools

import jax
import jax.numpy as jnp
from jax import lax
from jax.experimental import pallas as pl
from jax.experimental.pallas import tpu as pltpu

F32 = jnp.float32
_MXU = jnp.bfloat16

HEAD_DIM = 64
CHUNK = 64
LEFT_CHUNKS = 8
PAD = LEFT_CHUNKS * CHUNK
BAND = PAD + CHUNK
REL_CLIP = 128
N_REL = 2 * REL_CLIP + 1
N_REL_PAD = 384
EPS = 1e-6
NEG = -1e30
NDEV = 8
SB_T = 128
CA_T = 2 * CHUNK
CA_W = CA_T + PAD
CA_V0 = CA_W - 2 * REL_CLIP
SB_SKIP = -104.0

ADAM_LR, ADAM_B1, ADAM_B2, ADAM_EPS, ADAM_WD, ADAM_STEP = 0.001, 0.9, 0.999, 1e-08, 0.01, 10

MESH = pl.DeviceIdType.MESH
VMEM_SPEC = pl.BlockSpec(memory_space=pltpu.VMEM)
SMEM_SPEC = pl.BlockSpec(memory_space=pltpu.SMEM)
ANY_SPEC = pl.BlockSpec(memory_space=pl.ANY)


def _nn(a, b):
    return lax.dot_general(a, b, (((1,), (0,)), ((), ())), preferred_element_type=F32)


def _nt(a, b):
    return lax.dot_general(a, b, (((1,), (1,)), ((), ())), preferred_element_type=F32)


def _tn(a, b):
    return lax.dot_general(a, b, (((0,), (0,)), ((), ())), preferred_element_type=F32)


def _blk(n, pref):
    return pref if n % pref == 0 else n


def _pos():
    return lax.axis_index("x"), lax.axis_index("y"), lax.axis_index("c")


def _flip(v, bit):
    return 1 - v if bit else v


def _all_gather_small(blk, name):
    R, C = blk.shape

    def body(x_ref, out_ref, send_sems, recv_sems):
        x, y, c = _pos()
        me = 4 * x + 2 * y + c

        def peer(k):
            return (_flip(x, k & 4), _flip(y, k & 2), _flip(c, k & 1))

        def copy(k, slot):
            return pltpu.make_async_remote_copy(
                src_ref=x_ref, dst_ref=out_ref.at[slot], send_sem=send_sems.at[k - 1],
                recv_sem=recv_sems.at[k - 1], device_id=peer(k), device_id_type=MESH)

        out_ref[pl.ds(me, 1), :, :] = x_ref[...].reshape(1, R, C)
        sends = [copy(k, me) for k in range(1, NDEV)]
        for cp in sends:
            cp.start()
        for k in range(1, NDEV):
            px, py, pc = peer(k)
            copy(k, 4 * px + 2 * py + pc).wait_recv()
        for cp in sends:
            cp.wait_send()

    return pl.pallas_call(
        body, name=name,
        out_shape=jax.ShapeDtypeStruct((NDEV, R, C), blk.dtype),
        in_specs=[VMEM_SPEC], out_specs=VMEM_SPEC,
        scratch_shapes=[pltpu.SemaphoreType.DMA((NDEV - 1,)), pltpu.SemaphoreType.DMA((NDEV - 1,))],
    )(blk)


def _all_gather_big(shards, name):
    n = len(shards)

    def body(*refs):
        ins, outs = refs[:n], refs[n:2 * n]
        send_sems, recv_sems, local_sems = refs[2 * n:]
        x, y, c = _pos()
        me, sibling = (x, y, c), (x, y, 1 - c)
        chips = [(1 - x, y), (x, 1 - y), (1 - x, 1 - y)]

        def copy(i, k, block, to, src=None):
            px, py, pc = block
            dst = outs[i].at[4 * px + 2 * py + pc]
            return pltpu.make_async_remote_copy(
                src_ref=dst if src is None else src, dst_ref=dst, send_sem=send_sems.at[7 * i + k],
                recv_sem=recv_sems.at[7 * i + k], device_id=to, device_id_type=MESH)

        mine, first, passed = [], [], []
        for i in range(n):
            mx, my, mc = me
            cp = pltpu.make_async_copy(ins[i], outs[i].at[4 * mx + 2 * my + mc], local_sems.at[i])
            cp.start()
            mine.append(cp)
            group = [copy(i, 0, me, sibling, src=ins[i])]
            group += [copy(i, 1 + j, me, (*chip, c), src=ins[i]) for j, chip in enumerate(chips)]
            for g in group:
                g.start()
            first += group
        for j, chip in enumerate(chips):
            for i in range(n):
                copy(i, 1 + j, (*chip, c), me).wait_recv()
                fw = copy(i, 4 + j, (*chip, c), sibling)
                fw.start()
                passed.append(fw)
        for i in range(n):
            copy(i, 0, sibling, me).wait_recv()
            for j, chip in enumerate(chips):
                copy(i, 4 + j, (*chip, 1 - c), me).wait_recv()
        for cp in first + passed:
            cp.wait_send()
        for cp in mine:
            cp.wait()

    return pl.pallas_call(
        body, name=name,
        out_shape=[jax.ShapeDtypeStruct((NDEV,) + s.shape, s.dtype) for s in shards],
        in_specs=[ANY_SPEC] * n, out_specs=[ANY_SPEC] * n,
        scratch_shapes=[pltpu.SemaphoreType.DMA((7 * n,)), pltpu.SemaphoreType.DMA((7 * n,)),
                        pltpu.SemaphoreType.DMA((n,))],
    )(*shards)


def _rs_sibling_exchange(grads, name):
    n = len(grads)

    def body(*refs):
        ins, outs = refs[:n], refs[n:2 * n]
        send_sems, recv_sems = refs[2 * n:]
        x, y, c = _pos()
        sibling = (x, y, 1 - c)
        copies = []
        for i in range(n):
            for q in range(4):
                cp = pltpu.make_async_remote_copy(
                    src_ref=ins[i].at[2 * q + (1 - c)], dst_ref=outs[i].at[q], send_sem=send_sems.at[4 * i + q],
                    recv_sem=recv_sems.at[4 * i + q], device_id=sibling, device_id_type=MESH)
                cp.start()
                copies.append(cp)
        for cp in copies:
            cp.wait()

    return pl.pallas_call(
        body, name=name,
        out_shape=[jax.ShapeDtypeStruct((4,) + g.shape[1:], g.dtype) for g in grads],
        in_specs=[ANY_SPEC] * n, out_specs=[ANY_SPEC] * n,
        scratch_shapes=[pltpu.SemaphoreType.DMA((4 * n,)), pltpu.SemaphoreType.DMA((4 * n,))],
    )(*grads)


def _rs_chip_exchange(parts, name):
    n = len(parts)

    def body(*refs):
        ins, outs = refs[:n], refs[n:2 * n]
        send_sems, recv_sems = refs[2 * n:]
        x, y, c = _pos()
        copies = []
        for i in range(n):
            for j in range(1, 4):
                cp = pltpu.make_async_remote_copy(
                    src_ref=ins[i].at[j - 1], dst_ref=outs[i].at[j - 1], send_sem=send_sems.at[3 * i + j - 1],
                    recv_sem=recv_sems.at[3 * i + j - 1], device_id=(_flip(x, j & 2), _flip(y, j & 1), c),
                    device_id_type=MESH)
                cp.start()
                copies.append(cp)
        for cp in copies:
            cp.wait()

    return pl.pallas_call(
        body, name=name,
        out_shape=[jax.ShapeDtypeStruct(p.shape, p.dtype) for p in parts],
        in_specs=[ANY_SPEC] * n, out_specs=[ANY_SPEC] * n,
        scratch_shapes=[pltpu.SemaphoreType.DMA((3 * n,)), pltpu.SemaphoreType.DMA((3 * n,))],
    )(*parts)


def _rs_chip_partial(place, grad, recv, name):
    _, R, C = grad.shape
    tr = _blk(R, 256)

    def body(place_ref, *refs):
        g_refs, r_refs = refs[:4], refs[4:8]
        own_ref, out_ref = refs[8:]
        own_ref[...] = g_refs[0][0] + r_refs[0][0]
        for j in range(1, 4):
            out_ref[j - 1] = (g_refs[j][0] + r_refs[j][0]).astype(out_ref.dtype)

    def g_map(j):
        return lambda i, p: (2 * jnp.bitwise_xor(p[0], j) + p[1], i, 0)

    def r_map(j):
        return lambda i, p: (jnp.bitwise_xor(p[0], j), i, 0)

    grid_spec = pltpu.PrefetchScalarGridSpec(
        num_scalar_prefetch=1, grid=(R // tr,),
        in_specs=[pl.BlockSpec((1, tr, C), g_map(j)) for j in range(4)]
        + [pl.BlockSpec((1, tr, C), r_map(j)) for j in range(4)],
        out_specs=[pl.BlockSpec((tr, C), lambda i, p: (i, 0)), pl.BlockSpec((3, tr, C), lambda i, p: (0, i, 0))])
    return pl.pallas_call(
        body, name=name, grid_spec=grid_spec,
        out_shape=[jax.ShapeDtypeStruct((R, C), F32), jax.ShapeDtypeStruct((3, R, C), _MXU)],
    )(place, *([grad] * 4), *([recv] * 4))


def _adamw_math(w, g, m, v):
    m = ADAM_B1 * m + (1.0 - ADAM_B1) * g
    v = ADAM_B2 * v + (1.0 - ADAM_B2) * (g * g)
    m_hat = m / (1.0 - ADAM_B1 ** ADAM_STEP)
    v_hat = v / (1.0 - ADAM_B2 ** ADAM_STEP)
    delta = -ADAM_LR * (m_hat / (jnp.sqrt(v_hat) + ADAM_EPS) + ADAM_WD * w)
    return delta, m, v


def _adamw(w, g, m, v, name):
    R, C = w.shape
    tr = _blk(R, 256)

    def body(w_ref, g_ref, m_ref, v_ref, d_ref, nm_ref, nv_ref):
        d, nm, nv = _adamw_math(w_ref[...], g_ref[...], m_ref[...], v_ref[...])
        d_ref[...] = d
        nm_ref[...] = nm
        nv_ref[...] = nv

    spec = pl.BlockSpec((tr, C), lambda i: (i, 0))
    return pl.pallas_call(
        body, name=name, grid=(R // tr,), in_specs=[spec] * 4, out_specs=[spec] * 3,
        out_shape=[jax.ShapeDtypeStruct((R, C), F32)] * 3,
    )(w, g, m, v)


def _rs_sum_adamw(owns, recvs, w, m, v, name):
    L, R, C = w.shape
    tr = _blk(R, 256)
    nr = R // tr

    def body(o0, o1, r0, r1, w_ref, m_ref, v_ref, g_ref, d_ref, nm_ref, nv_ref):
        def step(o_ref, r_ref):
            g = o_ref[...]
            for j in range(3):
                g = g + r_ref[j].astype(F32)
            d, nm, nv = _adamw_math(w_ref[0], g, m_ref[0], v_ref[0])
            g_ref[0] = g
            d_ref[0] = d
            nm_ref[0] = nm
            nv_ref[0] = nv

        pl.when(pl.program_id(0) == 0)(lambda: step(o0, r0))
        pl.when(pl.program_id(0) == 1)(lambda: step(o1, r1))

    def hold(layer):
        if layer == 0:
            return lambda l, i: i * (1 - l) + (nr - 1) * l
        return lambda l, i: i * l

    own_spec = [pl.BlockSpec((tr, C), functools.partial(lambda l, i, f: (f(l, i), 0), f=hold(k))) for k in range(2)]
    recv_spec = [pl.BlockSpec((3, tr, C), functools.partial(lambda l, i, f: (0, f(l, i), 0), f=hold(k)))
                 for k in range(2)]
    lay = pl.BlockSpec((1, tr, C), lambda l, i: (l, i, 0))
    return pl.pallas_call(
        body, name=name, grid=(L, nr),
        in_specs=own_spec + recv_spec + [lay] * 3, out_specs=[lay] * 4,
        out_shape=[jax.ShapeDtypeStruct((L, R, C), F32)] * 4,
    )(owns[0], owns[1], recvs[0], recvs[1], w, m, v)


def _silu(x):
    return x / (1.0 + jnp.exp(-x))


def _mod_partial(c_all, w_ada, b_cols, name):
    L, D, Ca = w_ada.shape

    def body(c_ref, w_ref, b_ref, o_ref):
        act = _silu(c_ref[...]).astype(_MXU)
        for l in range(L):
            o_ref[:, l * Ca:(l + 1) * Ca] = _nn(act, w_ref[l].astype(_MXU)) + b_ref[l:l + 1, :]

    return pl.pallas_call(
        body, name=name, out_shape=jax.ShapeDtypeStruct((NDEV, L * Ca), F32),
        in_specs=[VMEM_SPEC] * 3, out_specs=VMEM_SPEC,
    )(c_all, w_ada, b_cols)


def _w_ada_grad(c_t, dmod_cols, name):
    L, _, Ca = dmod_cols.shape
    D = c_t.shape[0]

    def body(c_ref, d_ref, o_ref):
        act = _silu(c_ref[...]).astype(_MXU)
        for l in range(L):
            o_ref[l] = _nn(act, d_ref[l].astype(_MXU))

    return pl.pallas_call(
        body, name=name, out_shape=jax.ShapeDtypeStruct((L, D, Ca), F32),
        in_specs=[VMEM_SPEC] * 2, out_specs=VMEM_SPEC,
    )(c_t, dmod_cols)


def _sum_devices(gathered, name):
    _, _, N = gathered.shape

    def body(x_ref, o_ref):
        acc = x_ref[0]
        for d in range(1, NDEV):
            acc = acc + x_ref[d]
        o_ref[...] = acc

    return pl.pallas_call(
        body, name=name, out_shape=jax.ShapeDtypeStruct((1, N), F32),
        in_specs=[VMEM_SPEC], out_specs=VMEM_SPEC,
    )(gathered)


def _ln_mod_matmul(x, g, sc, sh, wg, name):
    S, D = x.shape
    nb, _, Cs = wg.shape
    tm = _blk(S, 1024)

    def body(x_ref, g_ref, sc_ref, sh_ref, w_ref, o_ref, h_ref, hs_ref):
        @pl.when(pl.program_id(1) == 0)
        def _():
            xv = x_ref[...]
            r = lax.rsqrt(jnp.mean(xv * xv, axis=-1, keepdims=True) + EPS)
            hv = ((xv * r) * g_ref[...]) * (1.0 + sc_ref[...]) + sh_ref[...]
            hb = hv.astype(_MXU)
            hs_ref[...] = hb
            h_ref[...] = hb

        o_ref[...] = _nn(hs_ref[...], w_ref[0]).astype(o_ref.dtype)

    vec = pl.BlockSpec((1, D), lambda i, j: (0, 0))
    return pl.pallas_call(
        body, name=name, grid=(S // tm, nb),
        in_specs=[pl.BlockSpec((tm, D), lambda i, j: (i, 0)), vec, vec, vec,
                  pl.BlockSpec((1, D, Cs), lambda i, j: (j, 0, 0))],
        out_specs=[pl.BlockSpec((tm, Cs), lambda i, j: (i, j)), pl.BlockSpec((tm, D), lambda i, j: (i, 0))],
        out_shape=[jax.ShapeDtypeStruct((S, nb * Cs), _MXU), jax.ShapeDtypeStruct((S, D), _MXU)],
        scratch_shapes=[pltpu.VMEM((tm, D), _MXU)],
    )(x, g, sc, sh, wg)


def _matmul_res_gate(a, w, xres, gt, relu2, name):
    S, K = a.shape
    N = w.shape[1]
    tm = _blk(S, 512)

    def body(a_ref, w_ref, x_ref, gt_ref, o_ref, f_ref):
        av = a_ref[...]
        if relu2:
            af = jnp.maximum(av.astype(F32), 0.0)
            av = (af * af).astype(_MXU)
        f = _nn(av, w_ref[...])
        f_ref[...] = f.astype(f_ref.dtype)
        o_ref[...] = x_ref[...] + gt_ref[...] * f

    row = lambda width: pl.BlockSpec((tm, width), lambda i: (i, 0))
    return pl.pallas_call(
        body, name=name, grid=(S // tm,),
        in_specs=[row(K), pl.BlockSpec((K, N), lambda i: (0, 0)), row(N), pl.BlockSpec((1, N), lambda i: (0, 0))],
        out_specs=[row(N), row(N)],
        out_shape=[jax.ShapeDtypeStruct((S, N), F32), jax.ShapeDtypeStruct((S, N), _MXU)],
    )(a, w, xres, gt)


def _loss_grad(y, t, name):
    S, D = y.shape
    tm = _blk(S, 512)
    last = S // tm - 1

    def body(y_ref, t_ref, dy_ref, l_ref, acc_ref):
        i = pl.program_id(0)
        e = y_ref[...] - t_ref[...]
        dy_ref[...] = e * (1.0 / D)
        part = jnp.sum(e * e, axis=0, keepdims=True)

        @pl.when(i == 0)
        def _():
            acc_ref[...] = part

        @pl.when(i > 0)
        def _():
            acc_ref[...] += part

        @pl.when(i == last)
        def _():
            l_ref[...] = (0.5 / D) * jnp.sum(acc_ref[...], axis=1, keepdims=True)

    row = pl.BlockSpec((tm, D), lambda i: (i, 0))
    return pl.pallas_call(
        body, name=name, grid=(S // tm,), in_specs=[row, row],
        out_specs=[row, pl.BlockSpec((1, 1), lambda i: (0, 0))],
        out_shape=[jax.ShapeDtypeStruct((S, D), F32), jax.ShapeDtypeStruct((1, 1), F32)],
        scratch_shapes=[pltpu.VMEM((1, D), F32)],
    )(y, t)


def _accumulate(ref, part, first):
    @pl.when(first)
    def _():
        ref[...] = part

    @pl.when(jnp.logical_not(first))
    def _():
        ref[...] += part


def _gate_nt_matmul(dx, f, gt, w, u, name):
    S, D = dx.shape
    N = w.shape[0]
    tm, tn = _blk(S, 512), _blk(N, 1024)
    with_u = u is not None

    def body(*refs):
        if with_u:
            dx_ref, f_ref, gt_ref, w_ref, u_ref, dz_ref, dgt_ref, res_ref, dzs_ref = refs
        else:
            dx_ref, f_ref, gt_ref, w_ref, dz_ref, dgt_ref, res_ref, dzs_ref = refs
        i, j = pl.program_id(0), pl.program_id(1)

        @pl.when(j == 0)
        def _():
            dxv = dx_ref[...]
            dz = (dxv * gt_ref[...]).astype(_MXU)
            dzs_ref[...] = dz
            dz_ref[...] = dz
            _accumulate(dgt_ref, jnp.sum(dxv * f_ref[...].astype(F32), axis=0, keepdims=True), i == 0)

        r = _nt(dzs_ref[...], w_ref[...])
        if with_u:
            r = r * (2.0 * jnp.maximum(u_ref[...].astype(F32), 0.0))
        res_ref[...] = r.astype(res_ref.dtype)

    row = pl.BlockSpec((tm, D), lambda i, j: (i, 0))
    tile = pl.BlockSpec((tm, tn), lambda i, j: (i, j))
    in_specs = [row, row, pl.BlockSpec((1, D), lambda i, j: (0, 0)), pl.BlockSpec((tn, D), lambda i, j: (j, 0))]
    args = [dx, f, gt, w]
    if with_u:
        in_specs.append(tile)
        args.append(u)
    return pl.pallas_call(
        body, name=name, grid=(S // tm, N // tn), in_specs=in_specs,
        out_specs=[row, pl.BlockSpec((1, D), lambda i, j: (0, 0)), tile],
        out_shape=[jax.ShapeDtypeStruct((S, D), _MXU), jax.ShapeDtypeStruct((1, D), F32),
                   jax.ShapeDtypeStruct((S, N), _MXU)],
        scratch_shapes=[pltpu.VMEM((tm, D), _MXU)],
    )(*args)


def _tn_matmul(a, b, by_col, relu2, name):
    S, Ka = a.shape
    Nb = b.shape[1]
    ts = _blk(S, 1024)
    if by_col:
        R, C = Ka, Nb // NDEV
        a_spec = pl.BlockSpec((ts, Ka), lambda d, k: (k, 0))
        b_spec = pl.BlockSpec((ts, C), lambda d, k: (k, d))
    else:
        R, C = Ka // NDEV, Nb
        a_spec = pl.BlockSpec((ts, R), lambda d, k: (k, d))
        b_spec = pl.BlockSpec((ts, Nb), lambda d, k: (k, 0))

    def body(a_ref, b_ref, o_ref):
        av = a_ref[...]
        if relu2:
            af = jnp.maximum(av.astype(F32), 0.0)
            av = (af * af).astype(_MXU)
        p = _tn(av, b_ref[...])
        k = pl.program_id(1)

        @pl.when(k == 0)
        def _():
            o_ref[0] = p

        @pl.when(k > 0)
        def _():
            o_ref[0] += p

    return pl.pallas_call(
        body, name=name, grid=(NDEV, S // ts), in_specs=[a_spec, b_spec],
        out_specs=pl.BlockSpec((1, R, C), lambda d, k: (d, 0, 0)),
        out_shape=jax.ShapeDtypeStruct((NDEV, R, C), F32),
    )(a, b)


def _nt_ln_bwd(dy, wg, x, g, sc, sh, dxres, name):
    S, D = x.shape
    nb, _, Cs = wg.shape
    tm = _blk(S, 512)

    def body(dy_ref, w_ref, x_ref, g_ref, sc_ref, sh_ref, dxr_ref, dx_ref, dsh_ref, dsc_ref, dg_ref, acc_ref):
        i, k = pl.program_id(0), pl.program_id(1)
        p = _nt(dy_ref[...], w_ref[0])

        @pl.when(k == 0)
        def _():
            acc_ref[...] = p

        @pl.when(k > 0)
        def _():
            acc_ref[...] += p

        @pl.when(k == nb - 1)
        def _():
            dh = acc_ref[...]
            xv = x_ref[...]
            r = lax.rsqrt(jnp.mean(xv * xv, axis=-1, keepdims=True) + EPS)
            xhat = xv * r
            gv = g_ref[...]
            dn = dh * (1.0 + sc_ref[...])
            dxhat = dn * gv
            dxv = r * (dxhat - xhat * jnp.mean(dxhat * xhat, axis=-1, keepdims=True))
            dx_ref[...] = dxr_ref[...] + dxv
            first = i == 0
            _accumulate(dsh_ref, jnp.sum(dh, axis=0, keepdims=True), first)
            _accumulate(dsc_ref, jnp.sum(dh * (xhat * gv), axis=0, keepdims=True), first)
            _accumulate(dg_ref, jnp.sum(dn * xhat, axis=0, keepdims=True), first)

    row = pl.BlockSpec((tm, D), lambda i, k: (i, 0))
    vec = pl.BlockSpec((1, D), lambda i, k: (0, 0))
    return pl.pallas_call(
        body, name=name, grid=(S // tm, nb),
        in_specs=[pl.BlockSpec((tm, Cs), lambda i, k: (i, k)), pl.BlockSpec((1, D, Cs), lambda i, k: (k, 0, 0)),
                  row, vec, vec, vec, row],
        out_specs=[row, vec, vec, vec],
        out_shape=[jax.ShapeDtypeStruct((S, D), F32)] + [jax.ShapeDtypeStruct((1, D), F32)] * 3,
        scratch_shapes=[pltpu.VMEM((tm, D), F32)],
    )(dy, wg, x, g, sc, sh, dxres)


def _split3(v):
    hi = v.astype(_MXU)
    r1 = v - hi.astype(F32)
    mid = r1.astype(_MXU)
    lo = (r1 - mid.astype(F32)).astype(_MXU)
    return hi, mid, lo


def _tri_sums(v, tri2):
    T = v.shape[0]
    hi, mid, lo = _split3(v)
    s = _nn(hi, tri2) + _nn(mid, tri2) + _nn(lo, tri2)
    return s[:, :T], s[:, T:]


def _tri2(T, inclusive):
    j = lax.broadcasted_iota(jnp.int32, (T, 2 * T), 0)
    s = lax.broadcasted_iota(jnp.int32, (T, 2 * T), 1)
    keep = (j >= s) if inclusive else (j > s)
    return jnp.where((s >= T) | keep, 1.0, 0.0).astype(_MXU)


def _log_sigmoid(z):
    return jnp.minimum(z, 0.0) - jnp.log(1.0 + jnp.exp(-jnp.abs(z)))


def _sb_block(qb, kblk, strict, tri2, carry):
    scale = HEAD_DIM ** -0.5
    z = _nt(qb, kblk) * scale
    lb = _log_sigmoid(z)
    l1 = lb - z
    if strict is not None:
        l1 = jnp.where(strict, l1, 0.0)
    sfx, tot = _tri_sums(l1, tri2)
    a = jnp.exp(lb + sfx + carry)
    if strict is not None:
        a = jnp.where(strict, a, 0.0)
    return lb, a, carry + tot


def _sb_fwd(q, k, v, name):
    H, S, hd = q.shape
    T = _blk(S, SB_T)

    def body(q_ref, k_ref, v_ref, o_ref, ox_ref):
        qi = pl.program_id(1)
        qb = q_ref[0]
        row = lax.broadcasted_iota(jnp.int32, (T, T), 0)
        col = lax.broadcasted_iota(jnp.int32, (T, T), 1)
        tri2 = _tri2(T, inclusive=False)

        def pair(kb, carry, acc, fine, strict):
            start = pl.multiple_of(kb * T, T)
            kblk = k_ref[0, pl.ds(start, T), :]
            vblk = v_ref[0, pl.ds(start, T), :]
            _, a, carry = _sb_block(qb, kblk, strict, tri2, carry)
            hi, mid, _ = _split3(a)
            return carry, acc + _nn(hi, vblk), fine + _nn(mid, vblk)

        zero = jnp.zeros((T, hd), F32)
        carry, acc, fine = pair(qi, jnp.zeros((T, T), F32), zero, zero, col < row)

        def cond(st):
            kb, carry, _, _ = st
            return jnp.logical_and(kb >= 0, jnp.max(carry) > SB_SKIP)

        def step(st):
            kb, carry, acc, fine = st
            carry, acc, fine = pair(kb, carry, acc, fine, None)
            return kb - 1, carry, acc, fine

        _, _, acc, fine = lax.while_loop(cond, step, (qi - 1, carry, acc, fine))
        o_ref[0] = acc.astype(o_ref.dtype)
        ox_ref[0] = acc + fine

    blk = pl.BlockSpec((1, T, hd), lambda h, i: (h, i, 0))
    full = pl.BlockSpec((1, S, hd), lambda h, i: (h, 0, 0))
    return pl.pallas_call(
        body, name=name, grid=(H, S // T), in_specs=[blk, full, full], out_specs=[blk, blk],
        out_shape=[jax.ShapeDtypeStruct((H, S, hd), _MXU), jax.ShapeDtypeStruct((H, S, hd), F32)],
    )(q, k, v)


def _sb_bwd(q, k, v, ox, do, name):
    H, S, hd = q.shape
    T = _blk(S, SB_T)
    scale = HEAD_DIM ** -0.5

    def body(q_ref, k_ref, v_ref, o_ref, do_ref, dq_ref, dk_ref, dv_ref):
        qi = pl.program_id(1)

        @pl.when(qi == 0)
        def _():
            dk_ref[...] = jnp.zeros_like(dk_ref)
            dv_ref[...] = jnp.zeros_like(dv_ref)

        qb = q_ref[0]
        dob = do_ref[0]
        delta = jnp.sum(dob.astype(F32) * o_ref[0], axis=-1, keepdims=True)
        row = lax.broadcasted_iota(jnp.int32, (T, T), 0)
        col = lax.broadcasted_iota(jnp.int32, (T, T), 1)
        tri_ex = _tri2(T, inclusive=False)
        tri_in = _tri2(T, inclusive=True)

        def pair(kb, carry, right, dq, strict):
            start = pl.multiple_of(kb * T, T)
            kblk = k_ref[0, pl.ds(start, T), :]
            vblk = v_ref[0, pl.ds(start, T), :]
            lb, a, carry = _sb_block(qb, kblk, strict, tri_ex, carry)
            ag = a * _nt(dob, vblk)
            sfx, tot = _tri_sums(ag, tri_in)
            left = delta - (sfx + right)
            beta = jnp.exp(lb)
            dz = ag * (1.0 - beta) - beta * left
            if strict is not None:
                dz = jnp.where(strict, dz, 0.0)
            dzb = (dz * scale).astype(_MXU)
            dk_ref[0, pl.ds(start, T), :] += _tn(dzb, qb)
            dv_ref[0, pl.ds(start, T), :] += _tn(a.astype(_MXU), dob)
            return carry, right + tot, dq + _nn(dzb, kblk)

        zero = jnp.zeros((T, T), F32)
        carry, right, dq = pair(qi, zero, zero, jnp.zeros((T, hd), F32), col < row)

        def cond(st):
            kb, carry, _, _ = st
            return jnp.logical_and(kb >= 0, jnp.max(carry) > SB_SKIP)

        def step(st):
            kb, carry, right, dq = st
            carry, right, dq = pair(kb, carry, right, dq, None)
            return kb - 1, carry, right, dq

        _, _, _, dq = lax.while_loop(cond, step, (qi - 1, carry, right, dq))
        dq_ref[0] = dq

    blk = pl.BlockSpec((1, T, hd), lambda h, i: (h, i, 0))
    full = pl.BlockSpec((1, S, hd), lambda h, i: (h, 0, 0))
    return pl.pallas_call(
        body, name=name, grid=(H, S // T), in_specs=[blk, full, full, blk, blk], out_specs=[blk, full, full],
        out_shape=[jax.ShapeDtypeStruct((H, S, hd), F32)] * 3,
    )(q, k, v, ox, do)


def _ca_index():
    i = lax.broadcasted_iota(jnp.int32, (CA_T, CA_W), 0)
    j = lax.broadcasted_iota(jnp.int32, (CA_T, CA_W), 1)
    a = i // CHUNK
    jj = j - a * CHUNK
    inband = (jj >= 0) & (jj < BAND)
    idx = jnp.clip((i - a * CHUNK) + PAD - jj, -REL_CLIP, REL_CLIP) + REL_CLIP
    return inband, idx


def _ca_bias(rel_bias, name):
    H = rel_bias.shape[0]

    def body(rb_ref, o_ref):
        h = pl.program_id(0)
        inband, idx = _ca_index()
        o_ref[0] = jnp.where(inband, rb_ref[h, N_REL - 1], NEG)
        idx_v = idx[:, CA_V0:]
        acc = lax.fori_loop(0, N_REL - 1, lambda r, acc: jnp.where(idx_v == r, rb_ref[h, r], acc),
                            jnp.full((CA_T, CA_W - CA_V0), rb_ref[h, N_REL - 1], F32))
        o_ref[0, :, CA_V0:] = jnp.where(inband[:, CA_V0:], acc, NEG)

    return pl.pallas_call(
        body, name=name, grid=(H,), in_specs=[SMEM_SPEC],
        out_specs=pl.BlockSpec((1, CA_T, CA_W), lambda h: (h, 0, 0)),
        out_shape=jax.ShapeDtypeStruct((H, CA_T, CA_W), F32),
    )(rel_bias)


def _ca_bias_bwd(dbias, name):
    H = dbias.shape[0]

    def body(d_ref, o_ref):
        inband, idx = _ca_index()
        d = jnp.where(inband, d_ref[0], 0.0)
        lane = lax.broadcasted_iota(jnp.int32, (1, N_REL_PAD), 1)
        d_v, idx_v = d[:, CA_V0:], idx[:, CA_V0:]

        def step(r, acc):
            return jnp.where(lane == r, jnp.sum(jnp.where(idx_v == r, d_v, 0.0)), acc)

        acc = lax.fori_loop(0, N_REL - 1, step, jnp.zeros((1, N_REL_PAD), F32))
        top = jnp.sum(jnp.where(idx == N_REL - 1, d, 0.0))
        o_ref[0] = jnp.where(lane == N_REL - 1, top, acc)

    return pl.pallas_call(
        body, name=name, grid=(H,), in_specs=[pl.BlockSpec((1, CA_T, CA_W), lambda h: (h, 0, 0))],
        out_specs=pl.BlockSpec((1, 1, N_REL_PAD), lambda h: (h, 0, 0)),
        out_shape=jax.ShapeDtypeStruct((H, 1, N_REL_PAD), F32),
    )(dbias)


def _head_norm(t, g):
    tf = t.astype(F32)
    r = lax.rsqrt(jnp.mean(tf * tf, axis=-1, keepdims=True) + EPS)
    hat = tf * r
    return hat * g, hat, r


def _head_norm_bwd(dn, hat, r, g):
    dhat = dn * g
    return r * (dhat - hat * jnp.mean(dhat * hat, axis=-1, keepdims=True))


def _ca_fill(k_ref, v_ref, gk_ref, kn_ref, vp_ref):
    S, hd = k_ref.shape[1:]
    kn, _, _ = _head_norm(k_ref[0], gk_ref[...])
    kn_ref[0:PAD, :] = jnp.zeros((PAD, hd), kn_ref.dtype)
    vp_ref[0:PAD, :] = jnp.zeros((PAD, hd), vp_ref.dtype)
    kn_ref[PAD:PAD + S, :] = kn.astype(kn_ref.dtype)
    vp_ref[PAD:PAD + S, :] = v_ref[0]


def _ca_probs(q_ref, b2_ref, gq_ref, kn_ref, qi):
    scale = HEAD_DIM ** -0.5
    qn, qhat, r = _head_norm(q_ref[0], gq_ref[...])
    qn = qn.astype(_MXU)
    start = pl.multiple_of(qi * CA_T, CA_T)
    s = _nt(qn, kn_ref[pl.ds(start, CA_W), :]) * scale + b2_ref[0]
    key_pos = qi * CA_T - PAD + lax.broadcasted_iota(jnp.int32, (CA_T, CA_W), 1)
    s = jnp.where(key_pos >= 0, s, NEG)
    e = jnp.exp(s - jnp.max(s, axis=-1, keepdims=True))
    p = e / jnp.sum(e, axis=-1, keepdims=True)
    return p, qn, qhat, r, start


def _ca_fwd(q, k, v, bias2, gq, gk, name):
    H, S, hd = q.shape

    def body(q_ref, k_ref, v_ref, b2_ref, gq_ref, gk_ref, o_ref, kn_ref, vp_ref):
        qi = pl.program_id(1)

        @pl.when(qi == 0)
        def _():
            _ca_fill(k_ref, v_ref, gk_ref, kn_ref, vp_ref)

        p, _, _, _, start = _ca_probs(q_ref, b2_ref, gq_ref, kn_ref, qi)
        o_ref[0] = _nn(p.astype(_MXU), vp_ref[pl.ds(start, CA_W), :]).astype(o_ref.dtype)

    blk = pl.BlockSpec((1, CA_T, hd), lambda h, i: (h, i, 0))
    full = pl.BlockSpec((1, S, hd), lambda h, i: (h, 0, 0))
    vec = pl.BlockSpec((1, hd), lambda h, i: (0, 0))
    return pl.pallas_call(
        body, name=name, grid=(H, S // CA_T),
        in_specs=[blk, full, full, pl.BlockSpec((1, CA_T, CA_W), lambda h, i: (h, 0, 0)), vec, vec],
        out_specs=blk, out_shape=jax.ShapeDtypeStruct((H, S, hd), _MXU),
        scratch_shapes=[pltpu.VMEM((PAD + S, hd), _MXU), pltpu.VMEM((PAD + S, hd), _MXU)],
    )(q, k, v, bias2, gq, gk)


def _ca_bwd(q, k, v, bias2, gq, gk, do, name):
    H, S, hd = q.shape
    scale = HEAD_DIM ** -0.5
    last = S // CA_T - 1

    def body(q_ref, k_ref, v_ref, b2_ref, gq_ref, gk_ref, do_ref,
             dq_ref, dk_ref, dv_ref, db_ref, dgq_ref, dgk_ref, kn_ref, vp_ref, dkn_ref, dvp_ref):
        h, qi = pl.program_id(0), pl.program_id(1)

        @pl.when(qi == 0)
        def _():
            _ca_fill(k_ref, v_ref, gk_ref, kn_ref, vp_ref)
            dkn_ref[...] = jnp.zeros_like(dkn_ref)
            dvp_ref[...] = jnp.zeros_like(dvp_ref)
            db_ref[...] = jnp.zeros_like(db_ref)

        @pl.when(jnp.logical_and(h == 0, qi == 0))
        def _():
            dgq_ref[...] = jnp.zeros_like(dgq_ref)
            dgk_ref[...] = jnp.zeros_like(dgk_ref)

        p, qn, qhat, r, start = _ca_probs(q_ref, b2_ref, gq_ref, kn_ref, qi)
        dob = do_ref[0]
        band = pl.ds(start, CA_W)
        dp = _nt(dob, vp_ref[band, :])
        ds = p * (dp - jnp.sum(p * dp, axis=-1, keepdims=True))
        db_ref[0] += ds
        dsb = (ds * scale).astype(_MXU)
        dqn = _nn(dsb, kn_ref[band, :])
        dkn_ref[band, :] += _tn(dsb, qn)
        dvp_ref[band, :] += _tn(p.astype(_MXU), dob)
        dgq_ref[...] += jnp.sum(dqn * qhat, axis=0, keepdims=True)
        dq_ref[0] = _head_norm_bwd(dqn, qhat, r, gq_ref[...])

        @pl.when(qi == last)
        def _():
            _, khat, rk = _head_norm(k_ref[0], gk_ref[...])
            dkn = dkn_ref[PAD:PAD + S, :]
            dgk_ref[...] += jnp.sum(dkn * khat, axis=0, keepdims=True)
            dk_ref[0] = _head_norm_bwd(dkn, khat, rk, gk_ref[...])
            dv_ref[0] = dvp_ref[PAD:PAD + S, :]

    blk = pl.BlockSpec((1, CA_T, hd), lambda h, i: (h, i, 0))
    full = pl.BlockSpec((1, S, hd), lambda h, i: (h, 0, 0))
    vec = pl.BlockSpec((1, hd), lambda h, i: (0, 0))
    tile = pl.BlockSpec((1, CA_T, CA_W), lambda h, i: (h, 0, 0))
    return pl.pallas_call(
        body, name=name, grid=(H, S // CA_T),
        in_specs=[blk, full, full, tile, vec, vec, blk],
        out_specs=[blk, full, full, tile, vec, vec],
        out_shape=[jax.ShapeDtypeStruct((H, S, hd), F32)] * 3
        + [jax.ShapeDtypeStruct((H, CA_T, CA_W), F32), jax.ShapeDtypeStruct((1, hd), F32),
           jax.ShapeDtypeStruct((1, hd), F32)],
        scratch_shapes=[pltpu.VMEM((PAD + S, hd), _MXU), pltpu.VMEM((PAD + S, hd), _MXU),
                        pltpu.VMEM((PAD + S, hd), F32), pltpu.VMEM((PAD + S, hd), F32)],
    )(q, k, v, bias2, gq, gk, do)


def _to_heads(t, H):
    S, W = t.shape
    return t.reshape(S, W // (H * HEAD_DIM), H, HEAD_DIM).transpose(1, 2, 0, 3)


def _from_heads(t):
    n, H, S, hd = t.shape
    return t.transpose(2, 0, 1, 3).reshape(S, n * H * hd)


def _pack_small(parts):
    flat = jnp.concatenate([p.reshape(-1) for layer in parts for p in layer])
    n = flat.shape[0]
    n_pad = -(-n // 1024) * 1024
    return jnp.pad(flat, (0, n_pad - n)).reshape(1, n_pad)


def _unpack_small(flat, shapes):
    out, off = [], 0
    for layer in shapes:
        cur = []
        for shp in layer:
            size = 1
            for s in shp:
                size *= s
            cur.append(flat[off:off + size].reshape(shp))
            off += size
        out.append(cur)
    return out


def kernel(x, c, g_norm1, w_in, g_q, g_k, rel_bias, w_o, g_norm2, w1, w2, w_ada, b_ada, loss_target, m_g_norm1, m_w_in, m_g_q, m_g_k, m_rel_bias, m_w_o, m_g_norm2, m_w1, m_w2, m_w_ada, m_b_ada, v_g_norm1, v_w_in, v_g_q, v_g_k, v_rel_bias, v_w_o, v_g_norm2, v_w1, v_w2, v_w_ada, v_b_ada):
    L = w_in.shape[0]
    S, D = x.shape[1:]
    H2 = D // HEAD_DIM // 2
    Ca = w_ada.shape[2]
    xi, yi, ci = _pos()
    me = 4 * xi + 2 * yi + ci
    place = jnp.stack([2 * xi + yi, ci]).astype(jnp.int32)

    c_all = _all_gather_small(c, "ag_c").reshape(NDEV, D)
    b_cols = lax.dynamic_slice(b_ada, (0, me * Ca), (L, Ca))
    mod_part = _mod_partial(c_all, w_ada, b_cols, "mod_partial")
    mod_all = _all_gather_small(mod_part, "ag_mod")
    mod = lax.dynamic_index_in_dim(mod_all, me, axis=1, keepdims=False)
    mod = mod.reshape(NDEV, L, Ca).transpose(1, 0, 2).reshape(L, 6, 1, D)

    shards = []
    for l in range(L):
        shards += [w_in[l].astype(_MXU), w_o[l].astype(_MXU), w1[l].astype(_MXU), w2[l].astype(_MXU)]
    gathered = _all_gather_big(shards, "ag_weights")
    W_in = [gathered[4 * l + 0] for l in range(L)]
    W_o = [gathered[4 * l + 1].reshape(D, D) for l in range(L)]
    W_1 = [gathered[4 * l + 2] for l in range(L)]
    W_2 = [gathered[4 * l + 3].reshape(4 * D, D) for l in range(L)]

    xs = [x[0]]
    saved = []
    for l in range(L):
        sh1, sc1, gt1, sh2, sc2, gt2 = [mod[l, i] for i in range(6)]
        gn1, gn2 = g_norm1[l:l + 1], g_norm2[l:l + 1]
        gq, gk = g_q[l:l + 1], g_k[l:l + 1]
        proj, h1 = _ln_mod_matmul(xs[-1], gn1, sc1, sh1, W_in[l], f"l{l}_proj")
        heads = _to_heads(proj, H2)
        o_sb, ox_sb = _sb_fwd(heads[0], heads[1], heads[2], f"l{l}_sb_fwd")
        bias2 = _ca_bias(rel_bias[l], f"l{l}_ca_bias")
        o_ca = _ca_fwd(heads[3], heads[4], heads[5], bias2, gq, gk, f"l{l}_ca_fwd")
        mixed = _from_heads(jnp.stack([o_sb, o_ca]))
        x1, f1 = _matmul_res_gate(mixed, W_o[l], xs[-1], gt1, False, f"l{l}_attn_out")
        u, h2 = _ln_mod_matmul(x1, gn2, sc2, sh2, W_1[l], f"l{l}_mlp_in")
        x2, f2 = _matmul_res_gate(u, W_2[l], x1, gt2, True, f"l{l}_mlp_out")
        saved.append(dict(x0=xs[-1], h1=h1, heads=heads, ox_sb=ox_sb, bias2=bias2, mixed=mixed, f1=f1, x1=x1,
                          h2=h2, u=u, f2=f2))
        xs.append(x2)

    dx, loss_part = _loss_grad(xs[-1], loss_target[0], "loss")
    loss = lax.psum(loss_part[0, 0], ("x", "y", "c"))

    big_grads = [None] * (4 * L)
    small_parts = [None] * L
    for l in reversed(range(L)):
        sv = saved[l]
        sh1, sc1, gt1, sh2, sc2, gt2 = [mod[l, i] for i in range(6)]
        gn1, gn2 = g_norm1[l:l + 1], g_norm2[l:l + 1]
        gq, gk = g_q[l:l + 1], g_k[l:l + 1]
        dz2, dgt2, du = _gate_nt_matmul(dx, sv["f2"], gt2, W_2[l], sv["u"], f"l{l}_mlp_out_bwd")
        gw2 = _tn_matmul(sv["u"], dz2, False, True, f"l{l}_gw2")
        gw1 = _tn_matmul(sv["h2"], du, True, False, f"l{l}_gw1")
        dx, dsh2, dsc2, dgn2 = _nt_ln_bwd(du, W_1[l], sv["x1"], gn2, sc2, sh2, dx, f"l{l}_mlp_in_bwd")
        dz1, dgt1, dmixed = _gate_nt_matmul(dx, sv["f1"], gt1, W_o[l], None, f"l{l}_attn_out_bwd")
        gwo = _tn_matmul(sv["mixed"], dz1, False, False, f"l{l}_gwo")
        do = _to_heads(dmixed, H2)
        hd = sv["heads"]
        dq_sb, dk_sb, dv_sb = _sb_bwd(hd[0], hd[1], hd[2], sv["ox_sb"], do[0], f"l{l}_sb_bwd")
        dq_ca, dk_ca, dv_ca, dbias2, dgq, dgk = _ca_bwd(hd[3], hd[4], hd[5], sv["bias2"], gq, gk, do[1],
                                                         f"l{l}_ca_bwd")
        drb = _ca_bias_bwd(dbias2, f"l{l}_ca_bias_bwd")[:, 0, :N_REL]
        dproj = _from_heads(jnp.stack([dq_sb, dk_sb, dv_sb, dq_ca, dk_ca, dv_ca]).astype(_MXU))
        gwin = _tn_matmul(sv["h1"], dproj, True, False, f"l{l}_gwin")
        dx, dsh1, dsc1, dgn1 = _nt_ln_bwd(dproj, W_in[l], sv["x0"], gn1, sc1, sh1, dx, f"l{l}_proj_bwd")
        big_grads[4 * l:4 * l + 4] = [gwin, gwo, gw1, gw2]
        dmod = jnp.concatenate([dsh1, dsc1, dgt1, dsh2, dsc2, dgt2], axis=1)
        small_parts[l] = [dgn1, dgq, dgk, drb, dgn2, dmod]
    grad_x = dx[None]

    recv_a = _rs_sibling_exchange(big_grads, "rs_sibling")
    owns, parts = [], []
    for i in range(4 * L):
        own, part = _rs_chip_partial(place, big_grads[i], recv_a[i], f"rs_partial_{i}")
        owns.append(own)
        parts.append(part)
    recv_b = _rs_chip_exchange(parts, "rs_chips")
    big_out = []
    for t, (w, m, v) in enumerate([(w_in, m_w_in, v_w_in), (w_o, m_w_o, v_w_o), (w1, m_w1, v_w1), (w2, m_w2, v_w2)]):
        big_out.append(_rs_sum_adamw([owns[4 * l + t] for l in range(L)], [recv_b[4 * l + t] for l in range(L)],
                                     w, m, v, f"adamw_big_{t}"))

    packed = _pack_small(small_parts)
    gathered_small = _all_gather_small(packed, "ag_small_grads")
    small_sum = _sum_devices(gathered_small, "sum_small_grads")
    shapes = [[(1, D), (1, HEAD_DIM), (1, HEAD_DIM), (H2, N_REL), (1, D), (1, 6 * D)]] * L
    names = ["g_norm1", "g_q", "g_k", "rel_bias", "g_norm2", "b_ada"]
    small_w = {"g_norm1": (g_norm1, m_g_norm1, v_g_norm1), "g_q": (g_q, m_g_q, v_g_q), "g_k": (g_k, m_g_k, v_g_k),
               "rel_bias": (rel_bias, m_rel_bias, v_rel_bias), "g_norm2": (g_norm2, m_g_norm2, v_g_norm2),
               "b_ada": (b_ada, m_b_ada, v_b_ada)}
    packs = [_pack_small([[small_w[n][k][l] for n in names] for l in range(L)]) for k in range(3)]
    n_pad = packed.shape[1]
    as_rows = lambda a: a.reshape(n_pad // 128, 128)
    sd, sm, sv_ = _adamw(as_rows(packs[0]), as_rows(small_sum), as_rows(packs[1]), as_rows(packs[2]), "adamw_small")
    small_out = {}
    for key, flat in [("grad", small_sum), ("delta", sd), ("m", sm), ("v", sv_)]:
        per_layer = _unpack_small(flat.reshape(-1), shapes)
        for i, n in enumerate(names):
            small_out[(key, n)] = jnp.stack([per_layer[l][i].reshape(small_w[n][0].shape[1:]) for l in range(L)])

    layer_len = 2 * D + 2 * HEAD_DIM + H2 * N_REL + 6 * D
    rows = gathered_small.reshape(NDEV, n_pad)
    dmod_all = jnp.stack([rows[:, l * layer_len + layer_len - 6 * D:(l + 1) * layer_len] for l in range(L)])
    dmod_cols = lax.dynamic_slice(dmod_all, (0, 0, me * Ca), (L, NDEV, Ca))
    dmod_cols = jnp.pad(dmod_cols, ((0, 0), (0, 128 - NDEV), (0, 0)))
    c_t = jnp.pad(c_all.T, ((0, 0), (0, 128 - NDEV)))
    g_ada = _w_ada_grad(c_t, dmod_cols, "w_ada_grad")
    flat2 = lambda a: a.reshape(L * D, Ca)
    ad, am, av = _adamw(flat2(w_ada), flat2(g_ada), flat2(m_w_ada), flat2(v_w_ada), "adamw_w_ada")
    ada_out = [g_ada] + [a.reshape(L, D, Ca) for a in (ad, am, av)]

    def leaf(kind):
        k = {"grad": 0, "delta": 1, "m": 2, "v": 3}[kind]
        return [small_out[(kind, "g_norm1")], big_out[0][k], small_out[(kind, "g_q")], small_out[(kind, "g_k")],
                small_out[(kind, "rel_bias")], big_out[1][k], small_out[(kind, "g_norm2")], big_out[2][k],
                big_out[3][k], ada_out[k], small_out[(kind, "b_ada")]]

    return (loss, grad_x, *leaf("grad"), *leaf("delta"), *leaf("m"), *leaf("v"))
```

```python
import functools

import jax
import jax.numpy as jnp
from jax import lax
from jax.experimental import pallas as pl
from jax.experimental.pallas import tpu as pltpu

F32 = jnp.float32
_MXU = jnp.bfloat16

HEAD_DIM = 64
CHUNK = 64
LEFT_CHUNKS = 8
PAD = LEFT_CHUNKS * CHUNK
BAND = PAD + CHUNK
REL_CLIP = 128
N_REL = 2 * REL_CLIP + 1
EPS = 1e-6
NEG = -1e30
NDEV = 8
SB_T = 128
CA_T = 2 * CHUNK
CA_W = CA_T + PAD
SB_SKIP = -104.0
SB_G = 4
CA_G_FWD = 4
CA_G_BWD = 2
SKEW_W = 767

ADAM_LR, ADAM_B1, ADAM_B2, ADAM_EPS, ADAM_WD, ADAM_STEP = 0.001, 0.9, 0.999, 1e-08, 0.01, 10

MESH = pl.DeviceIdType.MESH
VMEM_SPEC = pl.BlockSpec(memory_space=pltpu.VMEM)
SMEM_SPEC = pl.BlockSpec(memory_space=pltpu.SMEM)
ANY_SPEC = pl.BlockSpec(memory_space=pl.ANY)


def _nn(a, b):
    return lax.dot_general(a, b, (((1,), (0,)), ((), ())), preferred_element_type=F32)


def _nt(a, b):
    return lax.dot_general(a, b, (((1,), (1,)), ((), ())), preferred_element_type=F32)


def _tn(a, b):
    return lax.dot_general(a, b, (((0,), (0,)), ((), ())), preferred_element_type=F32)


def _blk(n, pref):
    return pref if n % pref == 0 else n


def _pos():
    return lax.axis_index("x"), lax.axis_index("y"), lax.axis_index("c")


def _flip(v, bit):
    return 1 - v if bit else v


def _all_gather_small(blk, name):
    R, C = blk.shape

    def body(x_ref, out_ref, send_sems, recv_sems):
        x, y, c = _pos()
        me = 4 * x + 2 * y + c

        def peer(k):
            return (_flip(x, k & 4), _flip(y, k & 2), _flip(c, k & 1))

        def copy(k, slot):
            return pltpu.make_async_remote_copy(
                src_ref=x_ref, dst_ref=out_ref.at[slot], send_sem=send_sems.at[k - 1],
                recv_sem=recv_sems.at[k - 1], device_id=peer(k), device_id_type=MESH)

        out_ref[pl.ds(me, 1), :, :] = x_ref[...].reshape(1, R, C)
        sends = [copy(k, me) for k in range(1, NDEV)]
        for cp in sends:
            cp.start()
        for k in range(1, NDEV):
            px, py, pc = peer(k)
            copy(k, 4 * px + 2 * py + pc).wait_recv()
        for cp in sends:
            cp.wait_send()

    return pl.pallas_call(
        body, name=name,
        out_shape=jax.ShapeDtypeStruct((NDEV, R, C), blk.dtype),
        in_specs=[VMEM_SPEC], out_specs=VMEM_SPEC,
        scratch_shapes=[pltpu.SemaphoreType.DMA((NDEV - 1,)), pltpu.SemaphoreType.DMA((NDEV - 1,))],
    )(blk)


def _all_gather_big(shards, name):
    n = len(shards)

    def body(*refs):
        ins, outs = refs[:n], refs[n:2 * n]
        send_sems, recv_sems, local_sems = refs[2 * n:]
        x, y, c = _pos()
        me, sibling = (x, y, c), (x, y, 1 - c)
        chips = [(1 - x, y), (x, 1 - y), (1 - x, 1 - y)]

        def copy(i, k, block, to, src=None):
            px, py, pc = block
            dst = outs[i].at[4 * px + 2 * py + pc]
            return pltpu.make_async_remote_copy(
                src_ref=dst if src is None else src, dst_ref=dst, send_sem=send_sems.at[7 * i + k],
                recv_sem=recv_sems.at[7 * i + k], device_id=to, device_id_type=MESH)

        mine, first, passed = [], [], []
        for i in range(n):
            mx, my, mc = me
            cp = pltpu.make_async_copy(ins[i], outs[i].at[4 * mx + 2 * my + mc], local_sems.at[i])
            cp.start()
            mine.append(cp)
            group = [copy(i, 0, me, sibling, src=ins[i])]
            group += [copy(i, 1 + j, me, (*chip, c), src=ins[i]) for j, chip in enumerate(chips)]
            for g in group:
                g.start()
            first += group
        for j, chip in enumerate(chips):
            for i in range(n):
                copy(i, 1 + j, (*chip, c), me).wait_recv()
                fw = copy(i, 4 + j, (*chip, c), sibling)
                fw.start()
                passed.append(fw)
        for i in range(n):
            copy(i, 0, sibling, me).wait_recv()
            for j, chip in enumerate(chips):
                copy(i, 4 + j, (*chip, 1 - c), me).wait_recv()
        for cp in first + passed:
            cp.wait_send()
        for cp in mine:
            cp.wait()

    return pl.pallas_call(
        body, name=name,
        out_shape=[jax.ShapeDtypeStruct((NDEV,) + s.shape, s.dtype) for s in shards],
        in_specs=[ANY_SPEC] * n, out_specs=[ANY_SPEC] * n,
        scratch_shapes=[pltpu.SemaphoreType.DMA((7 * n,)), pltpu.SemaphoreType.DMA((7 * n,)),
                        pltpu.SemaphoreType.DMA((n,))],
    )(*shards)


def _rs_sibling_exchange(grads, name):
    n = len(grads)

    def body(*refs):
        ins, outs = refs[:n], refs[n:2 * n]
        send_sems, recv_sems = refs[2 * n:]
        x, y, c = _pos()
        sibling = (x, y, 1 - c)
        copies = []
        for i in range(n):
            for q in range(4):
                cp = pltpu.make_async_remote_copy(
                    src_ref=ins[i].at[2 * q + (1 - c)], dst_ref=outs[i].at[q], send_sem=send_sems.at[4 * i + q],
                    recv_sem=recv_sems.at[4 * i + q], device_id=sibling, device_id_type=MESH)
                cp.start()
                copies.append(cp)
        for cp in copies:
            cp.wait()

    return pl.pallas_call(
        body, name=name,
        out_shape=[jax.ShapeDtypeStruct((4,) + g.shape[1:], g.dtype) for g in grads],
        in_specs=[ANY_SPEC] * n, out_specs=[ANY_SPEC] * n,
        scratch_shapes=[pltpu.SemaphoreType.DMA((4 * n,)), pltpu.SemaphoreType.DMA((4 * n,))],
    )(*grads)


def _rs_chip_exchange(parts, name):
    n = len(parts)

    def body(*refs):
        ins, outs = refs[:n], refs[n:2 * n]
        send_sems, recv_sems = refs[2 * n:]
        x, y, c = _pos()
        copies = []
        for i in range(n):
            for j in range(1, 4):
                cp = pltpu.make_async_remote_copy(
                    src_ref=ins[i].at[j - 1], dst_ref=outs[i].at[j - 1], send_sem=send_sems.at[3 * i + j - 1],
                    recv_sem=recv_sems.at[3 * i + j - 1], device_id=(_flip(x, j & 2), _flip(y, j & 1), c),
                    device_id_type=MESH)
                cp.start()
                copies.append(cp)
        for cp in copies:
            cp.wait()

    return pl.pallas_call(
        body, name=name,
        out_shape=[jax.ShapeDtypeStruct(p.shape, p.dtype) for p in parts],
        in_specs=[ANY_SPEC] * n, out_specs=[ANY_SPEC] * n,
        scratch_shapes=[pltpu.SemaphoreType.DMA((3 * n,)), pltpu.SemaphoreType.DMA((3 * n,))],
    )(*parts)


def _rs_chip_partial(place, grad, recv, name):
    _, R, C = grad.shape
    tr = _blk(R, 256)

    def body(place_ref, *refs):
        g_refs, r_refs = refs[:4], refs[4:8]
        own_ref, out_ref = refs[8:]
        own_ref[...] = g_refs[0][0] + r_refs[0][0]
        for j in range(1, 4):
            out_ref[j - 1] = (g_refs[j][0] + r_refs[j][0]).astype(out_ref.dtype)

    def g_map(j):
        return lambda i, p: (2 * jnp.bitwise_xor(p[0], j) + p[1], i, 0)

    def r_map(j):
        return lambda i, p: (jnp.bitwise_xor(p[0], j), i, 0)

    grid_spec = pltpu.PrefetchScalarGridSpec(
        num_scalar_prefetch=1, grid=(R // tr,),
        in_specs=[pl.BlockSpec((1, tr, C), g_map(j)) for j in range(4)]
        + [pl.BlockSpec((1, tr, C), r_map(j)) for j in range(4)],
        out_specs=[pl.BlockSpec((tr, C), lambda i, p: (i, 0)), pl.BlockSpec((3, tr, C), lambda i, p: (0, i, 0))])
    return pl.pallas_call(
        body, name=name, grid_spec=grid_spec,
        out_shape=[jax.ShapeDtypeStruct((R, C), F32), jax.ShapeDtypeStruct((3, R, C), _MXU)],
    )(place, *([grad] * 4), *([recv] * 4))


def _adamw_math(w, g, m, v):
    m = ADAM_B1 * m + (1.0 - ADAM_B1) * g
    v = ADAM_B2 * v + (1.0 - ADAM_B2) * (g * g)
    m_hat = m / (1.0 - ADAM_B1 ** ADAM_STEP)
    v_hat = v / (1.0 - ADAM_B2 ** ADAM_STEP)
    delta = -ADAM_LR * (m_hat / (jnp.sqrt(v_hat) + ADAM_EPS) + ADAM_WD * w)
    return delta, m, v


def _adamw(w, g, m, v, name):
    R, C = w.shape
    tr = _blk(R, 256)

    def body(w_ref, g_ref, m_ref, v_ref, d_ref, nm_ref, nv_ref):
        d, nm, nv = _adamw_math(w_ref[...], g_ref[...], m_ref[...], v_ref[...])
        d_ref[...] = d
        nm_ref[...] = nm
        nv_ref[...] = nv

    spec = pl.BlockSpec((tr, C), lambda i: (i, 0))
    return pl.pallas_call(
        body, name=name, grid=(R // tr,), in_specs=[spec] * 4, out_specs=[spec] * 3,
        out_shape=[jax.ShapeDtypeStruct((R, C), F32)] * 3,
    )(w, g, m, v)


def _rs_sum_adamw(owns, recvs, w, m, v, name):
    L, R, C = w.shape
    tr = _blk(R, 256)
    nr = R // tr

    def body(o0, o1, r0, r1, w_ref, m_ref, v_ref, g_ref, d_ref, nm_ref, nv_ref):
        def step(o_ref, r_ref):
            g = o_ref[...]
            for j in range(3):
                g = g + r_ref[j].astype(F32)
            d, nm, nv = _adamw_math(w_ref[0], g, m_ref[0], v_ref[0])
            g_ref[0] = g
            d_ref[0] = d
            nm_ref[0] = nm
            nv_ref[0] = nv

        pl.when(pl.program_id(0) == 0)(lambda: step(o0, r0))
        pl.when(pl.program_id(0) == 1)(lambda: step(o1, r1))

    def hold(layer):
        if layer == 0:
            return lambda l, i: i * (1 - l) + (nr - 1) * l
        return lambda l, i: i * l

    own_spec = [pl.BlockSpec((tr, C), functools.partial(lambda l, i, f: (f(l, i), 0), f=hold(k))) for k in range(2)]
    recv_spec = [pl.BlockSpec((3, tr, C), functools.partial(lambda l, i, f: (0, f(l, i), 0), f=hold(k)))
                 for k in range(2)]
    lay = pl.BlockSpec((1, tr, C), lambda l, i: (l, i, 0))
    return pl.pallas_call(
        body, name=name, grid=(L, nr),
        in_specs=own_spec + recv_spec + [lay] * 3, out_specs=[lay] * 4,
        out_shape=[jax.ShapeDtypeStruct((L, R, C), F32)] * 4,
    )(owns[0], owns[1], recvs[0], recvs[1], w, m, v)


def _silu(x):
    return x / (1.0 + jnp.exp(-x))


def _mod_partial(c_all, w_ada, b_cols, name):
    L, D, Ca = w_ada.shape

    def body(c_ref, w_ref, b_ref, o_ref):
        act = _silu(c_ref[...]).astype(_MXU)
        for l in range(L):
            o_ref[:, l * Ca:(l + 1) * Ca] = _nn(act, w_ref[l].astype(_MXU)) + b_ref[l:l + 1, :]

    return pl.pallas_call(
        body, name=name, out_shape=jax.ShapeDtypeStruct((NDEV, L * Ca), F32),
        in_specs=[VMEM_SPEC] * 3, out_specs=VMEM_SPEC,
    )(c_all, w_ada, b_cols)


def _w_ada_grad(c_t, dmod_cols, name):
    L, _, Ca = dmod_cols.shape
    D = c_t.shape[0]

    def body(c_ref, d_ref, o_ref):
        act = _silu(c_ref[...]).astype(_MXU)
        for l in range(L):
            o_ref[l] = _nn(act, d_ref[l].astype(_MXU))

    return pl.pallas_call(
        body, name=name, out_shape=jax.ShapeDtypeStruct((L, D, Ca), F32),
        in_specs=[VMEM_SPEC] * 2, out_specs=VMEM_SPEC,
    )(c_t, dmod_cols)


def _sum_devices(gathered, name):
    _, _, N = gathered.shape

    def body(x_ref, o_ref):
        acc = x_ref[0]
        for d in range(1, NDEV):
            acc = acc + x_ref[d]
        o_ref[...] = acc

    return pl.pallas_call(
        body, name=name, out_shape=jax.ShapeDtypeStruct((1, N), F32),
        in_specs=[VMEM_SPEC], out_specs=VMEM_SPEC,
    )(gathered)


def _ln_mod_matmul(x, g, sc, sh, wg, name):
    S, D = x.shape
    nb, _, Cs = wg.shape
    tm = _blk(S, 1024)

    def body(x_ref, g_ref, sc_ref, sh_ref, w_ref, o_ref, h_ref, hs_ref):
        @pl.when(pl.program_id(1) == 0)
        def _():
            xv = x_ref[...]
            r = lax.rsqrt(jnp.mean(xv * xv, axis=-1, keepdims=True) + EPS)
            hv = ((xv * r) * g_ref[...]) * (1.0 + sc_ref[...]) + sh_ref[...]
            hb = hv.astype(_MXU)
            hs_ref[...] = hb
            h_ref[...] = hb

        o_ref[...] = _nn(hs_ref[...], w_ref[0]).astype(o_ref.dtype)

    vec = pl.BlockSpec((1, D), lambda i, j: (0, 0))
    return pl.pallas_call(
        body, name=name, grid=(S // tm, nb),
        in_specs=[pl.BlockSpec((tm, D), lambda i, j: (i, 0)), vec, vec, vec,
                  pl.BlockSpec((1, D, Cs), lambda i, j: (j, 0, 0))],
        out_specs=[pl.BlockSpec((tm, Cs), lambda i, j: (i, j)), pl.BlockSpec((tm, D), lambda i, j: (i, 0))],
        out_shape=[jax.ShapeDtypeStruct((S, nb * Cs), _MXU), jax.ShapeDtypeStruct((S, D), _MXU)],
        scratch_shapes=[pltpu.VMEM((tm, D), _MXU)],
    )(x, g, sc, sh, wg)


def _matmul_res_gate(a, w, xres, gt, relu2, name):
    S, K = a.shape
    N = w.shape[1]
    tm = _blk(S, 512)

    def body(a_ref, w_ref, x_ref, gt_ref, o_ref, f_ref):
        av = a_ref[...]
        if relu2:
            af = jnp.maximum(av.astype(F32), 0.0)
            av = (af * af).astype(_MXU)
        f = _nn(av, w_ref[...])
        f_ref[...] = f.astype(f_ref.dtype)
        o_ref[...] = x_ref[...] + gt_ref[...] * f

    row = lambda width: pl.BlockSpec((tm, width), lambda i: (i, 0))
    return pl.pallas_call(
        body, name=name, grid=(S // tm,),
        in_specs=[row(K), pl.BlockSpec((K, N), lambda i: (0, 0)), row(N), pl.BlockSpec((1, N), lambda i: (0, 0))],
        out_specs=[row(N), row(N)],
        out_shape=[jax.ShapeDtypeStruct((S, N), F32), jax.ShapeDtypeStruct((S, N), _MXU)],
    )(a, w, xres, gt)


def _loss_grad(y, t, name):
    S, D = y.shape
    tm = _blk(S, 512)
    last = S // tm - 1

    def body(y_ref, t_ref, dy_ref, l_ref, acc_ref):
        i = pl.program_id(0)
        e = y_ref[...] - t_ref[...]
        dy_ref[...] = e * (1.0 / D)
        part = jnp.sum(e * e, axis=0, keepdims=True)

        @pl.when(i == 0)
        def _():
            acc_ref[...] = part

        @pl.when(i > 0)
        def _():
            acc_ref[...] += part

        @pl.when(i == last)
        def _():
            l_ref[...] = (0.5 / D) * jnp.sum(acc_ref[...], axis=1, keepdims=True)

    row = pl.BlockSpec((tm, D), lambda i: (i, 0))
    return pl.pallas_call(
        body, name=name, grid=(S // tm,), in_specs=[row, row],
        out_specs=[row, pl.BlockSpec((1, 1), lambda i: (0, 0))],
        out_shape=[jax.ShapeDtypeStruct((S, D), F32), jax.ShapeDtypeStruct((1, 1), F32)],
        scratch_shapes=[pltpu.VMEM((1, D), F32)],
    )(y, t)


def _accumulate(ref, part, first):
    @pl.when(first)
    def _():
        ref[...] = part

    @pl.when(jnp.logical_not(first))
    def _():
        ref[...] += part


def _gate_nt_matmul(dx, f, gt, w, u, name):
    S, D = dx.shape
    N = w.shape[0]
    tm, tn = _blk(S, 512), _blk(N, 1024)
    with_u = u is not None

    def body(*refs):
        if with_u:
            dx_ref, f_ref, gt_ref, w_ref, u_ref, dz_ref, dgt_ref, res_ref, dzs_ref = refs
        else:
            dx_ref, f_ref, gt_ref, w_ref, dz_ref, dgt_ref, res_ref, dzs_ref = refs
        i, j = pl.program_id(0), pl.program_id(1)

        @pl.when(j == 0)
        def _():
            dxv = dx_ref[...]
            dz = (dxv * gt_ref[...]).astype(_MXU)
            dzs_ref[...] = dz
            dz_ref[...] = dz
            _accumulate(dgt_ref, jnp.sum(dxv * f_ref[...].astype(F32), axis=0, keepdims=True), i == 0)

        r = _nt(dzs_ref[...], w_ref[...])
        if with_u:
            r = r * (2.0 * jnp.maximum(u_ref[...].astype(F32), 0.0))
        res_ref[...] = r.astype(res_ref.dtype)

    row = pl.BlockSpec((tm, D), lambda i, j: (i, 0))
    tile = pl.BlockSpec((tm, tn), lambda i, j: (i, j))
    in_specs = [row, row, pl.BlockSpec((1, D), lambda i, j: (0, 0)), pl.BlockSpec((tn, D), lambda i, j: (j, 0))]
    args = [dx, f, gt, w]
    if with_u:
        in_specs.append(tile)
        args.append(u)
    return pl.pallas_call(
        body, name=name, grid=(S // tm, N // tn), in_specs=in_specs,
        out_specs=[row, pl.BlockSpec((1, D), lambda i, j: (0, 0)), tile],
        out_shape=[jax.ShapeDtypeStruct((S, D), _MXU), jax.ShapeDtypeStruct((1, D), F32),
                   jax.ShapeDtypeStruct((S, N), _MXU)],
        scratch_shapes=[pltpu.VMEM((tm, D), _MXU)],
    )(*args)


def _tn_matmul(a, b, by_col, relu2, name):
    S, Ka = a.shape
    Nb = b.shape[1]
    ts = _blk(S, 1024)
    if by_col:
        R, C = Ka, Nb // NDEV
        a_spec = pl.BlockSpec((ts, Ka), lambda d, k: (k, 0))
        b_spec = pl.BlockSpec((ts, C), lambda d, k: (k, d))
    else:
        R, C = Ka // NDEV, Nb
        a_spec = pl.BlockSpec((ts, R), lambda d, k: (k, d))
        b_spec = pl.BlockSpec((ts, Nb), lambda d, k: (k, 0))

    def body(a_ref, b_ref, o_ref):
        av = a_ref[...]
        if relu2:
            af = jnp.maximum(av.astype(F32), 0.0)
            av = (af * af).astype(_MXU)
        p = _tn(av, b_ref[...])
        k = pl.program_id(1)

        @pl.when(k == 0)
        def _():
            o_ref[0] = p

        @pl.when(k > 0)
        def _():
            o_ref[0] += p

    return pl.pallas_call(
        body, name=name, grid=(NDEV, S // ts), in_specs=[a_spec, b_spec],
        out_specs=pl.BlockSpec((1, R, C), lambda d, k: (d, 0, 0)),
        out_shape=jax.ShapeDtypeStruct((NDEV, R, C), F32),
    )(a, b)


def _nt_ln_bwd(dy, wg, x, g, sc, sh, dxres, name):
    S, D = x.shape
    nb, _, Cs = wg.shape
    tm = _blk(S, 512)

    def body(dy_ref, w_ref, x_ref, g_ref, sc_ref, sh_ref, dxr_ref, dx_ref, dsh_ref, dsc_ref, dg_ref, acc_ref):
        i, k = pl.program_id(0), pl.program_id(1)
        p = _nt(dy_ref[...], w_ref[0])

        @pl.when(k == 0)
        def _():
            acc_ref[...] = p

        @pl.when(k > 0)
        def _():
            acc_ref[...] += p

        @pl.when(k == nb - 1)
        def _():
            dh = acc_ref[...]
            xv = x_ref[...]
            r = lax.rsqrt(jnp.mean(xv * xv, axis=-1, keepdims=True) + EPS)
            xhat = xv * r
            gv = g_ref[...]
            dn = dh * (1.0 + sc_ref[...])
            dxhat = dn * gv
            dxv = r * (dxhat - xhat * jnp.mean(dxhat * xhat, axis=-1, keepdims=True))
            dx_ref[...] = dxr_ref[...] + dxv
            first = i == 0
            _accumulate(dsh_ref, jnp.sum(dh, axis=0, keepdims=True), first)
            _accumulate(dsc_ref, jnp.sum(dh * (xhat * gv), axis=0, keepdims=True), first)
            _accumulate(dg_ref, jnp.sum(dn * xhat, axis=0, keepdims=True), first)

    row = pl.BlockSpec((tm, D), lambda i, k: (i, 0))
    vec = pl.BlockSpec((1, D), lambda i, k: (0, 0))
    return pl.pallas_call(
        body, name=name, grid=(S // tm, nb),
        in_specs=[pl.BlockSpec((tm, Cs), lambda i, k: (i, k)), pl.BlockSpec((1, D, Cs), lambda i, k: (k, 0, 0)),
                  row, vec, vec, vec, row],
        out_specs=[row, vec, vec, vec],
        out_shape=[jax.ShapeDtypeStruct((S, D), F32)] + [jax.ShapeDtypeStruct((1, D), F32)] * 3,
        scratch_shapes=[pltpu.VMEM((tm, D), F32)],
    )(dy, wg, x, g, sc, sh, dxres)


def _split3(v):
    hi = v.astype(_MXU)
    r1 = v - hi.astype(F32)
    mid = r1.astype(_MXU)
    lo = (r1 - mid.astype(F32)).astype(_MXU)
    return hi, mid, lo


def _tri_sums(v, tri2):
    T = v.shape[0]
    hi, mid, lo = _split3(v)
    s = _nn(hi, tri2) + _nn(mid, tri2) + _nn(lo, tri2)
    return s[:, :T], s[:, T:]


def _tri2(T, inclusive):
    j = lax.broadcasted_iota(jnp.int32, (T, 2 * T), 0)
    s = lax.broadcasted_iota(jnp.int32, (T, 2 * T), 1)
    keep = (j >= s) if inclusive else (j > s)
    return jnp.where((s >= T) | keep, 1.0, 0.0).astype(_MXU)


def _log_sigmoid(z):
    return jnp.minimum(z, 0.0) - jnp.log(1.0 + jnp.exp(-jnp.abs(z)))


def _sb_blocks(qbs, kblks, strict, tri2, carry):
    scale = HEAD_DIM ** -0.5
    zs = [_nt(qb, kblk) * scale for qb, kblk in zip(qbs, kblks)]
    lbs, sums = [], []
    for z in zs:
        lb = _log_sigmoid(z)
        l1 = lb - z
        if strict is not None:
            l1 = jnp.where(strict, l1, 0.0)
        lbs.append(lb)
        sums.append(_tri_sums(l1, tri2))
    amps, new_carry = [], []
    for lb, (sfx, tot), c in zip(lbs, sums, carry):
        a = jnp.exp(lb + sfx + c)
        if strict is not None:
            a = jnp.where(strict, a, 0.0)
        amps.append(a)
        new_carry.append(c + tot)
    return lbs, amps, new_carry


def _sb_alive(carry):
    top = carry[0]
    for c in carry[1:]:
        top = jnp.maximum(top, c)
    return jnp.max(top) > SB_SKIP


def _sb_fwd(q, k, v, name):
    H, S, hd = q.shape
    T = _blk(S, SB_T)
    G = _blk(H, SB_G)

    def body(q_ref, k_ref, v_ref, o_ref, ox_ref):
        qi = pl.program_id(1)
        qbs = [q_ref[g] for g in range(G)]
        row = lax.broadcasted_iota(jnp.int32, (T, T), 0)
        col = lax.broadcasted_iota(jnp.int32, (T, T), 1)
        tri2 = _tri2(T, inclusive=False)

        def pairs(kb, carry, acc, fine, strict):
            start = pl.multiple_of(kb * T, T)
            kblks = [k_ref[g, pl.ds(start, T), :] for g in range(G)]
            _, amps, carry = _sb_blocks(qbs, kblks, strict, tri2, carry)
            parts = [_split3(a) for a in amps]
            vblks = [v_ref[g, pl.ds(start, T), :] for g in range(G)]
            acc = tuple(acc[g] + _nn(parts[g][0], vblks[g]) for g in range(G))
            fine = tuple(fine[g] + _nn(parts[g][1], vblks[g]) for g in range(G))
            return tuple(carry), acc, fine

        zero = (jnp.zeros((T, hd), F32),) * G
        carry, acc, fine = pairs(qi, (jnp.zeros((T, T), F32),) * G, zero, zero, col < row)

        def cond(st):
            kb, carry, _, _ = st
            return jnp.logical_and(kb >= 0, _sb_alive(carry))

        def step(st):
            kb, carry, acc, fine = st
            carry, acc, fine = pairs(kb, carry, acc, fine, None)
            return kb - 1, carry, acc, fine

        _, _, acc, fine = lax.while_loop(cond, step, (qi - 1, carry, acc, fine))
        for g in range(G):
            o_ref[g] = acc[g].astype(o_ref.dtype)
            ox_ref[g] = acc[g] + fine[g]

    blk = pl.BlockSpec((G, T, hd), lambda h, i: (h, i, 0))
    full = pl.BlockSpec((G, S, hd), lambda h, i: (h, 0, 0))
    return pl.pallas_call(
        body, name=name, grid=(H // G, S // T), in_specs=[blk, full, full], out_specs=[blk, blk],
        out_shape=[jax.ShapeDtypeStruct((H, S, hd), _MXU), jax.ShapeDtypeStruct((H, S, hd), F32)],
    )(q, k, v)


def _sb_bwd(q, k, v, ox, do, name):
    H, S, hd = q.shape
    T = _blk(S, SB_T)
    G = _blk(H, SB_G)
    scale = HEAD_DIM ** -0.5

    def body(q_ref, k_ref, v_ref, o_ref, do_ref, dq_ref, dk_ref, dv_ref):
        qi = pl.program_id(1)

        @pl.when(qi == 0)
        def _():
            dk_ref[...] = jnp.zeros_like(dk_ref)
            dv_ref[...] = jnp.zeros_like(dv_ref)

        qbs = [q_ref[g] for g in range(G)]
        dobs = [do_ref[g] for g in range(G)]
        deltas = [jnp.sum(dobs[g].astype(F32) * o_ref[g], axis=-1, keepdims=True) for g in range(G)]
        row = lax.broadcasted_iota(jnp.int32, (T, T), 0)
        col = lax.broadcasted_iota(jnp.int32, (T, T), 1)
        tri_ex = _tri2(T, inclusive=False)
        tri_in = _tri2(T, inclusive=True)

        def pairs(kb, carry, right, dq, strict):
            start = pl.multiple_of(kb * T, T)
            rows = pl.ds(start, T)
            kblks = [k_ref[g, rows, :] for g in range(G)]
            gs = [_nt(dobs[g], v_ref[g, rows, :]) for g in range(G)]
            lbs, amps, carry = _sb_blocks(qbs, kblks, strict, tri_ex, carry)
            ags = [a * gg for a, gg in zip(amps, gs)]
            sums = [_tri_sums(ag, tri_in) for ag in ags]
            dzbs = []
            for g in range(G):
                sfx, _ = sums[g]
                left = deltas[g] - (sfx + right[g])
                beta = jnp.exp(lbs[g])
                dz = ags[g] * (1.0 - beta) - beta * left
                if strict is not None:
                    dz = jnp.where(strict, dz, 0.0)
                dzbs.append((dz * scale).astype(_MXU))
            for g in range(G):
                dk_ref[g, rows, :] += _tn(dzbs[g], qbs[g])
                dv_ref[g, rows, :] += _tn(amps[g].astype(_MXU), dobs[g])
            right = tuple(right[g] + sums[g][1] for g in range(G))
            dq = tuple(dq[g] + _nn(dzbs[g], kblks[g]) for g in range(G))
            return tuple(carry), right, dq

        zero = (jnp.zeros((T, T), F32),) * G
        carry, right, dq = pairs(qi, zero, zero, (jnp.zeros((T, hd), F32),) * G, col < row)

        def cond(st):
            kb, carry, _, _ = st
            return jnp.logical_and(kb >= 0, _sb_alive(carry))

        def step(st):
            kb, carry, right, dq = st
            carry, right, dq = pairs(kb, carry, right, dq, None)
            return kb - 1, carry, right, dq

        _, _, _, dq = lax.while_loop(cond, step, (qi - 1, carry, right, dq))
        for g in range(G):
            dq_ref[g] = dq[g]

    blk = pl.BlockSpec((G, T, hd), lambda h, i: (h, i, 0))
    full = pl.BlockSpec((G, S, hd), lambda h, i: (h, 0, 0))
    return pl.pallas_call(
        body, name=name, grid=(H // G, S // T), in_specs=[blk, full, full, blk, blk], out_specs=[blk, full, full],
        out_shape=[jax.ShapeDtypeStruct((H, S, hd), F32)] * 3,
    )(q, k, v, ox, do)


def _skew_index():
    i = lax.broadcasted_iota(jnp.int32, (CA_T, SKEW_W + 1), 0)
    m = lax.broadcasted_iota(jnp.int32, (CA_T, SKEW_W + 1), 1)
    wrapped = i + m >= SKEW_W
    row = jnp.where(wrapped, i + 1, i)
    j = jnp.where(wrapped, i + m - SKEW_W, i + m)
    a = row // CHUNK
    jj = j - a * CHUNK
    inband = (jj >= 0) & (jj < BAND) & (j < CA_W) & (row < CA_T)
    idx = jnp.clip((row - a * CHUNK) + PAD - jj, -REL_CLIP, REL_CLIP) + REL_CLIP
    return inband, idx, wrapped


def _skew(tile):
    H = tile.shape[0]
    flat = jnp.pad(tile, ((0, 0), (0, 0), (0, SKEW_W - CA_W))).reshape(H, CA_T * SKEW_W)
    return jnp.pad(flat, ((0, 0), (0, CA_T))).reshape(H, CA_T, SKEW_W + 1)


def _unskew(view):
    H = view.shape[0]
    flat = view.reshape(H, CA_T * (SKEW_W + 1))[:, :CA_T * SKEW_W]
    return flat.reshape(H, CA_T, SKEW_W)[:, :, :CA_W]


def _ca_bias(rel_bias, name):
    H = rel_bias.shape[0]
    top = rel_bias[:, N_REL - 1:]
    by_offset = jnp.concatenate(
        [jnp.broadcast_to(top, (H, PAD - REL_CLIP + 1)), jnp.flip(rel_bias[:, :N_REL - 1], axis=1),
         jnp.broadcast_to(top, (H, SKEW_W + 1 - (PAD - REL_CLIP + 1) - (N_REL - 1)))], axis=1)

    def body(t_ref, o_ref):
        inband, _, wrapped = _skew_index()
        vals = jnp.where(wrapped, t_ref[0][:, 0:1], t_ref[0])
        o_ref[0] = jnp.where(inband, vals, NEG)

    view = pl.pallas_call(
        body, name=name, grid=(H,), in_specs=[pl.BlockSpec((1, 1, SKEW_W + 1), lambda h: (h, 0, 0))],
        out_specs=pl.BlockSpec((1, CA_T, SKEW_W + 1), lambda h: (h, 0, 0)),
        out_shape=jax.ShapeDtypeStruct((H, CA_T, SKEW_W + 1), F32),
    )(by_offset.reshape(H, 1, SKEW_W + 1))
    return _unskew(view)


def _ca_bias_bwd(dbias, name):
    H = dbias.shape[0]

    def body(d_ref, o_ref):
        inband, idx, _ = _skew_index()
        d = jnp.where(inband, d_ref[0], 0.0)
        clipped = idx == N_REL - 1
        by_offset = jnp.sum(jnp.where(clipped, 0.0, d), axis=0, keepdims=True)
        top = jnp.sum(jnp.sum(jnp.where(clipped, d, 0.0), axis=0, keepdims=True), axis=1, keepdims=True)
        lane = lax.broadcasted_iota(jnp.int32, (1, SKEW_W + 1), 1)
        o_ref[0] = jnp.where(lane == 0, top, by_offset)

    out = pl.pallas_call(
        body, name=name, grid=(H,), in_specs=[pl.BlockSpec((1, CA_T, SKEW_W + 1), lambda h: (h, 0, 0))],
        out_specs=pl.BlockSpec((1, 1, SKEW_W + 1), lambda h: (h, 0, 0)),
        out_shape=jax.ShapeDtypeStruct((H, 1, SKEW_W + 1), F32),
    )(_skew(dbias))[:, 0]
    first = PAD - REL_CLIP + 1
    return jnp.concatenate([jnp.flip(out[:, first:first + N_REL - 1], axis=1), out[:, 0:1]], axis=1)


def _head_norm(t, g):
    tf = t.astype(F32)
    r = lax.rsqrt(jnp.mean(tf * tf, axis=-1, keepdims=True) + EPS)
    hat = tf * r
    return hat * g, hat, r


def _head_norm_bwd(dn, hat, r, g):
    dhat = dn * g
    return r * (dhat - hat * jnp.mean(dhat * hat, axis=-1, keepdims=True))


def _ca_fill(g, k_ref, v_ref, gk_ref, kn_ref, vp_ref):
    S, hd = k_ref.shape[1:]
    kn, _, _ = _head_norm(k_ref[g], gk_ref[...])
    kn_ref[g, 0:PAD, :] = jnp.zeros((PAD, hd), kn_ref.dtype)
    vp_ref[g, 0:PAD, :] = jnp.zeros((PAD, hd), vp_ref.dtype)
    kn_ref[g, PAD:PAD + S, :] = kn.astype(kn_ref.dtype)
    vp_ref[g, PAD:PAD + S, :] = v_ref[g]


def _ca_scores(g, q_ref, b2_ref, gq_ref, kn_ref, qi):
    qn, qhat, r = _head_norm(q_ref[g], gq_ref[...])
    qn = (qn * HEAD_DIM ** -0.5).astype(_MXU)
    start = pl.multiple_of(qi * CA_T, CA_T)
    s = _nt(qn, kn_ref[g, pl.ds(start, CA_W), :]) + b2_ref[g]
    key_pos = qi * CA_T - PAD + lax.broadcasted_iota(jnp.int32, (CA_T, CA_W), 1)
    return jnp.where(key_pos >= 0, s, NEG), qn, qhat, r


def _softmax(s):
    e = jnp.exp(s - jnp.max(s, axis=-1, keepdims=True))
    return e * (1.0 / jnp.sum(e, axis=-1, keepdims=True))


def _ca_fwd(q, k, v, bias2, gq, gk, name):
    H, S, hd = q.shape
    G = _blk(H, CA_G_FWD)

    def body(q_ref, k_ref, v_ref, b2_ref, gq_ref, gk_ref, o_ref, kn_ref, vp_ref):
        qi = pl.program_id(1)

        @pl.when(qi == 0)
        def _():
            for g in range(G):
                _ca_fill(g, k_ref, v_ref, gk_ref, kn_ref, vp_ref)

        band = pl.ds(pl.multiple_of(qi * CA_T, CA_T), CA_W)
        scores = [_ca_scores(g, q_ref, b2_ref, gq_ref, kn_ref, qi)[0] for g in range(G)]
        probs = [_softmax(s).astype(_MXU) for s in scores]
        for g in range(G):
            o_ref[g] = _nn(probs[g], vp_ref[g, band, :]).astype(o_ref.dtype)

    blk = pl.BlockSpec((G, CA_T, hd), lambda h, i: (h, i, 0))
    full = pl.BlockSpec((G, S, hd), lambda h, i: (h, 0, 0))
    vec = pl.BlockSpec((1, hd), lambda h, i: (0, 0))
    return pl.pallas_call(
        body, name=name, grid=(H // G, S // CA_T),
        in_specs=[blk, full, full, pl.BlockSpec((G, CA_T, CA_W), lambda h, i: (h, 0, 0)), vec, vec],
        out_specs=blk, out_shape=jax.ShapeDtypeStruct((H, S, hd), _MXU),
        scratch_shapes=[pltpu.VMEM((G, PAD + S, hd), _MXU), pltpu.VMEM((G, PAD + S, hd), _MXU)],
    )(q, k, v, bias2, gq, gk)


def _ca_bwd(q, k, v, bias2, gq, gk, do, name):
    H, S, hd = q.shape
    G = _blk(H, CA_G_BWD)
    scale = HEAD_DIM ** -0.5
    last = S // CA_T - 1

    def body(q_ref, k_ref, v_ref, b2_ref, gq_ref, gk_ref, do_ref,
             dq_ref, dk_ref, dv_ref, db_ref, dgq_ref, dgk_ref, kn_ref, vp_ref, dkn_ref, dvp_ref):
        h, qi = pl.program_id(0), pl.program_id(1)

        @pl.when(qi == 0)
        def _():
            for g in range(G):
                _ca_fill(g, k_ref, v_ref, gk_ref, kn_ref, vp_ref)
            dkn_ref[...] = jnp.zeros_like(dkn_ref)
            dvp_ref[...] = jnp.zeros_like(dvp_ref)
            db_ref[...] = jnp.zeros_like(db_ref)

        @pl.when(jnp.logical_and(h == 0, qi == 0))
        def _():
            dgq_ref[...] = jnp.zeros_like(dgq_ref)
            dgk_ref[...] = jnp.zeros_like(dgk_ref)

        band = pl.ds(pl.multiple_of(qi * CA_T, CA_T), CA_W)
        heads = [_ca_scores(g, q_ref, b2_ref, gq_ref, kn_ref, qi) for g in range(G)]
        dobs = [do_ref[g] for g in range(G)]
        dps = [_nt(dobs[g], vp_ref[g, band, :]) for g in range(G)]
        probs, dsbs = [], []
        for g in range(G):
            p = _softmax(heads[g][0])
            ds = p * (dps[g] - jnp.sum(p * dps[g], axis=-1, keepdims=True))
            db_ref[g] += ds
            probs.append(p.astype(_MXU))
            dsbs.append(ds.astype(_MXU))
        dqns = [_nn(dsbs[g], kn_ref[g, band, :]) * scale for g in range(G)]
        for g in range(G):
            dkn_ref[g, band, :] += _tn(dsbs[g], heads[g][1])
            dvp_ref[g, band, :] += _tn(probs[g], dobs[g])
        dgq = jnp.zeros((1, hd), F32)
        for g in range(G):
            _, _, qhat, r = heads[g]
            dgq = dgq + jnp.sum(dqns[g] * qhat, axis=0, keepdims=True)
            dq_ref[g] = _head_norm_bwd(dqns[g], qhat, r, gq_ref[...])
        dgq_ref[...] += dgq

        @pl.when(qi == last)
        def _():
            for g in range(G):
                _, khat, rk = _head_norm(k_ref[g], gk_ref[...])
                dkn = dkn_ref[g, PAD:PAD + S, :]
                dgk_ref[...] += jnp.sum(dkn * khat, axis=0, keepdims=True)
                dk_ref[g] = _head_norm_bwd(dkn, khat, rk, gk_ref[...])
                dv_ref[g] = dvp_ref[g, PAD:PAD + S, :]

    blk = pl.BlockSpec((G, CA_T, hd), lambda h, i: (h, i, 0))
    full = pl.BlockSpec((G, S, hd), lambda h, i: (h, 0, 0))
    vec = pl.BlockSpec((1, hd), lambda h, i: (0, 0))
    tile = pl.BlockSpec((G, CA_T, CA_W), lambda h, i: (h, 0, 0))
    return pl.pallas_call(
        body, name=name, grid=(H // G, S // CA_T),
        in_specs=[blk, full, full, tile, vec, vec, blk],
        out_specs=[blk, full, full, tile, vec, vec],
        out_shape=[jax.ShapeDtypeStruct((H, S, hd), F32)] * 3
        + [jax.ShapeDtypeStruct((H, CA_T, CA_W), F32), jax.ShapeDtypeStruct((1, hd), F32),
           jax.ShapeDtypeStruct((1, hd), F32)],
        scratch_shapes=[pltpu.VMEM((G, PAD + S, hd), _MXU), pltpu.VMEM((G, PAD + S, hd), _MXU),
                        pltpu.VMEM((G, PAD + S, hd), F32), pltpu.VMEM((G, PAD + S, hd), F32)],
    )(q, k, v, bias2, gq, gk, do)


def _to_heads(t, H):
    S, W = t.shape
    return t.reshape(S, W // (H * HEAD_DIM), H, HEAD_DIM).transpose(1, 2, 0, 3)


def _from_heads(t):
    n, H, S, hd = t.shape
    return t.transpose(2, 0, 1, 3).reshape(S, n * H * hd)


def _pack_small(parts):
    flat = jnp.concatenate([p.reshape(-1) for layer in parts for p in layer])
    n = flat.shape[0]
    n_pad = -(-n // 1024) * 1024
    return jnp.pad(flat, (0, n_pad - n)).reshape(1, n_pad)


def _unpack_small(flat, shapes):
    out, off = [], 0
    for layer in shapes:
        cur = []
        for shp in layer:
            size = 1
            for s in shp:
                size *= s
            cur.append(flat[off:off + size].reshape(shp))
            off += size
        out.append(cur)
    return out


def kernel(x, c, g_norm1, w_in, g_q, g_k, rel_bias, w_o, g_norm2, w1, w2, w_ada, b_ada, loss_target, m_g_norm1, m_w_in, m_g_q, m_g_k, m_rel_bias, m_w_o, m_g_norm2, m_w1, m_w2, m_w_ada, m_b_ada, v_g_norm1, v_w_in, v_g_q, v_g_k, v_rel_bias, v_w_o, v_g_norm2, v_w1, v_w2, v_w_ada, v_b_ada):
    L = w_in.shape[0]
    S, D = x.shape[1:]
    H2 = D // HEAD_DIM // 2
    Ca = w_ada.shape[2]
    xi, yi, ci = _pos()
    me = 4 * xi + 2 * yi + ci
    place = jnp.stack([2 * xi + yi, ci]).astype(jnp.int32)

    c_all = _all_gather_small(c, "ag_c").reshape(NDEV, D)
    b_cols = lax.dynamic_slice(b_ada, (0, me * Ca), (L, Ca))
    mod_part = _mod_partial(c_all, w_ada, b_cols, "mod_partial")
    mod_all = _all_gather_small(mod_part, "ag_mod")
    mod = lax.dynamic_index_in_dim(mod_all, me, axis=1, keepdims=False)
    mod = mod.reshape(NDEV, L, Ca).transpose(1, 0, 2).reshape(L, 6, 1, D)

    shards = []
    for l in range(L):
        shards += [w_in[l].astype(_MXU), w_o[l].astype(_MXU), w1[l].astype(_MXU), w2[l].astype(_MXU)]
    gathered = _all_gather_big(shards, "ag_weights")
    W_in = [gathered[4 * l + 0] for l in range(L)]
    W_o = [gathered[4 * l + 1].reshape(D, D) for l in range(L)]
    W_1 = [gathered[4 * l + 2] for l in range(L)]
    W_2 = [gathered[4 * l + 3].reshape(4 * D, D) for l in range(L)]

    xs = [x[0]]
    saved = []
    for l in range(L):
        sh1, sc1, gt1, sh2, sc2, gt2 = [mod[l, i] for i in range(6)]
        gn1, gn2 = g_norm1[l:l + 1], g_norm2[l:l + 1]
        gq, gk = g_q[l:l + 1], g_k[l:l + 1]
        proj, h1 = _ln_mod_matmul(xs[-1], gn1, sc1, sh1, W_in[l], f"l{l}_proj")
        heads = _to_heads(proj, H2)
        o_sb, ox_sb = _sb_fwd(heads[0], heads[1], heads[2], f"l{l}_sb_fwd")
        bias2 = _ca_bias(rel_bias[l], f"l{l}_ca_bias")
        o_ca = _ca_fwd(heads[3], heads[4], heads[5], bias2, gq, gk, f"l{l}_ca_fwd")
        mixed = _from_heads(jnp.stack([o_sb, o_ca]))
        x1, f1 = _matmul_res_gate(mixed, W_o[l], xs[-1], gt1, False, f"l{l}_attn_out")
        u, h2 = _ln_mod_matmul(x1, gn2, sc2, sh2, W_1[l], f"l{l}_mlp_in")
        x2, f2 = _matmul_res_gate(u, W_2[l], x1, gt2, True, f"l{l}_mlp_out")
        saved.append(dict(x0=xs[-1], h1=h1, heads=heads, ox_sb=ox_sb, bias2=bias2, mixed=mixed, f1=f1, x1=x1,
                          h2=h2, u=u, f2=f2))
        xs.append(x2)

    dx, loss_part = _loss_grad(xs[-1], loss_target[0], "loss")
    loss = lax.psum(loss_part[0, 0], ("x", "y", "c"))

    big_grads = [None] * (4 * L)
    small_parts = [None] * L
    for l in reversed(range(L)):
        sv = saved[l]
        sh1, sc1, gt1, sh2, sc2, gt2 = [mod[l, i] for i in range(6)]
        gn1, gn2 = g_norm1[l:l + 1], g_norm2[l:l + 1]
        gq, gk = g_q[l:l + 1], g_k[l:l + 1]
        dz2, dgt2, du = _gate_nt_matmul(dx, sv["f2"], gt2, W_2[l], sv["u"], f"l{l}_mlp_out_bwd")
        gw2 = _tn_matmul(sv["u"], dz2, False, True, f"l{l}_gw2")
        gw1 = _tn_matmul(sv["h2"], du, True, False, f"l{l}_gw1")
        dx, dsh2, dsc2, dgn2 = _nt_ln_bwd(du, W_1[l], sv["x1"], gn2, sc2, sh2, dx, f"l{l}_mlp_in_bwd")
        dz1, dgt1, dmixed = _gate_nt_matmul(dx, sv["f1"], gt1, W_o[l], None, f"l{l}_attn_out_bwd")
        gwo = _tn_matmul(sv["mixed"], dz1, False, False, f"l{l}_gwo")
        do = _to_heads(dmixed, H2)
        hd = sv["heads"]
        dq_sb, dk_sb, dv_sb = _sb_bwd(hd[0], hd[1], hd[2], sv["ox_sb"], do[0], f"l{l}_sb_bwd")
        dq_ca, dk_ca, dv_ca, dbias2, dgq, dgk = _ca_bwd(hd[3], hd[4], hd[5], sv["bias2"], gq, gk, do[1],
                                                         f"l{l}_ca_bwd")
        drb = _ca_bias_bwd(dbias2, f"l{l}_ca_bias_bwd")
        dproj = _from_heads(jnp.stack([dq_sb, dk_sb, dv_sb, dq_ca, dk_ca, dv_ca]).astype(_MXU))
        gwin = _tn_matmul(sv["h1"], dproj, True, False, f"l{l}_gwin")
        dx, dsh1, dsc1, dgn1 = _nt_ln_bwd(dproj, W_in[l], sv["x0"], gn1, sc1, sh1, dx, f"l{l}_proj_bwd")
        big_grads[4 * l:4 * l + 4] = [gwin, gwo, gw1, gw2]
        dmod = jnp.concatenate([dsh1, dsc1, dgt1, dsh2, dsc2, dgt2], axis=1)
        small_parts[l] = [dgn1, dgq, dgk, drb, dgn2, dmod]
    grad_x = dx[None]

    recv_a = _rs_sibling_exchange(big_grads, "rs_sibling")
    owns, parts = [], []
    for i in range(4 * L):
        own, part = _rs_chip_partial(place, big_grads[i], recv_a[i], f"rs_partial_{i}")
        owns.append(own)
        parts.append(part)
    recv_b = _rs_chip_exchange(parts, "rs_chips")
    big_out = []
    for t, (w, m, v) in enumerate([(w_in, m_w_in, v_w_in), (w_o, m_w_o, v_w_o), (w1, m_w1, v_w1), (w2, m_w2, v_w2)]):
        big_out.append(_rs_sum_adamw([owns[4 * l + t] for l in range(L)], [recv_b[4 * l + t] for l in range(L)],
                                     w, m, v, f"adamw_big_{t}"))

    packed = _pack_small(small_parts)
    gathered_small = _all_gather_small(packed, "ag_small_grads")
    small_sum = _sum_devices(gathered_small, "sum_small_grads")
    shapes = [[(1, D), (1, HEAD_DIM), (1, HEAD_DIM), (H2, N_REL), (1, D), (1, 6 * D)]] * L
    names = ["g_norm1", "g_q", "g_k", "rel_bias", "g_norm2", "b_ada"]
    small_w = {"g_norm1": (g_norm1, m_g_norm1, v_g_norm1), "g_q": (g_q, m_g_q, v_g_q), "g_k": (g_k, m_g_k, v_g_k),
               "rel_bias": (rel_bias, m_rel_bias, v_rel_bias), "g_norm2": (g_norm2, m_g_norm2, v_g_norm2),
               "b_ada": (b_ada, m_b_ada, v_b_ada)}
    packs = [_pack_small([[small_w[n][k][l] for n in names] for l in range(L)]) for k in range(3)]
    n_pad = packed.shape[1]
    as_rows = lambda a: a.reshape(n_pad // 128, 128)
    sd, sm, sv_ = _adamw(as_rows(packs[0]), as_rows(small_sum), as_rows(packs[1]), as_rows(packs[2]), "adamw_small")
    small_out = {}
    for key, flat in [("grad", small_sum), ("delta", sd), ("m", sm), ("v", sv_)]:
        per_layer = _unpack_small(flat.reshape(-1), shapes)
        for i, n in enumerate(names):
            small_out[(key, n)] = jnp.stack([per_layer[l][i].reshape(small_w[n][0].shape[1:]) for l in range(L)])

    layer_len = 2 * D + 2 * HEAD_DIM + H2 * N_REL + 6 * D
    rows = gathered_small.reshape(NDEV, n_pad)
    dmod_all = jnp.stack([rows[:, l * layer_len + layer_len - 6 * D:(l + 1) * layer_len] for l in range(L)])
    dmod_cols = lax.dynamic_slice(dmod_all, (0, 0, me * Ca), (L, NDEV, Ca))
    dmod_cols = jnp.pad(dmod_cols, ((0, 0), (0, 128 - NDEV), (0, 0)))
    c_t = jnp.pad(c_all.T, ((0, 0), (0, 128 - NDEV)))
    g_ada = _w_ada_grad(c_t, dmod_cols, "w_ada_grad")
    flat2 = lambda a: a.reshape(L * D, Ca)
    ad, am, av = _adamw(flat2(w_ada), flat2(g_ada), flat2(m_w_ada), flat2(v_w_ada), "adamw_w_ada")
    ada_out = [g_ada] + [a.reshape(L, D, Ca) for a in (ad, am, av)]

    def leaf(kind):
        k = {"grad": 0, "delta": 1, "m": 2, "v": 3}[kind]
        return [small_out[(kind, "g_norm1")], big_out[0][k], small_out[(kind, "g_q")], small_out[(kind, "g_k")],
                small_out[(kind, "rel_bias")], big_out[1][k], small_out[(kind, "g_norm2")], big_out[2][k],
                big_out[3][k], ada_out[k], small_out[(kind, "b_ada")]]

    return (loss, grad_x, *leaf("grad"), *leaf("delta"), *leaf("m"), *leaf("v"))
```

```python
import functools

import jax
import jax.numpy as jnp
from jax import lax
from jax.experimental import pallas as pl
from jax.experimental.pallas import tpu as pltpu

F32 = jnp.float32
_MXU = jnp.bfloat16

HEAD_DIM = 64
CHUNK = 64
LEFT_CHUNKS = 8
PAD = LEFT_CHUNKS * CHUNK
BAND = PAD + CHUNK
REL_CLIP = 128
N_REL = 2 * REL_CLIP + 1
EPS = 1e-6
NEG = -1e30
NDEV = 8
SB_T = 128
CA_T = 2 * CHUNK
CA_W = CA_T + PAD
SB_SKIP = -104.0
PAIR = 2 * HEAD_DIM
SB_PAIRS = 2
CA_PAIRS_FWD = 2
CA_PAIRS_BWD = 1
ROW_BLOCK = 512
SKEW_W = 767

ADAM_LR, ADAM_B1, ADAM_B2, ADAM_EPS, ADAM_WD, ADAM_STEP = 0.001, 0.9, 0.999, 1e-08, 0.01, 10

MESH = pl.DeviceIdType.MESH
VMEM_SPEC = pl.BlockSpec(memory_space=pltpu.VMEM)
SMEM_SPEC = pl.BlockSpec(memory_space=pltpu.SMEM)
ANY_SPEC = pl.BlockSpec(memory_space=pl.ANY)


def _nn(a, b):
    return lax.dot_general(a, b, (((1,), (0,)), ((), ())), preferred_element_type=F32)


def _nt(a, b):
    return lax.dot_general(a, b, (((1,), (1,)), ((), ())), preferred_element_type=F32)


def _tn(a, b):
    return lax.dot_general(a, b, (((0,), (0,)), ((), ())), preferred_element_type=F32)


def _blk(n, pref):
    return pref if n % pref == 0 else n


def _pos():
    return lax.axis_index("x"), lax.axis_index("y"), lax.axis_index("c")


def _flip(v, bit):
    return 1 - v if bit else v


def _all_gather_small(blk, name):
    R, C = blk.shape

    def body(x_ref, out_ref, send_sems, recv_sems):
        x, y, c = _pos()
        me = 4 * x + 2 * y + c

        def peer(k):
            return (_flip(x, k & 4), _flip(y, k & 2), _flip(c, k & 1))

        def copy(k, slot):
            return pltpu.make_async_remote_copy(
                src_ref=x_ref, dst_ref=out_ref.at[slot], send_sem=send_sems.at[k - 1],
                recv_sem=recv_sems.at[k - 1], device_id=peer(k), device_id_type=MESH)

        out_ref[pl.ds(me, 1), :, :] = x_ref[...].reshape(1, R, C)
        sends = [copy(k, me) for k in range(1, NDEV)]
        for cp in sends:
            cp.start()
        for k in range(1, NDEV):
            px, py, pc = peer(k)
            copy(k, 4 * px + 2 * py + pc).wait_recv()
        for cp in sends:
            cp.wait_send()

    return pl.pallas_call(
        body, name=name,
        out_shape=jax.ShapeDtypeStruct((NDEV, R, C), blk.dtype),
        in_specs=[VMEM_SPEC], out_specs=VMEM_SPEC,
        scratch_shapes=[pltpu.SemaphoreType.DMA((NDEV - 1,)), pltpu.SemaphoreType.DMA((NDEV - 1,))],
    )(blk)


def _all_gather_big(shards, name):
    n = len(shards)

    def body(*refs):
        ins, outs = refs[:n], refs[n:2 * n]
        send_sems, recv_sems, local_sems = refs[2 * n:]
        x, y, c = _pos()
        me, sibling = (x, y, c), (x, y, 1 - c)
        chips = [(1 - x, y), (x, 1 - y), (1 - x, 1 - y)]

        def copy(i, k, block, to, src=None):
            px, py, pc = block
            dst = outs[i].at[4 * px + 2 * py + pc]
            return pltpu.make_async_remote_copy(
                src_ref=dst if src is None else src, dst_ref=dst, send_sem=send_sems.at[7 * i + k],
                recv_sem=recv_sems.at[7 * i + k], device_id=to, device_id_type=MESH)

        mine, first, passed = [], [], []
        for i in range(n):
            mx, my, mc = me
            cp = pltpu.make_async_copy(ins[i], outs[i].at[4 * mx + 2 * my + mc], local_sems.at[i])
            cp.start()
            mine.append(cp)
            group = [copy(i, 0, me, sibling, src=ins[i])]
            group += [copy(i, 1 + j, me, (*chip, c), src=ins[i]) for j, chip in enumerate(chips)]
            for g in group:
                g.start()
            first += group
        for j, chip in enumerate(chips):
            for i in range(n):
                copy(i, 1 + j, (*chip, c), me).wait_recv()
                fw = copy(i, 4 + j, (*chip, c), sibling)
                fw.start()
                passed.append(fw)
        for i in range(n):
            copy(i, 0, sibling, me).wait_recv()
            for j, chip in enumerate(chips):
                copy(i, 4 + j, (*chip, 1 - c), me).wait_recv()
        for cp in first + passed:
            cp.wait_send()
        for cp in mine:
            cp.wait()

    return pl.pallas_call(
        body, name=name,
        out_shape=[jax.ShapeDtypeStruct((NDEV,) + s.shape, s.dtype) for s in shards],
        in_specs=[ANY_SPEC] * n, out_specs=[ANY_SPEC] * n,
        scratch_shapes=[pltpu.SemaphoreType.DMA((7 * n,)), pltpu.SemaphoreType.DMA((7 * n,)),
                        pltpu.SemaphoreType.DMA((n,))],
    )(*shards)


def _rs_sibling_exchange(grads, name):
    n = len(grads)

    def body(*refs):
        ins, outs = refs[:n], refs[n:2 * n]
        send_sems, recv_sems = refs[2 * n:]
        x, y, c = _pos()
        sibling = (x, y, 1 - c)
        copies = []
        for i in range(n):
            for q in range(4):
                cp = pltpu.make_async_remote_copy(
                    src_ref=ins[i].at[2 * q + (1 - c)], dst_ref=outs[i].at[q], send_sem=send_sems.at[4 * i + q],
                    recv_sem=recv_sems.at[4 * i + q], device_id=sibling, device_id_type=MESH)
                cp.start()
                copies.append(cp)
        for cp in copies:
            cp.wait()

    return pl.pallas_call(
        body, name=name,
        out_shape=[jax.ShapeDtypeStruct((4,) + g.shape[1:], g.dtype) for g in grads],
        in_specs=[ANY_SPEC] * n, out_specs=[ANY_SPEC] * n,
        scratch_shapes=[pltpu.SemaphoreType.DMA((4 * n,)), pltpu.SemaphoreType.DMA((4 * n,))],
    )(*grads)


def _rs_chip_exchange(parts, name):
    n = len(parts)

    def body(*refs):
        ins, outs = refs[:n], refs[n:2 * n]
        send_sems, recv_sems = refs[2 * n:]
        x, y, c = _pos()
        copies = []
        for i in range(n):
            for j in range(1, 4):
                cp = pltpu.make_async_remote_copy(
                    src_ref=ins[i].at[j - 1], dst_ref=outs[i].at[j - 1], send_sem=send_sems.at[3 * i + j - 1],
                    recv_sem=recv_sems.at[3 * i + j - 1], device_id=(_flip(x, j & 2), _flip(y, j & 1), c),
                    device_id_type=MESH)
                cp.start()
                copies.append(cp)
        for cp in copies:
            cp.wait()

    return pl.pallas_call(
        body, name=name,
        out_shape=[jax.ShapeDtypeStruct(p.shape, p.dtype) for p in parts],
        in_specs=[ANY_SPEC] * n, out_specs=[ANY_SPEC] * n,
        scratch_shapes=[pltpu.SemaphoreType.DMA((3 * n,)), pltpu.SemaphoreType.DMA((3 * n,))],
    )(*parts)


def _rs_chip_partial(place, grad, recv, name):
    _, R, C = grad.shape
    tr = _blk(R, 256)

    def body(place_ref, *refs):
        g_refs, r_refs = refs[:4], refs[4:8]
        own_ref, out_ref = refs[8:]
        own_ref[...] = g_refs[0][0] + r_refs[0][0]
        for j in range(1, 4):
            out_ref[j - 1] = (g_refs[j][0] + r_refs[j][0]).astype(out_ref.dtype)

    def g_map(j):
        return lambda i, p: (2 * jnp.bitwise_xor(p[0], j) + p[1], i, 0)

    def r_map(j):
        return lambda i, p: (jnp.bitwise_xor(p[0], j), i, 0)

    grid_spec = pltpu.PrefetchScalarGridSpec(
        num_scalar_prefetch=1, grid=(R // tr,),
        in_specs=[pl.BlockSpec((1, tr, C), g_map(j)) for j in range(4)]
        + [pl.BlockSpec((1, tr, C), r_map(j)) for j in range(4)],
        out_specs=[pl.BlockSpec((tr, C), lambda i, p: (i, 0)), pl.BlockSpec((3, tr, C), lambda i, p: (0, i, 0))])
    return pl.pallas_call(
        body, name=name, grid_spec=grid_spec,
        out_shape=[jax.ShapeDtypeStruct((R, C), F32), jax.ShapeDtypeStruct((3, R, C), _MXU)],
    )(place, *([grad] * 4), *([recv] * 4))


def _adamw_math(w, g, m, v):
    m = ADAM_B1 * m + (1.0 - ADAM_B1) * g
    v = ADAM_B2 * v + (1.0 - ADAM_B2) * (g * g)
    m_hat = m / (1.0 - ADAM_B1 ** ADAM_STEP)
    v_hat = v / (1.0 - ADAM_B2 ** ADAM_STEP)
    delta = -ADAM_LR * (m_hat / (jnp.sqrt(v_hat) + ADAM_EPS) + ADAM_WD * w)
    return delta, m, v


def _adamw(w, g, m, v, name):
    R, C = w.shape
    tr = _blk(R, 256)

    def body(w_ref, g_ref, m_ref, v_ref, d_ref, nm_ref, nv_ref):
        d, nm, nv = _adamw_math(w_ref[...], g_ref[...], m_ref[...], v_ref[...])
        d_ref[...] = d
        nm_ref[...] = nm
        nv_ref[...] = nv

    spec = pl.BlockSpec((tr, C), lambda i: (i, 0))
    return pl.pallas_call(
        body, name=name, grid=(R // tr,), in_specs=[spec] * 4, out_specs=[spec] * 3,
        out_shape=[jax.ShapeDtypeStruct((R, C), F32)] * 3,
    )(w, g, m, v)


def _rs_sum_adamw(owns, recvs, w, m, v, name):
    L, R, C = w.shape
    tr = _blk(R, 256)
    nr = R // tr

    def body(o0, o1, r0, r1, w_ref, m_ref, v_ref, g_ref, d_ref, nm_ref, nv_ref):
        def step(o_ref, r_ref):
            g = o_ref[...]
            for j in range(3):
                g = g + r_ref[j].astype(F32)
            d, nm, nv = _adamw_math(w_ref[0], g, m_ref[0], v_ref[0])
            g_ref[0] = g
            d_ref[0] = d
            nm_ref[0] = nm
            nv_ref[0] = nv

        pl.when(pl.program_id(0) == 0)(lambda: step(o0, r0))
        pl.when(pl.program_id(0) == 1)(lambda: step(o1, r1))

    def hold(layer):
        if layer == 0:
            return lambda l, i: i * (1 - l) + (nr - 1) * l
        return lambda l, i: i * l

    own_spec = [pl.BlockSpec((tr, C), functools.partial(lambda l, i, f: (f(l, i), 0), f=hold(k))) for k in range(2)]
    recv_spec = [pl.BlockSpec((3, tr, C), functools.partial(lambda l, i, f: (0, f(l, i), 0), f=hold(k)))
                 for k in range(2)]
    lay = pl.BlockSpec((1, tr, C), lambda l, i: (l, i, 0))
    return pl.pallas_call(
        body, name=name, grid=(L, nr),
        in_specs=own_spec + recv_spec + [lay] * 3, out_specs=[lay] * 4,
        out_shape=[jax.ShapeDtypeStruct((L, R, C), F32)] * 4,
    )(owns[0], owns[1], recvs[0], recvs[1], w, m, v)


def _silu(x):
    return x / (1.0 + jnp.exp(-x))


def _mod_partial(c_all, w_ada, b_cols, name):
    L, D, Ca = w_ada.shape

    def body(c_ref, w_ref, b_ref, o_ref):
        act = _silu(c_ref[...]).astype(_MXU)
        for l in range(L):
            o_ref[:, l * Ca:(l + 1) * Ca] = _nn(act, w_ref[l].astype(_MXU)) + b_ref[l:l + 1, :]

    return pl.pallas_call(
        body, name=name, out_shape=jax.ShapeDtypeStruct((NDEV, L * Ca), F32),
        in_specs=[VMEM_SPEC] * 3, out_specs=VMEM_SPEC,
    )(c_all, w_ada, b_cols)


def _w_ada_grad(c_t, dmod_cols, name):
    L, _, Ca = dmod_cols.shape
    D = c_t.shape[0]

    def body(c_ref, d_ref, o_ref):
        act = _silu(c_ref[...]).astype(_MXU)
        for l in range(L):
            o_ref[l] = _nn(act, d_ref[l].astype(_MXU))

    return pl.pallas_call(
        body, name=name, out_shape=jax.ShapeDtypeStruct((L, D, Ca), F32),
        in_specs=[VMEM_SPEC] * 2, out_specs=VMEM_SPEC,
    )(c_t, dmod_cols)


def _sum_devices(gathered, name):
    _, _, N = gathered.shape

    def body(x_ref, o_ref):
        acc = x_ref[0]
        for d in range(1, NDEV):
            acc = acc + x_ref[d]
        o_ref[...] = acc

    return pl.pallas_call(
        body, name=name, out_shape=jax.ShapeDtypeStruct((1, N), F32),
        in_specs=[VMEM_SPEC], out_specs=VMEM_SPEC,
    )(gathered)


def _ln_mod_matmul(x, g, sc, sh, w, name):
    S, D = x.shape
    N = w.shape[1]
    tm = _blk(S, ROW_BLOCK)

    def body(x_ref, g_ref, sc_ref, sh_ref, w_ref, o_ref, h_ref):
        xv = x_ref[...]
        r = lax.rsqrt(jnp.mean(xv * xv, axis=-1, keepdims=True) + EPS)
        hv = ((xv * r) * g_ref[...]) * (1.0 + sc_ref[...]) + sh_ref[...]
        hb = hv.astype(_MXU)
        h_ref[...] = hb
        o_ref[...] = _nn(hb, w_ref[...]).astype(o_ref.dtype)

    vec = pl.BlockSpec((1, D), lambda i: (0, 0))
    row = lambda width: pl.BlockSpec((tm, width), lambda i: (i, 0))
    return pl.pallas_call(
        body, name=name, grid=(S // tm,),
        in_specs=[row(D), vec, vec, vec, pl.BlockSpec((D, N), lambda i: (0, 0))],
        out_specs=[row(N), row(D)],
        out_shape=[jax.ShapeDtypeStruct((S, N), _MXU), jax.ShapeDtypeStruct((S, D), _MXU)],
    )(x, g, sc, sh, w)


def _matmul_res_gate(a, w, xres, gt, relu2, name):
    S, K = a.shape
    N = w.shape[1]
    tm = _blk(S, 512)

    def body(a_ref, w_ref, x_ref, gt_ref, o_ref, f_ref):
        av = a_ref[...]
        if relu2:
            af = jnp.maximum(av.astype(F32), 0.0)
            av = (af * af).astype(_MXU)
        f = _nn(av, w_ref[...])
        f_ref[...] = f.astype(f_ref.dtype)
        o_ref[...] = x_ref[...] + gt_ref[...] * f

    row = lambda width: pl.BlockSpec((tm, width), lambda i: (i, 0))
    return pl.pallas_call(
        body, name=name, grid=(S // tm,),
        in_specs=[row(K), pl.BlockSpec((K, N), lambda i: (0, 0)), row(N), pl.BlockSpec((1, N), lambda i: (0, 0))],
        out_specs=[row(N), row(N)],
        out_shape=[jax.ShapeDtypeStruct((S, N), F32), jax.ShapeDtypeStruct((S, N), _MXU)],
    )(a, w, xres, gt)


def _loss_grad(y, t, name):
    S, D = y.shape
    tm = _blk(S, 512)
    last = S // tm - 1

    def body(y_ref, t_ref, dy_ref, l_ref, acc_ref):
        i = pl.program_id(0)
        e = y_ref[...] - t_ref[...]
        dy_ref[...] = e * (1.0 / D)
        part = jnp.sum(e * e, axis=0, keepdims=True)

        @pl.when(i == 0)
        def _():
            acc_ref[...] = part

        @pl.when(i > 0)
        def _():
            acc_ref[...] += part

        @pl.when(i == last)
        def _():
            l_ref[...] = (0.5 / D) * jnp.sum(acc_ref[...], axis=1, keepdims=True)

    row = pl.BlockSpec((tm, D), lambda i: (i, 0))
    return pl.pallas_call(
        body, name=name, grid=(S // tm,), in_specs=[row, row],
        out_specs=[row, pl.BlockSpec((1, 1), lambda i: (0, 0))],
        out_shape=[jax.ShapeDtypeStruct((S, D), F32), jax.ShapeDtypeStruct((1, 1), F32)],
        scratch_shapes=[pltpu.VMEM((1, D), F32)],
    )(y, t)


def _accumulate(ref, part, first):
    @pl.when(first)
    def _():
        ref[...] = part

    @pl.when(jnp.logical_not(first))
    def _():
        ref[...] += part


def _gate_nt_matmul(dx, f, gt, w, u, name):
    S, D = dx.shape
    N = w.shape[0]
    tm = _blk(S, ROW_BLOCK if N <= D else ROW_BLOCK // 2)
    with_u = u is not None

    def body(*refs):
        if with_u:
            dx_ref, f_ref, gt_ref, w_ref, u_ref, dz_ref, dgt_ref, res_ref = refs
        else:
            dx_ref, f_ref, gt_ref, w_ref, dz_ref, dgt_ref, res_ref = refs
        dxv = dx_ref[...]
        dz = (dxv * gt_ref[...]).astype(_MXU)
        dz_ref[...] = dz
        _accumulate(dgt_ref, jnp.sum(dxv * f_ref[...].astype(F32), axis=0, keepdims=True), pl.program_id(0) == 0)
        r = _nt(dz, w_ref[...])
        if with_u:
            r = r * (2.0 * jnp.maximum(u_ref[...].astype(F32), 0.0))
        res_ref[...] = r.astype(res_ref.dtype)

    row = lambda width: pl.BlockSpec((tm, width), lambda i: (i, 0))
    in_specs = [row(D), row(D), pl.BlockSpec((1, D), lambda i: (0, 0)), pl.BlockSpec((N, D), lambda i: (0, 0))]
    args = [dx, f, gt, w]
    if with_u:
        in_specs.append(row(N))
        args.append(u)
    return pl.pallas_call(
        body, name=name, grid=(S // tm,), in_specs=in_specs,
        out_specs=[row(D), pl.BlockSpec((1, D), lambda i: (0, 0)), row(N)],
        out_shape=[jax.ShapeDtypeStruct((S, D), _MXU), jax.ShapeDtypeStruct((1, D), F32),
                   jax.ShapeDtypeStruct((S, N), _MXU)],
    )(*args)


def _tn_matmul(a, b, by_col, relu2, name):
    S, Ka = a.shape
    Nb = b.shape[1]
    ts = _blk(S, ROW_BLOCK)
    half = NDEV // 2
    if by_col:
        R, C = Ka, Nb // NDEV
        a_spec = pl.BlockSpec((ts, Ka), lambda h, k: (k, 0))
        b_spec = pl.BlockSpec((ts, half * C), lambda h, k: (k, h))
    else:
        R, C = Ka // NDEV, Nb
        a_spec = pl.BlockSpec((ts, half * R), lambda h, k: (k, h))
        b_spec = pl.BlockSpec((ts, Nb), lambda h, k: (k, 0))

    def body(a_ref, b_ref, o_ref):
        av = a_ref[...]
        if relu2:
            af = jnp.maximum(av.astype(F32), 0.0)
            av = (af * af).astype(_MXU)
        p = _tn(av, b_ref[...])
        first = pl.program_id(1) == 0
        for d in range(half):
            part = p[:, d * C:(d + 1) * C] if by_col else p[d * R:(d + 1) * R, :]
            _accumulate(o_ref.at[d], part, first)

    return pl.pallas_call(
        body, name=name, grid=(NDEV // half, S // ts), in_specs=[a_spec, b_spec],
        out_specs=pl.BlockSpec((half, R, C), lambda h, k: (h, 0, 0)),
        out_shape=jax.ShapeDtypeStruct((NDEV, R, C), F32),
    )(a, b)


def _nt_ln_bwd(dy, w, x, g, sc, sh, dxres, name):
    S, D = x.shape
    N = w.shape[1]
    tm = _blk(S, ROW_BLOCK)

    def body(dy_ref, w_ref, x_ref, g_ref, sc_ref, sh_ref, dxr_ref, dx_ref, dsh_ref, dsc_ref, dg_ref):
        dh = _nt(dy_ref[...], w_ref[...])
        xv = x_ref[...]
        r = lax.rsqrt(jnp.mean(xv * xv, axis=-1, keepdims=True) + EPS)
        xhat = xv * r
        gv = g_ref[...]
        dn = dh * (1.0 + sc_ref[...])
        dxhat = dn * gv
        dxv = r * (dxhat - xhat * jnp.mean(dxhat * xhat, axis=-1, keepdims=True))
        dx_ref[...] = dxr_ref[...] + dxv
        first = pl.program_id(0) == 0
        _accumulate(dsh_ref, jnp.sum(dh, axis=0, keepdims=True), first)
        _accumulate(dsc_ref, jnp.sum(dh * (xhat * gv), axis=0, keepdims=True), first)
        _accumulate(dg_ref, jnp.sum(dn * xhat, axis=0, keepdims=True), first)

    row = lambda width: pl.BlockSpec((tm, width), lambda i: (i, 0))
    vec = pl.BlockSpec((1, D), lambda i: (0, 0))
    return pl.pallas_call(
        body, name=name, grid=(S // tm,),
        in_specs=[row(N), pl.BlockSpec((D, N), lambda i: (0, 0)), row(D), vec, vec, vec, row(D)],
        out_specs=[row(D), vec, vec, vec],
        out_shape=[jax.ShapeDtypeStruct((S, D), F32)] + [jax.ShapeDtypeStruct((1, D), F32)] * 3,
    )(dy, w, x, g, sc, sh, dxres)


def _split3(v):
    hi = v.astype(_MXU)
    r1 = v - hi.astype(F32)
    mid = r1.astype(_MXU)
    lo = (r1 - mid.astype(F32)).astype(_MXU)
    return hi, mid, lo


def _tri_sums(v, tri2):
    T = v.shape[0]
    hi, mid, lo = _split3(v)
    s = _nn(hi, tri2) + _nn(mid, tri2) + _nn(lo, tri2)
    return s[:, :T], s[:, T:]


def _tri2(T, inclusive):
    j = lax.broadcasted_iota(jnp.int32, (T, 2 * T), 0)
    s = lax.broadcasted_iota(jnp.int32, (T, 2 * T), 1)
    keep = (j >= s) if inclusive else (j > s)
    return jnp.where((s >= T) | keep, 1.0, 0.0).astype(_MXU)


def _log_sigmoid(z):
    return jnp.minimum(z, 0.0) - jnp.log(1.0 + jnp.exp(-jnp.abs(z)))


def _sb_blocks(qbs, kblks, strict, tri2, carry):
    scale = HEAD_DIM ** -0.5
    zs = [_nt(qb, kblk) * scale for qb, kblk in zip(qbs, kblks)]
    lbs, sums = [], []
    for z in zs:
        lb = _log_sigmoid(z)
        l1 = lb - z
        if strict is not None:
            l1 = jnp.where(strict, l1, 0.0)
        lbs.append(lb)
        sums.append(_tri_sums(l1, tri2))
    amps, new_carry = [], []
    for lb, (sfx, tot), c in zip(lbs, sums, carry):
        a = jnp.exp(lb + sfx + c)
        if strict is not None:
            a = jnp.where(strict, a, 0.0)
        amps.append(a)
        new_carry.append(c + tot)
    return lbs, amps, new_carry


def _sb_alive(carry):
    top = carry[0]
    for c in carry[1:]:
        top = jnp.maximum(top, c)
    return jnp.max(top) > SB_SKIP


def _sb_fwd(q, k, v, name):
    H, S, hd = q.shape
    T = _blk(S, SB_T)
    G = _blk(H, SB_G)

    def body(q_ref, k_ref, v_ref, o_ref, ox_ref):
        qi = pl.program_id(1)
        qbs = [q_ref[g] for g in range(G)]
        row = lax.broadcasted_iota(jnp.int32, (T, T), 0)
        col = lax.broadcasted_iota(jnp.int32, (T, T), 1)
        tri2 = _tri2(T, inclusive=False)

        def pairs(kb, carry, acc, fine, strict):
            start = pl.multiple_of(kb * T, T)
            kblks = [k_ref[g, pl.ds(start, T), :] for g in range(G)]
            _, amps, carry = _sb_blocks(qbs, kblks, strict, tri2, carry)
            parts = [_split3(a) for a in amps]
            vblks = [v_ref[g, pl.ds(start, T), :] for g in range(G)]
            acc = tuple(acc[g] + _nn(parts[g][0], vblks[g]) for g in range(G))
            fine = tuple(fine[g] + _nn(parts[g][1], vblks[g]) for g in range(G))
            return tuple(carry), acc, fine

        zero = (jnp.zeros((T, hd), F32),) * G
        carry, acc, fine = pairs(qi, (jnp.zeros((T, T), F32),) * G, zero, zero, col < row)

        def cond(st):
            kb, carry, _, _ = st
            return jnp.logical_and(kb >= 0, _sb_alive(carry))

        def step(st):
            kb, carry, acc, fine = st
            carry, acc, fine = pairs(kb, carry, acc, fine, None)
            return kb - 1, carry, acc, fine

        _, _, acc, fine = lax.while_loop(cond, step, (qi - 1, carry, acc, fine))
        for g in range(G):
            o_ref[g] = acc[g].astype(o_ref.dtype)
            ox_ref[g] = acc[g] + fine[g]

    blk = pl.BlockSpec((G, T, hd), lambda h, i: (h, i, 0))
    full = pl.BlockSpec((G, S, hd), lambda h, i: (h, 0, 0))
    return pl.pallas_call(
        body, name=name, grid=(H // G, S // T), in_specs=[blk, full, full], out_specs=[blk, blk],
        out_shape=[jax.ShapeDtypeStruct((H, S, hd), _MXU), jax.ShapeDtypeStruct((H, S, hd), F32)],
    )(q, k, v)


def _sb_bwd(q, k, v, ox, do, name):
    H, S, hd = q.shape
    T = _blk(S, SB_T)
    G = _blk(H, SB_G)
    scale = HEAD_DIM ** -0.5

    def body(q_ref, k_ref, v_ref, o_ref, do_ref, dq_ref, dk_ref, dv_ref):
        qi = pl.program_id(1)

        @pl.when(qi == 0)
        def _():
            dk_ref[...] = jnp.zeros_like(dk_ref)
            dv_ref[...] = jnp.zeros_like(dv_ref)

        qbs = [q_ref[g] for g in range(G)]
        dobs = [do_ref[g] for g in range(G)]
        deltas = [jnp.sum(dobs[g].astype(F32) * o_ref[g], axis=-1, keepdims=True) for g in range(G)]
        row = lax.broadcasted_iota(jnp.int32, (T, T), 0)
        col = lax.broadcasted_iota(jnp.int32, (T, T), 1)
        tri_ex = _tri2(T, inclusive=False)
        tri_in = _tri2(T, inclusive=True)

        def pairs(kb, carry, right, dq, strict):
            start = pl.multiple_of(kb * T, T)
            rows = pl.ds(start, T)
            kblks = [k_ref[g, rows, :] for g in range(G)]
            gs = [_nt(dobs[g], v_ref[g, rows, :]) for g in range(G)]
            lbs, amps, carry = _sb_blocks(qbs, kblks, strict, tri_ex, carry)
            ags = [a * gg for a, gg in zip(amps, gs)]
            sums = [_tri_sums(ag, tri_in) for ag in ags]
            dzbs = []
            for g in range(G):
                sfx, _ = sums[g]
                left = deltas[g] - (sfx + right[g])
                beta = jnp.exp(lbs[g])
                dz = ags[g] * (1.0 - beta) - beta * left
                if strict is not None:
                    dz = jnp.where(strict, dz, 0.0)
                dzbs.append((dz * scale).astype(_MXU))
            for g in range(G):
                dk_ref[g, rows, :] += _tn(dzbs[g], qbs[g])
                dv_ref[g, rows, :] += _tn(amps[g].astype(_MXU), dobs[g])
            right = tuple(right[g] + sums[g][1] for g in range(G))
            dq = tuple(dq[g] + _nn(dzbs[g], kblks[g]) for g in range(G))
            return tuple(carry), right, dq

        zero = (jnp.zeros((T, T), F32),) * G
        carry, right, dq = pairs(qi, zero, zero, (jnp.zeros((T, hd), F32),) * G, col < row)

        def cond(st):
            kb, carry, _, _ = st
            return jnp.logical_and(kb >= 0, _sb_alive(carry))

        def step(st):
            kb, carry, right, dq = st
            carry, right, dq = pairs(kb, carry, right, dq, None)
            return kb - 1, carry, right, dq

        _, _, _, dq = lax.while_loop(cond, step, (qi - 1, carry, right, dq))
        for g in range(G):
            dq_ref[g] = dq[g]

    blk = pl.BlockSpec((G, T, hd), lambda h, i: (h, i, 0))
    full = pl.BlockSpec((G, S, hd), lambda h, i: (h, 0, 0))
    return pl.pallas_call(
        body, name=name, grid=(H // G, S // T), in_specs=[blk, full, full, blk, blk], out_specs=[blk, full, full],
        out_shape=[jax.ShapeDtypeStruct((H, S, hd), F32)] * 3,
    )(q, k, v, ox, do)


def _skew_index():
    i = lax.broadcasted_iota(jnp.int32, (CA_T, SKEW_W + 1), 0)
    m = lax.broadcasted_iota(jnp.int32, (CA_T, SKEW_W + 1), 1)
    wrapped = i + m >= SKEW_W
    row = jnp.where(wrapped, i + 1, i)
    j = jnp.where(wrapped, i + m - SKEW_W, i + m)
    a = row // CHUNK
    jj = j - a * CHUNK
    inband = (jj >= 0) & (jj < BAND) & (j < CA_W) & (row < CA_T)
    idx = jnp.clip((row - a * CHUNK) + PAD - jj, -REL_CLIP, REL_CLIP) + REL_CLIP
    return inband, idx, wrapped


def _skew(tile):
    H = tile.shape[0]
    flat = jnp.pad(tile, ((0, 0), (0, 0), (0, SKEW_W - CA_W))).reshape(H, CA_T * SKEW_W)
    return jnp.pad(flat, ((0, 0), (0, CA_T))).reshape(H, CA_T, SKEW_W + 1)


def _unskew(view):
    H = view.shape[0]
    flat = view.reshape(H, CA_T * (SKEW_W + 1))[:, :CA_T * SKEW_W]
    return flat.reshape(H, CA_T, SKEW_W)[:, :, :CA_W]


def _ca_bias(rel_bias, name):
    H = rel_bias.shape[0]
    top = rel_bias[:, N_REL - 1:]
    by_offset = jnp.concatenate(
        [jnp.broadcast_to(top, (H, PAD - REL_CLIP + 1)), jnp.flip(rel_bias[:, :N_REL - 1], axis=1),
         jnp.broadcast_to(top, (H, SKEW_W + 1 - (PAD - REL_CLIP + 1) - (N_REL - 1)))], axis=1)

    def body(t_ref, o_ref):
        inband, _, wrapped = _skew_index()
        vals = jnp.where(wrapped, t_ref[0][:, 0:1], t_ref[0])
        o_ref[0] = jnp.where(inband, vals, NEG)

    view = pl.pallas_call(
        body, name=name, grid=(H,), in_specs=[pl.BlockSpec((1, 1, SKEW_W + 1), lambda h: (h, 0, 0))],
        out_specs=pl.BlockSpec((1, CA_T, SKEW_W + 1), lambda h: (h, 0, 0)),
        out_shape=jax.ShapeDtypeStruct((H, CA_T, SKEW_W + 1), F32),
    )(by_offset.reshape(H, 1, SKEW_W + 1))
    return _unskew(view)


def _ca_bias_bwd(dbias, name):
    H = dbias.shape[0]

    def body(d_ref, o_ref):
        inband, idx, _ = _skew_index()
        d = jnp.where(inband, d_ref[0], 0.0)
        clipped = idx == N_REL - 1
        by_offset = jnp.sum(jnp.where(clipped, 0.0, d), axis=0, keepdims=True)
        top = jnp.sum(jnp.sum(jnp.where(clipped, d, 0.0), axis=0, keepdims=True), axis=1, keepdims=True)
        lane = lax.broadcasted_iota(jnp.int32, (1, SKEW_W + 1), 1)
        o_ref[0] = jnp.where(lane == 0, top, by_offset)

    out = pl.pallas_call(
        body, name=name, grid=(H,), in_specs=[pl.BlockSpec((1, CA_T, SKEW_W + 1), lambda h: (h, 0, 0))],
        out_specs=pl.BlockSpec((1, 1, SKEW_W + 1), lambda h: (h, 0, 0)),
        out_shape=jax.ShapeDtypeStruct((H, 1, SKEW_W + 1), F32),
    )(_skew(dbias))[:, 0]
    first = PAD - REL_CLIP + 1
    return jnp.concatenate([jnp.flip(out[:, first:first + N_REL - 1], axis=1), out[:, 0:1]], axis=1)


def _head_norm(t, g):
    tf = t.astype(F32)
    r = lax.rsqrt(jnp.mean(tf * tf, axis=-1, keepdims=True) + EPS)
    hat = tf * r
    return hat * g, hat, r


def _head_norm_bwd(dn, hat, r, g):
    dhat = dn * g
    return r * (dhat - hat * jnp.mean(dhat * hat, axis=-1, keepdims=True))


def _ca_fill(g, k_ref, v_ref, gk_ref, kn_ref, vp_ref):
    S, hd = k_ref.shape[1:]
    kn, _, _ = _head_norm(k_ref[g], gk_ref[...])
    kn_ref[g, 0:PAD, :] = jnp.zeros((PAD, hd), kn_ref.dtype)
    vp_ref[g, 0:PAD, :] = jnp.zeros((PAD, hd), vp_ref.dtype)
    kn_ref[g, PAD:PAD + S, :] = kn.astype(kn_ref.dtype)
    vp_ref[g, PAD:PAD + S, :] = v_ref[g]


def _ca_scores(g, q_ref, b2_ref, gq_ref, kn_ref, qi):
    qn, qhat, r = _head_norm(q_ref[g], gq_ref[...])
    qn = (qn * HEAD_DIM ** -0.5).astype(_MXU)
    start = pl.multiple_of(qi * CA_T, CA_T)
    s = _nt(qn, kn_ref[g, pl.ds(start, CA_W), :]) + b2_ref[g]
    key_pos = qi * CA_T - PAD + lax.broadcasted_iota(jnp.int32, (CA_T, CA_W), 1)
    return jnp.where(key_pos >= 0, s, NEG), qn, qhat, r


def _softmax(s):
    e = jnp.exp(s - jnp.max(s, axis=-1, keepdims=True))
    return e * (1.0 / jnp.sum(e, axis=-1, keepdims=True))


def _ca_fwd(q, k, v, bias2, gq, gk, name):
    H, S, hd = q.shape
    G = _blk(H, CA_G_FWD)

    def body(q_ref, k_ref, v_ref, b2_ref, gq_ref, gk_ref, o_ref, kn_ref, vp_ref):
        qi = pl.program_id(1)

        @pl.when(qi == 0)
        def _():
            for g in range(G):
                _ca_fill(g, k_ref, v_ref, gk_ref, kn_ref, vp_ref)

        band = pl.ds(pl.multiple_of(qi * CA_T, CA_T), CA_W)
        scores = [_ca_scores(g, q_ref, b2_ref, gq_ref, kn_ref, qi)[0] for g in range(G)]
        probs = [_softmax(s).astype(_MXU) for s in scores]
        for g in range(G):
            o_ref[g] = _nn(probs[g], vp_ref[g, band, :]).astype(o_ref.dtype)

    blk = pl.BlockSpec((G, CA_T, hd), lambda h, i: (h, i, 0))
    full = pl.BlockSpec((G, S, hd), lambda h, i: (h, 0, 0))
    vec = pl.BlockSpec((1, hd), lambda h, i: (0, 0))
    return pl.pallas_call(
        body, name=name, grid=(H // G, S // CA_T),
        in_specs=[blk, full, full, pl.BlockSpec((G, CA_T, CA_W), lambda h, i: (h, 0, 0)), vec, vec],
        out_specs=blk, out_shape=jax.ShapeDtypeStruct((H, S, hd), _MXU),
        scratch_shapes=[pltpu.VMEM((G, PAD + S, hd), _MXU), pltpu.VMEM((G, PAD + S, hd), _MXU)],
    )(q, k, v, bias2, gq, gk)


def _ca_bwd(q, k, v, bias2, gq, gk, do, name):
    H, S, hd = q.shape
    G = _blk(H, CA_G_BWD)
    scale = HEAD_DIM ** -0.5
    last = S // CA_T - 1

    def body(q_ref, k_ref, v_ref, b2_ref, gq_ref, gk_ref, do_ref,
             dq_ref, dk_ref, dv_ref, db_ref, dgq_ref, dgk_ref, kn_ref, vp_ref, dkn_ref, dvp_ref):
        h, qi = pl.program_id(0), pl.program_id(1)

        @pl.when(qi == 0)
        def _():
            for g in range(G):
                _ca_fill(g, k_ref, v_ref, gk_ref, kn_ref, vp_ref)
            dkn_ref[...] = jnp.zeros_like(dkn_ref)
            dvp_ref[...] = jnp.zeros_like(dvp_ref)
            db_ref[...] = jnp.zeros_like(db_ref)

        @pl.when(jnp.logical_and(h == 0, qi == 0))
        def _():
            dgq_ref[...] = jnp.zeros_like(dgq_ref)
            dgk_ref[...] = jnp.zeros_like(dgk_ref)

        band = pl.ds(pl.multiple_of(qi * CA_T, CA_T), CA_W)
        heads = [_ca_scores(g, q_ref, b2_ref, gq_ref, kn_ref, qi) for g in range(G)]
        dobs = [do_ref[g] for g in range(G)]
        dps = [_nt(dobs[g], vp_ref[g, band, :]) for g in range(G)]
        probs, dsbs = [], []
        for g in range(G):
            p = _softmax(heads[g][0])
            ds = p * (dps[g] - jnp.sum(p * dps[g], axis=-1, keepdims=True))
            db_ref[g] += ds
            probs.append(p.astype(_MXU))
            dsbs.append(ds.astype(_MXU))
        dqns = [_nn(dsbs[g], kn_ref[g, band, :]) * scale for g in range(G)]
        for g in range(G):
            dkn_ref[g, band, :] += _tn(dsbs[g], heads[g][1])
            dvp_ref[g, band, :] += _tn(probs[g], dobs[g])
        dgq = jnp.zeros((1, hd), F32)
        for g in range(G):
            _, _, qhat, r = heads[g]
            dgq = dgq + jnp.sum(dqns[g] * qhat, axis=0, keepdims=True)
            dq_ref[g] = _head_norm_bwd(dqns[g], qhat, r, gq_ref[...])
        dgq_ref[...] += dgq

        @pl.when(qi == last)
        def _():
            for g in range(G):
                _, khat, rk = _head_norm(k_ref[g], gk_ref[...])
                dkn = dkn_ref[g, PAD:PAD + S, :]
                dgk_ref[...] += jnp.sum(dkn * khat, axis=0, keepdims=True)
                dk_ref[g] = _head_norm_bwd(dkn, khat, rk, gk_ref[...])
                dv_ref[g] = dvp_ref[g, PAD:PAD + S, :]

    blk = pl.BlockSpec((G, CA_T, hd), lambda h, i: (h, i, 0))
    full = pl.BlockSpec((G, S, hd), lambda h, i: (h, 0, 0))
    vec = pl.BlockSpec((1, hd), lambda h, i: (0, 0))
    tile = pl.BlockSpec((G, CA_T, CA_W), lambda h, i: (h, 0, 0))
    return pl.pallas_call(
        body, name=name, grid=(H // G, S // CA_T),
        in_specs=[blk, full, full, tile, vec, vec, blk],
        out_specs=[blk, full, full, tile, vec, vec],
        out_shape=[jax.ShapeDtypeStruct((H, S, hd), F32)] * 3
        + [jax.ShapeDtypeStruct((H, CA_T, CA_W), F32), jax.ShapeDtypeStruct((1, hd), F32),
           jax.ShapeDtypeStruct((1, hd), F32)],
        scratch_shapes=[pltpu.VMEM((G, PAD + S, hd), _MXU), pltpu.VMEM((G, PAD + S, hd), _MXU),
                        pltpu.VMEM((G, PAD + S, hd), F32), pltpu.VMEM((G, PAD + S, hd), F32)],
    )(q, k, v, bias2, gq, gk, do)


def _low_lanes(rows):
    return lax.broadcasted_iota(jnp.int32, (rows, PAIR), 1) < HEAD_DIM


def _one_head(t2, low, first):
    tf = t2.astype(F32)
    return (jnp.where(low, tf, 0.0) if first else jnp.where(low, 0.0, tf)).astype(_MXU)


def _sb_fwd(proj, name):
    S, W = proj.shape
    half = W // 6
    npair = half // PAIR
    T = _blk(S, SB_T)
    GP = _blk(npair, SB_PAIRS)
    GW = GP * PAIR
    nb = npair // GP

    def body(q_ref, k_ref, v_ref, o_ref, ox_ref):
        qi = pl.program_id(1)
        low = _low_lanes(T)
        qbs = []
        for j in range(GP):
            q2 = q_ref[:, j * PAIR:(j + 1) * PAIR]
            qbs += [_one_head(q2, low, True), _one_head(q2, low, False)]
        row = lax.broadcasted_iota(jnp.int32, (T, T), 0)
        col = lax.broadcasted_iota(jnp.int32, (T, T), 1)
        tri2 = _tri2(T, inclusive=False)

        def pairs(kb, carry, acc, fine, strict):
            rows = pl.ds(pl.multiple_of(kb * T, T), T)
            k2 = [k_ref[rows, j * PAIR:(j + 1) * PAIR] for j in range(GP)]
            v2 = [v_ref[rows, j * PAIR:(j + 1) * PAIR] for j in range(GP)]
            _, amps, carry = _sb_blocks(qbs, [k2[h // 2] for h in range(2 * GP)], strict, tri2, carry)
            parts = [_split3(a) for a in amps]
            hi = [_nn(parts[h][0], v2[h // 2]) for h in range(2 * GP)]
            mid = [_nn(parts[h][1], v2[h // 2]) for h in range(2 * GP)]
            acc = tuple(acc[j] + jnp.where(low, hi[2 * j], hi[2 * j + 1]) for j in range(GP))
            fine = tuple(fine[j] + jnp.where(low, mid[2 * j], mid[2 * j + 1]) for j in range(GP))
            return tuple(carry), acc, fine

        zero = (jnp.zeros((T, PAIR), F32),) * GP
        carry, acc, fine = pairs(qi, (jnp.zeros((T, T), F32),) * (2 * GP), zero, zero, col < row)

        def cond(st):
            kb, carry, _, _ = st
            return jnp.logical_and(kb >= 0, _sb_alive(carry))

        def step(st):
            kb, carry, acc, fine = st
            carry, acc, fine = pairs(kb, carry, acc, fine, None)
            return kb - 1, carry, acc, fine

        _, _, acc, fine = lax.while_loop(cond, step, (qi - 1, carry, acc, fine))
        for j in range(GP):
            o_ref[:, j * PAIR:(j + 1) * PAIR] = acc[j].astype(o_ref.dtype)
            ox_ref[:, j * PAIR:(j + 1) * PAIR] = acc[j] + fine[j]

    blk = pl.BlockSpec((T, GW), lambda p, i: (i, p))
    return pl.pallas_call(
        body, name=name, grid=(nb, S // T),
        in_specs=[blk, pl.BlockSpec((S, GW), lambda p, i: (0, nb + p)), pl.BlockSpec((S, GW), lambda p, i: (0, 2 * nb + p))],
        out_specs=[blk, blk],
        out_shape=[jax.ShapeDtypeStruct((S, half), _MXU), jax.ShapeDtypeStruct((S, half), F32)],
    )(proj, proj, proj)


def _sb_bwd(proj, ox, dmixed, name):
    S, W = proj.shape
    half = W // 6
    npair = half // PAIR
    T = _blk(S, SB_T)
    GP = _blk(npair, SB_PAIRS)
    GW = GP * PAIR
    nb = npair // GP
    last = S // T - 1
    scale = HEAD_DIM ** -0.5

    def body(q_ref, k_ref, v_ref, ox_ref, do_ref, dq_ref, dk_ref, dv_ref, dka_ref, dva_ref):
        qi = pl.program_id(1)

        @pl.when(qi == 0)
        def _():
            dka_ref[...] = jnp.zeros_like(dka_ref)
            dva_ref[...] = jnp.zeros_like(dva_ref)

        low = _low_lanes(T)
        q2, do2, qbs, dobs, deltas = [], [], [], [], []
        for j in range(GP):
            cols = slice(j * PAIR, (j + 1) * PAIR)
            q2.append(q_ref[:, cols])
            do2.append(do_ref[:, cols])
            qbs += [_one_head(q2[j], low, True), _one_head(q2[j], low, False)]
            dobs += [_one_head(do2[j], low, True), _one_head(do2[j], low, False)]
            for e in range(2):
                deltas.append(jnp.sum(dobs[2 * j + e].astype(F32) * ox_ref[:, cols], axis=-1, keepdims=True))
        row = lax.broadcasted_iota(jnp.int32, (T, T), 0)
        col = lax.broadcasted_iota(jnp.int32, (T, T), 1)
        tri_ex = _tri2(T, inclusive=False)
        tri_in = _tri2(T, inclusive=True)

        def pairs(kb, carry, right, dq, strict):
            rows = pl.ds(pl.multiple_of(kb * T, T), T)
            k2 = [k_ref[rows, j * PAIR:(j + 1) * PAIR] for j in range(GP)]
            v2 = [v_ref[rows, j * PAIR:(j + 1) * PAIR] for j in range(GP)]
            nh = 2 * GP
            gs = [_nt(dobs[h], v2[h // 2]) for h in range(nh)]
            lbs, amps, carry = _sb_blocks(qbs, [k2[h // 2] for h in range(nh)], strict, tri_ex, carry)
            ags = [a * gg for a, gg in zip(amps, gs)]
            sums = [_tri_sums(ag, tri_in) for ag in ags]
            dzbs = []
            for h in range(nh):
                left = deltas[h] - (sums[h][0] + right[h])
                beta = jnp.exp(lbs[h])
                dz = ags[h] * (1.0 - beta) - beta * left
                if strict is not None:
                    dz = jnp.where(strict, dz, 0.0)
                dzbs.append((dz * scale).astype(_MXU))
            abs_ = [a.astype(_MXU) for a in amps]
            dks = [_tn(dzbs[h], q2[h // 2]) for h in range(nh)]
            dvs = [_tn(abs_[h], do2[h // 2]) for h in range(nh)]
            dqs = [_nn(dzbs[h], k2[h // 2]) for h in range(nh)]
            for j in range(GP):
                cols = slice(j * PAIR, (j + 1) * PAIR)
                dka_ref[rows, cols] += jnp.where(low, dks[2 * j], dks[2 * j + 1])
                dva_ref[rows, cols] += jnp.where(low, dvs[2 * j], dvs[2 * j + 1])
            right = tuple(right[h] + sums[h][1] for h in range(nh))
            dq = tuple(dq[j] + jnp.where(low, dqs[2 * j], dqs[2 * j + 1]) for j in range(GP))
            return tuple(carry), right, dq

        zero = (jnp.zeros((T, T), F32),) * (2 * GP)
        carry, right, dq = pairs(qi, zero, zero, (jnp.zeros((T, PAIR), F32),) * GP, col < row)

        def cond(st):
            kb, carry, _, _ = st
            return jnp.logical_and(kb >= 0, _sb_alive(carry))

        def step(st):
            kb, carry, right, dq = st
            carry, right, dq = pairs(kb, carry, right, dq, None)
            return kb - 1, carry, right, dq

        _, _, _, dq = lax.while_loop(cond, step, (qi - 1, carry, right, dq))
        for j in range(GP):
            dq_ref[:, j * PAIR:(j + 1) * PAIR] = dq[j].astype(dq_ref.dtype)

        @pl.when(qi == last)
        def _():
            dk_ref[...] = dka_ref[...].astype(dk_ref.dtype)
            dv_ref[...] = dva_ref[...].astype(dv_ref.dtype)

    blk = pl.BlockSpec((T, GW), lambda p, i: (i, p))
    full = pl.BlockSpec((S, GW), lambda p, i: (0, p))
    return pl.pallas_call(
        body, name=name, grid=(nb, S // T),
        in_specs=[blk, pl.BlockSpec((S, GW), lambda p, i: (0, nb + p)), pl.BlockSpec((S, GW), lambda p, i: (0, 2 * nb + p)),
                  blk, blk],
        out_specs=[blk, full, full],
        out_shape=[jax.ShapeDtypeStruct((S, half), _MXU)] * 3,
        scratch_shapes=[pltpu.VMEM((S, GW), F32), pltpu.VMEM((S, GW), F32)],
    )(proj, proj, proj, ox, dmixed)


def _pair_norm(t2, g2, low):
    tf = t2.astype(F32)
    sq = tf * tf
    both = jnp.sum(sq, axis=-1, keepdims=True)
    first = jnp.sum(jnp.where(low, sq, 0.0), axis=-1, keepdims=True)
    r = jnp.where(low, lax.rsqrt(first * (1.0 / HEAD_DIM) + EPS), lax.rsqrt((both - first) * (1.0 / HEAD_DIM) + EPS))
    hat = tf * r
    return hat * g2, hat, r


def _pair_norm_bwd(dn, hat, r, g2, low):
    dhat = dn * g2
    prod = dhat * hat
    both = jnp.sum(prod, axis=-1, keepdims=True)
    first = jnp.sum(jnp.where(low, prod, 0.0), axis=-1, keepdims=True)
    mean = jnp.where(low, first, both - first) * (1.0 / HEAD_DIM)
    return r * (dhat - hat * mean)


def _ca_fill(j, k_ref, v_ref, gk_ref, kn_ref, vp_ref):
    S = k_ref.shape[0]
    cols = slice(j * PAIR, (j + 1) * PAIR)
    kn, _, _ = _pair_norm(k_ref[:, cols], gk_ref[...], _low_lanes(S))
    kn_ref[j, 0:PAD, :] = jnp.zeros((PAD, PAIR), kn_ref.dtype)
    vp_ref[j, 0:PAD, :] = jnp.zeros((PAD, PAIR), vp_ref.dtype)
    kn_ref[j, PAD:PAD + S, :] = kn.astype(kn_ref.dtype)
    vp_ref[j, PAD:PAD + S, :] = v_ref[:, cols]


def _ca_scores(j, q_ref, b2_ref, gq_ref, kn_ref, qi, low):
    qn, qhat, r = _pair_norm(q_ref[:, j * PAIR:(j + 1) * PAIR], gq_ref[...], low)
    qn = qn * HEAD_DIM ** -0.5
    band = pl.ds(pl.multiple_of(qi * CA_T, CA_T), CA_W)
    key_pos = qi * CA_T - PAD + lax.broadcasted_iota(jnp.int32, (CA_T, CA_W), 1)
    scores = []
    for e in range(2):
        s = _nt(_one_head(qn, low, e == 0), kn_ref[j, band, :]) + b2_ref[2 * j + e]
        scores.append(jnp.where(key_pos >= 0, s, NEG))
    return scores, qn.astype(_MXU), qhat, r


def _softmax(s):
    e = jnp.exp(s - jnp.max(s, axis=-1, keepdims=True))
    return e * (1.0 / jnp.sum(e, axis=-1, keepdims=True))


def _ca_fwd(proj, bias2, gq2, gk2, name):
    S, W = proj.shape
    half = W // 6
    npair = half // PAIR
    GP = _blk(npair, CA_PAIRS_FWD)
    GW = GP * PAIR
    nb = npair // GP

    def body(q_ref, k_ref, v_ref, b2_ref, gq_ref, gk_ref, o_ref, kn_ref, vp_ref):
        qi = pl.program_id(1)

        @pl.when(qi == 0)
        def _():
            for j in range(GP):
                _ca_fill(j, k_ref, v_ref, gk_ref, kn_ref, vp_ref)

        low = _low_lanes(CA_T)
        band = pl.ds(pl.multiple_of(qi * CA_T, CA_T), CA_W)
        scores = [_ca_scores(j, q_ref, b2_ref, gq_ref, kn_ref, qi, low)[0] for j in range(GP)]
        probs = [[_softmax(s).astype(_MXU) for s in pair] for pair in scores]
        for j in range(GP):
            outs = [_nn(probs[j][e], vp_ref[j, band, :]) for e in range(2)]
            o_ref[:, j * PAIR:(j + 1) * PAIR] = jnp.where(low, outs[0], outs[1]).astype(o_ref.dtype)

    vec = pl.BlockSpec((1, PAIR), lambda p, i: (0, 0))
    return pl.pallas_call(
        body, name=name, grid=(nb, S // CA_T),
        in_specs=[pl.BlockSpec((CA_T, GW), lambda p, i: (i, 3 * nb + p)),
                  pl.BlockSpec((S, GW), lambda p, i: (0, 4 * nb + p)), pl.BlockSpec((S, GW), lambda p, i: (0, 5 * nb + p)),
                  pl.BlockSpec((2 * GP, CA_T, CA_W), lambda p, i: (p, 0, 0)), vec, vec],
        out_specs=pl.BlockSpec((CA_T, GW), lambda p, i: (i, p)),
        out_shape=jax.ShapeDtypeStruct((S, half), _MXU),
        scratch_shapes=[pltpu.VMEM((GP, PAD + S, PAIR), _MXU), pltpu.VMEM((GP, PAD + S, PAIR), _MXU)],
    )(proj, proj, proj, bias2, gq2, gk2)


def _ca_bwd(proj, bias2, gq2, gk2, dmixed, name):
    S, W = proj.shape
    half = W // 6
    npair = half // PAIR
    GP = _blk(npair, CA_PAIRS_BWD)
    GW = GP * PAIR
    nb = npair // GP
    scale = HEAD_DIM ** -0.5
    last = S // CA_T - 1

    def body(q_ref, k_ref, v_ref, b2_ref, gq_ref, gk_ref, do_ref,
             dq_ref, dk_ref, dv_ref, db_ref, dgq_ref, dgk_ref, kn_ref, vp_ref, dkn_ref, dvp_ref):
        p_id, qi = pl.program_id(0), pl.program_id(1)

        @pl.when(qi == 0)
        def _():
            for j in range(GP):
                _ca_fill(j, k_ref, v_ref, gk_ref, kn_ref, vp_ref)
            dkn_ref[...] = jnp.zeros_like(dkn_ref)
            dvp_ref[...] = jnp.zeros_like(dvp_ref)
            db_ref[...] = jnp.zeros_like(db_ref)

        @pl.when(jnp.logical_and(p_id == 0, qi == 0))
        def _():
            dgq_ref[...] = jnp.zeros_like(dgq_ref)
            dgk_ref[...] = jnp.zeros_like(dgk_ref)

        low = _low_lanes(CA_T)
        low_w = _low_lanes(CA_W)
        band = pl.ds(pl.multiple_of(qi * CA_T, CA_T), CA_W)
        pairs = [_ca_scores(j, q_ref, b2_ref, gq_ref, kn_ref, qi, low) for j in range(GP)]
        do2 = [do_ref[:, j * PAIR:(j + 1) * PAIR] for j in range(GP)]
        dps = [[_nt(_one_head(do2[j], low, e == 0), vp_ref[j, band, :]) for e in range(2)] for j in range(GP)]
        probs, dsbs = [], []
        for j in range(GP):
            pj, dj = [], []
            for e in range(2):
                p = _softmax(pairs[j][0][e])
                ds = p * (dps[j][e] - jnp.sum(p * dps[j][e], axis=-1, keepdims=True))
                db_ref[2 * j + e] += ds
                pj.append(p.astype(_MXU))
                dj.append(ds.astype(_MXU))
            probs.append(pj)
            dsbs.append(dj)
        dgq = jnp.zeros((1, PAIR), F32)
        for j in range(GP):
            _, qn, qhat, r = pairs[j]
            dq_h = [_nn(dsbs[j][e], kn_ref[j, band, :]) for e in range(2)]
            dk_h = [_tn(dsbs[j][e], qn) for e in range(2)]
            dv_h = [_tn(probs[j][e], do2[j]) for e in range(2)]
            dkn_ref[j, band, :] += jnp.where(low_w, dk_h[0], dk_h[1])
            dvp_ref[j, band, :] += jnp.where(low_w, dv_h[0], dv_h[1])
            dqn = jnp.where(low, dq_h[0], dq_h[1]) * scale
            dgq = dgq + jnp.sum(dqn * qhat, axis=0, keepdims=True)
            dq_ref[:, j * PAIR:(j + 1) * PAIR] = _pair_norm_bwd(dqn, qhat, r, gq_ref[...], low).astype(dq_ref.dtype)
        dgq_ref[...] += dgq

        @pl.when(qi == last)
        def _():
            low_s = _low_lanes(S)
            for j in range(GP):
                cols = slice(j * PAIR, (j + 1) * PAIR)
                _, khat, rk = _pair_norm(k_ref[:, cols], gk_ref[...], low_s)
                dkn = dkn_ref[j, PAD:PAD + S, :]
                dgk_ref[...] += jnp.sum(dkn * khat, axis=0, keepdims=True)
                dk_ref[:, cols] = _pair_norm_bwd(dkn, khat, rk, gk_ref[...], low_s).astype(dk_ref.dtype)
                dv_ref[:, cols] = dvp_ref[j, PAD:PAD + S, :].astype(dv_ref.dtype)

    vec = pl.BlockSpec((1, PAIR), lambda p, i: (0, 0))
    tile = pl.BlockSpec((2 * GP, CA_T, CA_W), lambda p, i: (p, 0, 0))
    full = pl.BlockSpec((S, GW), lambda p, i: (0, p))
    return pl.pallas_call(
        body, name=name, grid=(nb, S // CA_T),
        in_specs=[pl.BlockSpec((CA_T, GW), lambda p, i: (i, 3 * nb + p)),
                  pl.BlockSpec((S, GW), lambda p, i: (0, 4 * nb + p)), pl.BlockSpec((S, GW), lambda p, i: (0, 5 * nb + p)),
                  tile, vec, vec, pl.BlockSpec((CA_T, GW), lambda p, i: (i, nb + p))],
        out_specs=[pl.BlockSpec((CA_T, GW), lambda p, i: (i, p)), full, full, tile, vec, vec],
        out_shape=[jax.ShapeDtypeStruct((S, half), _MXU)] * 3
        + [jax.ShapeDtypeStruct(bias2.shape, F32), jax.ShapeDtypeStruct((1, PAIR), F32),
           jax.ShapeDtypeStruct((1, PAIR), F32)],
        scratch_shapes=[pltpu.VMEM((GP, PAD + S, PAIR), _MXU), pltpu.VMEM((GP, PAD + S, PAIR), _MXU),
                        pltpu.VMEM((GP, PAD + S, PAIR), F32), pltpu.VMEM((GP, PAD + S, PAIR), F32)],
    )(proj, proj, proj, bias2, gq2, gk2, dmixed)


def _to_heads(t, H):
    S, W = t.shape
    return t.reshape(S, W // (H * HEAD_DIM), H, HEAD_DIM).transpose(1, 2, 0, 3)


def _from_heads(t):
    n, H, S, hd = t.shape
    return t.transpose(2, 0, 1, 3).reshape(S, n * H * hd)


def _pack_small(parts):
    flat = jnp.concatenate([p.reshape(-1) for layer in parts for p in layer])
    n = flat.shape[0]
    n_pad = -(-n // 1024) * 1024
    return jnp.pad(flat, (0, n_pad - n)).reshape(1, n_pad)


def _unpack_small(flat, shapes):
    out, off = [], 0
    for layer in shapes:
        cur = []
        for shp in layer:
            size = 1
            for s in shp:
                size *= s
            cur.append(flat[off:off + size].reshape(shp))
            off += size
        out.append(cur)
    return out


def kernel(x, c, g_norm1, w_in, g_q, g_k, rel_bias, w_o, g_norm2, w1, w2, w_ada, b_ada, loss_target, m_g_norm1, m_w_in, m_g_q, m_g_k, m_rel_bias, m_w_o, m_g_norm2, m_w1, m_w2, m_w_ada, m_b_ada, v_g_norm1, v_w_in, v_g_q, v_g_k, v_rel_bias, v_w_o, v_g_norm2, v_w1, v_w2, v_w_ada, v_b_ada):
    L = w_in.shape[0]
    S, D = x.shape[1:]
    H2 = D // HEAD_DIM // 2
    Ca = w_ada.shape[2]
    xi, yi, ci = _pos()
    me = 4 * xi + 2 * yi + ci
    place = jnp.stack([2 * xi + yi, ci]).astype(jnp.int32)

    c_all = _all_gather_small(c, "ag_c").reshape(NDEV, D)
    b_cols = lax.dynamic_slice(b_ada, (0, me * Ca), (L, Ca))
    mod_part = _mod_partial(c_all, w_ada, b_cols, "mod_partial")
    mod_all = _all_gather_small(mod_part, "ag_mod")
    mod = lax.dynamic_index_in_dim(mod_all, me, axis=1, keepdims=False)
    mod = mod.reshape(NDEV, L, Ca).transpose(1, 0, 2).reshape(L, 6, 1, D)

    shards = []
    for l in range(L):
        shards += [w_in[l].astype(_MXU), w_o[l].astype(_MXU), w1[l].astype(_MXU), w2[l].astype(_MXU)]
    gathered = _all_gather_big(shards, "ag_weights")
    W_in = [gathered[4 * l + 0].transpose(1, 0, 2).reshape(D, 3 * D) for l in range(L)]
    W_o = [gathered[4 * l + 1].reshape(D, D) for l in range(L)]
    W_1 = [gathered[4 * l + 2].transpose(1, 0, 2).reshape(D, 4 * D) for l in range(L)]
    W_2 = [gathered[4 * l + 3].reshape(4 * D, D) for l in range(L)]

    xs = [x[0]]
    saved = []
    for l in range(L):
        sh1, sc1, gt1, sh2, sc2, gt2 = [mod[l, i] for i in range(6)]
        gn1, gn2 = g_norm1[l:l + 1], g_norm2[l:l + 1]
        gq, gk = g_q[l:l + 1], g_k[l:l + 1]
        proj, h1 = _ln_mod_matmul(xs[-1], gn1, sc1, sh1, W_in[l], f"l{l}_proj")
        o_sb, ox_sb = _sb_fwd(proj, f"l{l}_sb_fwd")
        bias2 = _ca_bias(rel_bias[l], f"l{l}_ca_bias")
        o_ca = _ca_fwd(proj, bias2, jnp.tile(gq, (1, 2)), jnp.tile(gk, (1, 2)), f"l{l}_ca_fwd")
        mixed = jnp.concatenate([o_sb, o_ca], axis=1)
        x1, f1 = _matmul_res_gate(mixed, W_o[l], xs[-1], gt1, False, f"l{l}_attn_out")
        u, h2 = _ln_mod_matmul(x1, gn2, sc2, sh2, W_1[l], f"l{l}_mlp_in")
        x2, f2 = _matmul_res_gate(u, W_2[l], x1, gt2, True, f"l{l}_mlp_out")
        saved.append(dict(x0=xs[-1], h1=h1, proj=proj, ox_sb=ox_sb, bias2=bias2, mixed=mixed, f1=f1, x1=x1,
                          h2=h2, u=u, f2=f2))
        xs.append(x2)

    dx, loss_part = _loss_grad(xs[-1], loss_target[0], "loss")

    big_grads = [None] * (4 * L)
    small_parts = [None] * L
    for l in reversed(range(L)):
        sv = saved[l]
        sh1, sc1, gt1, sh2, sc2, gt2 = [mod[l, i] for i in range(6)]
        gn1, gn2 = g_norm1[l:l + 1], g_norm2[l:l + 1]
        gq, gk = g_q[l:l + 1], g_k[l:l + 1]
        dz2, dgt2, du = _gate_nt_matmul(dx, sv["f2"], gt2, W_2[l], sv["u"], f"l{l}_mlp_out_bwd")
        gw2 = _tn_matmul(sv["u"], dz2, False, True, f"l{l}_gw2")
        gw1 = _tn_matmul(sv["h2"], du, True, False, f"l{l}_gw1")
        dx, dsh2, dsc2, dgn2 = _nt_ln_bwd(du, W_1[l], sv["x1"], gn2, sc2, sh2, dx, f"l{l}_mlp_in_bwd")
        dz1, dgt1, dmixed = _gate_nt_matmul(dx, sv["f1"], gt1, W_o[l], None, f"l{l}_attn_out_bwd")
        gwo = _tn_matmul(sv["mixed"], dz1, False, False, f"l{l}_gwo")
        dq_sb, dk_sb, dv_sb = _sb_bwd(sv["proj"], sv["ox_sb"], dmixed, f"l{l}_sb_bwd")
        dq_ca, dk_ca, dv_ca, dbias2, dgq2, dgk2 = _ca_bwd(sv["proj"], sv["bias2"], jnp.tile(gq, (1, 2)),
                                                           jnp.tile(gk, (1, 2)), dmixed, f"l{l}_ca_bwd")
        dgq = dgq2[:, :HEAD_DIM] + dgq2[:, HEAD_DIM:]
        dgk = dgk2[:, :HEAD_DIM] + dgk2[:, HEAD_DIM:]
        drb = _ca_bias_bwd(dbias2, f"l{l}_ca_bias_bwd")
        dproj = jnp.concatenate([dq_sb, dk_sb, dv_sb, dq_ca, dk_ca, dv_ca], axis=1)
        gwin = _tn_matmul(sv["h1"], dproj, True, False, f"l{l}_gwin")
        dx, dsh1, dsc1, dgn1 = _nt_ln_bwd(dproj, W_in[l], sv["x0"], gn1, sc1, sh1, dx, f"l{l}_proj_bwd")
        big_grads[4 * l:4 * l + 4] = [gwin, gwo, gw1, gw2]
        dmod = jnp.concatenate([dsh1, dsc1, dgt1, dsh2, dsc2, dgt2], axis=1)
        small_parts[l] = [dgn1, dgq, dgk, drb, dgn2, dmod]
    grad_x = dx[None]

    recv_a = _rs_sibling_exchange(big_grads, "rs_sibling")
    owns, parts = [], []
    for i in range(4 * L):
        own, part = _rs_chip_partial(place, big_grads[i], recv_a[i], f"rs_partial_{i}")
        owns.append(own)
        parts.append(part)
    recv_b = _rs_chip_exchange(parts, "rs_chips")
    big_out = []
    for t, (w, m, v) in enumerate([(w_in, m_w_in, v_w_in), (w_o, m_w_o, v_w_o), (w1, m_w1, v_w1), (w2, m_w2, v_w2)]):
        big_out.append(_rs_sum_adamw([owns[4 * l + t] for l in range(L)], [recv_b[4 * l + t] for l in range(L)],
                                     w, m, v, f"adamw_big_{t}"))

    packed = _pack_small(small_parts)
    gathered_small = _all_gather_small(packed, "ag_small_grads")
    small_sum = _sum_devices(gathered_small, "sum_small_grads")
    shapes = [[(1, D), (1, HEAD_DIM), (1, HEAD_DIM), (H2, N_REL), (1, D), (1, 6 * D)]] * L
    names = ["g_norm1", "g_q", "g_k", "rel_bias", "g_norm2", "b_ada"]
    small_w = {"g_norm1": (g_norm1, m_g_norm1, v_g_norm1), "g_q": (g_q, m_g_q, v_g_q), "g_k": (g_k, m_g_k, v_g_k),
               "rel_bias": (rel_bias, m_rel_bias, v_rel_bias), "g_norm2": (g_norm2, m_g_norm2, v_g_norm2),
               "b_ada": (b_ada, m_b_ada, v_b_ada)}
    packs = [_pack_small([[small_w[n][k][l] for n in names] for l in range(L)]) for k in range(3)]
    n_pad = packed.shape[1]
    as_rows = lambda a: a.reshape(n_pad // 128, 128)
    sd, sm, sv_ = _adamw(as_rows(packs[0]), as_rows(small_sum), as_rows(packs[1]), as_rows(packs[2]), "adamw_small")
    small_out = {}
    for key, flat in [("grad", small_sum), ("delta", sd), ("m", sm), ("v", sv_)]:
        per_layer = _unpack_small(flat.reshape(-1), shapes)
        for i, n in enumerate(names):
            small_out[(key, n)] = jnp.stack([per_layer[l][i].reshape(small_w[n][0].shape[1:]) for l in range(L)])

    layer_len = 2 * D + 2 * HEAD_DIM + H2 * N_REL + 6 * D
    rows = gathered_small.reshape(NDEV, n_pad)
    dmod_all = jnp.stack([rows[:, l * layer_len + layer_len - 6 * D:(l + 1) * layer_len] for l in range(L)])
    dmod_cols = lax.dynamic_slice(dmod_all, (0, 0, me * Ca), (L, NDEV, Ca))
    dmod_cols = jnp.pad(dmod_cols, ((0, 0), (0, 128 - NDEV), (0, 0)))
    c_t = jnp.pad(c_all.T, ((0, 0), (0, 128 - NDEV)))
    g_ada = _w_ada_grad(c_t, dmod_cols, "w_ada_grad")
    flat2 = lambda a: a.reshape(L * D, Ca)
    ad, am, av = _adamw(flat2(w_ada), flat2(g_ada), flat2(m_w_ada), flat2(v_w_ada), "adamw_w_ada")
    ada_out = [g_ada] + [a.reshape(L, D, Ca) for a in (ad, am, av)]

    def leaf(kind):
        k = {"grad": 0, "delta": 1, "m": 2, "v": 3}[kind]
        return [small_out[(kind, "g_norm1")], big_out[0][k], small_out[(kind, "g_q")], small_out[(kind, "g_k")],
                small_out[(kind, "rel_bias")], big_out[1][k], small_out[(kind, "g_norm2")], big_out[2][k],
                big_out[3][k], ada_out[k], small_out[(kind, "b_ada")]]

    loss = lax.psum(loss_part[0, 0], ("x", "y", "c"))
    return (loss, grad_x, *leaf("grad"), *leaf("delta"), *leaf("m"), *leaf("v"))
```

```python
import functools

import jax
import jax.numpy as jnp
from jax import lax
from jax.experimental import pallas as pl
from jax.experimental.pallas import tpu as pltpu

F32 = jnp.float32
_MXU = jnp.bfloat16

HEAD_DIM = 64
CHUNK = 64
LEFT_CHUNKS = 8
PAD = LEFT_CHUNKS * CHUNK
BAND = PAD + CHUNK
REL_CLIP = 128
N_REL = 2 * REL_CLIP + 1
EPS = 1e-6
NEG = -1e30
NDEV = 8
SB_T = 128
CA_T = 2 * CHUNK
CA_W = CA_T + PAD
SB_SKIP = -104.0
PAIR = 2 * HEAD_DIM
SB_PAIRS = 4
CA_PAIRS_FWD = 2
CA_PAIRS_BWD = 1
ROW_BLOCK = 512
SKEW_W = 767

ADAM_LR, ADAM_B1, ADAM_B2, ADAM_EPS, ADAM_WD, ADAM_STEP = 0.001, 0.9, 0.999, 1e-08, 0.01, 10

MESH = pl.DeviceIdType.MESH
VMEM_SPEC = pl.BlockSpec(memory_space=pltpu.VMEM)
SMEM_SPEC = pl.BlockSpec(memory_space=pltpu.SMEM)
ANY_SPEC = pl.BlockSpec(memory_space=pl.ANY)


def _nn(a, b):
    return lax.dot_general(a, b, (((1,), (0,)), ((), ())), preferred_element_type=F32)


def _nt(a, b):
    return lax.dot_general(a, b, (((1,), (1,)), ((), ())), preferred_element_type=F32)


def _tn(a, b):
    return lax.dot_general(a, b, (((0,), (0,)), ((), ())), preferred_element_type=F32)


def _blk(n, pref):
    return pref if n % pref == 0 else n


def _pos():
    return lax.axis_index("x"), lax.axis_index("y"), lax.axis_index("c")


def _flip(v, bit):
    return 1 - v if bit else v


def _all_gather_small(blk, name):
    R, C = blk.shape

    def body(x_ref, out_ref, send_sems, recv_sems):
        x, y, c = _pos()
        me = 4 * x + 2 * y + c

        def peer(k):
            return (_flip(x, k & 4), _flip(y, k & 2), _flip(c, k & 1))

        def copy(k, slot):
            return pltpu.make_async_remote_copy(
                src_ref=x_ref, dst_ref=out_ref.at[slot], send_sem=send_sems.at[k - 1],
                recv_sem=recv_sems.at[k - 1], device_id=peer(k), device_id_type=MESH)

        out_ref[pl.ds(me, 1), :, :] = x_ref[...].reshape(1, R, C)
        sends = [copy(k, me) for k in range(1, NDEV)]
        for cp in sends:
            cp.start()
        for k in range(1, NDEV):
            px, py, pc = peer(k)
            copy(k, 4 * px + 2 * py + pc).wait_recv()
        for cp in sends:
            cp.wait_send()

    return pl.pallas_call(
        body, name=name,
        out_shape=jax.ShapeDtypeStruct((NDEV, R, C), blk.dtype),
        in_specs=[VMEM_SPEC], out_specs=VMEM_SPEC,
        scratch_shapes=[pltpu.SemaphoreType.DMA((NDEV - 1,)), pltpu.SemaphoreType.DMA((NDEV - 1,))],
    )(blk)


class _Exchange:
    def __init__(self, inputs, out_shapes, sems, start, finish, middle=None):
        self.inputs, self.out_shapes, self.sems = list(inputs), list(out_shapes), list(sems)
        self.start, self.middle, self.finish = start, middle, finish


def _run_exchange(ex, name):
    n_in, n_out = len(ex.inputs), len(ex.out_shapes)

    def body(*refs):
        ins, outs, sems = refs[:n_in], refs[n_in:n_in + n_out], refs[n_in + n_out:]
        ex.start(ins, outs, sems)
        if ex.middle is not None:
            ex.middle(ins, outs, sems)
        ex.finish(ins, outs, sems)

    return pl.pallas_call(
        body, name=name, out_shape=ex.out_shapes, in_specs=[ANY_SPEC] * n_in, out_specs=[ANY_SPEC] * n_out,
        scratch_shapes=ex.sems,
    )(*ex.inputs)


def _hosted(body, n_in, n_out, ex, step, steps):
    if ex is None:
        return body
    xi, xo = len(ex.inputs), len(ex.out_shapes)

    def wrapped(*refs):
        own_in, ex_in = refs[:n_in], refs[n_in:n_in + xi]
        rest = refs[n_in + xi:]
        own_out, ex_out = rest[:n_out], rest[n_out:n_out + xo]
        rest = rest[n_out + xo:]
        own_scratch, ex_sems = rest[:len(rest) - len(ex.sems)], rest[len(rest) - len(ex.sems):]
        t = step()
        pl.when(t == 0)(lambda: ex.start(ex_in, ex_out, ex_sems))
        body(*own_in, *own_out, *own_scratch)
        if ex.middle is not None:
            pl.when(t == (steps * 3) // 5)(lambda: ex.middle(ex_in, ex_out, ex_sems))
        pl.when(t == steps - 1)(lambda: ex.finish(ex_in, ex_out, ex_sems))

    return wrapped


def _call_hosted(body, name, grid, in_specs, out_specs, out_shape, scratch, args, ex):
    n_in, n_out = len(in_specs), len(out_specs)
    steps = 1
    for extent in grid:
        steps *= extent

    def step():
        t = pl.program_id(0)
        for axis in range(1, len(grid)):
            t = t * grid[axis] + pl.program_id(axis)
        return t

    if ex is not None:
        in_specs = in_specs + [ANY_SPEC] * len(ex.inputs)
        out_specs = out_specs + [ANY_SPEC] * len(ex.out_shapes)
        out_shape = out_shape + ex.out_shapes
        scratch = scratch + ex.sems
        args = args + ex.inputs
    outs = pl.pallas_call(
        _hosted(body, n_in, n_out, ex, step, steps), name=name, grid=grid, in_specs=in_specs, out_specs=out_specs,
        out_shape=out_shape, scratch_shapes=scratch,
    )(*args)
    return list(outs[:n_out]), list(outs[n_out:])


def _gather_exchange(shards):
    n = len(shards)

    def setup(ins, outs, sems):
        send_sems, recv_sems, local_sems = sems
        x, y, c = _pos()
        me, sibling = (x, y, c), (x, y, 1 - c)
        chips = [(1 - x, y), (x, 1 - y), (1 - x, 1 - y)]

        def copy(i, k, block, to, src=None):
            px, py, pc = block
            dst = outs[i].at[4 * px + 2 * py + pc]
            return pltpu.make_async_remote_copy(
                src_ref=dst if src is None else src, dst_ref=dst, send_sem=send_sems.at[7 * i + k],
                recv_sem=recv_sems.at[7 * i + k], device_id=to, device_id_type=MESH)

        def mine(i):
            return pltpu.make_async_copy(ins[i], outs[i].at[4 * x + 2 * y + c], local_sems.at[i])

        def first(i):
            return [copy(i, 0, me, sibling, src=ins[i])] + [
                copy(i, 1 + j, me, (*chip, c), src=ins[i]) for j, chip in enumerate(chips)]

        def passed(i, j):
            return copy(i, 4 + j, (*chips[j], c), sibling)

        return me, sibling, chips, c, copy, mine, first, passed

    def start(ins, outs, sems):
        _, _, _, _, _, mine, first, _ = setup(ins, outs, sems)
        for i in range(n):
            mine(i).start()
            for cp in first(i):
                cp.start()

    def middle(ins, outs, sems):
        me, _, chips, c, copy, _, _, passed = setup(ins, outs, sems)
        for j, chip in enumerate(chips):
            for i in range(n):
                copy(i, 1 + j, (*chip, c), me).wait_recv()
                passed(i, j).start()

    def finish(ins, outs, sems):
        me, sibling, chips, c, copy, mine, first, passed = setup(ins, outs, sems)
        for i in range(n):
            copy(i, 0, sibling, me).wait_recv()
            for j, chip in enumerate(chips):
                copy(i, 4 + j, (*chip, 1 - c), me).wait_recv()
        for i in range(n):
            for cp in first(i) + [passed(i, j) for j in range(3)]:
                cp.wait_send()
            mine(i).wait()

    return _Exchange(
        shards, [jax.ShapeDtypeStruct((NDEV,) + s.shape, s.dtype) for s in shards],
        [pltpu.SemaphoreType.DMA((7 * n,)), pltpu.SemaphoreType.DMA((7 * n,)), pltpu.SemaphoreType.DMA((n,))],
        start, finish, middle)


def _sibling_exchange(grads):
    n = len(grads)

    def copies(ins, outs, sems):
        send_sems, recv_sems = sems
        x, y, c = _pos()
        return [pltpu.make_async_remote_copy(
            src_ref=ins[i].at[2 * q + (1 - c)], dst_ref=outs[i].at[q], send_sem=send_sems.at[4 * i + q],
            recv_sem=recv_sems.at[4 * i + q], device_id=(x, y, 1 - c), device_id_type=MESH)
            for i in range(n) for q in range(4)]

    def start(ins, outs, sems):
        for cp in copies(ins, outs, sems):
            cp.start()

    def finish(ins, outs, sems):
        for cp in copies(ins, outs, sems):
            cp.wait()

    return _Exchange(
        grads, [jax.ShapeDtypeStruct((4,) + g.shape[1:], g.dtype) for g in grads],
        [pltpu.SemaphoreType.DMA((4 * n,)), pltpu.SemaphoreType.DMA((4 * n,))], start, finish)


def _chip_exchange(parts):
    n = len(parts)

    def copies(ins, outs, sems):
        send_sems, recv_sems = sems
        x, y, c = _pos()
        return [pltpu.make_async_remote_copy(
            src_ref=ins[i].at[j - 1], dst_ref=outs[i].at[j - 1], send_sem=send_sems.at[3 * i + j - 1],
            recv_sem=recv_sems.at[3 * i + j - 1], device_id=(_flip(x, j & 2), _flip(y, j & 1), c),
            device_id_type=MESH) for i in range(n) for j in range(1, 4)]

    def start(ins, outs, sems):
        for cp in copies(ins, outs, sems):
            cp.start()

    def finish(ins, outs, sems):
        for cp in copies(ins, outs, sems):
            cp.wait()

    return _Exchange(
        parts, [jax.ShapeDtypeStruct(p.shape, p.dtype) for p in parts],
        [pltpu.SemaphoreType.DMA((3 * n,)), pltpu.SemaphoreType.DMA((3 * n,))], start, finish)


def _rs_chip_partial(place, grad, recv, name):
    _, R, C = grad.shape
    tr = _blk(R, 256)

    def body(place_ref, *refs):
        g_refs, r_refs = refs[:4], refs[4:8]
        own_ref, out_ref = refs[8:]
        own_ref[...] = g_refs[0][0] + r_refs[0][0]
        for j in range(1, 4):
            out_ref[j - 1] = (g_refs[j][0] + r_refs[j][0]).astype(out_ref.dtype)

    def g_map(j):
        return lambda i, p: (2 * jnp.bitwise_xor(p[0], j) + p[1], i, 0)

    def r_map(j):
        return lambda i, p: (jnp.bitwise_xor(p[0], j), i, 0)

    grid_spec = pltpu.PrefetchScalarGridSpec(
        num_scalar_prefetch=1, grid=(R // tr,),
        in_specs=[pl.BlockSpec((1, tr, C), g_map(j)) for j in range(4)]
        + [pl.BlockSpec((1, tr, C), r_map(j)) for j in range(4)],
        out_specs=[pl.BlockSpec((tr, C), lambda i, p: (i, 0)), pl.BlockSpec((3, tr, C), lambda i, p: (0, i, 0))])
    return pl.pallas_call(
        body, name=name, grid_spec=grid_spec,
        out_shape=[jax.ShapeDtypeStruct((R, C), F32), jax.ShapeDtypeStruct((3, R, C), _MXU)],
    )(place, *([grad] * 4), *([recv] * 4))


def _adamw_math(w, g, m, v):
    m = ADAM_B1 * m + (1.0 - ADAM_B1) * g
    v = ADAM_B2 * v + (1.0 - ADAM_B2) * (g * g)
    m_hat = m / (1.0 - ADAM_B1 ** ADAM_STEP)
    v_hat = v / (1.0 - ADAM_B2 ** ADAM_STEP)
    delta = -ADAM_LR * (m_hat / (jnp.sqrt(v_hat) + ADAM_EPS) + ADAM_WD * w)
    return delta, m, v


def _adamw(w, g, m, v, name):
    R, C = w.shape
    tr = _blk(R, 256)

    def body(w_ref, g_ref, m_ref, v_ref, d_ref, nm_ref, nv_ref):
        d, nm, nv = _adamw_math(w_ref[...], g_ref[...], m_ref[...], v_ref[...])
        d_ref[...] = d
        nm_ref[...] = nm
        nv_ref[...] = nv

    spec = pl.BlockSpec((tr, C), lambda i: (i, 0))
    return pl.pallas_call(
        body, name=name, grid=(R // tr,), in_specs=[spec] * 4, out_specs=[spec] * 3,
        out_shape=[jax.ShapeDtypeStruct((R, C), F32)] * 3,
    )(w, g, m, v)


def _rs_sum_adamw(owns, recvs, w, m, v, name):
    L, R, C = w.shape
    tr = _blk(R, 256)
    nr = R // tr

    def body(o0, o1, r0, r1, w_ref, m_ref, v_ref, g_ref, d_ref, nm_ref, nv_ref):
        def step(o_ref, r_ref):
            g = o_ref[...]
            for j in range(3):
                g = g + r_ref[j].astype(F32)
            d, nm, nv = _adamw_math(w_ref[0], g, m_ref[0], v_ref[0])
            g_ref[0] = g
            d_ref[0] = d
            nm_ref[0] = nm
            nv_ref[0] = nv

        pl.when(pl.program_id(0) == 0)(lambda: step(o0, r0))
        pl.when(pl.program_id(0) == 1)(lambda: step(o1, r1))

    def hold(layer):
        if layer == 0:
            return lambda l, i: i * (1 - l) + (nr - 1) * l
        return lambda l, i: i * l

    own_spec = [pl.BlockSpec((tr, C), functools.partial(lambda l, i, f: (f(l, i), 0), f=hold(k))) for k in range(2)]
    recv_spec = [pl.BlockSpec((3, tr, C), functools.partial(lambda l, i, f: (0, f(l, i), 0), f=hold(k)))
                 for k in range(2)]
    lay = pl.BlockSpec((1, tr, C), lambda l, i: (l, i, 0))
    return pl.pallas_call(
        body, name=name, grid=(L, nr),
        in_specs=own_spec + recv_spec + [lay] * 3, out_specs=[lay] * 4,
        out_shape=[jax.ShapeDtypeStruct((L, R, C), F32)] * 4,
    )(owns[0], owns[1], recvs[0], recvs[1], w, m, v)


def _silu(x):
    return x / (1.0 + jnp.exp(-x))


def _mod_partial(c_all, w_ada, b_cols, name):
    L, D, Ca = w_ada.shape

    def body(c_ref, w_ref, b_ref, o_ref):
        act = _silu(c_ref[...]).astype(_MXU)
        for l in range(L):
            o_ref[:, l * Ca:(l + 1) * Ca] = _nn(act, w_ref[l].astype(_MXU)) + b_ref[l:l + 1, :]

    return pl.pallas_call(
        body, name=name, out_shape=jax.ShapeDtypeStruct((NDEV, L * Ca), F32),
        in_specs=[VMEM_SPEC] * 3, out_specs=VMEM_SPEC,
    )(c_all, w_ada, b_cols)


def _w_ada_grad(c_t, dmod_cols, name):
    L, _, Ca = dmod_cols.shape
    D = c_t.shape[0]

    def body(c_ref, d_ref, o_ref):
        act = _silu(c_ref[...]).astype(_MXU)
        for l in range(L):
            o_ref[l] = _nn(act, d_ref[l].astype(_MXU))

    return pl.pallas_call(
        body, name=name, out_shape=jax.ShapeDtypeStruct((L, D, Ca), F32),
        in_specs=[VMEM_SPEC] * 2, out_specs=VMEM_SPEC,
    )(c_t, dmod_cols)


def _sum_devices(gathered, name):
    _, _, N = gathered.shape

    def body(x_ref, o_ref):
        acc = x_ref[0]
        for d in range(1, NDEV):
            acc = acc + x_ref[d]
        o_ref[...] = acc

    return pl.pallas_call(
        body, name=name, out_shape=jax.ShapeDtypeStruct((1, N), F32),
        in_specs=[VMEM_SPEC], out_specs=VMEM_SPEC,
    )(gathered)


def _ln_mod_matmul(x, g, sc, sh, w, name):
    S, D = x.shape
    N = w.shape[1]
    tm = _blk(S, ROW_BLOCK)

    def body(x_ref, g_ref, sc_ref, sh_ref, w_ref, o_ref, h_ref):
        xv = x_ref[...]
        r = lax.rsqrt(jnp.mean(xv * xv, axis=-1, keepdims=True) + EPS)
        hv = ((xv * r) * g_ref[...]) * (1.0 + sc_ref[...]) + sh_ref[...]
        hb = hv.astype(_MXU)
        h_ref[...] = hb
        o_ref[...] = _nn(hb, w_ref[...]).astype(o_ref.dtype)

    vec = pl.BlockSpec((1, D), lambda i: (0, 0))
    row = lambda width: pl.BlockSpec((tm, width), lambda i: (i, 0))
    return pl.pallas_call(
        body, name=name, grid=(S // tm,),
        in_specs=[row(D), vec, vec, vec, pl.BlockSpec((D, N), lambda i: (0, 0))],
        out_specs=[row(N), row(D)],
        out_shape=[jax.ShapeDtypeStruct((S, N), _MXU), jax.ShapeDtypeStruct((S, D), _MXU)],
    )(x, g, sc, sh, w)


def _matmul_res_gate(a, w, xres, gt, relu2, name):
    S, K = a.shape
    N = w.shape[1]
    tm = _blk(S, 512)

    def body(a_ref, w_ref, x_ref, gt_ref, o_ref, f_ref):
        av = a_ref[...]
        if relu2:
            af = jnp.maximum(av.astype(F32), 0.0)
            av = (af * af).astype(_MXU)
        f = _nn(av, w_ref[...])
        f_ref[...] = f.astype(f_ref.dtype)
        o_ref[...] = x_ref[...] + gt_ref[...] * f

    row = lambda width: pl.BlockSpec((tm, width), lambda i: (i, 0))
    return pl.pallas_call(
        body, name=name, grid=(S // tm,),
        in_specs=[row(K), pl.BlockSpec((K, N), lambda i: (0, 0)), row(N), pl.BlockSpec((1, N), lambda i: (0, 0))],
        out_specs=[row(N), row(N)],
        out_shape=[jax.ShapeDtypeStruct((S, N), F32), jax.ShapeDtypeStruct((S, N), _MXU)],
    )(a, w, xres, gt)


def _loss_grad(y, t, name):
    S, D = y.shape
    tm = _blk(S, 512)
    last = S // tm - 1

    def body(y_ref, t_ref, dy_ref, l_ref, acc_ref):
        i = pl.program_id(0)
        e = y_ref[...] - t_ref[...]
        dy_ref[...] = e * (1.0 / D)
        part = jnp.sum(e * e, axis=0, keepdims=True)

        @pl.when(i == 0)
        def _():
            acc_ref[...] = part

        @pl.when(i > 0)
        def _():
            acc_ref[...] += part

        @pl.when(i == last)
        def _():
            l_ref[...] = (0.5 / D) * jnp.sum(acc_ref[...], axis=1, keepdims=True)

    row = pl.BlockSpec((tm, D), lambda i: (i, 0))
    return pl.pallas_call(
        body, name=name, grid=(S // tm,), in_specs=[row, row],
        out_specs=[row, pl.BlockSpec((1, 1), lambda i: (0, 0))],
        out_shape=[jax.ShapeDtypeStruct((S, D), F32), jax.ShapeDtypeStruct((1, 1), F32)],
        scratch_shapes=[pltpu.VMEM((1, D), F32)],
    )(y, t)


def _accumulate(ref, part, first):
    @pl.when(first)
    def _():
        ref[...] = part

    @pl.when(jnp.logical_not(first))
    def _():
        ref[...] += part


def _gate_nt_matmul(dx, f, gt, w, u, name):
    S, D = dx.shape
    N = w.shape[0]
    tm = _blk(S, ROW_BLOCK if N <= D else ROW_BLOCK // 2)
    with_u = u is not None

    def body(*refs):
        if with_u:
            dx_ref, f_ref, gt_ref, w_ref, u_ref, dz_ref, dgt_ref, res_ref = refs
        else:
            dx_ref, f_ref, gt_ref, w_ref, dz_ref, dgt_ref, res_ref = refs
        dxv = dx_ref[...]
        dz = (dxv * gt_ref[...]).astype(_MXU)
        dz_ref[...] = dz
        _accumulate(dgt_ref, jnp.sum(dxv * f_ref[...].astype(F32), axis=0, keepdims=True), pl.program_id(0) == 0)
        r = _nt(dz, w_ref[...])
        if with_u:
            r = r * (2.0 * jnp.maximum(u_ref[...].astype(F32), 0.0))
        res_ref[...] = r.astype(res_ref.dtype)

    row = lambda width: pl.BlockSpec((tm, width), lambda i: (i, 0))
    in_specs = [row(D), row(D), pl.BlockSpec((1, D), lambda i: (0, 0)), pl.BlockSpec((N, D), lambda i: (0, 0))]
    args = [dx, f, gt, w]
    if with_u:
        in_specs.append(row(N))
        args.append(u)
    return pl.pallas_call(
        body, name=name, grid=(S // tm,), in_specs=in_specs,
        out_specs=[row(D), pl.BlockSpec((1, D), lambda i: (0, 0)), row(N)],
        out_shape=[jax.ShapeDtypeStruct((S, D), _MXU), jax.ShapeDtypeStruct((1, D), F32),
                   jax.ShapeDtypeStruct((S, N), _MXU)],
    )(*args)


def _tn_matmul(a, b, by_col, relu2, name):
    S, Ka = a.shape
    Nb = b.shape[1]
    ts = _blk(S, ROW_BLOCK)
    half = NDEV // 2
    if by_col:
        R, C = Ka, Nb // NDEV
        a_spec = pl.BlockSpec((ts, Ka), lambda h, k: (k, 0))
        b_spec = pl.BlockSpec((ts, half * C), lambda h, k: (k, h))
    else:
        R, C = Ka // NDEV, Nb
        a_spec = pl.BlockSpec((ts, half * R), lambda h, k: (k, h))
        b_spec = pl.BlockSpec((ts, Nb), lambda h, k: (k, 0))

    def body(a_ref, b_ref, o_ref):
        av = a_ref[...]
        if relu2:
            af = jnp.maximum(av.astype(F32), 0.0)
            av = (af * af).astype(_MXU)
        p = _tn(av, b_ref[...])
        first = pl.program_id(1) == 0
        for d in range(half):
            part = p[:, d * C:(d + 1) * C] if by_col else p[d * R:(d + 1) * R, :]
            _accumulate(o_ref.at[d], part, first)

    return pl.pallas_call(
        body, name=name, grid=(NDEV // half, S // ts), in_specs=[a_spec, b_spec],
        out_specs=pl.BlockSpec((half, R, C), lambda h, k: (h, 0, 0)),
        out_shape=jax.ShapeDtypeStruct((NDEV, R, C), F32),
    )(a, b)


def _nt_ln_bwd(dy, w, x, g, sc, sh, dxres, name):
    S, D = x.shape
    N = w.shape[1]
    tm = _blk(S, ROW_BLOCK)

    def body(dy_ref, w_ref, x_ref, g_ref, sc_ref, sh_ref, dxr_ref, dx_ref, dsh_ref, dsc_ref, dg_ref):
        dh = _nt(dy_ref[...], w_ref[...])
        xv = x_ref[...]
        r = lax.rsqrt(jnp.mean(xv * xv, axis=-1, keepdims=True) + EPS)
        xhat = xv * r
        gv = g_ref[...]
        dn = dh * (1.0 + sc_ref[...])
        dxhat = dn * gv
        dxv = r * (dxhat - xhat * jnp.mean(dxhat * xhat, axis=-1, keepdims=True))
        dx_ref[...] = dxr_ref[...] + dxv
        first = pl.program_id(0) == 0
        _accumulate(dsh_ref, jnp.sum(dh, axis=0, keepdims=True), first)
        _accumulate(dsc_ref, jnp.sum(dh * (xhat * gv), axis=0, keepdims=True), first)
        _accumulate(dg_ref, jnp.sum(dn * xhat, axis=0, keepdims=True), first)

    row = lambda width: pl.BlockSpec((tm, width), lambda i: (i, 0))
    vec = pl.BlockSpec((1, D), lambda i: (0, 0))
    return pl.pallas_call(
        body, name=name, grid=(S // tm,),
        in_specs=[row(N), pl.BlockSpec((D, N), lambda i: (0, 0)), row(D), vec, vec, vec, row(D)],
        out_specs=[row(D), vec, vec, vec],
        out_shape=[jax.ShapeDtypeStruct((S, D), F32)] + [jax.ShapeDtypeStruct((1, D), F32)] * 3,
    )(dy, w, x, g, sc, sh, dxres)


def _split2(v):
    hi = v.astype(_MXU)
    mid = (v - hi.astype(F32)).astype(_MXU)
    return hi, mid


def _tri_sums(v, tri2):
    T = v.shape[0]
    hi, mid = _split2(v)
    s = _nn(hi, tri2) + _nn(mid, tri2)
    return s[:, :T], s[:, T:]


def _tri2(T, inclusive):
    j = lax.broadcasted_iota(jnp.int32, (T, 2 * T), 0)
    s = lax.broadcasted_iota(jnp.int32, (T, 2 * T), 1)
    keep = (j >= s) if inclusive else (j > s)
    return jnp.where((s >= T) | keep, 1.0, 0.0).astype(_MXU)


def _log_sigmoid(z):
    return jnp.minimum(z, 0.0) - jnp.log(1.0 + jnp.exp(-jnp.abs(z)))


def _sb_blocks(qbs, kblks, strict, tri2, carry):
    scale = HEAD_DIM ** -0.5
    zs = [_nt(qb, kblk) * scale for qb, kblk in zip(qbs, kblks)]
    lbs, sums = [], []
    for z in zs:
        lb = _log_sigmoid(z)
        l1 = lb - z
        if strict is not None:
            l1 = jnp.where(strict, l1, 0.0)
        lbs.append(lb)
        sums.append(_tri_sums(l1, tri2))
    amps, new_carry = [], []
    for lb, (sfx, tot), c in zip(lbs, sums, carry):
        a = jnp.exp(lb + sfx + c)
        if strict is not None:
            a = jnp.where(strict, a, 0.0)
        amps.append(a)
        new_carry.append(c + tot)
    return lbs, amps, new_carry


def _sb_alive(carry):
    top = carry[0]
    for c in carry[1:]:
        top = jnp.maximum(top, c)
    return jnp.max(top) > SB_SKIP


def _skew_index():
    i = lax.broadcasted_iota(jnp.int32, (CA_T, SKEW_W + 1), 0)
    m = lax.broadcasted_iota(jnp.int32, (CA_T, SKEW_W + 1), 1)
    wrapped = i + m >= SKEW_W
    row = jnp.where(wrapped, i + 1, i)
    j = jnp.where(wrapped, i + m - SKEW_W, i + m)
    a = row // CHUNK
    jj = j - a * CHUNK
    inband = (jj >= 0) & (jj < BAND) & (j < CA_W) & (row < CA_T)
    idx = jnp.clip((row - a * CHUNK) + PAD - jj, -REL_CLIP, REL_CLIP) + REL_CLIP
    return inband, idx, wrapped


def _skew(tile):
    H = tile.shape[0]
    flat = jnp.pad(tile, ((0, 0), (0, 0), (0, SKEW_W - CA_W))).reshape(H, CA_T * SKEW_W)
    return jnp.pad(flat, ((0, 0), (0, CA_T))).reshape(H, CA_T, SKEW_W + 1)


def _unskew(view):
    H = view.shape[0]
    flat = view.reshape(H, CA_T * (SKEW_W + 1))[:, :CA_T * SKEW_W]
    return flat.reshape(H, CA_T, SKEW_W)[:, :, :CA_W]


def _ca_bias(rel_bias, name):
    H = rel_bias.shape[0]
    top = rel_bias[:, N_REL - 1:]
    by_offset = jnp.concatenate(
        [jnp.broadcast_to(top, (H, PAD - REL_CLIP + 1)), jnp.flip(rel_bias[:, :N_REL - 1], axis=1),
         jnp.broadcast_to(top, (H, SKEW_W + 1 - (PAD - REL_CLIP + 1) - (N_REL - 1)))], axis=1)

    def body(t_ref, o_ref):
        inband, _, wrapped = _skew_index()
        vals = jnp.where(wrapped, t_ref[0][:, 0:1], t_ref[0])
        o_ref[0] = jnp.where(inband, vals, NEG)

    view = pl.pallas_call(
        body, name=name, grid=(H,), in_specs=[pl.BlockSpec((1, 1, SKEW_W + 1), lambda h: (h, 0, 0))],
        out_specs=pl.BlockSpec((1, CA_T, SKEW_W + 1), lambda h: (h, 0, 0)),
        out_shape=jax.ShapeDtypeStruct((H, CA_T, SKEW_W + 1), F32),
    )(by_offset.reshape(H, 1, SKEW_W + 1))
    return _unskew(view)


def _ca_bias_bwd(dbias, name):
    H = dbias.shape[0]

    def body(d_ref, o_ref):
        inband, idx, _ = _skew_index()
        d = jnp.where(inband, d_ref[0], 0.0)
        clipped = idx == N_REL - 1
        by_offset = jnp.sum(jnp.where(clipped, 0.0, d), axis=0, keepdims=True)
        top = jnp.sum(jnp.sum(jnp.where(clipped, d, 0.0), axis=0, keepdims=True), axis=1, keepdims=True)
        lane = lax.broadcasted_iota(jnp.int32, (1, SKEW_W + 1), 1)
        o_ref[0] = jnp.where(lane == 0, top, by_offset)

    out = pl.pallas_call(
        body, name=name, grid=(H,), in_specs=[pl.BlockSpec((1, CA_T, SKEW_W + 1), lambda h: (h, 0, 0))],
        out_specs=pl.BlockSpec((1, 1, SKEW_W + 1), lambda h: (h, 0, 0)),
        out_shape=jax.ShapeDtypeStruct((H, 1, SKEW_W + 1), F32),
    )(_skew(dbias))[:, 0]
    first = PAD - REL_CLIP + 1
    return jnp.concatenate([jnp.flip(out[:, first:first + N_REL - 1], axis=1), out[:, 0:1]], axis=1)


def _low_lanes(rows):
    return lax.broadcasted_iota(jnp.int32, (rows, PAIR), 1) < HEAD_DIM


def _one_head(t2, low, first):
    tf = t2.astype(F32)
    return (jnp.where(low, tf, 0.0) if first else jnp.where(low, 0.0, tf)).astype(_MXU)


def _sb_fwd(proj, name, ex=None):
    S, W = proj.shape
    half = W // 6
    npair = half // PAIR
    T = _blk(S, SB_T)
    GP = _blk(npair, SB_PAIRS)
    GW = GP * PAIR
    nb = npair // GP

    def body(q_ref, k_ref, v_ref, o_ref, ox_ref):
        qi = pl.program_id(1)
        low = _low_lanes(T)
        qbs = []
        for j in range(GP):
            q2 = q_ref[:, j * PAIR:(j + 1) * PAIR]
            qbs += [_one_head(q2, low, True), _one_head(q2, low, False)]
        row = lax.broadcasted_iota(jnp.int32, (T, T), 0)
        col = lax.broadcasted_iota(jnp.int32, (T, T), 1)
        tri2 = _tri2(T, inclusive=False)

        def pairs(kb, carry, acc, fine, strict):
            rows = pl.ds(pl.multiple_of(kb * T, T), T)
            k2 = [k_ref[rows, j * PAIR:(j + 1) * PAIR] for j in range(GP)]
            v2 = [v_ref[rows, j * PAIR:(j + 1) * PAIR] for j in range(GP)]
            _, amps, carry = _sb_blocks(qbs, [k2[h // 2] for h in range(2 * GP)], strict, tri2, carry)
            parts = [_split2(a) for a in amps]
            hi = [_nn(parts[h][0], v2[h // 2]) for h in range(2 * GP)]
            mid = [_nn(parts[h][1], v2[h // 2]) for h in range(2 * GP)]
            acc = tuple(acc[j] + jnp.where(low, hi[2 * j], hi[2 * j + 1]) for j in range(GP))
            fine = tuple(fine[j] + jnp.where(low, mid[2 * j], mid[2 * j + 1]) for j in range(GP))
            return tuple(carry), acc, fine

        zero = (jnp.zeros((T, PAIR), F32),) * GP
        carry, acc, fine = pairs(qi, (jnp.zeros((T, T), F32),) * (2 * GP), zero, zero, col < row)

        def cond(st):
            kb, carry, _, _ = st
            return jnp.logical_and(kb >= 0, _sb_alive(carry))

        def step(st):
            kb, carry, acc, fine = st
            carry, acc, fine = pairs(kb, carry, acc, fine, None)
            return kb - 1, carry, acc, fine

        _, _, acc, fine = lax.while_loop(cond, step, (qi - 1, carry, acc, fine))
        for j in range(GP):
            o_ref[:, j * PAIR:(j + 1) * PAIR] = acc[j].astype(o_ref.dtype)
            ox_ref[:, j * PAIR:(j + 1) * PAIR] = acc[j] + fine[j]

    blk = pl.BlockSpec((T, GW), lambda p, i: (i, p))
    return _call_hosted(
        body, name, (nb, S // T),
        [blk, pl.BlockSpec((S, GW), lambda p, i: (0, nb + p)), pl.BlockSpec((S, GW), lambda p, i: (0, 2 * nb + p))],
        [blk, blk], [jax.ShapeDtypeStruct((S, half), _MXU), jax.ShapeDtypeStruct((S, half), F32)],
        [], [proj, proj, proj], ex)


def _sb_bwd(proj, ox, dmixed, name, ex=None):
    S, W = proj.shape
    half = W // 6
    npair = half // PAIR
    T = _blk(S, SB_T)
    GP = _blk(npair, SB_PAIRS)
    GW = GP * PAIR
    nb = npair // GP
    last = S // T - 1
    scale = HEAD_DIM ** -0.5

    def body(q_ref, k_ref, v_ref, ox_ref, do_ref, dq_ref, dk_ref, dv_ref, dka_ref, dva_ref):
        qi = pl.program_id(1)

        @pl.when(qi == 0)
        def _():
            dka_ref[...] = jnp.zeros_like(dka_ref)
            dva_ref[...] = jnp.zeros_like(dva_ref)

        low = _low_lanes(T)
        q2, do2, qbs, dobs, deltas = [], [], [], [], []
        for j in range(GP):
            cols = slice(j * PAIR, (j + 1) * PAIR)
            q2.append(q_ref[:, cols])
            do2.append(do_ref[:, cols])
            qbs += [_one_head(q2[j], low, True), _one_head(q2[j], low, False)]
            dobs += [_one_head(do2[j], low, True), _one_head(do2[j], low, False)]
            for e in range(2):
                deltas.append(jnp.sum(dobs[2 * j + e].astype(F32) * ox_ref[:, cols], axis=-1, keepdims=True))
        row = lax.broadcasted_iota(jnp.int32, (T, T), 0)
        col = lax.broadcasted_iota(jnp.int32, (T, T), 1)
        tri_ex = _tri2(T, inclusive=False)
        tri_in = _tri2(T, inclusive=True)

        def pairs(kb, carry, right, dq, strict):
            rows = pl.ds(pl.multiple_of(kb * T, T), T)
            k2 = [k_ref[rows, j * PAIR:(j + 1) * PAIR] for j in range(GP)]
            v2 = [v_ref[rows, j * PAIR:(j + 1) * PAIR] for j in range(GP)]
            nh = 2 * GP
            gs = [_nt(dobs[h], v2[h // 2]) for h in range(nh)]
            lbs, amps, carry = _sb_blocks(qbs, [k2[h // 2] for h in range(nh)], strict, tri_ex, carry)
            ags = [a * gg for a, gg in zip(amps, gs)]
            sums = [_tri_sums(ag, tri_in) for ag in ags]
            dzbs = []
            for h in range(nh):
                left = deltas[h] - (sums[h][0] + right[h])
                beta = jnp.exp(lbs[h])
                dz = ags[h] * (1.0 - beta) - beta * left
                if strict is not None:
                    dz = jnp.where(strict, dz, 0.0)
                dzbs.append((dz * scale).astype(_MXU))
            abs_ = [a.astype(_MXU) for a in amps]
            dks = [_tn(dzbs[h], q2[h // 2]) for h in range(nh)]
            dvs = [_tn(abs_[h], do2[h // 2]) for h in range(nh)]
            dqs = [_nn(dzbs[h], k2[h // 2]) for h in range(nh)]
            for j in range(GP):
                cols = slice(j * PAIR, (j + 1) * PAIR)
                dka_ref[rows, cols] += jnp.where(low, dks[2 * j], dks[2 * j + 1])
                dva_ref[rows, cols] += jnp.where(low, dvs[2 * j], dvs[2 * j + 1])
            right = tuple(right[h] + sums[h][1] for h in range(nh))
            dq = tuple(dq[j] + jnp.where(low, dqs[2 * j], dqs[2 * j + 1]) for j in range(GP))
            return tuple(carry), right, dq

        zero = (jnp.zeros((T, T), F32),) * (2 * GP)
        carry, right, dq = pairs(qi, zero, zero, (jnp.zeros((T, PAIR), F32),) * GP, col < row)

        def cond(st):
            kb, carry, _, _ = st
            return jnp.logical_and(kb >= 0, _sb_alive(carry))

        def step(st):
            kb, carry, right, dq = st
            carry, right, dq = pairs(kb, carry, right, dq, None)
            return kb - 1, carry, right, dq

        _, _, _, dq = lax.while_loop(cond, step, (qi - 1, carry, right, dq))
        for j in range(GP):
            dq_ref[:, j * PAIR:(j + 1) * PAIR] = dq[j].astype(dq_ref.dtype)

        @pl.when(qi == last)
        def _():
            dk_ref[...] = dka_ref[...].astype(dk_ref.dtype)
            dv_ref[...] = dva_ref[...].astype(dv_ref.dtype)

    blk = pl.BlockSpec((T, GW), lambda p, i: (i, p))
    full = pl.BlockSpec((S, GW), lambda p, i: (0, p))
    return _call_hosted(
        body, name, (nb, S // T),
        [blk, pl.BlockSpec((S, GW), lambda p, i: (0, nb + p)), pl.BlockSpec((S, GW), lambda p, i: (0, 2 * nb + p)),
         blk, blk],
        [blk, full, full], [jax.ShapeDtypeStruct((S, half), _MXU)] * 3,
        [pltpu.VMEM((S, GW), F32), pltpu.VMEM((S, GW), F32)], [proj, proj, proj, ox, dmixed], ex)


def _pair_norm(t2, g2, low):
    tf = t2.astype(F32)
    sq = tf * tf
    both = jnp.sum(sq, axis=-1, keepdims=True)
    first = jnp.sum(jnp.where(low, sq, 0.0), axis=-1, keepdims=True)
    r = jnp.where(low, lax.rsqrt(first * (1.0 / HEAD_DIM) + EPS), lax.rsqrt((both - first) * (1.0 / HEAD_DIM) + EPS))
    hat = tf * r
    return hat * g2, hat, r


def _pair_norm_bwd(dn, hat, r, g2, low):
    dhat = dn * g2
    prod = dhat * hat
    both = jnp.sum(prod, axis=-1, keepdims=True)
    first = jnp.sum(jnp.where(low, prod, 0.0), axis=-1, keepdims=True)
    mean = jnp.where(low, first, both - first) * (1.0 / HEAD_DIM)
    return r * (dhat - hat * mean)


def _ca_fill(j, k_ref, v_ref, gk_ref, kn_ref, vp_ref):
    S = k_ref.shape[0]
    cols = slice(j * PAIR, (j + 1) * PAIR)
    kn, _, _ = _pair_norm(k_ref[:, cols], gk_ref[...], _low_lanes(S))
    kn_ref[j, 0:PAD, :] = jnp.zeros((PAD, PAIR), kn_ref.dtype)
    vp_ref[j, 0:PAD, :] = jnp.zeros((PAD, PAIR), vp_ref.dtype)
    kn_ref[j, PAD:PAD + S, :] = kn.astype(kn_ref.dtype)
    vp_ref[j, PAD:PAD + S, :] = v_ref[:, cols]


def _ca_scores(j, q_ref, b2_ref, gq_ref, kn_ref, qi, low):
    qn, qhat, r = _pair_norm(q_ref[:, j * PAIR:(j + 1) * PAIR], gq_ref[...], low)
    qn = qn * HEAD_DIM ** -0.5
    band = pl.ds(pl.multiple_of(qi * CA_T, CA_T), CA_W)
    key_pos = qi * CA_T - PAD + lax.broadcasted_iota(jnp.int32, (CA_T, CA_W), 1)
    scores = []
    for e in range(2):
        s = _nt(_one_head(qn, low, e == 0), kn_ref[j, band, :]) + b2_ref[2 * j + e]
        scores.append(jnp.where(key_pos >= 0, s, NEG))
    return scores, qn.astype(_MXU), qhat, r


def _softmax(s):
    e = jnp.exp(s - jnp.max(s, axis=-1, keepdims=True))
    return e * (1.0 / jnp.sum(e, axis=-1, keepdims=True))


def _ca_fwd(proj, bias2, gq2, gk2, name, ex=None):
    S, W = proj.shape
    half = W // 6
    npair = half // PAIR
    GP = _blk(npair, CA_PAIRS_FWD)
    GW = GP * PAIR
    nb = npair // GP

    def body(q_ref, k_ref, v_ref, b2_ref, gq_ref, gk_ref, o_ref, kn_ref, vp_ref):
        qi = pl.program_id(1)

        @pl.when(qi == 0)
        def _():
            for j in range(GP):
                _ca_fill(j, k_ref, v_ref, gk_ref, kn_ref, vp_ref)

        low = _low_lanes(CA_T)
        band = pl.ds(pl.multiple_of(qi * CA_T, CA_T), CA_W)
        scores = [_ca_scores(j, q_ref, b2_ref, gq_ref, kn_ref, qi, low)[0] for j in range(GP)]
        probs = [[_softmax(s).astype(_MXU) for s in pair] for pair in scores]
        for j in range(GP):
            outs = [_nn(probs[j][e], vp_ref[j, band, :]) for e in range(2)]
            o_ref[:, j * PAIR:(j + 1) * PAIR] = jnp.where(low, outs[0], outs[1]).astype(o_ref.dtype)

    vec = pl.BlockSpec((1, PAIR), lambda p, i: (0, 0))
    return _call_hosted(
        body, name, (nb, S // CA_T),
        [pl.BlockSpec((CA_T, GW), lambda p, i: (i, 3 * nb + p)),
         pl.BlockSpec((S, GW), lambda p, i: (0, 4 * nb + p)), pl.BlockSpec((S, GW), lambda p, i: (0, 5 * nb + p)),
         pl.BlockSpec((2 * GP, CA_T, CA_W), lambda p, i: (p, 0, 0)), vec, vec],
        [pl.BlockSpec((CA_T, GW), lambda p, i: (i, p))], [jax.ShapeDtypeStruct((S, half), _MXU)],
        [pltpu.VMEM((GP, PAD + S, PAIR), _MXU), pltpu.VMEM((GP, PAD + S, PAIR), _MXU)],
        [proj, proj, proj, bias2, gq2, gk2], ex)


def _ca_bwd(proj, bias2, gq2, gk2, dmixed, name, ex=None):
    S, W = proj.shape
    half = W // 6
    npair = half // PAIR
    GP = _blk(npair, CA_PAIRS_BWD)
    GW = GP * PAIR
    nb = npair // GP
    scale = HEAD_DIM ** -0.5
    last = S // CA_T - 1

    def body(q_ref, k_ref, v_ref, b2_ref, gq_ref, gk_ref, do_ref,
             dq_ref, dk_ref, dv_ref, db_ref, dgq_ref, dgk_ref, kn_ref, vp_ref, dkn_ref, dvp_ref):
        p_id, qi = pl.program_id(0), pl.program_id(1)

        @pl.when(qi == 0)
        def _():
            for j in range(GP):
                _ca_fill(j, k_ref, v_ref, gk_ref, kn_ref, vp_ref)
            dkn_ref[...] = jnp.zeros_like(dkn_ref)
            dvp_ref[...] = jnp.zeros_like(dvp_ref)
            db_ref[...] = jnp.zeros_like(db_ref)

        @pl.when(jnp.logical_and(p_id == 0, qi == 0))
        def _():
            dgq_ref[...] = jnp.zeros_like(dgq_ref)
            dgk_ref[...] = jnp.zeros_like(dgk_ref)

        low = _low_lanes(CA_T)
        low_w = _low_lanes(CA_W)
        band = pl.ds(pl.multiple_of(qi * CA_T, CA_T), CA_W)
        pairs = [_ca_scores(j, q_ref, b2_ref, gq_ref, kn_ref, qi, low) for j in range(GP)]
        do2 = [do_ref[:, j * PAIR:(j + 1) * PAIR] for j in range(GP)]
        dps = [[_nt(_one_head(do2[j], low, e == 0), vp_ref[j, band, :]) for e in range(2)] for j in range(GP)]
        probs, dsbs = [], []
        for j in range(GP):
            pj, dj = [], []
            for e in range(2):
                p = _softmax(pairs[j][0][e])
                ds = p * (dps[j][e] - jnp.sum(p * dps[j][e], axis=-1, keepdims=True))
                db_ref[2 * j + e] += ds
                pj.append(p.astype(_MXU))
                dj.append(ds.astype(_MXU))
            probs.append(pj)
            dsbs.append(dj)
        dgq = jnp.zeros((1, PAIR), F32)
        for j in range(GP):
            _, qn, qhat, r = pairs[j]
            dq_h = [_nn(dsbs[j][e], kn_ref[j, band, :]) for e in range(2)]
            dk_h = [_tn(dsbs[j][e], qn) for e in range(2)]
            dv_h = [_tn(probs[j][e], do2[j]) for e in range(2)]
            dkn_ref[j, band, :] += jnp.where(low_w, dk_h[0], dk_h[1])
            dvp_ref[j, band, :] += jnp.where(low_w, dv_h[0], dv_h[1])
            dqn = jnp.where(low, dq_h[0], dq_h[1]) * scale
            dgq = dgq + jnp.sum(dqn * qhat, axis=0, keepdims=True)
            dq_ref[:, j * PAIR:(j + 1) * PAIR] = _pair_norm_bwd(dqn, qhat, r, gq_ref[...], low).astype(dq_ref.dtype)
        dgq_ref[...] += dgq

        @pl.when(qi == last)
        def _():
            low_s = _low_lanes(S)
            for j in range(GP):
                cols = slice(j * PAIR, (j + 1) * PAIR)
                _, khat, rk = _pair_norm(k_ref[:, cols], gk_ref[...], low_s)
                dkn = dkn_ref[j, PAD:PAD + S, :]
                dgk_ref[...] += jnp.sum(dkn * khat, axis=0, keepdims=True)
                dk_ref[:, cols] = _pair_norm_bwd(dkn, khat, rk, gk_ref[...], low_s).astype(dk_ref.dtype)
                dv_ref[:, cols] = dvp_ref[j, PAD:PAD + S, :].astype(dv_ref.dtype)

    vec = pl.BlockSpec((1, PAIR), lambda p, i: (0, 0))
    tile = pl.BlockSpec((2 * GP, CA_T, CA_W), lambda p, i: (p, 0, 0))
    full = pl.BlockSpec((S, GW), lambda p, i: (0, p))
    return _call_hosted(
        body, name, (nb, S // CA_T),
        [pl.BlockSpec((CA_T, GW), lambda p, i: (i, 3 * nb + p)),
         pl.BlockSpec((S, GW), lambda p, i: (0, 4 * nb + p)), pl.BlockSpec((S, GW), lambda p, i: (0, 5 * nb + p)),
         tile, vec, vec, pl.BlockSpec((CA_T, GW), lambda p, i: (i, nb + p))],
        [pl.BlockSpec((CA_T, GW), lambda p, i: (i, p)), full, full, tile, vec, vec],
        [jax.ShapeDtypeStruct((S, half), _MXU)] * 3
        + [jax.ShapeDtypeStruct(bias2.shape, F32), jax.ShapeDtypeStruct((1, PAIR), F32),
           jax.ShapeDtypeStruct((1, PAIR), F32)],
        [pltpu.VMEM((GP, PAD + S, PAIR), _MXU), pltpu.VMEM((GP, PAD + S, PAIR), _MXU),
         pltpu.VMEM((GP, PAD + S, PAIR), F32), pltpu.VMEM((GP, PAD + S, PAIR), F32)],
        [proj, proj, proj, bias2, gq2, gk2, dmixed], ex)


def _pack_small(parts):
    flat = jnp.concatenate([p.reshape(-1) for layer in parts for p in layer])
    n = flat.shape[0]
    n_pad = -(-n // 1024) * 1024
    return jnp.pad(flat, (0, n_pad - n)).reshape(1, n_pad)


def _unpack_small(flat, shapes):
    out, off = [], 0
    for layer in shapes:
        cur = []
        for shp in layer:
            size = 1
            for s in shp:
                size *= s
            cur.append(flat[off:off + size].reshape(shp))
            off += size
        out.append(cur)
    return out


def kernel(x, c, g_norm1, w_in, g_q, g_k, rel_bias, w_o, g_norm2, w1, w2, w_ada, b_ada, loss_target, m_g_norm1, m_w_in, m_g_q, m_g_k, m_rel_bias, m_w_o, m_g_norm2, m_w1, m_w2, m_w_ada, m_b_ada, v_g_norm1, v_w_in, v_g_q, v_g_k, v_rel_bias, v_w_o, v_g_norm2, v_w1, v_w2, v_w_ada, v_b_ada):
    L = w_in.shape[0]
    S, D = x.shape[1:]
    H2 = D // HEAD_DIM // 2
    Ca = w_ada.shape[2]
    xi, yi, ci = _pos()
    me = 4 * xi + 2 * yi + ci
    place = jnp.stack([2 * xi + yi, ci]).astype(jnp.int32)

    c_all = _all_gather_small(c, "ag_c").reshape(NDEV, D)
    b_cols = lax.dynamic_slice(b_ada, (0, me * Ca), (L, Ca))
    mod_part = _mod_partial(c_all, w_ada, b_cols, "mod_partial")
    mod_all = _all_gather_small(mod_part, "ag_mod")
    mod = lax.dynamic_index_in_dim(mod_all, me, axis=1, keepdims=False)
    mod = mod.reshape(NDEV, L, Ca).transpose(1, 0, 2).reshape(L, 6, 1, D)

    wire = lambda a: a.astype(_MXU)
    by_cols = lambda g: g.transpose(1, 0, 2).reshape(D, g.shape[0] * g.shape[2])
    W_in = {0: by_cols(_run_exchange(_gather_exchange([wire(w_in[0])]), "ag_w_in0")[0])}
    W_o, W_1, W_2 = {}, {}, {}

    xs = [x[0]]
    saved = []
    for l in range(L):
        sh1, sc1, gt1, sh2, sc2, gt2 = [mod[l, i] for i in range(6)]
        gn1, gn2 = g_norm1[l:l + 1], g_norm2[l:l + 1]
        gq2, gk2 = jnp.tile(g_q[l:l + 1], (1, 2)), jnp.tile(g_k[l:l + 1], (1, 2))
        proj, h1 = _ln_mod_matmul(xs[-1], gn1, sc1, sh1, W_in[l], f"l{l}_proj")
        (o_sb, ox_sb), got = _sb_fwd(proj, f"l{l}_sb_fwd",
                                     _gather_exchange([wire(w_o[l]), wire(w1[l]), wire(w2[l])]))
        W_o[l], W_1[l], W_2[l] = got[0].reshape(D, D), by_cols(got[1]), got[2].reshape(4 * D, D)
        bias2 = _ca_bias(rel_bias[l], f"l{l}_ca_bias")
        (o_ca,), got = _ca_fwd(proj, bias2, gq2, gk2, f"l{l}_ca_fwd",
                               _gather_exchange([wire(w_in[l + 1])]) if l + 1 < L else None)
        if got:
            W_in[l + 1] = by_cols(got[0])
        mixed = jnp.concatenate([o_sb, o_ca], axis=1)
        x1, f1 = _matmul_res_gate(mixed, W_o[l], xs[-1], gt1, False, f"l{l}_attn_out")
        u, h2 = _ln_mod_matmul(x1, gn2, sc2, sh2, W_1[l], f"l{l}_mlp_in")
        x2, f2 = _matmul_res_gate(u, W_2[l], x1, gt2, True, f"l{l}_mlp_out")
        saved.append(dict(x0=xs[-1], h1=h1, proj=proj, ox_sb=ox_sb, bias2=bias2, mixed=mixed, f1=f1, x1=x1,
                          h2=h2, u=u, f2=f2))
        xs.append(x2)

    dx, loss_part = _loss_grad(xs[-1], loss_target[0], "loss")

    owns, recv_b = {}, {}
    ready = []
    small_parts = [None] * L

    def partials(keys, grads, recv_a):
        parts = []
        for key, g, r in zip(keys, grads, recv_a):
            owns[key], part = _rs_chip_partial(place, g, r, f"rs_partial_l{key[0]}_{key[1]}")
            parts.append(part)
        return parts

    for l in reversed(range(L)):
        sv = saved[l]
        sh1, sc1, gt1, sh2, sc2, gt2 = [mod[l, i] for i in range(6)]
        gn1, gn2 = g_norm1[l:l + 1], g_norm2[l:l + 1]
        gq2, gk2 = jnp.tile(g_q[l:l + 1], (1, 2)), jnp.tile(g_k[l:l + 1], (1, 2))
        dz2, dgt2, du = _gate_nt_matmul(dx, sv["f2"], gt2, W_2[l], sv["u"], f"l{l}_mlp_out_bwd")
        gw2 = _tn_matmul(sv["u"], dz2, False, True, f"l{l}_gw2")
        gw1 = _tn_matmul(sv["h2"], du, True, False, f"l{l}_gw1")
        dx, dsh2, dsc2, dgn2 = _nt_ln_bwd(du, W_1[l], sv["x1"], gn2, sc2, sh2, dx, f"l{l}_mlp_in_bwd")
        dz1, dgt1, dmixed = _gate_nt_matmul(dx, sv["f1"], gt1, W_o[l], None, f"l{l}_attn_out_bwd")
        gwo = _tn_matmul(sv["mixed"], dz1, False, False, f"l{l}_gwo")
        ready += [((l, 1), gwo), ((l, 2), gw1), ((l, 3), gw2)]
        keys, grads = [k for k, _ in ready], [g for _, g in ready]
        (dq_sb, dk_sb, dv_sb), recv_a = _sb_bwd(sv["proj"], sv["ox_sb"], dmixed, f"l{l}_sb_bwd",
                                                _sibling_exchange(grads))
        parts = partials(keys, grads, recv_a)
        (dq_ca, dk_ca, dv_ca, dbias2, dgq2, dgk2), got = _ca_bwd(sv["proj"], sv["bias2"], gq2, gk2, dmixed,
                                                                 f"l{l}_ca_bwd", _chip_exchange(parts))
        recv_b.update(zip(keys, got))
        dgq = dgq2[:, :HEAD_DIM] + dgq2[:, HEAD_DIM:]
        dgk = dgk2[:, :HEAD_DIM] + dgk2[:, HEAD_DIM:]
        drb = _ca_bias_bwd(dbias2, f"l{l}_ca_bias_bwd")
        dproj = jnp.concatenate([dq_sb, dk_sb, dv_sb, dq_ca, dk_ca, dv_ca], axis=1)
        gwin = _tn_matmul(sv["h1"], dproj, True, False, f"l{l}_gwin")
        ready = [((l, 0), gwin)]
        dx, dsh1, dsc1, dgn1 = _nt_ln_bwd(dproj, W_in[l], sv["x0"], gn1, sc1, sh1, dx, f"l{l}_proj_bwd")
        dmod = jnp.concatenate([dsh1, dsc1, dgt1, dsh2, dsc2, dgt2], axis=1)
        small_parts[l] = [dgn1, dgq, dgk, drb, dgn2, dmod]
    grad_x = dx[None]

    keys, grads = [k for k, _ in ready], [g for _, g in ready]
    parts = partials(keys, grads, _run_exchange(_sibling_exchange(grads), "rs_sibling_last"))
    recv_b.update(zip(keys, _run_exchange(_chip_exchange(parts), "rs_chips_last")))
    big_out = []
    for t, (w, m, v) in enumerate([(w_in, m_w_in, v_w_in), (w_o, m_w_o, v_w_o), (w1, m_w1, v_w1), (w2, m_w2, v_w2)]):
        big_out.append(_rs_sum_adamw([owns[(l, t)] for l in range(L)], [recv_b[(l, t)] for l in range(L)],
                                     w, m, v, f"adamw_big_{t}"))

    packed = _pack_small(small_parts)
    gathered_small = _all_gather_small(packed, "ag_small_grads")
    small_sum = _sum_devices(gathered_small, "sum_small_grads")
    shapes = [[(1, D), (1, HEAD_DIM), (1, HEAD_DIM), (H2, N_REL), (1, D), (1, 6 * D)]] * L
    names = ["g_norm1", "g_q", "g_k", "rel_bias", "g_norm2", "b_ada"]
    small_w = {"g_norm1": (g_norm1, m_g_norm1, v_g_norm1), "g_q": (g_q, m_g_q, v_g_q), "g_k": (g_k, m_g_k, v_g_k),
               "rel_bias": (rel_bias, m_rel_bias, v_rel_bias), "g_norm2": (g_norm2, m_g_norm2, v_g_norm2),
               "b_ada": (b_ada, m_b_ada, v_b_ada)}
    packs = [_pack_small([[small_w[n][k][l] for n in names] for l in range(L)]) for k in range(3)]
    n_pad = packed.shape[1]
    as_rows = lambda a: a.reshape(n_pad // 128, 128)
    sd, sm, sv_ = _adamw(as_rows(packs[0]), as_rows(small_sum), as_rows(packs[1]), as_rows(packs[2]), "adamw_small")
    small_out = {}
    for key, flat in [("grad", small_sum), ("delta", sd), ("m", sm), ("v", sv_)]:
        per_layer = _unpack_small(flat.reshape(-1), shapes)
        for i, n in enumerate(names):
            small_out[(key, n)] = jnp.stack([per_layer[l][i].reshape(small_w[n][0].shape[1:]) for l in range(L)])

    layer_len = 2 * D + 2 * HEAD_DIM + H2 * N_REL + 6 * D
    rows = gathered_small.reshape(NDEV, n_pad)
    dmod_all = jnp.stack([rows[:, l * layer_len + layer_len - 6 * D:(l + 1) * layer_len] for l in range(L)])
    dmod_cols = lax.dynamic_slice(dmod_all, (0, 0, me * Ca), (L, NDEV, Ca))
    dmod_cols = jnp.pad(dmod_cols, ((0, 0), (0, 128 - NDEV), (0, 0)))
    c_t = jnp.pad(c_all.T, ((0, 0), (0, 128 - NDEV)))
    g_ada = _w_ada_grad(c_t, dmod_cols, "w_ada_grad")
    flat2 = lambda a: a.reshape(L * D, Ca)
    ad, am, av = _adamw(flat2(w_ada), flat2(g_ada), flat2(m_w_ada), flat2(v_w_ada), "adamw_w_ada")
    ada_out = [g_ada] + [a.reshape(L, D, Ca) for a in (ad, am, av)]

    def leaf(kind):
        k = {"grad": 0, "delta": 1, "m": 2, "v": 3}[kind]
        return [small_out[(kind, "g_norm1")], big_out[0][k], small_out[(kind, "g_q")], small_out[(kind, "g_k")],
                small_out[(kind, "rel_bias")], big_out[1][k], small_out[(kind, "g_norm2")], big_out[2][k],
                big_out[3][k], ada_out[k], small_out[(kind, "b_ada")]]

    loss = lax.psum(loss_part[0, 0], ("x", "y", "c"))
    return (loss, grad_x, *leaf("grad"), *leaf("delta"), *leaf("m"), *leaf("v"))
```

```python
import functools

import jax
import jax.numpy as jnp
from jax import lax
from jax.experimental import pallas as pl
from jax.experimental.pallas import tpu as pltpu

F32 = jnp.float32
_MXU = jnp.bfloat16

HEAD_DIM = 64
CHUNK = 64
LEFT_CHUNKS = 8
PAD = LEFT_CHUNKS * CHUNK
BAND = PAD + CHUNK
REL_CLIP = 128
N_REL = 2 * REL_CLIP + 1
EPS = 1e-6
NEG = -1e30
NDEV = 8
SB_T = 128
CA_T = 2 * CHUNK
CA_W = CA_T + PAD
SB_SKIP = -104.0
PAIR = 2 * HEAD_DIM
SB_PAIRS = 4
CA_PAIRS_FWD = 4
CA_PAIRS_BWD = 2
ROW_BLOCK = 512
SKEW_W = 767

ADAM_LR, ADAM_B1, ADAM_B2, ADAM_EPS, ADAM_WD, ADAM_STEP = 0.001, 0.9, 0.999, 1e-08, 0.01, 10

MESH = pl.DeviceIdType.MESH
VMEM_SPEC = pl.BlockSpec(memory_space=pltpu.VMEM)
SMEM_SPEC = pl.BlockSpec(memory_space=pltpu.SMEM)
ANY_SPEC = pl.BlockSpec(memory_space=pl.ANY)


def _nn(a, b):
    return lax.dot_general(a, b, (((1,), (0,)), ((), ())), preferred_element_type=F32)


def _nt(a, b):
    return lax.dot_general(a, b, (((1,), (1,)), ((), ())), preferred_element_type=F32)


def _tn(a, b):
    return lax.dot_general(a, b, (((0,), (0,)), ((), ())), preferred_element_type=F32)


def _blk(n, pref):
    return pref if n % pref == 0 else n


def _pos():
    return lax.axis_index("x"), lax.axis_index("y"), lax.axis_index("c")


def _flip(v, bit):
    return 1 - v if bit else v


def _all_gather_small(blk, name):
    R, C = blk.shape

    def body(x_ref, out_ref, send_sems, recv_sems):
        x, y, c = _pos()
        me = 4 * x + 2 * y + c

        def peer(k):
            return (_flip(x, k & 4), _flip(y, k & 2), _flip(c, k & 1))

        def copy(k, slot):
            return pltpu.make_async_remote_copy(
                src_ref=x_ref, dst_ref=out_ref.at[slot], send_sem=send_sems.at[k - 1],
                recv_sem=recv_sems.at[k - 1], device_id=peer(k), device_id_type=MESH)

        out_ref[pl.ds(me, 1), :, :] = x_ref[...].reshape(1, R, C)
        sends = [copy(k, me) for k in range(1, NDEV)]
        for cp in sends:
            cp.start()
        for k in range(1, NDEV):
            px, py, pc = peer(k)
            copy(k, 4 * px + 2 * py + pc).wait_recv()
        for cp in sends:
            cp.wait_send()

    return pl.pallas_call(
        body, name=name,
        out_shape=jax.ShapeDtypeStruct((NDEV, R, C), blk.dtype),
        in_specs=[VMEM_SPEC], out_specs=VMEM_SPEC,
        scratch_shapes=[pltpu.SemaphoreType.DMA((NDEV - 1,)), pltpu.SemaphoreType.DMA((NDEV - 1,))],
    )(blk)


class _Exchange:
    def __init__(self, inputs, out_shapes, sems, start, finish, middle=None):
        self.inputs, self.out_shapes, self.sems = list(inputs), list(out_shapes), list(sems)
        self.start, self.middle, self.finish = start, middle, finish


def _run_exchange(ex, name):
    n_in, n_out = len(ex.inputs), len(ex.out_shapes)

    def body(*refs):
        ins, outs, sems = refs[:n_in], refs[n_in:n_in + n_out], refs[n_in + n_out:]
        ex.start(ins, outs, sems)
        if ex.middle is not None:
            ex.middle(ins, outs, sems)
        ex.finish(ins, outs, sems)

    return pl.pallas_call(
        body, name=name, out_shape=ex.out_shapes, in_specs=[ANY_SPEC] * n_in, out_specs=[ANY_SPEC] * n_out,
        scratch_shapes=ex.sems,
    )(*ex.inputs)


def _hosted(body, n_in, n_out, ex, step, steps):
    if ex is None:
        return body
    xi, xo = len(ex.inputs), len(ex.out_shapes)

    def wrapped(*refs):
        own_in, ex_in = refs[:n_in], refs[n_in:n_in + xi]
        rest = refs[n_in + xi:]
        own_out, ex_out = rest[:n_out], rest[n_out:n_out + xo]
        rest = rest[n_out + xo:]
        own_scratch, ex_sems = rest[:len(rest) - len(ex.sems)], rest[len(rest) - len(ex.sems):]
        t = step()
        pl.when(t == 0)(lambda: ex.start(ex_in, ex_out, ex_sems))
        body(*own_in, *own_out, *own_scratch)
        if ex.middle is not None:
            pl.when(t == (steps * 3) // 5)(lambda: ex.middle(ex_in, ex_out, ex_sems))
        pl.when(t == steps - 1)(lambda: ex.finish(ex_in, ex_out, ex_sems))

    return wrapped


def _call_hosted(body, name, grid, in_specs, out_specs, out_shape, scratch, args, ex):
    n_in, n_out = len(in_specs), len(out_specs)
    steps = 1
    for extent in grid:
        steps *= extent

    def step():
        t = pl.program_id(0)
        for axis in range(1, len(grid)):
            t = t * grid[axis] + pl.program_id(axis)
        return t

    if ex is not None:
        in_specs = in_specs + [ANY_SPEC] * len(ex.inputs)
        out_specs = out_specs + [ANY_SPEC] * len(ex.out_shapes)
        out_shape = out_shape + ex.out_shapes
        scratch = scratch + ex.sems
        args = args + ex.inputs
    outs = pl.pallas_call(
        _hosted(body, n_in, n_out, ex, step, steps), name=name, grid=grid, in_specs=in_specs, out_specs=out_specs,
        out_shape=out_shape, scratch_shapes=scratch,
    )(*args)
    return list(outs[:n_out]), list(outs[n_out:])


def _gather_exchange(shards):
    n = len(shards)

    def setup(ins, outs, sems):
        send_sems, recv_sems, local_sems = sems
        x, y, c = _pos()
        me, sibling = (x, y, c), (x, y, 1 - c)
        chips = [(1 - x, y), (x, 1 - y), (1 - x, 1 - y)]

        def copy(i, k, block, to, src=None):
            px, py, pc = block
            dst = outs[i].at[4 * px + 2 * py + pc]
            return pltpu.make_async_remote_copy(
                src_ref=dst if src is None else src, dst_ref=dst, send_sem=send_sems.at[7 * i + k],
                recv_sem=recv_sems.at[7 * i + k], device_id=to, device_id_type=MESH)

        def mine(i):
            return pltpu.make_async_copy(ins[i], outs[i].at[4 * x + 2 * y + c], local_sems.at[i])

        def first(i):
            return [copy(i, 0, me, sibling, src=ins[i])] + [
                copy(i, 1 + j, me, (*chip, c), src=ins[i]) for j, chip in enumerate(chips)]

        def passed(i, j):
            return copy(i, 4 + j, (*chips[j], c), sibling)

        return me, sibling, chips, c, copy, mine, first, passed

    def start(ins, outs, sems):
        _, _, _, _, _, mine, first, _ = setup(ins, outs, sems)
        for i in range(n):
            mine(i).start()
            for cp in first(i):
                cp.start()

    def middle(ins, outs, sems):
        me, _, chips, c, copy, _, _, passed = setup(ins, outs, sems)
        for j, chip in enumerate(chips):
            for i in range(n):
                copy(i, 1 + j, (*chip, c), me).wait_recv()
                passed(i, j).start()

    def finish(ins, outs, sems):
        me, sibling, chips, c, copy, mine, first, passed = setup(ins, outs, sems)
        for i in range(n):
            copy(i, 0, sibling, me).wait_recv()
            for j, chip in enumerate(chips):
                copy(i, 4 + j, (*chip, 1 - c), me).wait_recv()
        for i in range(n):
            for cp in first(i) + [passed(i, j) for j in range(3)]:
                cp.wait_send()
            mine(i).wait()

    return _Exchange(
        shards, [jax.ShapeDtypeStruct((NDEV,) + s.shape, s.dtype) for s in shards],
        [pltpu.SemaphoreType.DMA((7 * n,)), pltpu.SemaphoreType.DMA((7 * n,)), pltpu.SemaphoreType.DMA((n,))],
        start, finish, middle)


def _sibling_exchange(grads):
    n = len(grads)

    def copies(ins, outs, sems):
        send_sems, recv_sems = sems
        x, y, c = _pos()
        return [pltpu.make_async_remote_copy(
            src_ref=ins[i].at[2 * q + (1 - c)], dst_ref=outs[i].at[q], send_sem=send_sems.at[4 * i + q],
            recv_sem=recv_sems.at[4 * i + q], device_id=(x, y, 1 - c), device_id_type=MESH)
            for i in range(n) for q in range(4)]

    def start(ins, outs, sems):
        for cp in copies(ins, outs, sems):
            cp.start()

    def finish(ins, outs, sems):
        for cp in copies(ins, outs, sems):
            cp.wait()

    return _Exchange(
        grads, [jax.ShapeDtypeStruct((4,) + g.shape[1:], g.dtype) for g in grads],
        [pltpu.SemaphoreType.DMA((4 * n,)), pltpu.SemaphoreType.DMA((4 * n,))], start, finish)


def _chip_exchange(parts):
    n = len(parts)

    def copies(ins, outs, sems):
        send_sems, recv_sems = sems
        x, y, c = _pos()
        return [pltpu.make_async_remote_copy(
            src_ref=ins[i].at[j - 1], dst_ref=outs[i].at[j - 1], send_sem=send_sems.at[3 * i + j - 1],
            recv_sem=recv_sems.at[3 * i + j - 1], device_id=(_flip(x, j & 2), _flip(y, j & 1), c),
            device_id_type=MESH) for i in range(n) for j in range(1, 4)]

    def start(ins, outs, sems):
        for cp in copies(ins, outs, sems):
            cp.start()

    def finish(ins, outs, sems):
        for cp in copies(ins, outs, sems):
            cp.wait()

    return _Exchange(
        parts, [jax.ShapeDtypeStruct(p.shape, p.dtype) for p in parts],
        [pltpu.SemaphoreType.DMA((3 * n,)), pltpu.SemaphoreType.DMA((3 * n,))], start, finish)


def _rs_chip_partial(place, grad, recv, name):
    _, R, C = grad.shape
    tr = _blk(R, 256)

    def body(place_ref, *refs):
        g_refs, r_refs = refs[:4], refs[4:8]
        own_ref, out_ref = refs[8:]
        own_ref[...] = g_refs[0][0] + r_refs[0][0]
        for j in range(1, 4):
            out_ref[j - 1] = (g_refs[j][0] + r_refs[j][0]).astype(out_ref.dtype)

    def g_map(j):
        return lambda i, p: (2 * jnp.bitwise_xor(p[0], j) + p[1], i, 0)

    def r_map(j):
        return lambda i, p: (jnp.bitwise_xor(p[0], j), i, 0)

    grid_spec = pltpu.PrefetchScalarGridSpec(
        num_scalar_prefetch=1, grid=(R // tr,),
        in_specs=[pl.BlockSpec((1, tr, C), g_map(j)) for j in range(4)]
        + [pl.BlockSpec((1, tr, C), r_map(j)) for j in range(4)],
        out_specs=[pl.BlockSpec((tr, C), lambda i, p: (i, 0)), pl.BlockSpec((3, tr, C), lambda i, p: (0, i, 0))])
    return pl.pallas_call(
        body, name=name, grid_spec=grid_spec,
        out_shape=[jax.ShapeDtypeStruct((R, C), F32), jax.ShapeDtypeStruct((3, R, C), _MXU)],
    )(place, *([grad] * 4), *([recv] * 4))


def _adamw_math(w, g, m, v):
    m = ADAM_B1 * m + (1.0 - ADAM_B1) * g
    v = ADAM_B2 * v + (1.0 - ADAM_B2) * (g * g)
    m_hat = m / (1.0 - ADAM_B1 ** ADAM_STEP)
    v_hat = v / (1.0 - ADAM_B2 ** ADAM_STEP)
    delta = -ADAM_LR * (m_hat / (jnp.sqrt(v_hat) + ADAM_EPS) + ADAM_WD * w)
    return delta, m, v


def _adamw(w, g, m, v, name):
    R, C = w.shape
    tr = _blk(R, 256)

    def body(w_ref, g_ref, m_ref, v_ref, d_ref, nm_ref, nv_ref):
        d, nm, nv = _adamw_math(w_ref[...], g_ref[...], m_ref[...], v_ref[...])
        d_ref[...] = d
        nm_ref[...] = nm
        nv_ref[...] = nv

    spec = pl.BlockSpec((tr, C), lambda i: (i, 0))
    return pl.pallas_call(
        body, name=name, grid=(R // tr,), in_specs=[spec] * 4, out_specs=[spec] * 3,
        out_shape=[jax.ShapeDtypeStruct((R, C), F32)] * 3,
    )(w, g, m, v)


def _rs_sum_adamw(owns, recvs, w, m, v, name):
    L, R, C = w.shape
    tr = _blk(R, 256)
    nr = R // tr

    def body(o0, o1, r0, r1, w_ref, m_ref, v_ref, g_ref, d_ref, nm_ref, nv_ref):
        def step(o_ref, r_ref):
            g = o_ref[...]
            for j in range(3):
                g = g + r_ref[j].astype(F32)
            d, nm, nv = _adamw_math(w_ref[0], g, m_ref[0], v_ref[0])
            g_ref[0] = g
            d_ref[0] = d
            nm_ref[0] = nm
            nv_ref[0] = nv

        pl.when(pl.program_id(0) == 0)(lambda: step(o0, r0))
        pl.when(pl.program_id(0) == 1)(lambda: step(o1, r1))

    def hold(layer):
        if layer == 0:
            return lambda l, i: i * (1 - l) + (nr - 1) * l
        return lambda l, i: i * l

    own_spec = [pl.BlockSpec((tr, C), functools.partial(lambda l, i, f: (f(l, i), 0), f=hold(k))) for k in range(2)]
    recv_spec = [pl.BlockSpec((3, tr, C), functools.partial(lambda l, i, f: (0, f(l, i), 0), f=hold(k)))
                 for k in range(2)]
    lay = pl.BlockSpec((1, tr, C), lambda l, i: (l, i, 0))
    return pl.pallas_call(
        body, name=name, grid=(L, nr),
        in_specs=own_spec + recv_spec + [lay] * 3, out_specs=[lay] * 4,
        out_shape=[jax.ShapeDtypeStruct((L, R, C), F32)] * 4,
    )(owns[0], owns[1], recvs[0], recvs[1], w, m, v)


def _silu(x):
    return x / (1.0 + jnp.exp(-x))


def _mod_partial(c_all, w_ada, b_cols, name):
    L, D, Ca = w_ada.shape

    def body(c_ref, w_ref, b_ref, o_ref):
        act = _silu(c_ref[...]).astype(_MXU)
        for l in range(L):
            o_ref[:, l * Ca:(l + 1) * Ca] = _nn(act, w_ref[l].astype(_MXU)) + b_ref[l:l + 1, :]

    return pl.pallas_call(
        body, name=name, out_shape=jax.ShapeDtypeStruct((NDEV, L * Ca), F32),
        in_specs=[VMEM_SPEC] * 3, out_specs=VMEM_SPEC,
    )(c_all, w_ada, b_cols)


def _w_ada_grad(c_t, dmod_cols, name):
    L, _, Ca = dmod_cols.shape
    D = c_t.shape[0]

    def body(c_ref, d_ref, o_ref):
        act = _silu(c_ref[...]).astype(_MXU)
        for l in range(L):
            o_ref[l] = _nn(act, d_ref[l].astype(_MXU))

    return pl.pallas_call(
        body, name=name, out_shape=jax.ShapeDtypeStruct((L, D, Ca), F32),
        in_specs=[VMEM_SPEC] * 2, out_specs=VMEM_SPEC,
    )(c_t, dmod_cols)


def _sum_devices(gathered, name):
    _, _, N = gathered.shape

    def body(x_ref, o_ref):
        acc = x_ref[0]
        for d in range(1, NDEV):
            acc = acc + x_ref[d]
        o_ref[...] = acc

    return pl.pallas_call(
        body, name=name, out_shape=jax.ShapeDtypeStruct((1, N), F32),
        in_specs=[VMEM_SPEC], out_specs=VMEM_SPEC,
    )(gathered)


def _ln_mod_matmul(x, g, sc, sh, w, name):
    S, D = x.shape
    N = w.shape[1]
    tm = _blk(S, ROW_BLOCK)

    def body(x_ref, g_ref, sc_ref, sh_ref, w_ref, o_ref, h_ref):
        xv = x_ref[...]
        r = lax.rsqrt(jnp.mean(xv * xv, axis=-1, keepdims=True) + EPS)
        hv = ((xv * r) * g_ref[...]) * (1.0 + sc_ref[...]) + sh_ref[...]
        hb = hv.astype(_MXU)
        h_ref[...] = hb
        o_ref[...] = _nn(hb, w_ref[...]).astype(o_ref.dtype)

    vec = pl.BlockSpec((1, D), lambda i: (0, 0))
    row = lambda width: pl.BlockSpec((tm, width), lambda i: (i, 0))
    return pl.pallas_call(
        body, name=name, grid=(S // tm,),
        in_specs=[row(D), vec, vec, vec, pl.BlockSpec((D, N), lambda i: (0, 0))],
        out_specs=[row(N), row(D)],
        out_shape=[jax.ShapeDtypeStruct((S, N), _MXU), jax.ShapeDtypeStruct((S, D), _MXU)],
    )(x, g, sc, sh, w)


def _matmul_res_gate(a, w, xres, gt, relu2, name):
    S, K = a.shape
    N = w.shape[1]
    tm = _blk(S, 512)

    def body(a_ref, w_ref, x_ref, gt_ref, o_ref, f_ref):
        av = a_ref[...]
        if relu2:
            af = jnp.maximum(av.astype(F32), 0.0)
            av = (af * af).astype(_MXU)
        f = _nn(av, w_ref[...])
        f_ref[...] = f.astype(f_ref.dtype)
        o_ref[...] = x_ref[...] + gt_ref[...] * f

    row = lambda width: pl.BlockSpec((tm, width), lambda i: (i, 0))
    return pl.pallas_call(
        body, name=name, grid=(S // tm,),
        in_specs=[row(K), pl.BlockSpec((K, N), lambda i: (0, 0)), row(N), pl.BlockSpec((1, N), lambda i: (0, 0))],
        out_specs=[row(N), row(N)],
        out_shape=[jax.ShapeDtypeStruct((S, N), F32), jax.ShapeDtypeStruct((S, N), _MXU)],
    )(a, w, xres, gt)


def _loss_grad(y, t, name):
    S, D = y.shape
    tm = _blk(S, 512)
    last = S // tm - 1

    def body(y_ref, t_ref, dy_ref, l_ref, acc_ref):
        i = pl.program_id(0)
        e = y_ref[...] - t_ref[...]
        dy_ref[...] = e * (1.0 / D)
        part = jnp.sum(e * e, axis=0, keepdims=True)

        @pl.when(i == 0)
        def _():
            acc_ref[...] = part

        @pl.when(i > 0)
        def _():
            acc_ref[...] += part

        @pl.when(i == last)
        def _():
            l_ref[...] = (0.5 / D) * jnp.sum(acc_ref[...], axis=1, keepdims=True)

    row = pl.BlockSpec((tm, D), lambda i: (i, 0))
    return pl.pallas_call(
        body, name=name, grid=(S // tm,), in_specs=[row, row],
        out_specs=[row, pl.BlockSpec((1, 1), lambda i: (0, 0))],
        out_shape=[jax.ShapeDtypeStruct((S, D), F32), jax.ShapeDtypeStruct((1, 1), F32)],
        scratch_shapes=[pltpu.VMEM((1, D), F32)],
    )(y, t)


def _accumulate(ref, part, first):
    @pl.when(first)
    def _():
        ref[...] = part

    @pl.when(jnp.logical_not(first))
    def _():
        ref[...] += part


def _gate_nt_matmul(dx, f, gt, w, u, name):
    S, D = dx.shape
    N = w.shape[0]
    tm = _blk(S, ROW_BLOCK if N <= D else ROW_BLOCK // 2)
    with_u = u is not None

    def body(*refs):
        if with_u:
            dx_ref, f_ref, gt_ref, w_ref, u_ref, dz_ref, dgt_ref, res_ref = refs
        else:
            dx_ref, f_ref, gt_ref, w_ref, dz_ref, dgt_ref, res_ref = refs
        dxv = dx_ref[...]
        dz = (dxv * gt_ref[...]).astype(_MXU)
        dz_ref[...] = dz
        _accumulate(dgt_ref, jnp.sum(dxv * f_ref[...].astype(F32), axis=0, keepdims=True), pl.program_id(0) == 0)
        r = _nt(dz, w_ref[...])
        if with_u:
            r = r * (2.0 * jnp.maximum(u_ref[...].astype(F32), 0.0))
        res_ref[...] = r.astype(res_ref.dtype)

    row = lambda width: pl.BlockSpec((tm, width), lambda i: (i, 0))
    in_specs = [row(D), row(D), pl.BlockSpec((1, D), lambda i: (0, 0)), pl.BlockSpec((N, D), lambda i: (0, 0))]
    args = [dx, f, gt, w]
    if with_u:
        in_specs.append(row(N))
        args.append(u)
    return pl.pallas_call(
        body, name=name, grid=(S // tm,), in_specs=in_specs,
        out_specs=[row(D), pl.BlockSpec((1, D), lambda i: (0, 0)), row(N)],
        out_shape=[jax.ShapeDtypeStruct((S, D), _MXU), jax.ShapeDtypeStruct((1, D), F32),
                   jax.ShapeDtypeStruct((S, N), _MXU)],
    )(*args)


def _tn_matmul(a, b, by_col, relu2, name):
    S, Ka = a.shape
    Nb = b.shape[1]
    ts = _blk(S, 2 * ROW_BLOCK)
    half = NDEV // 2
    if by_col:
        R, C = Ka, Nb // NDEV
        a_spec = pl.BlockSpec((ts, Ka), lambda h, k: (k, 0))
        b_spec = pl.BlockSpec((ts, half * C), lambda h, k: (k, h))
    else:
        R, C = Ka // NDEV, Nb
        a_spec = pl.BlockSpec((ts, half * R), lambda h, k: (k, h))
        b_spec = pl.BlockSpec((ts, Nb), lambda h, k: (k, 0))

    def body(a_ref, b_ref, o_ref):
        av = a_ref[...]
        if relu2:
            af = jnp.maximum(av.astype(F32), 0.0)
            av = (af * af).astype(_MXU)
        p = _tn(av, b_ref[...])
        first = pl.program_id(1) == 0
        for d in range(half):
            part = p[:, d * C:(d + 1) * C] if by_col else p[d * R:(d + 1) * R, :]
            _accumulate(o_ref.at[d], part, first)

    return pl.pallas_call(
        body, name=name, grid=(NDEV // half, S // ts), in_specs=[a_spec, b_spec],
        out_specs=pl.BlockSpec((half, R, C), lambda h, k: (h, 0, 0)),
        out_shape=jax.ShapeDtypeStruct((NDEV, R, C), F32),
    )(a, b)


def _nt_ln_bwd(dy, w, x, g, sc, sh, dxres, name):
    S, D = x.shape
    N = w.shape[1]
    tm = _blk(S, ROW_BLOCK)

    def body(dy_ref, w_ref, x_ref, g_ref, sc_ref, sh_ref, dxr_ref, dx_ref, dsh_ref, dsc_ref, dg_ref):
        dh = _nt(dy_ref[...], w_ref[...])
        xv = x_ref[...]
        r = lax.rsqrt(jnp.mean(xv * xv, axis=-1, keepdims=True) + EPS)
        xhat = xv * r
        gv = g_ref[...]
        dn = dh * (1.0 + sc_ref[...])
        dxhat = dn * gv
        dxv = r * (dxhat - xhat * jnp.mean(dxhat * xhat, axis=-1, keepdims=True))
        dx_ref[...] = dxr_ref[...] + dxv
        first = pl.program_id(0) == 0
        _accumulate(dsh_ref, jnp.sum(dh, axis=0, keepdims=True), first)
        _accumulate(dsc_ref, jnp.sum(dh * (xhat * gv), axis=0, keepdims=True), first)
        _accumulate(dg_ref, jnp.sum(dn * xhat, axis=0, keepdims=True), first)

    row = lambda width: pl.BlockSpec((tm, width), lambda i: (i, 0))
    vec = pl.BlockSpec((1, D), lambda i: (0, 0))
    return pl.pallas_call(
        body, name=name, grid=(S // tm,),
        in_specs=[row(N), pl.BlockSpec((D, N), lambda i: (0, 0)), row(D), vec, vec, vec, row(D)],
        out_specs=[row(D), vec, vec, vec],
        out_shape=[jax.ShapeDtypeStruct((S, D), F32)] + [jax.ShapeDtypeStruct((1, D), F32)] * 3,
    )(dy, w, x, g, sc, sh, dxres)


def _split2(v):
    hi = v.astype(_MXU)
    mid = (v - hi.astype(F32)).astype(_MXU)
    return hi, mid


def _tri_sums(v, tri2):
    T = v.shape[0]
    hi, mid = _split2(v)
    s = _nn(hi, tri2) + _nn(mid, tri2)
    return s[:, :T], s[:, T:]


def _tri2(T, inclusive):
    j = lax.broadcasted_iota(jnp.int32, (T, 2 * T), 0)
    s = lax.broadcasted_iota(jnp.int32, (T, 2 * T), 1)
    keep = (j >= s) if inclusive else (j > s)
    return jnp.where((s >= T) | keep, 1.0, 0.0).astype(_MXU)


def _log_sigmoid(z):
    return jnp.minimum(z, 0.0) - jnp.log(1.0 + jnp.exp(-jnp.abs(z)))


def _sb_blocks(qbs, kblks, strict, tri2, carry):
    scale = HEAD_DIM ** -0.5
    zs = [_nt(qb, kblk) * scale for qb, kblk in zip(qbs, kblks)]
    lbs, sums = [], []
    for z in zs:
        lb = _log_sigmoid(z)
        l1 = lb - z
        if strict is not None:
            l1 = jnp.where(strict, l1, 0.0)
        lbs.append(lb)
        sums.append(_tri_sums(l1, tri2))
    amps, new_carry = [], []
    for lb, (sfx, tot), c in zip(lbs, sums, carry):
        a = jnp.exp(lb + sfx + c)
        if strict is not None:
            a = jnp.where(strict, a, 0.0)
        amps.append(a)
        new_carry.append(c + tot)
    return lbs, amps, new_carry


def _sb_alive(carry):
    top = carry[0]
    for c in carry[1:]:
        top = jnp.maximum(top, c)
    return jnp.max(top) > SB_SKIP


def _skew_index():
    i = lax.broadcasted_iota(jnp.int32, (CA_T, SKEW_W + 1), 0)
    m = lax.broadcasted_iota(jnp.int32, (CA_T, SKEW_W + 1), 1)
    wrapped = i + m >= SKEW_W
    row = jnp.where(wrapped, i + 1, i)
    j = jnp.where(wrapped, i + m - SKEW_W, i + m)
    a = row // CHUNK
    jj = j - a * CHUNK
    inband = (jj >= 0) & (jj < BAND) & (j < CA_W) & (row < CA_T)
    idx = jnp.clip((row - a * CHUNK) + PAD - jj, -REL_CLIP, REL_CLIP) + REL_CLIP
    return inband, idx, wrapped


def _skew(tile):
    H = tile.shape[0]
    flat = jnp.pad(tile, ((0, 0), (0, 0), (0, SKEW_W - CA_W))).reshape(H, CA_T * SKEW_W)
    return jnp.pad(flat, ((0, 0), (0, CA_T))).reshape(H, CA_T, SKEW_W + 1)


def _unskew(view):
    H = view.shape[0]
    flat = view.reshape(H, CA_T * (SKEW_W + 1))[:, :CA_T * SKEW_W]
    return flat.reshape(H, CA_T, SKEW_W)[:, :, :CA_W]


def _ca_bias(rel_bias, name):
    H = rel_bias.shape[0]
    top = rel_bias[:, N_REL - 1:]
    by_offset = jnp.concatenate(
        [jnp.broadcast_to(top, (H, PAD - REL_CLIP + 1)), jnp.flip(rel_bias[:, :N_REL - 1], axis=1),
         jnp.broadcast_to(top, (H, SKEW_W + 1 - (PAD - REL_CLIP + 1) - (N_REL - 1)))], axis=1)

    def body(t_ref, o_ref):
        inband, _, wrapped = _skew_index()
        vals = jnp.where(wrapped, t_ref[0][:, 0:1], t_ref[0])
        o_ref[0] = jnp.where(inband, vals, NEG)

    view = pl.pallas_call(
        body, name=name, grid=(H,), in_specs=[pl.BlockSpec((1, 1, SKEW_W + 1), lambda h: (h, 0, 0))],
        out_specs=pl.BlockSpec((1, CA_T, SKEW_W + 1), lambda h: (h, 0, 0)),
        out_shape=jax.ShapeDtypeStruct((H, CA_T, SKEW_W + 1), F32),
    )(by_offset.reshape(H, 1, SKEW_W + 1))
    return _unskew(view)


def _ca_bias_bwd(dbias, name):
    H = dbias.shape[0]

    def body(d_ref, o_ref):
        inband, idx, _ = _skew_index()
        d = jnp.where(inband, d_ref[0], 0.0)
        clipped = idx == N_REL - 1
        by_offset = jnp.sum(jnp.where(clipped, 0.0, d), axis=0, keepdims=True)
        top = jnp.sum(jnp.sum(jnp.where(clipped, d, 0.0), axis=0, keepdims=True), axis=1, keepdims=True)
        lane = lax.broadcasted_iota(jnp.int32, (1, SKEW_W + 1), 1)
        o_ref[0] = jnp.where(lane == 0, top, by_offset)

    out = pl.pallas_call(
        body, name=name, grid=(H,), in_specs=[pl.BlockSpec((1, CA_T, SKEW_W + 1), lambda h: (h, 0, 0))],
        out_specs=pl.BlockSpec((1, 1, SKEW_W + 1), lambda h: (h, 0, 0)),
        out_shape=jax.ShapeDtypeStruct((H, 1, SKEW_W + 1), F32),
    )(_skew(dbias))[:, 0]
    first = PAD - REL_CLIP + 1
    return jnp.concatenate([jnp.flip(out[:, first:first + N_REL - 1], axis=1), out[:, 0:1]], axis=1)


def _low_lanes(rows):
    return lax.broadcasted_iota(jnp.int32, (rows, PAIR), 1) < HEAD_DIM


def _one_head(t2, low, first):
    tf = t2.astype(F32)
    return (jnp.where(low, tf, 0.0) if first else jnp.where(low, 0.0, tf)).astype(_MXU)


def _sb_fwd(proj, name, ex=None):
    S, W = proj.shape
    half = W // 6
    npair = half // PAIR
    T = _blk(S, SB_T)
    GP = _blk(npair, SB_PAIRS)
    GW = GP * PAIR
    nb = npair // GP

    def body(q_ref, k_ref, v_ref, o_ref, ox_ref):
        qi = pl.program_id(1)
        low = _low_lanes(T)
        qbs = []
        for j in range(GP):
            q2 = q_ref[:, j * PAIR:(j + 1) * PAIR]
            qbs += [_one_head(q2, low, True), _one_head(q2, low, False)]
        row = lax.broadcasted_iota(jnp.int32, (T, T), 0)
        col = lax.broadcasted_iota(jnp.int32, (T, T), 1)
        tri2 = _tri2(T, inclusive=False)

        def pairs(kb, carry, acc, fine, strict):
            rows = pl.ds(pl.multiple_of(kb * T, T), T)
            k2 = [k_ref[rows, j * PAIR:(j + 1) * PAIR] for j in range(GP)]
            v2 = [v_ref[rows, j * PAIR:(j + 1) * PAIR] for j in range(GP)]
            _, amps, carry = _sb_blocks(qbs, [k2[h // 2] for h in range(2 * GP)], strict, tri2, carry)
            parts = [_split2(a) for a in amps]
            hi = [_nn(parts[h][0], v2[h // 2]) for h in range(2 * GP)]
            mid = [_nn(parts[h][1], v2[h // 2]) for h in range(2 * GP)]
            acc = tuple(acc[j] + jnp.where(low, hi[2 * j], hi[2 * j + 1]) for j in range(GP))
            fine = tuple(fine[j] + jnp.where(low, mid[2 * j], mid[2 * j + 1]) for j in range(GP))
            return tuple(carry), acc, fine

        zero = (jnp.zeros((T, PAIR), F32),) * GP
        carry, acc, fine = pairs(qi, (jnp.zeros((T, T), F32),) * (2 * GP), zero, zero, col < row)

        def cond(st):
            kb, carry, _, _ = st
            return jnp.logical_and(kb >= 0, _sb_alive(carry))

        def step(st):
            kb, carry, acc, fine = st
            carry, acc, fine = pairs(kb, carry, acc, fine, None)
            return kb - 1, carry, acc, fine

        _, _, acc, fine = lax.while_loop(cond, step, (qi - 1, carry, acc, fine))
        for j in range(GP):
            o_ref[:, j * PAIR:(j + 1) * PAIR] = acc[j].astype(o_ref.dtype)
            ox_ref[:, j * PAIR:(j + 1) * PAIR] = acc[j] + fine[j]

    blk = pl.BlockSpec((T, GW), lambda p, i: (i, p))
    return _call_hosted(
        body, name, (nb, S // T),
        [blk, pl.BlockSpec((S, GW), lambda p, i: (0, nb + p)), pl.BlockSpec((S, GW), lambda p, i: (0, 2 * nb + p))],
        [blk, blk], [jax.ShapeDtypeStruct((S, half), _MXU), jax.ShapeDtypeStruct((S, half), F32)],
        [], [proj, proj, proj], ex)


def _sb_bwd(proj, ox, dmixed, name, ex=None):
    S, W = proj.shape
    half = W // 6
    npair = half // PAIR
    T = _blk(S, SB_T)
    GP = _blk(npair, SB_PAIRS)
    GW = GP * PAIR
    nb = npair // GP
    last = S // T - 1
    scale = HEAD_DIM ** -0.5

    def body(q_ref, k_ref, v_ref, ox_ref, do_ref, dq_ref, dk_ref, dv_ref, dka_ref, dva_ref):
        qi = pl.program_id(1)

        @pl.when(qi == 0)
        def _():
            dka_ref[...] = jnp.zeros_like(dka_ref)
            dva_ref[...] = jnp.zeros_like(dva_ref)

        low = _low_lanes(T)
        q2, do2, qbs, dobs, deltas = [], [], [], [], []
        for j in range(GP):
            cols = slice(j * PAIR, (j + 1) * PAIR)
            q2.append(q_ref[:, cols])
            do2.append(do_ref[:, cols])
            qbs += [_one_head(q2[j], low, True), _one_head(q2[j], low, False)]
            dobs += [_one_head(do2[j], low, True), _one_head(do2[j], low, False)]
            for e in range(2):
                deltas.append(jnp.sum(dobs[2 * j + e].astype(F32) * ox_ref[:, cols], axis=-1, keepdims=True))
        row = lax.broadcasted_iota(jnp.int32, (T, T), 0)
        col = lax.broadcasted_iota(jnp.int32, (T, T), 1)
        tri_ex = _tri2(T, inclusive=False)
        tri_in = _tri2(T, inclusive=True)

        def pairs(kb, carry, right, dq, strict):
            rows = pl.ds(pl.multiple_of(kb * T, T), T)
            k2 = [k_ref[rows, j * PAIR:(j + 1) * PAIR] for j in range(GP)]
            v2 = [v_ref[rows, j * PAIR:(j + 1) * PAIR] for j in range(GP)]
            nh = 2 * GP
            gs = [_nt(dobs[h], v2[h // 2]) for h in range(nh)]
            lbs, amps, carry = _sb_blocks(qbs, [k2[h // 2] for h in range(nh)], strict, tri_ex, carry)
            ags = [a * gg for a, gg in zip(amps, gs)]
            sums = [_tri_sums(ag, tri_in) for ag in ags]
            dzbs = []
            for h in range(nh):
                left = deltas[h] - (sums[h][0] + right[h])
                beta = jnp.exp(lbs[h])
                dz = ags[h] * (1.0 - beta) - beta * left
                if strict is not None:
                    dz = jnp.where(strict, dz, 0.0)
                dzbs.append((dz * scale).astype(_MXU))
            abs_ = [a.astype(_MXU) for a in amps]
            dks = [_tn(dzbs[h], q2[h // 2]) for h in range(nh)]
            dvs = [_tn(abs_[h], do2[h // 2]) for h in range(nh)]
            dqs = [_nn(dzbs[h], k2[h // 2]) for h in range(nh)]
            for j in range(GP):
                cols = slice(j * PAIR, (j + 1) * PAIR)
                dka_ref[rows, cols] += jnp.where(low, dks[2 * j], dks[2 * j + 1])
                dva_ref[rows, cols] += jnp.where(low, dvs[2 * j], dvs[2 * j + 1])
            right = tuple(right[h] + sums[h][1] for h in range(nh))
            dq = tuple(dq[j] + jnp.where(low, dqs[2 * j], dqs[2 * j + 1]) for j in range(GP))
            return tuple(carry), right, dq

        zero = (jnp.zeros((T, T), F32),) * (2 * GP)
        carry, right, dq = pairs(qi, zero, zero, (jnp.zeros((T, PAIR), F32),) * GP, col < row)

        def cond(st):
            kb, carry, _, _ = st
            return jnp.logical_and(kb >= 0, _sb_alive(carry))

        def step(st):
            kb, carry, right, dq = st
            carry, right, dq = pairs(kb, carry, right, dq, None)
            return kb - 1, carry, right, dq

        _, _, _, dq = lax.while_loop(cond, step, (qi - 1, carry, right, dq))
        for j in range(GP):
            dq_ref[:, j * PAIR:(j + 1) * PAIR] = dq[j].astype(dq_ref.dtype)

        @pl.when(qi == last)
        def _():
            dk_ref[...] = dka_ref[...].astype(dk_ref.dtype)
            dv_ref[...] = dva_ref[...].astype(dv_ref.dtype)

    blk = pl.BlockSpec((T, GW), lambda p, i: (i, p))
    full = pl.BlockSpec((S, GW), lambda p, i: (0, p))
    return _call_hosted(
        body, name, (nb, S // T),
        [blk, pl.BlockSpec((S, GW), lambda p, i: (0, nb + p)), pl.BlockSpec((S, GW), lambda p, i: (0, 2 * nb + p)),
         blk, blk],
        [blk, full, full], [jax.ShapeDtypeStruct((S, half), _MXU)] * 3,
        [pltpu.VMEM((S, GW), F32), pltpu.VMEM((S, GW), F32)], [proj, proj, proj, ox, dmixed], ex)


def _pair_norm(t2, g2, low):
    tf = t2.astype(F32)
    sq = tf * tf
    both = jnp.sum(sq, axis=-1, keepdims=True)
    first = jnp.sum(jnp.where(low, sq, 0.0), axis=-1, keepdims=True)
    r = jnp.where(low, lax.rsqrt(first * (1.0 / HEAD_DIM) + EPS), lax.rsqrt((both - first) * (1.0 / HEAD_DIM) + EPS))
    hat = tf * r
    return hat * g2, hat, r


def _pair_norm_bwd(dn, hat, r, g2, low):
    dhat = dn * g2
    prod = dhat * hat
    both = jnp.sum(prod, axis=-1, keepdims=True)
    first = jnp.sum(jnp.where(low, prod, 0.0), axis=-1, keepdims=True)
    mean = jnp.where(low, first, both - first) * (1.0 / HEAD_DIM)
    return r * (dhat - hat * mean)


def _ca_fill(j, k_ref, v_ref, gk_ref, kn_ref, vp_ref):
    S = k_ref.shape[0]
    cols = slice(j * PAIR, (j + 1) * PAIR)
    kn, _, _ = _pair_norm(k_ref[:, cols], gk_ref[...], _low_lanes(S))
    kn_ref[j, 0:PAD, :] = jnp.zeros((PAD, PAIR), kn_ref.dtype)
    vp_ref[j, 0:PAD, :] = jnp.zeros((PAD, PAIR), vp_ref.dtype)
    kn_ref[j, PAD:PAD + S, :] = kn.astype(kn_ref.dtype)
    vp_ref[j, PAD:PAD + S, :] = v_ref[:, cols]


def _ca_scores(j, q_ref, b2_ref, gq_ref, kn_ref, qi, low):
    qn, qhat, r = _pair_norm(q_ref[:, j * PAIR:(j + 1) * PAIR], gq_ref[...], low)
    qn = qn * HEAD_DIM ** -0.5
    band = pl.ds(pl.multiple_of(qi * CA_T, CA_T), CA_W)
    key_pos = qi * CA_T - PAD + lax.broadcasted_iota(jnp.int32, (CA_T, CA_W), 1)
    scores = []
    for e in range(2):
        s = _nt(_one_head(qn, low, e == 0), kn_ref[j, band, :]) + b2_ref[2 * j + e]
        scores.append(jnp.where(key_pos >= 0, s, NEG))
    return scores, qn.astype(_MXU), qhat, r


def _softmax(s):
    e = jnp.exp(s - jnp.max(s, axis=-1, keepdims=True))
    return e * (1.0 / jnp.sum(e, axis=-1, keepdims=True))


def _ca_fwd(proj, bias2, gq2, gk2, name, ex=None):
    S, W = proj.shape
    half = W // 6
    npair = half // PAIR
    GP = _blk(npair, CA_PAIRS_FWD)
    GW = GP * PAIR
    nb = npair // GP

    def body(q_ref, k_ref, v_ref, b2_ref, gq_ref, gk_ref, o_ref, kn_ref, vp_ref):
        qi = pl.program_id(1)

        @pl.when(qi == 0)
        def _():
            for j in range(GP):
                _ca_fill(j, k_ref, v_ref, gk_ref, kn_ref, vp_ref)

        low = _low_lanes(CA_T)
        band = pl.ds(pl.multiple_of(qi * CA_T, CA_T), CA_W)
        scores = [_ca_scores(j, q_ref, b2_ref, gq_ref, kn_ref, qi, low)[0] for j in range(GP)]
        probs = [[_softmax(s).astype(_MXU) for s in pair] for pair in scores]
        for j in range(GP):
            outs = [_nn(probs[j][e], vp_ref[j, band, :]) for e in range(2)]
            o_ref[:, j * PAIR:(j + 1) * PAIR] = jnp.where(low, outs[0], outs[1]).astype(o_ref.dtype)

    vec = pl.BlockSpec((1, PAIR), lambda p, i: (0, 0))
    return _call_hosted(
        body, name, (nb, S // CA_T),
        [pl.BlockSpec((CA_T, GW), lambda p, i: (i, 3 * nb + p)),
         pl.BlockSpec((S, GW), lambda p, i: (0, 4 * nb + p)), pl.BlockSpec((S, GW), lambda p, i: (0, 5 * nb + p)),
         pl.BlockSpec((2 * GP, CA_T, CA_W), lambda p, i: (p, 0, 0)), vec, vec],
        [pl.BlockSpec((CA_T, GW), lambda p, i: (i, p))], [jax.ShapeDtypeStruct((S, half), _MXU)],
        [pltpu.VMEM((GP, PAD + S, PAIR), _MXU), pltpu.VMEM((GP, PAD + S, PAIR), _MXU)],
        [proj, proj, proj, bias2, gq2, gk2], ex)


def _ca_bwd(proj, bias2, gq2, gk2, dmixed, name, ex=None):
    S, W = proj.shape
    half = W // 6
    npair = half // PAIR
    GP = _blk(npair, CA_PAIRS_BWD)
    GW = GP * PAIR
    nb = npair // GP
    scale = HEAD_DIM ** -0.5
    last = S // CA_T - 1

    def body(q_ref, k_ref, v_ref, b2_ref, gq_ref, gk_ref, do_ref,
             dq_ref, dk_ref, dv_ref, db_ref, dgq_ref, dgk_ref, kn_ref, vp_ref, dkn_ref, dvp_ref):
        p_id, qi = pl.program_id(0), pl.program_id(1)

        @pl.when(qi == 0)
        def _():
            for j in range(GP):
                _ca_fill(j, k_ref, v_ref, gk_ref, kn_ref, vp_ref)
            dkn_ref[...] = jnp.zeros_like(dkn_ref)
            dvp_ref[...] = jnp.zeros_like(dvp_ref)
            db_ref[...] = jnp.zeros_like(db_ref)

        @pl.when(jnp.logical_and(p_id == 0, qi == 0))
        def _():
            dgq_ref[...] = jnp.zeros_like(dgq_ref)
            dgk_ref[...] = jnp.zeros_like(dgk_ref)

        low = _low_lanes(CA_T)
        top_w = lax.broadcasted_iota(jnp.int32, (PAIR, CA_W), 0) < HEAD_DIM
        band = pl.ds(pl.multiple_of(qi * CA_T, CA_T), CA_W)
        pairs = [_ca_scores(j, q_ref, b2_ref, gq_ref, kn_ref, qi, low) for j in range(GP)]
        do2 = [do_ref[:, j * PAIR:(j + 1) * PAIR] for j in range(GP)]
        dps = [[_nt(_one_head(do2[j], low, e == 0), vp_ref[j, band, :]) for e in range(2)] for j in range(GP)]
        probs, dsbs = [], []
        for j in range(GP):
            pj, dj = [], []
            for e in range(2):
                p = _softmax(pairs[j][0][e])
                ds = p * (dps[j][e] - jnp.sum(p * dps[j][e], axis=-1, keepdims=True))
                db_ref[2 * j + e] += ds
                pj.append(p.astype(_MXU))
                dj.append(ds.astype(_MXU))
            probs.append(pj)
            dsbs.append(dj)
        dgq = jnp.zeros((1, PAIR), F32)
        for j in range(GP):
            _, qn, qhat, r = pairs[j]
            dq_h = [_nn(dsbs[j][e], kn_ref[j, band, :]) for e in range(2)]
            dk_h = [_tn(qn, dsbs[j][e]) for e in range(2)]
            dv_h = [_tn(do2[j], probs[j][e]) for e in range(2)]
            dkn_ref[j, :, band] += jnp.where(top_w, dk_h[0], dk_h[1])
            dvp_ref[j, :, band] += jnp.where(top_w, dv_h[0], dv_h[1])
            dqn = jnp.where(low, dq_h[0], dq_h[1]) * scale
            dgq = dgq + jnp.sum(dqn * qhat, axis=0, keepdims=True)
            dq_ref[:, j * PAIR:(j + 1) * PAIR] = _pair_norm_bwd(dqn, qhat, r, gq_ref[...], low).astype(dq_ref.dtype)
        dgq_ref[...] += dgq

        @pl.when(qi == last)
        def _():
            low_s = _low_lanes(S)
            for j in range(GP):
                cols = slice(j * PAIR, (j + 1) * PAIR)
                _, khat, rk = _pair_norm(k_ref[:, cols], gk_ref[...], low_s)
                dkn = dkn_ref[j, :, PAD:PAD + S].T
                dgk_ref[...] += jnp.sum(dkn * khat, axis=0, keepdims=True)
                dk_ref[:, cols] = _pair_norm_bwd(dkn, khat, rk, gk_ref[...], low_s).astype(dk_ref.dtype)
                dv_ref[:, cols] = dvp_ref[j, :, PAD:PAD + S].T.astype(dv_ref.dtype)

    vec = pl.BlockSpec((1, PAIR), lambda p, i: (0, 0))
    tile = pl.BlockSpec((2 * GP, CA_T, CA_W), lambda p, i: (p, 0, 0))
    full = pl.BlockSpec((S, GW), lambda p, i: (0, p))
    return _call_hosted(
        body, name, (nb, S // CA_T),
        [pl.BlockSpec((CA_T, GW), lambda p, i: (i, 3 * nb + p)),
         pl.BlockSpec((S, GW), lambda p, i: (0, 4 * nb + p)), pl.BlockSpec((S, GW), lambda p, i: (0, 5 * nb + p)),
         tile, vec, vec, pl.BlockSpec((CA_T, GW), lambda p, i: (i, nb + p))],
        [pl.BlockSpec((CA_T, GW), lambda p, i: (i, p)), full, full, tile, vec, vec],
        [jax.ShapeDtypeStruct((S, half), _MXU)] * 3
        + [jax.ShapeDtypeStruct(bias2.shape, F32), jax.ShapeDtypeStruct((1, PAIR), F32),
           jax.ShapeDtypeStruct((1, PAIR), F32)],
        [pltpu.VMEM((GP, PAD + S, PAIR), _MXU), pltpu.VMEM((GP, PAD + S, PAIR), _MXU),
         pltpu.VMEM((GP, PAIR, PAD + S), F32), pltpu.VMEM((GP, PAIR, PAD + S), F32)],
        [proj, proj, proj, bias2, gq2, gk2, dmixed], ex)


def _pack_small(parts):
    flat = jnp.concatenate([p.reshape(-1) for layer in parts for p in layer])
    n = flat.shape[0]
    n_pad = -(-n // 1024) * 1024
    return jnp.pad(flat, (0, n_pad - n)).reshape(1, n_pad)


def _unpack_small(flat, shapes):
    out, off = [], 0
    for layer in shapes:
        cur = []
        for shp in layer:
            size = 1
            for s in shp:
                size *= s
            cur.append(flat[off:off + size].reshape(shp))
            off += size
        out.append(cur)
    return out


def kernel(x, c, g_norm1, w_in, g_q, g_k, rel_bias, w_o, g_norm2, w1, w2, w_ada, b_ada, loss_target, m_g_norm1, m_w_in, m_g_q, m_g_k, m_rel_bias, m_w_o, m_g_norm2, m_w1, m_w2, m_w_ada, m_b_ada, v_g_norm1, v_w_in, v_g_q, v_g_k, v_rel_bias, v_w_o, v_g_norm2, v_w1, v_w2, v_w_ada, v_b_ada):
    L = w_in.shape[0]
    S, D = x.shape[1:]
    H2 = D // HEAD_DIM // 2
    Ca = w_ada.shape[2]
    xi, yi, ci = _pos()
    me = 4 * xi + 2 * yi + ci
    place = jnp.stack([2 * xi + yi, ci]).astype(jnp.int32)

    c_all = _all_gather_small(c, "ag_c").reshape(NDEV, D)
    b_cols = lax.dynamic_slice(b_ada, (0, me * Ca), (L, Ca))
    mod_part = _mod_partial(c_all, w_ada, b_cols, "mod_partial")
    mod_all = _all_gather_small(mod_part, "ag_mod")
    mod = lax.dynamic_index_in_dim(mod_all, me, axis=1, keepdims=False)
    mod = mod.reshape(NDEV, L, Ca).transpose(1, 0, 2).reshape(L, 6, 1, D)

    wire = lambda a: a.astype(_MXU)
    by_cols = lambda g: g.transpose(1, 0, 2).reshape(D, g.shape[0] * g.shape[2])
    W_in = {0: by_cols(_run_exchange(_gather_exchange([wire(w_in[0])]), "ag_w_in0")[0])}
    W_o, W_1, W_2 = {}, {}, {}

    xs = [x[0]]
    saved = []
    for l in range(L):
        sh1, sc1, gt1, sh2, sc2, gt2 = [mod[l, i] for i in range(6)]
        gn1, gn2 = g_norm1[l:l + 1], g_norm2[l:l + 1]
        gq2, gk2 = jnp.tile(g_q[l:l + 1], (1, 2)), jnp.tile(g_k[l:l + 1], (1, 2))
        proj, h1 = _ln_mod_matmul(xs[-1], gn1, sc1, sh1, W_in[l], f"l{l}_proj")
        (o_sb, ox_sb), got = _sb_fwd(proj, f"l{l}_sb_fwd",
                                     _gather_exchange([wire(w_o[l]), wire(w1[l]), wire(w2[l])]))
        W_o[l], W_1[l], W_2[l] = got[0].reshape(D, D), by_cols(got[1]), got[2].reshape(4 * D, D)
        bias2 = _ca_bias(rel_bias[l], f"l{l}_ca_bias")
        (o_ca,), got = _ca_fwd(proj, bias2, gq2, gk2, f"l{l}_ca_fwd",
                               _gather_exchange([wire(w_in[l + 1])]) if l + 1 < L else None)
        if got:
            W_in[l + 1] = by_cols(got[0])
        mixed = jnp.concatenate([o_sb, o_ca], axis=1)
        x1, f1 = _matmul_res_gate(mixed, W_o[l], xs[-1], gt1, False, f"l{l}_attn_out")
        u, h2 = _ln_mod_matmul(x1, gn2, sc2, sh2, W_1[l], f"l{l}_mlp_in")
        x2, f2 = _matmul_res_gate(u, W_2[l], x1, gt2, True, f"l{l}_mlp_out")
        saved.append(dict(x0=xs[-1], h1=h1, proj=proj, ox_sb=ox_sb, bias2=bias2, mixed=mixed, f1=f1, x1=x1,
                          h2=h2, u=u, f2=f2))
        xs.append(x2)

    dx, loss_part = _loss_grad(xs[-1], loss_target[0], "loss")

    owns, recv_b = {}, {}
    ready = []
    small_parts = [None] * L

    def partials(keys, grads, recv_a):
        parts = []
        for key, g, r in zip(keys, grads, recv_a):
            owns[key], part = _rs_chip_partial(place, g, r, f"rs_partial_l{key[0]}_{key[1]}")
            parts.append(part)
        return parts

    for l in reversed(range(L)):
        sv = saved[l]
        sh1, sc1, gt1, sh2, sc2, gt2 = [mod[l, i] for i in range(6)]
        gn1, gn2 = g_norm1[l:l + 1], g_norm2[l:l + 1]
        gq2, gk2 = jnp.tile(g_q[l:l + 1], (1, 2)), jnp.tile(g_k[l:l + 1], (1, 2))
        dz2, dgt2, du = _gate_nt_matmul(dx, sv["f2"], gt2, W_2[l], sv["u"], f"l{l}_mlp_out_bwd")
        gw2 = _tn_matmul(sv["u"], dz2, False, True, f"l{l}_gw2")
        gw1 = _tn_matmul(sv["h2"], du, True, False, f"l{l}_gw1")
        dx, dsh2, dsc2, dgn2 = _nt_ln_bwd(du, W_1[l], sv["x1"], gn2, sc2, sh2, dx, f"l{l}_mlp_in_bwd")
        dz1, dgt1, dmixed = _gate_nt_matmul(dx, sv["f1"], gt1, W_o[l], None, f"l{l}_attn_out_bwd")
        gwo = _tn_matmul(sv["mixed"], dz1, False, False, f"l{l}_gwo")
        ready += [((l, 1), gwo), ((l, 2), gw1), ((l, 3), gw2)]
        keys, grads = [k for k, _ in ready], [g for _, g in ready]
        (dq_sb, dk_sb, dv_sb), recv_a = _sb_bwd(sv["proj"], sv["ox_sb"], dmixed, f"l{l}_sb_bwd",
                                                _sibling_exchange(grads))
        parts = partials(keys, grads, recv_a)
        (dq_ca, dk_ca, dv_ca, dbias2, dgq2, dgk2), got = _ca_bwd(sv["proj"], sv["bias2"], gq2, gk2, dmixed,
                                                                 f"l{l}_ca_bwd", _chip_exchange(parts))
        recv_b.update(zip(keys, got))
        dgq = dgq2[:, :HEAD_DIM] + dgq2[:, HEAD_DIM:]
        dgk = dgk2[:, :HEAD_DIM] + dgk2[:, HEAD_DIM:]
        drb = _ca_bias_bwd(dbias2, f"l{l}_ca_bias_bwd")
        dproj = jnp.concatenate([dq_sb, dk_sb, dv_sb, dq_ca, dk_ca, dv_ca], axis=1)
        gwin = _tn_matmul(sv["h1"], dproj, True, False, f"l{l}_gwin")
        ready = [((l, 0), gwin)]
        dx, dsh1, dsc1, dgn1 = _nt_ln_bwd(dproj, W_in[l], sv["x0"], gn1, sc1, sh1, dx, f"l{l}_proj_bwd")
        dmod = jnp.concatenate([dsh1, dsc1, dgt1, dsh2, dsc2, dgt2], axis=1)
        small_parts[l] = [dgn1, dgq, dgk, drb, dgn2, dmod]
    grad_x = dx[None]

    keys, grads = [k for k, _ in ready], [g for _, g in ready]
    parts = partials(keys, grads, _run_exchange(_sibling_exchange(grads), "rs_sibling_last"))
    recv_b.update(zip(keys, _run_exchange(_chip_exchange(parts), "rs_chips_last")))
    big_out = []
    for t, (w, m, v) in enumerate([(w_in, m_w_in, v_w_in), (w_o, m_w_o, v_w_o), (w1, m_w1, v_w1), (w2, m_w2, v_w2)]):
        big_out.append(_rs_sum_adamw([owns[(l, t)] for l in range(L)], [recv_b[(l, t)] for l in range(L)],
                                     w, m, v, f"adamw_big_{t}"))

    packed = _pack_small(small_parts)
    gathered_small = _all_gather_small(packed, "ag_small_grads")
    small_sum = _sum_devices(gathered_small, "sum_small_grads")
    shapes = [[(1, D), (1, HEAD_DIM), (1, HEAD_DIM), (H2, N_REL), (1, D), (1, 6 * D)]] * L
    names = ["g_norm1", "g_q", "g_k", "rel_bias", "g_norm2", "b_ada"]
    small_w = {"g_norm1": (g_norm1, m_g_norm1, v_g_norm1), "g_q": (g_q, m_g_q, v_g_q), "g_k": (g_k, m_g_k, v_g_k),
               "rel_bias": (rel_bias, m_rel_bias, v_rel_bias), "g_norm2": (g_norm2, m_g_norm2, v_g_norm2),
               "b_ada": (b_ada, m_b_ada, v_b_ada)}
    packs = [_pack_small([[small_w[n][k][l] for n in names] for l in range(L)]) for k in range(3)]
    n_pad = packed.shape[1]
    as_rows = lambda a: a.reshape(n_pad // 128, 128)
    sd, sm, sv_ = _adamw(as_rows(packs[0]), as_rows(small_sum), as_rows(packs[1]), as_rows(packs[2]), "adamw_small")
    small_out = {}
    for key, flat in [("grad", small_sum), ("delta", sd), ("m", sm), ("v", sv_)]:
        per_layer = _unpack_small(flat.reshape(-1), shapes)
        for i, n in enumerate(names):
            small_out[(key, n)] = jnp.stack([per_layer[l][i].reshape(small_w[n][0].shape[1:]) for l in range(L)])

    layer_len = 2 * D + 2 * HEAD_DIM + H2 * N_REL + 6 * D
    rows = gathered_small.reshape(NDEV, n_pad)
    dmod_all = jnp.stack([rows[:, l * layer_len + layer_len - 6 * D:(l + 1) * layer_len] for l in range(L)])
    dmod_cols = lax.dynamic_slice(dmod_all, (0, 0, me * Ca), (L, NDEV, Ca))
    dmod_cols = jnp.pad(dmod_cols, ((0, 0), (0, 128 - NDEV), (0, 0)))
    c_t = jnp.pad(c_all.T, ((0, 0), (0, 128 - NDEV)))
    g_ada = _w_ada_grad(c_t, dmod_cols, "w_ada_grad")
    flat2 = lambda a: a.reshape(L * D, Ca)
    ad, am, av = _adamw(flat2(w_ada), flat2(g_ada), flat2(m_w_ada), flat2(v_w_ada), "adamw_w_ada")
    ada_out = [g_ada] + [a.reshape(L, D, Ca) for a in (ad, am, av)]

    def leaf(kind):
        k = {"grad": 0, "delta": 1, "m": 2, "v": 3}[kind]
        return [small_out[(kind, "g_norm1")], big_out[0][k], small_out[(kind, "g_q")], small_out[(kind, "g_k")],
                small_out[(kind, "rel_bias")], big_out[1][k], small_out[(kind, "g_norm2")], big_out[2][k],
                big_out[3][k], ada_out[k], small_out[(kind, "b_ada")]]

    loss = lax.psum(loss_part[0, 0], ("x", "y", "c"))
    return (loss, grad_x, *leaf("grad"), *leaf("delta"), *leaf("m"), *leaf("v"))
```

```python
import functools

import jax
import jax.numpy as jnp
from jax import lax
from jax.experimental import pallas as pl
from jax.experimental.pallas import tpu as pltpu

F32 = jnp.float32
_MXU = jnp.bfloat16

HEAD_DIM = 64
CHUNK = 64
LEFT_CHUNKS = 8
PAD = LEFT_CHUNKS * CHUNK
BAND = PAD + CHUNK
REL_CLIP = 128
N_REL = 2 * REL_CLIP + 1
EPS = 1e-6
NEG = -1e30
NDEV = 8
SB_T = 128
CA_T = 2 * CHUNK
CA_W = CA_T + PAD
SB_SKIP = -104.0
PAIR = 2 * HEAD_DIM
SB_PAIRS = 4
CA_PAIRS_FWD = 4
CA_PAIRS_BWD = 2
ROW_BLOCK = 512
SKEW_W = 767

ADAM_LR, ADAM_B1, ADAM_B2, ADAM_EPS, ADAM_WD, ADAM_STEP = 0.001, 0.9, 0.999, 1e-08, 0.01, 10

MESH = pl.DeviceIdType.MESH
VMEM_SPEC = pl.BlockSpec(memory_space=pltpu.VMEM)
SMEM_SPEC = pl.BlockSpec(memory_space=pltpu.SMEM)
ANY_SPEC = pl.BlockSpec(memory_space=pl.ANY)


def _nn(a, b):
    return lax.dot_general(a, b, (((1,), (0,)), ((), ())), preferred_element_type=F32)


def _nt(a, b):
    return lax.dot_general(a, b, (((1,), (1,)), ((), ())), preferred_element_type=F32)


def _tn(a, b):
    return lax.dot_general(a, b, (((0,), (0,)), ((), ())), preferred_element_type=F32)


def _blk(n, pref):
    return pref if n % pref == 0 else n


def _pos():
    return lax.axis_index("x"), lax.axis_index("y"), lax.axis_index("c")


def _flip(v, bit):
    return 1 - v if bit else v


def _all_gather_small(blk, name):
    R, C = blk.shape

    def body(x_ref, out_ref, send_sems, recv_sems):
        x, y, c = _pos()
        me = 4 * x + 2 * y + c

        def peer(k):
            return (_flip(x, k & 4), _flip(y, k & 2), _flip(c, k & 1))

        def copy(k, slot):
            return pltpu.make_async_remote_copy(
                src_ref=x_ref, dst_ref=out_ref.at[slot], send_sem=send_sems.at[k - 1],
                recv_sem=recv_sems.at[k - 1], device_id=peer(k), device_id_type=MESH)

        out_ref[pl.ds(me, 1), :, :] = x_ref[...].reshape(1, R, C)
        sends = [copy(k, me) for k in range(1, NDEV)]
        for cp in sends:
            cp.start()
        for k in range(1, NDEV):
            px, py, pc = peer(k)
            copy(k, 4 * px + 2 * py + pc).wait_recv()
        for cp in sends:
            cp.wait_send()

    return pl.pallas_call(
        body, name=name,
        out_shape=jax.ShapeDtypeStruct((NDEV, R, C), blk.dtype),
        in_specs=[VMEM_SPEC], out_specs=VMEM_SPEC,
        scratch_shapes=[pltpu.SemaphoreType.DMA((NDEV - 1,)), pltpu.SemaphoreType.DMA((NDEV - 1,))],
    )(blk)


class _Exchange:
    def __init__(self, inputs, out_shapes, sems, start, finish, middle=None):
        self.inputs, self.out_shapes, self.sems = list(inputs), list(out_shapes), list(sems)
        self.start, self.middle, self.finish = start, middle, finish


def _run_exchange(ex, name):
    n_in, n_out = len(ex.inputs), len(ex.out_shapes)

    def body(*refs):
        ins, outs, sems = refs[:n_in], refs[n_in:n_in + n_out], refs[n_in + n_out:]
        ex.start(ins, outs, sems)
        if ex.middle is not None:
            ex.middle(ins, outs, sems)
        ex.finish(ins, outs, sems)

    return pl.pallas_call(
        body, name=name, out_shape=ex.out_shapes, in_specs=[ANY_SPEC] * n_in, out_specs=[ANY_SPEC] * n_out,
        scratch_shapes=ex.sems,
    )(*ex.inputs)


def _hosted(body, n_in, n_out, ex, step, steps):
    if ex is None:
        return body
    xi, xo = len(ex.inputs), len(ex.out_shapes)

    def wrapped(*refs):
        own_in, ex_in = refs[:n_in], refs[n_in:n_in + xi]
        rest = refs[n_in + xi:]
        own_out, ex_out = rest[:n_out], rest[n_out:n_out + xo]
        rest = rest[n_out + xo:]
        own_scratch, ex_sems = rest[:len(rest) - len(ex.sems)], rest[len(rest) - len(ex.sems):]
        t = step()
        pl.when(t == 0)(lambda: ex.start(ex_in, ex_out, ex_sems))
        body(*own_in, *own_out, *own_scratch)
        if ex.middle is not None:
            pl.when(t == (steps * 3) // 5)(lambda: ex.middle(ex_in, ex_out, ex_sems))
        pl.when(t == steps - 1)(lambda: ex.finish(ex_in, ex_out, ex_sems))

    return wrapped


def _call_hosted(body, name, grid, in_specs, out_specs, out_shape, scratch, args, ex):
    n_in, n_out = len(in_specs), len(out_specs)
    steps = 1
    for extent in grid:
        steps *= extent

    def step():
        t = pl.program_id(0)
        for axis in range(1, len(grid)):
            t = t * grid[axis] + pl.program_id(axis)
        return t

    if ex is not None:
        in_specs = in_specs + [ANY_SPEC] * len(ex.inputs)
        out_specs = out_specs + [ANY_SPEC] * len(ex.out_shapes)
        out_shape = out_shape + ex.out_shapes
        scratch = scratch + ex.sems
        args = args + ex.inputs
    outs = pl.pallas_call(
        _hosted(body, n_in, n_out, ex, step, steps), name=name, grid=grid, in_specs=in_specs, out_specs=out_specs,
        out_shape=out_shape, scratch_shapes=scratch,
    )(*args)
    return list(outs[:n_out]), list(outs[n_out:])


def _gather_exchange(shards):
    n = len(shards)

    def setup(ins, outs, sems):
        send_sems, recv_sems, local_sems = sems
        x, y, c = _pos()
        me, sibling = (x, y, c), (x, y, 1 - c)
        chips = [(1 - x, y), (x, 1 - y), (1 - x, 1 - y)]

        def copy(i, k, block, to, src=None):
            px, py, pc = block
            dst = outs[i].at[4 * px + 2 * py + pc]
            return pltpu.make_async_remote_copy(
                src_ref=dst if src is None else src, dst_ref=dst, send_sem=send_sems.at[7 * i + k],
                recv_sem=recv_sems.at[7 * i + k], device_id=to, device_id_type=MESH)

        def mine(i):
            return pltpu.make_async_copy(ins[i], outs[i].at[4 * x + 2 * y + c], local_sems.at[i])

        def first(i):
            return [copy(i, 0, me, sibling, src=ins[i])] + [
                copy(i, 1 + j, me, (*chip, c), src=ins[i]) for j, chip in enumerate(chips)]

        def passed(i, j):
            return copy(i, 4 + j, (*chips[j], c), sibling)

        return me, sibling, chips, c, copy, mine, first, passed

    def start(ins, outs, sems):
        _, _, _, _, _, mine, first, _ = setup(ins, outs, sems)
        for i in range(n):
            mine(i).start()
            for cp in first(i):
                cp.start()

    def middle(ins, outs, sems):
        me, _, chips, c, copy, _, _, passed = setup(ins, outs, sems)
        for j, chip in enumerate(chips):
            for i in range(n):
                copy(i, 1 + j, (*chip, c), me).wait_recv()
                passed(i, j).start()

    def finish(ins, outs, sems):
        me, sibling, chips, c, copy, mine, first, passed = setup(ins, outs, sems)
        for i in range(n):
            copy(i, 0, sibling, me).wait_recv()
            for j, chip in enumerate(chips):
                copy(i, 4 + j, (*chip, 1 - c), me).wait_recv()
        for i in range(n):
            for cp in first(i) + [passed(i, j) for j in range(3)]:
                cp.wait_send()
            mine(i).wait()

    return _Exchange(
        shards, [jax.ShapeDtypeStruct((NDEV,) + s.shape, s.dtype) for s in shards],
        [pltpu.SemaphoreType.DMA((7 * n,)), pltpu.SemaphoreType.DMA((7 * n,)), pltpu.SemaphoreType.DMA((n,))],
        start, finish, middle)


def _sibling_exchange(grads):
    n = len(grads)

    def copies(ins, outs, sems):
        send_sems, recv_sems = sems
        x, y, c = _pos()
        return [pltpu.make_async_remote_copy(
            src_ref=ins[i].at[2 * q + (1 - c)], dst_ref=outs[i].at[q], send_sem=send_sems.at[4 * i + q],
            recv_sem=recv_sems.at[4 * i + q], device_id=(x, y, 1 - c), device_id_type=MESH)
            for i in range(n) for q in range(4)]

    def start(ins, outs, sems):
        for cp in copies(ins, outs, sems):
            cp.start()

    def finish(ins, outs, sems):
        for cp in copies(ins, outs, sems):
            cp.wait()

    return _Exchange(
        grads, [jax.ShapeDtypeStruct((4,) + g.shape[1:], g.dtype) for g in grads],
        [pltpu.SemaphoreType.DMA((4 * n,)), pltpu.SemaphoreType.DMA((4 * n,))], start, finish)


def _chip_exchange(parts):
    n = len(parts)

    def copies(ins, outs, sems):
        send_sems, recv_sems = sems
        x, y, c = _pos()
        return [pltpu.make_async_remote_copy(
            src_ref=ins[i].at[j - 1], dst_ref=outs[i].at[j - 1], send_sem=send_sems.at[3 * i + j - 1],
            recv_sem=recv_sems.at[3 * i + j - 1], device_id=(_flip(x, j & 2), _flip(y, j & 1), c),
            device_id_type=MESH) for i in range(n) for j in range(1, 4)]

    def start(ins, outs, sems):
        for cp in copies(ins, outs, sems):
            cp.start()

    def finish(ins, outs, sems):
        for cp in copies(ins, outs, sems):
            cp.wait()

    return _Exchange(
        parts, [jax.ShapeDtypeStruct(p.shape, p.dtype) for p in parts],
        [pltpu.SemaphoreType.DMA((3 * n,)), pltpu.SemaphoreType.DMA((3 * n,))], start, finish)


def _rs_chip_partial(place, grad, recv, name):
    _, R, C = grad.shape
    tr = _blk(R, 256)

    def body(place_ref, *refs):
        g_refs, r_refs = refs[:4], refs[4:8]
        own_ref, out_ref = refs[8:]
        own_ref[...] = g_refs[0][0] + r_refs[0][0]
        for j in range(1, 4):
            out_ref[j - 1] = (g_refs[j][0] + r_refs[j][0]).astype(out_ref.dtype)

    def g_map(j):
        return lambda i, p: (2 * jnp.bitwise_xor(p[0], j) + p[1], i, 0)

    def r_map(j):
        return lambda i, p: (jnp.bitwise_xor(p[0], j), i, 0)

    grid_spec = pltpu.PrefetchScalarGridSpec(
        num_scalar_prefetch=1, grid=(R // tr,),
        in_specs=[pl.BlockSpec((1, tr, C), g_map(j)) for j in range(4)]
        + [pl.BlockSpec((1, tr, C), r_map(j)) for j in range(4)],
        out_specs=[pl.BlockSpec((tr, C), lambda i, p: (i, 0)), pl.BlockSpec((3, tr, C), lambda i, p: (0, i, 0))])
    return pl.pallas_call(
        body, name=name, grid_spec=grid_spec,
        out_shape=[jax.ShapeDtypeStruct((R, C), F32), jax.ShapeDtypeStruct((3, R, C), _MXU)],
    )(place, *([grad] * 4), *([recv] * 4))


def _adamw_math(w, g, m, v):
    m = ADAM_B1 * m + (1.0 - ADAM_B1) * g
    v = ADAM_B2 * v + (1.0 - ADAM_B2) * (g * g)
    m_hat = m / (1.0 - ADAM_B1 ** ADAM_STEP)
    v_hat = v / (1.0 - ADAM_B2 ** ADAM_STEP)
    delta = -ADAM_LR * (m_hat / (jnp.sqrt(v_hat) + ADAM_EPS) + ADAM_WD * w)
    return delta, m, v


def _adamw(w, g, m, v, name):
    R, C = w.shape
    tr = _blk(R, 256)

    def body(w_ref, g_ref, m_ref, v_ref, d_ref, nm_ref, nv_ref):
        d, nm, nv = _adamw_math(w_ref[...], g_ref[...], m_ref[...], v_ref[...])
        d_ref[...] = d
        nm_ref[...] = nm
        nv_ref[...] = nv

    spec = pl.BlockSpec((tr, C), lambda i: (i, 0))
    return pl.pallas_call(
        body, name=name, grid=(R // tr,), in_specs=[spec] * 4, out_specs=[spec] * 3,
        out_shape=[jax.ShapeDtypeStruct((R, C), F32)] * 3,
    )(w, g, m, v)


def _rs_sum_adamw(owns, recvs, w, m, v, name):
    L, R, C = w.shape
    tr = _blk(R, 256)
    nr = R // tr

    def body(o0, o1, r0, r1, w_ref, m_ref, v_ref, g_ref, d_ref, nm_ref, nv_ref):
        def step(o_ref, r_ref):
            g = o_ref[...]
            for j in range(3):
                g = g + r_ref[j].astype(F32)
            d, nm, nv = _adamw_math(w_ref[0], g, m_ref[0], v_ref[0])
            g_ref[0] = g
            d_ref[0] = d
            nm_ref[0] = nm
            nv_ref[0] = nv

        pl.when(pl.program_id(0) == 0)(lambda: step(o0, r0))
        pl.when(pl.program_id(0) == 1)(lambda: step(o1, r1))

    def hold(layer):
        if layer == 0:
            return lambda l, i: i * (1 - l) + (nr - 1) * l
        return lambda l, i: i * l

    own_spec = [pl.BlockSpec((tr, C), functools.partial(lambda l, i, f: (f(l, i), 0), f=hold(k))) for k in range(2)]
    recv_spec = [pl.BlockSpec((3, tr, C), functools.partial(lambda l, i, f: (0, f(l, i), 0), f=hold(k)))
                 for k in range(2)]
    lay = pl.BlockSpec((1, tr, C), lambda l, i: (l, i, 0))
    return pl.pallas_call(
        body, name=name, grid=(L, nr),
        in_specs=own_spec + recv_spec + [lay] * 3, out_specs=[lay] * 4,
        out_shape=[jax.ShapeDtypeStruct((L, R, C), F32)] * 4,
    )(owns[0], owns[1], recvs[0], recvs[1], w, m, v)


def _silu(x):
    return x / (1.0 + jnp.exp(-x))


def _mod_partial(c_all, w_ada, b_cols, name):
    L, D, Ca = w_ada.shape

    def body(c_ref, w_ref, b_ref, o_ref):
        act = _silu(c_ref[...]).astype(_MXU)
        for l in range(L):
            o_ref[:, l * Ca:(l + 1) * Ca] = _nn(act, w_ref[l].astype(_MXU)) + b_ref[l:l + 1, :]

    return pl.pallas_call(
        body, name=name, out_shape=jax.ShapeDtypeStruct((NDEV, L * Ca), F32),
        in_specs=[VMEM_SPEC] * 3, out_specs=VMEM_SPEC,
    )(c_all, w_ada, b_cols)


def _w_ada_grad(c_t, dmod_cols, name):
    L, _, Ca = dmod_cols.shape
    D = c_t.shape[0]

    def body(c_ref, d_ref, o_ref):
        act = _silu(c_ref[...]).astype(_MXU)
        for l in range(L):
            o_ref[l] = _nn(act, d_ref[l].astype(_MXU))

    return pl.pallas_call(
        body, name=name, out_shape=jax.ShapeDtypeStruct((L, D, Ca), F32),
        in_specs=[VMEM_SPEC] * 2, out_specs=VMEM_SPEC,
    )(c_t, dmod_cols)


def _sum_devices(gathered, name):
    _, _, N = gathered.shape

    def body(x_ref, o_ref):
        acc = x_ref[0]
        for d in range(1, NDEV):
            acc = acc + x_ref[d]
        o_ref[...] = acc

    return pl.pallas_call(
        body, name=name, out_shape=jax.ShapeDtypeStruct((1, N), F32),
        in_specs=[VMEM_SPEC], out_specs=VMEM_SPEC,
    )(gathered)


def _ln_mod_matmul(x, g, sc, sh, w, name):
    S, D = x.shape
    N = w.shape[1]
    tm = _blk(S, ROW_BLOCK)

    def body(x_ref, g_ref, sc_ref, sh_ref, w_ref, o_ref, h_ref):
        xv = x_ref[...]
        r = lax.rsqrt(jnp.mean(xv * xv, axis=-1, keepdims=True) + EPS)
        hv = ((xv * r) * g_ref[...]) * (1.0 + sc_ref[...]) + sh_ref[...]
        hb = hv.astype(_MXU)
        h_ref[...] = hb
        o_ref[...] = _nn(hb, w_ref[...]).astype(o_ref.dtype)

    vec = pl.BlockSpec((1, D), lambda i: (0, 0))
    row = lambda width: pl.BlockSpec((tm, width), lambda i: (i, 0))
    return pl.pallas_call(
        body, name=name, grid=(S // tm,),
        in_specs=[row(D), vec, vec, vec, pl.BlockSpec((D, N), lambda i: (0, 0))],
        out_specs=[row(N), row(D)],
        out_shape=[jax.ShapeDtypeStruct((S, N), _MXU), jax.ShapeDtypeStruct((S, D), _MXU)],
    )(x, g, sc, sh, w)


def _matmul_res_gate(a, w, xres, gt, relu2, name):
    S, K = a.shape
    N = w.shape[1]
    tm = _blk(S, 512)

    def body(a_ref, w_ref, x_ref, gt_ref, o_ref, f_ref):
        av = a_ref[...]
        if relu2:
            af = jnp.maximum(av.astype(F32), 0.0)
            av = (af * af).astype(_MXU)
        f = _nn(av, w_ref[...])
        f_ref[...] = f.astype(f_ref.dtype)
        o_ref[...] = x_ref[...] + gt_ref[...] * f

    row = lambda width: pl.BlockSpec((tm, width), lambda i: (i, 0))
    return pl.pallas_call(
        body, name=name, grid=(S // tm,),
        in_specs=[row(K), pl.BlockSpec((K, N), lambda i: (0, 0)), row(N), pl.BlockSpec((1, N), lambda i: (0, 0))],
        out_specs=[row(N), row(N)],
        out_shape=[jax.ShapeDtypeStruct((S, N), F32), jax.ShapeDtypeStruct((S, N), _MXU)],
    )(a, w, xres, gt)


def _loss_grad(y, t, name):
    S, D = y.shape
    tm = _blk(S, 512)
    last = S // tm - 1

    def body(y_ref, t_ref, dy_ref, l_ref, acc_ref):
        i = pl.program_id(0)
        e = y_ref[...] - t_ref[...]
        dy_ref[...] = e * (1.0 / D)
        part = jnp.sum(e * e, axis=0, keepdims=True)

        @pl.when(i == 0)
        def _():
            acc_ref[...] = part

        @pl.when(i > 0)
        def _():
            acc_ref[...] += part

        @pl.when(i == last)
        def _():
            l_ref[...] = (0.5 / D) * jnp.sum(acc_ref[...], axis=1, keepdims=True)

    row = pl.BlockSpec((tm, D), lambda i: (i, 0))
    return pl.pallas_call(
        body, name=name, grid=(S // tm,), in_specs=[row, row],
        out_specs=[row, pl.BlockSpec((1, 1), lambda i: (0, 0))],
        out_shape=[jax.ShapeDtypeStruct((S, D), F32), jax.ShapeDtypeStruct((1, 1), F32)],
        scratch_shapes=[pltpu.VMEM((1, D), F32)],
    )(y, t)


def _accumulate(ref, part, first):
    @pl.when(first)
    def _():
        ref[...] = part

    @pl.when(jnp.logical_not(first))
    def _():
        ref[...] += part


def _gate_nt_matmul(dx, f, gt, w, u, name):
    S, D = dx.shape
    N = w.shape[0]
    tm = _blk(S, ROW_BLOCK if N <= D else ROW_BLOCK // 2)
    with_u = u is not None

    def body(*refs):
        if with_u:
            dx_ref, f_ref, gt_ref, w_ref, u_ref, dz_ref, dgt_ref, res_ref = refs
        else:
            dx_ref, f_ref, gt_ref, w_ref, dz_ref, dgt_ref, res_ref = refs
        dxv = dx_ref[...]
        dz = (dxv * gt_ref[...]).astype(_MXU)
        dz_ref[...] = dz
        _accumulate(dgt_ref, jnp.sum(dxv * f_ref[...].astype(F32), axis=0, keepdims=True), pl.program_id(0) == 0)
        r = _nt(dz, w_ref[...])
        if with_u:
            r = r * (2.0 * jnp.maximum(u_ref[...].astype(F32), 0.0))
        res_ref[...] = r.astype(res_ref.dtype)

    row = lambda width: pl.BlockSpec((tm, width), lambda i: (i, 0))
    in_specs = [row(D), row(D), pl.BlockSpec((1, D), lambda i: (0, 0)), pl.BlockSpec((N, D), lambda i: (0, 0))]
    args = [dx, f, gt, w]
    if with_u:
        in_specs.append(row(N))
        args.append(u)
    return pl.pallas_call(
        body, name=name, grid=(S // tm,), in_specs=in_specs,
        out_specs=[row(D), pl.BlockSpec((1, D), lambda i: (0, 0)), row(N)],
        out_shape=[jax.ShapeDtypeStruct((S, D), _MXU), jax.ShapeDtypeStruct((1, D), F32),
                   jax.ShapeDtypeStruct((S, N), _MXU)],
    )(*args)


def _tn_matmul(a, b, by_col, relu2, name):
    S, Ka = a.shape
    Nb = b.shape[1]
    ts = _blk(S, 2 * ROW_BLOCK)
    half = NDEV // 2
    if by_col:
        R, C = Ka, Nb // NDEV
        a_spec = pl.BlockSpec((ts, Ka), lambda h, k: (k, 0))
        b_spec = pl.BlockSpec((ts, half * C), lambda h, k: (k, h))
    else:
        R, C = Ka // NDEV, Nb
        a_spec = pl.BlockSpec((ts, half * R), lambda h, k: (k, h))
        b_spec = pl.BlockSpec((ts, Nb), lambda h, k: (k, 0))

    def body(a_ref, b_ref, o_ref):
        av = a_ref[...]
        if relu2:
            af = jnp.maximum(av.astype(F32), 0.0)
            av = (af * af).astype(_MXU)
        p = _tn(av, b_ref[...])
        first = pl.program_id(1) == 0
        for d in range(half):
            part = p[:, d * C:(d + 1) * C] if by_col else p[d * R:(d + 1) * R, :]
            _accumulate(o_ref.at[d], part, first)

    return pl.pallas_call(
        body, name=name, grid=(NDEV // half, S // ts), in_specs=[a_spec, b_spec],
        out_specs=pl.BlockSpec((half, R, C), lambda h, k: (h, 0, 0)),
        out_shape=jax.ShapeDtypeStruct((NDEV, R, C), F32),
    )(a, b)


def _nt_ln_bwd(dy, w, x, g, sc, sh, dxres, name):
    S, D = x.shape
    N = w.shape[1]
    tm = _blk(S, ROW_BLOCK)

    def body(dy_ref, w_ref, x_ref, g_ref, sc_ref, sh_ref, dxr_ref, dx_ref, dsh_ref, dsc_ref, dg_ref):
        dh = _nt(dy_ref[...], w_ref[...])
        xv = x_ref[...]
        r = lax.rsqrt(jnp.mean(xv * xv, axis=-1, keepdims=True) + EPS)
        xhat = xv * r
        gv = g_ref[...]
        dn = dh * (1.0 + sc_ref[...])
        dxhat = dn * gv
        dxv = r * (dxhat - xhat * jnp.mean(dxhat * xhat, axis=-1, keepdims=True))
        dx_ref[...] = dxr_ref[...] + dxv
        first = pl.program_id(0) == 0
        _accumulate(dsh_ref, jnp.sum(dh, axis=0, keepdims=True), first)
        _accumulate(dsc_ref, jnp.sum(dh * (xhat * gv), axis=0, keepdims=True), first)
        _accumulate(dg_ref, jnp.sum(dn * xhat, axis=0, keepdims=True), first)

    row = lambda width: pl.BlockSpec((tm, width), lambda i: (i, 0))
    vec = pl.BlockSpec((1, D), lambda i: (0, 0))
    return pl.pallas_call(
        body, name=name, grid=(S // tm,),
        in_specs=[row(N), pl.BlockSpec((D, N), lambda i: (0, 0)), row(D), vec, vec, vec, row(D)],
        out_specs=[row(D), vec, vec, vec],
        out_shape=[jax.ShapeDtypeStruct((S, D), F32)] + [jax.ShapeDtypeStruct((1, D), F32)] * 3,
    )(dy, w, x, g, sc, sh, dxres)


def _split2(v):
    hi = v.astype(_MXU)
    mid = (v - hi.astype(F32)).astype(_MXU)
    return hi, mid


def _tri_sums(vs, tri2):
    T = vs[0].shape[0]
    out = []
    for j in range(len(vs) // 2):
        (hi0, mid0), (hi1, mid1) = _split2(vs[2 * j]), _split2(vs[2 * j + 1])
        s = _per_head(_nn(jnp.concatenate([hi0, hi1, mid0, mid1], axis=0), tri2), T)
        for e in range(2):
            both = s[e] + s[2 + e]
            out.append((both[:, :T], both[:, T:]))
    return out


def _tri2(T, inclusive):
    j = lax.broadcasted_iota(jnp.int32, (T, 2 * T), 0)
    s = lax.broadcasted_iota(jnp.int32, (T, 2 * T), 1)
    keep = (j >= s) if inclusive else (j > s)
    return jnp.where((s >= T) | keep, 1.0, 0.0).astype(_MXU)


def _log_sigmoid(z):
    return jnp.minimum(z, 0.0) - jnp.log(1.0 + jnp.exp(-jnp.abs(z)))


def _per_head(tall, T):
    return [tall[h * T:(h + 1) * T] for h in range(tall.shape[0] // T)]


def _sb_blocks(q_tall, k2, strict, tri2, carry):
    scale = HEAD_DIM ** -0.5
    T = k2[0].shape[0]
    zs = []
    for qt, kblk in zip(q_tall, k2):
        zs += _per_head(_nt(qt, kblk) * scale, T)
    lbs, l1s = [], []
    for z in zs:
        lb = _log_sigmoid(z)
        l1 = lb - z
        if strict is not None:
            l1 = jnp.where(strict, l1, 0.0)
        lbs.append(lb)
        l1s.append(l1)
    sums = _tri_sums(l1s, tri2)
    amps, new_carry = [], []
    for lb, (sfx, tot), c in zip(lbs, sums, carry):
        a = jnp.exp(lb + sfx + c)
        if strict is not None:
            a = jnp.where(strict, a, 0.0)
        amps.append(a)
        new_carry.append(c + tot)
    return lbs, amps, new_carry


def _sb_alive(carry):
    top = carry[0]
    for c in carry[1:]:
        top = jnp.maximum(top, c)
    return jnp.max(top) > SB_SKIP


def _skew_index():
    i = lax.broadcasted_iota(jnp.int32, (CA_T, SKEW_W + 1), 0)
    m = lax.broadcasted_iota(jnp.int32, (CA_T, SKEW_W + 1), 1)
    wrapped = i + m >= SKEW_W
    row = jnp.where(wrapped, i + 1, i)
    j = jnp.where(wrapped, i + m - SKEW_W, i + m)
    a = row // CHUNK
    jj = j - a * CHUNK
    inband = (jj >= 0) & (jj < BAND) & (j < CA_W) & (row < CA_T)
    idx = jnp.clip((row - a * CHUNK) + PAD - jj, -REL_CLIP, REL_CLIP) + REL_CLIP
    return inband, idx, wrapped


def _skew(tile):
    H = tile.shape[0]
    flat = jnp.pad(tile, ((0, 0), (0, 0), (0, SKEW_W - CA_W))).reshape(H, CA_T * SKEW_W)
    return jnp.pad(flat, ((0, 0), (0, CA_T))).reshape(H, CA_T, SKEW_W + 1)


def _unskew(view):
    H = view.shape[0]
    flat = view.reshape(H, CA_T * (SKEW_W + 1))[:, :CA_T * SKEW_W]
    return flat.reshape(H, CA_T, SKEW_W)[:, :, :CA_W]


def _ca_bias(rel_bias, name):
    H = rel_bias.shape[0]
    top = rel_bias[:, N_REL - 1:]
    by_offset = jnp.concatenate(
        [jnp.broadcast_to(top, (H, PAD - REL_CLIP + 1)), jnp.flip(rel_bias[:, :N_REL - 1], axis=1),
         jnp.broadcast_to(top, (H, SKEW_W + 1 - (PAD - REL_CLIP + 1) - (N_REL - 1)))], axis=1)

    def body(t_ref, o_ref):
        inband, _, wrapped = _skew_index()
        vals = jnp.where(wrapped, t_ref[0][:, 0:1], t_ref[0])
        o_ref[0] = jnp.where(inband, vals, NEG)

    view = pl.pallas_call(
        body, name=name, grid=(H,), in_specs=[pl.BlockSpec((1, 1, SKEW_W + 1), lambda h: (h, 0, 0))],
        out_specs=pl.BlockSpec((1, CA_T, SKEW_W + 1), lambda h: (h, 0, 0)),
        out_shape=jax.ShapeDtypeStruct((H, CA_T, SKEW_W + 1), F32),
    )(by_offset.reshape(H, 1, SKEW_W + 1))
    return _unskew(view)


def _ca_bias_bwd(dbias, name):
    H = dbias.shape[0]

    def body(d_ref, o_ref):
        inband, idx, _ = _skew_index()
        d = jnp.where(inband, d_ref[0], 0.0)
        clipped = idx == N_REL - 1
        by_offset = jnp.sum(jnp.where(clipped, 0.0, d), axis=0, keepdims=True)
        top = jnp.sum(jnp.sum(jnp.where(clipped, d, 0.0), axis=0, keepdims=True), axis=1, keepdims=True)
        lane = lax.broadcasted_iota(jnp.int32, (1, SKEW_W + 1), 1)
        o_ref[0] = jnp.where(lane == 0, top, by_offset)

    out = pl.pallas_call(
        body, name=name, grid=(H,), in_specs=[pl.BlockSpec((1, CA_T, SKEW_W + 1), lambda h: (h, 0, 0))],
        out_specs=pl.BlockSpec((1, 1, SKEW_W + 1), lambda h: (h, 0, 0)),
        out_shape=jax.ShapeDtypeStruct((H, 1, SKEW_W + 1), F32),
    )(_skew(dbias))[:, 0]
    first = PAD - REL_CLIP + 1
    return jnp.concatenate([jnp.flip(out[:, first:first + N_REL - 1], axis=1), out[:, 0:1]], axis=1)


def _low_lanes(rows):
    return lax.broadcasted_iota(jnp.int32, (rows, PAIR), 1) < HEAD_DIM


def _one_head(t2, low, first):
    tf = t2.astype(F32)
    return (jnp.where(low, tf, 0.0) if first else jnp.where(low, 0.0, tf)).astype(_MXU)


def _sb_fwd(proj, name, ex=None):
    S, W = proj.shape
    half = W // 6
    npair = half // PAIR
    T = _blk(S, SB_T)
    GP = _blk(npair, SB_PAIRS)
    GW = GP * PAIR
    nb = npair // GP

    def body(q_ref, k_ref, v_ref, o_ref, ox_ref):
        qi = pl.program_id(1)
        low = _low_lanes(T)
        q_tall = []
        for j in range(GP):
            q2 = q_ref[:, j * PAIR:(j + 1) * PAIR]
            q_tall.append(jnp.concatenate([_one_head(q2, low, True), _one_head(q2, low, False)], axis=0))
        row = lax.broadcasted_iota(jnp.int32, (T, T), 0)
        col = lax.broadcasted_iota(jnp.int32, (T, T), 1)
        tri2 = _tri2(T, inclusive=False)

        def pairs(kb, carry, acc, fine, strict):
            rows = pl.ds(pl.multiple_of(kb * T, T), T)
            k2 = [k_ref[rows, j * PAIR:(j + 1) * PAIR] for j in range(GP)]
            v2 = [v_ref[rows, j * PAIR:(j + 1) * PAIR] for j in range(GP)]
            _, amps, carry = _sb_blocks(q_tall, k2, strict, tri2, carry)
            parts = [_split2(a) for a in amps]
            new_acc, new_fine = [], []
            for j in range(GP):
                tall = jnp.concatenate([parts[2 * j][0], parts[2 * j + 1][0], parts[2 * j][1], parts[2 * j + 1][1]],
                                       axis=0)
                hi0, hi1, mid0, mid1 = _per_head(_nn(tall, v2[j]), T)
                new_acc.append(acc[j] + jnp.where(low, hi0, hi1))
                new_fine.append(fine[j] + jnp.where(low, mid0, mid1))
            return tuple(carry), tuple(new_acc), tuple(new_fine)

        zero = (jnp.zeros((T, PAIR), F32),) * GP
        carry, acc, fine = pairs(qi, (jnp.zeros((T, T), F32),) * (2 * GP), zero, zero, col < row)

        def cond(st):
            kb, carry, _, _ = st
            return jnp.logical_and(kb >= 0, _sb_alive(carry))

        def step(st):
            kb, carry, acc, fine = st
            carry, acc, fine = pairs(kb, carry, acc, fine, None)
            return kb - 1, carry, acc, fine

        _, _, acc, fine = lax.while_loop(cond, step, (qi - 1, carry, acc, fine))
        for j in range(GP):
            o_ref[:, j * PAIR:(j + 1) * PAIR] = acc[j].astype(o_ref.dtype)
            ox_ref[:, j * PAIR:(j + 1) * PAIR] = acc[j] + fine[j]

    blk = pl.BlockSpec((T, GW), lambda p, i: (i, p))
    return _call_hosted(
        body, name, (nb, S // T),
        [blk, pl.BlockSpec((S, GW), lambda p, i: (0, nb + p)), pl.BlockSpec((S, GW), lambda p, i: (0, 2 * nb + p))],
        [blk, blk], [jax.ShapeDtypeStruct((S, half), _MXU), jax.ShapeDtypeStruct((S, half), F32)],
        [], [proj, proj, proj], ex)


def _sb_bwd(proj, ox, dmixed, name, ex=None):
    S, W = proj.shape
    half = W // 6
    npair = half // PAIR
    T = _blk(S, SB_T)
    GP = _blk(npair, SB_PAIRS)
    GW = GP * PAIR
    nb = npair // GP
    last = S // T - 1
    scale = HEAD_DIM ** -0.5

    def body(q_ref, k_ref, v_ref, ox_ref, do_ref, dq_ref, dk_ref, dv_ref, dka_ref, dva_ref):
        qi = pl.program_id(1)

        @pl.when(qi == 0)
        def _():
            dka_ref[...] = jnp.zeros_like(dka_ref)
            dva_ref[...] = jnp.zeros_like(dva_ref)

        low = _low_lanes(T)
        q2, do2, q_tall, do_tall, deltas = [], [], [], [], []
        for j in range(GP):
            cols = slice(j * PAIR, (j + 1) * PAIR)
            q2.append(q_ref[:, cols])
            do2.append(do_ref[:, cols])
            q_tall.append(jnp.concatenate([_one_head(q2[j], low, True), _one_head(q2[j], low, False)], axis=0))
            dobs = [_one_head(do2[j], low, True), _one_head(do2[j], low, False)]
            do_tall.append(jnp.concatenate(dobs, axis=0))
            for e in range(2):
                deltas.append(jnp.sum(dobs[e].astype(F32) * ox_ref[:, cols], axis=-1, keepdims=True))
        row = lax.broadcasted_iota(jnp.int32, (T, T), 0)
        col = lax.broadcasted_iota(jnp.int32, (T, T), 1)
        tri_ex = _tri2(T, inclusive=False)
        tri_in = _tri2(T, inclusive=True)

        def pairs(kb, carry, right, dq, strict):
            rows = pl.ds(pl.multiple_of(kb * T, T), T)
            k2 = [k_ref[rows, j * PAIR:(j + 1) * PAIR] for j in range(GP)]
            v2 = [v_ref[rows, j * PAIR:(j + 1) * PAIR] for j in range(GP)]
            nh = 2 * GP
            gs = []
            for j in range(GP):
                gs += _per_head(_nt(do_tall[j], v2[j]), T)
            lbs, amps, carry = _sb_blocks(q_tall, k2, strict, tri_ex, carry)
            ags = [a * gg for a, gg in zip(amps, gs)]
            sums = _tri_sums(ags, tri_in)
            dzbs = []
            for h in range(nh):
                left = deltas[h] - (sums[h][0] + right[h])
                beta = jnp.exp(lbs[h])
                dz = ags[h] * (1.0 - beta) - beta * left
                if strict is not None:
                    dz = jnp.where(strict, dz, 0.0)
                dzbs.append((dz * scale).astype(_MXU))
            abs_ = [a.astype(_MXU) for a in amps]
            new_dq = []
            for j in range(GP):
                cols = slice(j * PAIR, (j + 1) * PAIR)
                dk0, dk1 = _per_head(_tn(jnp.concatenate(dzbs[2 * j:2 * j + 2], axis=1), q2[j]), T)
                dv0, dv1 = _per_head(_tn(jnp.concatenate(abs_[2 * j:2 * j + 2], axis=1), do2[j]), T)
                dq0, dq1 = _per_head(_nn(jnp.concatenate(dzbs[2 * j:2 * j + 2], axis=0), k2[j]), T)
                dka_ref[rows, cols] += jnp.where(low, dk0, dk1)
                dva_ref[rows, cols] += jnp.where(low, dv0, dv1)
                new_dq.append(dq[j] + jnp.where(low, dq0, dq1))
            right = tuple(right[h] + sums[h][1] for h in range(nh))
            return tuple(carry), right, tuple(new_dq)

        zero = (jnp.zeros((T, T), F32),) * (2 * GP)
        carry, right, dq = pairs(qi, zero, zero, (jnp.zeros((T, PAIR), F32),) * GP, col < row)

        def cond(st):
            kb, carry, _, _ = st
            return jnp.logical_and(kb >= 0, _sb_alive(carry))

        def step(st):
            kb, carry, right, dq = st
            carry, right, dq = pairs(kb, carry, right, dq, None)
            return kb - 1, carry, right, dq

        _, _, _, dq = lax.while_loop(cond, step, (qi - 1, carry, right, dq))
        for j in range(GP):
            dq_ref[:, j * PAIR:(j + 1) * PAIR] = dq[j].astype(dq_ref.dtype)

        @pl.when(qi == last)
        def _():
            dk_ref[...] = dka_ref[...].astype(dk_ref.dtype)
            dv_ref[...] = dva_ref[...].astype(dv_ref.dtype)

    blk = pl.BlockSpec((T, GW), lambda p, i: (i, p))
    full = pl.BlockSpec((S, GW), lambda p, i: (0, p))
    return _call_hosted(
        body, name, (nb, S // T),
        [blk, pl.BlockSpec((S, GW), lambda p, i: (0, nb + p)), pl.BlockSpec((S, GW), lambda p, i: (0, 2 * nb + p)),
         blk, blk],
        [blk, full, full], [jax.ShapeDtypeStruct((S, half), _MXU)] * 3,
        [pltpu.VMEM((S, GW), F32), pltpu.VMEM((S, GW), F32)], [proj, proj, proj, ox, dmixed], ex)


def _pair_norm(t2, g2, low):
    tf = t2.astype(F32)
    sq = tf * tf
    both = jnp.sum(sq, axis=-1, keepdims=True)
    first = jnp.sum(jnp.where(low, sq, 0.0), axis=-1, keepdims=True)
    r = jnp.where(low, lax.rsqrt(first * (1.0 / HEAD_DIM) + EPS), lax.rsqrt((both - first) * (1.0 / HEAD_DIM) + EPS))
    hat = tf * r
    return hat * g2, hat, r


def _pair_norm_bwd(dn, hat, r, g2, low):
    dhat = dn * g2
    prod = dhat * hat
    both = jnp.sum(prod, axis=-1, keepdims=True)
    first = jnp.sum(jnp.where(low, prod, 0.0), axis=-1, keepdims=True)
    mean = jnp.where(low, first, both - first) * (1.0 / HEAD_DIM)
    return r * (dhat - hat * mean)


def _ca_fill(j, k_ref, v_ref, gk_ref, kn_ref, vp_ref):
    S = k_ref.shape[0]
    cols = slice(j * PAIR, (j + 1) * PAIR)
    kn, _, _ = _pair_norm(k_ref[:, cols], gk_ref[...], _low_lanes(S))
    kn_ref[j, 0:PAD, :] = jnp.zeros((PAD, PAIR), kn_ref.dtype)
    vp_ref[j, 0:PAD, :] = jnp.zeros((PAD, PAIR), vp_ref.dtype)
    kn_ref[j, PAD:PAD + S, :] = kn.astype(kn_ref.dtype)
    vp_ref[j, PAD:PAD + S, :] = v_ref[:, cols]


def _ca_scores(j, q_ref, b2_ref, gq_ref, kn_ref, qi, low):
    qn, qhat, r = _pair_norm(q_ref[:, j * PAIR:(j + 1) * PAIR], gq_ref[...], low)
    qn = qn * HEAD_DIM ** -0.5
    band = pl.ds(pl.multiple_of(qi * CA_T, CA_T), CA_W)
    key_pos = qi * CA_T - PAD + lax.broadcasted_iota(jnp.int32, (CA_T, CA_W), 1)
    tall = jnp.concatenate([_one_head(qn, low, True), _one_head(qn, low, False)], axis=0)
    both = _per_head(_nt(tall, kn_ref[j, band, :]), CA_T)
    scores = [jnp.where(key_pos >= 0, both[e] + b2_ref[2 * j + e], NEG) for e in range(2)]
    return scores, qn.astype(_MXU), qhat, r


def _softmax(s):
    e = jnp.exp(s - jnp.max(s, axis=-1, keepdims=True))
    return e * (1.0 / jnp.sum(e, axis=-1, keepdims=True))


def _ca_fwd(proj, bias2, gq2, gk2, name, ex=None):
    S, W = proj.shape
    half = W // 6
    npair = half // PAIR
    GP = _blk(npair, CA_PAIRS_FWD)
    GW = GP * PAIR
    nb = npair // GP

    def body(q_ref, k_ref, v_ref, b2_ref, gq_ref, gk_ref, o_ref, kn_ref, vp_ref):
        qi = pl.program_id(1)

        @pl.when(qi == 0)
        def _():
            for j in range(GP):
                _ca_fill(j, k_ref, v_ref, gk_ref, kn_ref, vp_ref)

        low = _low_lanes(CA_T)
        band = pl.ds(pl.multiple_of(qi * CA_T, CA_T), CA_W)
        scores = [_ca_scores(j, q_ref, b2_ref, gq_ref, kn_ref, qi, low)[0] for j in range(GP)]
        probs = [[_softmax(s).astype(_MXU) for s in pair] for pair in scores]
        for j in range(GP):
            outs = _per_head(_nn(jnp.concatenate(probs[j], axis=0), vp_ref[j, band, :]), CA_T)
            o_ref[:, j * PAIR:(j + 1) * PAIR] = jnp.where(low, outs[0], outs[1]).astype(o_ref.dtype)

    vec = pl.BlockSpec((1, PAIR), lambda p, i: (0, 0))
    return _call_hosted(
        body, name, (nb, S // CA_T),
        [pl.BlockSpec((CA_T, GW), lambda p, i: (i, 3 * nb + p)),
         pl.BlockSpec((S, GW), lambda p, i: (0, 4 * nb + p)), pl.BlockSpec((S, GW), lambda p, i: (0, 5 * nb + p)),
         pl.BlockSpec((2 * GP, CA_T, CA_W), lambda p, i: (p, 0, 0)), vec, vec],
        [pl.BlockSpec((CA_T, GW), lambda p, i: (i, p))], [jax.ShapeDtypeStruct((S, half), _MXU)],
        [pltpu.VMEM((GP, PAD + S, PAIR), _MXU), pltpu.VMEM((GP, PAD + S, PAIR), _MXU)],
        [proj, proj, proj, bias2, gq2, gk2], ex)


def _ca_bwd(proj, bias2, gq2, gk2, dmixed, name, ex=None):
    S, W = proj.shape
    half = W // 6
    npair = half // PAIR
    GP = _blk(npair, CA_PAIRS_BWD)
    GW = GP * PAIR
    nb = npair // GP
    scale = HEAD_DIM ** -0.5
    last = S // CA_T - 1

    def body(q_ref, k_ref, v_ref, b2_ref, gq_ref, gk_ref, do_ref,
             dq_ref, dk_ref, dv_ref, db_ref, dgq_ref, dgk_ref, kn_ref, vp_ref, dkn_ref, dvp_ref):
        p_id, qi = pl.program_id(0), pl.program_id(1)

        @pl.when(qi == 0)
        def _():
            for j in range(GP):
                _ca_fill(j, k_ref, v_ref, gk_ref, kn_ref, vp_ref)
            dkn_ref[...] = jnp.zeros_like(dkn_ref)
            dvp_ref[...] = jnp.zeros_like(dvp_ref)
            db_ref[...] = jnp.zeros_like(db_ref)

        @pl.when(jnp.logical_and(p_id == 0, qi == 0))
        def _():
            dgq_ref[...] = jnp.zeros_like(dgq_ref)
            dgk_ref[...] = jnp.zeros_like(dgk_ref)

        low = _low_lanes(CA_T)
        top_w = lax.broadcasted_iota(jnp.int32, (PAIR, CA_W), 0) < HEAD_DIM
        band = pl.ds(pl.multiple_of(qi * CA_T, CA_T), CA_W)
        pairs = [_ca_scores(j, q_ref, b2_ref, gq_ref, kn_ref, qi, low) for j in range(GP)]
        do2 = [do_ref[:, j * PAIR:(j + 1) * PAIR] for j in range(GP)]
        dps = [_per_head(_nt(jnp.concatenate([_one_head(do2[j], low, True), _one_head(do2[j], low, False)], axis=0),
                             vp_ref[j, band, :]), CA_T) for j in range(GP)]
        probs, dsbs = [], []
        for j in range(GP):
            pj, dj = [], []
            for e in range(2):
                p = _softmax(pairs[j][0][e])
                ds = p * (dps[j][e] - jnp.sum(p * dps[j][e], axis=-1, keepdims=True))
                db_ref[2 * j + e] += ds
                pj.append(p.astype(_MXU))
                dj.append(ds.astype(_MXU))
            probs.append(pj)
            dsbs.append(dj)
        dgq = jnp.zeros((1, PAIR), F32)
        for j in range(GP):
            _, qn, qhat, r = pairs[j]
            dq_h = _per_head(_nn(jnp.concatenate(dsbs[j], axis=0), kn_ref[j, band, :]), CA_T)
            dk_t = _tn(qn, jnp.concatenate(dsbs[j], axis=1))
            dv_t = _tn(do2[j], jnp.concatenate(probs[j], axis=1))
            dkn_ref[j, :, band] += jnp.where(top_w, dk_t[:, :CA_W], dk_t[:, CA_W:])
            dvp_ref[j, :, band] += jnp.where(top_w, dv_t[:, :CA_W], dv_t[:, CA_W:])
            dqn = jnp.where(low, dq_h[0], dq_h[1]) * scale
            dgq = dgq + jnp.sum(dqn * qhat, axis=0, keepdims=True)
            dq_ref[:, j * PAIR:(j + 1) * PAIR] = _pair_norm_bwd(dqn, qhat, r, gq_ref[...], low).astype(dq_ref.dtype)
        dgq_ref[...] += dgq

        @pl.when(qi == last)
        def _():
            low_s = _low_lanes(S)
            for j in range(GP):
                cols = slice(j * PAIR, (j + 1) * PAIR)
                _, khat, rk = _pair_norm(k_ref[:, cols], gk_ref[...], low_s)
                dkn = dkn_ref[j, :, PAD:PAD + S].T
                dgk_ref[...] += jnp.sum(dkn * khat, axis=0, keepdims=True)
                dk_ref[:, cols] = _pair_norm_bwd(dkn, khat, rk, gk_ref[...], low_s).astype(dk_ref.dtype)
                dv_ref[:, cols] = dvp_ref[j, :, PAD:PAD + S].T.astype(dv_ref.dtype)

    vec = pl.BlockSpec((1, PAIR), lambda p, i: (0, 0))
    tile = pl.BlockSpec((2 * GP, CA_T, CA_W), lambda p, i: (p, 0, 0))
    full = pl.BlockSpec((S, GW), lambda p, i: (0, p))
    return _call_hosted(
        body, name, (nb, S // CA_T),
        [pl.BlockSpec((CA_T, GW), lambda p, i: (i, 3 * nb + p)),
         pl.BlockSpec((S, GW), lambda p, i: (0, 4 * nb + p)), pl.BlockSpec((S, GW), lambda p, i: (0, 5 * nb + p)),
         tile, vec, vec, pl.BlockSpec((CA_T, GW), lambda p, i: (i, nb + p))],
        [pl.BlockSpec((CA_T, GW), lambda p, i: (i, p)), full, full, tile, vec, vec],
        [jax.ShapeDtypeStruct((S, half), _MXU)] * 3
        + [jax.ShapeDtypeStruct(bias2.shape, F32), jax.ShapeDtypeStruct((1, PAIR), F32),
           jax.ShapeDtypeStruct((1, PAIR), F32)],
        [pltpu.VMEM((GP, PAD + S, PAIR), _MXU), pltpu.VMEM((GP, PAD + S, PAIR), _MXU),
         pltpu.VMEM((GP, PAIR, PAD + S), F32), pltpu.VMEM((GP, PAIR, PAD + S), F32)],
        [proj, proj, proj, bias2, gq2, gk2, dmixed], ex)


def _pack_small(parts):
    flat = jnp.concatenate([p.reshape(-1) for layer in parts for p in layer])
    n = flat.shape[0]
    n_pad = -(-n // 1024) * 1024
    return jnp.pad(flat, (0, n_pad - n)).reshape(1, n_pad)


def _unpack_small(flat, shapes):
    out, off = [], 0
    for layer in shapes:
        cur = []
        for shp in layer:
            size = 1
            for s in shp:
                size *= s
            cur.append(flat[off:off + size].reshape(shp))
            off += size
        out.append(cur)
    return out


def kernel(x, c, g_norm1, w_in, g_q, g_k, rel_bias, w_o, g_norm2, w1, w2, w_ada, b_ada, loss_target, m_g_norm1, m_w_in, m_g_q, m_g_k, m_rel_bias, m_w_o, m_g_norm2, m_w1, m_w2, m_w_ada, m_b_ada, v_g_norm1, v_w_in, v_g_q, v_g_k, v_rel_bias, v_w_o, v_g_norm2, v_w1, v_w2, v_w_ada, v_b_ada):
    L = w_in.shape[0]
    S, D = x.shape[1:]
    H2 = D // HEAD_DIM // 2
    Ca = w_ada.shape[2]
    xi, yi, ci = _pos()
    me = 4 * xi + 2 * yi + ci
    place = jnp.stack([2 * xi + yi, ci]).astype(jnp.int32)

    c_all = _all_gather_small(c, "ag_c").reshape(NDEV, D)
    b_cols = lax.dynamic_slice(b_ada, (0, me * Ca), (L, Ca))
    mod_part = _mod_partial(c_all, w_ada, b_cols, "mod_partial")
    mod_all = _all_gather_small(mod_part, "ag_mod")
    mod = lax.dynamic_index_in_dim(mod_all, me, axis=1, keepdims=False)
    mod = mod.reshape(NDEV, L, Ca).transpose(1, 0, 2).reshape(L, 6, 1, D)

    wire = lambda a: a.astype(_MXU)
    by_cols = lambda g: g.transpose(1, 0, 2).reshape(D, g.shape[0] * g.shape[2])
    W_in = {0: by_cols(_run_exchange(_gather_exchange([wire(w_in[0])]), "ag_w_in0")[0])}
    W_o, W_1, W_2 = {}, {}, {}

    xs = [x[0]]
    saved = []
    for l in range(L):
        sh1, sc1, gt1, sh2, sc2, gt2 = [mod[l, i] for i in range(6)]
        gn1, gn2 = g_norm1[l:l + 1], g_norm2[l:l + 1]
        gq2, gk2 = jnp.tile(g_q[l:l + 1], (1, 2)), jnp.tile(g_k[l:l + 1], (1, 2))
        proj, h1 = _ln_mod_matmul(xs[-1], gn1, sc1, sh1, W_in[l], f"l{l}_proj")
        (o_sb, ox_sb), got = _sb_fwd(proj, f"l{l}_sb_fwd",
                                     _gather_exchange([wire(w_o[l]), wire(w1[l]), wire(w2[l])]))
        W_o[l], W_1[l], W_2[l] = got[0].reshape(D, D), by_cols(got[1]), got[2].reshape(4 * D, D)
        bias2 = _ca_bias(rel_bias[l], f"l{l}_ca_bias")
        (o_ca,), got = _ca_fwd(proj, bias2, gq2, gk2, f"l{l}_ca_fwd",
                               _gather_exchange([wire(w_in[l + 1])]) if l + 1 < L else None)
        if got:
            W_in[l + 1] = by_cols(got[0])
        mixed = jnp.concatenate([o_sb, o_ca], axis=1)
        x1, f1 = _matmul_res_gate(mixed, W_o[l], xs[-1], gt1, False, f"l{l}_attn_out")
        u, h2 = _ln_mod_matmul(x1, gn2, sc2, sh2, W_1[l], f"l{l}_mlp_in")
        x2, f2 = _matmul_res_gate(u, W_2[l], x1, gt2, True, f"l{l}_mlp_out")
        saved.append(dict(x0=xs[-1], h1=h1, proj=proj, ox_sb=ox_sb, bias2=bias2, mixed=mixed, f1=f1, x1=x1,
                          h2=h2, u=u, f2=f2))
        xs.append(x2)

    dx, loss_part = _loss_grad(xs[-1], loss_target[0], "loss")

    owns, recv_b = {}, {}
    ready = []
    small_parts = [None] * L

    def partials(keys, grads, recv_a):
        parts = []
        for key, g, r in zip(keys, grads, recv_a):
            owns[key], part = _rs_chip_partial(place, g, r, f"rs_partial_l{key[0]}_{key[1]}")
            parts.append(part)
        return parts

    for l in reversed(range(L)):
        sv = saved[l]
        sh1, sc1, gt1, sh2, sc2, gt2 = [mod[l, i] for i in range(6)]
        gn1, gn2 = g_norm1[l:l + 1], g_norm2[l:l + 1]
        gq2, gk2 = jnp.tile(g_q[l:l + 1], (1, 2)), jnp.tile(g_k[l:l + 1], (1, 2))
        dz2, dgt2, du = _gate_nt_matmul(dx, sv["f2"], gt2, W_2[l], sv["u"], f"l{l}_mlp_out_bwd")
        gw2 = _tn_matmul(sv["u"], dz2, False, True, f"l{l}_gw2")
        gw1 = _tn_matmul(sv["h2"], du, True, False, f"l{l}_gw1")
        dx, dsh2, dsc2, dgn2 = _nt_ln_bwd(du, W_1[l], sv["x1"], gn2, sc2, sh2, dx, f"l{l}_mlp_in_bwd")
        dz1, dgt1, dmixed = _gate_nt_matmul(dx, sv["f1"], gt1, W_o[l], None, f"l{l}_attn_out_bwd")
        gwo = _tn_matmul(sv["mixed"], dz1, False, False, f"l{l}_gwo")
        ready += [((l, 1), gwo), ((l, 2), gw1), ((l, 3), gw2)]
        keys, grads = [k for k, _ in ready], [g for _, g in ready]
        (dq_sb, dk_sb, dv_sb), recv_a = _sb_bwd(sv["proj"], sv["ox_sb"], dmixed, f"l{l}_sb_bwd",
                                                _sibling_exchange(grads))
        parts = partials(keys, grads, recv_a)
        (dq_ca, dk_ca, dv_ca, dbias2, dgq2, dgk2), got = _ca_bwd(sv["proj"], sv["bias2"], gq2, gk2, dmixed,
                                                                 f"l{l}_ca_bwd", _chip_exchange(parts))
        recv_b.update(zip(keys, got))
        dgq = dgq2[:, :HEAD_DIM] + dgq2[:, HEAD_DIM:]
        dgk = dgk2[:, :HEAD_DIM] + dgk2[:, HEAD_DIM:]
        drb = _ca_bias_bwd(dbias2, f"l{l}_ca_bias_bwd")
        dproj = jnp.concatenate([dq_sb, dk_sb, dv_sb, dq_ca, dk_ca, dv_ca], axis=1)
        gwin = _tn_matmul(sv["h1"], dproj, True, False, f"l{l}_gwin")
        ready = [((l, 0), gwin)]
        dx, dsh1, dsc1, dgn1 = _nt_ln_bwd(dproj, W_in[l], sv["x0"], gn1, sc1, sh1, dx, f"l{l}_proj_bwd")
        dmod = jnp.concatenate([dsh1, dsc1, dgt1, dsh2, dsc2, dgt2], axis=1)
        small_parts[l] = [dgn1, dgq, dgk, drb, dgn2, dmod]
    grad_x = dx[None]

    keys, grads = [k for k, _ in ready], [g for _, g in ready]
    parts = partials(keys, grads, _run_exchange(_sibling_exchange(grads), "rs_sibling_last"))
    recv_b.update(zip(keys, _run_exchange(_chip_exchange(parts), "rs_chips_last")))
    big_out = []
    for t, (w, m, v) in enumerate([(w_in, m_w_in, v_w_in), (w_o, m_w_o, v_w_o), (w1, m_w1, v_w1), (w2, m_w2, v_w2)]):
        big_out.append(_rs_sum_adamw([owns[(l, t)] for l in range(L)], [recv_b[(l, t)] for l in range(L)],
                                     w, m, v, f"adamw_big_{t}"))

    packed = _pack_small(small_parts)
    gathered_small = _all_gather_small(packed, "ag_small_grads")
    small_sum = _sum_devices(gathered_small, "sum_small_grads")
    shapes = [[(1, D), (1, HEAD_DIM), (1, HEAD_DIM), (H2, N_REL), (1, D), (1, 6 * D)]] * L
    names = ["g_norm1", "g_q", "g_k", "rel_bias", "g_norm2", "b_ada"]
    small_w = {"g_norm1": (g_norm1, m_g_norm1, v_g_norm1), "g_q": (g_q, m_g_q, v_g_q), "g_k": (g_k, m_g_k, v_g_k),
               "rel_bias": (rel_bias, m_rel_bias, v_rel_bias), "g_norm2": (g_norm2, m_g_norm2, v_g_norm2),
               "b_ada": (b_ada, m_b_ada, v_b_ada)}
    packs = [_pack_small([[small_w[n][k][l] for n in names] for l in range(L)]) for k in range(3)]
    n_pad = packed.shape[1]
    as_rows = lambda a: a.reshape(n_pad // 128, 128)
    sd, sm, sv_ = _adamw(as_rows(packs[0]), as_rows(small_sum), as_rows(packs[1]), as_rows(packs[2]), "adamw_small")
    small_out = {}
    for key, flat in [("grad", small_sum), ("delta", sd), ("m", sm), ("v", sv_)]:
        per_layer = _unpack_small(flat.reshape(-1), shapes)
        for i, n in enumerate(names):
            small_out[(key, n)] = jnp.stack([per_layer[l][i].reshape(small_w[n][0].shape[1:]) for l in range(L)])

    layer_len = 2 * D + 2 * HEAD_DIM + H2 * N_REL + 6 * D
    rows = gathered_small.reshape(NDEV, n_pad)
    dmod_all = jnp.stack([rows[:, l * layer_len + layer_len - 6 * D:(l + 1) * layer_len] for l in range(L)])
    dmod_cols = lax.dynamic_slice(dmod_all, (0, 0, me * Ca), (L, NDEV, Ca))
    dmod_cols = jnp.pad(dmod_cols, ((0, 0), (0, 128 - NDEV), (0, 0)))
    c_t = jnp.pad(c_all.T, ((0, 0), (0, 128 - NDEV)))
    g_ada = _w_ada_grad(c_t, dmod_cols, "w_ada_grad")
    flat2 = lambda a: a.reshape(L * D, Ca)
    ad, am, av = _adamw(flat2(w_ada), flat2(g_ada), flat2(m_w_ada), flat2(v_w_ada), "adamw_w_ada")
    ada_out = [g_ada] + [a.reshape(L, D, Ca) for a in (ad, am, av)]

    def leaf(kind):
        k = {"grad": 0, "delta": 1, "m": 2, "v": 3}[kind]
        return [small_out[(kind, "g_norm1")], big_out[0][k], small_out[(kind, "g_q")], small_out[(kind, "g_k")],
                small_out[(kind, "rel_bias")], big_out[1][k], small_out[(kind, "g_norm2")], big_out[2][k],
                big_out[3][k], ada_out[k], small_out[(kind, "b_ada")]]

    loss = lax.psum(loss_part[0, 0], ("x", "y", "c"))
    return (loss, grad_x, *leaf("grad"), *leaf("delta"), *leaf("m"), *leaf("v"))
```

```python
import functools

import jax
import jax.numpy as jnp
from jax import lax
from jax.experimental import pallas as pl
from jax.experimental.pallas import tpu as pltpu

F32 = jnp.float32
_MXU = jnp.bfloat16

HEAD_DIM = 64
CHUNK = 64
LEFT_CHUNKS = 8
PAD = LEFT_CHUNKS * CHUNK
BAND = PAD + CHUNK
REL_CLIP = 128
N_REL = 2 * REL_CLIP + 1
EPS = 1e-6
NEG = -1e30
NDEV = 8
SB_T = 128
CA_T = 2 * CHUNK
CA_W = CA_T + PAD
SB_SKIP = -104.0
PAIR = 2 * HEAD_DIM
SB_PAIRS = 4
CA_PAIRS_FWD = 4
CA_PAIRS_BWD = 2
ROW_BLOCK = 512
SKEW_W = 767

ADAM_LR, ADAM_B1, ADAM_B2, ADAM_EPS, ADAM_WD, ADAM_STEP = 0.001, 0.9, 0.999, 1e-08, 0.01, 10

MESH = pl.DeviceIdType.MESH
VMEM_SPEC = pl.BlockSpec(memory_space=pltpu.VMEM)
SMEM_SPEC = pl.BlockSpec(memory_space=pltpu.SMEM)
ANY_SPEC = pl.BlockSpec(memory_space=pl.ANY)


def _nn(a, b):
    return lax.dot_general(a, b, (((1,), (0,)), ((), ())), preferred_element_type=F32)


def _nt(a, b):
    return lax.dot_general(a, b, (((1,), (1,)), ((), ())), preferred_element_type=F32)


def _tn(a, b):
    return lax.dot_general(a, b, (((0,), (0,)), ((), ())), preferred_element_type=F32)


def _blk(n, pref):
    return pref if n % pref == 0 else n


def _pos():
    return lax.axis_index("x"), lax.axis_index("y"), lax.axis_index("c")


def _flip(v, bit):
    return 1 - v if bit else v


def _all_gather_small(blk, name):
    R, C = blk.shape

    def body(x_ref, out_ref, send_sems, recv_sems):
        x, y, c = _pos()
        me = 4 * x + 2 * y + c

        def peer(k):
            return (_flip(x, k & 4), _flip(y, k & 2), _flip(c, k & 1))

        def copy(k, slot):
            return pltpu.make_async_remote_copy(
                src_ref=x_ref, dst_ref=out_ref.at[slot], send_sem=send_sems.at[k - 1],
                recv_sem=recv_sems.at[k - 1], device_id=peer(k), device_id_type=MESH)

        out_ref[pl.ds(me, 1), :, :] = x_ref[...].reshape(1, R, C)
        sends = [copy(k, me) for k in range(1, NDEV)]
        for cp in sends:
            cp.start()
        for k in range(1, NDEV):
            px, py, pc = peer(k)
            copy(k, 4 * px + 2 * py + pc).wait_recv()
        for cp in sends:
            cp.wait_send()

    return pl.pallas_call(
        body, name=name,
        out_shape=jax.ShapeDtypeStruct((NDEV, R, C), blk.dtype),
        in_specs=[VMEM_SPEC], out_specs=VMEM_SPEC,
        scratch_shapes=[pltpu.SemaphoreType.DMA((NDEV - 1,)), pltpu.SemaphoreType.DMA((NDEV - 1,))],
    )(blk)


class _Exchange:
    def __init__(self, inputs, out_shapes, sems, start, finish, middle=None):
        self.inputs, self.out_shapes, self.sems = list(inputs), list(out_shapes), list(sems)
        self.start, self.middle, self.finish = start, middle, finish


def _run_exchange(ex, name):
    n_in, n_out = len(ex.inputs), len(ex.out_shapes)

    def body(*refs):
        ins, outs, sems = refs[:n_in], refs[n_in:n_in + n_out], refs[n_in + n_out:]
        ex.start(ins, outs, sems)
        if ex.middle is not None:
            ex.middle(ins, outs, sems)
        ex.finish(ins, outs, sems)

    return pl.pallas_call(
        body, name=name, out_shape=ex.out_shapes, in_specs=[ANY_SPEC] * n_in, out_specs=[ANY_SPEC] * n_out,
        scratch_shapes=ex.sems,
    )(*ex.inputs)


def _hosted(body, n_in, n_out, ex, step, steps):
    if ex is None:
        return body
    xi, xo = len(ex.inputs), len(ex.out_shapes)

    def wrapped(*refs):
        own_in, ex_in = refs[:n_in], refs[n_in:n_in + xi]
        rest = refs[n_in + xi:]
        own_out, ex_out = rest[:n_out], rest[n_out:n_out + xo]
        rest = rest[n_out + xo:]
        own_scratch, ex_sems = rest[:len(rest) - len(ex.sems)], rest[len(rest) - len(ex.sems):]
        t = step()
        pl.when(t == 0)(lambda: ex.start(ex_in, ex_out, ex_sems))
        body(*own_in, *own_out, *own_scratch)
        if ex.middle is not None:
            pl.when(t == (steps * 3) // 5)(lambda: ex.middle(ex_in, ex_out, ex_sems))
        pl.when(t == steps - 1)(lambda: ex.finish(ex_in, ex_out, ex_sems))

    return wrapped


def _call_hosted(body, name, grid, in_specs, out_specs, out_shape, scratch, args, ex):
    n_in, n_out = len(in_specs), len(out_specs)
    steps = 1
    for extent in grid:
        steps *= extent

    def step():
        t = pl.program_id(0)
        for axis in range(1, len(grid)):
            t = t * grid[axis] + pl.program_id(axis)
        return t

    if ex is not None:
        in_specs = in_specs + [ANY_SPEC] * len(ex.inputs)
        out_specs = out_specs + [ANY_SPEC] * len(ex.out_shapes)
        out_shape = out_shape + ex.out_shapes
        scratch = scratch + ex.sems
        args = args + ex.inputs
    outs = pl.pallas_call(
        _hosted(body, n_in, n_out, ex, step, steps), name=name, grid=grid, in_specs=in_specs, out_specs=out_specs,
        out_shape=out_shape, scratch_shapes=scratch,
    )(*args)
    return list(outs[:n_out]), list(outs[n_out:])


def _gather_exchange(shards):
    n = len(shards)

    def setup(ins, outs, sems):
        send_sems, recv_sems, local_sems = sems
        x, y, c = _pos()
        me, sibling = (x, y, c), (x, y, 1 - c)
        chips = [(1 - x, y), (x, 1 - y), (1 - x, 1 - y)]

        def copy(i, k, block, to, src=None):
            px, py, pc = block
            dst = outs[i].at[4 * px + 2 * py + pc]
            return pltpu.make_async_remote_copy(
                src_ref=dst if src is None else src, dst_ref=dst, send_sem=send_sems.at[7 * i + k],
                recv_sem=recv_sems.at[7 * i + k], device_id=to, device_id_type=MESH)

        def mine(i):
            return pltpu.make_async_copy(ins[i], outs[i].at[4 * x + 2 * y + c], local_sems.at[i])

        def first(i):
            return [copy(i, 0, me, sibling, src=ins[i])] + [
                copy(i, 1 + j, me, (*chip, c), src=ins[i]) for j, chip in enumerate(chips)]

        def passed(i, j):
            return copy(i, 4 + j, (*chips[j], c), sibling)

        return me, sibling, chips, c, copy, mine, first, passed

    def start(ins, outs, sems):
        _, _, _, _, _, mine, first, _ = setup(ins, outs, sems)
        for i in range(n):
            mine(i).start()
            for cp in first(i):
                cp.start()

    def middle(ins, outs, sems):
        me, _, chips, c, copy, _, _, passed = setup(ins, outs, sems)
        for j, chip in enumerate(chips):
            for i in range(n):
                copy(i, 1 + j, (*chip, c), me).wait_recv()
                passed(i, j).start()

    def finish(ins, outs, sems):
        me, sibling, chips, c, copy, mine, first, passed = setup(ins, outs, sems)
        for i in range(n):
            copy(i, 0, sibling, me).wait_recv()
            for j, chip in enumerate(chips):
                copy(i, 4 + j, (*chip, 1 - c), me).wait_recv()
        for i in range(n):
            for cp in first(i) + [passed(i, j) for j in range(3)]:
                cp.wait_send()
            mine(i).wait()

    return _Exchange(
        shards, [jax.ShapeDtypeStruct((NDEV,) + s.shape, s.dtype) for s in shards],
        [pltpu.SemaphoreType.DMA((7 * n,)), pltpu.SemaphoreType.DMA((7 * n,)), pltpu.SemaphoreType.DMA((n,))],
        start, finish, middle)


def _sibling_exchange(grads):
    n = len(grads)

    def copies(ins, outs, sems):
        send_sems, recv_sems = sems
        x, y, c = _pos()
        return [pltpu.make_async_remote_copy(
            src_ref=ins[i].at[2 * q + (1 - c)], dst_ref=outs[i].at[q], send_sem=send_sems.at[4 * i + q],
            recv_sem=recv_sems.at[4 * i + q], device_id=(x, y, 1 - c), device_id_type=MESH)
            for i in range(n) for q in range(4)]

    def start(ins, outs, sems):
        for cp in copies(ins, outs, sems):
            cp.start()

    def finish(ins, outs, sems):
        for cp in copies(ins, outs, sems):
            cp.wait()

    return _Exchange(
        grads, [jax.ShapeDtypeStruct((4,) + g.shape[1:], g.dtype) for g in grads],
        [pltpu.SemaphoreType.DMA((4 * n,)), pltpu.SemaphoreType.DMA((4 * n,))], start, finish)


def _chip_exchange(parts):
    n = len(parts)

    def copies(ins, outs, sems):
        send_sems, recv_sems = sems
        x, y, c = _pos()
        return [pltpu.make_async_remote_copy(
            src_ref=ins[i].at[j - 1], dst_ref=outs[i].at[j - 1], send_sem=send_sems.at[3 * i + j - 1],
            recv_sem=recv_sems.at[3 * i + j - 1], device_id=(_flip(x, j & 2), _flip(y, j & 1), c),
            device_id_type=MESH) for i in range(n) for j in range(1, 4)]

    def start(ins, outs, sems):
        for cp in copies(ins, outs, sems):
            cp.start()

    def finish(ins, outs, sems):
        for cp in copies(ins, outs, sems):
            cp.wait()

    return _Exchange(
        parts, [jax.ShapeDtypeStruct(p.shape, p.dtype) for p in parts],
        [pltpu.SemaphoreType.DMA((3 * n,)), pltpu.SemaphoreType.DMA((3 * n,))], start, finish)


def _rs_chip_partial(place, grad, recv, name):
    _, R, C = grad.shape
    tr = _blk(R, 256)

    def body(place_ref, *refs):
        g_refs, r_refs = refs[:4], refs[4:8]
        own_ref, out_ref = refs[8:]
        own_ref[...] = g_refs[0][0] + r_refs[0][0]
        for j in range(1, 4):
            out_ref[j - 1] = (g_refs[j][0] + r_refs[j][0]).astype(out_ref.dtype)

    def g_map(j):
        return lambda i, p: (2 * jnp.bitwise_xor(p[0], j) + p[1], i, 0)

    def r_map(j):
        return lambda i, p: (jnp.bitwise_xor(p[0], j), i, 0)

    grid_spec = pltpu.PrefetchScalarGridSpec(
        num_scalar_prefetch=1, grid=(R // tr,),
        in_specs=[pl.BlockSpec((1, tr, C), g_map(j)) for j in range(4)]
        + [pl.BlockSpec((1, tr, C), r_map(j)) for j in range(4)],
        out_specs=[pl.BlockSpec((tr, C), lambda i, p: (i, 0)), pl.BlockSpec((3, tr, C), lambda i, p: (0, i, 0))])
    return pl.pallas_call(
        body, name=name, grid_spec=grid_spec,
        out_shape=[jax.ShapeDtypeStruct((R, C), F32), jax.ShapeDtypeStruct((3, R, C), _MXU)],
    )(place, *([grad] * 4), *([recv] * 4))


def _adamw_math(w, g, m, v):
    m = ADAM_B1 * m + (1.0 - ADAM_B1) * g
    v = ADAM_B2 * v + (1.0 - ADAM_B2) * (g * g)
    m_hat = m / (1.0 - ADAM_B1 ** ADAM_STEP)
    v_hat = v / (1.0 - ADAM_B2 ** ADAM_STEP)
    delta = -ADAM_LR * (m_hat / (jnp.sqrt(v_hat) + ADAM_EPS) + ADAM_WD * w)
    return delta, m, v


def _adamw(w, g, m, v, name):
    R, C = w.shape
    tr = _blk(R, 256)

    def body(w_ref, g_ref, m_ref, v_ref, d_ref, nm_ref, nv_ref):
        d, nm, nv = _adamw_math(w_ref[...], g_ref[...], m_ref[...], v_ref[...])
        d_ref[...] = d
        nm_ref[...] = nm
        nv_ref[...] = nv

    spec = pl.BlockSpec((tr, C), lambda i: (i, 0))
    return pl.pallas_call(
        body, name=name, grid=(R // tr,), in_specs=[spec] * 4, out_specs=[spec] * 3,
        out_shape=[jax.ShapeDtypeStruct((R, C), F32)] * 3,
    )(w, g, m, v)


def _rs_sum_adamw(owns, recvs, w, m, v, name):
    L, R, C = w.shape
    tr = _blk(R, 256)
    nr = R // tr

    def body(o0, o1, r0, r1, w_ref, m_ref, v_ref, g_ref, d_ref, nm_ref, nv_ref):
        def step(o_ref, r_ref):
            g = o_ref[...]
            for j in range(3):
                g = g + r_ref[j].astype(F32)
            d, nm, nv = _adamw_math(w_ref[0], g, m_ref[0], v_ref[0])
            g_ref[0] = g
            d_ref[0] = d
            nm_ref[0] = nm
            nv_ref[0] = nv

        pl.when(pl.program_id(0) == 0)(lambda: step(o0, r0))
        pl.when(pl.program_id(0) == 1)(lambda: step(o1, r1))

    def hold(layer):
        if layer == 0:
            return lambda l, i: i * (1 - l) + (nr - 1) * l
        return lambda l, i: i * l

    own_spec = [pl.BlockSpec((tr, C), functools.partial(lambda l, i, f: (f(l, i), 0), f=hold(k))) for k in range(2)]
    recv_spec = [pl.BlockSpec((3, tr, C), functools.partial(lambda l, i, f: (0, f(l, i), 0), f=hold(k)))
                 for k in range(2)]
    lay = pl.BlockSpec((1, tr, C), lambda l, i: (l, i, 0))
    return pl.pallas_call(
        body, name=name, grid=(L, nr),
        in_specs=own_spec + recv_spec + [lay] * 3, out_specs=[lay] * 4,
        out_shape=[jax.ShapeDtypeStruct((L, R, C), F32)] * 4,
    )(owns[0], owns[1], recvs[0], recvs[1], w, m, v)


def _silu(x):
    return x / (1.0 + jnp.exp(-x))


def _mod_partial(c_all, w_ada, b_cols, name):
    L, D, Ca = w_ada.shape

    def body(c_ref, w_ref, b_ref, o_ref):
        act = _silu(c_ref[...]).astype(_MXU)
        for l in range(L):
            o_ref[:, l * Ca:(l + 1) * Ca] = _nn(act, w_ref[l].astype(_MXU)) + b_ref[l:l + 1, :]

    return pl.pallas_call(
        body, name=name, out_shape=jax.ShapeDtypeStruct((NDEV, L * Ca), F32),
        in_specs=[VMEM_SPEC] * 3, out_specs=VMEM_SPEC,
    )(c_all, w_ada, b_cols)


def _w_ada_grad(c_t, dmod_cols, name):
    L, _, Ca = dmod_cols.shape
    D = c_t.shape[0]

    def body(c_ref, d_ref, o_ref):
        act = _silu(c_ref[...]).astype(_MXU)
        for l in range(L):
            o_ref[l] = _nn(act, d_ref[l].astype(_MXU))

    return pl.pallas_call(
        body, name=name, out_shape=jax.ShapeDtypeStruct((L, D, Ca), F32),
        in_specs=[VMEM_SPEC] * 2, out_specs=VMEM_SPEC,
    )(c_t, dmod_cols)


def _sum_devices(gathered, name):
    _, _, N = gathered.shape

    def body(x_ref, o_ref):
        acc = x_ref[0]
        for d in range(1, NDEV):
            acc = acc + x_ref[d]
        o_ref[...] = acc

    return pl.pallas_call(
        body, name=name, out_shape=jax.ShapeDtypeStruct((1, N), F32),
        in_specs=[VMEM_SPEC], out_specs=VMEM_SPEC,
    )(gathered)


def _ln_mod_matmul(x, g, sc, sh, w, name):
    S, D = x.shape
    N = w.shape[1]
    tm = _blk(S, ROW_BLOCK)

    def body(x_ref, g_ref, sc_ref, sh_ref, w_ref, o_ref, h_ref):
        xv = x_ref[...]
        r = lax.rsqrt(jnp.mean(xv * xv, axis=-1, keepdims=True) + EPS)
        hv = ((xv * r) * g_ref[...]) * (1.0 + sc_ref[...]) + sh_ref[...]
        hb = hv.astype(_MXU)
        h_ref[...] = hb
        o_ref[...] = _nn(hb, w_ref[...]).astype(o_ref.dtype)

    vec = pl.BlockSpec((1, D), lambda i: (0, 0))
    row = lambda width: pl.BlockSpec((tm, width), lambda i: (i, 0))
    return pl.pallas_call(
        body, name=name, grid=(S // tm,),
        in_specs=[row(D), vec, vec, vec, pl.BlockSpec((D, N), lambda i: (0, 0))],
        out_specs=[row(N), row(D)],
        out_shape=[jax.ShapeDtypeStruct((S, N), _MXU), jax.ShapeDtypeStruct((S, D), _MXU)],
    )(x, g, sc, sh, w)


def _matmul_res_gate(a, w, xres, gt, relu2, name):
    S, K = a.shape
    N = w.shape[1]
    tm = _blk(S, 512)

    def body(a_ref, w_ref, x_ref, gt_ref, o_ref, f_ref):
        av = a_ref[...]
        if relu2:
            af = jnp.maximum(av.astype(F32), 0.0)
            av = (af * af).astype(_MXU)
        f = _nn(av, w_ref[...])
        f_ref[...] = f.astype(f_ref.dtype)
        o_ref[...] = x_ref[...] + gt_ref[...] * f

    row = lambda width: pl.BlockSpec((tm, width), lambda i: (i, 0))
    return pl.pallas_call(
        body, name=name, grid=(S // tm,),
        in_specs=[row(K), pl.BlockSpec((K, N), lambda i: (0, 0)), row(N), pl.BlockSpec((1, N), lambda i: (0, 0))],
        out_specs=[row(N), row(N)],
        out_shape=[jax.ShapeDtypeStruct((S, N), F32), jax.ShapeDtypeStruct((S, N), _MXU)],
    )(a, w, xres, gt)


def _loss_grad(y, t, name):
    S, D = y.shape
    tm = _blk(S, 512)
    last = S // tm - 1

    def body(y_ref, t_ref, dy_ref, l_ref, acc_ref):
        i = pl.program_id(0)
        e = y_ref[...] - t_ref[...]
        dy_ref[...] = e * (1.0 / D)
        part = jnp.sum(e * e, axis=0, keepdims=True)

        @pl.when(i == 0)
        def _():
            acc_ref[...] = part

        @pl.when(i > 0)
        def _():
            acc_ref[...] += part

        @pl.when(i == last)
        def _():
            l_ref[...] = (0.5 / D) * jnp.sum(acc_ref[...], axis=1, keepdims=True)

    row = pl.BlockSpec((tm, D), lambda i: (i, 0))
    return pl.pallas_call(
        body, name=name, grid=(S // tm,), in_specs=[row, row],
        out_specs=[row, pl.BlockSpec((1, 1), lambda i: (0, 0))],
        out_shape=[jax.ShapeDtypeStruct((S, D), F32), jax.ShapeDtypeStruct((1, 1), F32)],
        scratch_shapes=[pltpu.VMEM((1, D), F32)],
    )(y, t)


def _accumulate(ref, part, first):
    @pl.when(first)
    def _():
        ref[...] = part

    @pl.when(jnp.logical_not(first))
    def _():
        ref[...] += part


def _gate_nt_matmul(dx, f, gt, w, u, name):
    S, D = dx.shape
    N = w.shape[0]
    tm = _blk(S, ROW_BLOCK)
    with_u = u is not None

    def body(*refs):
        if with_u:
            dx_ref, f_ref, gt_ref, w_ref, u_ref, dz_ref, dgt_ref, res_ref = refs
        else:
            dx_ref, f_ref, gt_ref, w_ref, dz_ref, dgt_ref, res_ref = refs
        dxv = dx_ref[...]
        dz = (dxv * gt_ref[...]).astype(_MXU)
        dz_ref[...] = dz
        _accumulate(dgt_ref, jnp.sum(dxv * f_ref[...].astype(F32), axis=0, keepdims=True), pl.program_id(0) == 0)
        r = _nt(dz, w_ref[...])
        if with_u:
            r = r * (2.0 * jnp.maximum(u_ref[...].astype(F32), 0.0))
        res_ref[...] = r.astype(res_ref.dtype)

    row = lambda width: pl.BlockSpec((tm, width), lambda i: (i, 0))
    in_specs = [row(D), row(D), pl.BlockSpec((1, D), lambda i: (0, 0)), pl.BlockSpec((N, D), lambda i: (0, 0))]
    args = [dx, f, gt, w]
    if with_u:
        in_specs.append(row(N))
        args.append(u)
    return pl.pallas_call(
        body, name=name, grid=(S // tm,), in_specs=in_specs,
        out_specs=[row(D), pl.BlockSpec((1, D), lambda i: (0, 0)), row(N)],
        out_shape=[jax.ShapeDtypeStruct((S, D), _MXU), jax.ShapeDtypeStruct((1, D), F32),
                   jax.ShapeDtypeStruct((S, N), _MXU)],
    )(*args)


def _tn_matmul(a, b, by_col, relu2, name):
    S, Ka = a.shape
    Nb = b.shape[1]
    ts = _blk(S, 2 * ROW_BLOCK)
    half = NDEV // 2
    if by_col:
        R, C = Ka, Nb // NDEV
        a_spec = pl.BlockSpec((ts, Ka), lambda h, k: (k, 0))
        b_spec = pl.BlockSpec((ts, half * C), lambda h, k: (k, h))
    else:
        R, C = Ka // NDEV, Nb
        a_spec = pl.BlockSpec((ts, half * R), lambda h, k: (k, h))
        b_spec = pl.BlockSpec((ts, Nb), lambda h, k: (k, 0))

    def body(a_ref, b_ref, o_ref):
        av = a_ref[...]
        if relu2:
            af = jnp.maximum(av.astype(F32), 0.0)
            av = (af * af).astype(_MXU)
        p = _tn(av, b_ref[...])
        first = pl.program_id(1) == 0
        for d in range(half):
            part = p[:, d * C:(d + 1) * C] if by_col else p[d * R:(d + 1) * R, :]
            _accumulate(o_ref.at[d], part, first)

    return pl.pallas_call(
        body, name=name, grid=(NDEV // half, S // ts), in_specs=[a_spec, b_spec],
        out_specs=pl.BlockSpec((half, R, C), lambda h, k: (h, 0, 0)),
        out_shape=jax.ShapeDtypeStruct((NDEV, R, C), F32),
    )(a, b)


def _nt_ln_bwd(dy, w, x, g, sc, sh, dxres, name):
    S, D = x.shape
    N = w.shape[1]
    tm = _blk(S, ROW_BLOCK)

    def body(dy_ref, w_ref, x_ref, g_ref, sc_ref, sh_ref, dxr_ref, dx_ref, dsh_ref, dsc_ref, dg_ref):
        dh = _nt(dy_ref[...], w_ref[...])
        xv = x_ref[...]
        r = lax.rsqrt(jnp.mean(xv * xv, axis=-1, keepdims=True) + EPS)
        xhat = xv * r
        gv = g_ref[...]
        dn = dh * (1.0 + sc_ref[...])
        dxhat = dn * gv
        dxv = r * (dxhat - xhat * jnp.mean(dxhat * xhat, axis=-1, keepdims=True))
        dx_ref[...] = dxr_ref[...] + dxv
        first = pl.program_id(0) == 0
        _accumulate(dsh_ref, jnp.sum(dh, axis=0, keepdims=True), first)
        _accumulate(dsc_ref, jnp.sum(dh * (xhat * gv), axis=0, keepdims=True), first)
        _accumulate(dg_ref, jnp.sum(dn * xhat, axis=0, keepdims=True), first)

    row = lambda width: pl.BlockSpec((tm, width), lambda i: (i, 0))
    vec = pl.BlockSpec((1, D), lambda i: (0, 0))
    return pl.pallas_call(
        body, name=name, grid=(S // tm,),
        in_specs=[row(N), pl.BlockSpec((D, N), lambda i: (0, 0)), row(D), vec, vec, vec, row(D)],
        out_specs=[row(D), vec, vec, vec],
        out_shape=[jax.ShapeDtypeStruct((S, D), F32)] + [jax.ShapeDtypeStruct((1, D), F32)] * 3,
    )(dy, w, x, g, sc, sh, dxres)


def _split2(v):
    hi = v.astype(_MXU)
    mid = (v - hi.astype(F32)).astype(_MXU)
    return hi, mid


def _tri_sums(vs, tri2):
    T = vs[0].shape[0]
    out = []
    for j in range(len(vs) // 2):
        wide = [jnp.concatenate(_split2(vs[2 * j + e]), axis=1) for e in range(2)]
        for both in _per_head(_nn(jnp.concatenate(wide, axis=0), tri2), T):
            out.append((both[:, :T], both[:, T:]))
    return out


def _tri2(T, inclusive):
    j = lax.broadcasted_iota(jnp.int32, (2 * T, 2 * T), 0) % T
    s = lax.broadcasted_iota(jnp.int32, (2 * T, 2 * T), 1)
    keep = (j >= s) if inclusive else (j > s)
    return jnp.where((s >= T) | keep, 1.0, 0.0).astype(_MXU)


def _log_sigmoid(z):
    return jnp.minimum(z, 0.0) - jnp.log(1.0 + jnp.exp(-jnp.abs(z)))


def _per_head(tall, T):
    return [tall[h * T:(h + 1) * T] for h in range(tall.shape[0] // T)]


def _sb_blocks(q_tall, k2, strict, tri2, carry):
    T = k2[0].shape[0]
    zs = []
    for qt, kblk in zip(q_tall, k2):
        zs += _per_head(_nt(qt, kblk), T)
    lbs, l1s = [], []
    for z in zs:
        lb = _log_sigmoid(z)
        l1 = lb - z
        if strict is not None:
            l1 = jnp.where(strict, l1, 0.0)
        lbs.append(lb)
        l1s.append(l1)
    sums = _tri_sums(l1s, tri2)
    amps, new_carry = [], []
    for lb, (sfx, tot), c in zip(lbs, sums, carry):
        a = jnp.exp(lb + sfx + c)
        if strict is not None:
            a = jnp.where(strict, a, 0.0)
        amps.append(a)
        new_carry.append(c + tot)
    return lbs, amps, new_carry


def _sb_alive(carry):
    top = carry[0]
    for c in carry[1:]:
        top = jnp.maximum(top, c)
    return jnp.max(top) > SB_SKIP


def _skew_index():
    i = lax.broadcasted_iota(jnp.int32, (CA_T, SKEW_W + 1), 0)
    m = lax.broadcasted_iota(jnp.int32, (CA_T, SKEW_W + 1), 1)
    wrapped = i + m >= SKEW_W
    row = jnp.where(wrapped, i + 1, i)
    j = jnp.where(wrapped, i + m - SKEW_W, i + m)
    a = row // CHUNK
    jj = j - a * CHUNK
    inband = (jj >= 0) & (jj < BAND) & (j < CA_W) & (row < CA_T)
    idx = jnp.clip((row - a * CHUNK) + PAD - jj, -REL_CLIP, REL_CLIP) + REL_CLIP
    return inband, idx, wrapped


def _skew(tile):
    H = tile.shape[0]
    flat = jnp.pad(tile, ((0, 0), (0, 0), (0, SKEW_W - CA_W))).reshape(H, CA_T * SKEW_W)
    return jnp.pad(flat, ((0, 0), (0, CA_T))).reshape(H, CA_T, SKEW_W + 1)


def _unskew(view):
    H = view.shape[0]
    flat = view.reshape(H, CA_T * (SKEW_W + 1))[:, :CA_T * SKEW_W]
    return flat.reshape(H, CA_T, SKEW_W)[:, :, :CA_W]


def _ca_bias(rel_bias, name):
    H = rel_bias.shape[0]
    top = rel_bias[:, N_REL - 1:]
    by_offset = jnp.concatenate(
        [jnp.broadcast_to(top, (H, PAD - REL_CLIP + 1)), jnp.flip(rel_bias[:, :N_REL - 1], axis=1),
         jnp.broadcast_to(top, (H, SKEW_W + 1 - (PAD - REL_CLIP + 1) - (N_REL - 1)))], axis=1)

    def body(t_ref, o_ref):
        inband, _, wrapped = _skew_index()
        vals = jnp.where(wrapped, t_ref[0][:, 0:1], t_ref[0])
        o_ref[0] = jnp.where(inband, vals, NEG)

    view = pl.pallas_call(
        body, name=name, grid=(H,), in_specs=[pl.BlockSpec((1, 1, SKEW_W + 1), lambda h: (h, 0, 0))],
        out_specs=pl.BlockSpec((1, CA_T, SKEW_W + 1), lambda h: (h, 0, 0)),
        out_shape=jax.ShapeDtypeStruct((H, CA_T, SKEW_W + 1), F32),
    )(by_offset.reshape(H, 1, SKEW_W + 1))
    return _unskew(view)


def _ca_bias_bwd(dbias, name):
    H = dbias.shape[0]

    def body(d_ref, o_ref):
        inband, idx, _ = _skew_index()
        d = jnp.where(inband, d_ref[0], 0.0)
        clipped = idx == N_REL - 1
        by_offset = jnp.sum(jnp.where(clipped, 0.0, d), axis=0, keepdims=True)
        top = jnp.sum(jnp.sum(jnp.where(clipped, d, 0.0), axis=0, keepdims=True), axis=1, keepdims=True)
        lane = lax.broadcasted_iota(jnp.int32, (1, SKEW_W + 1), 1)
        o_ref[0] = jnp.where(lane == 0, top, by_offset)

    out = pl.pallas_call(
        body, name=name, grid=(H,), in_specs=[pl.BlockSpec((1, CA_T, SKEW_W + 1), lambda h: (h, 0, 0))],
        out_specs=pl.BlockSpec((1, 1, SKEW_W + 1), lambda h: (h, 0, 0)),
        out_shape=jax.ShapeDtypeStruct((H, 1, SKEW_W + 1), F32),
    )(_skew(dbias))[:, 0]
    first = PAD - REL_CLIP + 1
    return jnp.concatenate([jnp.flip(out[:, first:first + N_REL - 1], axis=1), out[:, 0:1]], axis=1)


def _low_lanes(rows):
    return lax.broadcasted_iota(jnp.int32, (rows, PAIR), 1) < HEAD_DIM


def _one_head(t2, low, first, scale=1.0):
    tf = t2.astype(F32) * scale
    return (jnp.where(low, tf, 0.0) if first else jnp.where(low, 0.0, tf)).astype(_MXU)


def _two_heads(t2, low, scale=1.0):
    return jnp.concatenate([_one_head(t2, low, True, scale), _one_head(t2, low, False, scale)], axis=0)


def _sb_fwd(proj, name, ex=None):
    S, W = proj.shape
    half = W // 6
    npair = half // PAIR
    T = _blk(S, SB_T)
    GP = _blk(npair, SB_PAIRS)
    GW = GP * PAIR
    nb = npair // GP

    def body(q_ref, k_ref, v_ref, o_ref, ox_ref):
        qi = pl.program_id(1)
        low = _low_lanes(T)
        q_tall = []
        for j in range(GP):
            q2 = q_ref[:, j * PAIR:(j + 1) * PAIR]
            q_tall.append(_two_heads(q2, low, HEAD_DIM ** -0.5))
        row = lax.broadcasted_iota(jnp.int32, (T, T), 0)
        col = lax.broadcasted_iota(jnp.int32, (T, T), 1)
        tri2 = _tri2(T, inclusive=False)

        def pairs(kb, carry, acc, fine, strict):
            rows = pl.ds(pl.multiple_of(kb * T, T), T)
            k2 = [k_ref[rows, j * PAIR:(j + 1) * PAIR] for j in range(GP)]
            v2 = [v_ref[rows, j * PAIR:(j + 1) * PAIR] for j in range(GP)]
            _, amps, carry = _sb_blocks(q_tall, k2, strict, tri2, carry)
            parts = [_split2(a) for a in amps]
            new_acc, new_fine = [], []
            for j in range(GP):
                tall = jnp.concatenate([parts[2 * j][0], parts[2 * j + 1][0], parts[2 * j][1], parts[2 * j + 1][1]],
                                       axis=0)
                hi0, hi1, mid0, mid1 = _per_head(_nn(tall, v2[j]), T)
                new_acc.append(acc[j] + jnp.where(low, hi0, hi1))
                new_fine.append(fine[j] + jnp.where(low, mid0, mid1))
            return tuple(carry), tuple(new_acc), tuple(new_fine)

        zero = (jnp.zeros((T, PAIR), F32),) * GP
        carry, acc, fine = pairs(qi, (jnp.zeros((T, T), F32),) * (2 * GP), zero, zero, col < row)

        def cond(st):
            kb, carry, _, _ = st
            return jnp.logical_and(kb >= 0, _sb_alive(carry))

        def step(st):
            kb, carry, acc, fine = st
            carry, acc, fine = pairs(kb, carry, acc, fine, None)
            return kb - 1, carry, acc, fine

        _, _, acc, fine = lax.while_loop(cond, step, (qi - 1, carry, acc, fine))
        for j in range(GP):
            o_ref[:, j * PAIR:(j + 1) * PAIR] = acc[j].astype(o_ref.dtype)
            ox_ref[:, j * PAIR:(j + 1) * PAIR] = acc[j] + fine[j]

    blk = pl.BlockSpec((T, GW), lambda p, i: (i, p))
    return _call_hosted(
        body, name, (nb, S // T),
        [blk, pl.BlockSpec((S, GW), lambda p, i: (0, nb + p)), pl.BlockSpec((S, GW), lambda p, i: (0, 2 * nb + p))],
        [blk, blk], [jax.ShapeDtypeStruct((S, half), _MXU), jax.ShapeDtypeStruct((S, half), F32)],
        [], [proj, proj, proj], ex)


def _sb_bwd(proj, ox, dmixed, name, ex=None):
    S, W = proj.shape
    half = W // 6
    npair = half // PAIR
    T = _blk(S, SB_T)
    GP = _blk(npair, SB_PAIRS)
    GW = GP * PAIR
    nb = npair // GP
    last = S // T - 1
    scale = HEAD_DIM ** -0.5

    def body(q_ref, k_ref, v_ref, ox_ref, do_ref, dq_ref, dk_ref, dv_ref, dka_ref, dva_ref):
        qi = pl.program_id(1)

        @pl.when(qi == 0)
        def _():
            dka_ref[...] = jnp.zeros_like(dka_ref)
            dva_ref[...] = jnp.zeros_like(dva_ref)

        low = _low_lanes(T)
        q2, do2, q_tall, do_tall, deltas = [], [], [], [], []
        for j in range(GP):
            cols = slice(j * PAIR, (j + 1) * PAIR)
            q2.append(q_ref[:, cols])
            do2.append(do_ref[:, cols])
            q_tall.append(_two_heads(q2[j], low, scale))
            dobs = [_one_head(do2[j], low, True), _one_head(do2[j], low, False)]
            do_tall.append(jnp.concatenate(dobs, axis=0))
            for e in range(2):
                deltas.append(jnp.sum(dobs[e].astype(F32) * ox_ref[:, cols], axis=-1, keepdims=True))
        row = lax.broadcasted_iota(jnp.int32, (T, T), 0)
        col = lax.broadcasted_iota(jnp.int32, (T, T), 1)
        tri_ex = _tri2(T, inclusive=False)
        tri_in = _tri2(T, inclusive=True)

        def pairs(kb, carry, right, dq, strict):
            rows = pl.ds(pl.multiple_of(kb * T, T), T)
            k2 = [k_ref[rows, j * PAIR:(j + 1) * PAIR] for j in range(GP)]
            v2 = [v_ref[rows, j * PAIR:(j + 1) * PAIR] for j in range(GP)]
            nh = 2 * GP
            gs = []
            for j in range(GP):
                gs += _per_head(_nt(do_tall[j], v2[j]), T)
            lbs, amps, carry = _sb_blocks(q_tall, k2, strict, tri_ex, carry)
            ags = [a * gg for a, gg in zip(amps, gs)]
            sums = _tri_sums(ags, tri_in)
            dzbs = []
            for h in range(nh):
                left = deltas[h] - (sums[h][0] + right[h])
                beta = jnp.exp(lbs[h])
                dz = ags[h] - beta * (ags[h] + left)
                if strict is not None:
                    dz = jnp.where(strict, dz, 0.0)
                dzbs.append(dz.astype(_MXU))
            abs_ = [a.astype(_MXU) for a in amps]
            new_dq = []
            for j in range(GP):
                cols = slice(j * PAIR, (j + 1) * PAIR)
                dk0, dk1 = _per_head(_tn(jnp.concatenate(dzbs[2 * j:2 * j + 2], axis=1), q2[j]), T)
                dv0, dv1 = _per_head(_tn(jnp.concatenate(abs_[2 * j:2 * j + 2], axis=1), do2[j]), T)
                dq0, dq1 = _per_head(_nn(jnp.concatenate(dzbs[2 * j:2 * j + 2], axis=0), k2[j]), T)
                dka_ref[rows, cols] += jnp.where(low, dk0, dk1)
                dva_ref[rows, cols] += jnp.where(low, dv0, dv1)
                new_dq.append(dq[j] + jnp.where(low, dq0, dq1))
            right = tuple(right[h] + sums[h][1] for h in range(nh))
            return tuple(carry), right, tuple(new_dq)

        zero = (jnp.zeros((T, T), F32),) * (2 * GP)
        carry, right, dq = pairs(qi, zero, zero, (jnp.zeros((T, PAIR), F32),) * GP, col < row)

        def cond(st):
            kb, carry, _, _ = st
            return jnp.logical_and(kb >= 0, _sb_alive(carry))

        def step(st):
            kb, carry, right, dq = st
            carry, right, dq = pairs(kb, carry, right, dq, None)
            return kb - 1, carry, right, dq

        _, _, _, dq = lax.while_loop(cond, step, (qi - 1, carry, right, dq))
        for j in range(GP):
            dq_ref[:, j * PAIR:(j + 1) * PAIR] = (dq[j] * scale).astype(dq_ref.dtype)

        @pl.when(qi == last)
        def _():
            dk_ref[...] = (dka_ref[...] * scale).astype(dk_ref.dtype)
            dv_ref[...] = dva_ref[...].astype(dv_ref.dtype)

    blk = pl.BlockSpec((T, GW), lambda p, i: (i, p))
    full = pl.BlockSpec((S, GW), lambda p, i: (0, p))
    return _call_hosted(
        body, name, (nb, S // T),
        [blk, pl.BlockSpec((S, GW), lambda p, i: (0, nb + p)), pl.BlockSpec((S, GW), lambda p, i: (0, 2 * nb + p)),
         blk, blk],
        [blk, full, full], [jax.ShapeDtypeStruct((S, half), _MXU)] * 3,
        [pltpu.VMEM((S, GW), F32), pltpu.VMEM((S, GW), F32)], [proj, proj, proj, ox, dmixed], ex)


def _pair_norm(t2, g2, low):
    tf = t2.astype(F32)
    sq = tf * tf
    both = jnp.sum(sq, axis=-1, keepdims=True)
    first = jnp.sum(jnp.where(low, sq, 0.0), axis=-1, keepdims=True)
    r = jnp.where(low, lax.rsqrt(first * (1.0 / HEAD_DIM) + EPS), lax.rsqrt((both - first) * (1.0 / HEAD_DIM) + EPS))
    hat = tf * r
    return hat * g2, hat, r


def _pair_norm_bwd(dn, hat, r, g2, low):
    dhat = dn * g2
    prod = dhat * hat
    both = jnp.sum(prod, axis=-1, keepdims=True)
    first = jnp.sum(jnp.where(low, prod, 0.0), axis=-1, keepdims=True)
    mean = jnp.where(low, first, both - first) * (1.0 / HEAD_DIM)
    return r * (dhat - hat * mean)


def _ca_fill(j, k_ref, v_ref, gk_ref, kn_ref, vp_ref):
    S = k_ref.shape[0]
    cols = slice(j * PAIR, (j + 1) * PAIR)
    kn, _, _ = _pair_norm(k_ref[:, cols], gk_ref[...], _low_lanes(S))
    kn_ref[j, 0:PAD, :] = jnp.zeros((PAD, PAIR), kn_ref.dtype)
    vp_ref[j, 0:PAD, :] = jnp.zeros((PAD, PAIR), vp_ref.dtype)
    kn_ref[j, PAD:PAD + S, :] = kn.astype(kn_ref.dtype)
    vp_ref[j, PAD:PAD + S, :] = v_ref[:, cols]


def _ca_scores(j, q_ref, b2_ref, gq_ref, kn_ref, qi, low):
    qn, qhat, r = _pair_norm(q_ref[:, j * PAIR:(j + 1) * PAIR], gq_ref[...], low)
    qn = qn * HEAD_DIM ** -0.5
    band = pl.ds(pl.multiple_of(qi * CA_T, CA_T), CA_W)
    key_pos = qi * CA_T - PAD + lax.broadcasted_iota(jnp.int32, (CA_T, CA_W), 1)
    both = _per_head(_nt(_two_heads(qn, low), kn_ref[j, band, :]), CA_T)
    scores = [jnp.where(key_pos >= 0, both[e] + b2_ref[2 * j + e], NEG) for e in range(2)]
    return scores, qn.astype(_MXU), qhat, r


def _softmax(s):
    e = jnp.exp(s - jnp.max(s, axis=-1, keepdims=True))
    return e * (1.0 / jnp.sum(e, axis=-1, keepdims=True))


def _ca_fwd(proj, bias2, gq2, gk2, name, ex=None):
    S, W = proj.shape
    half = W // 6
    npair = half // PAIR
    GP = _blk(npair, CA_PAIRS_FWD)
    GW = GP * PAIR
    nb = npair // GP

    def body(q_ref, k_ref, v_ref, b2_ref, gq_ref, gk_ref, o_ref, kn_ref, vp_ref):
        qi = pl.program_id(1)

        @pl.when(qi == 0)
        def _():
            for j in range(GP):
                _ca_fill(j, k_ref, v_ref, gk_ref, kn_ref, vp_ref)

        low = _low_lanes(CA_T)
        band = pl.ds(pl.multiple_of(qi * CA_T, CA_T), CA_W)
        scores = [_ca_scores(j, q_ref, b2_ref, gq_ref, kn_ref, qi, low)[0] for j in range(GP)]
        probs = [[_softmax(s).astype(_MXU) for s in pair] for pair in scores]
        for j in range(GP):
            outs = _per_head(_nn(jnp.concatenate(probs[j], axis=0), vp_ref[j, band, :]), CA_T)
            o_ref[:, j * PAIR:(j + 1) * PAIR] = jnp.where(low, outs[0], outs[1]).astype(o_ref.dtype)

    vec = pl.BlockSpec((1, PAIR), lambda p, i: (0, 0))
    return _call_hosted(
        body, name, (nb, S // CA_T),
        [pl.BlockSpec((CA_T, GW), lambda p, i: (i, 3 * nb + p)),
         pl.BlockSpec((S, GW), lambda p, i: (0, 4 * nb + p)), pl.BlockSpec((S, GW), lambda p, i: (0, 5 * nb + p)),
         pl.BlockSpec((2 * GP, CA_T, CA_W), lambda p, i: (p, 0, 0)), vec, vec],
        [pl.BlockSpec((CA_T, GW), lambda p, i: (i, p))], [jax.ShapeDtypeStruct((S, half), _MXU)],
        [pltpu.VMEM((GP, PAD + S, PAIR), _MXU), pltpu.VMEM((GP, PAD + S, PAIR), _MXU)],
        [proj, proj, proj, bias2, gq2, gk2], ex)


def _ca_bwd(proj, bias2, gq2, gk2, dmixed, name, ex=None):
    S, W = proj.shape
    half = W // 6
    npair = half // PAIR
    GP = _blk(npair, CA_PAIRS_BWD)
    GW = GP * PAIR
    nb = npair // GP
    scale = HEAD_DIM ** -0.5
    last = S // CA_T - 1

    def body(q_ref, k_ref, v_ref, b2_ref, gq_ref, gk_ref, do_ref,
             dq_ref, dk_ref, dv_ref, db_ref, dgq_ref, dgk_ref, kn_ref, vp_ref, dkn_ref, dvp_ref):
        p_id, qi = pl.program_id(0), pl.program_id(1)

        @pl.when(qi == 0)
        def _():
            for j in range(GP):
                _ca_fill(j, k_ref, v_ref, gk_ref, kn_ref, vp_ref)
            dkn_ref[...] = jnp.zeros_like(dkn_ref)
            dvp_ref[...] = jnp.zeros_like(dvp_ref)
            db_ref[...] = jnp.zeros_like(db_ref)

        @pl.when(jnp.logical_and(p_id == 0, qi == 0))
        def _():
            dgq_ref[...] = jnp.zeros_like(dgq_ref)
            dgk_ref[...] = jnp.zeros_like(dgk_ref)

        low = _low_lanes(CA_T)
        top_w = lax.broadcasted_iota(jnp.int32, (PAIR, CA_W), 0) < HEAD_DIM
        band = pl.ds(pl.multiple_of(qi * CA_T, CA_T), CA_W)
        pairs = [_ca_scores(j, q_ref, b2_ref, gq_ref, kn_ref, qi, low) for j in range(GP)]
        do2 = [do_ref[:, j * PAIR:(j + 1) * PAIR] for j in range(GP)]
        dps = [_per_head(_nt(_two_heads(do2[j], low), vp_ref[j, band, :]), CA_T) for j in range(GP)]
        probs, dsbs = [], []
        for j in range(GP):
            pj, dj = [], []
            for e in range(2):
                p = _softmax(pairs[j][0][e])
                ds = p * (dps[j][e] - jnp.sum(p * dps[j][e], axis=-1, keepdims=True))
                db_ref[2 * j + e] += ds
                pj.append(p.astype(_MXU))
                dj.append(ds.astype(_MXU))
            probs.append(pj)
            dsbs.append(dj)
        dgq = jnp.zeros((1, PAIR), F32)
        for j in range(GP):
            _, qn, qhat, r = pairs[j]
            dq_h = _per_head(_nn(jnp.concatenate(dsbs[j], axis=0), kn_ref[j, band, :]), CA_T)
            dk_t = _tn(qn, jnp.concatenate(dsbs[j], axis=1))
            dv_t = _tn(do2[j], jnp.concatenate(probs[j], axis=1))
            dkn_ref[j, :, band] += jnp.where(top_w, dk_t[:, :CA_W], dk_t[:, CA_W:])
            dvp_ref[j, :, band] += jnp.where(top_w, dv_t[:, :CA_W], dv_t[:, CA_W:])
            dqn = jnp.where(low, dq_h[0], dq_h[1]) * scale
            dgq = dgq + jnp.sum(dqn * qhat, axis=0, keepdims=True)
            dq_ref[:, j * PAIR:(j + 1) * PAIR] = _pair_norm_bwd(dqn, qhat, r, gq_ref[...], low).astype(dq_ref.dtype)
        dgq_ref[...] += dgq

        @pl.when(qi == last)
        def _():
            low_s = _low_lanes(S)
            for j in range(GP):
                cols = slice(j * PAIR, (j + 1) * PAIR)
                _, khat, rk = _pair_norm(k_ref[:, cols], gk_ref[...], low_s)
                dkn = dkn_ref[j, :, PAD:PAD + S].T
                dgk_ref[...] += jnp.sum(dkn * khat, axis=0, keepdims=True)
                dk_ref[:, cols] = _pair_norm_bwd(dkn, khat, rk, gk_ref[...], low_s).astype(dk_ref.dtype)
                dv_ref[:, cols] = dvp_ref[j, :, PAD:PAD + S].T.astype(dv_ref.dtype)

    vec = pl.BlockSpec((1, PAIR), lambda p, i: (0, 0))
    tile = pl.BlockSpec((2 * GP, CA_T, CA_W), lambda p, i: (p, 0, 0))
    full = pl.BlockSpec((S, GW), lambda p, i: (0, p))
    return _call_hosted(
        body, name, (nb, S // CA_T),
        [pl.BlockSpec((CA_T, GW), lambda p, i: (i, 3 * nb + p)),
         pl.BlockSpec((S, GW), lambda p, i: (0, 4 * nb + p)), pl.BlockSpec((S, GW), lambda p, i: (0, 5 * nb + p)),
         tile, vec, vec, pl.BlockSpec((CA_T, GW), lambda p, i: (i, nb + p))],
        [pl.BlockSpec((CA_T, GW), lambda p, i: (i, p)), full, full, tile, vec, vec],
        [jax.ShapeDtypeStruct((S, half), _MXU)] * 3
        + [jax.ShapeDtypeStruct(bias2.shape, F32), jax.ShapeDtypeStruct((1, PAIR), F32),
           jax.ShapeDtypeStruct((1, PAIR), F32)],
        [pltpu.VMEM((GP, PAD + S, PAIR), _MXU), pltpu.VMEM((GP, PAD + S, PAIR), _MXU),
         pltpu.VMEM((GP, PAIR, PAD + S), F32), pltpu.VMEM((GP, PAIR, PAD + S), F32)],
        [proj, proj, proj, bias2, gq2, gk2, dmixed], ex)


def _pack_small(parts):
    flat = jnp.concatenate([p.reshape(-1) for layer in parts for p in layer])
    n = flat.shape[0]
    n_pad = -(-n // 1024) * 1024
    return jnp.pad(flat, (0, n_pad - n)).reshape(1, n_pad)


def _unpack_small(flat, shapes):
    out, off = [], 0
    for layer in shapes:
        cur = []
        for shp in layer:
            size = 1
            for s in shp:
                size *= s
            cur.append(flat[off:off + size].reshape(shp))
            off += size
        out.append(cur)
    return out


def kernel(x, c, g_norm1, w_in, g_q, g_k, rel_bias, w_o, g_norm2, w1, w2, w_ada, b_ada, loss_target, m_g_norm1, m_w_in, m_g_q, m_g_k, m_rel_bias, m_w_o, m_g_norm2, m_w1, m_w2, m_w_ada, m_b_ada, v_g_norm1, v_w_in, v_g_q, v_g_k, v_rel_bias, v_w_o, v_g_norm2, v_w1, v_w2, v_w_ada, v_b_ada):
    L = w_in.shape[0]
    S, D = x.shape[1:]
    H2 = D // HEAD_DIM // 2
    Ca = w_ada.shape[2]
    xi, yi, ci = _pos()
    me = 4 * xi + 2 * yi + ci
    place = jnp.stack([2 * xi + yi, ci]).astype(jnp.int32)

    c_all = _all_gather_small(c, "ag_c").reshape(NDEV, D)
    b_cols = lax.dynamic_slice(b_ada, (0, me * Ca), (L, Ca))
    mod_part = _mod_partial(c_all, w_ada, b_cols, "mod_partial")
    mod_all = _all_gather_small(mod_part, "ag_mod")
    mod = lax.dynamic_index_in_dim(mod_all, me, axis=1, keepdims=False)
    mod = mod.reshape(NDEV, L, Ca).transpose(1, 0, 2).reshape(L, 6, 1, D)

    wire = lambda a: a.astype(_MXU)
    by_cols = lambda g: g.transpose(1, 0, 2).reshape(D, g.shape[0] * g.shape[2])
    W_in = {0: by_cols(_run_exchange(_gather_exchange([wire(w_in[0])]), "ag_w_in0")[0])}
    W_o, W_1, W_2 = {}, {}, {}

    xs = [x[0]]
    saved = []
    for l in range(L):
        sh1, sc1, gt1, sh2, sc2, gt2 = [mod[l, i] for i in range(6)]
        gn1, gn2 = g_norm1[l:l + 1], g_norm2[l:l + 1]
        gq2, gk2 = jnp.tile(g_q[l:l + 1], (1, 2)), jnp.tile(g_k[l:l + 1], (1, 2))
        proj, h1 = _ln_mod_matmul(xs[-1], gn1, sc1, sh1, W_in[l], f"l{l}_proj")
        (o_sb, ox_sb), got = _sb_fwd(proj, f"l{l}_sb_fwd",
                                     _gather_exchange([wire(w_o[l]), wire(w1[l]), wire(w2[l])]))
        W_o[l], W_1[l], W_2[l] = got[0].reshape(D, D), by_cols(got[1]), got[2].reshape(4 * D, D)
        bias2 = _ca_bias(rel_bias[l], f"l{l}_ca_bias")
        (o_ca,), got = _ca_fwd(proj, bias2, gq2, gk2, f"l{l}_ca_fwd",
                               _gather_exchange([wire(w_in[l + 1])]) if l + 1 < L else None)
        if got:
            W_in[l + 1] = by_cols(got[0])
        mixed = jnp.concatenate([o_sb, o_ca], axis=1)
        x1, f1 = _matmul_res_gate(mixed, W_o[l], xs[-1], gt1, False, f"l{l}_attn_out")
        u, h2 = _ln_mod_matmul(x1, gn2, sc2, sh2, W_1[l], f"l{l}_mlp_in")
        x2, f2 = _matmul_res_gate(u, W_2[l], x1, gt2, True, f"l{l}_mlp_out")
        saved.append(dict(x0=xs[-1], h1=h1, proj=proj, ox_sb=ox_sb, bias2=bias2, mixed=mixed, f1=f1, x1=x1,
                          h2=h2, u=u, f2=f2))
        xs.append(x2)

    dx, loss_part = _loss_grad(xs[-1], loss_target[0], "loss")

    owns, recv_b = {}, {}
    ready = []
    small_parts = [None] * L

    def partials(keys, grads, recv_a):
        parts = []
        for key, g, r in zip(keys, grads, recv_a):
            owns[key], part = _rs_chip_partial(place, g, r, f"rs_partial_l{key[0]}_{key[1]}")
            parts.append(part)
        return parts

    for l in reversed(range(L)):
        sv = saved[l]
        sh1, sc1, gt1, sh2, sc2, gt2 = [mod[l, i] for i in range(6)]
        gn1, gn2 = g_norm1[l:l + 1], g_norm2[l:l + 1]
        gq2, gk2 = jnp.tile(g_q[l:l + 1], (1, 2)), jnp.tile(g_k[l:l + 1], (1, 2))
        dz2, dgt2, du = _gate_nt_matmul(dx, sv["f2"], gt2, W_2[l], sv["u"], f"l{l}_mlp_out_bwd")
        gw2 = _tn_matmul(sv["u"], dz2, False, True, f"l{l}_gw2")
        gw1 = _tn_matmul(sv["h2"], du, True, False, f"l{l}_gw1")
        dx, dsh2, dsc2, dgn2 = _nt_ln_bwd(du, W_1[l], sv["x1"], gn2, sc2, sh2, dx, f"l{l}_mlp_in_bwd")
        dz1, dgt1, dmixed = _gate_nt_matmul(dx, sv["f1"], gt1, W_o[l], None, f"l{l}_attn_out_bwd")
        gwo = _tn_matmul(sv["mixed"], dz1, False, False, f"l{l}_gwo")
        ready += [((l, 1), gwo), ((l, 2), gw1), ((l, 3), gw2)]
        keys, grads = [k for k, _ in ready], [g for _, g in ready]
        (dq_sb, dk_sb, dv_sb), recv_a = _sb_bwd(sv["proj"], sv["ox_sb"], dmixed, f"l{l}_sb_bwd",
                                                _sibling_exchange(grads))
        parts = partials(keys, grads, recv_a)
        (dq_ca, dk_ca, dv_ca, dbias2, dgq2, dgk2), got = _ca_bwd(sv["proj"], sv["bias2"], gq2, gk2, dmixed,
                                                                 f"l{l}_ca_bwd", _chip_exchange(parts))
        recv_b.update(zip(keys, got))
        dgq = dgq2[:, :HEAD_DIM] + dgq2[:, HEAD_DIM:]
        dgk = dgk2[:, :HEAD_DIM] + dgk2[:, HEAD_DIM:]
        drb = _ca_bias_bwd(dbias2, f"l{l}_ca_bias_bwd")
        dproj = jnp.concatenate([dq_sb, dk_sb, dv_sb, dq_ca, dk_ca, dv_ca], axis=1)
        gwin = _tn_matmul(sv["h1"], dproj, True, False, f"l{l}_gwin")
        ready = [((l, 0), gwin)]
        dx, dsh1, dsc1, dgn1 = _nt_ln_bwd(dproj, W_in[l], sv["x0"], gn1, sc1, sh1, dx, f"l{l}_proj_bwd")
        dmod = jnp.concatenate([dsh1, dsc1, dgt1, dsh2, dsc2, dgt2], axis=1)
        small_parts[l] = [dgn1, dgq, dgk, drb, dgn2, dmod]
    grad_x = dx[None]

    keys, grads = [k for k, _ in ready], [g for _, g in ready]
    parts = partials(keys, grads, _run_exchange(_sibling_exchange(grads), "rs_sibling_last"))
    recv_b.update(zip(keys, _run_exchange(_chip_exchange(parts), "rs_chips_last")))
    big_out = []
    for t, (w, m, v) in enumerate([(w_in, m_w_in, v_w_in), (w_o, m_w_o, v_w_o), (w1, m_w1, v_w1), (w2, m_w2, v_w2)]):
        big_out.append(_rs_sum_adamw([owns[(l, t)] for l in range(L)], [recv_b[(l, t)] for l in range(L)],
                                     w, m, v, f"adamw_big_{t}"))

    packed = _pack_small(small_parts)
    gathered_small = _all_gather_small(packed, "ag_small_grads")
    small_sum = _sum_devices(gathered_small, "sum_small_grads")
    shapes = [[(1, D), (1, HEAD_DIM), (1, HEAD_DIM), (H2, N_REL), (1, D), (1, 6 * D)]] * L
    names = ["g_norm1", "g_q", "g_k", "rel_bias", "g_norm2", "b_ada"]
    small_w = {"g_norm1": (g_norm1, m_g_norm1, v_g_norm1), "g_q": (g_q, m_g_q, v_g_q), "g_k": (g_k, m_g_k, v_g_k),
               "rel_bias": (rel_bias, m_rel_bias, v_rel_bias), "g_norm2": (g_norm2, m_g_norm2, v_g_norm2),
               "b_ada": (b_ada, m_b_ada, v_b_ada)}
    packs = [_pack_small([[small_w[n][k][l] for n in names] for l in range(L)]) for k in range(3)]
    n_pad = packed.shape[1]
    as_rows = lambda a: a.reshape(n_pad // 128, 128)
    sd, sm, sv_ = _adamw(as_rows(packs[0]), as_rows(small_sum), as_rows(packs[1]), as_rows(packs[2]), "adamw_small")
    small_out = {}
    for key, flat in [("grad", small_sum), ("delta", sd), ("m", sm), ("v", sv_)]:
        per_layer = _unpack_small(flat.reshape(-1), shapes)
        for i, n in enumerate(names):
            small_out[(key, n)] = jnp.stack([per_layer[l][i].reshape(small_w[n][0].shape[1:]) for l in range(L)])

    layer_len = 2 * D + 2 * HEAD_DIM + H2 * N_REL + 6 * D
    rows = gathered_small.reshape(NDEV, n_pad)
    dmod_all = jnp.stack([rows[:, l * layer_len + layer_len - 6 * D:(l + 1) * layer_len] for l in range(L)])
    dmod_cols = lax.dynamic_slice(dmod_all, (0, 0, me * Ca), (L, NDEV, Ca))
    dmod_cols = jnp.pad(dmod_cols, ((0, 0), (0, 128 - NDEV), (0, 0)))
    c_t = jnp.pad(c_all.T, ((0, 0), (0, 128 - NDEV)))
    g_ada = _w_ada_grad(c_t, dmod_cols, "w_ada_grad")
    flat2 = lambda a: a.reshape(L * D, Ca)
    ad, am, av = _adamw(flat2(w_ada), flat2(g_ada), flat2(m_w_ada), flat2(v_w_ada), "adamw_w_ada")
    ada_out = [g_ada] + [a.reshape(L, D, Ca) for a in (ad, am, av)]

    def leaf(kind):
        k = {"grad": 0, "delta": 1, "m": 2, "v": 3}[kind]
        return [small_out[(kind, "g_norm1")], big_out[0][k], small_out[(kind, "g_q")], small_out[(kind, "g_k")],
                small_out[(kind, "rel_bias")], big_out[1][k], small_out[(kind, "g_norm2")], big_out[2][k],
                big_out[3][k], ada_out[k], small_out[(kind, "b_ada")]]

    loss = lax.psum(loss_part[0, 0], ("x", "y", "c"))
    return (loss, grad_x, *leaf("grad"), *leaf("delta"), *leaf("m"), *leaf("v"))
```

```python
import functools

import jax
import jax.numpy as jnp
from jax import lax
from jax.experimental import pallas as pl
from jax.experimental.pallas import tpu as pltpu

F32 = jnp.float32
_MXU = jnp.bfloat16

HEAD_DIM = 64
CHUNK = 64
LEFT_CHUNKS = 8
PAD = LEFT_CHUNKS * CHUNK
BAND = PAD + CHUNK
REL_CLIP = 128
N_REL = 2 * REL_CLIP + 1
EPS = 1e-6
NEG = -1e30
NDEV = 8
SB_T = 128
CA_T = 2 * CHUNK
CA_W = CA_T + PAD
SB_SKIP = -104.0
PAIR = 2 * HEAD_DIM
SB_PAIRS = 4
CA_PAIRS_FWD = 4
CA_PAIRS_BWD = 2
ROW_BLOCK = 512
SKEW_W = 767

ADAM_LR, ADAM_B1, ADAM_B2, ADAM_EPS, ADAM_WD, ADAM_STEP = 0.001, 0.9, 0.999, 1e-08, 0.01, 10

MESH = pl.DeviceIdType.MESH
VMEM_SPEC = pl.BlockSpec(memory_space=pltpu.VMEM)
SMEM_SPEC = pl.BlockSpec(memory_space=pltpu.SMEM)
ANY_SPEC = pl.BlockSpec(memory_space=pl.ANY)


def _nn(a, b):
    return lax.dot_general(a, b, (((1,), (0,)), ((), ())), preferred_element_type=F32)


def _nt(a, b):
    return lax.dot_general(a, b, (((1,), (1,)), ((), ())), preferred_element_type=F32)


def _tn(a, b):
    return lax.dot_general(a, b, (((0,), (0,)), ((), ())), preferred_element_type=F32)


def _blk(n, pref):
    return pref if n % pref == 0 else n


def _pos():
    return lax.axis_index("x"), lax.axis_index("y"), lax.axis_index("c")


def _flip(v, bit):
    return 1 - v if bit else v


def _all_gather_small(blk, name):
    R, C = blk.shape

    def body(x_ref, out_ref, send_sems, recv_sems):
        x, y, c = _pos()
        me = 4 * x + 2 * y + c

        def peer(k):
            return (_flip(x, k & 4), _flip(y, k & 2), _flip(c, k & 1))

        def copy(k, slot):
            return pltpu.make_async_remote_copy(
                src_ref=x_ref, dst_ref=out_ref.at[slot], send_sem=send_sems.at[k - 1],
                recv_sem=recv_sems.at[k - 1], device_id=peer(k), device_id_type=MESH)

        out_ref[pl.ds(me, 1), :, :] = x_ref[...].reshape(1, R, C)
        sends = [copy(k, me) for k in range(1, NDEV)]
        for cp in sends:
            cp.start()
        for k in range(1, NDEV):
            px, py, pc = peer(k)
            copy(k, 4 * px + 2 * py + pc).wait_recv()
        for cp in sends:
            cp.wait_send()

    return pl.pallas_call(
        body, name=name,
        out_shape=jax.ShapeDtypeStruct((NDEV, R, C), blk.dtype),
        in_specs=[VMEM_SPEC], out_specs=VMEM_SPEC,
        scratch_shapes=[pltpu.SemaphoreType.DMA((NDEV - 1,)), pltpu.SemaphoreType.DMA((NDEV - 1,))],
    )(blk)


class _Exchange:
    def __init__(self, inputs, out_shapes, sems, start, finish, middle=None):
        self.inputs, self.out_shapes, self.sems = list(inputs), list(out_shapes), list(sems)
        self.start, self.middle, self.finish = start, middle, finish


def _run_exchange(ex, name):
    n_in, n_out = len(ex.inputs), len(ex.out_shapes)

    def body(*refs):
        ins, outs, sems = refs[:n_in], refs[n_in:n_in + n_out], refs[n_in + n_out:]
        ex.start(ins, outs, sems)
        if ex.middle is not None:
            ex.middle(ins, outs, sems)
        ex.finish(ins, outs, sems)

    return pl.pallas_call(
        body, name=name, out_shape=ex.out_shapes, in_specs=[ANY_SPEC] * n_in, out_specs=[ANY_SPEC] * n_out,
        scratch_shapes=ex.sems,
    )(*ex.inputs)


def _hosted(body, n_in, n_out, ex, step, steps):
    if ex is None:
        return body
    xi, xo = len(ex.inputs), len(ex.out_shapes)

    def wrapped(*refs):
        own_in, ex_in = refs[:n_in], refs[n_in:n_in + xi]
        rest = refs[n_in + xi:]
        own_out, ex_out = rest[:n_out], rest[n_out:n_out + xo]
        rest = rest[n_out + xo:]
        own_scratch, ex_sems = rest[:len(rest) - len(ex.sems)], rest[len(rest) - len(ex.sems):]
        t = step()
        pl.when(t == 0)(lambda: ex.start(ex_in, ex_out, ex_sems))
        body(*own_in, *own_out, *own_scratch)
        if ex.middle is not None:
            pl.when(t == (steps * 3) // 5)(lambda: ex.middle(ex_in, ex_out, ex_sems))
        pl.when(t == steps - 1)(lambda: ex.finish(ex_in, ex_out, ex_sems))

    return wrapped


def _call_hosted(body, name, grid, in_specs, out_specs, out_shape, scratch, args, ex):
    n_in, n_out = len(in_specs), len(out_specs)
    steps = 1
    for extent in grid:
        steps *= extent

    def step():
        t = pl.program_id(0)
        for axis in range(1, len(grid)):
            t = t * grid[axis] + pl.program_id(axis)
        return t

    if ex is not None:
        in_specs = in_specs + [ANY_SPEC] * len(ex.inputs)
        out_specs = out_specs + [ANY_SPEC] * len(ex.out_shapes)
        out_shape = out_shape + ex.out_shapes
        scratch = scratch + ex.sems
        args = args + ex.inputs
    outs = pl.pallas_call(
        _hosted(body, n_in, n_out, ex, step, steps), name=name, grid=grid, in_specs=in_specs, out_specs=out_specs,
        out_shape=out_shape, scratch_shapes=scratch,
    )(*args)
    return list(outs[:n_out]), list(outs[n_out:])


def _gather_exchange(shards):
    n = len(shards)

    def setup(ins, outs, sems):
        send_sems, recv_sems, local_sems = sems
        x, y, c = _pos()
        me, sibling = (x, y, c), (x, y, 1 - c)
        chips = [(1 - x, y), (x, 1 - y), (1 - x, 1 - y)]

        def copy(i, k, block, to, src=None):
            px, py, pc = block
            dst = outs[i].at[4 * px + 2 * py + pc]
            return pltpu.make_async_remote_copy(
                src_ref=dst if src is None else src, dst_ref=dst, send_sem=send_sems.at[7 * i + k],
                recv_sem=recv_sems.at[7 * i + k], device_id=to, device_id_type=MESH)

        def mine(i):
            return pltpu.make_async_copy(ins[i], outs[i].at[4 * x + 2 * y + c], local_sems.at[i])

        def first(i):
            return [copy(i, 0, me, sibling, src=ins[i])] + [
                copy(i, 1 + j, me, (*chip, c), src=ins[i]) for j, chip in enumerate(chips)]

        def passed(i, j):
            return copy(i, 4 + j, (*chips[j], c), sibling)

        return me, sibling, chips, c, copy, mine, first, passed

    def start(ins, outs, sems):
        _, _, _, _, _, mine, first, _ = setup(ins, outs, sems)
        for i in range(n):
            mine(i).start()
            for cp in first(i):
                cp.start()

    def middle(ins, outs, sems):
        me, _, chips, c, copy, _, _, passed = setup(ins, outs, sems)
        for j, chip in enumerate(chips):
            for i in range(n):
                copy(i, 1 + j, (*chip, c), me).wait_recv()
                passed(i, j).start()

    def finish(ins, outs, sems):
        me, sibling, chips, c, copy, mine, first, passed = setup(ins, outs, sems)
        for i in range(n):
            copy(i, 0, sibling, me).wait_recv()
            for j, chip in enumerate(chips):
                copy(i, 4 + j, (*chip, 1 - c), me).wait_recv()
        for i in range(n):
            for cp in first(i) + [passed(i, j) for j in range(3)]:
                cp.wait_send()
            mine(i).wait()

    return _Exchange(
        shards, [jax.ShapeDtypeStruct((NDEV,) + s.shape, s.dtype) for s in shards],
        [pltpu.SemaphoreType.DMA((7 * n,)), pltpu.SemaphoreType.DMA((7 * n,)), pltpu.SemaphoreType.DMA((n,))],
        start, finish, middle)


def _sibling_exchange(grads):
    n = len(grads)

    def copies(ins, outs, sems):
        send_sems, recv_sems = sems
        x, y, c = _pos()
        return [pltpu.make_async_remote_copy(
            src_ref=ins[i].at[2 * q + (1 - c)], dst_ref=outs[i].at[q], send_sem=send_sems.at[4 * i + q],
            recv_sem=recv_sems.at[4 * i + q], device_id=(x, y, 1 - c), device_id_type=MESH)
            for i in range(n) for q in range(4)]

    def start(ins, outs, sems):
        for cp in copies(ins, outs, sems):
            cp.start()

    def finish(ins, outs, sems):
        for cp in copies(ins, outs, sems):
            cp.wait()

    return _Exchange(
        grads, [jax.ShapeDtypeStruct((4,) + g.shape[1:], g.dtype) for g in grads],
        [pltpu.SemaphoreType.DMA((4 * n,)), pltpu.SemaphoreType.DMA((4 * n,))], start, finish)


def _chip_exchange(parts):
    n = len(parts)

    def copies(ins, outs, sems):
        send_sems, recv_sems = sems
        x, y, c = _pos()
        return [pltpu.make_async_remote_copy(
            src_ref=ins[i].at[j - 1], dst_ref=outs[i].at[j - 1], send_sem=send_sems.at[3 * i + j - 1],
            recv_sem=recv_sems.at[3 * i + j - 1], device_id=(_flip(x, j & 2), _flip(y, j & 1), c),
            device_id_type=MESH) for i in range(n) for j in range(1, 4)]

    def start(ins, outs, sems):
        for cp in copies(ins, outs, sems):
            cp.start()

    def finish(ins, outs, sems):
        for cp in copies(ins, outs, sems):
            cp.wait()

    return _Exchange(
        parts, [jax.ShapeDtypeStruct(p.shape, p.dtype) for p in parts],
        [pltpu.SemaphoreType.DMA((3 * n,)), pltpu.SemaphoreType.DMA((3 * n,))], start, finish)


def _rs_chip_partial(place, grad, recv, name):
    _, R, C = grad.shape
    tr = _blk(R, 256)

    def body(place_ref, *refs):
        g_refs, r_refs = refs[:4], refs[4:8]
        own_ref, out_ref = refs[8:]
        own_ref[...] = g_refs[0][0] + r_refs[0][0]
        for j in range(1, 4):
            out_ref[j - 1] = (g_refs[j][0] + r_refs[j][0]).astype(out_ref.dtype)

    def g_map(j):
        return lambda i, p: (2 * jnp.bitwise_xor(p[0], j) + p[1], i, 0)

    def r_map(j):
        return lambda i, p: (jnp.bitwise_xor(p[0], j), i, 0)

    grid_spec = pltpu.PrefetchScalarGridSpec(
        num_scalar_prefetch=1, grid=(R // tr,),
        in_specs=[pl.BlockSpec((1, tr, C), g_map(j)) for j in range(4)]
        + [pl.BlockSpec((1, tr, C), r_map(j)) for j in range(4)],
        out_specs=[pl.BlockSpec((tr, C), lambda i, p: (i, 0)), pl.BlockSpec((3, tr, C), lambda i, p: (0, i, 0))])
    return pl.pallas_call(
        body, name=name, grid_spec=grid_spec,
        out_shape=[jax.ShapeDtypeStruct((R, C), F32), jax.ShapeDtypeStruct((3, R, C), _MXU)],
    )(place, *([grad] * 4), *([recv] * 4))


def _adamw_math(w, g, m, v):
    m = ADAM_B1 * m + (1.0 - ADAM_B1) * g
    v = ADAM_B2 * v + (1.0 - ADAM_B2) * (g * g)
    m_hat = m / (1.0 - ADAM_B1 ** ADAM_STEP)
    v_hat = v / (1.0 - ADAM_B2 ** ADAM_STEP)
    delta = -ADAM_LR * (m_hat / (jnp.sqrt(v_hat) + ADAM_EPS) + ADAM_WD * w)
    return delta, m, v


def _adamw(w, g, m, v, name):
    R, C = w.shape
    tr = _blk(R, 256)

    def body(w_ref, g_ref, m_ref, v_ref, d_ref, nm_ref, nv_ref):
        d, nm, nv = _adamw_math(w_ref[...], g_ref[...], m_ref[...], v_ref[...])
        d_ref[...] = d
        nm_ref[...] = nm
        nv_ref[...] = nv

    spec = pl.BlockSpec((tr, C), lambda i: (i, 0))
    return pl.pallas_call(
        body, name=name, grid=(R // tr,), in_specs=[spec] * 4, out_specs=[spec] * 3,
        out_shape=[jax.ShapeDtypeStruct((R, C), F32)] * 3,
    )(w, g, m, v)


def _rs_sum_adamw(owns, recvs, w, m, v, name, ex=None):
    L, R, C = w.shape
    tr = _blk(R, 256)
    nr = R // tr

    def body(o0, o1, r0, r1, w_ref, m_ref, v_ref, g_ref, d_ref, nm_ref, nv_ref):
        def step(o_ref, r_ref):
            g = o_ref[...]
            for j in range(3):
                g = g + r_ref[j].astype(F32)
            d, nm, nv = _adamw_math(w_ref[0], g, m_ref[0], v_ref[0])
            g_ref[0] = g
            d_ref[0] = d
            nm_ref[0] = nm
            nv_ref[0] = nv

        pl.when(pl.program_id(0) == 0)(lambda: step(o0, r0))
        pl.when(pl.program_id(0) == 1)(lambda: step(o1, r1))

    def hold(layer):
        if layer == 0:
            return lambda l, i: i * (1 - l) + (nr - 1) * l
        return lambda l, i: i * l

    own_spec = [pl.BlockSpec((tr, C), functools.partial(lambda l, i, f: (f(l, i), 0), f=hold(k))) for k in range(2)]
    recv_spec = [pl.BlockSpec((3, tr, C), functools.partial(lambda l, i, f: (0, f(l, i), 0), f=hold(k)))
                 for k in range(2)]
    lay = pl.BlockSpec((1, tr, C), lambda l, i: (l, i, 0))
    return _call_hosted(
        body, name, (L, nr), own_spec + recv_spec + [lay] * 3, [lay] * 4,
        [jax.ShapeDtypeStruct((L, R, C), F32)] * 4, [], [owns[0], owns[1], recvs[0], recvs[1], w, m, v], ex)


def _silu(x):
    return x / (1.0 + jnp.exp(-x))


def _mod_partial(c_all, w_ada, b_cols, name):
    L, D, Ca = w_ada.shape

    def body(c_ref, w_ref, b_ref, o_ref):
        act = _silu(c_ref[...]).astype(_MXU)
        for l in range(L):
            o_ref[:, l * Ca:(l + 1) * Ca] = _nn(act, w_ref[l].astype(_MXU)) + b_ref[l:l + 1, :]

    return pl.pallas_call(
        body, name=name, out_shape=jax.ShapeDtypeStruct((NDEV, L * Ca), F32),
        in_specs=[VMEM_SPEC] * 3, out_specs=VMEM_SPEC,
    )(c_all, w_ada, b_cols)


def _w_ada_grad(c_t, dmod_cols, name):
    L, _, Ca = dmod_cols.shape
    D = c_t.shape[0]

    def body(c_ref, d_ref, o_ref):
        act = _silu(c_ref[...]).astype(_MXU)
        for l in range(L):
            o_ref[l] = _nn(act, d_ref[l].astype(_MXU))

    return pl.pallas_call(
        body, name=name, out_shape=jax.ShapeDtypeStruct((L, D, Ca), F32),
        in_specs=[VMEM_SPEC] * 2, out_specs=VMEM_SPEC,
    )(c_t, dmod_cols)


def _sum_devices(gathered, name):
    _, _, N = gathered.shape

    def body(x_ref, o_ref):
        acc = x_ref[0]
        for d in range(1, NDEV):
            acc = acc + x_ref[d]
        o_ref[...] = acc

    return pl.pallas_call(
        body, name=name, out_shape=jax.ShapeDtypeStruct((1, N), F32),
        in_specs=[VMEM_SPEC], out_specs=VMEM_SPEC,
    )(gathered)


def _ln_mod_matmul(x, g, sc, sh, w, name):
    S, D = x.shape
    N = w.shape[1]
    tm = _blk(S, ROW_BLOCK)

    def body(x_ref, g_ref, sc_ref, sh_ref, w_ref, o_ref, h_ref):
        xv = x_ref[...]
        r = lax.rsqrt(jnp.mean(xv * xv, axis=-1, keepdims=True) + EPS)
        hv = ((xv * r) * g_ref[...]) * (1.0 + sc_ref[...]) + sh_ref[...]
        hb = hv.astype(_MXU)
        h_ref[...] = hb
        o_ref[...] = _nn(hb, w_ref[...]).astype(o_ref.dtype)

    vec = pl.BlockSpec((1, D), lambda i: (0, 0))
    row = lambda width: pl.BlockSpec((tm, width), lambda i: (i, 0))
    return pl.pallas_call(
        body, name=name, grid=(S // tm,),
        in_specs=[row(D), vec, vec, vec, pl.BlockSpec((D, N), lambda i: (0, 0))],
        out_specs=[row(N), row(D)],
        out_shape=[jax.ShapeDtypeStruct((S, N), _MXU), jax.ShapeDtypeStruct((S, D), _MXU)],
    )(x, g, sc, sh, w)


def _matmul_res_gate(a, w, xres, gt, relu2, name):
    S, K = a.shape
    N = w.shape[1]
    tm = _blk(S, 512)

    def body(a_ref, w_ref, x_ref, gt_ref, o_ref, f_ref):
        av = a_ref[...]
        if relu2:
            af = jnp.maximum(av.astype(F32), 0.0)
            av = (af * af).astype(_MXU)
        f = _nn(av, w_ref[...])
        f_ref[...] = f.astype(f_ref.dtype)
        o_ref[...] = x_ref[...] + gt_ref[...] * f

    row = lambda width: pl.BlockSpec((tm, width), lambda i: (i, 0))
    return pl.pallas_call(
        body, name=name, grid=(S // tm,),
        in_specs=[row(K), pl.BlockSpec((K, N), lambda i: (0, 0)), row(N), pl.BlockSpec((1, N), lambda i: (0, 0))],
        out_specs=[row(N), row(N)],
        out_shape=[jax.ShapeDtypeStruct((S, N), F32), jax.ShapeDtypeStruct((S, N), _MXU)],
    )(a, w, xres, gt)


def _loss_grad(y, t, name):
    S, D = y.shape
    tm = _blk(S, 512)
    last = S // tm - 1

    def body(y_ref, t_ref, dy_ref, l_ref, acc_ref):
        i = pl.program_id(0)
        e = y_ref[...] - t_ref[...]
        dy_ref[...] = e * (1.0 / D)
        part = jnp.sum(e * e, axis=0, keepdims=True)

        @pl.when(i == 0)
        def _():
            acc_ref[...] = part

        @pl.when(i > 0)
        def _():
            acc_ref[...] += part

        @pl.when(i == last)
        def _():
            l_ref[...] = (0.5 / D) * jnp.sum(acc_ref[...], axis=1, keepdims=True)

    row = pl.BlockSpec((tm, D), lambda i: (i, 0))
    return pl.pallas_call(
        body, name=name, grid=(S // tm,), in_specs=[row, row],
        out_specs=[row, pl.BlockSpec((1, 1), lambda i: (0, 0))],
        out_shape=[jax.ShapeDtypeStruct((S, D), F32), jax.ShapeDtypeStruct((1, 1), F32)],
        scratch_shapes=[pltpu.VMEM((1, D), F32)],
    )(y, t)


def _accumulate(ref, part, first):
    @pl.when(first)
    def _():
        ref[...] = part

    @pl.when(jnp.logical_not(first))
    def _():
        ref[...] += part


def _gate_nt_matmul(dx, f, gt, w, u, name):
    S, D = dx.shape
    N = w.shape[0]
    tm = _blk(S, ROW_BLOCK)
    with_u = u is not None

    def body(*refs):
        if with_u:
            dx_ref, f_ref, gt_ref, w_ref, u_ref, dz_ref, dgt_ref, res_ref = refs
        else:
            dx_ref, f_ref, gt_ref, w_ref, dz_ref, dgt_ref, res_ref = refs
        dxv = dx_ref[...]
        dz = (dxv * gt_ref[...]).astype(_MXU)
        dz_ref[...] = dz
        _accumulate(dgt_ref, jnp.sum(dxv * f_ref[...].astype(F32), axis=0, keepdims=True), pl.program_id(0) == 0)
        r = _nt(dz, w_ref[...])
        if with_u:
            r = r * (2.0 * jnp.maximum(u_ref[...].astype(F32), 0.0))
        res_ref[...] = r.astype(res_ref.dtype)

    row = lambda width: pl.BlockSpec((tm, width), lambda i: (i, 0))
    in_specs = [row(D), row(D), pl.BlockSpec((1, D), lambda i: (0, 0)), pl.BlockSpec((N, D), lambda i: (0, 0))]
    args = [dx, f, gt, w]
    if with_u:
        in_specs.append(row(N))
        args.append(u)
    return pl.pallas_call(
        body, name=name, grid=(S // tm,), in_specs=in_specs,
        out_specs=[row(D), pl.BlockSpec((1, D), lambda i: (0, 0)), row(N)],
        out_shape=[jax.ShapeDtypeStruct((S, D), _MXU), jax.ShapeDtypeStruct((1, D), F32),
                   jax.ShapeDtypeStruct((S, N), _MXU)],
    )(*args)


def _tn_matmul(a, b, by_col, relu2, name):
    S, Ka = a.shape
    Nb = b.shape[1]
    ts = _blk(S, 2 * ROW_BLOCK)
    half = NDEV // 2
    if by_col:
        R, C = Ka, Nb // NDEV
        a_spec = pl.BlockSpec((ts, Ka), lambda h, k: (k, 0))
        b_spec = pl.BlockSpec((ts, half * C), lambda h, k: (k, h))
    else:
        R, C = Ka // NDEV, Nb
        a_spec = pl.BlockSpec((ts, half * R), lambda h, k: (k, h))
        b_spec = pl.BlockSpec((ts, Nb), lambda h, k: (k, 0))

    def body(a_ref, b_ref, o_ref):
        av = a_ref[...]
        if relu2:
            af = jnp.maximum(av.astype(F32), 0.0)
            av = (af * af).astype(_MXU)
        p = _tn(av, b_ref[...])
        first = pl.program_id(1) == 0
        for d in range(half):
            part = p[:, d * C:(d + 1) * C] if by_col else p[d * R:(d + 1) * R, :]
            _accumulate(o_ref.at[d], part, first)

    return pl.pallas_call(
        body, name=name, grid=(NDEV // half, S // ts), in_specs=[a_spec, b_spec],
        out_specs=pl.BlockSpec((half, R, C), lambda h, k: (h, 0, 0)),
        out_shape=jax.ShapeDtypeStruct((NDEV, R, C), F32),
    )(a, b)


def _nt_ln_bwd(dy, w, x, g, sc, sh, dxres, name, ex=None):
    S, D = x.shape
    N = w.shape[1]
    tm = _blk(S, ROW_BLOCK)

    def body(dy_ref, w_ref, x_ref, g_ref, sc_ref, sh_ref, dxr_ref, dx_ref, dsh_ref, dsc_ref, dg_ref):
        dh = _nt(dy_ref[...], w_ref[...])
        xv = x_ref[...]
        r = lax.rsqrt(jnp.mean(xv * xv, axis=-1, keepdims=True) + EPS)
        xhat = xv * r
        gv = g_ref[...]
        dn = dh * (1.0 + sc_ref[...])
        dxhat = dn * gv
        dxv = r * (dxhat - xhat * jnp.mean(dxhat * xhat, axis=-1, keepdims=True))
        dx_ref[...] = dxr_ref[...] + dxv
        first = pl.program_id(0) == 0
        _accumulate(dsh_ref, jnp.sum(dh, axis=0, keepdims=True), first)
        _accumulate(dsc_ref, jnp.sum(dh * (xhat * gv), axis=0, keepdims=True), first)
        _accumulate(dg_ref, jnp.sum(dn * xhat, axis=0, keepdims=True), first)

    row = lambda width: pl.BlockSpec((tm, width), lambda i: (i, 0))
    vec = pl.BlockSpec((1, D), lambda i: (0, 0))
    return _call_hosted(
        body, name, (S // tm,),
        [row(N), pl.BlockSpec((D, N), lambda i: (0, 0)), row(D), vec, vec, vec, row(D)],
        [row(D), vec, vec, vec], [jax.ShapeDtypeStruct((S, D), F32)] + [jax.ShapeDtypeStruct((1, D), F32)] * 3,
        [], [dy, w, x, g, sc, sh, dxres], ex)


def _split2(v):
    hi = v.astype(_MXU)
    mid = (v - hi.astype(F32)).astype(_MXU)
    return hi, mid


def _tri_sums(vs, tri2):
    T = vs[0].shape[0]
    out = []
    for j in range(len(vs) // 2):
        wide = [jnp.concatenate(_split2(vs[2 * j + e]), axis=1) for e in range(2)]
        for both in _per_head(_nn(jnp.concatenate(wide, axis=0), tri2), T):
            out.append((both[:, :T], both[:, T:]))
    return out


def _tri2(T, inclusive):
    j = lax.broadcasted_iota(jnp.int32, (2 * T, 2 * T), 0) % T
    s = lax.broadcasted_iota(jnp.int32, (2 * T, 2 * T), 1)
    keep = (j >= s) if inclusive else (j > s)
    return jnp.where((s >= T) | keep, 1.0, 0.0).astype(_MXU)


def _log_sigmoid(z):
    return jnp.minimum(z, 0.0) - jnp.log(1.0 + jnp.exp(-jnp.abs(z)))


def _per_head(tall, T):
    return [tall[h * T:(h + 1) * T] for h in range(tall.shape[0] // T)]


def _sb_blocks(q_tall, k2, strict, tri2, carry):
    T = k2[0].shape[0]
    zs = []
    for qt, kblk in zip(q_tall, k2):
        zs += _per_head(_nt(qt, kblk), T)
    lbs, l1s = [], []
    for z in zs:
        lb = _log_sigmoid(z)
        l1 = lb - z
        if strict is not None:
            l1 = jnp.where(strict, l1, 0.0)
        lbs.append(lb)
        l1s.append(l1)
    sums = _tri_sums(l1s, tri2)
    amps, new_carry = [], []
    for lb, (sfx, tot), c in zip(lbs, sums, carry):
        a = jnp.exp(lb + sfx + c)
        if strict is not None:
            a = jnp.where(strict, a, 0.0)
        amps.append(a)
        new_carry.append(c + tot)
    return lbs, amps, new_carry


def _sb_alive(carry):
    top = carry[0]
    for c in carry[1:]:
        top = jnp.maximum(top, c)
    return jnp.max(top) > SB_SKIP


def _skew_index():
    i = lax.broadcasted_iota(jnp.int32, (CA_T, SKEW_W + 1), 0)
    m = lax.broadcasted_iota(jnp.int32, (CA_T, SKEW_W + 1), 1)
    wrapped = i + m >= SKEW_W
    row = jnp.where(wrapped, i + 1, i)
    j = jnp.where(wrapped, i + m - SKEW_W, i + m)
    a = row // CHUNK
    jj = j - a * CHUNK
    inband = (jj >= 0) & (jj < BAND) & (j < CA_W) & (row < CA_T)
    idx = jnp.clip((row - a * CHUNK) + PAD - jj, -REL_CLIP, REL_CLIP) + REL_CLIP
    return inband, idx, wrapped


def _skew(tile):
    H = tile.shape[0]
    flat = jnp.pad(tile, ((0, 0), (0, 0), (0, SKEW_W - CA_W))).reshape(H, CA_T * SKEW_W)
    return jnp.pad(flat, ((0, 0), (0, CA_T))).reshape(H, CA_T, SKEW_W + 1)


def _unskew(view):
    H = view.shape[0]
    flat = view.reshape(H, CA_T * (SKEW_W + 1))[:, :CA_T * SKEW_W]
    return flat.reshape(H, CA_T, SKEW_W)[:, :, :CA_W]


def _ca_bias(rel_bias, name):
    H = rel_bias.shape[0]
    top = rel_bias[:, N_REL - 1:]
    by_offset = jnp.concatenate(
        [jnp.broadcast_to(top, (H, PAD - REL_CLIP + 1)), jnp.flip(rel_bias[:, :N_REL - 1], axis=1),
         jnp.broadcast_to(top, (H, SKEW_W + 1 - (PAD - REL_CLIP + 1) - (N_REL - 1)))], axis=1)

    def body(t_ref, o_ref):
        inband, _, wrapped = _skew_index()
        vals = jnp.where(wrapped, t_ref[0][:, 0:1], t_ref[0])
        o_ref[0] = jnp.where(inband, vals, NEG)

    view = pl.pallas_call(
        body, name=name, grid=(H,), in_specs=[pl.BlockSpec((1, 1, SKEW_W + 1), lambda h: (h, 0, 0))],
        out_specs=pl.BlockSpec((1, CA_T, SKEW_W + 1), lambda h: (h, 0, 0)),
        out_shape=jax.ShapeDtypeStruct((H, CA_T, SKEW_W + 1), F32),
    )(by_offset.reshape(H, 1, SKEW_W + 1))
    return _unskew(view)


def _ca_bias_bwd(dbias, name):
    H = dbias.shape[0]

    def body(d_ref, o_ref):
        inband, idx, _ = _skew_index()
        d = jnp.where(inband, d_ref[0], 0.0)
        clipped = idx == N_REL - 1
        by_offset = jnp.sum(jnp.where(clipped, 0.0, d), axis=0, keepdims=True)
        top = jnp.sum(jnp.sum(jnp.where(clipped, d, 0.0), axis=0, keepdims=True), axis=1, keepdims=True)
        lane = lax.broadcasted_iota(jnp.int32, (1, SKEW_W + 1), 1)
        o_ref[0] = jnp.where(lane == 0, top, by_offset)

    out = pl.pallas_call(
        body, name=name, grid=(H,), in_specs=[pl.BlockSpec((1, CA_T, SKEW_W + 1), lambda h: (h, 0, 0))],
        out_specs=pl.BlockSpec((1, 1, SKEW_W + 1), lambda h: (h, 0, 0)),
        out_shape=jax.ShapeDtypeStruct((H, 1, SKEW_W + 1), F32),
    )(_skew(dbias))[:, 0]
    first = PAD - REL_CLIP + 1
    return jnp.concatenate([jnp.flip(out[:, first:first + N_REL - 1], axis=1), out[:, 0:1]], axis=1)


def _low_lanes(rows):
    return lax.broadcasted_iota(jnp.int32, (rows, PAIR), 1) < HEAD_DIM


def _one_head(t2, low, first, scale=1.0):
    tf = t2.astype(F32) * scale
    return (jnp.where(low, tf, 0.0) if first else jnp.where(low, 0.0, tf)).astype(_MXU)


def _two_heads(t2, low, scale=1.0):
    return jnp.concatenate([_one_head(t2, low, True, scale), _one_head(t2, low, False, scale)], axis=0)


def _sb_fwd(proj, name, ex=None):
    S, W = proj.shape
    half = W // 6
    npair = half // PAIR
    T = _blk(S, SB_T)
    GP = _blk(npair, SB_PAIRS)
    GW = GP * PAIR
    nb = npair // GP

    def body(q_ref, k_ref, v_ref, o_ref, ox_ref):
        qi = pl.program_id(1)
        low = _low_lanes(T)
        q_tall = []
        for j in range(GP):
            q2 = q_ref[:, j * PAIR:(j + 1) * PAIR]
            q_tall.append(_two_heads(q2, low, HEAD_DIM ** -0.5))
        row = lax.broadcasted_iota(jnp.int32, (T, T), 0)
        col = lax.broadcasted_iota(jnp.int32, (T, T), 1)
        tri2 = _tri2(T, inclusive=False)

        def pairs(kb, carry, acc, fine, strict):
            rows = pl.ds(pl.multiple_of(kb * T, T), T)
            k2 = [k_ref[rows, j * PAIR:(j + 1) * PAIR] for j in range(GP)]
            v2 = [v_ref[rows, j * PAIR:(j + 1) * PAIR] for j in range(GP)]
            _, amps, carry = _sb_blocks(q_tall, k2, strict, tri2, carry)
            parts = [_split2(a) for a in amps]
            new_acc, new_fine = [], []
            for j in range(GP):
                tall = jnp.concatenate([parts[2 * j][0], parts[2 * j + 1][0], parts[2 * j][1], parts[2 * j + 1][1]],
                                       axis=0)
                hi0, hi1, mid0, mid1 = _per_head(_nn(tall, v2[j]), T)
                new_acc.append(acc[j] + jnp.where(low, hi0, hi1))
                new_fine.append(fine[j] + jnp.where(low, mid0, mid1))
            return tuple(carry), tuple(new_acc), tuple(new_fine)

        zero = (jnp.zeros((T, PAIR), F32),) * GP
        carry, acc, fine = pairs(qi, (jnp.zeros((T, T), F32),) * (2 * GP), zero, zero, col < row)

        def cond(st):
            kb, carry, _, _ = st
            return jnp.logical_and(kb >= 0, _sb_alive(carry))

        def step(st):
            kb, carry, acc, fine = st
            carry, acc, fine = pairs(kb, carry, acc, fine, None)
            return kb - 1, carry, acc, fine

        _, _, acc, fine = lax.while_loop(cond, step, (qi - 1, carry, acc, fine))
        for j in range(GP):
            o_ref[:, j * PAIR:(j + 1) * PAIR] = acc[j].astype(o_ref.dtype)
            ox_ref[:, j * PAIR:(j + 1) * PAIR] = acc[j] + fine[j]

    blk = pl.BlockSpec((T, GW), lambda p, i: (i, p))
    return _call_hosted(
        body, name, (nb, S // T),
        [blk, pl.BlockSpec((S, GW), lambda p, i: (0, nb + p)), pl.BlockSpec((S, GW), lambda p, i: (0, 2 * nb + p))],
        [blk, blk], [jax.ShapeDtypeStruct((S, half), _MXU), jax.ShapeDtypeStruct((S, half), F32)],
        [], [proj, proj, proj], ex)


def _sb_bwd(proj, ox, dmixed, name, ex=None):
    S, W = proj.shape
    half = W // 6
    npair = half // PAIR
    T = _blk(S, SB_T)
    GP = _blk(npair, SB_PAIRS)
    GW = GP * PAIR
    nb = npair // GP
    last = S // T - 1
    scale = HEAD_DIM ** -0.5

    def body(q_ref, k_ref, v_ref, ox_ref, do_ref, dq_ref, dk_ref, dv_ref, dka_ref, dva_ref):
        qi = pl.program_id(1)

        @pl.when(qi == 0)
        def _():
            dka_ref[...] = jnp.zeros_like(dka_ref)
            dva_ref[...] = jnp.zeros_like(dva_ref)

        low = _low_lanes(T)
        q2, do2, q_tall, do_tall, deltas = [], [], [], [], []
        for j in range(GP):
            cols = slice(j * PAIR, (j + 1) * PAIR)
            q2.append(q_ref[:, cols])
            do2.append(do_ref[:, cols])
            q_tall.append(_two_heads(q2[j], low, scale))
            dobs = [_one_head(do2[j], low, True), _one_head(do2[j], low, False)]
            do_tall.append(jnp.concatenate(dobs, axis=0))
            for e in range(2):
                deltas.append(jnp.sum(dobs[e].astype(F32) * ox_ref[:, cols], axis=-1, keepdims=True))
        row = lax.broadcasted_iota(jnp.int32, (T, T), 0)
        col = lax.broadcasted_iota(jnp.int32, (T, T), 1)
        tri_ex = _tri2(T, inclusive=False)
        tri_in = _tri2(T, inclusive=True)

        def pairs(kb, carry, right, dq, strict):
            rows = pl.ds(pl.multiple_of(kb * T, T), T)
            k2 = [k_ref[rows, j * PAIR:(j + 1) * PAIR] for j in range(GP)]
            v2 = [v_ref[rows, j * PAIR:(j + 1) * PAIR] for j in range(GP)]
            nh = 2 * GP
            gs = []
            for j in range(GP):
                gs += _per_head(_nt(do_tall[j], v2[j]), T)
            lbs, amps, carry = _sb_blocks(q_tall, k2, strict, tri_ex, carry)
            ags = [a * gg for a, gg in zip(amps, gs)]
            sums = _tri_sums(ags, tri_in)
            dzbs = []
            for h in range(nh):
                left = deltas[h] - (sums[h][0] + right[h])
                beta = jnp.exp(lbs[h])
                dz = ags[h] - beta * (ags[h] + left)
                if strict is not None:
                    dz = jnp.where(strict, dz, 0.0)
                dzbs.append(dz.astype(_MXU))
            abs_ = [a.astype(_MXU) for a in amps]
            new_dq = []
            for j in range(GP):
                cols = slice(j * PAIR, (j + 1) * PAIR)
                dk0, dk1 = _per_head(_tn(jnp.concatenate(dzbs[2 * j:2 * j + 2], axis=1), q2[j]), T)
                dv0, dv1 = _per_head(_tn(jnp.concatenate(abs_[2 * j:2 * j + 2], axis=1), do2[j]), T)
                dq0, dq1 = _per_head(_nn(jnp.concatenate(dzbs[2 * j:2 * j + 2], axis=0), k2[j]), T)
                dka_ref[rows, cols] += jnp.where(low, dk0, dk1)
                dva_ref[rows, cols] += jnp.where(low, dv0, dv1)
                new_dq.append(dq[j] + jnp.where(low, dq0, dq1))
            right = tuple(right[h] + sums[h][1] for h in range(nh))
            return tuple(carry), right, tuple(new_dq)

        zero = (jnp.zeros((T, T), F32),) * (2 * GP)
        carry, right, dq = pairs(qi, zero, zero, (jnp.zeros((T, PAIR), F32),) * GP, col < row)

        def cond(st):
            kb, carry, _, _ = st
            return jnp.logical_and(kb >= 0, _sb_alive(carry))

        def step(st):
            kb, carry, right, dq = st
            carry, right, dq = pairs(kb, carry, right, dq, None)
            return kb - 1, carry, right, dq

        _, _, _, dq = lax.while_loop(cond, step, (qi - 1, carry, right, dq))
        for j in range(GP):
            dq_ref[:, j * PAIR:(j + 1) * PAIR] = (dq[j] * scale).astype(dq_ref.dtype)

        @pl.when(qi == last)
        def _():
            dk_ref[...] = (dka_ref[...] * scale).astype(dk_ref.dtype)
            dv_ref[...] = dva_ref[...].astype(dv_ref.dtype)

    blk = pl.BlockSpec((T, GW), lambda p, i: (i, p))
    full = pl.BlockSpec((S, GW), lambda p, i: (0, p))
    return _call_hosted(
        body, name, (nb, S // T),
        [blk, pl.BlockSpec((S, GW), lambda p, i: (0, nb + p)), pl.BlockSpec((S, GW), lambda p, i: (0, 2 * nb + p)),
         blk, blk],
        [blk, full, full], [jax.ShapeDtypeStruct((S, half), _MXU)] * 3,
        [pltpu.VMEM((S, GW), F32), pltpu.VMEM((S, GW), F32)], [proj, proj, proj, ox, dmixed], ex)


def _pair_norm(t2, g2, low):
    tf = t2.astype(F32)
    sq = tf * tf
    both = jnp.sum(sq, axis=-1, keepdims=True)
    first = jnp.sum(jnp.where(low, sq, 0.0), axis=-1, keepdims=True)
    r = jnp.where(low, lax.rsqrt(first * (1.0 / HEAD_DIM) + EPS), lax.rsqrt((both - first) * (1.0 / HEAD_DIM) + EPS))
    hat = tf * r
    return hat * g2, hat, r


def _pair_norm_bwd(dn, hat, r, g2, low):
    dhat = dn * g2
    prod = dhat * hat
    both = jnp.sum(prod, axis=-1, keepdims=True)
    first = jnp.sum(jnp.where(low, prod, 0.0), axis=-1, keepdims=True)
    mean = jnp.where(low, first, both - first) * (1.0 / HEAD_DIM)
    return r * (dhat - hat * mean)


def _ca_fill(j, k_ref, v_ref, gk_ref, kn_ref, vp_ref):
    S = k_ref.shape[0]
    cols = slice(j * PAIR, (j + 1) * PAIR)
    kn, _, _ = _pair_norm(k_ref[:, cols], gk_ref[...], _low_lanes(S))
    kn_ref[j, 0:PAD, :] = jnp.zeros((PAD, PAIR), kn_ref.dtype)
    vp_ref[j, 0:PAD, :] = jnp.zeros((PAD, PAIR), vp_ref.dtype)
    kn_ref[j, PAD:PAD + S, :] = kn.astype(kn_ref.dtype)
    vp_ref[j, PAD:PAD + S, :] = v_ref[:, cols]


def _ca_scores(j, q_ref, b2_ref, gq_ref, kn_ref, qi, low):
    qn, qhat, r = _pair_norm(q_ref[:, j * PAIR:(j + 1) * PAIR], gq_ref[...], low)
    qn = qn * HEAD_DIM ** -0.5
    band = pl.ds(pl.multiple_of(qi * CA_T, CA_T), CA_W)
    key_pos = qi * CA_T - PAD + lax.broadcasted_iota(jnp.int32, (CA_T, CA_W), 1)
    both = _per_head(_nt(_two_heads(qn, low), kn_ref[j, band, :]), CA_T)
    scores = [jnp.where(key_pos >= 0, both[e] + b2_ref[2 * j + e], NEG) for e in range(2)]
    return scores, qn.astype(_MXU), qhat, r


def _softmax(s):
    e = jnp.exp(s - jnp.max(s, axis=-1, keepdims=True))
    return e * (1.0 / jnp.sum(e, axis=-1, keepdims=True))


def _ca_fwd(proj, bias2, gq2, gk2, name, ex=None):
    S, W = proj.shape
    half = W // 6
    npair = half // PAIR
    GP = _blk(npair, CA_PAIRS_FWD)
    GW = GP * PAIR
    nb = npair // GP

    def body(q_ref, k_ref, v_ref, b2_ref, gq_ref, gk_ref, o_ref, kn_ref, vp_ref):
        qi = pl.program_id(1)

        @pl.when(qi == 0)
        def _():
            for j in range(GP):
                _ca_fill(j, k_ref, v_ref, gk_ref, kn_ref, vp_ref)

        low = _low_lanes(CA_T)
        band = pl.ds(pl.multiple_of(qi * CA_T, CA_T), CA_W)
        scores = [_ca_scores(j, q_ref, b2_ref, gq_ref, kn_ref, qi, low)[0] for j in range(GP)]
        probs = [[_softmax(s).astype(_MXU) for s in pair] for pair in scores]
        for j in range(GP):
            outs = _per_head(_nn(jnp.concatenate(probs[j], axis=0), vp_ref[j, band, :]), CA_T)
            o_ref[:, j * PAIR:(j + 1) * PAIR] = jnp.where(low, outs[0], outs[1]).astype(o_ref.dtype)

    vec = pl.BlockSpec((1, PAIR), lambda p, i: (0, 0))
    return _call_hosted(
        body, name, (nb, S // CA_T),
        [pl.BlockSpec((CA_T, GW), lambda p, i: (i, 3 * nb + p)),
         pl.BlockSpec((S, GW), lambda p, i: (0, 4 * nb + p)), pl.BlockSpec((S, GW), lambda p, i: (0, 5 * nb + p)),
         pl.BlockSpec((2 * GP, CA_T, CA_W), lambda p, i: (p, 0, 0)), vec, vec],
        [pl.BlockSpec((CA_T, GW), lambda p, i: (i, p))], [jax.ShapeDtypeStruct((S, half), _MXU)],
        [pltpu.VMEM((GP, PAD + S, PAIR), _MXU), pltpu.VMEM((GP, PAD + S, PAIR), _MXU)],
        [proj, proj, proj, bias2, gq2, gk2], ex)


def _ca_bwd(proj, bias2, gq2, gk2, dmixed, name, ex=None):
    S, W = proj.shape
    half = W // 6
    npair = half // PAIR
    GP = _blk(npair, CA_PAIRS_BWD)
    GW = GP * PAIR
    nb = npair // GP
    scale = HEAD_DIM ** -0.5
    last = S // CA_T - 1

    def body(q_ref, k_ref, v_ref, b2_ref, gq_ref, gk_ref, do_ref,
             dq_ref, dk_ref, dv_ref, db_ref, dgq_ref, dgk_ref, kn_ref, vp_ref, dkn_ref, dvp_ref):
        p_id, qi = pl.program_id(0), pl.program_id(1)

        @pl.when(qi == 0)
        def _():
            for j in range(GP):
                _ca_fill(j, k_ref, v_ref, gk_ref, kn_ref, vp_ref)
            dkn_ref[...] = jnp.zeros_like(dkn_ref)
            dvp_ref[...] = jnp.zeros_like(dvp_ref)
            db_ref[...] = jnp.zeros_like(db_ref)

        @pl.when(jnp.logical_and(p_id == 0, qi == 0))
        def _():
            dgq_ref[...] = jnp.zeros_like(dgq_ref)
            dgk_ref[...] = jnp.zeros_like(dgk_ref)

        low = _low_lanes(CA_T)
        top_w = lax.broadcasted_iota(jnp.int32, (PAIR, CA_W), 0) < HEAD_DIM
        band = pl.ds(pl.multiple_of(qi * CA_T, CA_T), CA_W)
        pairs = [_ca_scores(j, q_ref, b2_ref, gq_ref, kn_ref, qi, low) for j in range(GP)]
        do2 = [do_ref[:, j * PAIR:(j + 1) * PAIR] for j in range(GP)]
        dps = [_per_head(_nt(_two_heads(do2[j], low), vp_ref[j, band, :]), CA_T) for j in range(GP)]
        probs, dsbs = [], []
        for j in range(GP):
            pj, dj = [], []
            for e in range(2):
                p = _softmax(pairs[j][0][e])
                ds = p * (dps[j][e] - jnp.sum(p * dps[j][e], axis=-1, keepdims=True))
                db_ref[2 * j + e] += ds
                pj.append(p.astype(_MXU))
                dj.append(ds.astype(_MXU))
            probs.append(pj)
            dsbs.append(dj)
        dgq = jnp.zeros((1, PAIR), F32)
        for j in range(GP):
            _, qn, qhat, r = pairs[j]
            dq_h = _per_head(_nn(jnp.concatenate(dsbs[j], axis=0), kn_ref[j, band, :]), CA_T)
            dk_t = _tn(qn, jnp.concatenate(dsbs[j], axis=1))
            dv_t = _tn(do2[j], jnp.concatenate(probs[j], axis=1))
            dkn_ref[j, :, band] += jnp.where(top_w, dk_t[:, :CA_W], dk_t[:, CA_W:])
            dvp_ref[j, :, band] += jnp.where(top_w, dv_t[:, :CA_W], dv_t[:, CA_W:])
            dqn = jnp.where(low, dq_h[0], dq_h[1]) * scale
            dgq = dgq + jnp.sum(dqn * qhat, axis=0, keepdims=True)
            dq_ref[:, j * PAIR:(j + 1) * PAIR] = _pair_norm_bwd(dqn, qhat, r, gq_ref[...], low).astype(dq_ref.dtype)
        dgq_ref[...] += dgq

        @pl.when(qi == last)
        def _():
            low_s = _low_lanes(S)
            for j in range(GP):
                cols = slice(j * PAIR, (j + 1) * PAIR)
                _, khat, rk = _pair_norm(k_ref[:, cols], gk_ref[...], low_s)
                dkn = dkn_ref[j, :, PAD:PAD + S].T
                dgk_ref[...] += jnp.sum(dkn * khat, axis=0, keepdims=True)
                dk_ref[:, cols] = _pair_norm_bwd(dkn, khat, rk, gk_ref[...], low_s).astype(dk_ref.dtype)
                dv_ref[:, cols] = dvp_ref[j, :, PAD:PAD + S].T.astype(dv_ref.dtype)

    vec = pl.BlockSpec((1, PAIR), lambda p, i: (0, 0))
    tile = pl.BlockSpec((2 * GP, CA_T, CA_W), lambda p, i: (p, 0, 0))
    full = pl.BlockSpec((S, GW), lambda p, i: (0, p))
    return _call_hosted(
        body, name, (nb, S // CA_T),
        [pl.BlockSpec((CA_T, GW), lambda p, i: (i, 3 * nb + p)),
         pl.BlockSpec((S, GW), lambda p, i: (0, 4 * nb + p)), pl.BlockSpec((S, GW), lambda p, i: (0, 5 * nb + p)),
         tile, vec, vec, pl.BlockSpec((CA_T, GW), lambda p, i: (i, nb + p))],
        [pl.BlockSpec((CA_T, GW), lambda p, i: (i, p)), full, full, tile, vec, vec],
        [jax.ShapeDtypeStruct((S, half), _MXU)] * 3
        + [jax.ShapeDtypeStruct(bias2.shape, F32), jax.ShapeDtypeStruct((1, PAIR), F32),
           jax.ShapeDtypeStruct((1, PAIR), F32)],
        [pltpu.VMEM((GP, PAD + S, PAIR), _MXU), pltpu.VMEM((GP, PAD + S, PAIR), _MXU),
         pltpu.VMEM((GP, PAIR, PAD + S), F32), pltpu.VMEM((GP, PAIR, PAD + S), F32)],
        [proj, proj, proj, bias2, gq2, gk2, dmixed], ex)


def _pack_small(parts):
    flat = jnp.concatenate([p.reshape(-1) for layer in parts for p in layer])
    n = flat.shape[0]
    n_pad = -(-n // 1024) * 1024
    return jnp.pad(flat, (0, n_pad - n)).reshape(1, n_pad)


def _unpack_small(flat, shapes):
    out, off = [], 0
    for layer in shapes:
        cur = []
        for shp in layer:
            size = 1
            for s in shp:
                size *= s
            cur.append(flat[off:off + size].reshape(shp))
            off += size
        out.append(cur)
    return out


def kernel(x, c, g_norm1, w_in, g_q, g_k, rel_bias, w_o, g_norm2, w1, w2, w_ada, b_ada, loss_target, m_g_norm1, m_w_in, m_g_q, m_g_k, m_rel_bias, m_w_o, m_g_norm2, m_w1, m_w2, m_w_ada, m_b_ada, v_g_norm1, v_w_in, v_g_q, v_g_k, v_rel_bias, v_w_o, v_g_norm2, v_w1, v_w2, v_w_ada, v_b_ada):
    L = w_in.shape[0]
    S, D = x.shape[1:]
    H2 = D // HEAD_DIM // 2
    Ca = w_ada.shape[2]
    xi, yi, ci = _pos()
    me = 4 * xi + 2 * yi + ci
    place = jnp.stack([2 * xi + yi, ci]).astype(jnp.int32)

    c_all = _all_gather_small(c, "ag_c").reshape(NDEV, D)
    b_cols = lax.dynamic_slice(b_ada, (0, me * Ca), (L, Ca))
    mod_part = _mod_partial(c_all, w_ada, b_cols, "mod_partial")
    mod_all = _all_gather_small(mod_part, "ag_mod")
    mod = lax.dynamic_index_in_dim(mod_all, me, axis=1, keepdims=False)
    mod = mod.reshape(NDEV, L, Ca).transpose(1, 0, 2).reshape(L, 6, 1, D)

    wire = lambda a: a.astype(_MXU)
    by_cols = lambda g: g.transpose(1, 0, 2).reshape(D, g.shape[0] * g.shape[2])
    W_in = {0: by_cols(_run_exchange(_gather_exchange([wire(w_in[0])]), "ag_w_in0")[0])}
    W_o, W_1, W_2 = {}, {}, {}

    xs = [x[0]]
    saved = []
    for l in range(L):
        sh1, sc1, gt1, sh2, sc2, gt2 = [mod[l, i] for i in range(6)]
        gn1, gn2 = g_norm1[l:l + 1], g_norm2[l:l + 1]
        gq2, gk2 = jnp.tile(g_q[l:l + 1], (1, 2)), jnp.tile(g_k[l:l + 1], (1, 2))
        proj, h1 = _ln_mod_matmul(xs[-1], gn1, sc1, sh1, W_in[l], f"l{l}_proj")
        (o_sb, ox_sb), got = _sb_fwd(proj, f"l{l}_sb_fwd",
                                     _gather_exchange([wire(w_o[l]), wire(w1[l]), wire(w2[l])]))
        W_o[l], W_1[l], W_2[l] = got[0].reshape(D, D), by_cols(got[1]), got[2].reshape(4 * D, D)
        bias2 = _ca_bias(rel_bias[l], f"l{l}_ca_bias")
        (o_ca,), got = _ca_fwd(proj, bias2, gq2, gk2, f"l{l}_ca_fwd",
                               _gather_exchange([wire(w_in[l + 1])]) if l + 1 < L else None)
        if got:
            W_in[l + 1] = by_cols(got[0])
        mixed = jnp.concatenate([o_sb, o_ca], axis=1)
        x1, f1 = _matmul_res_gate(mixed, W_o[l], xs[-1], gt1, False, f"l{l}_attn_out")
        u, h2 = _ln_mod_matmul(x1, gn2, sc2, sh2, W_1[l], f"l{l}_mlp_in")
        x2, f2 = _matmul_res_gate(u, W_2[l], x1, gt2, True, f"l{l}_mlp_out")
        saved.append(dict(x0=xs[-1], h1=h1, proj=proj, ox_sb=ox_sb, bias2=bias2, mixed=mixed, f1=f1, x1=x1,
                          h2=h2, u=u, f2=f2))
        xs.append(x2)

    dx, loss_part = _loss_grad(xs[-1], loss_target[0], "loss")

    owns, recv_b = {}, {}
    pending = []
    small_parts = [None] * L

    def partials(keys, grads, recv_a):
        parts = []
        for key, g, r in zip(keys, grads, recv_a):
            owns[key], part = _rs_chip_partial(place, g, r, f"rs_partial_l{key[0]}_{key[1]}")
            parts.append(part)
        return parts

    for l in reversed(range(L)):
        sv = saved[l]
        sh1, sc1, gt1, sh2, sc2, gt2 = [mod[l, i] for i in range(6)]
        gn1, gn2 = g_norm1[l:l + 1], g_norm2[l:l + 1]
        gq2, gk2 = jnp.tile(g_q[l:l + 1], (1, 2)), jnp.tile(g_k[l:l + 1], (1, 2))
        dz2, dgt2, du = _gate_nt_matmul(dx, sv["f2"], gt2, W_2[l], sv["u"], f"l{l}_mlp_out_bwd")
        gw2 = _tn_matmul(sv["u"], dz2, False, True, f"l{l}_gw2")
        gw1 = _tn_matmul(sv["h2"], du, True, False, f"l{l}_gw1")
        (dx, dsh2, dsc2, dgn2), _ = _nt_ln_bwd(du, W_1[l], sv["x1"], gn2, sc2, sh2, dx, f"l{l}_mlp_in_bwd")
        dz1, dgt1, dmixed = _gate_nt_matmul(dx, sv["f1"], gt1, W_o[l], None, f"l{l}_attn_out_bwd")
        gwo = _tn_matmul(sv["mixed"], dz1, False, False, f"l{l}_gwo")
        keys, grads = [(l, 1), (l, 2), (l, 3)], [gwo, gw1, gw2]
        (dq_sb, dk_sb, dv_sb), recv_a = _sb_bwd(sv["proj"], sv["ox_sb"], dmixed, f"l{l}_sb_bwd",
                                                _sibling_exchange(grads))
        pending += zip(keys, partials(keys, grads, recv_a))
        (dq_ca, dk_ca, dv_ca, dbias2, dgq2, dgk2), got = _ca_bwd(sv["proj"], sv["bias2"], gq2, gk2, dmixed,
                                                                 f"l{l}_ca_bwd", _chip_exchange([p for _, p in pending]))
        recv_b.update(zip([k for k, _ in pending], got))
        dgq = dgq2[:, :HEAD_DIM] + dgq2[:, HEAD_DIM:]
        dgk = dgk2[:, :HEAD_DIM] + dgk2[:, HEAD_DIM:]
        drb = _ca_bias_bwd(dbias2, f"l{l}_ca_bias_bwd")
        dproj = jnp.concatenate([dq_sb, dk_sb, dv_sb, dq_ca, dk_ca, dv_ca], axis=1)
        gwin = _tn_matmul(sv["h1"], dproj, True, False, f"l{l}_gwin")
        (dx, dsh1, dsc1, dgn1), recv_a = _nt_ln_bwd(dproj, W_in[l], sv["x0"], gn1, sc1, sh1, dx, f"l{l}_proj_bwd",
                                                    _sibling_exchange([gwin]))
        pending = list(zip([(l, 0)], partials([(l, 0)], [gwin], recv_a)))
        dmod = jnp.concatenate([dsh1, dsc1, dgt1, dsh2, dsc2, dgt2], axis=1)
        small_parts[l] = [dgn1, dgq, dgk, drb, dgn2, dmod]
    grad_x = dx[None]

    big = [(w_in, m_w_in, v_w_in), (w_o, m_w_o, v_w_o), (w1, m_w1, v_w1), (w2, m_w2, v_w2)]
    big_out = [None] * 4
    for t in (3, 2, 1, 0):
        w, m, v = big[t]
        big_out[t], got = _rs_sum_adamw([owns[(l, t)] for l in range(L)], [recv_b[(l, t)] for l in range(L)],
                                        w, m, v, f"adamw_big_{t}",
                                        _chip_exchange([p for _, p in pending]) if t == 3 else None)
        if got:
            recv_b.update(zip([k for k, _ in pending], got))

    packed = _pack_small(small_parts)
    gathered_small = _all_gather_small(packed, "ag_small_grads")
    small_sum = _sum_devices(gathered_small, "sum_small_grads")
    shapes = [[(1, D), (1, HEAD_DIM), (1, HEAD_DIM), (H2, N_REL), (1, D), (1, 6 * D)]] * L
    names = ["g_norm1", "g_q", "g_k", "rel_bias", "g_norm2", "b_ada"]
    small_w = {"g_norm1": (g_norm1, m_g_norm1, v_g_norm1), "g_q": (g_q, m_g_q, v_g_q), "g_k": (g_k, m_g_k, v_g_k),
               "rel_bias": (rel_bias, m_rel_bias, v_rel_bias), "g_norm2": (g_norm2, m_g_norm2, v_g_norm2),
               "b_ada": (b_ada, m_b_ada, v_b_ada)}
    packs = [_pack_small([[small_w[n][k][l] for n in names] for l in range(L)]) for k in range(3)]
    n_pad = packed.shape[1]
    as_rows = lambda a: a.reshape(n_pad // 128, 128)
    sd, sm, sv_ = _adamw(as_rows(packs[0]), as_rows(small_sum), as_rows(packs[1]), as_rows(packs[2]), "adamw_small")
    small_out = {}
    for key, flat in [("grad", small_sum), ("delta", sd), ("m", sm), ("v", sv_)]:
        per_layer = _unpack_small(flat.reshape(-1), shapes)
        for i, n in enumerate(names):
            small_out[(key, n)] = jnp.stack([per_layer[l][i].reshape(small_w[n][0].shape[1:]) for l in range(L)])

    layer_len = 2 * D + 2 * HEAD_DIM + H2 * N_REL + 6 * D
    rows = gathered_small.reshape(NDEV, n_pad)
    dmod_all = jnp.stack([rows[:, l * layer_len + layer_len - 6 * D:(l + 1) * layer_len] for l in range(L)])
    dmod_cols = lax.dynamic_slice(dmod_all, (0, 0, me * Ca), (L, NDEV, Ca))
    dmod_cols = jnp.pad(dmod_cols, ((0, 0), (0, 128 - NDEV), (0, 0)))
    c_t = jnp.pad(c_all.T, ((0, 0), (0, 128 - NDEV)))
    g_ada = _w_ada_grad(c_t, dmod_cols, "w_ada_grad")
    flat2 = lambda a: a.reshape(L * D, Ca)
    ad, am, av = _adamw(flat2(w_ada), flat2(g_ada), flat2(m_w_ada), flat2(v_w_ada), "adamw_w_ada")
    ada_out = [g_ada] + [a.reshape(L, D, Ca) for a in (ad, am, av)]

    def leaf(kind):
        k = {"grad": 0, "delta": 1, "m": 2, "v": 3}[kind]
        return [small_out[(kind, "g_norm1")], big_out[0][k], small_out[(kind, "g_q")], small_out[(kind, "g_k")],
                small_out[(kind, "rel_bias")], big_out[1][k], small_out[(kind, "g_norm2")], big_out[2][k],
                big_out[3][k], ada_out[k], small_out[(kind, "b_ada")]]

    loss = lax.psum(loss_part[0, 0], ("x", "y", "c"))
    return (loss, grad_x, *leaf("grad"), *leaf("delta"), *leaf("m"), *leaf("v"))
```

```python
import functools

import jax
import jax.numpy as jnp
from jax import lax
from jax.experimental import pallas as pl
from jax.experimental.pallas import tpu as pltpu

F32 = jnp.float32
_MXU = jnp.bfloat16

HEAD_DIM = 64
CHUNK = 64
LEFT_CHUNKS = 8
PAD = LEFT_CHUNKS * CHUNK
BAND = PAD + CHUNK
REL_CLIP = 128
N_REL = 2 * REL_CLIP + 1
EPS = 1e-6
NEG = -1e30
NDEV = 8
SB_T = 128
CA_T = 2 * CHUNK
CA_W = CA_T + PAD
SB_SKIP = -104.0
PAIR = 2 * HEAD_DIM
SB_PAIRS = 4
CA_PAIRS_FWD = 4
CA_PAIRS_BWD = 2
ROW_BLOCK = 512
SKEW_W = 767

ADAM_LR, ADAM_B1, ADAM_B2, ADAM_EPS, ADAM_WD, ADAM_STEP = 0.001, 0.9, 0.999, 1e-08, 0.01, 10

MESH = pl.DeviceIdType.MESH
VMEM_SPEC = pl.BlockSpec(memory_space=pltpu.VMEM)
SMEM_SPEC = pl.BlockSpec(memory_space=pltpu.SMEM)
ANY_SPEC = pl.BlockSpec(memory_space=pl.ANY)


def _nn(a, b):
    return lax.dot_general(a, b, (((1,), (0,)), ((), ())), preferred_element_type=F32)


def _nt(a, b):
    return lax.dot_general(a, b, (((1,), (1,)), ((), ())), preferred_element_type=F32)


def _tn(a, b):
    return lax.dot_general(a, b, (((0,), (0,)), ((), ())), preferred_element_type=F32)


def _blk(n, pref):
    return pref if n % pref == 0 else n


def _pos():
    return lax.axis_index("x"), lax.axis_index("y"), lax.axis_index("c")


def _flip(v, bit):
    return 1 - v if bit else v


def _all_gather_small(blk, name):
    R, C = blk.shape

    def body(x_ref, out_ref, send_sems, recv_sems):
        x, y, c = _pos()
        me = 4 * x + 2 * y + c

        def peer(k):
            return (_flip(x, k & 4), _flip(y, k & 2), _flip(c, k & 1))

        def copy(k, slot):
            return pltpu.make_async_remote_copy(
                src_ref=x_ref, dst_ref=out_ref.at[slot], send_sem=send_sems.at[k - 1],
                recv_sem=recv_sems.at[k - 1], device_id=peer(k), device_id_type=MESH)

        out_ref[pl.ds(me, 1), :, :] = x_ref[...].reshape(1, R, C)
        sends = [copy(k, me) for k in range(1, NDEV)]
        for cp in sends:
            cp.start()
        for k in range(1, NDEV):
            px, py, pc = peer(k)
            copy(k, 4 * px + 2 * py + pc).wait_recv()
        for cp in sends:
            cp.wait_send()

    return pl.pallas_call(
        body, name=name,
        out_shape=jax.ShapeDtypeStruct((NDEV, R, C), blk.dtype),
        in_specs=[VMEM_SPEC], out_specs=VMEM_SPEC,
        scratch_shapes=[pltpu.SemaphoreType.DMA((NDEV - 1,)), pltpu.SemaphoreType.DMA((NDEV - 1,))],
    )(blk)


class _Exchange:
    def __init__(self, inputs, out_shapes, sems, start, finish, middle=None):
        self.inputs, self.out_shapes, self.sems = list(inputs), list(out_shapes), list(sems)
        self.start, self.middle, self.finish = start, middle, finish


def _run_exchange(ex, name):
    n_in, n_out = len(ex.inputs), len(ex.out_shapes)

    def body(*refs):
        ins, outs, sems = refs[:n_in], refs[n_in:n_in + n_out], refs[n_in + n_out:]
        ex.start(ins, outs, sems)
        if ex.middle is not None:
            ex.middle(ins, outs, sems)
        ex.finish(ins, outs, sems)

    return pl.pallas_call(
        body, name=name, out_shape=ex.out_shapes, in_specs=[ANY_SPEC] * n_in, out_specs=[ANY_SPEC] * n_out,
        scratch_shapes=ex.sems,
    )(*ex.inputs)


def _hosted(body, n_in, n_out, ex, step, steps):
    if ex is None:
        return body
    xi, xo = len(ex.inputs), len(ex.out_shapes)

    def wrapped(*refs):
        own_in, ex_in = refs[:n_in], refs[n_in:n_in + xi]
        rest = refs[n_in + xi:]
        own_out, ex_out = rest[:n_out], rest[n_out:n_out + xo]
        rest = rest[n_out + xo:]
        own_scratch, ex_sems = rest[:len(rest) - len(ex.sems)], rest[len(rest) - len(ex.sems):]
        t = step()
        pl.when(t == 0)(lambda: ex.start(ex_in, ex_out, ex_sems))
        body(*own_in, *own_out, *own_scratch)
        if ex.middle is not None:
            pl.when(t == (steps * 3) // 5)(lambda: ex.middle(ex_in, ex_out, ex_sems))
        pl.when(t == steps - 1)(lambda: ex.finish(ex_in, ex_out, ex_sems))

    return wrapped


def _call_hosted(body, name, grid, in_specs, out_specs, out_shape, scratch, args, ex):
    n_in, n_out = len(in_specs), len(out_specs)
    steps = 1
    for extent in grid:
        steps *= extent

    def step():
        t = pl.program_id(0)
        for axis in range(1, len(grid)):
            t = t * grid[axis] + pl.program_id(axis)
        return t

    if ex is not None:
        in_specs = in_specs + [ANY_SPEC] * len(ex.inputs)
        out_specs = out_specs + [ANY_SPEC] * len(ex.out_shapes)
        out_shape = out_shape + ex.out_shapes
        scratch = scratch + ex.sems
        args = args + ex.inputs
    outs = pl.pallas_call(
        _hosted(body, n_in, n_out, ex, step, steps), name=name, grid=grid, in_specs=in_specs, out_specs=out_specs,
        out_shape=out_shape, scratch_shapes=scratch,
    )(*args)
    return list(outs[:n_out]), list(outs[n_out:])


def _gather_exchange(shards):
    n = len(shards)

    def setup(ins, outs, sems):
        send_sems, recv_sems, local_sems = sems
        x, y, c = _pos()
        me, sibling = (x, y, c), (x, y, 1 - c)
        chips = [(1 - x, y), (x, 1 - y), (1 - x, 1 - y)]

        def copy(i, k, block, to, src=None):
            px, py, pc = block
            dst = outs[i].at[4 * px + 2 * py + pc]
            return pltpu.make_async_remote_copy(
                src_ref=dst if src is None else src, dst_ref=dst, send_sem=send_sems.at[7 * i + k],
                recv_sem=recv_sems.at[7 * i + k], device_id=to, device_id_type=MESH)

        def mine(i):
            return pltpu.make_async_copy(ins[i], outs[i].at[4 * x + 2 * y + c], local_sems.at[i])

        def first(i):
            return [copy(i, 0, me, sibling, src=ins[i])] + [
                copy(i, 1 + j, me, (*chip, c), src=ins[i]) for j, chip in enumerate(chips)]

        def passed(i, j):
            return copy(i, 4 + j, (*chips[j], c), sibling)

        return me, sibling, chips, c, copy, mine, first, passed

    def start(ins, outs, sems):
        _, _, _, _, _, mine, first, _ = setup(ins, outs, sems)
        for i in range(n):
            mine(i).start()
            for cp in first(i):
                cp.start()

    def middle(ins, outs, sems):
        me, _, chips, c, copy, _, _, passed = setup(ins, outs, sems)
        for j, chip in enumerate(chips):
            for i in range(n):
                copy(i, 1 + j, (*chip, c), me).wait_recv()
                passed(i, j).start()

    def finish(ins, outs, sems):
        me, sibling, chips, c, copy, mine, first, passed = setup(ins, outs, sems)
        for i in range(n):
            copy(i, 0, sibling, me).wait_recv()
            for j, chip in enumerate(chips):
                copy(i, 4 + j, (*chip, 1 - c), me).wait_recv()
        for i in range(n):
            for cp in first(i) + [passed(i, j) for j in range(3)]:
                cp.wait_send()
            mine(i).wait()

    return _Exchange(
        shards, [jax.ShapeDtypeStruct((NDEV,) + s.shape, s.dtype) for s in shards],
        [pltpu.SemaphoreType.DMA((7 * n,)), pltpu.SemaphoreType.DMA((7 * n,)), pltpu.SemaphoreType.DMA((n,))],
        start, finish, middle)


def _sibling_exchange(grads):
    n = len(grads)

    def copies(ins, outs, sems):
        send_sems, recv_sems = sems
        x, y, c = _pos()
        return [pltpu.make_async_remote_copy(
            src_ref=ins[i].at[2 * q + (1 - c)], dst_ref=outs[i].at[q], send_sem=send_sems.at[4 * i + q],
            recv_sem=recv_sems.at[4 * i + q], device_id=(x, y, 1 - c), device_id_type=MESH)
            for i in range(n) for q in range(4)]

    def start(ins, outs, sems):
        for cp in copies(ins, outs, sems):
            cp.start()

    def finish(ins, outs, sems):
        for cp in copies(ins, outs, sems):
            cp.wait()

    return _Exchange(
        grads, [jax.ShapeDtypeStruct((4,) + g.shape[1:], g.dtype) for g in grads],
        [pltpu.SemaphoreType.DMA((4 * n,)), pltpu.SemaphoreType.DMA((4 * n,))], start, finish)


def _chip_exchange(parts):
    n = len(parts)

    def copies(ins, outs, sems):
        send_sems, recv_sems = sems
        x, y, c = _pos()
        return [pltpu.make_async_remote_copy(
            src_ref=ins[i].at[j - 1], dst_ref=outs[i].at[j - 1], send_sem=send_sems.at[3 * i + j - 1],
            recv_sem=recv_sems.at[3 * i + j - 1], device_id=(_flip(x, j & 2), _flip(y, j & 1), c),
            device_id_type=MESH) for i in range(n) for j in range(1, 4)]

    def start(ins, outs, sems):
        for cp in copies(ins, outs, sems):
            cp.start()

    def finish(ins, outs, sems):
        for cp in copies(ins, outs, sems):
            cp.wait()

    return _Exchange(
        parts, [jax.ShapeDtypeStruct(p.shape, p.dtype) for p in parts],
        [pltpu.SemaphoreType.DMA((3 * n,)), pltpu.SemaphoreType.DMA((3 * n,))], start, finish)


def _rs_chip_partial(place, grad, recv, name):
    _, R, C = grad.shape
    tr = _blk(R, 256)

    def body(place_ref, *refs):
        g_refs, r_refs = refs[:4], refs[4:8]
        own_ref, out_ref = refs[8:]
        own_ref[...] = g_refs[0][0] + r_refs[0][0]
        for j in range(1, 4):
            out_ref[j - 1] = (g_refs[j][0] + r_refs[j][0]).astype(out_ref.dtype)

    def g_map(j):
        return lambda i, p: (2 * jnp.bitwise_xor(p[0], j) + p[1], i, 0)

    def r_map(j):
        return lambda i, p: (jnp.bitwise_xor(p[0], j), i, 0)

    grid_spec = pltpu.PrefetchScalarGridSpec(
        num_scalar_prefetch=1, grid=(R // tr,),
        in_specs=[pl.BlockSpec((1, tr, C), g_map(j)) for j in range(4)]
        + [pl.BlockSpec((1, tr, C), r_map(j)) for j in range(4)],
        out_specs=[pl.BlockSpec((tr, C), lambda i, p: (i, 0)), pl.BlockSpec((3, tr, C), lambda i, p: (0, i, 0))])
    return pl.pallas_call(
        body, name=name, grid_spec=grid_spec,
        out_shape=[jax.ShapeDtypeStruct((R, C), F32), jax.ShapeDtypeStruct((3, R, C), _MXU)],
    )(place, *([grad] * 4), *([recv] * 4))


def _adamw_math(w, g, m, v):
    m = ADAM_B1 * m + (1.0 - ADAM_B1) * g
    v = ADAM_B2 * v + (1.0 - ADAM_B2) * (g * g)
    m_hat = m / (1.0 - ADAM_B1 ** ADAM_STEP)
    v_hat = v / (1.0 - ADAM_B2 ** ADAM_STEP)
    delta = -ADAM_LR * (m_hat / (jnp.sqrt(v_hat) + ADAM_EPS) + ADAM_WD * w)
    return delta, m, v


def _adamw(w, g, m, v, name):
    R, C = w.shape
    tr = _blk(R, 256)

    def body(w_ref, g_ref, m_ref, v_ref, d_ref, nm_ref, nv_ref):
        d, nm, nv = _adamw_math(w_ref[...], g_ref[...], m_ref[...], v_ref[...])
        d_ref[...] = d
        nm_ref[...] = nm
        nv_ref[...] = nv

    spec = pl.BlockSpec((tr, C), lambda i: (i, 0))
    return pl.pallas_call(
        body, name=name, grid=(R // tr,), in_specs=[spec] * 4, out_specs=[spec] * 3,
        out_shape=[jax.ShapeDtypeStruct((R, C), F32)] * 3,
    )(w, g, m, v)


def _rs_sum_adamw(owns, recvs, w, m, v, name):
    L, R, C = w.shape
    tr = _blk(R, 256)
    nr = R // tr

    def body(o0, o1, r0, r1, w_ref, m_ref, v_ref, g_ref, d_ref, nm_ref, nv_ref):
        def step(o_ref, r_ref):
            g = o_ref[...]
            for j in range(3):
                g = g + r_ref[j].astype(F32)
            d, nm, nv = _adamw_math(w_ref[0], g, m_ref[0], v_ref[0])
            g_ref[0] = g
            d_ref[0] = d
            nm_ref[0] = nm
            nv_ref[0] = nv

        pl.when(pl.program_id(0) == 0)(lambda: step(o0, r0))
        pl.when(pl.program_id(0) == 1)(lambda: step(o1, r1))

    def hold(layer):
        if layer == 0:
            return lambda l, i: i * (1 - l) + (nr - 1) * l
        return lambda l, i: i * l

    own_spec = [pl.BlockSpec((tr, C), functools.partial(lambda l, i, f: (f(l, i), 0), f=hold(k))) for k in range(2)]
    recv_spec = [pl.BlockSpec((3, tr, C), functools.partial(lambda l, i, f: (0, f(l, i), 0), f=hold(k)))
                 for k in range(2)]
    lay = pl.BlockSpec((1, tr, C), lambda l, i: (l, i, 0))
    return pl.pallas_call(
        body, name=name, grid=(L, nr),
        in_specs=own_spec + recv_spec + [lay] * 3, out_specs=[lay] * 4,
        out_shape=[jax.ShapeDtypeStruct((L, R, C), F32)] * 4,
    )(owns[0], owns[1], recvs[0], recvs[1], w, m, v)


def _silu(x):
    return x / (1.0 + jnp.exp(-x))


def _mod_partial(c_all, w_ada, b_cols, name):
    L, D, Ca = w_ada.shape

    def body(c_ref, w_ref, b_ref, o_ref):
        act = _silu(c_ref[...]).astype(_MXU)
        for l in range(L):
            o_ref[:, l * Ca:(l + 1) * Ca] = _nn(act, w_ref[l].astype(_MXU)) + b_ref[l:l + 1, :]

    return pl.pallas_call(
        body, name=name, out_shape=jax.ShapeDtypeStruct((NDEV, L * Ca), F32),
        in_specs=[VMEM_SPEC] * 3, out_specs=VMEM_SPEC,
    )(c_all, w_ada, b_cols)


def _w_ada_grad(c_t, dmod_cols, name):
    L, _, Ca = dmod_cols.shape
    D = c_t.shape[0]

    def body(c_ref, d_ref, o_ref):
        act = _silu(c_ref[...]).astype(_MXU)
        for l in range(L):
            o_ref[l] = _nn(act, d_ref[l].astype(_MXU))

    return pl.pallas_call(
        body, name=name, out_shape=jax.ShapeDtypeStruct((L, D, Ca), F32),
        in_specs=[VMEM_SPEC] * 2, out_specs=VMEM_SPEC,
    )(c_t, dmod_cols)


def _sum_devices(gathered, name):
    _, _, N = gathered.shape

    def body(x_ref, o_ref):
        acc = x_ref[0]
        for d in range(1, NDEV):
            acc = acc + x_ref[d]
        o_ref[...] = acc

    return pl.pallas_call(
        body, name=name, out_shape=jax.ShapeDtypeStruct((1, N), F32),
        in_specs=[VMEM_SPEC], out_specs=VMEM_SPEC,
    )(gathered)


def _ln_mod_matmul(x, g, sc, sh, w, name):
    S, D = x.shape
    N = w.shape[1]
    tm = _blk(S, ROW_BLOCK)

    def body(x_ref, g_ref, sc_ref, sh_ref, w_ref, o_ref, h_ref):
        xv = x_ref[...]
        r = lax.rsqrt(jnp.mean(xv * xv, axis=-1, keepdims=True) + EPS)
        hv = ((xv * r) * g_ref[...]) * (1.0 + sc_ref[...]) + sh_ref[...]
        hb = hv.astype(_MXU)
        h_ref[...] = hb
        o_ref[...] = _nn(hb, w_ref[...]).astype(o_ref.dtype)

    vec = pl.BlockSpec((1, D), lambda i: (0, 0))
    row = lambda width: pl.BlockSpec((tm, width), lambda i: (i, 0))
    return pl.pallas_call(
        body, name=name, grid=(S // tm,),
        in_specs=[row(D), vec, vec, vec, pl.BlockSpec((D, N), lambda i: (0, 0))],
        out_specs=[row(N), row(D)],
        out_shape=[jax.ShapeDtypeStruct((S, N), _MXU), jax.ShapeDtypeStruct((S, D), _MXU)],
    )(x, g, sc, sh, w)


def _matmul_res_gate(a, w, xres, gt, relu2, name):
    S, K = a.shape
    N = w.shape[1]
    tm = _blk(S, 512)

    def body(a_ref, w_ref, x_ref, gt_ref, o_ref, f_ref):
        av = a_ref[...]
        if relu2:
            af = jnp.maximum(av.astype(F32), 0.0)
            av = (af * af).astype(_MXU)
        f = _nn(av, w_ref[...])
        f_ref[...] = f.astype(f_ref.dtype)
        o_ref[...] = x_ref[...] + gt_ref[...] * f

    row = lambda width: pl.BlockSpec((tm, width), lambda i: (i, 0))
    return pl.pallas_call(
        body, name=name, grid=(S // tm,),
        in_specs=[row(K), pl.BlockSpec((K, N), lambda i: (0, 0)), row(N), pl.BlockSpec((1, N), lambda i: (0, 0))],
        out_specs=[row(N), row(N)],
        out_shape=[jax.ShapeDtypeStruct((S, N), F32), jax.ShapeDtypeStruct((S, N), _MXU)],
    )(a, w, xres, gt)


def _loss_grad(y, t, name):
    S, D = y.shape
    tm = _blk(S, 512)
    last = S // tm - 1

    def body(y_ref, t_ref, dy_ref, l_ref, acc_ref):
        i = pl.program_id(0)
        e = y_ref[...] - t_ref[...]
        dy_ref[...] = e * (1.0 / D)
        part = jnp.sum(e * e, axis=0, keepdims=True)

        @pl.when(i == 0)
        def _():
            acc_ref[...] = part

        @pl.when(i > 0)
        def _():
            acc_ref[...] += part

        @pl.when(i == last)
        def _():
            l_ref[...] = (0.5 / D) * jnp.sum(acc_ref[...], axis=1, keepdims=True)

    row = pl.BlockSpec((tm, D), lambda i: (i, 0))
    return pl.pallas_call(
        body, name=name, grid=(S // tm,), in_specs=[row, row],
        out_specs=[row, pl.BlockSpec((1, 1), lambda i: (0, 0))],
        out_shape=[jax.ShapeDtypeStruct((S, D), F32), jax.ShapeDtypeStruct((1, 1), F32)],
        scratch_shapes=[pltpu.VMEM((1, D), F32)],
    )(y, t)


def _accumulate(ref, part, first):
    @pl.when(first)
    def _():
        ref[...] = part

    @pl.when(jnp.logical_not(first))
    def _():
        ref[...] += part


def _gate_nt_matmul(dx, f, gt, w, u, name):
    S, D = dx.shape
    N = w.shape[0]
    tm = _blk(S, ROW_BLOCK)
    with_u = u is not None

    def body(*refs):
        if with_u:
            dx_ref, f_ref, gt_ref, w_ref, u_ref, dz_ref, dgt_ref, res_ref = refs
        else:
            dx_ref, f_ref, gt_ref, w_ref, dz_ref, dgt_ref, res_ref = refs
        dxv = dx_ref[...]
        dz = (dxv * gt_ref[...]).astype(_MXU)
        dz_ref[...] = dz
        _accumulate(dgt_ref, jnp.sum(dxv * f_ref[...].astype(F32), axis=0, keepdims=True), pl.program_id(0) == 0)
        r = _nt(dz, w_ref[...])
        if with_u:
            r = r * (2.0 * jnp.maximum(u_ref[...].astype(F32), 0.0))
        res_ref[...] = r.astype(res_ref.dtype)

    row = lambda width: pl.BlockSpec((tm, width), lambda i: (i, 0))
    in_specs = [row(D), row(D), pl.BlockSpec((1, D), lambda i: (0, 0)), pl.BlockSpec((N, D), lambda i: (0, 0))]
    args = [dx, f, gt, w]
    if with_u:
        in_specs.append(row(N))
        args.append(u)
    return pl.pallas_call(
        body, name=name, grid=(S // tm,), in_specs=in_specs,
        out_specs=[row(D), pl.BlockSpec((1, D), lambda i: (0, 0)), row(N)],
        out_shape=[jax.ShapeDtypeStruct((S, D), _MXU), jax.ShapeDtypeStruct((1, D), F32),
                   jax.ShapeDtypeStruct((S, N), _MXU)],
    )(*args)


def _tn_matmul(a, b, by_col, relu2, name):
    S, Ka = a.shape
    Nb = b.shape[1]
    ts = _blk(S, 2 * ROW_BLOCK)
    half = NDEV // 2
    if by_col:
        R, C = Ka, Nb // NDEV
        a_spec = pl.BlockSpec((ts, Ka), lambda h, k: (k, 0))
        b_spec = pl.BlockSpec((ts, half * C), lambda h, k: (k, h))
    else:
        R, C = Ka // NDEV, Nb
        a_spec = pl.BlockSpec((ts, half * R), lambda h, k: (k, h))
        b_spec = pl.BlockSpec((ts, Nb), lambda h, k: (k, 0))

    def body(a_ref, b_ref, o_ref):
        av = a_ref[...]
        if relu2:
            af = jnp.maximum(av.astype(F32), 0.0)
            av = (af * af).astype(_MXU)
        p = _tn(av, b_ref[...])
        first = pl.program_id(1) == 0
        for d in range(half):
            part = p[:, d * C:(d + 1) * C] if by_col else p[d * R:(d + 1) * R, :]
            _accumulate(o_ref.at[d], part, first)

    return pl.pallas_call(
        body, name=name, grid=(NDEV // half, S // ts), in_specs=[a_spec, b_spec],
        out_specs=pl.BlockSpec((half, R, C), lambda h, k: (h, 0, 0)),
        out_shape=jax.ShapeDtypeStruct((NDEV, R, C), F32),
    )(a, b)


def _nt_ln_bwd(dy, w, x, g, sc, sh, dxres, name):
    S, D = x.shape
    N = w.shape[1]
    tm = _blk(S, ROW_BLOCK)

    def body(dy_ref, w_ref, x_ref, g_ref, sc_ref, sh_ref, dxr_ref, dx_ref, dsh_ref, dsc_ref, dg_ref):
        dh = _nt(dy_ref[...], w_ref[...])
        xv = x_ref[...]
        r = lax.rsqrt(jnp.mean(xv * xv, axis=-1, keepdims=True) + EPS)
        xhat = xv * r
        gv = g_ref[...]
        dn = dh * (1.0 + sc_ref[...])
        dxhat = dn * gv
        dxv = r * (dxhat - xhat * jnp.mean(dxhat * xhat, axis=-1, keepdims=True))
        dx_ref[...] = dxr_ref[...] + dxv
        first = pl.program_id(0) == 0
        _accumulate(dsh_ref, jnp.sum(dh, axis=0, keepdims=True), first)
        _accumulate(dsc_ref, jnp.sum(dh * (xhat * gv), axis=0, keepdims=True), first)
        _accumulate(dg_ref, jnp.sum(dn * xhat, axis=0, keepdims=True), first)

    row = lambda width: pl.BlockSpec((tm, width), lambda i: (i, 0))
    vec = pl.BlockSpec((1, D), lambda i: (0, 0))
    return pl.pallas_call(
        body, name=name, grid=(S // tm,),
        in_specs=[row(N), pl.BlockSpec((D, N), lambda i: (0, 0)), row(D), vec, vec, vec, row(D)],
        out_specs=[row(D), vec, vec, vec],
        out_shape=[jax.ShapeDtypeStruct((S, D), F32)] + [jax.ShapeDtypeStruct((1, D), F32)] * 3,
    )(dy, w, x, g, sc, sh, dxres)


def _split2(v):
    hi = v.astype(_MXU)
    mid = (v - hi.astype(F32)).astype(_MXU)
    return hi, mid


def _tri_sums(vs, tri2):
    T = vs[0].shape[0]
    out = []
    for j in range(len(vs) // 2):
        wide = [jnp.concatenate(_split2(vs[2 * j + e]), axis=1) for e in range(2)]
        for both in _per_head(_nn(jnp.concatenate(wide, axis=0), tri2), T):
            out.append((both[:, :T], both[:, T:]))
    return out


def _tri2(T, inclusive):
    j = lax.broadcasted_iota(jnp.int32, (2 * T, 2 * T), 0) % T
    s = lax.broadcasted_iota(jnp.int32, (2 * T, 2 * T), 1)
    keep = (j >= s) if inclusive else (j > s)
    return jnp.where((s >= T) | keep, 1.0, 0.0).astype(_MXU)


def _log_sigmoid(z):
    return jnp.minimum(z, 0.0) - jnp.log(1.0 + jnp.exp(-jnp.abs(z)))


def _per_head(tall, T):
    return [tall[h * T:(h + 1) * T] for h in range(tall.shape[0] // T)]


def _sb_blocks(q_tall, k2, strict, tri2, carry):
    T = k2[0].shape[0]
    zs = []
    for qt, kblk in zip(q_tall, k2):
        zs += _per_head(_nt(qt, kblk), T)
    lbs, l1s = [], []
    for z in zs:
        lb = _log_sigmoid(z)
        l1 = lb - z
        if strict is not None:
            l1 = jnp.where(strict, l1, 0.0)
        lbs.append(lb)
        l1s.append(l1)
    sums = _tri_sums(l1s, tri2)
    amps, new_carry = [], []
    for lb, (sfx, tot), c in zip(lbs, sums, carry):
        a = jnp.exp(lb + sfx + c)
        if strict is not None:
            a = jnp.where(strict, a, 0.0)
        amps.append(a)
        new_carry.append(c + tot)
    return lbs, amps, new_carry


def _sb_alive(carry):
    top = carry[0]
    for c in carry[1:]:
        top = jnp.maximum(top, c)
    return jnp.max(top) > SB_SKIP


def _skew_index():
    i = lax.broadcasted_iota(jnp.int32, (CA_T, SKEW_W + 1), 0)
    m = lax.broadcasted_iota(jnp.int32, (CA_T, SKEW_W + 1), 1)
    wrapped = i + m >= SKEW_W
    row = jnp.where(wrapped, i + 1, i)
    j = jnp.where(wrapped, i + m - SKEW_W, i + m)
    a = row // CHUNK
    jj = j - a * CHUNK
    inband = (jj >= 0) & (jj < BAND) & (j < CA_W) & (row < CA_T)
    idx = jnp.clip((row - a * CHUNK) + PAD - jj, -REL_CLIP, REL_CLIP) + REL_CLIP
    return inband, idx, wrapped


def _skew(tile):
    H = tile.shape[0]
    flat = jnp.pad(tile, ((0, 0), (0, 0), (0, SKEW_W - CA_W))).reshape(H, CA_T * SKEW_W)
    return jnp.pad(flat, ((0, 0), (0, CA_T))).reshape(H, CA_T, SKEW_W + 1)


def _unskew(view):
    H = view.shape[0]
    flat = view.reshape(H, CA_T * (SKEW_W + 1))[:, :CA_T * SKEW_W]
    return flat.reshape(H, CA_T, SKEW_W)[:, :, :CA_W]


def _ca_bias(rel_bias, name):
    H = rel_bias.shape[0]
    top = rel_bias[:, N_REL - 1:]
    by_offset = jnp.concatenate(
        [jnp.broadcast_to(top, (H, PAD - REL_CLIP + 1)), jnp.flip(rel_bias[:, :N_REL - 1], axis=1),
         jnp.broadcast_to(top, (H, SKEW_W + 1 - (PAD - REL_CLIP + 1) - (N_REL - 1)))], axis=1)

    def body(t_ref, o_ref):
        inband, _, wrapped = _skew_index()
        vals = jnp.where(wrapped, t_ref[0][:, 0:1], t_ref[0])
        o_ref[0] = jnp.where(inband, vals, NEG)

    view = pl.pallas_call(
        body, name=name, grid=(H,), in_specs=[pl.BlockSpec((1, 1, SKEW_W + 1), lambda h: (h, 0, 0))],
        out_specs=pl.BlockSpec((1, CA_T, SKEW_W + 1), lambda h: (h, 0, 0)),
        out_shape=jax.ShapeDtypeStruct((H, CA_T, SKEW_W + 1), F32),
    )(by_offset.reshape(H, 1, SKEW_W + 1))
    return _unskew(view)


def _ca_bias_bwd(dbias, name):
    H = dbias.shape[0]

    def body(d_ref, o_ref):
        inband, idx, _ = _skew_index()
        d = jnp.where(inband, d_ref[0], 0.0)
        clipped = idx == N_REL - 1
        by_offset = jnp.sum(jnp.where(clipped, 0.0, d), axis=0, keepdims=True)
        top = jnp.sum(jnp.sum(jnp.where(clipped, d, 0.0), axis=0, keepdims=True), axis=1, keepdims=True)
        lane = lax.broadcasted_iota(jnp.int32, (1, SKEW_W + 1), 1)
        o_ref[0] = jnp.where(lane == 0, top, by_offset)

    out = pl.pallas_call(
        body, name=name, grid=(H,), in_specs=[pl.BlockSpec((1, CA_T, SKEW_W + 1), lambda h: (h, 0, 0))],
        out_specs=pl.BlockSpec((1, 1, SKEW_W + 1), lambda h: (h, 0, 0)),
        out_shape=jax.ShapeDtypeStruct((H, 1, SKEW_W + 1), F32),
    )(_skew(dbias))[:, 0]
    first = PAD - REL_CLIP + 1
    return jnp.concatenate([jnp.flip(out[:, first:first + N_REL - 1], axis=1), out[:, 0:1]], axis=1)


def _low_lanes(rows):
    return lax.broadcasted_iota(jnp.int32, (rows, PAIR), 1) < HEAD_DIM


def _one_head(t2, low, first, scale=1.0):
    tf = t2.astype(F32) * scale
    return (jnp.where(low, tf, 0.0) if first else jnp.where(low, 0.0, tf)).astype(_MXU)


def _two_heads(t2, low, scale=1.0):
    return jnp.concatenate([_one_head(t2, low, True, scale), _one_head(t2, low, False, scale)], axis=0)


def _sb_fwd(proj, name, ex=None):
    S, W = proj.shape
    half = W // 6
    npair = half // PAIR
    T = _blk(S, SB_T)
    GP = _blk(npair, SB_PAIRS)
    GW = GP * PAIR
    nb = npair // GP

    def body(q_ref, k_ref, v_ref, o_ref, ox_ref):
        qi = pl.program_id(1)
        low = _low_lanes(T)
        q_tall = []
        for j in range(GP):
            q2 = q_ref[:, j * PAIR:(j + 1) * PAIR]
            q_tall.append(_two_heads(q2, low, HEAD_DIM ** -0.5))
        row = lax.broadcasted_iota(jnp.int32, (T, T), 0)
        col = lax.broadcasted_iota(jnp.int32, (T, T), 1)
        tri2 = _tri2(T, inclusive=False)

        def pairs(kb, carry, acc, fine, strict):
            rows = pl.ds(pl.multiple_of(kb * T, T), T)
            k2 = [k_ref[rows, j * PAIR:(j + 1) * PAIR] for j in range(GP)]
            v2 = [v_ref[rows, j * PAIR:(j + 1) * PAIR] for j in range(GP)]
            _, amps, carry = _sb_blocks(q_tall, k2, strict, tri2, carry)
            parts = [_split2(a) for a in amps]
            new_acc, new_fine = [], []
            for j in range(GP):
                tall = jnp.concatenate([parts[2 * j][0], parts[2 * j + 1][0], parts[2 * j][1], parts[2 * j + 1][1]],
                                       axis=0)
                hi0, hi1, mid0, mid1 = _per_head(_nn(tall, v2[j]), T)
                new_acc.append(acc[j] + jnp.where(low, hi0, hi1))
                new_fine.append(fine[j] + jnp.where(low, mid0, mid1))
            return tuple(carry), tuple(new_acc), tuple(new_fine)

        zero = (jnp.zeros((T, PAIR), F32),) * GP
        carry, acc, fine = pairs(qi, (jnp.zeros((T, T), F32),) * (2 * GP), zero, zero, col < row)

        def cond(st):
            kb, alive, _, _, _ = st
            return jnp.logical_and(kb >= 0, alive)

        def step(st):
            kb, _, carry, acc, fine = st
            carry, acc, fine = pairs(kb, carry, acc, fine, None)
            return kb - 1, _sb_alive(carry), carry, acc, fine

        _, _, _, acc, fine = lax.while_loop(cond, step, (qi - 1, _sb_alive(carry), carry, acc, fine))
        for j in range(GP):
            o_ref[:, j * PAIR:(j + 1) * PAIR] = acc[j].astype(o_ref.dtype)
            ox_ref[:, j * PAIR:(j + 1) * PAIR] = acc[j] + fine[j]

    blk = pl.BlockSpec((T, GW), lambda p, i: (i, p))
    return _call_hosted(
        body, name, (nb, S // T),
        [blk, pl.BlockSpec((S, GW), lambda p, i: (0, nb + p)), pl.BlockSpec((S, GW), lambda p, i: (0, 2 * nb + p))],
        [blk, blk], [jax.ShapeDtypeStruct((S, half), _MXU), jax.ShapeDtypeStruct((S, half), F32)],
        [], [proj, proj, proj], ex)


def _sb_bwd(proj, ox, dmixed, name, ex=None):
    S, W = proj.shape
    half = W // 6
    npair = half // PAIR
    T = _blk(S, SB_T)
    GP = _blk(npair, SB_PAIRS)
    GW = GP * PAIR
    nb = npair // GP
    last = S // T - 1
    scale = HEAD_DIM ** -0.5

    def body(q_ref, k_ref, v_ref, ox_ref, do_ref, dq_ref, dk_ref, dv_ref, dka_ref, dva_ref):
        qi = pl.program_id(1)

        @pl.when(qi == 0)
        def _():
            dka_ref[...] = jnp.zeros_like(dka_ref)
            dva_ref[...] = jnp.zeros_like(dva_ref)

        low = _low_lanes(T)
        q2, do2, q_tall, do_tall, deltas = [], [], [], [], []
        for j in range(GP):
            cols = slice(j * PAIR, (j + 1) * PAIR)
            q2.append(q_ref[:, cols])
            do2.append(do_ref[:, cols])
            q_tall.append(_two_heads(q2[j], low, scale))
            dobs = [_one_head(do2[j], low, True), _one_head(do2[j], low, False)]
            do_tall.append(jnp.concatenate(dobs, axis=0))
            for e in range(2):
                deltas.append(jnp.sum(dobs[e].astype(F32) * ox_ref[:, cols], axis=-1, keepdims=True))
        row = lax.broadcasted_iota(jnp.int32, (T, T), 0)
        col = lax.broadcasted_iota(jnp.int32, (T, T), 1)
        tri_ex = _tri2(T, inclusive=False)
        tri_in = _tri2(T, inclusive=True)

        def pairs(kb, carry, right, dq, strict):
            rows = pl.ds(pl.multiple_of(kb * T, T), T)
            k2 = [k_ref[rows, j * PAIR:(j + 1) * PAIR] for j in range(GP)]
            v2 = [v_ref[rows, j * PAIR:(j + 1) * PAIR] for j in range(GP)]
            nh = 2 * GP
            gs = []
            for j in range(GP):
                gs += _per_head(_nt(do_tall[j], v2[j]), T)
            lbs, amps, carry = _sb_blocks(q_tall, k2, strict, tri_ex, carry)
            ags = [a * gg for a, gg in zip(amps, gs)]
            sums = _tri_sums(ags, tri_in)
            dzbs = []
            for h in range(nh):
                left = deltas[h] - (sums[h][0] + right[h])
                beta = jnp.exp(lbs[h])
                dz = ags[h] - beta * (ags[h] + left)
                if strict is not None:
                    dz = jnp.where(strict, dz, 0.0)
                dzbs.append(dz.astype(_MXU))
            abs_ = [a.astype(_MXU) for a in amps]
            new_dq = []
            for j in range(GP):
                cols = slice(j * PAIR, (j + 1) * PAIR)
                dk0, dk1 = _per_head(_tn(jnp.concatenate(dzbs[2 * j:2 * j + 2], axis=1), q2[j]), T)
                dv0, dv1 = _per_head(_tn(jnp.concatenate(abs_[2 * j:2 * j + 2], axis=1), do2[j]), T)
                dq0, dq1 = _per_head(_nn(jnp.concatenate(dzbs[2 * j:2 * j + 2], axis=0), k2[j]), T)
                dka_ref[rows, cols] += jnp.where(low, dk0, dk1)
                dva_ref[rows, cols] += jnp.where(low, dv0, dv1)
                new_dq.append(dq[j] + jnp.where(low, dq0, dq1))
            right = tuple(right[h] + sums[h][1] for h in range(nh))
            return tuple(carry), right, tuple(new_dq)

        zero = (jnp.zeros((T, T), F32),) * (2 * GP)
        carry, right, dq = pairs(qi, zero, zero, (jnp.zeros((T, PAIR), F32),) * GP, col < row)

        def cond(st):
            kb, alive, _, _, _ = st
            return jnp.logical_and(kb >= 0, alive)

        def step(st):
            kb, _, carry, right, dq = st
            carry, right, dq = pairs(kb, carry, right, dq, None)
            return kb - 1, _sb_alive(carry), carry, right, dq

        _, _, _, _, dq = lax.while_loop(cond, step, (qi - 1, _sb_alive(carry), carry, right, dq))
        for j in range(GP):
            dq_ref[:, j * PAIR:(j + 1) * PAIR] = (dq[j] * scale).astype(dq_ref.dtype)

        @pl.when(qi == last)
        def _():
            dk_ref[...] = (dka_ref[...] * scale).astype(dk_ref.dtype)
            dv_ref[...] = dva_ref[...].astype(dv_ref.dtype)

    blk = pl.BlockSpec((T, GW), lambda p, i: (i, p))
    full = pl.BlockSpec((S, GW), lambda p, i: (0, p))
    return _call_hosted(
        body, name, (nb, S // T),
        [blk, pl.BlockSpec((S, GW), lambda p, i: (0, nb + p)), pl.BlockSpec((S, GW), lambda p, i: (0, 2 * nb + p)),
         blk, blk],
        [blk, full, full], [jax.ShapeDtypeStruct((S, half), _MXU)] * 3,
        [pltpu.VMEM((S, GW), F32), pltpu.VMEM((S, GW), F32)], [proj, proj, proj, ox, dmixed], ex)


def _pair_norm(t2, g2, low):
    tf = t2.astype(F32)
    sq = tf * tf
    both = jnp.sum(sq, axis=-1, keepdims=True)
    first = jnp.sum(jnp.where(low, sq, 0.0), axis=-1, keepdims=True)
    r = jnp.where(low, lax.rsqrt(first * (1.0 / HEAD_DIM) + EPS), lax.rsqrt((both - first) * (1.0 / HEAD_DIM) + EPS))
    hat = tf * r
    return hat * g2, hat, r


def _pair_norm_bwd(dn, hat, r, g2, low):
    dhat = dn * g2
    prod = dhat * hat
    both = jnp.sum(prod, axis=-1, keepdims=True)
    first = jnp.sum(jnp.where(low, prod, 0.0), axis=-1, keepdims=True)
    mean = jnp.where(low, first, both - first) * (1.0 / HEAD_DIM)
    return r * (dhat - hat * mean)


def _ca_fill(j, k_ref, v_ref, gk_ref, kn_ref, vp_ref):
    S = k_ref.shape[0]
    cols = slice(j * PAIR, (j + 1) * PAIR)
    kn, _, _ = _pair_norm(k_ref[:, cols], gk_ref[...], _low_lanes(S))
    kn_ref[j, 0:PAD, :] = jnp.zeros((PAD, PAIR), kn_ref.dtype)
    vp_ref[j, 0:PAD, :] = jnp.zeros((PAD, PAIR), vp_ref.dtype)
    kn_ref[j, PAD:PAD + S, :] = kn.astype(kn_ref.dtype)
    vp_ref[j, PAD:PAD + S, :] = v_ref[:, cols]


def _ca_scores(j, q_ref, b2_ref, gq_ref, kn_ref, qi, low):
    qn, qhat, r = _pair_norm(q_ref[:, j * PAIR:(j + 1) * PAIR], gq_ref[...], low)
    qn = qn * HEAD_DIM ** -0.5
    band = pl.ds(pl.multiple_of(qi * CA_T, CA_T), CA_W)
    key_pos = qi * CA_T - PAD + lax.broadcasted_iota(jnp.int32, (CA_T, CA_W), 1)
    both = _per_head(_nt(_two_heads(qn, low), kn_ref[j, band, :]), CA_T)
    scores = [jnp.where(key_pos >= 0, both[e] + b2_ref[2 * j + e], NEG) for e in range(2)]
    return scores, qn.astype(_MXU), qhat, r


def _softmax(s):
    e = jnp.exp(s - jnp.max(s, axis=-1, keepdims=True))
    return e * (1.0 / jnp.sum(e, axis=-1, keepdims=True))


def _ca_fwd(proj, bias2, gq2, gk2, name, ex=None):
    S, W = proj.shape
    half = W // 6
    npair = half // PAIR
    GP = _blk(npair, CA_PAIRS_FWD)
    GW = GP * PAIR
    nb = npair // GP

    def body(q_ref, k_ref, v_ref, b2_ref, gq_ref, gk_ref, o_ref, kn_ref, vp_ref):
        qi = pl.program_id(1)

        @pl.when(qi == 0)
        def _():
            for j in range(GP):
                _ca_fill(j, k_ref, v_ref, gk_ref, kn_ref, vp_ref)

        low = _low_lanes(CA_T)
        band = pl.ds(pl.multiple_of(qi * CA_T, CA_T), CA_W)
        scores = [_ca_scores(j, q_ref, b2_ref, gq_ref, kn_ref, qi, low)[0] for j in range(GP)]
        probs = [[_softmax(s).astype(_MXU) for s in pair] for pair in scores]
        for j in range(GP):
            outs = _per_head(_nn(jnp.concatenate(probs[j], axis=0), vp_ref[j, band, :]), CA_T)
            o_ref[:, j * PAIR:(j + 1) * PAIR] = jnp.where(low, outs[0], outs[1]).astype(o_ref.dtype)

    vec = pl.BlockSpec((1, PAIR), lambda p, i: (0, 0))
    return _call_hosted(
        body, name, (nb, S // CA_T),
        [pl.BlockSpec((CA_T, GW), lambda p, i: (i, 3 * nb + p)),
         pl.BlockSpec((S, GW), lambda p, i: (0, 4 * nb + p)), pl.BlockSpec((S, GW), lambda p, i: (0, 5 * nb + p)),
         pl.BlockSpec((2 * GP, CA_T, CA_W), lambda p, i: (p, 0, 0)), vec, vec],
        [pl.BlockSpec((CA_T, GW), lambda p, i: (i, p))], [jax.ShapeDtypeStruct((S, half), _MXU)],
        [pltpu.VMEM((GP, PAD + S, PAIR), _MXU), pltpu.VMEM((GP, PAD + S, PAIR), _MXU)],
        [proj, proj, proj, bias2, gq2, gk2], ex)


def _ca_bwd(proj, bias2, gq2, gk2, dmixed, name, ex=None):
    S, W = proj.shape
    half = W // 6
    npair = half // PAIR
    GP = _blk(npair, CA_PAIRS_BWD)
    GW = GP * PAIR
    nb = npair // GP
    scale = HEAD_DIM ** -0.5
    last = S // CA_T - 1

    def body(q_ref, k_ref, v_ref, b2_ref, gq_ref, gk_ref, do_ref,
             dq_ref, dk_ref, dv_ref, db_ref, dgq_ref, dgk_ref, kn_ref, vp_ref, dkn_ref, dvp_ref):
        p_id, qi = pl.program_id(0), pl.program_id(1)

        @pl.when(qi == 0)
        def _():
            for j in range(GP):
                _ca_fill(j, k_ref, v_ref, gk_ref, kn_ref, vp_ref)
            dkn_ref[...] = jnp.zeros_like(dkn_ref)
            dvp_ref[...] = jnp.zeros_like(dvp_ref)
            db_ref[...] = jnp.zeros_like(db_ref)

        @pl.when(jnp.logical_and(p_id == 0, qi == 0))
        def _():
            dgq_ref[...] = jnp.zeros_like(dgq_ref)
            dgk_ref[...] = jnp.zeros_like(dgk_ref)

        low = _low_lanes(CA_T)
        top_w = lax.broadcasted_iota(jnp.int32, (PAIR, CA_W), 0) < HEAD_DIM
        band = pl.ds(pl.multiple_of(qi * CA_T, CA_T), CA_W)
        pairs = [_ca_scores(j, q_ref, b2_ref, gq_ref, kn_ref, qi, low) for j in range(GP)]
        do2 = [do_ref[:, j * PAIR:(j + 1) * PAIR] for j in range(GP)]
        dps = [_per_head(_nt(_two_heads(do2[j], low), vp_ref[j, band, :]), CA_T) for j in range(GP)]
        probs, dsbs = [], []
        for j in range(GP):
            pj, dj = [], []
            for e in range(2):
                p = _softmax(pairs[j][0][e])
                ds = p * (dps[j][e] - jnp.sum(p * dps[j][e], axis=-1, keepdims=True))
                db_ref[2 * j + e] += ds
                pj.append(p.astype(_MXU))
                dj.append(ds.astype(_MXU))
            probs.append(pj)
            dsbs.append(dj)
        dgq = jnp.zeros((1, PAIR), F32)
        for j in range(GP):
            _, qn, qhat, r = pairs[j]
            dq_h = _per_head(_nn(jnp.concatenate(dsbs[j], axis=0), kn_ref[j, band, :]), CA_T)
            dk_t = _tn(qn, jnp.concatenate(dsbs[j], axis=1))
            dv_t = _tn(do2[j], jnp.concatenate(probs[j], axis=1))
            dkn_ref[j, :, band] += jnp.where(top_w, dk_t[:, :CA_W], dk_t[:, CA_W:])
            dvp_ref[j, :, band] += jnp.where(top_w, dv_t[:, :CA_W], dv_t[:, CA_W:])
            dqn = jnp.where(low, dq_h[0], dq_h[1]) * scale
            dgq = dgq + jnp.sum(dqn * qhat, axis=0, keepdims=True)
            dq_ref[:, j * PAIR:(j + 1) * PAIR] = _pair_norm_bwd(dqn, qhat, r, gq_ref[...], low).astype(dq_ref.dtype)
        dgq_ref[...] += dgq

        @pl.when(qi == last)
        def _():
            low_s = _low_lanes(S)
            for j in range(GP):
                cols = slice(j * PAIR, (j + 1) * PAIR)
                _, khat, rk = _pair_norm(k_ref[:, cols], gk_ref[...], low_s)
                dkn = dkn_ref[j, :, PAD:PAD + S].T
                dgk_ref[...] += jnp.sum(dkn * khat, axis=0, keepdims=True)
                dk_ref[:, cols] = _pair_norm_bwd(dkn, khat, rk, gk_ref[...], low_s).astype(dk_ref.dtype)
                dv_ref[:, cols] = dvp_ref[j, :, PAD:PAD + S].T.astype(dv_ref.dtype)

    vec = pl.BlockSpec((1, PAIR), lambda p, i: (0, 0))
    tile = pl.BlockSpec((2 * GP, CA_T, CA_W), lambda p, i: (p, 0, 0))
    full = pl.BlockSpec((S, GW), lambda p, i: (0, p))
    return _call_hosted(
        body, name, (nb, S // CA_T),
        [pl.BlockSpec((CA_T, GW), lambda p, i: (i, 3 * nb + p)),
         pl.BlockSpec((S, GW), lambda p, i: (0, 4 * nb + p)), pl.BlockSpec((S, GW), lambda p, i: (0, 5 * nb + p)),
         tile, vec, vec, pl.BlockSpec((CA_T, GW), lambda p, i: (i, nb + p))],
        [pl.BlockSpec((CA_T, GW), lambda p, i: (i, p)), full, full, tile, vec, vec],
        [jax.ShapeDtypeStruct((S, half), _MXU)] * 3
        + [jax.ShapeDtypeStruct(bias2.shape, F32), jax.ShapeDtypeStruct((1, PAIR), F32),
           jax.ShapeDtypeStruct((1, PAIR), F32)],
        [pltpu.VMEM((GP, PAD + S, PAIR), _MXU), pltpu.VMEM((GP, PAD + S, PAIR), _MXU),
         pltpu.VMEM((GP, PAIR, PAD + S), F32), pltpu.VMEM((GP, PAIR, PAD + S), F32)],
        [proj, proj, proj, bias2, gq2, gk2, dmixed], ex)


def _pack_small(parts):
    flat = jnp.concatenate([p.reshape(-1) for layer in parts for p in layer])
    n = flat.shape[0]
    n_pad = -(-n // 1024) * 1024
    return jnp.pad(flat, (0, n_pad - n)).reshape(1, n_pad)


def _unpack_small(flat, shapes):
    out, off = [], 0
    for layer in shapes:
        cur = []
        for shp in layer:
            size = 1
            for s in shp:
                size *= s
            cur.append(flat[off:off + size].reshape(shp))
            off += size
        out.append(cur)
    return out


def kernel(x, c, g_norm1, w_in, g_q, g_k, rel_bias, w_o, g_norm2, w1, w2, w_ada, b_ada, loss_target, m_g_norm1, m_w_in, m_g_q, m_g_k, m_rel_bias, m_w_o, m_g_norm2, m_w1, m_w2, m_w_ada, m_b_ada, v_g_norm1, v_w_in, v_g_q, v_g_k, v_rel_bias, v_w_o, v_g_norm2, v_w1, v_w2, v_w_ada, v_b_ada):
    L = w_in.shape[0]
    S, D = x.shape[1:]
    H2 = D // HEAD_DIM // 2
    Ca = w_ada.shape[2]
    xi, yi, ci = _pos()
    me = 4 * xi + 2 * yi + ci
    place = jnp.stack([2 * xi + yi, ci]).astype(jnp.int32)

    c_all = _all_gather_small(c, "ag_c").reshape(NDEV, D)
    b_cols = lax.dynamic_slice(b_ada, (0, me * Ca), (L, Ca))
    mod_part = _mod_partial(c_all, w_ada, b_cols, "mod_partial")
    mod_all = _all_gather_small(mod_part, "ag_mod")
    mod = lax.dynamic_index_in_dim(mod_all, me, axis=1, keepdims=False)
    mod = mod.reshape(NDEV, L, Ca).transpose(1, 0, 2).reshape(L, 6, 1, D)

    wire = lambda a: a.astype(_MXU)
    by_cols = lambda g: g.transpose(1, 0, 2).reshape(D, g.shape[0] * g.shape[2])
    W_in = {0: by_cols(_run_exchange(_gather_exchange([wire(w_in[0])]), "ag_w_in0")[0])}
    W_o, W_1, W_2 = {}, {}, {}

    xs = [x[0]]
    saved = []
    for l in range(L):
        sh1, sc1, gt1, sh2, sc2, gt2 = [mod[l, i] for i in range(6)]
        gn1, gn2 = g_norm1[l:l + 1], g_norm2[l:l + 1]
        gq2, gk2 = jnp.tile(g_q[l:l + 1], (1, 2)), jnp.tile(g_k[l:l + 1], (1, 2))
        proj, h1 = _ln_mod_matmul(xs[-1], gn1, sc1, sh1, W_in[l], f"l{l}_proj")
        (o_sb, ox_sb), got = _sb_fwd(proj, f"l{l}_sb_fwd",
                                     _gather_exchange([wire(w_o[l]), wire(w1[l]), wire(w2[l])]))
        W_o[l], W_1[l], W_2[l] = got[0].reshape(D, D), by_cols(got[1]), got[2].reshape(4 * D, D)
        bias2 = _ca_bias(rel_bias[l], f"l{l}_ca_bias")
        (o_ca,), got = _ca_fwd(proj, bias2, gq2, gk2, f"l{l}_ca_fwd",
                               _gather_exchange([wire(w_in[l + 1])]) if l + 1 < L else None)
        if got:
            W_in[l + 1] = by_cols(got[0])
        mixed = jnp.concatenate([o_sb, o_ca], axis=1)
        x1, f1 = _matmul_res_gate(mixed, W_o[l], xs[-1], gt1, False, f"l{l}_attn_out")
        u, h2 = _ln_mod_matmul(x1, gn2, sc2, sh2, W_1[l], f"l{l}_mlp_in")
        x2, f2 = _matmul_res_gate(u, W_2[l], x1, gt2, True, f"l{l}_mlp_out")
        saved.append(dict(x0=xs[-1], h1=h1, proj=proj, ox_sb=ox_sb, bias2=bias2, mixed=mixed, f1=f1, x1=x1,
                          h2=h2, u=u, f2=f2))
        xs.append(x2)

    dx, loss_part = _loss_grad(xs[-1], loss_target[0], "loss")

    owns, recv_b = {}, {}
    ready = []
    small_parts = [None] * L

    def partials(keys, grads, recv_a):
        parts = []
        for key, g, r in zip(keys, grads, recv_a):
            owns[key], part = _rs_chip_partial(place, g, r, f"rs_partial_l{key[0]}_{key[1]}")
            parts.append(part)
        return parts

    for l in reversed(range(L)):
        sv = saved[l]
        sh1, sc1, gt1, sh2, sc2, gt2 = [mod[l, i] for i in range(6)]
        gn1, gn2 = g_norm1[l:l + 1], g_norm2[l:l + 1]
        gq2, gk2 = jnp.tile(g_q[l:l + 1], (1, 2)), jnp.tile(g_k[l:l + 1], (1, 2))
        dz2, dgt2, du = _gate_nt_matmul(dx, sv["f2"], gt2, W_2[l], sv["u"], f"l{l}_mlp_out_bwd")
        gw2 = _tn_matmul(sv["u"], dz2, False, True, f"l{l}_gw2")
        gw1 = _tn_matmul(sv["h2"], du, True, False, f"l{l}_gw1")
        dx, dsh2, dsc2, dgn2 = _nt_ln_bwd(du, W_1[l], sv["x1"], gn2, sc2, sh2, dx, f"l{l}_mlp_in_bwd")
        dz1, dgt1, dmixed = _gate_nt_matmul(dx, sv["f1"], gt1, W_o[l], None, f"l{l}_attn_out_bwd")
        gwo = _tn_matmul(sv["mixed"], dz1, False, False, f"l{l}_gwo")
        ready += [((l, 1), gwo), ((l, 2), gw1), ((l, 3), gw2)]
        keys, grads = [k for k, _ in ready], [g for _, g in ready]
        (dq_sb, dk_sb, dv_sb), recv_a = _sb_bwd(sv["proj"], sv["ox_sb"], dmixed, f"l{l}_sb_bwd",
                                                _sibling_exchange(grads))
        parts = partials(keys, grads, recv_a)
        (dq_ca, dk_ca, dv_ca, dbias2, dgq2, dgk2), got = _ca_bwd(sv["proj"], sv["bias2"], gq2, gk2, dmixed,
                                                                 f"l{l}_ca_bwd", _chip_exchange(parts))
        recv_b.update(zip(keys, got))
        dgq = dgq2[:, :HEAD_DIM] + dgq2[:, HEAD_DIM:]
        dgk = dgk2[:, :HEAD_DIM] + dgk2[:, HEAD_DIM:]
        drb = _ca_bias_bwd(dbias2, f"l{l}_ca_bias_bwd")
        dproj = jnp.concatenate([dq_sb, dk_sb, dv_sb, dq_ca, dk_ca, dv_ca], axis=1)
        gwin = _tn_matmul(sv["h1"], dproj, True, False, f"l{l}_gwin")
        ready = [((l, 0), gwin)]
        dx, dsh1, dsc1, dgn1 = _nt_ln_bwd(dproj, W_in[l], sv["x0"], gn1, sc1, sh1, dx, f"l{l}_proj_bwd")
        dmod = jnp.concatenate([dsh1, dsc1, dgt1, dsh2, dsc2, dgt2], axis=1)
        small_parts[l] = [dgn1, dgq, dgk, drb, dgn2, dmod]
    grad_x = dx[None]

    keys, grads = [k for k, _ in ready], [g for _, g in ready]
    parts = partials(keys, grads, _run_exchange(_sibling_exchange(grads), "rs_sibling_last"))
    recv_b.update(zip(keys, _run_exchange(_chip_exchange(parts), "rs_chips_last")))
    big_out = []
    for t, (w, m, v) in enumerate([(w_in, m_w_in, v_w_in), (w_o, m_w_o, v_w_o), (w1, m_w1, v_w1), (w2, m_w2, v_w2)]):
        big_out.append(_rs_sum_adamw([owns[(l, t)] for l in range(L)], [recv_b[(l, t)] for l in range(L)],
                                     w, m, v, f"adamw_big_{t}"))

    packed = _pack_small(small_parts)
    gathered_small = _all_gather_small(packed, "ag_small_grads")
    small_sum = _sum_devices(gathered_small, "sum_small_grads")
    shapes = [[(1, D), (1, HEAD_DIM), (1, HEAD_DIM), (H2, N_REL), (1, D), (1, 6 * D)]] * L
    names = ["g_norm1", "g_q", "g_k", "rel_bias", "g_norm2", "b_ada"]
    small_w = {"g_norm1": (g_norm1, m_g_norm1, v_g_norm1), "g_q": (g_q, m_g_q, v_g_q), "g_k": (g_k, m_g_k, v_g_k),
               "rel_bias": (rel_bias, m_rel_bias, v_rel_bias), "g_norm2": (g_norm2, m_g_norm2, v_g_norm2),
               "b_ada": (b_ada, m_b_ada, v_b_ada)}
    packs = [_pack_small([[small_w[n][k][l] for n in names] for l in range(L)]) for k in range(3)]
    n_pad = packed.shape[1]
    as_rows = lambda a: a.reshape(n_pad // 128, 128)
    sd, sm, sv_ = _adamw(as_rows(packs[0]), as_rows(small_sum), as_rows(packs[1]), as_rows(packs[2]), "adamw_small")
    small_out = {}
    for key, flat in [("grad", small_sum), ("delta", sd), ("m", sm), ("v", sv_)]:
        per_layer = _unpack_small(flat.reshape(-1), shapes)
        for i, n in enumerate(names):
            small_out[(key, n)] = jnp.stack([per_layer[l][i].reshape(small_w[n][0].shape[1:]) for l in range(L)])

    layer_len = 2 * D + 2 * HEAD_DIM + H2 * N_REL + 6 * D
    rows = gathered_small.reshape(NDEV, n_pad)
    dmod_all = jnp.stack([rows[:, l * layer_len + layer_len - 6 * D:(l + 1) * layer_len] for l in range(L)])
    dmod_cols = lax.dynamic_slice(dmod_all, (0, 0, me * Ca), (L, NDEV, Ca))
    dmod_cols = jnp.pad(dmod_cols, ((0, 0), (0, 128 - NDEV), (0, 0)))
    c_t = jnp.pad(c_all.T, ((0, 0), (0, 128 - NDEV)))
    g_ada = _w_ada_grad(c_t, dmod_cols, "w_ada_grad")
    flat2 = lambda a: a.reshape(L * D, Ca)
    ad, am, av = _adamw(flat2(w_ada), flat2(g_ada), flat2(m_w_ada), flat2(v_w_ada), "adamw_w_ada")
    ada_out = [g_ada] + [a.reshape(L, D, Ca) for a in (ad, am, av)]

    def leaf(kind):
        k = {"grad": 0, "delta": 1, "m": 2, "v": 3}[kind]
        return [small_out[(kind, "g_norm1")], big_out[0][k], small_out[(kind, "g_q")], small_out[(kind, "g_k")],
                small_out[(kind, "rel_bias")], big_out[1][k], small_out[(kind, "g_norm2")], big_out[2][k],
                big_out[3][k], ada_out[k], small_out[(kind, "b_ada")]]

    loss = lax.psum(loss_part[0, 0], ("x", "y", "c"))
    return (loss, grad_x, *leaf("grad"), *leaf("delta"), *leaf("m"), *leaf("v"))
```

```python
import functools

import jax
import jax.numpy as jnp
from jax import lax
from jax.experimental import pallas as pl
from jax.experimental.pallas import tpu as pltpu

F32 = jnp.float32
_MXU = jnp.bfloat16

HEAD_DIM = 64
CHUNK = 64
LEFT_CHUNKS = 8
PAD = LEFT_CHUNKS * CHUNK
BAND = PAD + CHUNK
REL_CLIP = 128
N_REL = 2 * REL_CLIP + 1
EPS = 1e-6
NEG = -1e30
NDEV = 8
SB_T = 128
CA_T = 2 * CHUNK
CA_W = CA_T + PAD
SB_SKIP = -104.0
PAIR = 2 * HEAD_DIM
SB_PAIRS = 4
CA_PAIRS_FWD = 4
CA_PAIRS_BWD = 2
ROW_BLOCK = 512
SKEW_W = 767

ADAM_LR, ADAM_B1, ADAM_B2, ADAM_EPS, ADAM_WD, ADAM_STEP = 0.001, 0.9, 0.999, 1e-08, 0.01, 10

MESH = pl.DeviceIdType.MESH
VMEM_SPEC = pl.BlockSpec(memory_space=pltpu.VMEM)
SMEM_SPEC = pl.BlockSpec(memory_space=pltpu.SMEM)
ANY_SPEC = pl.BlockSpec(memory_space=pl.ANY)


def _nn(a, b):
    return lax.dot_general(a, b, (((1,), (0,)), ((), ())), preferred_element_type=F32)


def _nt(a, b):
    return lax.dot_general(a, b, (((1,), (1,)), ((), ())), preferred_element_type=F32)


def _tn(a, b):
    return lax.dot_general(a, b, (((0,), (0,)), ((), ())), preferred_element_type=F32)


def _blk(n, pref):
    return pref if n % pref == 0 else n


def _pos():
    return lax.axis_index("x"), lax.axis_index("y"), lax.axis_index("c")


def _flip(v, bit):
    return 1 - v if bit else v


def _all_gather_small(blk, name):
    R, C = blk.shape

    def body(x_ref, out_ref, send_sems, recv_sems):
        x, y, c = _pos()
        me = 4 * x + 2 * y + c

        def peer(k):
            return (_flip(x, k & 4), _flip(y, k & 2), _flip(c, k & 1))

        def copy(k, slot):
            return pltpu.make_async_remote_copy(
                src_ref=x_ref, dst_ref=out_ref.at[slot], send_sem=send_sems.at[k - 1],
                recv_sem=recv_sems.at[k - 1], device_id=peer(k), device_id_type=MESH)

        out_ref[pl.ds(me, 1), :, :] = x_ref[...].reshape(1, R, C)
        sends = [copy(k, me) for k in range(1, NDEV)]
        for cp in sends:
            cp.start()
        for k in range(1, NDEV):
            px, py, pc = peer(k)
            copy(k, 4 * px + 2 * py + pc).wait_recv()
        for cp in sends:
            cp.wait_send()

    return pl.pallas_call(
        body, name=name,
        out_shape=jax.ShapeDtypeStruct((NDEV, R, C), blk.dtype),
        in_specs=[VMEM_SPEC], out_specs=VMEM_SPEC,
        scratch_shapes=[pltpu.SemaphoreType.DMA((NDEV - 1,)), pltpu.SemaphoreType.DMA((NDEV - 1,))],
    )(blk)


class _Exchange:
    def __init__(self, inputs, out_shapes, sems, start, finish, middle=None):
        self.inputs, self.out_shapes, self.sems = list(inputs), list(out_shapes), list(sems)
        self.start, self.middle, self.finish = start, middle, finish


def _run_exchange(ex, name):
    n_in, n_out = len(ex.inputs), len(ex.out_shapes)

    def body(*refs):
        ins, outs, sems = refs[:n_in], refs[n_in:n_in + n_out], refs[n_in + n_out:]
        ex.start(ins, outs, sems)
        if ex.middle is not None:
            ex.middle(ins, outs, sems)
        ex.finish(ins, outs, sems)

    return pl.pallas_call(
        body, name=name, out_shape=ex.out_shapes, in_specs=[ANY_SPEC] * n_in, out_specs=[ANY_SPEC] * n_out,
        scratch_shapes=ex.sems,
    )(*ex.inputs)


def _hosted(body, n_in, n_out, ex, step, steps):
    if ex is None:
        return body
    xi, xo = len(ex.inputs), len(ex.out_shapes)

    def wrapped(*refs):
        own_in, ex_in = refs[:n_in], refs[n_in:n_in + xi]
        rest = refs[n_in + xi:]
        own_out, ex_out = rest[:n_out], rest[n_out:n_out + xo]
        rest = rest[n_out + xo:]
        own_scratch, ex_sems = rest[:len(rest) - len(ex.sems)], rest[len(rest) - len(ex.sems):]
        t = step()
        pl.when(t == 0)(lambda: ex.start(ex_in, ex_out, ex_sems))
        body(*own_in, *own_out, *own_scratch)
        if ex.middle is not None:
            pl.when(t == (steps * 7) // 8)(lambda: ex.middle(ex_in, ex_out, ex_sems))
        pl.when(t == steps - 1)(lambda: ex.finish(ex_in, ex_out, ex_sems))

    return wrapped


def _call_hosted(body, name, grid, in_specs, out_specs, out_shape, scratch, args, ex):
    n_in, n_out = len(in_specs), len(out_specs)
    steps = 1
    for extent in grid:
        steps *= extent

    def step():
        t = pl.program_id(0)
        for axis in range(1, len(grid)):
            t = t * grid[axis] + pl.program_id(axis)
        return t

    if ex is not None:
        in_specs = in_specs + [ANY_SPEC] * len(ex.inputs)
        out_specs = out_specs + [ANY_SPEC] * len(ex.out_shapes)
        out_shape = out_shape + ex.out_shapes
        scratch = scratch + ex.sems
        args = args + ex.inputs
    outs = pl.pallas_call(
        _hosted(body, n_in, n_out, ex, step, steps), name=name, grid=grid, in_specs=in_specs, out_specs=out_specs,
        out_shape=out_shape, scratch_shapes=scratch,
    )(*args)
    return list(outs[:n_out]), list(outs[n_out:])


def _gather_exchange(shards):
    n = len(shards)

    def setup(ins, outs, sems):
        send_sems, recv_sems, local_sems = sems
        x, y, c = _pos()
        me, sibling = (x, y, c), (x, y, 1 - c)
        chips = [(1 - x, y), (x, 1 - y), (1 - x, 1 - y)]

        def copy(i, k, block, to, src=None):
            px, py, pc = block
            dst = outs[i].at[4 * px + 2 * py + pc]
            return pltpu.make_async_remote_copy(
                src_ref=dst if src is None else src, dst_ref=dst, send_sem=send_sems.at[7 * i + k],
                recv_sem=recv_sems.at[7 * i + k], device_id=to, device_id_type=MESH)

        def mine(i):
            return pltpu.make_async_copy(ins[i], outs[i].at[4 * x + 2 * y + c], local_sems.at[i])

        def first(i):
            return [copy(i, 0, me, sibling, src=ins[i])] + [
                copy(i, 1 + j, me, (*chip, c), src=ins[i]) for j, chip in enumerate(chips)]

        def passed(i, j):
            return copy(i, 4 + j, (*chips[j], c), sibling)

        return me, sibling, chips, c, copy, mine, first, passed

    def start(ins, outs, sems):
        _, _, _, _, _, mine, first, _ = setup(ins, outs, sems)
        for i in range(n):
            mine(i).start()
            for cp in first(i):
                cp.start()

    def middle(ins, outs, sems):
        me, _, chips, c, copy, _, _, passed = setup(ins, outs, sems)
        for j, chip in enumerate(chips):
            for i in range(n):
                copy(i, 1 + j, (*chip, c), me).wait_recv()
                passed(i, j).start()

    def finish(ins, outs, sems):
        me, sibling, chips, c, copy, mine, first, passed = setup(ins, outs, sems)
        for i in range(n):
            copy(i, 0, sibling, me).wait_recv()
            for j, chip in enumerate(chips):
                copy(i, 4 + j, (*chip, 1 - c), me).wait_recv()
        for i in range(n):
            for cp in first(i) + [passed(i, j) for j in range(3)]:
                cp.wait_send()
            mine(i).wait()

    return _Exchange(
        shards, [jax.ShapeDtypeStruct((NDEV,) + s.shape, s.dtype) for s in shards],
        [pltpu.SemaphoreType.DMA((7 * n,)), pltpu.SemaphoreType.DMA((7 * n,)), pltpu.SemaphoreType.DMA((n,))],
        start, finish, middle)


def _sibling_exchange(grads):
    n = len(grads)

    def copies(ins, outs, sems):
        send_sems, recv_sems = sems
        x, y, c = _pos()
        return [pltpu.make_async_remote_copy(
            src_ref=ins[i].at[2 * q + (1 - c)], dst_ref=outs[i].at[q], send_sem=send_sems.at[4 * i + q],
            recv_sem=recv_sems.at[4 * i + q], device_id=(x, y, 1 - c), device_id_type=MESH)
            for i in range(n) for q in range(4)]

    def start(ins, outs, sems):
        for cp in copies(ins, outs, sems):
            cp.start()

    def finish(ins, outs, sems):
        for cp in copies(ins, outs, sems):
            cp.wait()

    return _Exchange(
        grads, [jax.ShapeDtypeStruct((4,) + g.shape[1:], g.dtype) for g in grads],
        [pltpu.SemaphoreType.DMA((4 * n,)), pltpu.SemaphoreType.DMA((4 * n,))], start, finish)


def _chip_exchange(parts):
    n = len(parts)

    def copies(ins, outs, sems):
        send_sems, recv_sems = sems
        x, y, c = _pos()
        return [pltpu.make_async_remote_copy(
            src_ref=ins[i].at[j - 1], dst_ref=outs[i].at[j - 1], send_sem=send_sems.at[3 * i + j - 1],
            recv_sem=recv_sems.at[3 * i + j - 1], device_id=(_flip(x, j & 2), _flip(y, j & 1), c),
            device_id_type=MESH) for i in range(n) for j in range(1, 4)]

    def start(ins, outs, sems):
        for cp in copies(ins, outs, sems):
            cp.start()

    def finish(ins, outs, sems):
        for cp in copies(ins, outs, sems):
            cp.wait()

    return _Exchange(
        parts, [jax.ShapeDtypeStruct(p.shape, p.dtype) for p in parts],
        [pltpu.SemaphoreType.DMA((3 * n,)), pltpu.SemaphoreType.DMA((3 * n,))], start, finish)


def _rs_chip_partial(place, grad, recv, name):
    _, R, C = grad.shape
    tr = _blk(R, 256)

    def body(place_ref, *refs):
        g_refs, r_refs = refs[:4], refs[4:8]
        own_ref, out_ref = refs[8:]
        own_ref[...] = g_refs[0][0] + r_refs[0][0]
        for j in range(1, 4):
            out_ref[j - 1] = (g_refs[j][0] + r_refs[j][0]).astype(out_ref.dtype)

    def g_map(j):
        return lambda i, p: (2 * jnp.bitwise_xor(p[0], j) + p[1], i, 0)

    def r_map(j):
        return lambda i, p: (jnp.bitwise_xor(p[0], j), i, 0)

    grid_spec = pltpu.PrefetchScalarGridSpec(
        num_scalar_prefetch=1, grid=(R // tr,),
        in_specs=[pl.BlockSpec((1, tr, C), g_map(j)) for j in range(4)]
        + [pl.BlockSpec((1, tr, C), r_map(j)) for j in range(4)],
        out_specs=[pl.BlockSpec((tr, C), lambda i, p: (i, 0)), pl.BlockSpec((3, tr, C), lambda i, p: (0, i, 0))])
    return pl.pallas_call(
        body, name=name, grid_spec=grid_spec,
        out_shape=[jax.ShapeDtypeStruct((R, C), F32), jax.ShapeDtypeStruct((3, R, C), _MXU)],
    )(place, *([grad] * 4), *([recv] * 4))


def _adamw_math(w, g, m, v):
    m = ADAM_B1 * m + (1.0 - ADAM_B1) * g
    v = ADAM_B2 * v + (1.0 - ADAM_B2) * (g * g)
    m_hat = m / (1.0 - ADAM_B1 ** ADAM_STEP)
    v_hat = v / (1.0 - ADAM_B2 ** ADAM_STEP)
    delta = -ADAM_LR * (m_hat / (jnp.sqrt(v_hat) + ADAM_EPS) + ADAM_WD * w)
    return delta, m, v


def _adamw(w, g, m, v, name):
    R, C = w.shape
    tr = _blk(R, 256)

    def body(w_ref, g_ref, m_ref, v_ref, d_ref, nm_ref, nv_ref):
        d, nm, nv = _adamw_math(w_ref[...], g_ref[...], m_ref[...], v_ref[...])
        d_ref[...] = d
        nm_ref[...] = nm
        nv_ref[...] = nv

    spec = pl.BlockSpec((tr, C), lambda i: (i, 0))
    return pl.pallas_call(
        body, name=name, grid=(R // tr,), in_specs=[spec] * 4, out_specs=[spec] * 3,
        out_shape=[jax.ShapeDtypeStruct((R, C), F32)] * 3,
    )(w, g, m, v)


def _rs_sum_adamw(owns, recvs, w, m, v, name):
    L, R, C = w.shape
    tr = _blk(R, 256)
    nr = R // tr

    def body(o0, o1, r0, r1, w_ref, m_ref, v_ref, g_ref, d_ref, nm_ref, nv_ref):
        def step(o_ref, r_ref):
            g = o_ref[...]
            for j in range(3):
                g = g + r_ref[j].astype(F32)
            d, nm, nv = _adamw_math(w_ref[0], g, m_ref[0], v_ref[0])
            g_ref[0] = g
            d_ref[0] = d
            nm_ref[0] = nm
            nv_ref[0] = nv

        pl.when(pl.program_id(0) == 0)(lambda: step(o0, r0))
        pl.when(pl.program_id(0) == 1)(lambda: step(o1, r1))

    def hold(layer):
        if layer == 0:
            return lambda l, i: i * (1 - l) + (nr - 1) * l
        return lambda l, i: i * l

    own_spec = [pl.BlockSpec((tr, C), functools.partial(lambda l, i, f: (f(l, i), 0), f=hold(k))) for k in range(2)]
    recv_spec = [pl.BlockSpec((3, tr, C), functools.partial(lambda l, i, f: (0, f(l, i), 0), f=hold(k)))
                 for k in range(2)]
    lay = pl.BlockSpec((1, tr, C), lambda l, i: (l, i, 0))
    return pl.pallas_call(
        body, name=name, grid=(L, nr),
        in_specs=own_spec + recv_spec + [lay] * 3, out_specs=[lay] * 4,
        out_shape=[jax.ShapeDtypeStruct((L, R, C), F32)] * 4,
    )(owns[0], owns[1], recvs[0], recvs[1], w, m, v)


def _silu(x):
    return x / (1.0 + jnp.exp(-x))


def _mod_partial(c_all, w_ada, b_cols, name):
    L, D, Ca = w_ada.shape

    def body(c_ref, w_ref, b_ref, o_ref):
        act = _silu(c_ref[...]).astype(_MXU)
        for l in range(L):
            o_ref[:, l * Ca:(l + 1) * Ca] = _nn(act, w_ref[l].astype(_MXU)) + b_ref[l:l + 1, :]

    return pl.pallas_call(
        body, name=name, out_shape=jax.ShapeDtypeStruct((NDEV, L * Ca), F32),
        in_specs=[VMEM_SPEC] * 3, out_specs=VMEM_SPEC,
    )(c_all, w_ada, b_cols)


def _w_ada_grad(c_t, dmod_cols, name):
    L, _, Ca = dmod_cols.shape
    D = c_t.shape[0]

    def body(c_ref, d_ref, o_ref):
        act = _silu(c_ref[...]).astype(_MXU)
        for l in range(L):
            o_ref[l] = _nn(act, d_ref[l].astype(_MXU))

    return pl.pallas_call(
        body, name=name, out_shape=jax.ShapeDtypeStruct((L, D, Ca), F32),
        in_specs=[VMEM_SPEC] * 2, out_specs=VMEM_SPEC,
    )(c_t, dmod_cols)


def _sum_devices(gathered, name):
    _, _, N = gathered.shape

    def body(x_ref, o_ref):
        acc = x_ref[0]
        for d in range(1, NDEV):
            acc = acc + x_ref[d]
        o_ref[...] = acc

    return pl.pallas_call(
        body, name=name, out_shape=jax.ShapeDtypeStruct((1, N), F32),
        in_specs=[VMEM_SPEC], out_specs=VMEM_SPEC,
    )(gathered)


def _ln_mod_matmul(x, g, sc, sh, w, name):
    S, D = x.shape
    N = w.shape[1]
    tm = _blk(S, ROW_BLOCK)

    def body(x_ref, g_ref, sc_ref, sh_ref, w_ref, o_ref, h_ref):
        xv = x_ref[...]
        r = lax.rsqrt(jnp.mean(xv * xv, axis=-1, keepdims=True) + EPS)
        hv = ((xv * r) * g_ref[...]) * (1.0 + sc_ref[...]) + sh_ref[...]
        hb = hv.astype(_MXU)
        h_ref[...] = hb
        o_ref[...] = _nn(hb, w_ref[...]).astype(o_ref.dtype)

    vec = pl.BlockSpec((1, D), lambda i: (0, 0))
    row = lambda width: pl.BlockSpec((tm, width), lambda i: (i, 0))
    return pl.pallas_call(
        body, name=name, grid=(S // tm,),
        in_specs=[row(D), vec, vec, vec, pl.BlockSpec((D, N), lambda i: (0, 0))],
        out_specs=[row(N), row(D)],
        out_shape=[jax.ShapeDtypeStruct((S, N), _MXU), jax.ShapeDtypeStruct((S, D), _MXU)],
    )(x, g, sc, sh, w)


def _matmul_res_gate(a, w, xres, gt, relu2, name):
    S, K = a.shape
    N = w.shape[1]
    tm = _blk(S, 512)

    def body(a_ref, w_ref, x_ref, gt_ref, o_ref, f_ref):
        av = a_ref[...]
        if relu2:
            af = jnp.maximum(av.astype(F32), 0.0)
            av = (af * af).astype(_MXU)
        f = _nn(av, w_ref[...])
        f_ref[...] = f.astype(f_ref.dtype)
        o_ref[...] = x_ref[...] + gt_ref[...] * f

    row = lambda width: pl.BlockSpec((tm, width), lambda i: (i, 0))
    return pl.pallas_call(
        body, name=name, grid=(S // tm,),
        in_specs=[row(K), pl.BlockSpec((K, N), lambda i: (0, 0)), row(N), pl.BlockSpec((1, N), lambda i: (0, 0))],
        out_specs=[row(N), row(N)],
        out_shape=[jax.ShapeDtypeStruct((S, N), F32), jax.ShapeDtypeStruct((S, N), _MXU)],
    )(a, w, xres, gt)


def _loss_grad(y, t, name):
    S, D = y.shape
    tm = _blk(S, 512)
    last = S // tm - 1

    def body(y_ref, t_ref, dy_ref, l_ref, acc_ref):
        i = pl.program_id(0)
        e = y_ref[...] - t_ref[...]
        dy_ref[...] = e * (1.0 / D)
        part = jnp.sum(e * e, axis=0, keepdims=True)

        @pl.when(i == 0)
        def _():
            acc_ref[...] = part

        @pl.when(i > 0)
        def _():
            acc_ref[...] += part

        @pl.when(i == last)
        def _():
            l_ref[...] = (0.5 / D) * jnp.sum(acc_ref[...], axis=1, keepdims=True)

    row = pl.BlockSpec((tm, D), lambda i: (i, 0))
    return pl.pallas_call(
        body, name=name, grid=(S // tm,), in_specs=[row, row],
        out_specs=[row, pl.BlockSpec((1, 1), lambda i: (0, 0))],
        out_shape=[jax.ShapeDtypeStruct((S, D), F32), jax.ShapeDtypeStruct((1, 1), F32)],
        scratch_shapes=[pltpu.VMEM((1, D), F32)],
    )(y, t)


def _accumulate(ref, part, first):
    @pl.when(first)
    def _():
        ref[...] = part

    @pl.when(jnp.logical_not(first))
    def _():
        ref[...] += part


def _gate_nt_matmul(dx, f, gt, w, u, name):
    S, D = dx.shape
    N = w.shape[0]
    tm = _blk(S, ROW_BLOCK)
    with_u = u is not None

    def body(*refs):
        if with_u:
            dx_ref, f_ref, gt_ref, w_ref, u_ref, dz_ref, dgt_ref, res_ref = refs
        else:
            dx_ref, f_ref, gt_ref, w_ref, dz_ref, dgt_ref, res_ref = refs
        dxv = dx_ref[...]
        dz = (dxv * gt_ref[...]).astype(_MXU)
        dz_ref[...] = dz
        _accumulate(dgt_ref, jnp.sum(dxv * f_ref[...].astype(F32), axis=0, keepdims=True), pl.program_id(0) == 0)
        r = _nt(dz, w_ref[...])
        if with_u:
            r = r * (2.0 * jnp.maximum(u_ref[...].astype(F32), 0.0))
        res_ref[...] = r.astype(res_ref.dtype)

    row = lambda width: pl.BlockSpec((tm, width), lambda i: (i, 0))
    in_specs = [row(D), row(D), pl.BlockSpec((1, D), lambda i: (0, 0)), pl.BlockSpec((N, D), lambda i: (0, 0))]
    args = [dx, f, gt, w]
    if with_u:
        in_specs.append(row(N))
        args.append(u)
    return pl.pallas_call(
        body, name=name, grid=(S // tm,), in_specs=in_specs,
        out_specs=[row(D), pl.BlockSpec((1, D), lambda i: (0, 0)), row(N)],
        out_shape=[jax.ShapeDtypeStruct((S, D), _MXU), jax.ShapeDtypeStruct((1, D), F32),
                   jax.ShapeDtypeStruct((S, N), _MXU)],
    )(*args)


def _tn_matmul(a, b, by_col, relu2, name):
    S, Ka = a.shape
    Nb = b.shape[1]
    ts = _blk(S, 2 * ROW_BLOCK)
    half = NDEV // 2
    if by_col:
        R, C = Ka, Nb // NDEV
        a_spec = pl.BlockSpec((ts, Ka), lambda h, k: (k, 0))
        b_spec = pl.BlockSpec((ts, half * C), lambda h, k: (k, h))
    else:
        R, C = Ka // NDEV, Nb
        a_spec = pl.BlockSpec((ts, half * R), lambda h, k: (k, h))
        b_spec = pl.BlockSpec((ts, Nb), lambda h, k: (k, 0))

    def body(a_ref, b_ref, o_ref):
        av = a_ref[...]
        if relu2:
            af = jnp.maximum(av.astype(F32), 0.0)
            av = (af * af).astype(_MXU)
        p = _tn(av, b_ref[...])
        first = pl.program_id(1) == 0
        for d in range(half):
            part = p[:, d * C:(d + 1) * C] if by_col else p[d * R:(d + 1) * R, :]
            _accumulate(o_ref.at[d], part, first)

    return pl.pallas_call(
        body, name=name, grid=(NDEV // half, S // ts), in_specs=[a_spec, b_spec],
        out_specs=pl.BlockSpec((half, R, C), lambda h, k: (h, 0, 0)),
        out_shape=jax.ShapeDtypeStruct((NDEV, R, C), F32),
    )(a, b)


def _nt_ln_bwd(dy, w, x, g, sc, sh, dxres, name):
    S, D = x.shape
    N = w.shape[1]
    tm = _blk(S, ROW_BLOCK)

    def body(dy_ref, w_ref, x_ref, g_ref, sc_ref, sh_ref, dxr_ref, dx_ref, dsh_ref, dsc_ref, dg_ref):
        dh = _nt(dy_ref[...], w_ref[...])
        xv = x_ref[...]
        r = lax.rsqrt(jnp.mean(xv * xv, axis=-1, keepdims=True) + EPS)
        xhat = xv * r
        gv = g_ref[...]
        dn = dh * (1.0 + sc_ref[...])
        dxhat = dn * gv
        dxv = r * (dxhat - xhat * jnp.mean(dxhat * xhat, axis=-1, keepdims=True))
        dx_ref[...] = dxr_ref[...] + dxv
        first = pl.program_id(0) == 0
        _accumulate(dsh_ref, jnp.sum(dh, axis=0, keepdims=True), first)
        _accumulate(dsc_ref, jnp.sum(dh * (xhat * gv), axis=0, keepdims=True), first)
        _accumulate(dg_ref, jnp.sum(dn * xhat, axis=0, keepdims=True), first)

    row = lambda width: pl.BlockSpec((tm, width), lambda i: (i, 0))
    vec = pl.BlockSpec((1, D), lambda i: (0, 0))
    return pl.pallas_call(
        body, name=name, grid=(S // tm,),
        in_specs=[row(N), pl.BlockSpec((D, N), lambda i: (0, 0)), row(D), vec, vec, vec, row(D)],
        out_specs=[row(D), vec, vec, vec],
        out_shape=[jax.ShapeDtypeStruct((S, D), F32)] + [jax.ShapeDtypeStruct((1, D), F32)] * 3,
    )(dy, w, x, g, sc, sh, dxres)


def _split2(v):
    hi = v.astype(_MXU)
    mid = (v - hi.astype(F32)).astype(_MXU)
    return hi, mid


def _tri_sums(vs, tri2):
    T = vs[0].shape[0]
    out = []
    for j in range(len(vs) // 2):
        wide = [jnp.concatenate(_split2(vs[2 * j + e]), axis=1) for e in range(2)]
        for both in _per_head(_nn(jnp.concatenate(wide, axis=0), tri2), T):
            out.append((both[:, :T], both[:, T:]))
    return out


def _tri2(T, inclusive):
    j = lax.broadcasted_iota(jnp.int32, (2 * T, 2 * T), 0) % T
    s = lax.broadcasted_iota(jnp.int32, (2 * T, 2 * T), 1)
    keep = (j >= s) if inclusive else (j > s)
    return jnp.where((s >= T) | keep, 1.0, 0.0).astype(_MXU)


def _log_sigmoid(z):
    return jnp.minimum(z, 0.0) - jnp.log(1.0 + jnp.exp(-jnp.abs(z)))


def _per_head(tall, T):
    return [tall[h * T:(h + 1) * T] for h in range(tall.shape[0] // T)]


def _sb_blocks(q_tall, k2, strict, tri2, carry):
    T = k2[0].shape[0]
    zs = []
    for qt, kblk in zip(q_tall, k2):
        zs += _per_head(_nt(qt, kblk), T)
    lbs, l1s = [], []
    for z in zs:
        lb = _log_sigmoid(z)
        l1 = lb - z
        if strict is not None:
            l1 = jnp.where(strict, l1, 0.0)
        lbs.append(lb)
        l1s.append(l1)
    sums = _tri_sums(l1s, tri2)
    amps, new_carry = [], []
    for lb, (sfx, tot), c in zip(lbs, sums, carry):
        a = jnp.exp(lb + sfx + c)
        if strict is not None:
            a = jnp.where(strict, a, 0.0)
        amps.append(a)
        new_carry.append(c + tot)
    return lbs, amps, new_carry


def _sb_alive(carry):
    top = carry[0]
    for c in carry[1:]:
        top = jnp.maximum(top, c)
    return jnp.max(top) > SB_SKIP


def _skew_index():
    i = lax.broadcasted_iota(jnp.int32, (CA_T, SKEW_W + 1), 0)
    m = lax.broadcasted_iota(jnp.int32, (CA_T, SKEW_W + 1), 1)
    wrapped = i + m >= SKEW_W
    row = jnp.where(wrapped, i + 1, i)
    j = jnp.where(wrapped, i + m - SKEW_W, i + m)
    a = row // CHUNK
    jj = j - a * CHUNK
    inband = (jj >= 0) & (jj < BAND) & (j < CA_W) & (row < CA_T)
    idx = jnp.clip((row - a * CHUNK) + PAD - jj, -REL_CLIP, REL_CLIP) + REL_CLIP
    return inband, idx, wrapped


def _skew(tile):
    H = tile.shape[0]
    flat = jnp.pad(tile, ((0, 0), (0, 0), (0, SKEW_W - CA_W))).reshape(H, CA_T * SKEW_W)
    return jnp.pad(flat, ((0, 0), (0, CA_T))).reshape(H, CA_T, SKEW_W + 1)


def _unskew(view):
    H = view.shape[0]
    flat = view.reshape(H, CA_T * (SKEW_W + 1))[:, :CA_T * SKEW_W]
    return flat.reshape(H, CA_T, SKEW_W)[:, :, :CA_W]


def _ca_bias(rel_bias, name):
    H = rel_bias.shape[0]
    top = rel_bias[:, N_REL - 1:]
    by_offset = jnp.concatenate(
        [jnp.broadcast_to(top, (H, PAD - REL_CLIP + 1)), jnp.flip(rel_bias[:, :N_REL - 1], axis=1),
         jnp.broadcast_to(top, (H, SKEW_W + 1 - (PAD - REL_CLIP + 1) - (N_REL - 1)))], axis=1)

    def body(t_ref, o_ref):
        inband, _, wrapped = _skew_index()
        vals = jnp.where(wrapped, t_ref[0][:, 0:1], t_ref[0])
        o_ref[0] = jnp.where(inband, vals, NEG)

    view = pl.pallas_call(
        body, name=name, grid=(H,), in_specs=[pl.BlockSpec((1, 1, SKEW_W + 1), lambda h: (h, 0, 0))],
        out_specs=pl.BlockSpec((1, CA_T, SKEW_W + 1), lambda h: (h, 0, 0)),
        out_shape=jax.ShapeDtypeStruct((H, CA_T, SKEW_W + 1), F32),
    )(by_offset.reshape(H, 1, SKEW_W + 1))
    return _unskew(view)


def _ca_bias_bwd(dbias, name):
    H = dbias.shape[0]

    def body(d_ref, o_ref):
        inband, idx, _ = _skew_index()
        d = jnp.where(inband, d_ref[0], 0.0)
        clipped = idx == N_REL - 1
        by_offset = jnp.sum(jnp.where(clipped, 0.0, d), axis=0, keepdims=True)
        top = jnp.sum(jnp.sum(jnp.where(clipped, d, 0.0), axis=0, keepdims=True), axis=1, keepdims=True)
        lane = lax.broadcasted_iota(jnp.int32, (1, SKEW_W + 1), 1)
        o_ref[0] = jnp.where(lane == 0, top, by_offset)

    out = pl.pallas_call(
        body, name=name, grid=(H,), in_specs=[pl.BlockSpec((1, CA_T, SKEW_W + 1), lambda h: (h, 0, 0))],
        out_specs=pl.BlockSpec((1, 1, SKEW_W + 1), lambda h: (h, 0, 0)),
        out_shape=jax.ShapeDtypeStruct((H, 1, SKEW_W + 1), F32),
    )(_skew(dbias))[:, 0]
    first = PAD - REL_CLIP + 1
    return jnp.concatenate([jnp.flip(out[:, first:first + N_REL - 1], axis=1), out[:, 0:1]], axis=1)


def _low_lanes(rows):
    return lax.broadcasted_iota(jnp.int32, (rows, PAIR), 1) < HEAD_DIM


def _one_head(t2, low, first, scale=1.0):
    tf = t2.astype(F32) * scale
    return (jnp.where(low, tf, 0.0) if first else jnp.where(low, 0.0, tf)).astype(_MXU)


def _two_heads(t2, low, scale=1.0):
    return jnp.concatenate([_one_head(t2, low, True, scale), _one_head(t2, low, False, scale)], axis=0)


def _sb_fwd(proj, name, ex=None):
    S, W = proj.shape
    half = W // 6
    npair = half // PAIR
    T = _blk(S, SB_T)
    GP = _blk(npair, SB_PAIRS)
    GW = GP * PAIR
    nb = npair // GP

    def body(q_ref, k_ref, v_ref, o_ref, ox_ref):
        qi = pl.program_id(1)
        low = _low_lanes(T)
        q_tall = []
        for j in range(GP):
            q2 = q_ref[:, j * PAIR:(j + 1) * PAIR]
            q_tall.append(_two_heads(q2, low, HEAD_DIM ** -0.5))
        row = lax.broadcasted_iota(jnp.int32, (T, T), 0)
        col = lax.broadcasted_iota(jnp.int32, (T, T), 1)
        tri2 = _tri2(T, inclusive=False)

        def pairs(kb, carry, acc, fine, strict):
            rows = pl.ds(pl.multiple_of(kb * T, T), T)
            k2 = [k_ref[rows, j * PAIR:(j + 1) * PAIR] for j in range(GP)]
            v2 = [v_ref[rows, j * PAIR:(j + 1) * PAIR] for j in range(GP)]
            _, amps, carry = _sb_blocks(q_tall, k2, strict, tri2, carry)
            parts = [_split2(a) for a in amps]
            new_acc, new_fine = [], []
            for j in range(GP):
                tall = jnp.concatenate([parts[2 * j][0], parts[2 * j + 1][0], parts[2 * j][1], parts[2 * j + 1][1]],
                                       axis=0)
                hi0, hi1, mid0, mid1 = _per_head(_nn(tall, v2[j]), T)
                new_acc.append(acc[j] + jnp.where(low, hi0, hi1))
                new_fine.append(fine[j] + jnp.where(low, mid0, mid1))
            return tuple(carry), tuple(new_acc), tuple(new_fine)

        zero = (jnp.zeros((T, PAIR), F32),) * GP
        carry, acc, fine = pairs(qi, (jnp.zeros((T, T), F32),) * (2 * GP), zero, zero, col < row)

        def cond(st):
            kb, alive, _, _, _ = st
            return jnp.logical_and(kb >= 0, alive)

        def step(st):
            kb, _, carry, acc, fine = st
            carry, acc, fine = pairs(kb, carry, acc, fine, None)
            return kb - 1, _sb_alive(carry), carry, acc, fine

        _, _, _, acc, fine = lax.while_loop(cond, step, (qi - 1, _sb_alive(carry), carry, acc, fine))
        for j in range(GP):
            o_ref[:, j * PAIR:(j + 1) * PAIR] = acc[j].astype(o_ref.dtype)
            ox_ref[:, j * PAIR:(j + 1) * PAIR] = acc[j] + fine[j]

    blk = pl.BlockSpec((T, GW), lambda p, i: (i, p))
    return _call_hosted(
        body, name, (nb, S // T),
        [blk, pl.BlockSpec((S, GW), lambda p, i: (0, nb + p)), pl.BlockSpec((S, GW), lambda p, i: (0, 2 * nb + p))],
        [blk, blk], [jax.ShapeDtypeStruct((S, half), _MXU), jax.ShapeDtypeStruct((S, half), F32)],
        [], [proj, proj, proj], ex)


def _sb_bwd(proj, ox, dmixed, name, ex=None):
    S, W = proj.shape
    half = W // 6
    npair = half // PAIR
    T = _blk(S, SB_T)
    GP = _blk(npair, SB_PAIRS)
    GW = GP * PAIR
    nb = npair // GP
    last = S // T - 1
    scale = HEAD_DIM ** -0.5

    def body(q_ref, k_ref, v_ref, ox_ref, do_ref, dq_ref, dk_ref, dv_ref, dka_ref, dva_ref):
        qi = pl.program_id(1)

        @pl.when(qi == 0)
        def _():
            dka_ref[...] = jnp.zeros_like(dka_ref)
            dva_ref[...] = jnp.zeros_like(dva_ref)

        low = _low_lanes(T)
        q2, do2, q_tall, do_tall, deltas = [], [], [], [], []
        for j in range(GP):
            cols = slice(j * PAIR, (j + 1) * PAIR)
            q2.append(q_ref[:, cols])
            do2.append(do_ref[:, cols])
            q_tall.append(_two_heads(q2[j], low, scale))
            dobs = [_one_head(do2[j], low, True), _one_head(do2[j], low, False)]
            do_tall.append(jnp.concatenate(dobs, axis=0))
            for e in range(2):
                deltas.append(jnp.sum(dobs[e].astype(F32) * ox_ref[:, cols], axis=-1, keepdims=True))
        row = lax.broadcasted_iota(jnp.int32, (T, T), 0)
        col = lax.broadcasted_iota(jnp.int32, (T, T), 1)
        tri_ex = _tri2(T, inclusive=False)
        tri_in = _tri2(T, inclusive=True)

        def pairs(kb, carry, right, dq, strict):
            rows = pl.ds(pl.multiple_of(kb * T, T), T)
            k2 = [k_ref[rows, j * PAIR:(j + 1) * PAIR] for j in range(GP)]
            v2 = [v_ref[rows, j * PAIR:(j + 1) * PAIR] for j in range(GP)]
            nh = 2 * GP
            gs = []
            for j in range(GP):
                gs += _per_head(_nt(do_tall[j], v2[j]), T)
            lbs, amps, carry = _sb_blocks(q_tall, k2, strict, tri_ex, carry)
            ags = [a * gg for a, gg in zip(amps, gs)]
            sums = _tri_sums(ags, tri_in)
            dzbs = []
            for h in range(nh):
                left = deltas[h] - (sums[h][0] + right[h])
                beta = jnp.exp(lbs[h])
                dz = ags[h] - beta * (ags[h] + left)
                if strict is not None:
                    dz = jnp.where(strict, dz, 0.0)
                dzbs.append(dz.astype(_MXU))
            abs_ = [a.astype(_MXU) for a in amps]
            new_dq = []
            for j in range(GP):
                cols = slice(j * PAIR, (j + 1) * PAIR)
                dk0, dk1 = _per_head(_tn(jnp.concatenate(dzbs[2 * j:2 * j + 2], axis=1), q2[j]), T)
                dv0, dv1 = _per_head(_tn(jnp.concatenate(abs_[2 * j:2 * j + 2], axis=1), do2[j]), T)
                dq0, dq1 = _per_head(_nn(jnp.concatenate(dzbs[2 * j:2 * j + 2], axis=0), k2[j]), T)
                dka_ref[rows, cols] += jnp.where(low, dk0, dk1)
                dva_ref[rows, cols] += jnp.where(low, dv0, dv1)
                new_dq.append(dq[j] + jnp.where(low, dq0, dq1))
            right = tuple(right[h] + sums[h][1] for h in range(nh))
            return tuple(carry), right, tuple(new_dq)

        zero = (jnp.zeros((T, T), F32),) * (2 * GP)
        carry, right, dq = pairs(qi, zero, zero, (jnp.zeros((T, PAIR), F32),) * GP, col < row)

        def cond(st):
            kb, alive, _, _, _ = st
            return jnp.logical_and(kb >= 0, alive)

        def step(st):
            kb, _, carry, right, dq = st
            carry, right, dq = pairs(kb, carry, right, dq, None)
            return kb - 1, _sb_alive(carry), carry, right, dq

        _, _, _, _, dq = lax.while_loop(cond, step, (qi - 1, _sb_alive(carry), carry, right, dq))
        for j in range(GP):
            dq_ref[:, j * PAIR:(j + 1) * PAIR] = (dq[j] * scale).astype(dq_ref.dtype)

        @pl.when(qi == last)
        def _():
            dk_ref[...] = (dka_ref[...] * scale).astype(dk_ref.dtype)
            dv_ref[...] = dva_ref[...].astype(dv_ref.dtype)

    blk = pl.BlockSpec((T, GW), lambda p, i: (i, p))
    full = pl.BlockSpec((S, GW), lambda p, i: (0, p))
    return _call_hosted(
        body, name, (nb, S // T),
        [blk, pl.BlockSpec((S, GW), lambda p, i: (0, nb + p)), pl.BlockSpec((S, GW), lambda p, i: (0, 2 * nb + p)),
         blk, blk],
        [blk, full, full], [jax.ShapeDtypeStruct((S, half), _MXU)] * 3,
        [pltpu.VMEM((S, GW), F32), pltpu.VMEM((S, GW), F32)], [proj, proj, proj, ox, dmixed], ex)


def _pair_norm(t2, g2, low):
    tf = t2.astype(F32)
    sq = tf * tf
    both = jnp.sum(sq, axis=-1, keepdims=True)
    first = jnp.sum(jnp.where(low, sq, 0.0), axis=-1, keepdims=True)
    r = jnp.where(low, lax.rsqrt(first * (1.0 / HEAD_DIM) + EPS), lax.rsqrt((both - first) * (1.0 / HEAD_DIM) + EPS))
    hat = tf * r
    return hat * g2, hat, r


def _pair_norm_bwd(dn, hat, r, g2, low):
    dhat = dn * g2
    prod = dhat * hat
    both = jnp.sum(prod, axis=-1, keepdims=True)
    first = jnp.sum(jnp.where(low, prod, 0.0), axis=-1, keepdims=True)
    mean = jnp.where(low, first, both - first) * (1.0 / HEAD_DIM)
    return r * (dhat - hat * mean)


def _ca_fill(j, k_ref, v_ref, gk_ref, kn_ref, vp_ref):
    S = k_ref.shape[0]
    cols = slice(j * PAIR, (j + 1) * PAIR)
    kn, _, _ = _pair_norm(k_ref[:, cols], gk_ref[...], _low_lanes(S))
    kn_ref[j, 0:PAD, :] = jnp.zeros((PAD, PAIR), kn_ref.dtype)
    vp_ref[j, 0:PAD, :] = jnp.zeros((PAD, PAIR), vp_ref.dtype)
    kn_ref[j, PAD:PAD + S, :] = kn.astype(kn_ref.dtype)
    vp_ref[j, PAD:PAD + S, :] = v_ref[:, cols]


def _ca_scores(j, q_ref, b2_ref, gq_ref, kn_ref, qi, low):
    qn, qhat, r = _pair_norm(q_ref[:, j * PAIR:(j + 1) * PAIR], gq_ref[...], low)
    qn = qn * HEAD_DIM ** -0.5
    band = pl.ds(pl.multiple_of(qi * CA_T, CA_T), CA_W)
    key_pos = qi * CA_T - PAD + lax.broadcasted_iota(jnp.int32, (CA_T, CA_W), 1)
    both = _per_head(_nt(_two_heads(qn, low), kn_ref[j, band, :]), CA_T)
    scores = [jnp.where(key_pos >= 0, both[e] + b2_ref[2 * j + e], NEG) for e in range(2)]
    return scores, qn.astype(_MXU), qhat, r


def _softmax(s):
    e = jnp.exp(s - jnp.max(s, axis=-1, keepdims=True))
    return e * (1.0 / jnp.sum(e, axis=-1, keepdims=True))


def _ca_fwd(proj, bias2, gq2, gk2, name, ex=None):
    S, W = proj.shape
    half = W // 6
    npair = half // PAIR
    GP = _blk(npair, CA_PAIRS_FWD)
    GW = GP * PAIR
    nb = npair // GP

    def body(q_ref, k_ref, v_ref, b2_ref, gq_ref, gk_ref, o_ref, kn_ref, vp_ref):
        qi = pl.program_id(1)

        @pl.when(qi == 0)
        def _():
            for j in range(GP):
                _ca_fill(j, k_ref, v_ref, gk_ref, kn_ref, vp_ref)

        low = _low_lanes(CA_T)
        band = pl.ds(pl.multiple_of(qi * CA_T, CA_T), CA_W)
        scores = [_ca_scores(j, q_ref, b2_ref, gq_ref, kn_ref, qi, low)[0] for j in range(GP)]
        probs = [[_softmax(s).astype(_MXU) for s in pair] for pair in scores]
        for j in range(GP):
            outs = _per_head(_nn(jnp.concatenate(probs[j], axis=0), vp_ref[j, band, :]), CA_T)
            o_ref[:, j * PAIR:(j + 1) * PAIR] = jnp.where(low, outs[0], outs[1]).astype(o_ref.dtype)

    vec = pl.BlockSpec((1, PAIR), lambda p, i: (0, 0))
    return _call_hosted(
        body, name, (nb, S // CA_T),
        [pl.BlockSpec((CA_T, GW), lambda p, i: (i, 3 * nb + p)),
         pl.BlockSpec((S, GW), lambda p, i: (0, 4 * nb + p)), pl.BlockSpec((S, GW), lambda p, i: (0, 5 * nb + p)),
         pl.BlockSpec((2 * GP, CA_T, CA_W), lambda p, i: (p, 0, 0)), vec, vec],
        [pl.BlockSpec((CA_T, GW), lambda p, i: (i, p))], [jax.ShapeDtypeStruct((S, half), _MXU)],
        [pltpu.VMEM((GP, PAD + S, PAIR), _MXU), pltpu.VMEM((GP, PAD + S, PAIR), _MXU)],
        [proj, proj, proj, bias2, gq2, gk2], ex)


def _ca_bwd(proj, bias2, gq2, gk2, dmixed, name, ex=None):
    S, W = proj.shape
    half = W // 6
    npair = half // PAIR
    GP = _blk(npair, CA_PAIRS_BWD)
    GW = GP * PAIR
    nb = npair // GP
    scale = HEAD_DIM ** -0.5
    last = S // CA_T - 1

    def body(q_ref, k_ref, v_ref, b2_ref, gq_ref, gk_ref, do_ref,
             dq_ref, dk_ref, dv_ref, db_ref, dgq_ref, dgk_ref, kn_ref, vp_ref, dkn_ref, dvp_ref):
        p_id, qi = pl.program_id(0), pl.program_id(1)

        @pl.when(qi == 0)
        def _():
            for j in range(GP):
                _ca_fill(j, k_ref, v_ref, gk_ref, kn_ref, vp_ref)
            dkn_ref[...] = jnp.zeros_like(dkn_ref)
            dvp_ref[...] = jnp.zeros_like(dvp_ref)
            db_ref[...] = jnp.zeros_like(db_ref)

        @pl.when(jnp.logical_and(p_id == 0, qi == 0))
        def _():
            dgq_ref[...] = jnp.zeros_like(dgq_ref)
            dgk_ref[...] = jnp.zeros_like(dgk_ref)

        low = _low_lanes(CA_T)
        top_w = lax.broadcasted_iota(jnp.int32, (PAIR, CA_W), 0) < HEAD_DIM
        band = pl.ds(pl.multiple_of(qi * CA_T, CA_T), CA_W)
        pairs = [_ca_scores(j, q_ref, b2_ref, gq_ref, kn_ref, qi, low) for j in range(GP)]
        do2 = [do_ref[:, j * PAIR:(j + 1) * PAIR] for j in range(GP)]
        dps = [_per_head(_nt(_two_heads(do2[j], low), vp_ref[j, band, :]), CA_T) for j in range(GP)]
        probs, dsbs = [], []
        for j in range(GP):
            pj, dj = [], []
            for e in range(2):
                p = _softmax(pairs[j][0][e])
                ds = p * (dps[j][e] - jnp.sum(p * dps[j][e], axis=-1, keepdims=True))
                db_ref[2 * j + e] += ds
                pj.append(p.astype(_MXU))
                dj.append(ds.astype(_MXU))
            probs.append(pj)
            dsbs.append(dj)
        dgq = jnp.zeros((1, PAIR), F32)
        for j in range(GP):
            _, qn, qhat, r = pairs[j]
            dq_h = _per_head(_nn(jnp.concatenate(dsbs[j], axis=0), kn_ref[j, band, :]), CA_T)
            dk_t = _tn(qn, jnp.concatenate(dsbs[j], axis=1))
            dv_t = _tn(do2[j], jnp.concatenate(probs[j], axis=1))
            dkn_ref[j, :, band] += jnp.where(top_w, dk_t[:, :CA_W], dk_t[:, CA_W:])
            dvp_ref[j, :, band] += jnp.where(top_w, dv_t[:, :CA_W], dv_t[:, CA_W:])
            dqn = jnp.where(low, dq_h[0], dq_h[1]) * scale
            dgq = dgq + jnp.sum(dqn * qhat, axis=0, keepdims=True)
            dq_ref[:, j * PAIR:(j + 1) * PAIR] = _pair_norm_bwd(dqn, qhat, r, gq_ref[...], low).astype(dq_ref.dtype)
        dgq_ref[...] += dgq

        @pl.when(qi == last)
        def _():
            low_s = _low_lanes(S)
            for j in range(GP):
                cols = slice(j * PAIR, (j + 1) * PAIR)
                _, khat, rk = _pair_norm(k_ref[:, cols], gk_ref[...], low_s)
                dkn = dkn_ref[j, :, PAD:PAD + S].T
                dgk_ref[...] += jnp.sum(dkn * khat, axis=0, keepdims=True)
                dk_ref[:, cols] = _pair_norm_bwd(dkn, khat, rk, gk_ref[...], low_s).astype(dk_ref.dtype)
                dv_ref[:, cols] = dvp_ref[j, :, PAD:PAD + S].T.astype(dv_ref.dtype)

    vec = pl.BlockSpec((1, PAIR), lambda p, i: (0, 0))
    tile = pl.BlockSpec((2 * GP, CA_T, CA_W), lambda p, i: (p, 0, 0))
    full = pl.BlockSpec((S, GW), lambda p, i: (0, p))
    return _call_hosted(
        body, name, (nb, S // CA_T),
        [pl.BlockSpec((CA_T, GW), lambda p, i: (i, 3 * nb + p)),
         pl.BlockSpec((S, GW), lambda p, i: (0, 4 * nb + p)), pl.BlockSpec((S, GW), lambda p, i: (0, 5 * nb + p)),
         tile, vec, vec, pl.BlockSpec((CA_T, GW), lambda p, i: (i, nb + p))],
        [pl.BlockSpec((CA_T, GW), lambda p, i: (i, p)), full, full, tile, vec, vec],
        [jax.ShapeDtypeStruct((S, half), _MXU)] * 3
        + [jax.ShapeDtypeStruct(bias2.shape, F32), jax.ShapeDtypeStruct((1, PAIR), F32),
           jax.ShapeDtypeStruct((1, PAIR), F32)],
        [pltpu.VMEM((GP, PAD + S, PAIR), _MXU), pltpu.VMEM((GP, PAD + S, PAIR), _MXU),
         pltpu.VMEM((GP, PAIR, PAD + S), F32), pltpu.VMEM((GP, PAIR, PAD + S), F32)],
        [proj, proj, proj, bias2, gq2, gk2, dmixed], ex)


def _pack_small(parts):
    flat = jnp.concatenate([p.reshape(-1) for layer in parts for p in layer])
    n = flat.shape[0]
    n_pad = -(-n // 1024) * 1024
    return jnp.pad(flat, (0, n_pad - n)).reshape(1, n_pad)


def _unpack_small(flat, shapes):
    out, off = [], 0
    for layer in shapes:
        cur = []
        for shp in layer:
            size = 1
            for s in shp:
                size *= s
            cur.append(flat[off:off + size].reshape(shp))
            off += size
        out.append(cur)
    return out


def kernel(x, c, g_norm1, w_in, g_q, g_k, rel_bias, w_o, g_norm2, w1, w2, w_ada, b_ada, loss_target, m_g_norm1, m_w_in, m_g_q, m_g_k, m_rel_bias, m_w_o, m_g_norm2, m_w1, m_w2, m_w_ada, m_b_ada, v_g_norm1, v_w_in, v_g_q, v_g_k, v_rel_bias, v_w_o, v_g_norm2, v_w1, v_w2, v_w_ada, v_b_ada):
    L = w_in.shape[0]
    S, D = x.shape[1:]
    H2 = D // HEAD_DIM // 2
    Ca = w_ada.shape[2]
    xi, yi, ci = _pos()
    me = 4 * xi + 2 * yi + ci
    place = jnp.stack([2 * xi + yi, ci]).astype(jnp.int32)

    c_all = _all_gather_small(c, "ag_c").reshape(NDEV, D)
    b_cols = lax.dynamic_slice(b_ada, (0, me * Ca), (L, Ca))
    mod_part = _mod_partial(c_all, w_ada, b_cols, "mod_partial")
    mod_all = _all_gather_small(mod_part, "ag_mod")
    mod = lax.dynamic_index_in_dim(mod_all, me, axis=1, keepdims=False)
    mod = mod.reshape(NDEV, L, Ca).transpose(1, 0, 2).reshape(L, 6, 1, D)

    wire = lambda a: a.astype(_MXU)
    by_cols = lambda g: g.transpose(1, 0, 2).reshape(D, g.shape[0] * g.shape[2])
    W_in = {0: by_cols(_run_exchange(_gather_exchange([wire(w_in[0])]), "ag_w_in0")[0])}
    W_o, W_1, W_2 = {}, {}, {}

    xs = [x[0]]
    saved = []
    for l in range(L):
        sh1, sc1, gt1, sh2, sc2, gt2 = [mod[l, i] for i in range(6)]
        gn1, gn2 = g_norm1[l:l + 1], g_norm2[l:l + 1]
        gq2, gk2 = jnp.tile(g_q[l:l + 1], (1, 2)), jnp.tile(g_k[l:l + 1], (1, 2))
        proj, h1 = _ln_mod_matmul(xs[-1], gn1, sc1, sh1, W_in[l], f"l{l}_proj")
        (o_sb, ox_sb), got = _sb_fwd(proj, f"l{l}_sb_fwd", _gather_exchange([wire(w1[l]), wire(w2[l])]))
        W_1[l], W_2[l] = by_cols(got[0]), got[1].reshape(4 * D, D)
        bias2 = _ca_bias(rel_bias[l], f"l{l}_ca_bias")
        nxt = [wire(w_in[l + 1])] if l + 1 < L else []
        (o_ca,), got = _ca_fwd(proj, bias2, gq2, gk2, f"l{l}_ca_fwd", _gather_exchange([wire(w_o[l])] + nxt))
        W_o[l] = got[0].reshape(D, D)
        if nxt:
            W_in[l + 1] = by_cols(got[1])
        mixed = jnp.concatenate([o_sb, o_ca], axis=1)
        x1, f1 = _matmul_res_gate(mixed, W_o[l], xs[-1], gt1, False, f"l{l}_attn_out")
        u, h2 = _ln_mod_matmul(x1, gn2, sc2, sh2, W_1[l], f"l{l}_mlp_in")
        x2, f2 = _matmul_res_gate(u, W_2[l], x1, gt2, True, f"l{l}_mlp_out")
        saved.append(dict(x0=xs[-1], h1=h1, proj=proj, ox_sb=ox_sb, bias2=bias2, mixed=mixed, f1=f1, x1=x1,
                          h2=h2, u=u, f2=f2))
        xs.append(x2)

    dx, loss_part = _loss_grad(xs[-1], loss_target[0], "loss")

    owns, recv_b = {}, {}
    ready = []
    small_parts = [None] * L

    def partials(keys, grads, recv_a):
        parts = []
        for key, g, r in zip(keys, grads, recv_a):
            owns[key], part = _rs_chip_partial(place, g, r, f"rs_partial_l{key[0]}_{key[1]}")
            parts.append(part)
        return parts

    for l in reversed(range(L)):
        sv = saved[l]
        sh1, sc1, gt1, sh2, sc2, gt2 = [mod[l, i] for i in range(6)]
        gn1, gn2 = g_norm1[l:l + 1], g_norm2[l:l + 1]
        gq2, gk2 = jnp.tile(g_q[l:l + 1], (1, 2)), jnp.tile(g_k[l:l + 1], (1, 2))
        dz2, dgt2, du = _gate_nt_matmul(dx, sv["f2"], gt2, W_2[l], sv["u"], f"l{l}_mlp_out_bwd")
        gw2 = _tn_matmul(sv["u"], dz2, False, True, f"l{l}_gw2")
        gw1 = _tn_matmul(sv["h2"], du, True, False, f"l{l}_gw1")
        dx, dsh2, dsc2, dgn2 = _nt_ln_bwd(du, W_1[l], sv["x1"], gn2, sc2, sh2, dx, f"l{l}_mlp_in_bwd")
        dz1, dgt1, dmixed = _gate_nt_matmul(dx, sv["f1"], gt1, W_o[l], None, f"l{l}_attn_out_bwd")
        gwo = _tn_matmul(sv["mixed"], dz1, False, False, f"l{l}_gwo")
        ready += [((l, 1), gwo), ((l, 2), gw1), ((l, 3), gw2)]
        keys, grads = [k for k, _ in ready], [g for _, g in ready]
        (dq_sb, dk_sb, dv_sb), recv_a = _sb_bwd(sv["proj"], sv["ox_sb"], dmixed, f"l{l}_sb_bwd",
                                                _sibling_exchange(grads))
        parts = partials(keys, grads, recv_a)
        (dq_ca, dk_ca, dv_ca, dbias2, dgq2, dgk2), got = _ca_bwd(sv["proj"], sv["bias2"], gq2, gk2, dmixed,
                                                                 f"l{l}_ca_bwd", _chip_exchange(parts))
        recv_b.update(zip(keys, got))
        dgq = dgq2[:, :HEAD_DIM] + dgq2[:, HEAD_DIM:]
        dgk = dgk2[:, :HEAD_DIM] + dgk2[:, HEAD_DIM:]
        drb = _ca_bias_bwd(dbias2, f"l{l}_ca_bias_bwd")
        dproj = jnp.concatenate([dq_sb, dk_sb, dv_sb, dq_ca, dk_ca, dv_ca], axis=1)
        gwin = _tn_matmul(sv["h1"], dproj, True, False, f"l{l}_gwin")
        ready = [((l, 0), gwin)]
        dx, dsh1, dsc1, dgn1 = _nt_ln_bwd(dproj, W_in[l], sv["x0"], gn1, sc1, sh1, dx, f"l{l}_proj_bwd")
        dmod = jnp.concatenate([dsh1, dsc1, dgt1, dsh2, dsc2, dgt2], axis=1)
        small_parts[l] = [dgn1, dgq, dgk, drb, dgn2, dmod]
    grad_x = dx[None]

    keys, grads = [k for k, _ in ready], [g for _, g in ready]
    parts = partials(keys, grads, _run_exchange(_sibling_exchange(grads), "rs_sibling_last"))
    recv_b.update(zip(keys, _run_exchange(_chip_exchange(parts), "rs_chips_last")))
    big_out = []
    for t, (w, m, v) in enumerate([(w_in, m_w_in, v_w_in), (w_o, m_w_o, v_w_o), (w1, m_w1, v_w1), (w2, m_w2, v_w2)]):
        big_out.append(_rs_sum_adamw([owns[(l, t)] for l in range(L)], [recv_b[(l, t)] for l in range(L)],
                                     w, m, v, f"adamw_big_{t}"))

    packed = _pack_small(small_parts)
    gathered_small = _all_gather_small(packed, "ag_small_grads")
    small_sum = _sum_devices(gathered_small, "sum_small_grads")
    shapes = [[(1, D), (1, HEAD_DIM), (1, HEAD_DIM), (H2, N_REL), (1, D), (1, 6 * D)]] * L
    names = ["g_norm1", "g_q", "g_k", "rel_bias", "g_norm2", "b_ada"]
    small_w = {"g_norm1": (g_norm1, m_g_norm1, v_g_norm1), "g_q": (g_q, m_g_q, v_g_q), "g_k": (g_k, m_g_k, v_g_k),
               "rel_bias": (rel_bias, m_rel_bias, v_rel_bias), "g_norm2": (g_norm2, m_g_norm2, v_g_norm2),
               "b_ada": (b_ada, m_b_ada, v_b_ada)}
    packs = [_pack_small([[small_w[n][k][l] for n in names] for l in range(L)]) for k in range(3)]
    n_pad = packed.shape[1]
    as_rows = lambda a: a.reshape(n_pad // 128, 128)
    sd, sm, sv_ = _adamw(as_rows(packs[0]), as_rows(small_sum), as_rows(packs[1]), as_rows(packs[2]), "adamw_small")
    small_out = {}
    for key, flat in [("grad", small_sum), ("delta", sd), ("m", sm), ("v", sv_)]:
        per_layer = _unpack_small(flat.reshape(-1), shapes)
        for i, n in enumerate(names):
            small_out[(key, n)] = jnp.stack([per_layer[l][i].reshape(small_w[n][0].shape[1:]) for l in range(L)])

    layer_len = 2 * D + 2 * HEAD_DIM + H2 * N_REL + 6 * D
    rows = gathered_small.reshape(NDEV, n_pad)
    dmod_all = jnp.stack([rows[:, l * layer_len + layer_len - 6 * D:(l + 1) * layer_len] for l in range(L)])
    dmod_cols = lax.dynamic_slice(dmod_all, (0, 0, me * Ca), (L, NDEV, Ca))
    dmod_cols = jnp.pad(dmod_cols, ((0, 0), (0, 128 - NDEV), (0, 0)))
    c_t = jnp.pad(c_all.T, ((0, 0), (0, 128 - NDEV)))
    g_ada = _w_ada_grad(c_t, dmod_cols, "w_ada_grad")
    flat2 = lambda a: a.reshape(L * D, Ca)
    ad, am, av = _adamw(flat2(w_ada), flat2(g_ada), flat2(m_w_ada), flat2(v_w_ada), "adamw_w_ada")
    ada_out = [g_ada] + [a.reshape(L, D, Ca) for a in (ad, am, av)]

    def leaf(kind):
        k = {"grad": 0, "delta": 1, "m": 2, "v": 3}[kind]
        return [small_out[(kind, "g_norm1")], big_out[0][k], small_out[(kind, "g_q")], small_out[(kind, "g_k")],
                small_out[(kind, "rel_bias")], big_out[1][k], small_out[(kind, "g_norm2")], big_out[2][k],
                big_out[3][k], ada_out[k], small_out[(kind, "b_ada")]]

    loss = lax.psum(loss_part[0, 0], ("x", "y", "c"))
    return (loss, grad_x, *leaf("grad"), *leaf("delta"), *leaf("m"), *leaf("v"))
```

```python
import functools

import jax
import jax.numpy as jnp
from jax import lax
from jax.experimental import pallas as pl
from jax.experimental.pallas import tpu as pltpu

F32 = jnp.float32
_MXU = jnp.bfloat16

HEAD_DIM = 64
CHUNK = 64
LEFT_CHUNKS = 8
PAD = LEFT_CHUNKS * CHUNK
BAND = PAD + CHUNK
REL_CLIP = 128
N_REL = 2 * REL_CLIP + 1
EPS = 1e-6
NEG = -1e30
NDEV = 8
SB_T = 128
CA_T = 2 * CHUNK
CA_W = CA_T + PAD
SB_SKIP = -104.0
PAIR = 2 * HEAD_DIM
SB_PAIRS = 4
CA_PAIRS_FWD = 4
CA_PAIRS_BWD = 2
ROW_BLOCK = 512
SKEW_W = CA_W + CA_T - 1

ADAM_LR, ADAM_B1, ADAM_B2, ADAM_EPS, ADAM_WD, ADAM_STEP = 0.001, 0.9, 0.999, 1e-08, 0.01, 10

MESH = pl.DeviceIdType.MESH
VMEM_SPEC = pl.BlockSpec(memory_space=pltpu.VMEM)
ANY_SPEC = pl.BlockSpec(memory_space=pl.ANY)


def _nn(a, b):
    return lax.dot_general(a, b, (((1,), (0,)), ((), ())), preferred_element_type=F32)


def _nt(a, b):
    return lax.dot_general(a, b, (((1,), (1,)), ((), ())), preferred_element_type=F32)


def _tn(a, b):
    return lax.dot_general(a, b, (((0,), (0,)), ((), ())), preferred_element_type=F32)


def _blk(n, pref):
    return pref if n % pref == 0 else n


def _pos():
    return lax.axis_index("x"), lax.axis_index("y"), lax.axis_index("c")


def _flip(v, bit):
    return 1 - v if bit else v


def _gather_small(x_ref, out_ref, send_sems, recv_sems):
    R, C = x_ref.shape
    x, y, c = _pos()
    me = 4 * x + 2 * y + c

    def peer(k):
        return (_flip(x, k & 4), _flip(y, k & 2), _flip(c, k & 1))

    def copy(k, slot):
        return pltpu.make_async_remote_copy(
            src_ref=x_ref, dst_ref=out_ref.at[slot], send_sem=send_sems.at[k - 1],
            recv_sem=recv_sems.at[k - 1], device_id=peer(k), device_id_type=MESH)

    out_ref[pl.ds(me, 1), :, :] = x_ref[...].reshape(1, R, C)
    sends = [copy(k, me) for k in range(1, NDEV)]
    for cp in sends:
        cp.start()
    for k in range(1, NDEV):
        px, py, pc = peer(k)
        copy(k, 4 * px + 2 * py + pc).wait_recv()
    for cp in sends:
        cp.wait_send()


def _all_gather_small(blk, name):
    return pl.pallas_call(
        lambda x_ref, out_ref, send_sems, recv_sems: _gather_small(x_ref, out_ref, send_sems, recv_sems), name=name,
        out_shape=jax.ShapeDtypeStruct((NDEV,) + blk.shape, blk.dtype),
        in_specs=[VMEM_SPEC], out_specs=VMEM_SPEC,
        scratch_shapes=[pltpu.SemaphoreType.DMA((NDEV - 1,)), pltpu.SemaphoreType.DMA((NDEV - 1,))],
    )(blk)


class _Exchange:
    def __init__(self, inputs, out_shapes, sems, start, finish, middle=None):
        self.inputs, self.out_shapes, self.sems = list(inputs), list(out_shapes), list(sems)
        self.start, self.middle, self.finish = start, middle, finish


def _run_exchange(ex, name):
    n_in, n_out = len(ex.inputs), len(ex.out_shapes)

    def body(*refs):
        ins, outs, sems = refs[:n_in], refs[n_in:n_in + n_out], refs[n_in + n_out:]
        ex.start(ins, outs, sems)
        if ex.middle is not None:
            ex.middle(ins, outs, sems)
        ex.finish(ins, outs, sems)

    return pl.pallas_call(
        body, name=name, out_shape=ex.out_shapes, in_specs=[ANY_SPEC] * n_in, out_specs=[ANY_SPEC] * n_out,
        scratch_shapes=ex.sems,
    )(*ex.inputs)


def _hosted(body, n_in, n_out, ex, step, steps):
    if ex is None:
        return body
    xi, xo = len(ex.inputs), len(ex.out_shapes)

    def wrapped(*refs):
        own_in, ex_in = refs[:n_in], refs[n_in:n_in + xi]
        rest = refs[n_in + xi:]
        own_out, ex_out = rest[:n_out], rest[n_out:n_out + xo]
        rest = rest[n_out + xo:]
        own_scratch, ex_sems = rest[:len(rest) - len(ex.sems)], rest[len(rest) - len(ex.sems):]
        t = step()
        pl.when(t == 0)(lambda: ex.start(ex_in, ex_out, ex_sems))
        body(*own_in, *own_out, *own_scratch)
        if ex.middle is not None:
            pl.when(t == (steps * 7) // 8)(lambda: ex.middle(ex_in, ex_out, ex_sems))
        pl.when(t == steps - 1)(lambda: ex.finish(ex_in, ex_out, ex_sems))

    return wrapped


def _call_hosted(body, name, grid, in_specs, out_specs, out_shape, scratch, args, ex):
    n_in, n_out = len(in_specs), len(out_specs)
    steps = 1
    for extent in grid:
        steps *= extent

    def step():
        t = pl.program_id(0)
        for axis in range(1, len(grid)):
            t = t * grid[axis] + pl.program_id(axis)
        return t

    if ex is not None:
        in_specs = in_specs + [ANY_SPEC] * len(ex.inputs)
        out_specs = out_specs + [ANY_SPEC] * len(ex.out_shapes)
        out_shape = out_shape + ex.out_shapes
        scratch = scratch + ex.sems
        args = args + ex.inputs
    outs = pl.pallas_call(
        _hosted(body, n_in, n_out, ex, step, steps), name=name, grid=grid, in_specs=in_specs, out_specs=out_specs,
        out_shape=out_shape, scratch_shapes=scratch,
    )(*args)
    return list(outs[:n_out]), list(outs[n_out:])


def _gather_exchange(shards):
    n = len(shards)

    def setup(ins, outs, sems):
        send_sems, recv_sems, local_sems = sems
        x, y, c = _pos()
        me, sibling = (x, y, c), (x, y, 1 - c)
        chips = [(1 - x, y), (x, 1 - y), (1 - x, 1 - y)]

        def copy(i, k, block, to, src=None):
            px, py, pc = block
            dst = outs[i].at[4 * px + 2 * py + pc]
            return pltpu.make_async_remote_copy(
                src_ref=dst if src is None else src, dst_ref=dst, send_sem=send_sems.at[7 * i + k],
                recv_sem=recv_sems.at[7 * i + k], device_id=to, device_id_type=MESH)

        def mine(i):
            return pltpu.make_async_copy(ins[i], outs[i].at[4 * x + 2 * y + c], local_sems.at[i])

        def first(i):
            return [copy(i, 0, me, sibling, src=ins[i])] + [
                copy(i, 1 + j, me, (*chip, c), src=ins[i]) for j, chip in enumerate(chips)]

        def passed(i, j):
            return copy(i, 4 + j, (*chips[j], c), sibling)

        return me, sibling, chips, c, copy, mine, first, passed

    def start(ins, outs, sems):
        _, _, _, _, _, mine, first, _ = setup(ins, outs, sems)
        for i in range(n):
            mine(i).start()
            for cp in first(i):
                cp.start()

    def middle(ins, outs, sems):
        me, _, chips, c, copy, _, _, passed = setup(ins, outs, sems)
        for j, chip in enumerate(chips):
            for i in range(n):
                copy(i, 1 + j, (*chip, c), me).wait_recv()
                passed(i, j).start()

    def finish(ins, outs, sems):
        me, sibling, chips, c, copy, mine, first, passed = setup(ins, outs, sems)
        for i in range(n):
            copy(i, 0, sibling, me).wait_recv()
            for j, chip in enumerate(chips):
                copy(i, 4 + j, (*chip, 1 - c), me).wait_recv()
        for i in range(n):
            for cp in first(i) + [passed(i, j) for j in range(3)]:
                cp.wait_send()
            mine(i).wait()

    return _Exchange(
        shards, [jax.ShapeDtypeStruct((NDEV,) + s.shape, s.dtype) for s in shards],
        [pltpu.SemaphoreType.DMA((7 * n,)), pltpu.SemaphoreType.DMA((7 * n,)), pltpu.SemaphoreType.DMA((n,))],
        start, finish, middle)


def _sibling_exchange(grads):
    n = len(grads)

    def copies(ins, outs, sems):
        send_sems, recv_sems = sems
        x, y, c = _pos()
        return [pltpu.make_async_remote_copy(
            src_ref=ins[i].at[2 * q + (1 - c)], dst_ref=outs[i].at[q], send_sem=send_sems.at[4 * i + q],
            recv_sem=recv_sems.at[4 * i + q], device_id=(x, y, 1 - c), device_id_type=MESH)
            for i in range(n) for q in range(4)]

    def start(ins, outs, sems):
        for cp in copies(ins, outs, sems):
            cp.start()

    def finish(ins, outs, sems):
        for cp in copies(ins, outs, sems):
            cp.wait()

    return _Exchange(
        grads, [jax.ShapeDtypeStruct((4,) + g.shape[1:], g.dtype) for g in grads],
        [pltpu.SemaphoreType.DMA((4 * n,)), pltpu.SemaphoreType.DMA((4 * n,))], start, finish)


def _chip_exchange(parts):
    n = len(parts)

    def copies(ins, outs, sems):
        send_sems, recv_sems = sems
        x, y, c = _pos()
        return [pltpu.make_async_remote_copy(
            src_ref=ins[i].at[j - 1], dst_ref=outs[i].at[j - 1], send_sem=send_sems.at[3 * i + j - 1],
            recv_sem=recv_sems.at[3 * i + j - 1], device_id=(_flip(x, j & 2), _flip(y, j & 1), c),
            device_id_type=MESH) for i in range(n) for j in range(1, 4)]

    def start(ins, outs, sems):
        for cp in copies(ins, outs, sems):
            cp.start()

    def finish(ins, outs, sems):
        for cp in copies(ins, outs, sems):
            cp.wait()

    return _Exchange(
        parts, [jax.ShapeDtypeStruct(p.shape, p.dtype) for p in parts],
        [pltpu.SemaphoreType.DMA((3 * n,)), pltpu.SemaphoreType.DMA((3 * n,))], start, finish)


def _rs_chip_partial(place, grad, recv, name):
    _, R, C = grad.shape
    tr = _blk(R, 256)

    def body(place_ref, *refs):
        g_refs, r_refs = refs[:4], refs[4:8]
        own_ref, out_ref = refs[8:]
        own_ref[...] = g_refs[0][0] + r_refs[0][0]
        for j in range(1, 4):
            out_ref[j - 1] = (g_refs[j][0] + r_refs[j][0]).astype(out_ref.dtype)

    def g_map(j):
        return lambda i, p: (2 * jnp.bitwise_xor(p[0], j) + p[1], i, 0)

    def r_map(j):
        return lambda i, p: (jnp.bitwise_xor(p[0], j), i, 0)

    grid_spec = pltpu.PrefetchScalarGridSpec(
        num_scalar_prefetch=1, grid=(R // tr,),
        in_specs=[pl.BlockSpec((1, tr, C), g_map(j)) for j in range(4)]
        + [pl.BlockSpec((1, tr, C), r_map(j)) for j in range(4)],
        out_specs=[pl.BlockSpec((tr, C), lambda i, p: (i, 0)), pl.BlockSpec((3, tr, C), lambda i, p: (0, i, 0))])
    return pl.pallas_call(
        body, name=name, grid_spec=grid_spec,
        out_shape=[jax.ShapeDtypeStruct((R, C), F32), jax.ShapeDtypeStruct((3, R, C), _MXU)],
    )(place, *([grad] * 4), *([recv] * 4))


def _adamw_math(w, g, m, v):
    m = ADAM_B1 * m + (1.0 - ADAM_B1) * g
    v = ADAM_B2 * v + (1.0 - ADAM_B2) * (g * g)
    m_hat = m / (1.0 - ADAM_B1 ** ADAM_STEP)
    v_hat = v / (1.0 - ADAM_B2 ** ADAM_STEP)
    delta = -ADAM_LR * (m_hat / (jnp.sqrt(v_hat) + ADAM_EPS) + ADAM_WD * w)
    return delta, m, v


def _adamw(w, g, m, v, name):
    R, C = w.shape
    tr = _blk(R, 256)

    def body(w_ref, g_ref, m_ref, v_ref, d_ref, nm_ref, nv_ref):
        d, nm, nv = _adamw_math(w_ref[...], g_ref[...], m_ref[...], v_ref[...])
        d_ref[...] = d
        nm_ref[...] = nm
        nv_ref[...] = nv

    spec = pl.BlockSpec((tr, C), lambda i: (i, 0))
    return pl.pallas_call(
        body, name=name, grid=(R // tr,), in_specs=[spec] * 4, out_specs=[spec] * 3,
        out_shape=[jax.ShapeDtypeStruct((R, C), F32)] * 3,
    )(w, g, m, v)


def _rs_sum_adamw(owns, recvs, w, m, v, name):
    L, R, C = w.shape
    tr = _blk(R, 256)
    nr = R // tr

    def body(o0, o1, r0, r1, w_ref, m_ref, v_ref, g_ref, d_ref, nm_ref, nv_ref):
        def step(o_ref, r_ref):
            g = o_ref[...]
            for j in range(3):
                g = g + r_ref[j].astype(F32)
            d, nm, nv = _adamw_math(w_ref[0], g, m_ref[0], v_ref[0])
            g_ref[0] = g
            d_ref[0] = d
            nm_ref[0] = nm
            nv_ref[0] = nv

        pl.when(pl.program_id(0) == 0)(lambda: step(o0, r0))
        pl.when(pl.program_id(0) == 1)(lambda: step(o1, r1))

    def hold(layer):
        if layer == 0:
            return lambda l, i: i * (1 - l) + (nr - 1) * l
        return lambda l, i: i * l

    own_spec = [pl.BlockSpec((tr, C), functools.partial(lambda l, i, f: (f(l, i), 0), f=hold(k))) for k in range(2)]
    recv_spec = [pl.BlockSpec((3, tr, C), functools.partial(lambda l, i, f: (0, f(l, i), 0), f=hold(k)))
                 for k in range(2)]
    lay = pl.BlockSpec((1, tr, C), lambda l, i: (l, i, 0))
    return pl.pallas_call(
        body, name=name, grid=(L, nr),
        in_specs=own_spec + recv_spec + [lay] * 3, out_specs=[lay] * 4,
        out_shape=[jax.ShapeDtypeStruct((L, R, C), F32)] * 4,
    )(owns[0], owns[1], recvs[0], recvs[1], w, m, v)


def _silu(x):
    return x / (1.0 + jnp.exp(-x))


def _start(c, w_ada, b_cols, w_first, name):
    L, D, Ca = w_ada.shape
    ex = _gather_exchange([w_first])

    def body(c_ref, w_ref, b_ref, first_ref, call_ref, mod_ref, gathered_ref, part_ref, c_send, c_recv, m_send,
             m_recv, *ex_sems):
        ex.start([first_ref], [gathered_ref], ex_sems)
        _gather_small(c_ref, call_ref, c_send, c_recv)
        act = _silu(call_ref[...].reshape(NDEV, D)).astype(_MXU)
        for l in range(L):
            part_ref[:, l * Ca:(l + 1) * Ca] = _nn(act, w_ref[l].astype(_MXU)) + b_ref[l:l + 1, :]
        _gather_small(part_ref, mod_ref, m_send, m_recv)
        ex.middle([first_ref], [gathered_ref], ex_sems)
        ex.finish([first_ref], [gathered_ref], ex_sems)

    pairs = [pltpu.SemaphoreType.DMA((NDEV - 1,))] * 4
    return pl.pallas_call(
        body, name=name,
        out_shape=[jax.ShapeDtypeStruct((NDEV, 1, D), F32), jax.ShapeDtypeStruct((NDEV, NDEV, L * Ca), F32)]
        + ex.out_shapes,
        in_specs=[VMEM_SPEC] * 3 + [ANY_SPEC], out_specs=[VMEM_SPEC, VMEM_SPEC, ANY_SPEC],
        scratch_shapes=[pltpu.VMEM((NDEV, L * Ca), F32)] + pairs + ex.sems,
    )(c, w_ada, b_cols, w_first)


def _w_ada_grad(c_t, dmod_cols, name):
    L, _, Ca = dmod_cols.shape
    D = c_t.shape[0]

    def body(c_ref, d_ref, o_ref):
        act = _silu(c_ref[...]).astype(_MXU)
        for l in range(L):
            o_ref[l] = _nn(act, d_ref[l].astype(_MXU))

    return pl.pallas_call(
        body, name=name, out_shape=jax.ShapeDtypeStruct((L, D, Ca), F32),
        in_specs=[VMEM_SPEC] * 2, out_specs=VMEM_SPEC,
    )(c_t, dmod_cols)


def _sum_devices(gathered, name):
    _, _, N = gathered.shape

    def body(x_ref, o_ref):
        acc = x_ref[0]
        for d in range(1, NDEV):
            acc = acc + x_ref[d]
        o_ref[...] = acc

    return pl.pallas_call(
        body, name=name, out_shape=jax.ShapeDtypeStruct((1, N), F32),
        in_specs=[VMEM_SPEC], out_specs=VMEM_SPEC,
    )(gathered)


def _ln_mod_matmul(x, g, sc, sh, w, name):
    S, D = x.shape
    N = w.shape[1]
    tm = _blk(S, ROW_BLOCK)

    def body(x_ref, g_ref, sc_ref, sh_ref, w_ref, o_ref, h_ref):
        xv = x_ref[...]
        r = lax.rsqrt(jnp.mean(xv * xv, axis=-1, keepdims=True) + EPS)
        hv = ((xv * r) * g_ref[...]) * (1.0 + sc_ref[...]) + sh_ref[...]
        hb = hv.astype(_MXU)
        h_ref[...] = hb
        o_ref[...] = _nn(hb, w_ref[...]).astype(o_ref.dtype)

    vec = pl.BlockSpec((1, D), lambda i: (0, 0))
    row = lambda width: pl.BlockSpec((tm, width), lambda i: (i, 0))
    return pl.pallas_call(
        body, name=name, grid=(S // tm,),
        in_specs=[row(D), vec, vec, vec, pl.BlockSpec((D, N), lambda i: (0, 0))],
        out_specs=[row(N), row(D)],
        out_shape=[jax.ShapeDtypeStruct((S, N), _MXU), jax.ShapeDtypeStruct((S, D), _MXU)],
    )(x, g, sc, sh, w)


def _matmul_res_gate(a, w, xres, gt, relu2, name):
    S, K = a.shape
    N = w.shape[1]
    tm = _blk(S, 512)

    def body(a_ref, w_ref, x_ref, gt_ref, o_ref, f_ref):
        av = a_ref[...]
        if relu2:
            af = jnp.maximum(av.astype(F32), 0.0)
            av = (af * af).astype(_MXU)
        f = _nn(av, w_ref[...])
        f_ref[...] = f.astype(f_ref.dtype)
        o_ref[...] = x_ref[...] + gt_ref[...] * f

    row = lambda width: pl.BlockSpec((tm, width), lambda i: (i, 0))
    return pl.pallas_call(
        body, name=name, grid=(S // tm,),
        in_specs=[row(K), pl.BlockSpec((K, N), lambda i: (0, 0)), row(N), pl.BlockSpec((1, N), lambda i: (0, 0))],
        out_specs=[row(N), row(N)],
        out_shape=[jax.ShapeDtypeStruct((S, N), F32), jax.ShapeDtypeStruct((S, N), _MXU)],
    )(a, w, xres, gt)


def _loss_grad(y, t, name):
    S, D = y.shape
    tm = _blk(S, 512)
    last = S // tm - 1

    def body(y_ref, t_ref, dy_ref, l_ref, acc_ref):
        i = pl.program_id(0)
        e = y_ref[...] - t_ref[...]
        dy_ref[...] = e * (1.0 / D)
        part = jnp.sum(e * e, axis=0, keepdims=True)

        @pl.when(i == 0)
        def _():
            acc_ref[...] = part

        @pl.when(i > 0)
        def _():
            acc_ref[...] += part

        @pl.when(i == last)
        def _():
            l_ref[...] = (0.5 / D) * jnp.sum(acc_ref[...], axis=1, keepdims=True)

    row = pl.BlockSpec((tm, D), lambda i: (i, 0))
    return pl.pallas_call(
        body, name=name, grid=(S // tm,), in_specs=[row, row],
        out_specs=[row, pl.BlockSpec((1, 1), lambda i: (0, 0))],
        out_shape=[jax.ShapeDtypeStruct((S, D), F32), jax.ShapeDtypeStruct((1, 1), F32)],
        scratch_shapes=[pltpu.VMEM((1, D), F32)],
    )(y, t)


def _accumulate(ref, part, first):
    @pl.when(first)
    def _():
        ref[...] = part

    @pl.when(jnp.logical_not(first))
    def _():
        ref[...] += part


def _gate_nt_matmul(dx, f, gt, w, u, name):
    S, D = dx.shape
    N = w.shape[0]
    tm = _blk(S, ROW_BLOCK)
    with_u = u is not None

    def body(*refs):
        if with_u:
            dx_ref, f_ref, gt_ref, w_ref, u_ref, dz_ref, dgt_ref, res_ref = refs
        else:
            dx_ref, f_ref, gt_ref, w_ref, dz_ref, dgt_ref, res_ref = refs
        dxv = dx_ref[...]
        dz = (dxv * gt_ref[...]).astype(_MXU)
        dz_ref[...] = dz
        _accumulate(dgt_ref, jnp.sum(dxv * f_ref[...].astype(F32), axis=0, keepdims=True), pl.program_id(0) == 0)
        r = _nt(dz, w_ref[...])
        if with_u:
            r = r * (2.0 * jnp.maximum(u_ref[...].astype(F32), 0.0))
        res_ref[...] = r.astype(res_ref.dtype)

    row = lambda width: pl.BlockSpec((tm, width), lambda i: (i, 0))
    in_specs = [row(D), row(D), pl.BlockSpec((1, D), lambda i: (0, 0)), pl.BlockSpec((N, D), lambda i: (0, 0))]
    args = [dx, f, gt, w]
    if with_u:
        in_specs.append(row(N))
        args.append(u)
    return pl.pallas_call(
        body, name=name, grid=(S // tm,), in_specs=in_specs,
        out_specs=[row(D), pl.BlockSpec((1, D), lambda i: (0, 0)), row(N)],
        out_shape=[jax.ShapeDtypeStruct((S, D), _MXU), jax.ShapeDtypeStruct((1, D), F32),
                   jax.ShapeDtypeStruct((S, N), _MXU)],
    )(*args)


def _tn_matmul(a, b, by_col, relu2, name):
    S, Ka = a.shape
    Nb = b.shape[1]
    ts = _blk(S, 2 * ROW_BLOCK)
    half = NDEV // 2
    if by_col:
        R, C = Ka, Nb // NDEV
        a_spec = pl.BlockSpec((ts, Ka), lambda h, k: (k, 0))
        b_spec = pl.BlockSpec((ts, half * C), lambda h, k: (k, h))
    else:
        R, C = Ka // NDEV, Nb
        a_spec = pl.BlockSpec((ts, half * R), lambda h, k: (k, h))
        b_spec = pl.BlockSpec((ts, Nb), lambda h, k: (k, 0))

    def body(a_ref, b_ref, o_ref):
        av = a_ref[...]
        if relu2:
            af = jnp.maximum(av.astype(F32), 0.0)
            av = (af * af).astype(_MXU)
        p = _tn(av, b_ref[...])
        first = pl.program_id(1) == 0
        for d in range(half):
            part = p[:, d * C:(d + 1) * C] if by_col else p[d * R:(d + 1) * R, :]
            _accumulate(o_ref.at[d], part, first)

    return pl.pallas_call(
        body, name=name, grid=(NDEV // half, S // ts), in_specs=[a_spec, b_spec],
        out_specs=pl.BlockSpec((half, R, C), lambda h, k: (h, 0, 0)),
        out_shape=jax.ShapeDtypeStruct((NDEV, R, C), F32),
    )(a, b)


def _nt_ln_bwd(dy, w, x, g, sc, sh, dxres, name):
    S, D = x.shape
    N = w.shape[1]
    tm = _blk(S, ROW_BLOCK)

    def body(dy_ref, w_ref, x_ref, g_ref, sc_ref, sh_ref, dxr_ref, dx_ref, dsh_ref, dsc_ref, dg_ref):
        dh = _nt(dy_ref[...], w_ref[...])
        xv = x_ref[...]
        r = lax.rsqrt(jnp.mean(xv * xv, axis=-1, keepdims=True) + EPS)
        xhat = xv * r
        gv = g_ref[...]
        dn = dh * (1.0 + sc_ref[...])
        dxhat = dn * gv
        dxv = r * (dxhat - xhat * jnp.mean(dxhat * xhat, axis=-1, keepdims=True))
        dx_ref[...] = dxr_ref[...] + dxv
        first = pl.program_id(0) == 0
        _accumulate(dsh_ref, jnp.sum(dh, axis=0, keepdims=True), first)
        _accumulate(dsc_ref, jnp.sum(dh * (xhat * gv), axis=0, keepdims=True), first)
        _accumulate(dg_ref, jnp.sum(dn * xhat, axis=0, keepdims=True), first)

    row = lambda width: pl.BlockSpec((tm, width), lambda i: (i, 0))
    vec = pl.BlockSpec((1, D), lambda i: (0, 0))
    return pl.pallas_call(
        body, name=name, grid=(S // tm,),
        in_specs=[row(N), pl.BlockSpec((D, N), lambda i: (0, 0)), row(D), vec, vec, vec, row(D)],
        out_specs=[row(D), vec, vec, vec],
        out_shape=[jax.ShapeDtypeStruct((S, D), F32)] + [jax.ShapeDtypeStruct((1, D), F32)] * 3,
    )(dy, w, x, g, sc, sh, dxres)


def _split2(v):
    hi = v.astype(_MXU)
    mid = (v - hi.astype(F32)).astype(_MXU)
    return hi, mid


def _tri_sums(vs, tri2):
    T = vs[0].shape[0]
    out = []
    for j in range(len(vs) // 2):
        wide = [jnp.concatenate(_split2(vs[2 * j + e]), axis=1) for e in range(2)]
        for both in _per_head(_nn(jnp.concatenate(wide, axis=0), tri2), T):
            out.append((both[:, :T], both[:, T:]))
    return out


def _tri2(T, inclusive):
    j = lax.broadcasted_iota(jnp.int32, (2 * T, 2 * T), 0) % T
    s = lax.broadcasted_iota(jnp.int32, (2 * T, 2 * T), 1)
    keep = (j >= s) if inclusive else (j > s)
    return jnp.where((s >= T) | keep, 1.0, 0.0).astype(_MXU)


def _log_sigmoid(z):
    return jnp.minimum(z, 0.0) - jnp.log(1.0 + jnp.exp(-jnp.abs(z)))


def _per_head(tall, T):
    return [tall[h * T:(h + 1) * T] for h in range(tall.shape[0] // T)]


def _sb_blocks(q_tall, k2, strict, tri2, carry):
    T = k2[0].shape[0]
    zs = []
    for qt, kblk in zip(q_tall, k2):
        zs += _per_head(_nt(qt, kblk), T)
    lbs, l1s = [], []
    for z in zs:
        lb = _log_sigmoid(z)
        l1 = lb - z
        if strict is not None:
            l1 = jnp.where(strict, l1, 0.0)
        lbs.append(lb)
        l1s.append(l1)
    sums = _tri_sums(l1s, tri2)
    amps, new_carry = [], []
    for lb, (sfx, tot), c in zip(lbs, sums, carry):
        a = jnp.exp(lb + sfx + c)
        if strict is not None:
            a = jnp.where(strict, a, 0.0)
        amps.append(a)
        new_carry.append(c + tot)
    return lbs, amps, new_carry


def _sb_alive(carry):
    top = carry[0]
    for c in carry[1:]:
        top = jnp.maximum(top, c)
    return jnp.max(top) > SB_SKIP


def _skew_index():
    i = lax.broadcasted_iota(jnp.int32, (CA_T, SKEW_W + 1), 0)
    m = lax.broadcasted_iota(jnp.int32, (CA_T, SKEW_W + 1), 1)
    wrapped = i + m >= SKEW_W
    row = jnp.where(wrapped, i + 1, i)
    j = jnp.where(wrapped, i + m - SKEW_W, i + m)
    a = row // CHUNK
    jj = j - a * CHUNK
    inband = (jj >= 0) & (jj < BAND) & (j < CA_W) & (row < CA_T)
    idx = jnp.clip((row - a * CHUNK) + PAD - jj, -REL_CLIP, REL_CLIP) + REL_CLIP
    return inband, idx, wrapped


def _skew(tile):
    H = tile.shape[0]
    flat = jnp.pad(tile, ((0, 0), (0, 0), (0, SKEW_W - CA_W))).reshape(H, CA_T * SKEW_W)
    return jnp.pad(flat, ((0, 0), (0, CA_T))).reshape(H, CA_T, SKEW_W + 1)


def _unskew(view):
    H = view.shape[0]
    flat = view.reshape(H, CA_T * (SKEW_W + 1))[:, :CA_T * SKEW_W]
    return flat.reshape(H, CA_T, SKEW_W)[:, :, :CA_W]


def _ca_bias(rel_bias, name):
    H = rel_bias.shape[0]
    top = rel_bias[:, N_REL - 1:]
    by_offset = jnp.concatenate(
        [jnp.broadcast_to(top, (H, PAD - REL_CLIP + 1)), jnp.flip(rel_bias[:, :N_REL - 1], axis=1),
         jnp.broadcast_to(top, (H, SKEW_W + 1 - (PAD - REL_CLIP + 1) - (N_REL - 1)))], axis=1)

    def body(t_ref, o_ref):
        inband, _, wrapped = _skew_index()
        vals = jnp.where(wrapped, t_ref[0][:, 0:1], t_ref[0])
        o_ref[0] = jnp.where(inband, vals, NEG)

    view = pl.pallas_call(
        body, name=name, grid=(H,), in_specs=[pl.BlockSpec((1, 1, SKEW_W + 1), lambda h: (h, 0, 0))],
        out_specs=pl.BlockSpec((1, CA_T, SKEW_W + 1), lambda h: (h, 0, 0)),
        out_shape=jax.ShapeDtypeStruct((H, CA_T, SKEW_W + 1), F32),
    )(by_offset.reshape(H, 1, SKEW_W + 1))
    return _unskew(view)


def _ca_bias_bwd(dbias, name):
    H = dbias.shape[0]

    def body(d_ref, o_ref):
        inband, idx, _ = _skew_index()
        d = jnp.where(inband, d_ref[0], 0.0)
        clipped = idx == N_REL - 1
        by_offset = jnp.sum(jnp.where(clipped, 0.0, d), axis=0, keepdims=True)
        top = jnp.sum(jnp.sum(jnp.where(clipped, d, 0.0), axis=0, keepdims=True), axis=1, keepdims=True)
        lane = lax.broadcasted_iota(jnp.int32, (1, SKEW_W + 1), 1)
        o_ref[0] = jnp.where(lane == 0, top, by_offset)

    out = pl.pallas_call(
        body, name=name, grid=(H,), in_specs=[pl.BlockSpec((1, CA_T, SKEW_W + 1), lambda h: (h, 0, 0))],
        out_specs=pl.BlockSpec((1, 1, SKEW_W + 1), lambda h: (h, 0, 0)),
        out_shape=jax.ShapeDtypeStruct((H, 1, SKEW_W + 1), F32),
    )(_skew(dbias))[:, 0]
    first = PAD - REL_CLIP + 1
    return jnp.concatenate([jnp.flip(out[:, first:first + N_REL - 1], axis=1), out[:, 0:1]], axis=1)


def _low_lanes(rows):
    return lax.broadcasted_iota(jnp.int32, (rows, PAIR), 1) < HEAD_DIM


def _one_head(t2, low, first, scale=1.0):
    tf = t2.astype(F32) * scale
    return (jnp.where(low, tf, 0.0) if first else jnp.where(low, 0.0, tf)).astype(_MXU)


def _two_heads(t2, low, scale=1.0):
    return jnp.concatenate([_one_head(t2, low, True, scale), _one_head(t2, low, False, scale)], axis=0)


def _sb_fwd(proj, name, ex=None):
    S, W = proj.shape
    half = W // 6
    npair = half // PAIR
    T = _blk(S, SB_T)
    GP = _blk(npair, SB_PAIRS)
    GW = GP * PAIR
    nb = npair // GP

    def body(q_ref, k_ref, v_ref, o_ref, ox_ref):
        qi = pl.program_id(1)
        low = _low_lanes(T)
        q_tall = []
        for j in range(GP):
            q2 = q_ref[:, j * PAIR:(j + 1) * PAIR]
            q_tall.append(_two_heads(q2, low, HEAD_DIM ** -0.5))
        row = lax.broadcasted_iota(jnp.int32, (T, T), 0)
        col = lax.broadcasted_iota(jnp.int32, (T, T), 1)
        tri2 = _tri2(T, inclusive=False)

        def pairs(kb, carry, acc, fine, strict):
            rows = pl.ds(pl.multiple_of(kb * T, T), T)
            k2 = [k_ref[rows, j * PAIR:(j + 1) * PAIR] for j in range(GP)]
            v2 = [v_ref[rows, j * PAIR:(j + 1) * PAIR] for j in range(GP)]
            _, amps, carry = _sb_blocks(q_tall, k2, strict, tri2, carry)
            parts = [_split2(a) for a in amps]
            new_acc, new_fine = [], []
            for j in range(GP):
                tall = jnp.concatenate([parts[2 * j][0], parts[2 * j + 1][0], parts[2 * j][1], parts[2 * j + 1][1]],
                                       axis=0)
                hi0, hi1, mid0, mid1 = _per_head(_nn(tall, v2[j]), T)
                new_acc.append(acc[j] + jnp.where(low, hi0, hi1))
                new_fine.append(fine[j] + jnp.where(low, mid0, mid1))
            return tuple(carry), tuple(new_acc), tuple(new_fine)

        zero = (jnp.zeros((T, PAIR), F32),) * GP
        carry, acc, fine = pairs(qi, (jnp.zeros((T, T), F32),) * (2 * GP), zero, zero, col < row)

        def cond(st):
            kb, alive, _, _, _ = st
            return jnp.logical_and(kb >= 0, alive)

        def step(st):
            kb, _, carry, acc, fine = st
            carry, acc, fine = pairs(kb, carry, acc, fine, None)
            return kb - 1, _sb_alive(carry), carry, acc, fine

        _, _, _, acc, fine = lax.while_loop(cond, step, (qi - 1, _sb_alive(carry), carry, acc, fine))
        for j in range(GP):
            o_ref[:, j * PAIR:(j + 1) * PAIR] = acc[j].astype(o_ref.dtype)
            ox_ref[:, j * PAIR:(j + 1) * PAIR] = acc[j] + fine[j]

    blk = pl.BlockSpec((T, GW), lambda p, i: (i, p))
    return _call_hosted(
        body, name, (nb, S // T),
        [blk, pl.BlockSpec((S, GW), lambda p, i: (0, nb + p)), pl.BlockSpec((S, GW), lambda p, i: (0, 2 * nb + p))],
        [blk, blk], [jax.ShapeDtypeStruct((S, half), _MXU), jax.ShapeDtypeStruct((S, half), F32)],
        [], [proj, proj, proj], ex)


def _sb_bwd(proj, ox, dmixed, name, ex=None):
    S, W = proj.shape
    half = W // 6
    npair = half // PAIR
    T = _blk(S, SB_T)
    GP = _blk(npair, SB_PAIRS)
    GW = GP * PAIR
    nb = npair // GP
    last = S // T - 1
    scale = HEAD_DIM ** -0.5

    def body(q_ref, k_ref, v_ref, ox_ref, do_ref, dq_ref, dk_ref, dv_ref, dka_ref, dva_ref):
        qi = pl.program_id(1)

        @pl.when(qi == 0)
        def _():
            dka_ref[...] = jnp.zeros_like(dka_ref)
            dva_ref[...] = jnp.zeros_like(dva_ref)

        low = _low_lanes(T)
        q2, do2, q_tall, do_tall, deltas = [], [], [], [], []
        for j in range(GP):
            cols = slice(j * PAIR, (j + 1) * PAIR)
            q2.append(q_ref[:, cols])
            do2.append(do_ref[:, cols])
            q_tall.append(_two_heads(q2[j], low, scale))
            dobs = [_one_head(do2[j], low, True), _one_head(do2[j], low, False)]
            do_tall.append(jnp.concatenate(dobs, axis=0))
            for e in range(2):
                deltas.append(jnp.sum(dobs[e].astype(F32) * ox_ref[:, cols], axis=-1, keepdims=True))
        row = lax.broadcasted_iota(jnp.int32, (T, T), 0)
        col = lax.broadcasted_iota(jnp.int32, (T, T), 1)
        tri_ex = _tri2(T, inclusive=False)
        tri_in = _tri2(T, inclusive=True)

        def pairs(kb, carry, right, dq, strict):
            rows = pl.ds(pl.multiple_of(kb * T, T), T)
            k2 = [k_ref[rows, j * PAIR:(j + 1) * PAIR] for j in range(GP)]
            v2 = [v_ref[rows, j * PAIR:(j + 1) * PAIR] for j in range(GP)]
            nh = 2 * GP
            gs = []
            for j in range(GP):
                gs += _per_head(_nt(do_tall[j], v2[j]), T)
            lbs, amps, carry = _sb_blocks(q_tall, k2, strict, tri_ex, carry)
            ags = [a * gg for a, gg in zip(amps, gs)]
            sums = _tri_sums(ags, tri_in)
            dzbs = []
            for h in range(nh):
                left = deltas[h] - (sums[h][0] + right[h])
                beta = jnp.exp(lbs[h])
                dz = ags[h] - beta * (ags[h] + left)
                if strict is not None:
                    dz = jnp.where(strict, dz, 0.0)
                dzbs.append(dz.astype(_MXU))
            abs_ = [a.astype(_MXU) for a in amps]
            new_dq = []
            for j in range(GP):
                cols = slice(j * PAIR, (j + 1) * PAIR)
                dk0, dk1 = _per_head(_tn(jnp.concatenate(dzbs[2 * j:2 * j + 2], axis=1), q2[j]), T)
                dv0, dv1 = _per_head(_tn(jnp.concatenate(abs_[2 * j:2 * j + 2], axis=1), do2[j]), T)
                dq0, dq1 = _per_head(_nn(jnp.concatenate(dzbs[2 * j:2 * j + 2], axis=0), k2[j]), T)
                dka_ref[rows, cols] += jnp.where(low, dk0, dk1)
                dva_ref[rows, cols] += jnp.where(low, dv0, dv1)
                new_dq.append(dq[j] + jnp.where(low, dq0, dq1))
            right = tuple(right[h] + sums[h][1] for h in range(nh))
            return tuple(carry), right, tuple(new_dq)

        zero = (jnp.zeros((T, T), F32),) * (2 * GP)
        carry, right, dq = pairs(qi, zero, zero, (jnp.zeros((T, PAIR), F32),) * GP, col < row)

        def cond(st):
            kb, alive, _, _, _ = st
            return jnp.logical_and(kb >= 0, alive)

        def step(st):
            kb, _, carry, right, dq = st
            carry, right, dq = pairs(kb, carry, right, dq, None)
            return kb - 1, _sb_alive(carry), carry, right, dq

        _, _, _, _, dq = lax.while_loop(cond, step, (qi - 1, _sb_alive(carry), carry, right, dq))
        for j in range(GP):
            dq_ref[:, j * PAIR:(j + 1) * PAIR] = (dq[j] * scale).astype(dq_ref.dtype)

        @pl.when(qi == last)
        def _():
            dk_ref[...] = (dka_ref[...] * scale).astype(dk_ref.dtype)
            dv_ref[...] = dva_ref[...].astype(dv_ref.dtype)

    blk = pl.BlockSpec((T, GW), lambda p, i: (i, p))
    full = pl.BlockSpec((S, GW), lambda p, i: (0, p))
    return _call_hosted(
        body, name, (nb, S // T),
        [blk, pl.BlockSpec((S, GW), lambda p, i: (0, nb + p)), pl.BlockSpec((S, GW), lambda p, i: (0, 2 * nb + p)),
         blk, blk],
        [blk, full, full], [jax.ShapeDtypeStruct((S, half), _MXU)] * 3,
        [pltpu.VMEM((S, GW), F32), pltpu.VMEM((S, GW), F32)], [proj, proj, proj, ox, dmixed], ex)


def _pair_norm(t2, g2, low):
    tf = t2.astype(F32)
    sq = tf * tf
    both = jnp.sum(sq, axis=-1, keepdims=True)
    first = jnp.sum(jnp.where(low, sq, 0.0), axis=-1, keepdims=True)
    r = jnp.where(low, lax.rsqrt(first * (1.0 / HEAD_DIM) + EPS), lax.rsqrt((both - first) * (1.0 / HEAD_DIM) + EPS))
    hat = tf * r
    return hat * g2, hat, r


def _pair_norm_bwd(dn, hat, r, g2, low):
    dhat = dn * g2
    prod = dhat * hat
    both = jnp.sum(prod, axis=-1, keepdims=True)
    first = jnp.sum(jnp.where(low, prod, 0.0), axis=-1, keepdims=True)
    mean = jnp.where(low, first, both - first) * (1.0 / HEAD_DIM)
    return r * (dhat - hat * mean)


def _ca_fill(j, k_ref, v_ref, gk_ref, kn_ref, vp_ref):
    S = k_ref.shape[0]
    cols = slice(j * PAIR, (j + 1) * PAIR)
    kn, _, _ = _pair_norm(k_ref[:, cols], gk_ref[...], _low_lanes(S))
    kn_ref[j, 0:PAD, :] = jnp.zeros((PAD, PAIR), kn_ref.dtype)
    vp_ref[j, 0:PAD, :] = jnp.zeros((PAD, PAIR), vp_ref.dtype)
    kn_ref[j, PAD:PAD + S, :] = kn.astype(kn_ref.dtype)
    vp_ref[j, PAD:PAD + S, :] = v_ref[:, cols]


def _ca_scores(j, q_ref, b2_ref, gq_ref, kn_ref, qi, low):
    qn, qhat, r = _pair_norm(q_ref[:, j * PAIR:(j + 1) * PAIR], gq_ref[...], low)
    qn = qn * HEAD_DIM ** -0.5
    band = pl.ds(pl.multiple_of(qi * CA_T, CA_T), CA_W)
    key_pos = qi * CA_T - PAD + lax.broadcasted_iota(jnp.int32, (CA_T, CA_W), 1)
    both = _per_head(_nt(_two_heads(qn, low), kn_ref[j, band, :]), CA_T)
    scores = [jnp.where(key_pos >= 0, both[e] + b2_ref[2 * j + e], NEG) for e in range(2)]
    return scores, qn.astype(_MXU), qhat, r


def _softmax(s):
    e = jnp.exp(s - jnp.max(s, axis=-1, keepdims=True))
    return e * (1.0 / jnp.sum(e, axis=-1, keepdims=True))


def _ca_fwd(proj, bias2, gq2, gk2, name, ex=None):
    S, W = proj.shape
    half = W // 6
    npair = half // PAIR
    GP = _blk(npair, CA_PAIRS_FWD)
    GW = GP * PAIR
    nb = npair // GP

    def body(q_ref, k_ref, v_ref, b2_ref, gq_ref, gk_ref, o_ref, kn_ref, vp_ref):
        qi = pl.program_id(1)

        @pl.when(qi == 0)
        def _():
            for j in range(GP):
                _ca_fill(j, k_ref, v_ref, gk_ref, kn_ref, vp_ref)

        low = _low_lanes(CA_T)
        band = pl.ds(pl.multiple_of(qi * CA_T, CA_T), CA_W)
        scores = [_ca_scores(j, q_ref, b2_ref, gq_ref, kn_ref, qi, low)[0] for j in range(GP)]
        probs = [[_softmax(s).astype(_MXU) for s in pair] for pair in scores]
        for j in range(GP):
            outs = _per_head(_nn(jnp.concatenate(probs[j], axis=0), vp_ref[j, band, :]), CA_T)
            o_ref[:, j * PAIR:(j + 1) * PAIR] = jnp.where(low, outs[0], outs[1]).astype(o_ref.dtype)

    vec = pl.BlockSpec((1, PAIR), lambda p, i: (0, 0))
    return _call_hosted(
        body, name, (nb, S // CA_T),
        [pl.BlockSpec((CA_T, GW), lambda p, i: (i, 3 * nb + p)),
         pl.BlockSpec((S, GW), lambda p, i: (0, 4 * nb + p)), pl.BlockSpec((S, GW), lambda p, i: (0, 5 * nb + p)),
         pl.BlockSpec((2 * GP, CA_T, CA_W), lambda p, i: (p, 0, 0)), vec, vec],
        [pl.BlockSpec((CA_T, GW), lambda p, i: (i, p))], [jax.ShapeDtypeStruct((S, half), _MXU)],
        [pltpu.VMEM((GP, PAD + S, PAIR), _MXU), pltpu.VMEM((GP, PAD + S, PAIR), _MXU)],
        [proj, proj, proj, bias2, gq2, gk2], ex)


def _ca_bwd(proj, bias2, gq2, gk2, dmixed, name, ex=None):
    S, W = proj.shape
    half = W // 6
    npair = half // PAIR
    GP = _blk(npair, CA_PAIRS_BWD)
    GW = GP * PAIR
    nb = npair // GP
    scale = HEAD_DIM ** -0.5
    last = S // CA_T - 1

    def body(q_ref, k_ref, v_ref, b2_ref, gq_ref, gk_ref, do_ref,
             dq_ref, dk_ref, dv_ref, db_ref, dgq_ref, dgk_ref, kn_ref, vp_ref, dkn_ref, dvp_ref):
        p_id, qi = pl.program_id(0), pl.program_id(1)

        @pl.when(qi == 0)
        def _():
            for j in range(GP):
                _ca_fill(j, k_ref, v_ref, gk_ref, kn_ref, vp_ref)
            dkn_ref[...] = jnp.zeros_like(dkn_ref)
            dvp_ref[...] = jnp.zeros_like(dvp_ref)
            db_ref[...] = jnp.zeros_like(db_ref)

        @pl.when(jnp.logical_and(p_id == 0, qi == 0))
        def _():
            dgq_ref[...] = jnp.zeros_like(dgq_ref)
            dgk_ref[...] = jnp.zeros_like(dgk_ref)

        low = _low_lanes(CA_T)
        top_w = lax.broadcasted_iota(jnp.int32, (PAIR, CA_W), 0) < HEAD_DIM
        band = pl.ds(pl.multiple_of(qi * CA_T, CA_T), CA_W)
        pairs = [_ca_scores(j, q_ref, b2_ref, gq_ref, kn_ref, qi, low) for j in range(GP)]
        do2 = [do_ref[:, j * PAIR:(j + 1) * PAIR] for j in range(GP)]
        dps = [_per_head(_nt(_two_heads(do2[j], low), vp_ref[j, band, :]), CA_T) for j in range(GP)]
        probs, dsbs = [], []
        for j in range(GP):
            pj, dj = [], []
            for e in range(2):
                p = _softmax(pairs[j][0][e])
                ds = p * (dps[j][e] - jnp.sum(p * dps[j][e], axis=-1, keepdims=True))
                db_ref[2 * j + e] += ds
                pj.append(p.astype(_MXU))
                dj.append(ds.astype(_MXU))
            probs.append(pj)
            dsbs.append(dj)
        dgq = jnp.zeros((1, PAIR), F32)
        for j in range(GP):
            _, qn, qhat, r = pairs[j]
            dq_h = _per_head(_nn(jnp.concatenate(dsbs[j], axis=0), kn_ref[j, band, :]), CA_T)
            dk_t = _tn(qn, jnp.concatenate(dsbs[j], axis=1))
            dv_t = _tn(do2[j], jnp.concatenate(probs[j], axis=1))
            dkn_ref[j, :, band] += jnp.where(top_w, dk_t[:, :CA_W], dk_t[:, CA_W:])
            dvp_ref[j, :, band] += jnp.where(top_w, dv_t[:, :CA_W], dv_t[:, CA_W:])
            dqn = jnp.where(low, dq_h[0], dq_h[1]) * scale
            dgq = dgq + jnp.sum(dqn * qhat, axis=0, keepdims=True)
            dq_ref[:, j * PAIR:(j + 1) * PAIR] = _pair_norm_bwd(dqn, qhat, r, gq_ref[...], low).astype(dq_ref.dtype)
        dgq_ref[...] += dgq

        @pl.when(qi == last)
        def _():
            low_s = _low_lanes(S)
            for j in range(GP):
                cols = slice(j * PAIR, (j + 1) * PAIR)
                _, khat, rk = _pair_norm(k_ref[:, cols], gk_ref[...], low_s)
                dkn = dkn_ref[j, :, PAD:PAD + S].T
                dgk_ref[...] += jnp.sum(dkn * khat, axis=0, keepdims=True)
                dk_ref[:, cols] = _pair_norm_bwd(dkn, khat, rk, gk_ref[...], low_s).astype(dk_ref.dtype)
                dv_ref[:, cols] = dvp_ref[j, :, PAD:PAD + S].T.astype(dv_ref.dtype)

    vec = pl.BlockSpec((1, PAIR), lambda p, i: (0, 0))
    tile = pl.BlockSpec((2 * GP, CA_T, CA_W), lambda p, i: (p, 0, 0))
    full = pl.BlockSpec((S, GW), lambda p, i: (0, p))
    return _call_hosted(
        body, name, (nb, S // CA_T),
        [pl.BlockSpec((CA_T, GW), lambda p, i: (i, 3 * nb + p)),
         pl.BlockSpec((S, GW), lambda p, i: (0, 4 * nb + p)), pl.BlockSpec((S, GW), lambda p, i: (0, 5 * nb + p)),
         tile, vec, vec, pl.BlockSpec((CA_T, GW), lambda p, i: (i, nb + p))],
        [pl.BlockSpec((CA_T, GW), lambda p, i: (i, p)), full, full, tile, vec, vec],
        [jax.ShapeDtypeStruct((S, half), _MXU)] * 3
        + [jax.ShapeDtypeStruct(bias2.shape, F32), jax.ShapeDtypeStruct((1, PAIR), F32),
           jax.ShapeDtypeStruct((1, PAIR), F32)],
        [pltpu.VMEM((GP, PAD + S, PAIR), _MXU), pltpu.VMEM((GP, PAD + S, PAIR), _MXU),
         pltpu.VMEM((GP, PAIR, PAD + S), F32), pltpu.VMEM((GP, PAIR, PAD + S), F32)],
        [proj, proj, proj, bias2, gq2, gk2, dmixed], ex)


def _pack_small(parts):
    flat = jnp.concatenate([p.reshape(-1) for layer in parts for p in layer])
    n = flat.shape[0]
    n_pad = -(-n // 1024) * 1024
    return jnp.pad(flat, (0, n_pad - n)).reshape(1, n_pad)


def _unpack_small(flat, shapes):
    out, off = [], 0
    for layer in shapes:
        cur = []
        for shp in layer:
            size = 1
            for s in shp:
                size *= s
            cur.append(flat[off:off + size].reshape(shp))
            off += size
        out.append(cur)
    return out


def kernel(x, c, g_norm1, w_in, g_q, g_k, rel_bias, w_o, g_norm2, w1, w2, w_ada, b_ada, loss_target, m_g_norm1, m_w_in, m_g_q, m_g_k, m_rel_bias, m_w_o, m_g_norm2, m_w1, m_w2, m_w_ada, m_b_ada, v_g_norm1, v_w_in, v_g_q, v_g_k, v_rel_bias, v_w_o, v_g_norm2, v_w1, v_w2, v_w_ada, v_b_ada):
    L = w_in.shape[0]
    S, D = x.shape[1:]
    H2 = D // HEAD_DIM // 2
    Ca = w_ada.shape[2]
    xi, yi, ci = _pos()
    me = 4 * xi + 2 * yi + ci
    place = jnp.stack([2 * xi + yi, ci]).astype(jnp.int32)

    wire = lambda a: a.astype(_MXU)
    by_cols = lambda g: g.transpose(1, 0, 2).reshape(D, g.shape[0] * g.shape[2])
    b_cols = lax.dynamic_slice(b_ada, (0, me * Ca), (L, Ca))
    c_all, mod_all, first = _start(c, w_ada, b_cols, wire(w_in[0]), "start")
    c_all = c_all.reshape(NDEV, D)
    mod = lax.dynamic_index_in_dim(mod_all, me, axis=1, keepdims=False)
    mod = mod.reshape(NDEV, L, Ca).transpose(1, 0, 2).reshape(L, 6, 1, D)
    W_in = {0: by_cols(first)}
    W_o, W_1, W_2 = {}, {}, {}

    xs = [x[0]]
    saved = []
    for l in range(L):
        sh1, sc1, gt1, sh2, sc2, gt2 = [mod[l, i] for i in range(6)]
        gn1, gn2 = g_norm1[l:l + 1], g_norm2[l:l + 1]
        gq2, gk2 = jnp.tile(g_q[l:l + 1], (1, 2)), jnp.tile(g_k[l:l + 1], (1, 2))
        proj, h1 = _ln_mod_matmul(xs[-1], gn1, sc1, sh1, W_in[l], f"l{l}_proj")
        (o_sb, ox_sb), got = _sb_fwd(proj, f"l{l}_sb_fwd", _gather_exchange([wire(w1[l]), wire(w2[l])]))
        W_1[l], W_2[l] = by_cols(got[0]), got[1].reshape(4 * D, D)
        bias2 = _ca_bias(rel_bias[l], f"l{l}_ca_bias")
        nxt = [wire(w_in[l + 1])] if l + 1 < L else []
        (o_ca,), got = _ca_fwd(proj, bias2, gq2, gk2, f"l{l}_ca_fwd", _gather_exchange([wire(w_o[l])] + nxt))
        W_o[l] = got[0].reshape(D, D)
        if nxt:
            W_in[l + 1] = by_cols(got[1])
        mixed = jnp.concatenate([o_sb, o_ca], axis=1)
        x1, f1 = _matmul_res_gate(mixed, W_o[l], xs[-1], gt1, False, f"l{l}_attn_out")
        u, h2 = _ln_mod_matmul(x1, gn2, sc2, sh2, W_1[l], f"l{l}_mlp_in")
        x2, f2 = _matmul_res_gate(u, W_2[l], x1, gt2, True, f"l{l}_mlp_out")
        saved.append(dict(x0=xs[-1], h1=h1, proj=proj, ox_sb=ox_sb, bias2=bias2, mixed=mixed, f1=f1, x1=x1,
                          h2=h2, u=u, f2=f2))
        xs.append(x2)

    dx, loss_part = _loss_grad(xs[-1], loss_target[0], "loss")

    owns, recv_b = {}, {}
    ready = []
    small_parts = [None] * L

    def partials(keys, grads, recv_a):
        parts = []
        for key, g, r in zip(keys, grads, recv_a):
            owns[key], part = _rs_chip_partial(place, g, r, f"rs_partial_l{key[0]}_{key[1]}")
            parts.append(part)
        return parts

    for l in reversed(range(L)):
        sv = saved[l]
        sh1, sc1, gt1, sh2, sc2, gt2 = [mod[l, i] for i in range(6)]
        gn1, gn2 = g_norm1[l:l + 1], g_norm2[l:l + 1]
        gq2, gk2 = jnp.tile(g_q[l:l + 1], (1, 2)), jnp.tile(g_k[l:l + 1], (1, 2))
        dz2, dgt2, du = _gate_nt_matmul(dx, sv["f2"], gt2, W_2[l], sv["u"], f"l{l}_mlp_out_bwd")
        gw2 = _tn_matmul(sv["u"], dz2, False, True, f"l{l}_gw2")
        gw1 = _tn_matmul(sv["h2"], du, True, False, f"l{l}_gw1")
        dx, dsh2, dsc2, dgn2 = _nt_ln_bwd(du, W_1[l], sv["x1"], gn2, sc2, sh2, dx, f"l{l}_mlp_in_bwd")
        dz1, dgt1, dmixed = _gate_nt_matmul(dx, sv["f1"], gt1, W_o[l], None, f"l{l}_attn_out_bwd")
        gwo = _tn_matmul(sv["mixed"], dz1, False, False, f"l{l}_gwo")
        ready += [((l, 1), gwo), ((l, 2), gw1), ((l, 3), gw2)]
        keys, grads = [k for k, _ in ready], [g for _, g in ready]
        (dq_sb, dk_sb, dv_sb), recv_a = _sb_bwd(sv["proj"], sv["ox_sb"], dmixed, f"l{l}_sb_bwd",
                                                _sibling_exchange(grads))
        parts = partials(keys, grads, recv_a)
        (dq_ca, dk_ca, dv_ca, dbias2, dgq2, dgk2), got = _ca_bwd(sv["proj"], sv["bias2"], gq2, gk2, dmixed,
                                                                 f"l{l}_ca_bwd", _chip_exchange(parts))
        recv_b.update(zip(keys, got))
        dgq = dgq2[:, :HEAD_DIM] + dgq2[:, HEAD_DIM:]
        dgk = dgk2[:, :HEAD_DIM] + dgk2[:, HEAD_DIM:]
        drb = _ca_bias_bwd(dbias2, f"l{l}_ca_bias_bwd")
        dproj = jnp.concatenate([dq_sb, dk_sb, dv_sb, dq_ca, dk_ca, dv_ca], axis=1)
        gwin = _tn_matmul(sv["h1"], dproj, True, False, f"l{l}_gwin")
        ready = [((l, 0), gwin)]
        dx, dsh1, dsc1, dgn1 = _nt_ln_bwd(dproj, W_in[l], sv["x0"], gn1, sc1, sh1, dx, f"l{l}_proj_bwd")
        dmod = jnp.concatenate([dsh1, dsc1, dgt1, dsh2, dsc2, dgt2], axis=1)
        small_parts[l] = [dgn1, dgq, dgk, drb, dgn2, dmod]
    grad_x = dx[None]

    keys, grads = [k for k, _ in ready], [g for _, g in ready]
    parts = partials(keys, grads, _run_exchange(_sibling_exchange(grads), "rs_sibling_last"))
    recv_b.update(zip(keys, _run_exchange(_chip_exchange(parts), "rs_chips_last")))
    big_out = []
    for t, (w, m, v) in enumerate([(w_in, m_w_in, v_w_in), (w_o, m_w_o, v_w_o), (w1, m_w1, v_w1), (w2, m_w2, v_w2)]):
        big_out.append(_rs_sum_adamw([owns[(l, t)] for l in range(L)], [recv_b[(l, t)] for l in range(L)],
                                     w, m, v, f"adamw_big_{t}"))

    packed = _pack_small(small_parts)
    gathered_small = _all_gather_small(packed, "ag_small_grads")
    small_sum = _sum_devices(gathered_small, "sum_small_grads")
    shapes = [[(1, D), (1, HEAD_DIM), (1, HEAD_DIM), (H2, N_REL), (1, D), (1, 6 * D)]] * L
    names = ["g_norm1", "g_q", "g_k", "rel_bias", "g_norm2", "b_ada"]
    small_w = {"g_norm1": (g_norm1, m_g_norm1, v_g_norm1), "g_q": (g_q, m_g_q, v_g_q), "g_k": (g_k, m_g_k, v_g_k),
               "rel_bias": (rel_bias, m_rel_bias, v_rel_bias), "g_norm2": (g_norm2, m_g_norm2, v_g_norm2),
               "b_ada": (b_ada, m_b_ada, v_b_ada)}
    packs = [_pack_small([[small_w[n][k][l] for n in names] for l in range(L)]) for k in range(3)]
    n_pad = packed.shape[1]
    as_rows = lambda a: a.reshape(n_pad // 128, 128)
    sd, sm, sv_ = _adamw(as_rows(packs[0]), as_rows(small_sum), as_rows(packs[1]), as_rows(packs[2]), "adamw_small")
    small_out = {}
    for key, flat in [("grad", small_sum), ("delta", sd), ("m", sm), ("v", sv_)]:
        per_layer = _unpack_small(flat.reshape(-1), shapes)
        for i, n in enumerate(names):
            small_out[(key, n)] = jnp.stack([per_layer[l][i].reshape(small_w[n][0].shape[1:]) for l in range(L)])

    layer_len = 2 * D + 2 * HEAD_DIM + H2 * N_REL + 6 * D
    rows = gathered_small.reshape(NDEV, n_pad)
    dmod_all = jnp.stack([rows[:, l * layer_len + layer_len - 6 * D:(l + 1) * layer_len] for l in range(L)])
    dmod_cols = lax.dynamic_slice(dmod_all, (0, 0, me * Ca), (L, NDEV, Ca))
    dmod_cols = jnp.pad(dmod_cols, ((0, 0), (0, 128 - NDEV), (0, 0)))
    c_t = jnp.pad(c_all.T, ((0, 0), (0, 128 - NDEV)))
    g_ada = _w_ada_grad(c_t, dmod_cols, "w_ada_grad")
    flat2 = lambda a: a.reshape(L * D, Ca)
    ad, am, av = _adamw(flat2(w_ada), flat2(g_ada), flat2(m_w_ada), flat2(v_w_ada), "adamw_w_ada")
    ada_out = [g_ada] + [a.reshape(L, D, Ca) for a in (ad, am, av)]

    def leaf(kind):
        k = {"grad": 0, "delta": 1, "m": 2, "v": 3}[kind]
        return [small_out[(kind, "g_norm1")], big_out[0][k], small_out[(kind, "g_q")], small_out[(kind, "g_k")],
                small_out[(kind, "rel_bias")], big_out[1][k], small_out[(kind, "g_norm2")], big_out[2][k],
                big_out[3][k], ada_out[k], small_out[(kind, "b_ada")]]

    loss = lax.psum(loss_part[0, 0], ("x", "y", "c"))
    return (loss, grad_x, *leaf("grad"), *leaf("delta"), *leaf("m"), *leaf("v"))
```

```python
import functools

import jax
import jax.numpy as jnp
from jax import lax
from jax.experimental import pallas as pl
from jax.experimental.pallas import tpu as pltpu

F32 = jnp.float32
_MXU = jnp.bfloat16

HEAD_DIM = 64
CHUNK = 64
LEFT_CHUNKS = 8
PAD = LEFT_CHUNKS * CHUNK
BAND = PAD + CHUNK
REL_CLIP = 128
N_REL = 2 * REL_CLIP + 1
EPS = 1e-6
NEG = -1e30
NDEV = 8
SB_T = 128
CA_T = 2 * CHUNK
CA_W = CA_T + PAD
SB_SKIP = -104.0
PAIR = 2 * HEAD_DIM
SB_PAIRS = 4
CA_PAIRS_FWD = 4
CA_PAIRS_BWD = 2
ROW_BLOCK = 512
SKEW_W = CA_W + CA_T - 1

ADAM_LR, ADAM_B1, ADAM_B2, ADAM_EPS, ADAM_WD, ADAM_STEP = 0.001, 0.9, 0.999, 1e-08, 0.01, 10

MESH = pl.DeviceIdType.MESH
VMEM_SPEC = pl.BlockSpec(memory_space=pltpu.VMEM)
ANY_SPEC = pl.BlockSpec(memory_space=pl.ANY)


def _nn(a, b):
    return lax.dot_general(a, b, (((1,), (0,)), ((), ())), preferred_element_type=F32)


def _nt(a, b):
    return lax.dot_general(a, b, (((1,), (1,)), ((), ())), preferred_element_type=F32)


def _tn(a, b):
    return lax.dot_general(a, b, (((0,), (0,)), ((), ())), preferred_element_type=F32)


def _blk(n, pref):
    return pref if n % pref == 0 else n


def _pos():
    return lax.axis_index("x"), lax.axis_index("y"), lax.axis_index("c")


def _flip(v, bit):
    return 1 - v if bit else v


def _gather_small(x_ref, out_ref, send_sems, recv_sems):
    R, C = x_ref.shape
    x, y, c = _pos()
    me = 4 * x + 2 * y + c

    def peer(k):
        return (_flip(x, k & 4), _flip(y, k & 2), _flip(c, k & 1))

    def copy(k, slot):
        return pltpu.make_async_remote_copy(
            src_ref=x_ref, dst_ref=out_ref.at[slot], send_sem=send_sems.at[k - 1],
            recv_sem=recv_sems.at[k - 1], device_id=peer(k), device_id_type=MESH)

    out_ref[pl.ds(me, 1), :, :] = x_ref[...].reshape(1, R, C)
    sends = [copy(k, me) for k in range(1, NDEV)]
    for cp in sends:
        cp.start()
    for k in range(1, NDEV):
        px, py, pc = peer(k)
        copy(k, 4 * px + 2 * py + pc).wait_recv()
    for cp in sends:
        cp.wait_send()


def _all_gather_small(blk, name):
    return pl.pallas_call(
        lambda x_ref, out_ref, send_sems, recv_sems: _gather_small(x_ref, out_ref, send_sems, recv_sems), name=name,
        out_shape=jax.ShapeDtypeStruct((NDEV,) + blk.shape, blk.dtype),
        in_specs=[VMEM_SPEC], out_specs=VMEM_SPEC,
        scratch_shapes=[pltpu.SemaphoreType.DMA((NDEV - 1,)), pltpu.SemaphoreType.DMA((NDEV - 1,))],
    )(blk)


class _Exchange:
    def __init__(self, inputs, out_shapes, sems, start, finish, middle=None):
        self.inputs, self.out_shapes, self.sems = list(inputs), list(out_shapes), list(sems)
        self.start, self.middle, self.finish = start, middle, finish


def _run_exchange(ex, name):
    n_in, n_out = len(ex.inputs), len(ex.out_shapes)

    def body(*refs):
        ins, outs, sems = refs[:n_in], refs[n_in:n_in + n_out], refs[n_in + n_out:]
        ex.start(ins, outs, sems)
        if ex.middle is not None:
            ex.middle(ins, outs, sems)
        ex.finish(ins, outs, sems)

    return pl.pallas_call(
        body, name=name, out_shape=ex.out_shapes, in_specs=[ANY_SPEC] * n_in, out_specs=[ANY_SPEC] * n_out,
        scratch_shapes=ex.sems,
    )(*ex.inputs)


def _hosted(body, n_in, n_out, ex, step, steps):
    if ex is None:
        return body
    xi, xo = len(ex.inputs), len(ex.out_shapes)

    def wrapped(*refs):
        own_in, ex_in = refs[:n_in], refs[n_in:n_in + xi]
        rest = refs[n_in + xi:]
        own_out, ex_out = rest[:n_out], rest[n_out:n_out + xo]
        rest = rest[n_out + xo:]
        own_scratch, ex_sems = rest[:len(rest) - len(ex.sems)], rest[len(rest) - len(ex.sems):]
        t = step()
        pl.when(t == 0)(lambda: ex.start(ex_in, ex_out, ex_sems))
        body(*own_in, *own_out, *own_scratch)
        if ex.middle is not None:
            pl.when(t == (steps * 7) // 8)(lambda: ex.middle(ex_in, ex_out, ex_sems))
        pl.when(t == steps - 1)(lambda: ex.finish(ex_in, ex_out, ex_sems))

    return wrapped


def _call_hosted(body, name, grid, in_specs, out_specs, out_shape, scratch, args, ex):
    n_in, n_out = len(in_specs), len(out_specs)
    steps = 1
    for extent in grid:
        steps *= extent

    def step():
        t = pl.program_id(0)
        for axis in range(1, len(grid)):
            t = t * grid[axis] + pl.program_id(axis)
        return t

    if ex is not None:
        in_specs = in_specs + [ANY_SPEC] * len(ex.inputs)
        out_specs = out_specs + [ANY_SPEC] * len(ex.out_shapes)
        out_shape = out_shape + ex.out_shapes
        scratch = scratch + ex.sems
        args = args + ex.inputs
    outs = pl.pallas_call(
        _hosted(body, n_in, n_out, ex, step, steps), name=name, grid=grid, in_specs=in_specs, out_specs=out_specs,
        out_shape=out_shape, scratch_shapes=scratch,
    )(*args)
    return list(outs[:n_out]), list(outs[n_out:])


def _gather_exchange(shards):
    n = len(shards)

    def setup(ins, outs, sems):
        send_sems, recv_sems, local_sems = sems
        x, y, c = _pos()
        me, sibling = (x, y, c), (x, y, 1 - c)
        chips = [(1 - x, y), (x, 1 - y), (1 - x, 1 - y)]

        def copy(i, k, block, to, src=None):
            px, py, pc = block
            dst = outs[i].at[4 * px + 2 * py + pc]
            return pltpu.make_async_remote_copy(
                src_ref=dst if src is None else src, dst_ref=dst, send_sem=send_sems.at[7 * i + k],
                recv_sem=recv_sems.at[7 * i + k], device_id=to, device_id_type=MESH)

        def mine(i):
            return pltpu.make_async_copy(ins[i], outs[i].at[4 * x + 2 * y + c], local_sems.at[i])

        def first(i):
            return [copy(i, 0, me, sibling, src=ins[i])] + [
                copy(i, 1 + j, me, (*chip, c), src=ins[i]) for j, chip in enumerate(chips)]

        def passed(i, j):
            return copy(i, 4 + j, (*chips[j], c), sibling)

        return me, sibling, chips, c, copy, mine, first, passed

    def start(ins, outs, sems):
        _, _, _, _, _, mine, first, _ = setup(ins, outs, sems)
        for i in range(n):
            mine(i).start()
            for cp in first(i):
                cp.start()

    def middle(ins, outs, sems):
        me, _, chips, c, copy, _, _, passed = setup(ins, outs, sems)
        for j, chip in enumerate(chips):
            for i in range(n):
                copy(i, 1 + j, (*chip, c), me).wait_recv()
                passed(i, j).start()

    def finish(ins, outs, sems):
        me, sibling, chips, c, copy, mine, first, passed = setup(ins, outs, sems)
        for i in range(n):
            copy(i, 0, sibling, me).wait_recv()
            for j, chip in enumerate(chips):
                copy(i, 4 + j, (*chip, 1 - c), me).wait_recv()
        for i in range(n):
            for cp in first(i) + [passed(i, j) for j in range(3)]:
                cp.wait_send()
            mine(i).wait()

    return _Exchange(
        shards, [jax.ShapeDtypeStruct((NDEV,) + s.shape, s.dtype) for s in shards],
        [pltpu.SemaphoreType.DMA((7 * n,)), pltpu.SemaphoreType.DMA((7 * n,)), pltpu.SemaphoreType.DMA((n,))],
        start, finish, middle)


def _sibling_exchange(grads):
    n = len(grads)

    def copies(ins, outs, sems):
        send_sems, recv_sems = sems
        x, y, c = _pos()
        return [pltpu.make_async_remote_copy(
            src_ref=ins[i].at[2 * q + (1 - c)], dst_ref=outs[i].at[q], send_sem=send_sems.at[4 * i + q],
            recv_sem=recv_sems.at[4 * i + q], device_id=(x, y, 1 - c), device_id_type=MESH)
            for i in range(n) for q in range(4)]

    def start(ins, outs, sems):
        for cp in copies(ins, outs, sems):
            cp.start()

    def finish(ins, outs, sems):
        for cp in copies(ins, outs, sems):
            cp.wait()

    return _Exchange(
        grads, [jax.ShapeDtypeStruct((4,) + g.shape[1:], g.dtype) for g in grads],
        [pltpu.SemaphoreType.DMA((4 * n,)), pltpu.SemaphoreType.DMA((4 * n,))], start, finish)


def _chip_exchange(parts):
    n = len(parts)

    def copies(ins, outs, sems):
        send_sems, recv_sems = sems
        x, y, c = _pos()
        return [pltpu.make_async_remote_copy(
            src_ref=ins[i].at[j - 1], dst_ref=outs[i].at[j - 1], send_sem=send_sems.at[3 * i + j - 1],
            recv_sem=recv_sems.at[3 * i + j - 1], device_id=(_flip(x, j & 2), _flip(y, j & 1), c),
            device_id_type=MESH) for i in range(n) for j in range(1, 4)]

    def start(ins, outs, sems):
        for cp in copies(ins, outs, sems):
            cp.start()

    def finish(ins, outs, sems):
        for cp in copies(ins, outs, sems):
            cp.wait()

    return _Exchange(
        parts, [jax.ShapeDtypeStruct(p.shape, p.dtype) for p in parts],
        [pltpu.SemaphoreType.DMA((3 * n,)), pltpu.SemaphoreType.DMA((3 * n,))], start, finish)


def _rs_chip_partial(place, grad, recv, name):
    _, R, C = grad.shape
    tr = _blk(R, 256)

    def body(place_ref, *refs):
        g_refs, r_refs = refs[:4], refs[4:8]
        own_ref, out_ref = refs[8:]
        own_ref[...] = g_refs[0][0] + r_refs[0][0]
        for j in range(1, 4):
            out_ref[j - 1] = (g_refs[j][0] + r_refs[j][0]).astype(out_ref.dtype)

    def g_map(j):
        return lambda i, p: (2 * jnp.bitwise_xor(p[0], j) + p[1], i, 0)

    def r_map(j):
        return lambda i, p: (jnp.bitwise_xor(p[0], j), i, 0)

    grid_spec = pltpu.PrefetchScalarGridSpec(
        num_scalar_prefetch=1, grid=(R // tr,),
        in_specs=[pl.BlockSpec((1, tr, C), g_map(j)) for j in range(4)]
        + [pl.BlockSpec((1, tr, C), r_map(j)) for j in range(4)],
        out_specs=[pl.BlockSpec((tr, C), lambda i, p: (i, 0)), pl.BlockSpec((3, tr, C), lambda i, p: (0, i, 0))])
    return pl.pallas_call(
        body, name=name, grid_spec=grid_spec,
        out_shape=[jax.ShapeDtypeStruct((R, C), F32), jax.ShapeDtypeStruct((3, R, C), _MXU)],
    )(place, *([grad] * 4), *([recv] * 4))


def _adamw_math(w, g, m, v):
    m = ADAM_B1 * m + (1.0 - ADAM_B1) * g
    v = ADAM_B2 * v + (1.0 - ADAM_B2) * (g * g)
    m_hat = m / (1.0 - ADAM_B1 ** ADAM_STEP)
    v_hat = v / (1.0 - ADAM_B2 ** ADAM_STEP)
    delta = -ADAM_LR * (m_hat / (jnp.sqrt(v_hat) + ADAM_EPS) + ADAM_WD * w)
    return delta, m, v


def _adamw(w, g, m, v, name):
    R, C = w.shape
    tr = _blk(R, 256)

    def body(w_ref, g_ref, m_ref, v_ref, d_ref, nm_ref, nv_ref):
        d, nm, nv = _adamw_math(w_ref[...], g_ref[...], m_ref[...], v_ref[...])
        d_ref[...] = d
        nm_ref[...] = nm
        nv_ref[...] = nv

    spec = pl.BlockSpec((tr, C), lambda i: (i, 0))
    return pl.pallas_call(
        body, name=name, grid=(R // tr,), in_specs=[spec] * 4, out_specs=[spec] * 3,
        out_shape=[jax.ShapeDtypeStruct((R, C), F32)] * 3,
    )(w, g, m, v)


def _rs_sum_adamw(owns, recvs, w, m, v, name):
    L, R, C = w.shape
    tr = _blk(R, 256)
    nr = R // tr

    def body(o0, o1, r0, r1, w_ref, m_ref, v_ref, g_ref, d_ref, nm_ref, nv_ref):
        def step(o_ref, r_ref):
            g = o_ref[...]
            for j in range(3):
                g = g + r_ref[j].astype(F32)
            d, nm, nv = _adamw_math(w_ref[0], g, m_ref[0], v_ref[0])
            g_ref[0] = g
            d_ref[0] = d
            nm_ref[0] = nm
            nv_ref[0] = nv

        pl.when(pl.program_id(0) == 0)(lambda: step(o0, r0))
        pl.when(pl.program_id(0) == 1)(lambda: step(o1, r1))

    def hold(layer):
        if layer == 0:
            return lambda l, i: i * (1 - l) + (nr - 1) * l
        return lambda l, i: i * l

    own_spec = [pl.BlockSpec((tr, C), functools.partial(lambda l, i, f: (f(l, i), 0), f=hold(k))) for k in range(2)]
    recv_spec = [pl.BlockSpec((3, tr, C), functools.partial(lambda l, i, f: (0, f(l, i), 0), f=hold(k)))
                 for k in range(2)]
    lay = pl.BlockSpec((1, tr, C), lambda l, i: (l, i, 0))
    return pl.pallas_call(
        body, name=name, grid=(L, nr),
        in_specs=own_spec + recv_spec + [lay] * 3, out_specs=[lay] * 4,
        out_shape=[jax.ShapeDtypeStruct((L, R, C), F32)] * 4,
    )(owns[0], owns[1], recvs[0], recvs[1], w, m, v)


def _silu(x):
    return x / (1.0 + jnp.exp(-x))


def _start(c, w_ada, b_cols, w_first, name):
    L, D, Ca = w_ada.shape
    ex = _gather_exchange([w_first])

    def body(c_ref, w_ref, b_ref, first_ref, call_ref, mod_ref, gathered_ref, part_ref, c_send, c_recv, m_send,
             m_recv, *ex_sems):
        ex.start([first_ref], [gathered_ref], ex_sems)
        _gather_small(c_ref, call_ref, c_send, c_recv)
        act = _silu(call_ref[...].reshape(NDEV, D)).astype(_MXU)
        for l in range(L):
            part_ref[:, l * Ca:(l + 1) * Ca] = _nn(act, w_ref[l].astype(_MXU)) + b_ref[l:l + 1, :]
        _gather_small(part_ref, mod_ref, m_send, m_recv)
        ex.middle([first_ref], [gathered_ref], ex_sems)
        ex.finish([first_ref], [gathered_ref], ex_sems)

    pairs = [pltpu.SemaphoreType.DMA((NDEV - 1,))] * 4
    return pl.pallas_call(
        body, name=name,
        out_shape=[jax.ShapeDtypeStruct((NDEV, 1, D), F32), jax.ShapeDtypeStruct((NDEV, NDEV, L * Ca), F32)]
        + ex.out_shapes,
        in_specs=[VMEM_SPEC] * 3 + [ANY_SPEC], out_specs=[VMEM_SPEC, VMEM_SPEC, ANY_SPEC],
        scratch_shapes=[pltpu.VMEM((NDEV, L * Ca), F32)] + pairs + ex.sems,
    )(c, w_ada, b_cols, w_first)


def _w_ada_grad(c_t, dmod_cols, name):
    L, _, Ca = dmod_cols.shape
    D = c_t.shape[0]

    def body(c_ref, d_ref, o_ref):
        act = _silu(c_ref[...]).astype(_MXU)
        for l in range(L):
            o_ref[l] = _nn(act, d_ref[l].astype(_MXU))

    return pl.pallas_call(
        body, name=name, out_shape=jax.ShapeDtypeStruct((L, D, Ca), F32),
        in_specs=[VMEM_SPEC] * 2, out_specs=VMEM_SPEC,
    )(c_t, dmod_cols)


def _sum_devices(gathered, name):
    _, _, N = gathered.shape

    def body(x_ref, o_ref):
        acc = x_ref[0]
        for d in range(1, NDEV):
            acc = acc + x_ref[d]
        o_ref[...] = acc

    return pl.pallas_call(
        body, name=name, out_shape=jax.ShapeDtypeStruct((1, N), F32),
        in_specs=[VMEM_SPEC], out_specs=VMEM_SPEC,
    )(gathered)


def _ln_mod_matmul(x, g, sc, sh, w, name):
    S, D = x.shape
    N = w.shape[1]
    tm = _blk(S, ROW_BLOCK)

    def body(x_ref, g_ref, sc_ref, sh_ref, w_ref, o_ref, h_ref):
        xv = x_ref[...]
        r = lax.rsqrt(jnp.mean(xv * xv, axis=-1, keepdims=True) + EPS)
        hv = ((xv * r) * g_ref[...]) * (1.0 + sc_ref[...]) + sh_ref[...]
        hb = hv.astype(_MXU)
        h_ref[...] = hb
        o_ref[...] = _nn(hb, w_ref[...]).astype(o_ref.dtype)

    vec = pl.BlockSpec((1, D), lambda i: (0, 0))
    row = lambda width: pl.BlockSpec((tm, width), lambda i: (i, 0))
    return pl.pallas_call(
        body, name=name, grid=(S // tm,),
        in_specs=[row(D), vec, vec, vec, pl.BlockSpec((D, N), lambda i: (0, 0))],
        out_specs=[row(N), row(D)],
        out_shape=[jax.ShapeDtypeStruct((S, N), _MXU), jax.ShapeDtypeStruct((S, D), _MXU)],
    )(x, g, sc, sh, w)


def _matmul_res_gate(a, w, xres, gt, relu2, name, target=None):
    S, K = a.shape
    N = w.shape[1]
    tm = _blk(S, ROW_BLOCK)
    last = S // tm - 1
    with_loss = target is not None

    def body(a_ref, w_ref, x_ref, gt_ref, *rest):
        av = a_ref[...]
        if relu2:
            af = jnp.maximum(av.astype(F32), 0.0)
            av = (af * af).astype(_MXU)
        f = _nn(av, w_ref[...])
        out = x_ref[...] + gt_ref[...] * f
        if not with_loss:
            o_ref, f_ref = rest
            o_ref[...] = out
        else:
            t_ref, o_ref, f_ref, l_ref, acc_ref = rest
            i = pl.program_id(0)
            e = out - t_ref[...]
            o_ref[...] = e * (1.0 / N)
            _accumulate(acc_ref, jnp.sum(e * e, axis=0, keepdims=True), i == 0)

            @pl.when(i == last)
            def _():
                l_ref[...] = (0.5 / N) * jnp.sum(acc_ref[...], axis=1, keepdims=True)
        f_ref[...] = f.astype(f_ref.dtype)

    row = lambda width: pl.BlockSpec((tm, width), lambda i: (i, 0))
    in_specs = [row(K), pl.BlockSpec((K, N), lambda i: (0, 0)), row(N), pl.BlockSpec((1, N), lambda i: (0, 0))]
    out_specs = [row(N), row(N)]
    out_shape = [jax.ShapeDtypeStruct((S, N), F32), jax.ShapeDtypeStruct((S, N), _MXU)]
    args = [a, w, xres, gt]
    if with_loss:
        in_specs.append(row(N))
        args.append(target)
        out_specs.append(pl.BlockSpec((1, 1), lambda i: (0, 0)))
        out_shape.append(jax.ShapeDtypeStruct((1, 1), F32))
    return pl.pallas_call(
        body, name=name, grid=(S // tm,), in_specs=in_specs, out_specs=out_specs, out_shape=out_shape,
        scratch_shapes=[pltpu.VMEM((1, N), F32)] if with_loss else [],
    )(*args)


def _accumulate(ref, part, first):
    @pl.when(first)
    def _():
        ref[...] = part

    @pl.when(jnp.logical_not(first))
    def _():
        ref[...] += part


def _gate_nt_matmul(dx, f, gt, w, u, name):
    S, D = dx.shape
    N = w.shape[0]
    tm = _blk(S, ROW_BLOCK)
    with_u = u is not None

    def body(*refs):
        if with_u:
            dx_ref, f_ref, gt_ref, w_ref, u_ref, dz_ref, dgt_ref, res_ref = refs
        else:
            dx_ref, f_ref, gt_ref, w_ref, dz_ref, dgt_ref, res_ref = refs
        dxv = dx_ref[...]
        dz = (dxv * gt_ref[...]).astype(_MXU)
        dz_ref[...] = dz
        _accumulate(dgt_ref, jnp.sum(dxv * f_ref[...].astype(F32), axis=0, keepdims=True), pl.program_id(0) == 0)
        r = _nt(dz, w_ref[...])
        if with_u:
            r = r * (2.0 * jnp.maximum(u_ref[...].astype(F32), 0.0))
        res_ref[...] = r.astype(res_ref.dtype)

    row = lambda width: pl.BlockSpec((tm, width), lambda i: (i, 0))
    in_specs = [row(D), row(D), pl.BlockSpec((1, D), lambda i: (0, 0)), pl.BlockSpec((N, D), lambda i: (0, 0))]
    args = [dx, f, gt, w]
    if with_u:
        in_specs.append(row(N))
        args.append(u)
    return pl.pallas_call(
        body, name=name, grid=(S // tm,), in_specs=in_specs,
        out_specs=[row(D), pl.BlockSpec((1, D), lambda i: (0, 0)), row(N)],
        out_shape=[jax.ShapeDtypeStruct((S, D), _MXU), jax.ShapeDtypeStruct((1, D), F32),
                   jax.ShapeDtypeStruct((S, N), _MXU)],
    )(*args)


def _tn_matmul(a, b, by_col, relu2, name):
    S, Ka = a.shape
    Nb = b.shape[1]
    ts = _blk(S, 2 * ROW_BLOCK)
    half = NDEV // 2
    if by_col:
        R, C = Ka, Nb // NDEV
        a_spec = pl.BlockSpec((ts, Ka), lambda h, k: (k, 0))
        b_spec = pl.BlockSpec((ts, half * C), lambda h, k: (k, h))
    else:
        R, C = Ka // NDEV, Nb
        a_spec = pl.BlockSpec((ts, half * R), lambda h, k: (k, h))
        b_spec = pl.BlockSpec((ts, Nb), lambda h, k: (k, 0))

    def body(a_ref, b_ref, o_ref):
        av = a_ref[...]
        if relu2:
            af = jnp.maximum(av.astype(F32), 0.0)
            av = (af * af).astype(_MXU)
        p = _tn(av, b_ref[...])
        first = pl.program_id(1) == 0
        for d in range(half):
            part = p[:, d * C:(d + 1) * C] if by_col else p[d * R:(d + 1) * R, :]
            _accumulate(o_ref.at[d], part, first)

    return pl.pallas_call(
        body, name=name, grid=(NDEV // half, S // ts), in_specs=[a_spec, b_spec],
        out_specs=pl.BlockSpec((half, R, C), lambda h, k: (h, 0, 0)),
        out_shape=jax.ShapeDtypeStruct((NDEV, R, C), F32),
    )(a, b)


def _nt_ln_bwd(dy, w, x, g, sc, sh, dxres, name):
    S, D = x.shape
    N = w.shape[1]
    tm = _blk(S, ROW_BLOCK)

    def body(dy_ref, w_ref, x_ref, g_ref, sc_ref, sh_ref, dxr_ref, dx_ref, dsh_ref, dsc_ref, dg_ref):
        dh = _nt(dy_ref[...], w_ref[...])
        xv = x_ref[...]
        r = lax.rsqrt(jnp.mean(xv * xv, axis=-1, keepdims=True) + EPS)
        xhat = xv * r
        gv = g_ref[...]
        dn = dh * (1.0 + sc_ref[...])
        dxhat = dn * gv
        dxv = r * (dxhat - xhat * jnp.mean(dxhat * xhat, axis=-1, keepdims=True))
        dx_ref[...] = dxr_ref[...] + dxv
        first = pl.program_id(0) == 0
        _accumulate(dsh_ref, jnp.sum(dh, axis=0, keepdims=True), first)
        _accumulate(dsc_ref, jnp.sum(dh * (xhat * gv), axis=0, keepdims=True), first)
        _accumulate(dg_ref, jnp.sum(dn * xhat, axis=0, keepdims=True), first)

    row = lambda width: pl.BlockSpec((tm, width), lambda i: (i, 0))
    vec = pl.BlockSpec((1, D), lambda i: (0, 0))
    return pl.pallas_call(
        body, name=name, grid=(S // tm,),
        in_specs=[row(N), pl.BlockSpec((D, N), lambda i: (0, 0)), row(D), vec, vec, vec, row(D)],
        out_specs=[row(D), vec, vec, vec],
        out_shape=[jax.ShapeDtypeStruct((S, D), F32)] + [jax.ShapeDtypeStruct((1, D), F32)] * 3,
    )(dy, w, x, g, sc, sh, dxres)


def _split2(v):
    hi = v.astype(_MXU)
    mid = (v - hi.astype(F32)).astype(_MXU)
    return hi, mid


def _tri_sums(vs, tri2):
    T = vs[0].shape[0]
    out = []
    for j in range(len(vs) // 2):
        wide = [jnp.concatenate(_split2(vs[2 * j + e]), axis=1) for e in range(2)]
        for both in _per_head(_nn(jnp.concatenate(wide, axis=0), tri2), T):
            out.append((both[:, :T], both[:, T:]))
    return out


def _tri2(T, inclusive):
    j = lax.broadcasted_iota(jnp.int32, (2 * T, 2 * T), 0) % T
    s = lax.broadcasted_iota(jnp.int32, (2 * T, 2 * T), 1)
    keep = (j >= s) if inclusive else (j > s)
    return jnp.where((s >= T) | keep, 1.0, 0.0).astype(_MXU)


def _log_sigmoid(z):
    return jnp.minimum(z, 0.0) - jnp.log(1.0 + jnp.exp(-jnp.abs(z)))


def _per_head(tall, T):
    return [tall[h * T:(h + 1) * T] for h in range(tall.shape[0] // T)]


def _sb_blocks(q_tall, k2, strict, tri2, carry):
    T = k2[0].shape[0]
    zs = []
    for qt, kblk in zip(q_tall, k2):
        zs += _per_head(_nt(qt, kblk), T)
    lbs, l1s = [], []
    for z in zs:
        lb = _log_sigmoid(z)
        l1 = lb - z
        if strict is not None:
            l1 = jnp.where(strict, l1, 0.0)
        lbs.append(lb)
        l1s.append(l1)
    sums = _tri_sums(l1s, tri2)
    amps, new_carry = [], []
    for lb, (sfx, tot), c in zip(lbs, sums, carry):
        a = jnp.exp(lb + sfx + c)
        if strict is not None:
            a = jnp.where(strict, a, 0.0)
        amps.append(a)
        new_carry.append(c + tot)
    return lbs, amps, new_carry


def _sb_alive(carry):
    top = carry[0]
    for c in carry[1:]:
        top = jnp.maximum(top, c)
    return jnp.max(top) > SB_SKIP


def _skew_index():
    i = lax.broadcasted_iota(jnp.int32, (CA_T, SKEW_W + 1), 0)
    m = lax.broadcasted_iota(jnp.int32, (CA_T, SKEW_W + 1), 1)
    wrapped = i + m >= SKEW_W
    row = jnp.where(wrapped, i + 1, i)
    j = jnp.where(wrapped, i + m - SKEW_W, i + m)
    a = row // CHUNK
    jj = j - a * CHUNK
    inband = (jj >= 0) & (jj < BAND) & (j < CA_W) & (row < CA_T)
    idx = jnp.clip((row - a * CHUNK) + PAD - jj, -REL_CLIP, REL_CLIP) + REL_CLIP
    return inband, idx, wrapped


def _skew(tile):
    H = tile.shape[0]
    flat = jnp.pad(tile, ((0, 0), (0, 0), (0, SKEW_W - CA_W))).reshape(H, CA_T * SKEW_W)
    return jnp.pad(flat, ((0, 0), (0, CA_T))).reshape(H, CA_T, SKEW_W + 1)


def _unskew(view):
    H = view.shape[0]
    flat = view.reshape(H, CA_T * (SKEW_W + 1))[:, :CA_T * SKEW_W]
    return flat.reshape(H, CA_T, SKEW_W)[:, :, :CA_W]


def _ca_bias(rel_bias, name):
    H = rel_bias.shape[0]
    top = rel_bias[:, N_REL - 1:]
    by_offset = jnp.concatenate(
        [jnp.broadcast_to(top, (H, PAD - REL_CLIP + 1)), jnp.flip(rel_bias[:, :N_REL - 1], axis=1),
         jnp.broadcast_to(top, (H, SKEW_W + 1 - (PAD - REL_CLIP + 1) - (N_REL - 1)))], axis=1)

    def body(t_ref, o_ref):
        inband, _, wrapped = _skew_index()
        vals = jnp.where(wrapped, t_ref[0][:, 0:1], t_ref[0])
        o_ref[0] = jnp.where(inband, vals, NEG)

    view = pl.pallas_call(
        body, name=name, grid=(H,), in_specs=[pl.BlockSpec((1, 1, SKEW_W + 1), lambda h: (h, 0, 0))],
        out_specs=pl.BlockSpec((1, CA_T, SKEW_W + 1), lambda h: (h, 0, 0)),
        out_shape=jax.ShapeDtypeStruct((H, CA_T, SKEW_W + 1), F32),
    )(by_offset.reshape(H, 1, SKEW_W + 1))
    return _unskew(view)


def _ca_bias_bwd(dbias, name):
    H = dbias.shape[0]

    def body(d_ref, o_ref):
        inband, idx, _ = _skew_index()
        d = jnp.where(inband, d_ref[0], 0.0)
        clipped = idx == N_REL - 1
        by_offset = jnp.sum(jnp.where(clipped, 0.0, d), axis=0, keepdims=True)
        top = jnp.sum(jnp.sum(jnp.where(clipped, d, 0.0), axis=0, keepdims=True), axis=1, keepdims=True)
        lane = lax.broadcasted_iota(jnp.int32, (1, SKEW_W + 1), 1)
        o_ref[0] = jnp.where(lane == 0, top, by_offset)

    out = pl.pallas_call(
        body, name=name, grid=(H,), in_specs=[pl.BlockSpec((1, CA_T, SKEW_W + 1), lambda h: (h, 0, 0))],
        out_specs=pl.BlockSpec((1, 1, SKEW_W + 1), lambda h: (h, 0, 0)),
        out_shape=jax.ShapeDtypeStruct((H, 1, SKEW_W + 1), F32),
    )(_skew(dbias))[:, 0]
    first = PAD - REL_CLIP + 1
    return jnp.concatenate([jnp.flip(out[:, first:first + N_REL - 1], axis=1), out[:, 0:1]], axis=1)


def _low_lanes(rows):
    return lax.broadcasted_iota(jnp.int32, (rows, PAIR), 1) < HEAD_DIM


def _one_head(t2, low, first, scale=1.0):
    tf = t2.astype(F32) * scale
    return (jnp.where(low, tf, 0.0) if first else jnp.where(low, 0.0, tf)).astype(_MXU)


def _two_heads(t2, low, scale=1.0):
    return jnp.concatenate([_one_head(t2, low, True, scale), _one_head(t2, low, False, scale)], axis=0)


def _sb_fwd(proj, name, ex=None):
    S, W = proj.shape
    half = W // 6
    npair = half // PAIR
    T = _blk(S, SB_T)
    GP = _blk(npair, SB_PAIRS)
    GW = GP * PAIR
    nb = npair // GP

    def body(q_ref, k_ref, v_ref, o_ref, ox_ref):
        qi = pl.program_id(1)
        low = _low_lanes(T)
        q_tall = []
        for j in range(GP):
            q2 = q_ref[:, j * PAIR:(j + 1) * PAIR]
            q_tall.append(_two_heads(q2, low, HEAD_DIM ** -0.5))
        row = lax.broadcasted_iota(jnp.int32, (T, T), 0)
        col = lax.broadcasted_iota(jnp.int32, (T, T), 1)
        tri2 = _tri2(T, inclusive=False)

        def pairs(kb, carry, acc, fine, strict):
            rows = pl.ds(pl.multiple_of(kb * T, T), T)
            k2 = [k_ref[rows, j * PAIR:(j + 1) * PAIR] for j in range(GP)]
            v2 = [v_ref[rows, j * PAIR:(j + 1) * PAIR] for j in range(GP)]
            _, amps, carry = _sb_blocks(q_tall, k2, strict, tri2, carry)
            parts = [_split2(a) for a in amps]
            new_acc, new_fine = [], []
            for j in range(GP):
                tall = jnp.concatenate([parts[2 * j][0], parts[2 * j + 1][0], parts[2 * j][1], parts[2 * j + 1][1]],
                                       axis=0)
                hi0, hi1, mid0, mid1 = _per_head(_nn(tall, v2[j]), T)
                new_acc.append(acc[j] + jnp.where(low, hi0, hi1))
                new_fine.append(fine[j] + jnp.where(low, mid0, mid1))
            return tuple(carry), tuple(new_acc), tuple(new_fine)

        zero = (jnp.zeros((T, PAIR), F32),) * GP
        carry, acc, fine = pairs(qi, (jnp.zeros((T, T), F32),) * (2 * GP), zero, zero, col < row)

        def cond(st):
            kb, alive, _, _, _ = st
            return jnp.logical_and(kb >= 0, alive)

        def step(st):
            kb, _, carry, acc, fine = st
            carry, acc, fine = pairs(kb, carry, acc, fine, None)
            return kb - 1, _sb_alive(carry), carry, acc, fine

        _, _, _, acc, fine = lax.while_loop(cond, step, (qi - 1, _sb_alive(carry), carry, acc, fine))
        for j in range(GP):
            o_ref[:, j * PAIR:(j + 1) * PAIR] = acc[j].astype(o_ref.dtype)
            ox_ref[:, j * PAIR:(j + 1) * PAIR] = acc[j] + fine[j]

    blk = pl.BlockSpec((T, GW), lambda p, i: (i, p))
    return _call_hosted(
        body, name, (nb, S // T),
        [blk, pl.BlockSpec((S, GW), lambda p, i: (0, nb + p)), pl.BlockSpec((S, GW), lambda p, i: (0, 2 * nb + p))],
        [blk, blk], [jax.ShapeDtypeStruct((S, half), _MXU), jax.ShapeDtypeStruct((S, half), F32)],
        [], [proj, proj, proj], ex)


def _sb_bwd(proj, ox, dmixed, name, ex=None):
    S, W = proj.shape
    half = W // 6
    npair = half // PAIR
    T = _blk(S, SB_T)
    GP = _blk(npair, SB_PAIRS)
    GW = GP * PAIR
    nb = npair // GP
    last = S // T - 1
    scale = HEAD_DIM ** -0.5

    def body(q_ref, k_ref, v_ref, ox_ref, do_ref, dq_ref, dk_ref, dv_ref, dka_ref, dva_ref):
        qi = pl.program_id(1)

        @pl.when(qi == 0)
        def _():
            dka_ref[...] = jnp.zeros_like(dka_ref)
            dva_ref[...] = jnp.zeros_like(dva_ref)

        low = _low_lanes(T)
        q2, do2, q_tall, do_tall, deltas = [], [], [], [], []
        for j in range(GP):
            cols = slice(j * PAIR, (j + 1) * PAIR)
            q2.append(q_ref[:, cols])
            do2.append(do_ref[:, cols])
            q_tall.append(_two_heads(q2[j], low, scale))
            dobs = [_one_head(do2[j], low, True), _one_head(do2[j], low, False)]
            do_tall.append(jnp.concatenate(dobs, axis=0))
            for e in range(2):
                deltas.append(jnp.sum(dobs[e].astype(F32) * ox_ref[:, cols], axis=-1, keepdims=True))
        row = lax.broadcasted_iota(jnp.int32, (T, T), 0)
        col = lax.broadcasted_iota(jnp.int32, (T, T), 1)
        tri_ex = _tri2(T, inclusive=False)
        tri_in = _tri2(T, inclusive=True)

        def pairs(kb, carry, right, dq, strict):
            rows = pl.ds(pl.multiple_of(kb * T, T), T)
            k2 = [k_ref[rows, j * PAIR:(j + 1) * PAIR] for j in range(GP)]
            v2 = [v_ref[rows, j * PAIR:(j + 1) * PAIR] for j in range(GP)]
            nh = 2 * GP
            gs = []
            for j in range(GP):
                gs += _per_head(_nt(do_tall[j], v2[j]), T)
            lbs, amps, carry = _sb_blocks(q_tall, k2, strict, tri_ex, carry)
            ags = [a * gg for a, gg in zip(amps, gs)]
            sums = _tri_sums(ags, tri_in)
            dzbs = []
            for h in range(nh):
                left = deltas[h] - (sums[h][0] + right[h])
                beta = jnp.exp(lbs[h])
                dz = ags[h] - beta * (ags[h] + left)
                if strict is not None:
                    dz = jnp.where(strict, dz, 0.0)
                dzbs.append(dz.astype(_MXU))
            abs_ = [a.astype(_MXU) for a in amps]
            new_dq = []
            for j in range(GP):
                cols = slice(j * PAIR, (j + 1) * PAIR)
                dk0, dk1 = _per_head(_tn(jnp.concatenate(dzbs[2 * j:2 * j + 2], axis=1), q2[j]), T)
                dv0, dv1 = _per_head(_tn(jnp.concatenate(abs_[2 * j:2 * j + 2], axis=1), do2[j]), T)
                dq0, dq1 = _per_head(_nn(jnp.concatenate(dzbs[2 * j:2 * j + 2], axis=0), k2[j]), T)
                dka_ref[rows, cols] += jnp.where(low, dk0, dk1)
                dva_ref[rows, cols] += jnp.where(low, dv0, dv1)
                new_dq.append(dq[j] + jnp.where(low, dq0, dq1))
            right = tuple(right[h] + sums[h][1] for h in range(nh))
            return tuple(carry), right, tuple(new_dq)

        zero = (jnp.zeros((T, T), F32),) * (2 * GP)
        carry, right, dq = pairs(qi, zero, zero, (jnp.zeros((T, PAIR), F32),) * GP, col < row)

        def cond(st):
            kb, alive, _, _, _ = st
            return jnp.logical_and(kb >= 0, alive)

        def step(st):
            kb, _, carry, right, dq = st
            carry, right, dq = pairs(kb, carry, right, dq, None)
            return kb - 1, _sb_alive(carry), carry, right, dq

        _, _, _, _, dq = lax.while_loop(cond, step, (qi - 1, _sb_alive(carry), carry, right, dq))
        for j in range(GP):
            dq_ref[:, j * PAIR:(j + 1) * PAIR] = (dq[j] * scale).astype(dq_ref.dtype)

        @pl.when(qi == last)
        def _():
            dk_ref[...] = (dka_ref[...] * scale).astype(dk_ref.dtype)
            dv_ref[...] = dva_ref[...].astype(dv_ref.dtype)

    blk = pl.BlockSpec((T, GW), lambda p, i: (i, p))
    full = pl.BlockSpec((S, GW), lambda p, i: (0, p))
    return _call_hosted(
        body, name, (nb, S // T),
        [blk, pl.BlockSpec((S, GW), lambda p, i: (0, nb + p)), pl.BlockSpec((S, GW), lambda p, i: (0, 2 * nb + p)),
         blk, blk],
        [blk, full, full], [jax.ShapeDtypeStruct((S, half), _MXU)] * 3,
        [pltpu.VMEM((S, GW), F32), pltpu.VMEM((S, GW), F32)], [proj, proj, proj, ox, dmixed], ex)


def _pair_norm(t2, g2, low):
    tf = t2.astype(F32)
    sq = tf * tf
    both = jnp.sum(sq, axis=-1, keepdims=True)
    first = jnp.sum(jnp.where(low, sq, 0.0), axis=-1, keepdims=True)
    r = jnp.where(low, lax.rsqrt(first * (1.0 / HEAD_DIM) + EPS), lax.rsqrt((both - first) * (1.0 / HEAD_DIM) + EPS))
    hat = tf * r
    return hat * g2, hat, r


def _pair_norm_bwd(dn, hat, r, g2, low):
    dhat = dn * g2
    prod = dhat * hat
    both = jnp.sum(prod, axis=-1, keepdims=True)
    first = jnp.sum(jnp.where(low, prod, 0.0), axis=-1, keepdims=True)
    mean = jnp.where(low, first, both - first) * (1.0 / HEAD_DIM)
    return r * (dhat - hat * mean)


def _ca_fill(j, k_ref, v_ref, gk_ref, kn_ref, vp_ref):
    S = k_ref.shape[0]
    cols = slice(j * PAIR, (j + 1) * PAIR)
    kn, _, _ = _pair_norm(k_ref[:, cols], gk_ref[...], _low_lanes(S))
    kn_ref[j, 0:PAD, :] = jnp.zeros((PAD, PAIR), kn_ref.dtype)
    vp_ref[j, 0:PAD, :] = jnp.zeros((PAD, PAIR), vp_ref.dtype)
    kn_ref[j, PAD:PAD + S, :] = kn.astype(kn_ref.dtype)
    vp_ref[j, PAD:PAD + S, :] = v_ref[:, cols]


def _ca_scores(j, q_ref, b2_ref, gq_ref, kn_ref, qi, low):
    qn, qhat, r = _pair_norm(q_ref[:, j * PAIR:(j + 1) * PAIR], gq_ref[...], low)
    qn = qn * HEAD_DIM ** -0.5
    band = pl.ds(pl.multiple_of(qi * CA_T, CA_T), CA_W)
    key_pos = qi * CA_T - PAD + lax.broadcasted_iota(jnp.int32, (CA_T, CA_W), 1)
    both = _per_head(_nt(_two_heads(qn, low), kn_ref[j, band, :]), CA_T)
    scores = [jnp.where(key_pos >= 0, both[e] + b2_ref[2 * j + e], NEG) for e in range(2)]
    return scores, qn.astype(_MXU), qhat, r


def _softmax(s):
    e = jnp.exp(s - jnp.max(s, axis=-1, keepdims=True))
    return e * (1.0 / jnp.sum(e, axis=-1, keepdims=True))


def _ca_fwd(proj, bias2, gq2, gk2, name, ex=None):
    S, W = proj.shape
    half = W // 6
    npair = half // PAIR
    GP = _blk(npair, CA_PAIRS_FWD)
    GW = GP * PAIR
    nb = npair // GP

    def body(q_ref, k_ref, v_ref, b2_ref, gq_ref, gk_ref, o_ref, kn_ref, vp_ref):
        qi = pl.program_id(1)

        @pl.when(qi == 0)
        def _():
            for j in range(GP):
                _ca_fill(j, k_ref, v_ref, gk_ref, kn_ref, vp_ref)

        low = _low_lanes(CA_T)
        band = pl.ds(pl.multiple_of(qi * CA_T, CA_T), CA_W)
        scores = [_ca_scores(j, q_ref, b2_ref, gq_ref, kn_ref, qi, low)[0] for j in range(GP)]
        probs = [[_softmax(s).astype(_MXU) for s in pair] for pair in scores]
        for j in range(GP):
            outs = _per_head(_nn(jnp.concatenate(probs[j], axis=0), vp_ref[j, band, :]), CA_T)
            o_ref[:, j * PAIR:(j + 1) * PAIR] = jnp.where(low, outs[0], outs[1]).astype(o_ref.dtype)

    vec = pl.BlockSpec((1, PAIR), lambda p, i: (0, 0))
    return _call_hosted(
        body, name, (nb, S // CA_T),
        [pl.BlockSpec((CA_T, GW), lambda p, i: (i, 3 * nb + p)),
         pl.BlockSpec((S, GW), lambda p, i: (0, 4 * nb + p)), pl.BlockSpec((S, GW), lambda p, i: (0, 5 * nb + p)),
         pl.BlockSpec((2 * GP, CA_T, CA_W), lambda p, i: (p, 0, 0)), vec, vec],
        [pl.BlockSpec((CA_T, GW), lambda p, i: (i, p))], [jax.ShapeDtypeStruct((S, half), _MXU)],
        [pltpu.VMEM((GP, PAD + S, PAIR), _MXU), pltpu.VMEM((GP, PAD + S, PAIR), _MXU)],
        [proj, proj, proj, bias2, gq2, gk2], ex)


def _ca_bwd(proj, bias2, gq2, gk2, dmixed, name, ex=None):
    S, W = proj.shape
    half = W // 6
    npair = half // PAIR
    GP = _blk(npair, CA_PAIRS_BWD)
    GW = GP * PAIR
    nb = npair // GP
    scale = HEAD_DIM ** -0.5
    last = S // CA_T - 1

    def body(q_ref, k_ref, v_ref, b2_ref, gq_ref, gk_ref, do_ref,
             dq_ref, dk_ref, dv_ref, db_ref, dgq_ref, dgk_ref, kn_ref, vp_ref, dkn_ref, dvp_ref):
        p_id, qi = pl.program_id(0), pl.program_id(1)

        @pl.when(qi == 0)
        def _():
            for j in range(GP):
                _ca_fill(j, k_ref, v_ref, gk_ref, kn_ref, vp_ref)
            dkn_ref[...] = jnp.zeros_like(dkn_ref)
            dvp_ref[...] = jnp.zeros_like(dvp_ref)
            db_ref[...] = jnp.zeros_like(db_ref)

        @pl.when(jnp.logical_and(p_id == 0, qi == 0))
        def _():
            dgq_ref[...] = jnp.zeros_like(dgq_ref)
            dgk_ref[...] = jnp.zeros_like(dgk_ref)

        low = _low_lanes(CA_T)
        top_w = lax.broadcasted_iota(jnp.int32, (PAIR, CA_W), 0) < HEAD_DIM
        band = pl.ds(pl.multiple_of(qi * CA_T, CA_T), CA_W)
        pairs = [_ca_scores(j, q_ref, b2_ref, gq_ref, kn_ref, qi, low) for j in range(GP)]
        do2 = [do_ref[:, j * PAIR:(j + 1) * PAIR] for j in range(GP)]
        dps = [_per_head(_nt(_two_heads(do2[j], low), vp_ref[j, band, :]), CA_T) for j in range(GP)]
        probs, dsbs = [], []
        for j in range(GP):
            pj, dj = [], []
            for e in range(2):
                p = _softmax(pairs[j][0][e])
                ds = p * (dps[j][e] - jnp.sum(p * dps[j][e], axis=-1, keepdims=True))
                db_ref[2 * j + e] += ds
                pj.append(p.astype(_MXU))
                dj.append(ds.astype(_MXU))
            probs.append(pj)
            dsbs.append(dj)
        dgq = jnp.zeros((1, PAIR), F32)
        for j in range(GP):
            _, qn, qhat, r = pairs[j]
            dq_h = _per_head(_nn(jnp.concatenate(dsbs[j], axis=0), kn_ref[j, band, :]), CA_T)
            dk_t = _tn(qn, jnp.concatenate(dsbs[j], axis=1))
            dv_t = _tn(do2[j], jnp.concatenate(probs[j], axis=1))
            dkn_ref[j, :, band] += jnp.where(top_w, dk_t[:, :CA_W], dk_t[:, CA_W:])
            dvp_ref[j, :, band] += jnp.where(top_w, dv_t[:, :CA_W], dv_t[:, CA_W:])
            dqn = jnp.where(low, dq_h[0], dq_h[1]) * scale
            dgq = dgq + jnp.sum(dqn * qhat, axis=0, keepdims=True)
            dq_ref[:, j * PAIR:(j + 1) * PAIR] = _pair_norm_bwd(dqn, qhat, r, gq_ref[...], low).astype(dq_ref.dtype)
        dgq_ref[...] += dgq

        @pl.when(qi == last)
        def _():
            low_s = _low_lanes(S)
            for j in range(GP):
                cols = slice(j * PAIR, (j + 1) * PAIR)
                _, khat, rk = _pair_norm(k_ref[:, cols], gk_ref[...], low_s)
                dkn = dkn_ref[j, :, PAD:PAD + S].T
                dgk_ref[...] += jnp.sum(dkn * khat, axis=0, keepdims=True)
                dk_ref[:, cols] = _pair_norm_bwd(dkn, khat, rk, gk_ref[...], low_s).astype(dk_ref.dtype)
                dv_ref[:, cols] = dvp_ref[j, :, PAD:PAD + S].T.astype(dv_ref.dtype)

    vec = pl.BlockSpec((1, PAIR), lambda p, i: (0, 0))
    tile = pl.BlockSpec((2 * GP, CA_T, CA_W), lambda p, i: (p, 0, 0))
    full = pl.BlockSpec((S, GW), lambda p, i: (0, p))
    return _call_hosted(
        body, name, (nb, S // CA_T),
        [pl.BlockSpec((CA_T, GW), lambda p, i: (i, 3 * nb + p)),
         pl.BlockSpec((S, GW), lambda p, i: (0, 4 * nb + p)), pl.BlockSpec((S, GW), lambda p, i: (0, 5 * nb + p)),
         tile, vec, vec, pl.BlockSpec((CA_T, GW), lambda p, i: (i, nb + p))],
        [pl.BlockSpec((CA_T, GW), lambda p, i: (i, p)), full, full, tile, vec, vec],
        [jax.ShapeDtypeStruct((S, half), _MXU)] * 3
        + [jax.ShapeDtypeStruct(bias2.shape, F32), jax.ShapeDtypeStruct((1, PAIR), F32),
           jax.ShapeDtypeStruct((1, PAIR), F32)],
        [pltpu.VMEM((GP, PAD + S, PAIR), _MXU), pltpu.VMEM((GP, PAD + S, PAIR), _MXU),
         pltpu.VMEM((GP, PAIR, PAD + S), F32), pltpu.VMEM((GP, PAIR, PAD + S), F32)],
        [proj, proj, proj, bias2, gq2, gk2, dmixed], ex)


def _pack_small(parts):
    flat = jnp.concatenate([p.reshape(-1) for layer in parts for p in layer])
    n = flat.shape[0]
    n_pad = -(-n // 1024) * 1024
    return jnp.pad(flat, (0, n_pad - n)).reshape(1, n_pad)


def _unpack_small(flat, shapes):
    out, off = [], 0
    for layer in shapes:
        cur = []
        for shp in layer:
            size = 1
            for s in shp:
                size *= s
            cur.append(flat[off:off + size].reshape(shp))
            off += size
        out.append(cur)
    return out


def kernel(x, c, g_norm1, w_in, g_q, g_k, rel_bias, w_o, g_norm2, w1, w2, w_ada, b_ada, loss_target, m_g_norm1, m_w_in, m_g_q, m_g_k, m_rel_bias, m_w_o, m_g_norm2, m_w1, m_w2, m_w_ada, m_b_ada, v_g_norm1, v_w_in, v_g_q, v_g_k, v_rel_bias, v_w_o, v_g_norm2, v_w1, v_w2, v_w_ada, v_b_ada):
    L = w_in.shape[0]
    S, D = x.shape[1:]
    H2 = D // HEAD_DIM // 2
    Ca = w_ada.shape[2]
    xi, yi, ci = _pos()
    me = 4 * xi + 2 * yi + ci
    place = jnp.stack([2 * xi + yi, ci]).astype(jnp.int32)

    wire = lambda a: a.astype(_MXU)
    by_cols = lambda g: g.transpose(1, 0, 2).reshape(D, g.shape[0] * g.shape[2])
    b_cols = lax.dynamic_slice(b_ada, (0, me * Ca), (L, Ca))
    c_all, mod_all, first = _start(c, w_ada, b_cols, wire(w_in[0]), "start")
    c_all = c_all.reshape(NDEV, D)
    mod = lax.dynamic_index_in_dim(mod_all, me, axis=1, keepdims=False)
    mod = mod.reshape(NDEV, L, Ca).transpose(1, 0, 2).reshape(L, 6, 1, D)
    W_in = {0: by_cols(first)}
    W_o, W_1, W_2 = {}, {}, {}

    xs = [x[0]]
    saved = []
    for l in range(L):
        sh1, sc1, gt1, sh2, sc2, gt2 = [mod[l, i] for i in range(6)]
        gn1, gn2 = g_norm1[l:l + 1], g_norm2[l:l + 1]
        gq2, gk2 = jnp.tile(g_q[l:l + 1], (1, 2)), jnp.tile(g_k[l:l + 1], (1, 2))
        proj, h1 = _ln_mod_matmul(xs[-1], gn1, sc1, sh1, W_in[l], f"l{l}_proj")
        (o_sb, ox_sb), got = _sb_fwd(proj, f"l{l}_sb_fwd", _gather_exchange([wire(w1[l]), wire(w2[l])]))
        W_1[l], W_2[l] = by_cols(got[0]), got[1].reshape(4 * D, D)
        bias2 = _ca_bias(rel_bias[l], f"l{l}_ca_bias")
        nxt = [wire(w_in[l + 1])] if l + 1 < L else []
        (o_ca,), got = _ca_fwd(proj, bias2, gq2, gk2, f"l{l}_ca_fwd", _gather_exchange([wire(w_o[l])] + nxt))
        W_o[l] = got[0].reshape(D, D)
        if nxt:
            W_in[l + 1] = by_cols(got[1])
        mixed = jnp.concatenate([o_sb, o_ca], axis=1)
        x1, f1 = _matmul_res_gate(mixed, W_o[l], xs[-1], gt1, False, f"l{l}_attn_out")
        u, h2 = _ln_mod_matmul(x1, gn2, sc2, sh2, W_1[l], f"l{l}_mlp_in")
        x0 = xs[-1]
        if l + 1 < L:
            x2, f2 = _matmul_res_gate(u, W_2[l], x1, gt2, True, f"l{l}_mlp_out")
            xs.append(x2)
        else:
            dx, f2, loss_part = _matmul_res_gate(u, W_2[l], x1, gt2, True, f"l{l}_mlp_out", loss_target[0])
        saved.append(dict(x0=x0, h1=h1, proj=proj, ox_sb=ox_sb, bias2=bias2, mixed=mixed, f1=f1, x1=x1,
                          h2=h2, u=u, f2=f2))

    owns, recv_b = {}, {}
    ready = []
    small_parts = [None] * L

    def partials(keys, grads, recv_a):
        parts = []
        for key, g, r in zip(keys, grads, recv_a):
            owns[key], part = _rs_chip_partial(place, g, r, f"rs_partial_l{key[0]}_{key[1]}")
            parts.append(part)
        return parts

    for l in reversed(range(L)):
        sv = saved[l]
        sh1, sc1, gt1, sh2, sc2, gt2 = [mod[l, i] for i in range(6)]
        gn1, gn2 = g_norm1[l:l + 1], g_norm2[l:l + 1]
        gq2, gk2 = jnp.tile(g_q[l:l + 1], (1, 2)), jnp.tile(g_k[l:l + 1], (1, 2))
        dz2, dgt2, du = _gate_nt_matmul(dx, sv["f2"], gt2, W_2[l], sv["u"], f"l{l}_mlp_out_bwd")
        gw2 = _tn_matmul(sv["u"], dz2, False, True, f"l{l}_gw2")
        gw1 = _tn_matmul(sv["h2"], du, True, False, f"l{l}_gw1")
        dx, dsh2, dsc2, dgn2 = _nt_ln_bwd(du, W_1[l], sv["x1"], gn2, sc2, sh2, dx, f"l{l}_mlp_in_bwd")
        dz1, dgt1, dmixed = _gate_nt_matmul(dx, sv["f1"], gt1, W_o[l], None, f"l{l}_attn_out_bwd")
        gwo = _tn_matmul(sv["mixed"], dz1, False, False, f"l{l}_gwo")
        ready += [((l, 1), gwo), ((l, 2), gw1), ((l, 3), gw2)]
        keys, grads = [k for k, _ in ready], [g for _, g in ready]
        (dq_sb, dk_sb, dv_sb), recv_a = _sb_bwd(sv["proj"], sv["ox_sb"], dmixed, f"l{l}_sb_bwd",
                                                _sibling_exchange(grads))
        parts = partials(keys, grads, recv_a)
        (dq_ca, dk_ca, dv_ca, dbias2, dgq2, dgk2), got = _ca_bwd(sv["proj"], sv["bias2"], gq2, gk2, dmixed,
                                                                 f"l{l}_ca_bwd", _chip_exchange(parts))
        recv_b.update(zip(keys, got))
        dgq = dgq2[:, :HEAD_DIM] + dgq2[:, HEAD_DIM:]
        dgk = dgk2[:, :HEAD_DIM] + dgk2[:, HEAD_DIM:]
        drb = _ca_bias_bwd(dbias2, f"l{l}_ca_bias_bwd")
        dproj = jnp.concatenate([dq_sb, dk_sb, dv_sb, dq_ca, dk_ca, dv_ca], axis=1)
        gwin = _tn_matmul(sv["h1"], dproj, True, False, f"l{l}_gwin")
        ready = [((l, 0), gwin)]
        dx, dsh1, dsc1, dgn1 = _nt_ln_bwd(dproj, W_in[l], sv["x0"], gn1, sc1, sh1, dx, f"l{l}_proj_bwd")
        dmod = jnp.concatenate([dsh1, dsc1, dgt1, dsh2, dsc2, dgt2], axis=1)
        small_parts[l] = [dgn1, dgq, dgk, drb, dgn2, dmod]
    grad_x = dx[None]

    keys, grads = [k for k, _ in ready], [g for _, g in ready]
    parts = partials(keys, grads, _run_exchange(_sibling_exchange(grads), "rs_sibling_last"))
    recv_b.update(zip(keys, _run_exchange(_chip_exchange(parts), "rs_chips_last")))
    big_out = []
    for t, (w, m, v) in enumerate([(w_in, m_w_in, v_w_in), (w_o, m_w_o, v_w_o), (w1, m_w1, v_w1), (w2, m_w2, v_w2)]):
        big_out.append(_rs_sum_adamw([owns[(l, t)] for l in range(L)], [recv_b[(l, t)] for l in range(L)],
                                     w, m, v, f"adamw_big_{t}"))

    packed = _pack_small(small_parts)
    gathered_small = _all_gather_small(packed, "ag_small_grads")
    small_sum = _sum_devices(gathered_small, "sum_small_grads")
    shapes = [[(1, D), (1, HEAD_DIM), (1, HEAD_DIM), (H2, N_REL), (1, D), (1, 6 * D)]] * L
    names = ["g_norm1", "g_q", "g_k", "rel_bias", "g_norm2", "b_ada"]
    small_w = {"g_norm1": (g_norm1, m_g_norm1, v_g_norm1), "g_q": (g_q, m_g_q, v_g_q), "g_k": (g_k, m_g_k, v_g_k),
               "rel_bias": (rel_bias, m_rel_bias, v_rel_bias), "g_norm2": (g_norm2, m_g_norm2, v_g_norm2),
               "b_ada": (b_ada, m_b_ada, v_b_ada)}
    packs = [_pack_small([[small_w[n][k][l] for n in names] for l in range(L)]) for k in range(3)]
    n_pad = packed.shape[1]
    as_rows = lambda a: a.reshape(n_pad // 128, 128)
    sd, sm, sv_ = _adamw(as_rows(packs[0]), as_rows(small_sum), as_rows(packs[1]), as_rows(packs[2]), "adamw_small")
    small_out = {}
    for key, flat in [("grad", small_sum), ("delta", sd), ("m", sm), ("v", sv_)]:
        per_layer = _unpack_small(flat.reshape(-1), shapes)
        for i, n in enumerate(names):
            small_out[(key, n)] = jnp.stack([per_layer[l][i].reshape(small_w[n][0].shape[1:]) for l in range(L)])

    layer_len = 2 * D + 2 * HEAD_DIM + H2 * N_REL + 6 * D
    rows = gathered_small.reshape(NDEV, n_pad)
    dmod_all = jnp.stack([rows[:, l * layer_len + layer_len - 6 * D:(l + 1) * layer_len] for l in range(L)])
    dmod_cols = lax.dynamic_slice(dmod_all, (0, 0, me * Ca), (L, NDEV, Ca))
    dmod_cols = jnp.pad(dmod_cols, ((0, 0), (0, 128 - NDEV), (0, 0)))
    c_t = jnp.pad(c_all.T, ((0, 0), (0, 128 - NDEV)))
    g_ada = _w_ada_grad(c_t, dmod_cols, "w_ada_grad")
    flat2 = lambda a: a.reshape(L * D, Ca)
    ad, am, av = _adamw(flat2(w_ada), flat2(g_ada), flat2(m_w_ada), flat2(v_w_ada), "adamw_w_ada")
    ada_out = [g_ada] + [a.reshape(L, D, Ca) for a in (ad, am, av)]

    def leaf(kind):
        k = {"grad": 0, "delta": 1, "m": 2, "v": 3}[kind]
        return [small_out[(kind, "g_norm1")], big_out[0][k], small_out[(kind, "g_q")], small_out[(kind, "g_k")],
                small_out[(kind, "rel_bias")], big_out[1][k], small_out[(kind, "g_norm2")], big_out[2][k],
                big_out[3][k], ada_out[k], small_out[(kind, "b_ada")]]

    loss = lax.psum(loss_part[0, 0], ("x", "y", "c"))
    return (loss, grad_x, *leaf("grad"), *leaf("delta"), *leaf("m"), *leaf("v"))
```

```python
import functools

import jax
import jax.numpy as jnp
from jax import lax
from jax.experimental import pallas as pl
from jax.experimental.pallas import tpu as pltpu

F32 = jnp.float32
_MXU = jnp.bfloat16

HEAD_DIM = 64
CHUNK = 64
LEFT_CHUNKS = 8
PAD = LEFT_CHUNKS * CHUNK
BAND = PAD + CHUNK
REL_CLIP = 128
N_REL = 2 * REL_CLIP + 1
EPS = 1e-6
NEG = -1e30
NDEV = 8
SB_T = 128
CA_T = 2 * CHUNK
CA_W = CA_T + PAD
SB_SKIP = -104.0
PAIR = 2 * HEAD_DIM
SB_PAIRS = 4
CA_PAIRS_FWD = 4
CA_PAIRS_BWD = 2
ROW_BLOCK = 512
SKEW_W = CA_W + CA_T - 1

ADAM_LR, ADAM_B1, ADAM_B2, ADAM_EPS, ADAM_WD, ADAM_STEP = 0.001, 0.9, 0.999, 1e-08, 0.01, 10

MESH = pl.DeviceIdType.MESH
VMEM_SPEC = pl.BlockSpec(memory_space=pltpu.VMEM)
ANY_SPEC = pl.BlockSpec(memory_space=pl.ANY)


def _nn(a, b):
    return lax.dot_general(a, b, (((1,), (0,)), ((), ())), preferred_element_type=F32)


def _nt(a, b):
    return lax.dot_general(a, b, (((1,), (1,)), ((), ())), preferred_element_type=F32)


def _tn(a, b):
    return lax.dot_general(a, b, (((0,), (0,)), ((), ())), preferred_element_type=F32)


def _blk(n, pref):
    return pref if n % pref == 0 else n


def _pos():
    return lax.axis_index("x"), lax.axis_index("y"), lax.axis_index("c")


def _flip(v, bit):
    return 1 - v if bit else v


def _gather_small(x_ref, out_ref, send_sems, recv_sems):
    R, C = x_ref.shape
    x, y, c = _pos()
    me = 4 * x + 2 * y + c

    def peer(k):
        return (_flip(x, k & 4), _flip(y, k & 2), _flip(c, k & 1))

    def copy(k, slot):
        return pltpu.make_async_remote_copy(
            src_ref=x_ref, dst_ref=out_ref.at[slot], send_sem=send_sems.at[k - 1],
            recv_sem=recv_sems.at[k - 1], device_id=peer(k), device_id_type=MESH)

    out_ref[pl.ds(me, 1), :, :] = x_ref[...].reshape(1, R, C)
    sends = [copy(k, me) for k in range(1, NDEV)]
    for cp in sends:
        cp.start()
    for k in range(1, NDEV):
        px, py, pc = peer(k)
        copy(k, 4 * px + 2 * py + pc).wait_recv()
    for cp in sends:
        cp.wait_send()


def _all_gather_small(blk, name):
    return pl.pallas_call(
        lambda x_ref, out_ref, send_sems, recv_sems: _gather_small(x_ref, out_ref, send_sems, recv_sems), name=name,
        out_shape=jax.ShapeDtypeStruct((NDEV,) + blk.shape, blk.dtype),
        in_specs=[VMEM_SPEC], out_specs=VMEM_SPEC,
        scratch_shapes=[pltpu.SemaphoreType.DMA((NDEV - 1,)), pltpu.SemaphoreType.DMA((NDEV - 1,))],
    )(blk)


class _Exchange:
    def __init__(self, inputs, out_shapes, sems, start, finish, middle=None):
        self.inputs, self.out_shapes, self.sems = list(inputs), list(out_shapes), list(sems)
        self.start, self.middle, self.finish = start, middle, finish


def _run_exchange(ex, name):
    n_in, n_out = len(ex.inputs), len(ex.out_shapes)

    def body(*refs):
        ins, outs, sems = refs[:n_in], refs[n_in:n_in + n_out], refs[n_in + n_out:]
        ex.start(ins, outs, sems)
        if ex.middle is not None:
            ex.middle(ins, outs, sems)
        ex.finish(ins, outs, sems)

    return pl.pallas_call(
        body, name=name, out_shape=ex.out_shapes, in_specs=[ANY_SPEC] * n_in, out_specs=[ANY_SPEC] * n_out,
        scratch_shapes=ex.sems,
    )(*ex.inputs)


def _hosted(body, n_in, n_out, ex, step, steps):
    if ex is None:
        return body
    xi, xo = len(ex.inputs), len(ex.out_shapes)

    def wrapped(*refs):
        own_in, ex_in = refs[:n_in], refs[n_in:n_in + xi]
        rest = refs[n_in + xi:]
        own_out, ex_out = rest[:n_out], rest[n_out:n_out + xo]
        rest = rest[n_out + xo:]
        own_scratch, ex_sems = rest[:len(rest) - len(ex.sems)], rest[len(rest) - len(ex.sems):]
        t = step()
        pl.when(t == 0)(lambda: ex.start(ex_in, ex_out, ex_sems))
        body(*own_in, *own_out, *own_scratch)
        if ex.middle is not None:
            pl.when(t == (steps * 7) // 8)(lambda: ex.middle(ex_in, ex_out, ex_sems))
        pl.when(t == steps - 1)(lambda: ex.finish(ex_in, ex_out, ex_sems))

    return wrapped


def _call_hosted(body, name, grid, in_specs, out_specs, out_shape, scratch, args, ex):
    n_in, n_out = len(in_specs), len(out_specs)
    steps = 1
    for extent in grid:
        steps *= extent

    def step():
        t = pl.program_id(0)
        for axis in range(1, len(grid)):
            t = t * grid[axis] + pl.program_id(axis)
        return t

    if ex is not None:
        in_specs = in_specs + [ANY_SPEC] * len(ex.inputs)
        out_specs = out_specs + [ANY_SPEC] * len(ex.out_shapes)
        out_shape = out_shape + ex.out_shapes
        scratch = scratch + ex.sems
        args = args + ex.inputs
    outs = pl.pallas_call(
        _hosted(body, n_in, n_out, ex, step, steps), name=name, grid=grid, in_specs=in_specs, out_specs=out_specs,
        out_shape=out_shape, scratch_shapes=scratch,
    )(*args)
    return list(outs[:n_out]), list(outs[n_out:])


def _gather_exchange(shards):
    n = len(shards)

    def setup(ins, outs, sems):
        send_sems, recv_sems, local_sems = sems
        x, y, c = _pos()
        me, sibling = (x, y, c), (x, y, 1 - c)
        chips = [(1 - x, y), (x, 1 - y), (1 - x, 1 - y)]

        def copy(i, k, block, to, src=None):
            px, py, pc = block
            dst = outs[i].at[4 * px + 2 * py + pc]
            return pltpu.make_async_remote_copy(
                src_ref=dst if src is None else src, dst_ref=dst, send_sem=send_sems.at[7 * i + k],
                recv_sem=recv_sems.at[7 * i + k], device_id=to, device_id_type=MESH)

        def mine(i):
            return pltpu.make_async_copy(ins[i], outs[i].at[4 * x + 2 * y + c], local_sems.at[i])

        def first(i):
            return [copy(i, 0, me, sibling, src=ins[i])] + [
                copy(i, 1 + j, me, (*chip, c), src=ins[i]) for j, chip in enumerate(chips)]

        def passed(i, j):
            return copy(i, 4 + j, (*chips[j], c), sibling)

        return me, sibling, chips, c, copy, mine, first, passed

    def start(ins, outs, sems):
        _, _, _, _, _, mine, first, _ = setup(ins, outs, sems)
        for i in range(n):
            mine(i).start()
            for cp in first(i):
                cp.start()

    def middle(ins, outs, sems):
        me, _, chips, c, copy, _, _, passed = setup(ins, outs, sems)
        for j, chip in enumerate(chips):
            for i in range(n):
                copy(i, 1 + j, (*chip, c), me).wait_recv()
                passed(i, j).start()

    def finish(ins, outs, sems):
        me, sibling, chips, c, copy, mine, first, passed = setup(ins, outs, sems)
        for i in range(n):
            copy(i, 0, sibling, me).wait_recv()
            for j, chip in enumerate(chips):
                copy(i, 4 + j, (*chip, 1 - c), me).wait_recv()
        for i in range(n):
            for cp in first(i) + [passed(i, j) for j in range(3)]:
                cp.wait_send()
            mine(i).wait()

    return _Exchange(
        shards, [jax.ShapeDtypeStruct((NDEV,) + s.shape, s.dtype) for s in shards],
        [pltpu.SemaphoreType.DMA((7 * n,)), pltpu.SemaphoreType.DMA((7 * n,)), pltpu.SemaphoreType.DMA((n,))],
        start, finish, middle)


def _sibling_exchange(grads):
    n = len(grads)

    def copies(ins, outs, sems):
        send_sems, recv_sems = sems
        x, y, c = _pos()
        return [pltpu.make_async_remote_copy(
            src_ref=ins[i].at[2 * q + (1 - c)], dst_ref=outs[i].at[q], send_sem=send_sems.at[4 * i + q],
            recv_sem=recv_sems.at[4 * i + q], device_id=(x, y, 1 - c), device_id_type=MESH)
            for i in range(n) for q in range(4)]

    def start(ins, outs, sems):
        for cp in copies(ins, outs, sems):
            cp.start()

    def finish(ins, outs, sems):
        for cp in copies(ins, outs, sems):
            cp.wait()

    return _Exchange(
        grads, [jax.ShapeDtypeStruct((4,) + g.shape[1:], g.dtype) for g in grads],
        [pltpu.SemaphoreType.DMA((4 * n,)), pltpu.SemaphoreType.DMA((4 * n,))], start, finish)


def _chip_exchange(parts):
    n = len(parts)

    def copies(ins, outs, sems):
        send_sems, recv_sems = sems
        x, y, c = _pos()
        return [pltpu.make_async_remote_copy(
            src_ref=ins[i].at[j - 1], dst_ref=outs[i].at[j - 1], send_sem=send_sems.at[3 * i + j - 1],
            recv_sem=recv_sems.at[3 * i + j - 1], device_id=(_flip(x, j & 2), _flip(y, j & 1), c),
            device_id_type=MESH) for i in range(n) for j in range(1, 4)]

    def start(ins, outs, sems):
        for cp in copies(ins, outs, sems):
            cp.start()

    def finish(ins, outs, sems):
        for cp in copies(ins, outs, sems):
            cp.wait()

    return _Exchange(
        parts, [jax.ShapeDtypeStruct(p.shape, p.dtype) for p in parts],
        [pltpu.SemaphoreType.DMA((3 * n,)), pltpu.SemaphoreType.DMA((3 * n,))], start, finish)


def _rs_chip_partial(place, grad, recv, name):
    _, R, C = grad.shape
    tr = _blk(R, 256)

    def body(place_ref, *refs):
        g_refs, r_refs = refs[:4], refs[4:8]
        own_ref, out_ref = refs[8:]
        own_ref[...] = g_refs[0][0] + r_refs[0][0]
        for j in range(1, 4):
            out_ref[j - 1] = (g_refs[j][0] + r_refs[j][0]).astype(out_ref.dtype)

    def g_map(j):
        return lambda i, p: (2 * jnp.bitwise_xor(p[0], j) + p[1], i, 0)

    def r_map(j):
        return lambda i, p: (jnp.bitwise_xor(p[0], j), i, 0)

    grid_spec = pltpu.PrefetchScalarGridSpec(
        num_scalar_prefetch=1, grid=(R // tr,),
        in_specs=[pl.BlockSpec((1, tr, C), g_map(j)) for j in range(4)]
        + [pl.BlockSpec((1, tr, C), r_map(j)) for j in range(4)],
        out_specs=[pl.BlockSpec((tr, C), lambda i, p: (i, 0)), pl.BlockSpec((3, tr, C), lambda i, p: (0, i, 0))])
    return pl.pallas_call(
        body, name=name, grid_spec=grid_spec,
        out_shape=[jax.ShapeDtypeStruct((R, C), F32), jax.ShapeDtypeStruct((3, R, C), _MXU)],
    )(place, *([grad] * 4), *([recv] * 4))


def _adamw_math(w, g, m, v):
    m = ADAM_B1 * m + (1.0 - ADAM_B1) * g
    v = ADAM_B2 * v + (1.0 - ADAM_B2) * (g * g)
    m_hat = m / (1.0 - ADAM_B1 ** ADAM_STEP)
    v_hat = v / (1.0 - ADAM_B2 ** ADAM_STEP)
    delta = -ADAM_LR * (m_hat / (jnp.sqrt(v_hat) + ADAM_EPS) + ADAM_WD * w)
    return delta, m, v


def _adamw(w, g, m, v, name):
    R, C = w.shape
    tr = _blk(R, 256)

    def body(w_ref, g_ref, m_ref, v_ref, d_ref, nm_ref, nv_ref):
        d, nm, nv = _adamw_math(w_ref[...], g_ref[...], m_ref[...], v_ref[...])
        d_ref[...] = d
        nm_ref[...] = nm
        nv_ref[...] = nv

    spec = pl.BlockSpec((tr, C), lambda i: (i, 0))
    return pl.pallas_call(
        body, name=name, grid=(R // tr,), in_specs=[spec] * 4, out_specs=[spec] * 3,
        out_shape=[jax.ShapeDtypeStruct((R, C), F32)] * 3,
    )(w, g, m, v)


def _rs_sum_adamw(owns, recvs, w, m, v, name):
    L, R, C = w.shape
    tr = _blk(R, 256)
    nr = R // tr

    def body(o0, o1, r0, r1, w_ref, m_ref, v_ref, g_ref, d_ref, nm_ref, nv_ref):
        def step(o_ref, r_ref):
            g = o_ref[...]
            for j in range(3):
                g = g + r_ref[j].astype(F32)
            d, nm, nv = _adamw_math(w_ref[0], g, m_ref[0], v_ref[0])
            g_ref[0] = g
            d_ref[0] = d
            nm_ref[0] = nm
            nv_ref[0] = nv

        pl.when(pl.program_id(0) == 0)(lambda: step(o0, r0))
        pl.when(pl.program_id(0) == 1)(lambda: step(o1, r1))

    def hold(layer):
        if layer == 0:
            return lambda l, i: i * (1 - l) + (nr - 1) * l
        return lambda l, i: i * l

    own_spec = [pl.BlockSpec((tr, C), functools.partial(lambda l, i, f: (f(l, i), 0), f=hold(k))) for k in range(2)]
    recv_spec = [pl.BlockSpec((3, tr, C), functools.partial(lambda l, i, f: (0, f(l, i), 0), f=hold(k)))
                 for k in range(2)]
    lay = pl.BlockSpec((1, tr, C), lambda l, i: (l, i, 0))
    return pl.pallas_call(
        body, name=name, grid=(L, nr),
        in_specs=own_spec + recv_spec + [lay] * 3, out_specs=[lay] * 4,
        out_shape=[jax.ShapeDtypeStruct((L, R, C), F32)] * 4,
    )(owns[0], owns[1], recvs[0], recvs[1], w, m, v)


def _silu(x):
    return x / (1.0 + jnp.exp(-x))


def _start(c, w_ada, b_cols, w_first, name):
    L, D, Ca = w_ada.shape
    ex = _gather_exchange([w_first])

    def body(c_ref, w_ref, b_ref, first_ref, call_ref, mod_ref, gathered_ref, part_ref, c_send, c_recv, m_send,
             m_recv, *ex_sems):
        _gather_small(c_ref, call_ref, c_send, c_recv)
        ex.start([first_ref], [gathered_ref], ex_sems)
        act = _silu(call_ref[...].reshape(NDEV, D)).astype(_MXU)
        for l in range(L):
            part_ref[:, l * Ca:(l + 1) * Ca] = _nn(act, w_ref[l].astype(_MXU)) + b_ref[l:l + 1, :]
        _gather_small(part_ref, mod_ref, m_send, m_recv)
        ex.middle([first_ref], [gathered_ref], ex_sems)
        ex.finish([first_ref], [gathered_ref], ex_sems)

    pairs = [pltpu.SemaphoreType.DMA((NDEV - 1,))] * 4
    return pl.pallas_call(
        body, name=name,
        out_shape=[jax.ShapeDtypeStruct((NDEV, 1, D), F32), jax.ShapeDtypeStruct((NDEV, NDEV, L * Ca), F32)]
        + ex.out_shapes,
        in_specs=[VMEM_SPEC] * 3 + [ANY_SPEC], out_specs=[VMEM_SPEC, VMEM_SPEC, ANY_SPEC],
        scratch_shapes=[pltpu.VMEM((NDEV, L * Ca), F32)] + pairs + ex.sems,
    )(c, w_ada, b_cols, w_first)


def _w_ada_grad(c_t, dmod_cols, name):
    L, _, Ca = dmod_cols.shape
    D = c_t.shape[0]

    def body(c_ref, d_ref, o_ref):
        act = _silu(c_ref[...]).astype(_MXU)
        for l in range(L):
            o_ref[l] = _nn(act, d_ref[l].astype(_MXU))

    return pl.pallas_call(
        body, name=name, out_shape=jax.ShapeDtypeStruct((L, D, Ca), F32),
        in_specs=[VMEM_SPEC] * 2, out_specs=VMEM_SPEC,
    )(c_t, dmod_cols)


def _sum_devices(gathered, name):
    _, _, N = gathered.shape

    def body(x_ref, o_ref):
        acc = x_ref[0]
        for d in range(1, NDEV):
            acc = acc + x_ref[d]
        o_ref[...] = acc

    return pl.pallas_call(
        body, name=name, out_shape=jax.ShapeDtypeStruct((1, N), F32),
        in_specs=[VMEM_SPEC], out_specs=VMEM_SPEC,
    )(gathered)


def _ln_mod_matmul(x, g, sc, sh, w, name):
    S, D = x.shape
    N = w.shape[1]
    tm = _blk(S, ROW_BLOCK)

    def body(x_ref, g_ref, sc_ref, sh_ref, w_ref, o_ref, h_ref):
        xv = x_ref[...]
        r = lax.rsqrt(jnp.mean(xv * xv, axis=-1, keepdims=True) + EPS)
        hv = ((xv * r) * g_ref[...]) * (1.0 + sc_ref[...]) + sh_ref[...]
        hb = hv.astype(_MXU)
        h_ref[...] = hb
        o_ref[...] = _nn(hb, w_ref[...]).astype(o_ref.dtype)

    vec = pl.BlockSpec((1, D), lambda i: (0, 0))
    row = lambda width: pl.BlockSpec((tm, width), lambda i: (i, 0))
    return pl.pallas_call(
        body, name=name, grid=(S // tm,),
        in_specs=[row(D), vec, vec, vec, pl.BlockSpec((D, N), lambda i: (0, 0))],
        out_specs=[row(N), row(D)],
        out_shape=[jax.ShapeDtypeStruct((S, N), _MXU), jax.ShapeDtypeStruct((S, D), _MXU)],
    )(x, g, sc, sh, w)


def _attn_out_mlp_in(mixed, w_o, xres, gt, g, sc, sh, w1, name):
    S, D = xres.shape
    N = w1.shape[1]
    tm = _blk(S, ROW_BLOCK // 2)

    def body(a_ref, wo_ref, x_ref, gt_ref, g_ref, sc_ref, sh_ref, w1_ref, x1_ref, f_ref, u_ref, h_ref):
        f = _nn(a_ref[...], wo_ref[...])
        f_ref[...] = f.astype(f_ref.dtype)
        xv = x_ref[...] + gt_ref[...] * f
        x1_ref[...] = xv
        r = lax.rsqrt(jnp.mean(xv * xv, axis=-1, keepdims=True) + EPS)
        hb = (((xv * r) * g_ref[...]) * (1.0 + sc_ref[...]) + sh_ref[...]).astype(_MXU)
        h_ref[...] = hb
        u_ref[...] = _nn(hb, w1_ref[...]).astype(u_ref.dtype)

    vec = pl.BlockSpec((1, D), lambda i: (0, 0))
    row = lambda width: pl.BlockSpec((tm, width), lambda i: (i, 0))
    whole = lambda arr: pl.BlockSpec(arr.shape, lambda i: (0, 0))
    return pl.pallas_call(
        body, name=name, grid=(S // tm,),
        in_specs=[row(D), whole(w_o), row(D), vec, vec, vec, vec, whole(w1)],
        out_specs=[row(D), row(D), row(N), row(D)],
        out_shape=[jax.ShapeDtypeStruct((S, D), F32), jax.ShapeDtypeStruct((S, D), _MXU),
                   jax.ShapeDtypeStruct((S, N), _MXU), jax.ShapeDtypeStruct((S, D), _MXU)],
    )(mixed, w_o, xres, gt, g, sc, sh, w1)


def _matmul_res_gate(a, w, xres, gt, relu2, name, target=None):
    S, K = a.shape
    N = w.shape[1]
    tm = _blk(S, ROW_BLOCK)
    last = S // tm - 1
    with_loss = target is not None

    def body(a_ref, w_ref, x_ref, gt_ref, *rest):
        av = a_ref[...]
        if relu2:
            af = jnp.maximum(av.astype(F32), 0.0)
            av = (af * af).astype(_MXU)
        f = _nn(av, w_ref[...])
        out = x_ref[...] + gt_ref[...] * f
        if not with_loss:
            o_ref, f_ref = rest
            o_ref[...] = out
        else:
            t_ref, o_ref, f_ref, l_ref, acc_ref = rest
            i = pl.program_id(0)
            e = out - t_ref[...]
            o_ref[...] = e * (1.0 / N)
            _accumulate(acc_ref, jnp.sum(e * e, axis=0, keepdims=True), i == 0)

            @pl.when(i == last)
            def _():
                l_ref[...] = (0.5 / N) * jnp.sum(acc_ref[...], axis=1, keepdims=True)
        f_ref[...] = f.astype(f_ref.dtype)

    row = lambda width: pl.BlockSpec((tm, width), lambda i: (i, 0))
    in_specs = [row(K), pl.BlockSpec((K, N), lambda i: (0, 0)), row(N), pl.BlockSpec((1, N), lambda i: (0, 0))]
    out_specs = [row(N), row(N)]
    out_shape = [jax.ShapeDtypeStruct((S, N), F32), jax.ShapeDtypeStruct((S, N), _MXU)]
    args = [a, w, xres, gt]
    if with_loss:
        in_specs.append(row(N))
        args.append(target)
        out_specs.append(pl.BlockSpec((1, 1), lambda i: (0, 0)))
        out_shape.append(jax.ShapeDtypeStruct((1, 1), F32))
    return pl.pallas_call(
        body, name=name, grid=(S // tm,), in_specs=in_specs, out_specs=out_specs, out_shape=out_shape,
        scratch_shapes=[pltpu.VMEM((1, N), F32)] if with_loss else [],
    )(*args)


def _accumulate(ref, part, first):
    @pl.when(first)
    def _():
        ref[...] = part

    @pl.when(jnp.logical_not(first))
    def _():
        ref[...] += part


def _gate_nt_matmul(dx, f, gt, w, u, name):
    S, D = dx.shape
    N = w.shape[0]
    tm = _blk(S, ROW_BLOCK)
    with_u = u is not None

    def body(*refs):
        if with_u:
            dx_ref, f_ref, gt_ref, w_ref, u_ref, dz_ref, dgt_ref, res_ref = refs
        else:
            dx_ref, f_ref, gt_ref, w_ref, dz_ref, dgt_ref, res_ref = refs
        dxv = dx_ref[...]
        dz = (dxv * gt_ref[...]).astype(_MXU)
        dz_ref[...] = dz
        _accumulate(dgt_ref, jnp.sum(dxv * f_ref[...].astype(F32), axis=0, keepdims=True), pl.program_id(0) == 0)
        r = _nt(dz, w_ref[...])
        if with_u:
            r = r * (2.0 * jnp.maximum(u_ref[...].astype(F32), 0.0))
        res_ref[...] = r.astype(res_ref.dtype)

    row = lambda width: pl.BlockSpec((tm, width), lambda i: (i, 0))
    in_specs = [row(D), row(D), pl.BlockSpec((1, D), lambda i: (0, 0)), pl.BlockSpec((N, D), lambda i: (0, 0))]
    args = [dx, f, gt, w]
    if with_u:
        in_specs.append(row(N))
        args.append(u)
    return pl.pallas_call(
        body, name=name, grid=(S // tm,), in_specs=in_specs,
        out_specs=[row(D), pl.BlockSpec((1, D), lambda i: (0, 0)), row(N)],
        out_shape=[jax.ShapeDtypeStruct((S, D), _MXU), jax.ShapeDtypeStruct((1, D), F32),
                   jax.ShapeDtypeStruct((S, N), _MXU)],
    )(*args)


def _tn_matmul(a, b, by_col, relu2, name):
    S, Ka = a.shape
    Nb = b.shape[1]
    ts = _blk(S, 2 * ROW_BLOCK)
    half = NDEV // 2
    if by_col:
        R, C = Ka, Nb // NDEV
        a_spec = pl.BlockSpec((ts, Ka), lambda h, k: (k, 0))
        b_spec = pl.BlockSpec((ts, half * C), lambda h, k: (k, h))
    else:
        R, C = Ka // NDEV, Nb
        a_spec = pl.BlockSpec((ts, half * R), lambda h, k: (k, h))
        b_spec = pl.BlockSpec((ts, Nb), lambda h, k: (k, 0))

    def body(a_ref, b_ref, o_ref):
        av = a_ref[...]
        if relu2:
            af = jnp.maximum(av.astype(F32), 0.0)
            av = (af * af).astype(_MXU)
        p = _tn(av, b_ref[...])
        first = pl.program_id(1) == 0
        for d in range(half):
            part = p[:, d * C:(d + 1) * C] if by_col else p[d * R:(d + 1) * R, :]
            _accumulate(o_ref.at[d], part, first)

    return pl.pallas_call(
        body, name=name, grid=(NDEV // half, S // ts), in_specs=[a_spec, b_spec],
        out_specs=pl.BlockSpec((half, R, C), lambda h, k: (h, 0, 0)),
        out_shape=jax.ShapeDtypeStruct((NDEV, R, C), F32),
    )(a, b)


def _nt_ln_bwd(dy, w, x, g, sc, sh, dxres, name):
    S, D = x.shape
    N = w.shape[1]
    tm = _blk(S, ROW_BLOCK)

    def body(dy_ref, w_ref, x_ref, g_ref, sc_ref, sh_ref, dxr_ref, dx_ref, dsh_ref, dsc_ref, dg_ref):
        dh = _nt(dy_ref[...], w_ref[...])
        xv = x_ref[...]
        r = lax.rsqrt(jnp.mean(xv * xv, axis=-1, keepdims=True) + EPS)
        xhat = xv * r
        gv = g_ref[...]
        dn = dh * (1.0 + sc_ref[...])
        dxhat = dn * gv
        dxv = r * (dxhat - xhat * jnp.mean(dxhat * xhat, axis=-1, keepdims=True))
        dx_ref[...] = dxr_ref[...] + dxv
        first = pl.program_id(0) == 0
        _accumulate(dsh_ref, jnp.sum(dh, axis=0, keepdims=True), first)
        _accumulate(dsc_ref, jnp.sum(dh * (xhat * gv), axis=0, keepdims=True), first)
        _accumulate(dg_ref, jnp.sum(dn * xhat, axis=0, keepdims=True), first)

    row = lambda width: pl.BlockSpec((tm, width), lambda i: (i, 0))
    vec = pl.BlockSpec((1, D), lambda i: (0, 0))
    return pl.pallas_call(
        body, name=name, grid=(S // tm,),
        in_specs=[row(N), pl.BlockSpec((D, N), lambda i: (0, 0)), row(D), vec, vec, vec, row(D)],
        out_specs=[row(D), vec, vec, vec],
        out_shape=[jax.ShapeDtypeStruct((S, D), F32)] + [jax.ShapeDtypeStruct((1, D), F32)] * 3,
    )(dy, w, x, g, sc, sh, dxres)


def _split2(v):
    hi = v.astype(_MXU)
    mid = (v - hi.astype(F32)).astype(_MXU)
    return hi, mid


def _tri_sums(vs, tri2):
    T = vs[0].shape[0]
    out = []
    for j in range(len(vs) // 2):
        wide = [jnp.concatenate(_split2(vs[2 * j + e]), axis=1) for e in range(2)]
        for both in _per_head(_nn(jnp.concatenate(wide, axis=0), tri2), T):
            out.append((both[:, :T], both[:, T:]))
    return out


def _tri2(T, inclusive):
    j = lax.broadcasted_iota(jnp.int32, (2 * T, 2 * T), 0) % T
    s = lax.broadcasted_iota(jnp.int32, (2 * T, 2 * T), 1)
    keep = (j >= s) if inclusive else (j > s)
    return jnp.where((s >= T) | keep, 1.0, 0.0).astype(_MXU)


def _log_sigmoid(z):
    return jnp.minimum(z, 0.0) - jnp.log(1.0 + jnp.exp(-jnp.abs(z)))


def _per_head(tall, T):
    return [tall[h * T:(h + 1) * T] for h in range(tall.shape[0] // T)]


def _sb_blocks(q_tall, k2, strict, tri2, carry):
    T = k2[0].shape[0]
    zs = []
    for qt, kblk in zip(q_tall, k2):
        zs += _per_head(_nt(qt, kblk), T)
    lbs, l1s = [], []
    for z in zs:
        lb = _log_sigmoid(z)
        l1 = lb - z
        if strict is not None:
            l1 = jnp.where(strict, l1, 0.0)
        lbs.append(lb)
        l1s.append(l1)
    sums = _tri_sums(l1s, tri2)
    amps, new_carry = [], []
    for lb, (sfx, tot), c in zip(lbs, sums, carry):
        a = jnp.exp(lb + sfx + c)
        if strict is not None:
            a = jnp.where(strict, a, 0.0)
        amps.append(a)
        new_carry.append(c + tot)
    return lbs, amps, new_carry


def _sb_alive(carry):
    top = carry[0]
    for c in carry[1:]:
        top = jnp.maximum(top, c)
    return jnp.max(top) > SB_SKIP


def _skew_index():
    i = lax.broadcasted_iota(jnp.int32, (CA_T, SKEW_W + 1), 0)
    m = lax.broadcasted_iota(jnp.int32, (CA_T, SKEW_W + 1), 1)
    wrapped = i + m >= SKEW_W
    row = jnp.where(wrapped, i + 1, i)
    j = jnp.where(wrapped, i + m - SKEW_W, i + m)
    a = row // CHUNK
    jj = j - a * CHUNK
    inband = (jj >= 0) & (jj < BAND) & (j < CA_W) & (row < CA_T)
    idx = jnp.clip((row - a * CHUNK) + PAD - jj, -REL_CLIP, REL_CLIP) + REL_CLIP
    return inband, idx, wrapped


def _skew(tile):
    H = tile.shape[0]
    flat = jnp.pad(tile, ((0, 0), (0, 0), (0, SKEW_W - CA_W))).reshape(H, CA_T * SKEW_W)
    return jnp.pad(flat, ((0, 0), (0, CA_T))).reshape(H, CA_T, SKEW_W + 1)


def _unskew(view):
    H = view.shape[0]
    flat = view.reshape(H, CA_T * (SKEW_W + 1))[:, :CA_T * SKEW_W]
    return flat.reshape(H, CA_T, SKEW_W)[:, :, :CA_W]


def _ca_bias(rel_bias, name):
    H = rel_bias.shape[0]
    top = rel_bias[:, N_REL - 1:]
    by_offset = jnp.concatenate(
        [jnp.broadcast_to(top, (H, PAD - REL_CLIP + 1)), jnp.flip(rel_bias[:, :N_REL - 1], axis=1),
         jnp.broadcast_to(top, (H, SKEW_W + 1 - (PAD - REL_CLIP + 1) - (N_REL - 1)))], axis=1)

    def body(t_ref, o_ref):
        inband, _, wrapped = _skew_index()
        vals = jnp.where(wrapped, t_ref[0][:, 0:1], t_ref[0])
        o_ref[0] = jnp.where(inband, vals, NEG)

    view = pl.pallas_call(
        body, name=name, grid=(H,), in_specs=[pl.BlockSpec((1, 1, SKEW_W + 1), lambda h: (h, 0, 0))],
        out_specs=pl.BlockSpec((1, CA_T, SKEW_W + 1), lambda h: (h, 0, 0)),
        out_shape=jax.ShapeDtypeStruct((H, CA_T, SKEW_W + 1), F32),
    )(by_offset.reshape(H, 1, SKEW_W + 1))
    return _unskew(view)


def _ca_bias_bwd(dbias, name):
    H = dbias.shape[0]

    def body(d_ref, o_ref):
        inband, idx, _ = _skew_index()
        d = jnp.where(inband, d_ref[0], 0.0)
        clipped = idx == N_REL - 1
        by_offset = jnp.sum(jnp.where(clipped, 0.0, d), axis=0, keepdims=True)
        top = jnp.sum(jnp.sum(jnp.where(clipped, d, 0.0), axis=0, keepdims=True), axis=1, keepdims=True)
        lane = lax.broadcasted_iota(jnp.int32, (1, SKEW_W + 1), 1)
        o_ref[0] = jnp.where(lane == 0, top, by_offset)

    out = pl.pallas_call(
        body, name=name, grid=(H,), in_specs=[pl.BlockSpec((1, CA_T, SKEW_W + 1), lambda h: (h, 0, 0))],
        out_specs=pl.BlockSpec((1, 1, SKEW_W + 1), lambda h: (h, 0, 0)),
        out_shape=jax.ShapeDtypeStruct((H, 1, SKEW_W + 1), F32),
    )(_skew(dbias))[:, 0]
    first = PAD - REL_CLIP + 1
    return jnp.concatenate([jnp.flip(out[:, first:first + N_REL - 1], axis=1), out[:, 0:1]], axis=1)


def _low_lanes(rows):
    return lax.broadcasted_iota(jnp.int32, (rows, PAIR), 1) < HEAD_DIM


def _one_head(t2, low, first, scale=1.0):
    tf = t2.astype(F32) * scale
    return (jnp.where(low, tf, 0.0) if first else jnp.where(low, 0.0, tf)).astype(_MXU)


def _two_heads(t2, low, scale=1.0):
    return jnp.concatenate([_one_head(t2, low, True, scale), _one_head(t2, low, False, scale)], axis=0)


def _sb_fwd(proj, name, ex=None):
    S, W = proj.shape
    half = W // 6
    npair = half // PAIR
    T = _blk(S, SB_T)
    GP = _blk(npair, SB_PAIRS)
    GW = GP * PAIR
    nb = npair // GP

    def body(q_ref, k_ref, v_ref, o_ref, ox_ref):
        qi = pl.program_id(1)
        low = _low_lanes(T)
        q_tall = []
        for j in range(GP):
            q2 = q_ref[:, j * PAIR:(j + 1) * PAIR]
            q_tall.append(_two_heads(q2, low, HEAD_DIM ** -0.5))
        row = lax.broadcasted_iota(jnp.int32, (T, T), 0)
        col = lax.broadcasted_iota(jnp.int32, (T, T), 1)
        tri2 = _tri2(T, inclusive=False)

        def pairs(kb, carry, acc, fine, strict):
            rows = pl.ds(pl.multiple_of(kb * T, T), T)
            k2 = [k_ref[rows, j * PAIR:(j + 1) * PAIR] for j in range(GP)]
            v2 = [v_ref[rows, j * PAIR:(j + 1) * PAIR] for j in range(GP)]
            _, amps, carry = _sb_blocks(q_tall, k2, strict, tri2, carry)
            parts = [_split2(a) for a in amps]
            new_acc, new_fine = [], []
            for j in range(GP):
                tall = jnp.concatenate([parts[2 * j][0], parts[2 * j + 1][0], parts[2 * j][1], parts[2 * j + 1][1]],
                                       axis=0)
                hi0, hi1, mid0, mid1 = _per_head(_nn(tall, v2[j]), T)
                new_acc.append(acc[j] + jnp.where(low, hi0, hi1))
                new_fine.append(fine[j] + jnp.where(low, mid0, mid1))
            return tuple(carry), tuple(new_acc), tuple(new_fine)

        zero = (jnp.zeros((T, PAIR), F32),) * GP
        carry, acc, fine = pairs(qi, (jnp.zeros((T, T), F32),) * (2 * GP), zero, zero, col < row)

        def cond(st):
            kb, alive, _, _, _ = st
            return jnp.logical_and(kb >= 0, alive)

        def step(st):
            kb, _, carry, acc, fine = st
            carry, acc, fine = pairs(kb, carry, acc, fine, None)
            return kb - 1, _sb_alive(carry), carry, acc, fine

        _, _, _, acc, fine = lax.while_loop(cond, step, (qi - 1, _sb_alive(carry), carry, acc, fine))
        for j in range(GP):
            o_ref[:, j * PAIR:(j + 1) * PAIR] = acc[j].astype(o_ref.dtype)
            ox_ref[:, j * PAIR:(j + 1) * PAIR] = acc[j] + fine[j]

    blk = pl.BlockSpec((T, GW), lambda p, i: (i, p))
    return _call_hosted(
        body, name, (nb, S // T),
        [blk, pl.BlockSpec((S, GW), lambda p, i: (0, nb + p)), pl.BlockSpec((S, GW), lambda p, i: (0, 2 * nb + p))],
        [blk, blk], [jax.ShapeDtypeStruct((S, half), _MXU), jax.ShapeDtypeStruct((S, half), F32)],
        [], [proj, proj, proj], ex)


def _sb_bwd(proj, ox, dmixed, name, ex=None):
    S, W = proj.shape
    half = W // 6
    npair = half // PAIR
    T = _blk(S, SB_T)
    GP = _blk(npair, SB_PAIRS)
    GW = GP * PAIR
    nb = npair // GP
    last = S // T - 1
    scale = HEAD_DIM ** -0.5

    def body(q_ref, k_ref, v_ref, ox_ref, do_ref, dq_ref, dk_ref, dv_ref, dka_ref, dva_ref):
        qi = pl.program_id(1)

        @pl.when(qi == 0)
        def _():
            dka_ref[...] = jnp.zeros_like(dka_ref)
            dva_ref[...] = jnp.zeros_like(dva_ref)

        low = _low_lanes(T)
        q2, do2, q_tall, do_tall, deltas = [], [], [], [], []
        for j in range(GP):
            cols = slice(j * PAIR, (j + 1) * PAIR)
            q2.append(q_ref[:, cols])
            do2.append(do_ref[:, cols])
            q_tall.append(_two_heads(q2[j], low, scale))
            dobs = [_one_head(do2[j], low, True), _one_head(do2[j], low, False)]
            do_tall.append(jnp.concatenate(dobs, axis=0))
            for e in range(2):
                deltas.append(jnp.sum(dobs[e].astype(F32) * ox_ref[:, cols], axis=-1, keepdims=True))
        row = lax.broadcasted_iota(jnp.int32, (T, T), 0)
        col = lax.broadcasted_iota(jnp.int32, (T, T), 1)
        tri_ex = _tri2(T, inclusive=False)
        tri_in = _tri2(T, inclusive=True)

        def pairs(kb, carry, right, dq, strict):
            rows = pl.ds(pl.multiple_of(kb * T, T), T)
            k2 = [k_ref[rows, j * PAIR:(j + 1) * PAIR] for j in range(GP)]
            v2 = [v_ref[rows, j * PAIR:(j + 1) * PAIR] for j in range(GP)]
            nh = 2 * GP
            gs = []
            for j in range(GP):
                gs += _per_head(_nt(do_tall[j], v2[j]), T)
            lbs, amps, carry = _sb_blocks(q_tall, k2, strict, tri_ex, carry)
            ags = [a * gg for a, gg in zip(amps, gs)]
            sums = _tri_sums(ags, tri_in)
            dzbs = []
            for h in range(nh):
                left = deltas[h] - (sums[h][0] + right[h])
                beta = jnp.exp(lbs[h])
                dz = ags[h] - beta * (ags[h] + left)
                if strict is not None:
                    dz = jnp.where(strict, dz, 0.0)
                dzbs.append(dz.astype(_MXU))
            abs_ = [a.astype(_MXU) for a in amps]
            new_dq = []
            for j in range(GP):
                cols = slice(j * PAIR, (j + 1) * PAIR)
                dk0, dk1 = _per_head(_tn(jnp.concatenate(dzbs[2 * j:2 * j + 2], axis=1), q2[j]), T)
                dv0, dv1 = _per_head(_tn(jnp.concatenate(abs_[2 * j:2 * j + 2], axis=1), do2[j]), T)
                dq0, dq1 = _per_head(_nn(jnp.concatenate(dzbs[2 * j:2 * j + 2], axis=0), k2[j]), T)
                dka_ref[rows, cols] += jnp.where(low, dk0, dk1)
                dva_ref[rows, cols] += jnp.where(low, dv0, dv1)
                new_dq.append(dq[j] + jnp.where(low, dq0, dq1))
            right = tuple(right[h] + sums[h][1] for h in range(nh))
            return tuple(carry), right, tuple(new_dq)

        zero = (jnp.zeros((T, T), F32),) * (2 * GP)
        carry, right, dq = pairs(qi, zero, zero, (jnp.zeros((T, PAIR), F32),) * GP, col < row)

        def cond(st):
            kb, alive, _, _, _ = st
            return jnp.logical_and(kb >= 0, alive)

        def step(st):
            kb, _, carry, right, dq = st
            carry, right, dq = pairs(kb, carry, right, dq, None)
            return kb - 1, _sb_alive(carry), carry, right, dq

        _, _, _, _, dq = lax.while_loop(cond, step, (qi - 1, _sb_alive(carry), carry, right, dq))
        for j in range(GP):
            dq_ref[:, j * PAIR:(j + 1) * PAIR] = (dq[j] * scale).astype(dq_ref.dtype)

        @pl.when(qi == last)
        def _():
            dk_ref[...] = (dka_ref[...] * scale).astype(dk_ref.dtype)
            dv_ref[...] = dva_ref[...].astype(dv_ref.dtype)

    blk = pl.BlockSpec((T, GW), lambda p, i: (i, p))
    full = pl.BlockSpec((S, GW), lambda p, i: (0, p))
    return _call_hosted(
        body, name, (nb, S // T),
        [blk, pl.BlockSpec((S, GW), lambda p, i: (0, nb + p)), pl.BlockSpec((S, GW), lambda p, i: (0, 2 * nb + p)),
         blk, blk],
        [blk, full, full], [jax.ShapeDtypeStruct((S, half), _MXU)] * 3,
        [pltpu.VMEM((S, GW), F32), pltpu.VMEM((S, GW), F32)], [proj, proj, proj, ox, dmixed], ex)


def _pair_norm(t2, g2, low):
    tf = t2.astype(F32)
    sq = tf * tf
    both = jnp.sum(sq, axis=-1, keepdims=True)
    first = jnp.sum(jnp.where(low, sq, 0.0), axis=-1, keepdims=True)
    r = jnp.where(low, lax.rsqrt(first * (1.0 / HEAD_DIM) + EPS), lax.rsqrt((both - first) * (1.0 / HEAD_DIM) + EPS))
    hat = tf * r
    return hat * g2, hat, r


def _pair_norm_bwd(dn, hat, r, g2, low):
    dhat = dn * g2
    prod = dhat * hat
    both = jnp.sum(prod, axis=-1, keepdims=True)
    first = jnp.sum(jnp.where(low, prod, 0.0), axis=-1, keepdims=True)
    mean = jnp.where(low, first, both - first) * (1.0 / HEAD_DIM)
    return r * (dhat - hat * mean)


def _ca_fill(j, k_ref, v_ref, gk_ref, kn_ref, vp_ref):
    S = k_ref.shape[0]
    cols = slice(j * PAIR, (j + 1) * PAIR)
    kn, _, _ = _pair_norm(k_ref[:, cols], gk_ref[...], _low_lanes(S))
    kn_ref[j, 0:PAD, :] = jnp.zeros((PAD, PAIR), kn_ref.dtype)
    vp_ref[j, 0:PAD, :] = jnp.zeros((PAD, PAIR), vp_ref.dtype)
    kn_ref[j, PAD:PAD + S, :] = kn.astype(kn_ref.dtype)
    vp_ref[j, PAD:PAD + S, :] = v_ref[:, cols]


def _ca_scores(j, q_ref, b2_ref, gq_ref, kn_ref, qi, low):
    qn, qhat, r = _pair_norm(q_ref[:, j * PAIR:(j + 1) * PAIR], gq_ref[...], low)
    qn = qn * HEAD_DIM ** -0.5
    band = pl.ds(pl.multiple_of(qi * CA_T, CA_T), CA_W)
    key_pos = qi * CA_T - PAD + lax.broadcasted_iota(jnp.int32, (CA_T, CA_W), 1)
    both = _per_head(_nt(_two_heads(qn, low), kn_ref[j, band, :]), CA_T)
    scores = [jnp.where(key_pos >= 0, both[e] + b2_ref[2 * j + e], NEG) for e in range(2)]
    return scores, qn.astype(_MXU), qhat, r


def _softmax(s):
    e = jnp.exp(s - jnp.max(s, axis=-1, keepdims=True))
    return e * (1.0 / jnp.sum(e, axis=-1, keepdims=True))


def _ca_fwd(proj, bias2, gq2, gk2, name, ex=None):
    S, W = proj.shape
    half = W // 6
    npair = half // PAIR
    GP = _blk(npair, CA_PAIRS_FWD)
    GW = GP * PAIR
    nb = npair // GP

    def body(q_ref, k_ref, v_ref, b2_ref, gq_ref, gk_ref, o_ref, kn_ref, vp_ref):
        qi = pl.program_id(1)

        @pl.when(qi == 0)
        def _():
            for j in range(GP):
                _ca_fill(j, k_ref, v_ref, gk_ref, kn_ref, vp_ref)

        low = _low_lanes(CA_T)
        band = pl.ds(pl.multiple_of(qi * CA_T, CA_T), CA_W)
        scores = [_ca_scores(j, q_ref, b2_ref, gq_ref, kn_ref, qi, low)[0] for j in range(GP)]
        probs = [[_softmax(s).astype(_MXU) for s in pair] for pair in scores]
        for j in range(GP):
            outs = _per_head(_nn(jnp.concatenate(probs[j], axis=0), vp_ref[j, band, :]), CA_T)
            o_ref[:, j * PAIR:(j + 1) * PAIR] = jnp.where(low, outs[0], outs[1]).astype(o_ref.dtype)

    vec = pl.BlockSpec((1, PAIR), lambda p, i: (0, 0))
    return _call_hosted(
        body, name, (nb, S // CA_T),
        [pl.BlockSpec((CA_T, GW), lambda p, i: (i, 3 * nb + p)),
         pl.BlockSpec((S, GW), lambda p, i: (0, 4 * nb + p)), pl.BlockSpec((S, GW), lambda p, i: (0, 5 * nb + p)),
         pl.BlockSpec((2 * GP, CA_T, CA_W), lambda p, i: (p, 0, 0)), vec, vec],
        [pl.BlockSpec((CA_T, GW), lambda p, i: (i, p))], [jax.ShapeDtypeStruct((S, half), _MXU)],
        [pltpu.VMEM((GP, PAD + S, PAIR), _MXU), pltpu.VMEM((GP, PAD + S, PAIR), _MXU)],
        [proj, proj, proj, bias2, gq2, gk2], ex)


def _ca_bwd(proj, bias2, gq2, gk2, dmixed, name, ex=None):
    S, W = proj.shape
    half = W // 6
    npair = half // PAIR
    GP = _blk(npair, CA_PAIRS_BWD)
    GW = GP * PAIR
    nb = npair // GP
    scale = HEAD_DIM ** -0.5
    last = S // CA_T - 1

    def body(q_ref, k_ref, v_ref, b2_ref, gq_ref, gk_ref, do_ref,
             dq_ref, dk_ref, dv_ref, db_ref, dgq_ref, dgk_ref, kn_ref, vp_ref, dkn_ref, dvp_ref):
        p_id, qi = pl.program_id(0), pl.program_id(1)

        @pl.when(qi == 0)
        def _():
            for j in range(GP):
                _ca_fill(j, k_ref, v_ref, gk_ref, kn_ref, vp_ref)
            dkn_ref[...] = jnp.zeros_like(dkn_ref)
            dvp_ref[...] = jnp.zeros_like(dvp_ref)
            db_ref[...] = jnp.zeros_like(db_ref)

        @pl.when(jnp.logical_and(p_id == 0, qi == 0))
        def _():
            dgq_ref[...] = jnp.zeros_like(dgq_ref)
            dgk_ref[...] = jnp.zeros_like(dgk_ref)

        low = _low_lanes(CA_T)
        top_w = lax.broadcasted_iota(jnp.int32, (PAIR, CA_W), 0) < HEAD_DIM
        band = pl.ds(pl.multiple_of(qi * CA_T, CA_T), CA_W)
        pairs = [_ca_scores(j, q_ref, b2_ref, gq_ref, kn_ref, qi, low) for j in range(GP)]
        do2 = [do_ref[:, j * PAIR:(j + 1) * PAIR] for j in range(GP)]
        dps = [_per_head(_nt(_two_heads(do2[j], low), vp_ref[j, band, :]), CA_T) for j in range(GP)]
        probs, dsbs = [], []
        for j in range(GP):
            pj, dj = [], []
            for e in range(2):
                p = _softmax(pairs[j][0][e])
                ds = p * (dps[j][e] - jnp.sum(p * dps[j][e], axis=-1, keepdims=True))
                db_ref[2 * j + e] += ds
                pj.append(p.astype(_MXU))
                dj.append(ds.astype(_MXU))
            probs.append(pj)
            dsbs.append(dj)
        dgq = jnp.zeros((1, PAIR), F32)
        for j in range(GP):
            _, qn, qhat, r = pairs[j]
            dq_h = _per_head(_nn(jnp.concatenate(dsbs[j], axis=0), kn_ref[j, band, :]), CA_T)
            dk_t = _tn(qn, jnp.concatenate(dsbs[j], axis=1))
            dv_t = _tn(do2[j], jnp.concatenate(probs[j], axis=1))
            dkn_ref[j, :, band] += jnp.where(top_w, dk_t[:, :CA_W], dk_t[:, CA_W:])
            dvp_ref[j, :, band] += jnp.where(top_w, dv_t[:, :CA_W], dv_t[:, CA_W:])
            dqn = jnp.where(low, dq_h[0], dq_h[1]) * scale
            dgq = dgq + jnp.sum(dqn * qhat, axis=0, keepdims=True)
            dq_ref[:, j * PAIR:(j + 1) * PAIR] = _pair_norm_bwd(dqn, qhat, r, gq_ref[...], low).astype(dq_ref.dtype)
        dgq_ref[...] += dgq

        @pl.when(qi == last)
        def _():
            low_s = _low_lanes(S)
            for j in range(GP):
                cols = slice(j * PAIR, (j + 1) * PAIR)
                _, khat, rk = _pair_norm(k_ref[:, cols], gk_ref[...], low_s)
                dkn = dkn_ref[j, :, PAD:PAD + S].T
                dgk_ref[...] += jnp.sum(dkn * khat, axis=0, keepdims=True)
                dk_ref[:, cols] = _pair_norm_bwd(dkn, khat, rk, gk_ref[...], low_s).astype(dk_ref.dtype)
                dv_ref[:, cols] = dvp_ref[j, :, PAD:PAD + S].T.astype(dv_ref.dtype)

    vec = pl.BlockSpec((1, PAIR), lambda p, i: (0, 0))
    tile = pl.BlockSpec((2 * GP, CA_T, CA_W), lambda p, i: (p, 0, 0))
    full = pl.BlockSpec((S, GW), lambda p, i: (0, p))
    return _call_hosted(
        body, name, (nb, S // CA_T),
        [pl.BlockSpec((CA_T, GW), lambda p, i: (i, 3 * nb + p)),
         pl.BlockSpec((S, GW), lambda p, i: (0, 4 * nb + p)), pl.BlockSpec((S, GW), lambda p, i: (0, 5 * nb + p)),
         tile, vec, vec, pl.BlockSpec((CA_T, GW), lambda p, i: (i, nb + p))],
        [pl.BlockSpec((CA_T, GW), lambda p, i: (i, p)), full, full, tile, vec, vec],
        [jax.ShapeDtypeStruct((S, half), _MXU)] * 3
        + [jax.ShapeDtypeStruct(bias2.shape, F32), jax.ShapeDtypeStruct((1, PAIR), F32),
           jax.ShapeDtypeStruct((1, PAIR), F32)],
        [pltpu.VMEM((GP, PAD + S, PAIR), _MXU), pltpu.VMEM((GP, PAD + S, PAIR), _MXU),
         pltpu.VMEM((GP, PAIR, PAD + S), F32), pltpu.VMEM((GP, PAIR, PAD + S), F32)],
        [proj, proj, proj, bias2, gq2, gk2, dmixed], ex)


def _pack_small(parts):
    flat = jnp.concatenate([p.reshape(-1) for layer in parts for p in layer])
    n = flat.shape[0]
    n_pad = -(-n // 1024) * 1024
    return jnp.pad(flat, (0, n_pad - n)).reshape(1, n_pad)


def _unpack_small(flat, shapes):
    out, off = [], 0
    for layer in shapes:
        cur = []
        for shp in layer:
            size = 1
            for s in shp:
                size *= s
            cur.append(flat[off:off + size].reshape(shp))
            off += size
        out.append(cur)
    return out


def kernel(x, c, g_norm1, w_in, g_q, g_k, rel_bias, w_o, g_norm2, w1, w2, w_ada, b_ada, loss_target, m_g_norm1, m_w_in, m_g_q, m_g_k, m_rel_bias, m_w_o, m_g_norm2, m_w1, m_w2, m_w_ada, m_b_ada, v_g_norm1, v_w_in, v_g_q, v_g_k, v_rel_bias, v_w_o, v_g_norm2, v_w1, v_w2, v_w_ada, v_b_ada):
    L = w_in.shape[0]
    S, D = x.shape[1:]
    H2 = D // HEAD_DIM // 2
    Ca = w_ada.shape[2]
    xi, yi, ci = _pos()
    me = 4 * xi + 2 * yi + ci
    place = jnp.stack([2 * xi + yi, ci]).astype(jnp.int32)

    wire = lambda a: a.astype(_MXU)
    by_cols = lambda g: g.transpose(1, 0, 2).reshape(D, g.shape[0] * g.shape[2])
    b_cols = lax.dynamic_slice(b_ada, (0, me * Ca), (L, Ca))
    c_all, mod_all, first = _start(c, w_ada, b_cols, wire(w_in[0]), "start")
    c_all = c_all.reshape(NDEV, D)
    mod = lax.dynamic_index_in_dim(mod_all, me, axis=1, keepdims=False)
    mod = mod.reshape(NDEV, L, Ca).transpose(1, 0, 2).reshape(L, 6, 1, D)
    W_in = {0: by_cols(first)}
    W_o, W_1, W_2 = {}, {}, {}

    xs = [x[0]]
    saved = []
    for l in range(L):
        sh1, sc1, gt1, sh2, sc2, gt2 = [mod[l, i] for i in range(6)]
        gn1, gn2 = g_norm1[l:l + 1], g_norm2[l:l + 1]
        gq2, gk2 = jnp.tile(g_q[l:l + 1], (1, 2)), jnp.tile(g_k[l:l + 1], (1, 2))
        proj, h1 = _ln_mod_matmul(xs[-1], gn1, sc1, sh1, W_in[l], f"l{l}_proj")
        (o_sb, ox_sb), got = _sb_fwd(proj, f"l{l}_sb_fwd", _gather_exchange([wire(w1[l]), wire(w2[l])]))
        W_1[l], W_2[l] = by_cols(got[0]), got[1].reshape(4 * D, D)
        bias2 = _ca_bias(rel_bias[l], f"l{l}_ca_bias")
        nxt = [wire(w_in[l + 1])] if l + 1 < L else []
        (o_ca,), got = _ca_fwd(proj, bias2, gq2, gk2, f"l{l}_ca_fwd", _gather_exchange([wire(w_o[l])] + nxt))
        W_o[l] = got[0].reshape(D, D)
        if nxt:
            W_in[l + 1] = by_cols(got[1])
        mixed = jnp.concatenate([o_sb, o_ca], axis=1)
        x1, f1, u, h2 = _attn_out_mlp_in(mixed, W_o[l], xs[-1], gt1, gn2, sc2, sh2, W_1[l], f"l{l}_attn_out_mlp_in")
        x0 = xs[-1]
        if l + 1 < L:
            x2, f2 = _matmul_res_gate(u, W_2[l], x1, gt2, True, f"l{l}_mlp_out")
            xs.append(x2)
        else:
            dx, f2, loss_part = _matmul_res_gate(u, W_2[l], x1, gt2, True, f"l{l}_mlp_out", loss_target[0])
        saved.append(dict(x0=x0, h1=h1, proj=proj, ox_sb=ox_sb, bias2=bias2, mixed=mixed, f1=f1, x1=x1,
                          h2=h2, u=u, f2=f2))

    owns, recv_b = {}, {}
    ready = []
    small_parts = [None] * L

    def partials(keys, grads, recv_a):
        parts = []
        for key, g, r in zip(keys, grads, recv_a):
            owns[key], part = _rs_chip_partial(place, g, r, f"rs_partial_l{key[0]}_{key[1]}")
            parts.append(part)
        return parts

    for l in reversed(range(L)):
        sv = saved[l]
        sh1, sc1, gt1, sh2, sc2, gt2 = [mod[l, i] for i in range(6)]
        gn1, gn2 = g_norm1[l:l + 1], g_norm2[l:l + 1]
        gq2, gk2 = jnp.tile(g_q[l:l + 1], (1, 2)), jnp.tile(g_k[l:l + 1], (1, 2))
        dz2, dgt2, du = _gate_nt_matmul(dx, sv["f2"], gt2, W_2[l], sv["u"], f"l{l}_mlp_out_bwd")
        gw2 = _tn_matmul(sv["u"], dz2, False, True, f"l{l}_gw2")
        gw1 = _tn_matmul(sv["h2"], du, True, False, f"l{l}_gw1")
        dx, dsh2, dsc2, dgn2 = _nt_ln_bwd(du, W_1[l], sv["x1"], gn2, sc2, sh2, dx, f"l{l}_mlp_in_bwd")
        dz1, dgt1, dmixed = _gate_nt_matmul(dx, sv["f1"], gt1, W_o[l], None, f"l{l}_attn_out_bwd")
        gwo = _tn_matmul(sv["mixed"], dz1, False, False, f"l{l}_gwo")
        ready += [((l, 1), gwo), ((l, 2), gw1), ((l, 3), gw2)]
        keys, grads = [k for k, _ in ready], [g for _, g in ready]
        (dq_sb, dk_sb, dv_sb), recv_a = _sb_bwd(sv["proj"], sv["ox_sb"], dmixed, f"l{l}_sb_bwd",
                                                _sibling_exchange(grads))
        parts = partials(keys, grads, recv_a)
        (dq_ca, dk_ca, dv_ca, dbias2, dgq2, dgk2), got = _ca_bwd(sv["proj"], sv["bias2"], gq2, gk2, dmixed,
                                                                 f"l{l}_ca_bwd", _chip_exchange(parts))
        recv_b.update(zip(keys, got))
        dgq = dgq2[:, :HEAD_DIM] + dgq2[:, HEAD_DIM:]
        dgk = dgk2[:, :HEAD_DIM] + dgk2[:, HEAD_DIM:]
        drb = _ca_bias_bwd(dbias2, f"l{l}_ca_bias_bwd")
        dproj = jnp.concatenate([dq_sb, dk_sb, dv_sb, dq_ca, dk_ca, dv_ca], axis=1)
        gwin = _tn_matmul(sv["h1"], dproj, True, False, f"l{l}_gwin")
        ready = [((l, 0), gwin)]
        dx, dsh1, dsc1, dgn1 = _nt_ln_bwd(dproj, W_in[l], sv["x0"], gn1, sc1, sh1, dx, f"l{l}_proj_bwd")
        dmod = jnp.concatenate([dsh1, dsc1, dgt1, dsh2, dsc2, dgt2], axis=1)
        small_parts[l] = [dgn1, dgq, dgk, drb, dgn2, dmod]
    grad_x = dx[None]

    keys, grads = [k for k, _ in ready], [g for _, g in ready]
    parts = partials(keys, grads, _run_exchange(_sibling_exchange(grads), "rs_sibling_last"))
    recv_b.update(zip(keys, _run_exchange(_chip_exchange(parts), "rs_chips_last")))
    big_out = []
    for t, (w, m, v) in enumerate([(w_in, m_w_in, v_w_in), (w_o, m_w_o, v_w_o), (w1, m_w1, v_w1), (w2, m_w2, v_w2)]):
        big_out.append(_rs_sum_adamw([owns[(l, t)] for l in range(L)], [recv_b[(l, t)] for l in range(L)],
                                     w, m, v, f"adamw_big_{t}"))

    packed = _pack_small(small_parts)
    gathered_small = _all_gather_small(packed, "ag_small_grads")
    small_sum = _sum_devices(gathered_small, "sum_small_grads")
    shapes = [[(1, D), (1, HEAD_DIM), (1, HEAD_DIM), (H2, N_REL), (1, D), (1, 6 * D)]] * L
    names = ["g_norm1", "g_q", "g_k", "rel_bias", "g_norm2", "b_ada"]
    small_w = {"g_norm1": (g_norm1, m_g_norm1, v_g_norm1), "g_q": (g_q, m_g_q, v_g_q), "g_k": (g_k, m_g_k, v_g_k),
               "rel_bias": (rel_bias, m_rel_bias, v_rel_bias), "g_norm2": (g_norm2, m_g_norm2, v_g_norm2),
               "b_ada": (b_ada, m_b_ada, v_b_ada)}
    packs = [_pack_small([[small_w[n][k][l] for n in names] for l in range(L)]) for k in range(3)]
    n_pad = packed.shape[1]
    as_rows = lambda a: a.reshape(n_pad // 128, 128)
    sd, sm, sv_ = _adamw(as_rows(packs[0]), as_rows(small_sum), as_rows(packs[1]), as_rows(packs[2]), "adamw_small")
    small_out = {}
    for key, flat in [("grad", small_sum), ("delta", sd), ("m", sm), ("v", sv_)]:
        per_layer = _unpack_small(flat.reshape(-1), shapes)
        for i, n in enumerate(names):
            small_out[(key, n)] = jnp.stack([per_layer[l][i].reshape(small_w[n][0].shape[1:]) for l in range(L)])

    layer_len = 2 * D + 2 * HEAD_DIM + H2 * N_REL + 6 * D
    rows = gathered_small.reshape(NDEV, n_pad)
    dmod_all = jnp.stack([rows[:, l * layer_len + layer_len - 6 * D:(l + 1) * layer_len] for l in range(L)])
    dmod_cols = lax.dynamic_slice(dmod_all, (0, 0, me * Ca), (L, NDEV, Ca))
    dmod_cols = jnp.pad(dmod_cols, ((0, 0), (0, 128 - NDEV), (0, 0)))
    c_t = jnp.pad(c_all.T, ((0, 0), (0, 128 - NDEV)))
    g_ada = _w_ada_grad(c_t, dmod_cols, "w_ada_grad")
    flat2 = lambda a: a.reshape(L * D, Ca)
    ad, am, av = _adamw(flat2(w_ada), flat2(g_ada), flat2(m_w_ada), flat2(v_w_ada), "adamw_w_ada")
    ada_out = [g_ada] + [a.reshape(L, D, Ca) for a in (ad, am, av)]

    def leaf(kind):
        k = {"grad": 0, "delta": 1, "m": 2, "v": 3}[kind]
        return [small_out[(kind, "g_norm1")], big_out[0][k], small_out[(kind, "g_q")], small_out[(kind, "g_k")],
                small_out[(kind, "rel_bias")], big_out[1][k], small_out[(kind, "g_norm2")], big_out[2][k],
                big_out[3][k], ada_out[k], small_out[(kind, "b_ada")]]

    loss = lax.psum(loss_part[0, 0], ("x", "y", "c"))
    return (loss, grad_x, *leaf("grad"), *leaf("delta"), *leaf("m"), *leaf("v"))
```

```python
import functools

import jax
import jax.numpy as jnp
from jax import lax
from jax.experimental import pallas as pl
from jax.experimental.pallas import tpu as pltpu

F32 = jnp.float32
_MXU = jnp.bfloat16

HEAD_DIM = 64
CHUNK = 64
LEFT_CHUNKS = 8
PAD = LEFT_CHUNKS * CHUNK
BAND = PAD + CHUNK
REL_CLIP = 128
N_REL = 2 * REL_CLIP + 1
EPS = 1e-6
NEG = -1e30
NDEV = 8
SB_T = 128
CA_T = 2 * CHUNK
CA_W = CA_T + PAD
SB_SKIP = -104.0
PAIR = 2 * HEAD_DIM
SB_PAIRS = 4
CA_PAIRS_FWD = 4
CA_PAIRS_BWD = 2
ROW_BLOCK = 512
SKEW_W = CA_W + CA_T - 1

ADAM_LR, ADAM_B1, ADAM_B2, ADAM_EPS, ADAM_WD, ADAM_STEP = 0.001, 0.9, 0.999, 1e-08, 0.01, 10

MESH = pl.DeviceIdType.MESH
VMEM_SPEC = pl.BlockSpec(memory_space=pltpu.VMEM)
ANY_SPEC = pl.BlockSpec(memory_space=pl.ANY)


def _nn(a, b):
    return lax.dot_general(a, b, (((1,), (0,)), ((), ())), preferred_element_type=F32)


def _nt(a, b):
    return lax.dot_general(a, b, (((1,), (1,)), ((), ())), preferred_element_type=F32)


def _tn(a, b):
    return lax.dot_general(a, b, (((0,), (0,)), ((), ())), preferred_element_type=F32)


def _blk(n, pref):
    return pref if n % pref == 0 else n


def _pos():
    return lax.axis_index("x"), lax.axis_index("y"), lax.axis_index("c")


def _flip(v, bit):
    return 1 - v if bit else v


def _gather_small(x_ref, out_ref, send_sems, recv_sems):
    R, C = x_ref.shape
    x, y, c = _pos()
    me = 4 * x + 2 * y + c

    def peer(k):
        return (_flip(x, k & 4), _flip(y, k & 2), _flip(c, k & 1))

    def copy(k, slot):
        return pltpu.make_async_remote_copy(
            src_ref=x_ref, dst_ref=out_ref.at[slot], send_sem=send_sems.at[k - 1],
            recv_sem=recv_sems.at[k - 1], device_id=peer(k), device_id_type=MESH)

    out_ref[pl.ds(me, 1), :, :] = x_ref[...].reshape(1, R, C)
    sends = [copy(k, me) for k in range(1, NDEV)]
    for cp in sends:
        cp.start()
    for k in range(1, NDEV):
        px, py, pc = peer(k)
        copy(k, 4 * px + 2 * py + pc).wait_recv()
    for cp in sends:
        cp.wait_send()


def _all_gather_small(blk, name):
    return pl.pallas_call(
        lambda x_ref, out_ref, send_sems, recv_sems: _gather_small(x_ref, out_ref, send_sems, recv_sems), name=name,
        out_shape=jax.ShapeDtypeStruct((NDEV,) + blk.shape, blk.dtype),
        in_specs=[VMEM_SPEC], out_specs=VMEM_SPEC,
        scratch_shapes=[pltpu.SemaphoreType.DMA((NDEV - 1,)), pltpu.SemaphoreType.DMA((NDEV - 1,))],
    )(blk)


class _Exchange:
    def __init__(self, inputs, out_shapes, sems, start, finish, middle=None):
        self.inputs, self.out_shapes, self.sems = list(inputs), list(out_shapes), list(sems)
        self.start, self.middle, self.finish = start, middle, finish


def _run_exchange(ex, name):
    n_in, n_out = len(ex.inputs), len(ex.out_shapes)

    def body(*refs):
        ins, outs, sems = refs[:n_in], refs[n_in:n_in + n_out], refs[n_in + n_out:]
        ex.start(ins, outs, sems)
        if ex.middle is not None:
            ex.middle(ins, outs, sems)
        ex.finish(ins, outs, sems)

    return pl.pallas_call(
        body, name=name, out_shape=ex.out_shapes, in_specs=[ANY_SPEC] * n_in, out_specs=[ANY_SPEC] * n_out,
        scratch_shapes=ex.sems,
    )(*ex.inputs)


def _hosted(body, n_in, n_out, ex, step, steps):
    if ex is None:
        return body
    xi, xo = len(ex.inputs), len(ex.out_shapes)

    def wrapped(*refs):
        own_in, ex_in = refs[:n_in], refs[n_in:n_in + xi]
        rest = refs[n_in + xi:]
        own_out, ex_out = rest[:n_out], rest[n_out:n_out + xo]
        rest = rest[n_out + xo:]
        own_scratch, ex_sems = rest[:len(rest) - len(ex.sems)], rest[len(rest) - len(ex.sems):]
        t = step()
        pl.when(t == 0)(lambda: ex.start(ex_in, ex_out, ex_sems))
        body(*own_in, *own_out, *own_scratch)
        if ex.middle is not None:
            pl.when(t == (steps * 7) // 8)(lambda: ex.middle(ex_in, ex_out, ex_sems))
        pl.when(t == steps - 1)(lambda: ex.finish(ex_in, ex_out, ex_sems))

    return wrapped


def _call_hosted(body, name, grid, in_specs, out_specs, out_shape, scratch, args, ex):
    n_in, n_out = len(in_specs), len(out_specs)
    steps = 1
    for extent in grid:
        steps *= extent

    def step():
        t = pl.program_id(0)
        for axis in range(1, len(grid)):
            t = t * grid[axis] + pl.program_id(axis)
        return t

    if ex is not None:
        in_specs = in_specs + [ANY_SPEC] * len(ex.inputs)
        out_specs = out_specs + [ANY_SPEC] * len(ex.out_shapes)
        out_shape = out_shape + ex.out_shapes
        scratch = scratch + ex.sems
        args = args + ex.inputs
    outs = pl.pallas_call(
        _hosted(body, n_in, n_out, ex, step, steps), name=name, grid=grid, in_specs=in_specs, out_specs=out_specs,
        out_shape=out_shape, scratch_shapes=scratch,
    )(*args)
    return list(outs[:n_out]), list(outs[n_out:])


def _gather_exchange(shards):
    n = len(shards)

    def setup(ins, outs, sems):
        send_sems, recv_sems, local_sems = sems
        x, y, c = _pos()
        me, sibling = (x, y, c), (x, y, 1 - c)
        chips = [(1 - x, y), (x, 1 - y), (1 - x, 1 - y)]

        def copy(i, k, block, to, src=None):
            px, py, pc = block
            dst = outs[i].at[4 * px + 2 * py + pc]
            return pltpu.make_async_remote_copy(
                src_ref=dst if src is None else src, dst_ref=dst, send_sem=send_sems.at[7 * i + k],
                recv_sem=recv_sems.at[7 * i + k], device_id=to, device_id_type=MESH)

        def mine(i):
            return pltpu.make_async_copy(ins[i], outs[i].at[4 * x + 2 * y + c], local_sems.at[i])

        def first(i):
            return [copy(i, 0, me, sibling, src=ins[i])] + [
                copy(i, 1 + j, me, (*chip, c), src=ins[i]) for j, chip in enumerate(chips)]

        def passed(i, j):
            return copy(i, 4 + j, (*chips[j], c), sibling)

        return me, sibling, chips, c, copy, mine, first, passed

    def start(ins, outs, sems):
        _, _, _, _, _, mine, first, _ = setup(ins, outs, sems)
        for i in range(n):
            mine(i).start()
            for cp in first(i):
                cp.start()

    def middle(ins, outs, sems):
        me, _, chips, c, copy, _, _, passed = setup(ins, outs, sems)
        for j, chip in enumerate(chips):
            for i in range(n):
                copy(i, 1 + j, (*chip, c), me).wait_recv()
                passed(i, j).start()

    def finish(ins, outs, sems):
        me, sibling, chips, c, copy, mine, first, passed = setup(ins, outs, sems)
        for i in range(n):
            copy(i, 0, sibling, me).wait_recv()
            for j, chip in enumerate(chips):
                copy(i, 4 + j, (*chip, 1 - c), me).wait_recv()
        for i in range(n):
            for cp in first(i) + [passed(i, j) for j in range(3)]:
                cp.wait_send()
            mine(i).wait()

    return _Exchange(
        shards, [jax.ShapeDtypeStruct((NDEV,) + s.shape, s.dtype) for s in shards],
        [pltpu.SemaphoreType.DMA((7 * n,)), pltpu.SemaphoreType.DMA((7 * n,)), pltpu.SemaphoreType.DMA((n,))],
        start, finish, middle)


def _sibling_exchange(grads):
    n = len(grads)

    def copies(ins, outs, sems):
        send_sems, recv_sems = sems
        x, y, c = _pos()
        return [pltpu.make_async_remote_copy(
            src_ref=ins[i].at[2 * q + (1 - c)], dst_ref=outs[i].at[q], send_sem=send_sems.at[4 * i + q],
            recv_sem=recv_sems.at[4 * i + q], device_id=(x, y, 1 - c), device_id_type=MESH)
            for i in range(n) for q in range(4)]

    def start(ins, outs, sems):
        for cp in copies(ins, outs, sems):
            cp.start()

    def finish(ins, outs, sems):
        for cp in copies(ins, outs, sems):
            cp.wait()

    return _Exchange(
        grads, [jax.ShapeDtypeStruct((4,) + g.shape[1:], g.dtype) for g in grads],
        [pltpu.SemaphoreType.DMA((4 * n,)), pltpu.SemaphoreType.DMA((4 * n,))], start, finish)


def _chip_exchange(parts):
    n = len(parts)

    def copies(ins, outs, sems):
        send_sems, recv_sems = sems
        x, y, c = _pos()
        return [pltpu.make_async_remote_copy(
            src_ref=ins[i].at[j - 1], dst_ref=outs[i].at[j - 1], send_sem=send_sems.at[3 * i + j - 1],
            recv_sem=recv_sems.at[3 * i + j - 1], device_id=(_flip(x, j & 2), _flip(y, j & 1), c),
            device_id_type=MESH) for i in range(n) for j in range(1, 4)]

    def start(ins, outs, sems):
        for cp in copies(ins, outs, sems):
            cp.start()

    def finish(ins, outs, sems):
        for cp in copies(ins, outs, sems):
            cp.wait()

    return _Exchange(
        parts, [jax.ShapeDtypeStruct(p.shape, p.dtype) for p in parts],
        [pltpu.SemaphoreType.DMA((3 * n,)), pltpu.SemaphoreType.DMA((3 * n,))], start, finish)


def _rs_chip_partial(place, grad, recv, name):
    _, R, C = grad.shape
    tr = _blk(R, 256)

    def body(place_ref, *refs):
        g_refs, r_refs = refs[:4], refs[4:8]
        own_ref, out_ref = refs[8:]
        own_ref[...] = g_refs[0][0] + r_refs[0][0]
        for j in range(1, 4):
            out_ref[j - 1] = (g_refs[j][0] + r_refs[j][0]).astype(out_ref.dtype)

    def g_map(j):
        return lambda i, p: (2 * jnp.bitwise_xor(p[0], j) + p[1], i, 0)

    def r_map(j):
        return lambda i, p: (jnp.bitwise_xor(p[0], j), i, 0)

    grid_spec = pltpu.PrefetchScalarGridSpec(
        num_scalar_prefetch=1, grid=(R // tr,),
        in_specs=[pl.BlockSpec((1, tr, C), g_map(j)) for j in range(4)]
        + [pl.BlockSpec((1, tr, C), r_map(j)) for j in range(4)],
        out_specs=[pl.BlockSpec((tr, C), lambda i, p: (i, 0)), pl.BlockSpec((3, tr, C), lambda i, p: (0, i, 0))])
    return pl.pallas_call(
        body, name=name, grid_spec=grid_spec,
        out_shape=[jax.ShapeDtypeStruct((R, C), F32), jax.ShapeDtypeStruct((3, R, C), _MXU)],
    )(place, *([grad] * 4), *([recv] * 4))


def _adamw_math(w, g, m, v):
    m = ADAM_B1 * m + (1.0 - ADAM_B1) * g
    v = ADAM_B2 * v + (1.0 - ADAM_B2) * (g * g)
    m_hat = m / (1.0 - ADAM_B1 ** ADAM_STEP)
    v_hat = v / (1.0 - ADAM_B2 ** ADAM_STEP)
    delta = -ADAM_LR * (m_hat / (jnp.sqrt(v_hat) + ADAM_EPS) + ADAM_WD * w)
    return delta, m, v


def _adamw(w, g, m, v, name):
    R, C = w.shape
    tr = _blk(R, 256)

    def body(w_ref, g_ref, m_ref, v_ref, d_ref, nm_ref, nv_ref):
        d, nm, nv = _adamw_math(w_ref[...], g_ref[...], m_ref[...], v_ref[...])
        d_ref[...] = d
        nm_ref[...] = nm
        nv_ref[...] = nv

    spec = pl.BlockSpec((tr, C), lambda i: (i, 0))
    return pl.pallas_call(
        body, name=name, grid=(R // tr,), in_specs=[spec] * 4, out_specs=[spec] * 3,
        out_shape=[jax.ShapeDtypeStruct((R, C), F32)] * 3,
    )(w, g, m, v)


def _rs_sum_adamw(owns, recvs, w, m, v, name):
    L, R, C = w.shape
    tr = _blk(R, 256)
    nr = R // tr

    def body(o0, o1, r0, r1, w_ref, m_ref, v_ref, g_ref, d_ref, nm_ref, nv_ref):
        def step(o_ref, r_ref):
            g = o_ref[...]
            for j in range(3):
                g = g + r_ref[j].astype(F32)
            d, nm, nv = _adamw_math(w_ref[0], g, m_ref[0], v_ref[0])
            g_ref[0] = g
            d_ref[0] = d
            nm_ref[0] = nm
            nv_ref[0] = nv

        pl.when(pl.program_id(0) == 0)(lambda: step(o0, r0))
        pl.when(pl.program_id(0) == 1)(lambda: step(o1, r1))

    def hold(layer):
        if layer == 0:
            return lambda l, i: i * (1 - l) + (nr - 1) * l
        return lambda l, i: i * l

    own_spec = [pl.BlockSpec((tr, C), functools.partial(lambda l, i, f: (f(l, i), 0), f=hold(k))) for k in range(2)]
    recv_spec = [pl.BlockSpec((3, tr, C), functools.partial(lambda l, i, f: (0, f(l, i), 0), f=hold(k)))
                 for k in range(2)]
    lay = pl.BlockSpec((1, tr, C), lambda l, i: (l, i, 0))
    return pl.pallas_call(
        body, name=name, grid=(L, nr),
        in_specs=own_spec + recv_spec + [lay] * 3, out_specs=[lay] * 4,
        out_shape=[jax.ShapeDtypeStruct((L, R, C), F32)] * 4,
    )(owns[0], owns[1], recvs[0], recvs[1], w, m, v)


def _silu(x):
    return x / (1.0 + jnp.exp(-x))


def _start(c, w_ada, b_cols, w_first, name):
    L, D, Ca = w_ada.shape
    ex = _gather_exchange([w_first])

    def body(c_ref, w_ref, b_ref, first_ref, call_ref, mod_ref, gathered_ref, part_ref, c_send, c_recv, m_send,
             m_recv, *ex_sems):
        _gather_small(c_ref, call_ref, c_send, c_recv)
        ex.start([first_ref], [gathered_ref], ex_sems)
        act = _silu(call_ref[...].reshape(NDEV, D)).astype(_MXU)
        for l in range(L):
            part_ref[:, l * Ca:(l + 1) * Ca] = _nn(act, w_ref[l].astype(_MXU)) + b_ref[l:l + 1, :]
        _gather_small(part_ref, mod_ref, m_send, m_recv)
        ex.middle([first_ref], [gathered_ref], ex_sems)
        ex.finish([first_ref], [gathered_ref], ex_sems)

    pairs = [pltpu.SemaphoreType.DMA((NDEV - 1,))] * 4
    return pl.pallas_call(
        body, name=name,
        out_shape=[jax.ShapeDtypeStruct((NDEV, 1, D), F32), jax.ShapeDtypeStruct((NDEV, NDEV, L * Ca), F32)]
        + ex.out_shapes,
        in_specs=[VMEM_SPEC] * 3 + [ANY_SPEC], out_specs=[VMEM_SPEC, VMEM_SPEC, ANY_SPEC],
        scratch_shapes=[pltpu.VMEM((NDEV, L * Ca), F32)] + pairs + ex.sems,
    )(c, w_ada, b_cols, w_first)


def _w_ada_grad(c_t, dmod_cols, name):
    L, _, Ca = dmod_cols.shape
    D = c_t.shape[0]

    def body(c_ref, d_ref, o_ref):
        act = _silu(c_ref[...]).astype(_MXU)
        for l in range(L):
            o_ref[l] = _nn(act, d_ref[l].astype(_MXU))

    return pl.pallas_call(
        body, name=name, out_shape=jax.ShapeDtypeStruct((L, D, Ca), F32),
        in_specs=[VMEM_SPEC] * 2, out_specs=VMEM_SPEC,
    )(c_t, dmod_cols)


def _sum_devices(gathered, name):
    _, _, N = gathered.shape

    def body(x_ref, o_ref):
        acc = x_ref[0]
        for d in range(1, NDEV):
            acc = acc + x_ref[d]
        o_ref[...] = acc

    return pl.pallas_call(
        body, name=name, out_shape=jax.ShapeDtypeStruct((1, N), F32),
        in_specs=[VMEM_SPEC], out_specs=VMEM_SPEC,
    )(gathered)


def _ln_mod_matmul(x, g, sc, sh, w, name):
    S, D = x.shape
    N = w.shape[1]
    tm = _blk(S, ROW_BLOCK)

    def body(x_ref, g_ref, sc_ref, sh_ref, w_ref, o_ref, h_ref):
        xv = x_ref[...]
        r = lax.rsqrt(jnp.mean(xv * xv, axis=-1, keepdims=True) + EPS)
        hv = ((xv * r) * g_ref[...]) * (1.0 + sc_ref[...]) + sh_ref[...]
        hb = hv.astype(_MXU)
        h_ref[...] = hb
        o_ref[...] = _nn(hb, w_ref[...]).astype(o_ref.dtype)

    vec = pl.BlockSpec((1, D), lambda i: (0, 0))
    row = lambda width: pl.BlockSpec((tm, width), lambda i: (i, 0))
    return pl.pallas_call(
        body, name=name, grid=(S // tm,),
        in_specs=[row(D), vec, vec, vec, pl.BlockSpec((D, N), lambda i: (0, 0))],
        out_specs=[row(N), row(D)],
        out_shape=[jax.ShapeDtypeStruct((S, N), _MXU), jax.ShapeDtypeStruct((S, D), _MXU)],
    )(x, g, sc, sh, w)


def _attn_out_mlp_in(mixed, w_o, xres, gt, g, sc, sh, w1, name):
    S, D = xres.shape
    N = w1.shape[1]
    tm = _blk(S, ROW_BLOCK)

    def body(a_ref, wo_ref, x_ref, gt_ref, g_ref, sc_ref, sh_ref, w1_ref, x1_ref, f_ref, u_ref, h_ref):
        f = _nn(a_ref[...], wo_ref[...])
        f_ref[...] = f.astype(f_ref.dtype)
        xv = x_ref[...] + gt_ref[...] * f
        x1_ref[...] = xv
        r = lax.rsqrt(jnp.mean(xv * xv, axis=-1, keepdims=True) + EPS)
        hb = (((xv * r) * g_ref[...]) * (1.0 + sc_ref[...]) + sh_ref[...]).astype(_MXU)
        h_ref[...] = hb
        u_ref[...] = _nn(hb, w1_ref[...]).astype(u_ref.dtype)

    vec = pl.BlockSpec((1, D), lambda i: (0, 0))
    row = lambda width: pl.BlockSpec((tm, width), lambda i: (i, 0))
    whole = lambda arr: pl.BlockSpec(arr.shape, lambda i: (0, 0))
    return pl.pallas_call(
        body, name=name, grid=(S // tm,),
        in_specs=[row(D), whole(w_o), row(D), vec, vec, vec, vec, whole(w1)],
        out_specs=[row(D), row(D), row(N), row(D)],
        out_shape=[jax.ShapeDtypeStruct((S, D), F32), jax.ShapeDtypeStruct((S, D), _MXU),
                   jax.ShapeDtypeStruct((S, N), _MXU), jax.ShapeDtypeStruct((S, D), _MXU)],
    )(mixed, w_o, xres, gt, g, sc, sh, w1)


def _matmul_res_gate(a, w, xres, gt, relu2, name, target=None):
    S, K = a.shape
    N = w.shape[1]
    tm = _blk(S, ROW_BLOCK)
    last = S // tm - 1
    with_loss = target is not None

    def body(a_ref, w_ref, x_ref, gt_ref, *rest):
        av = a_ref[...]
        if relu2:
            af = jnp.maximum(av.astype(F32), 0.0)
            av = (af * af).astype(_MXU)
        f = _nn(av, w_ref[...])
        out = x_ref[...] + gt_ref[...] * f
        if not with_loss:
            o_ref, f_ref = rest
            o_ref[...] = out
        else:
            t_ref, o_ref, f_ref, l_ref, acc_ref = rest
            i = pl.program_id(0)
            e = out - t_ref[...]
            o_ref[...] = e * (1.0 / N)
            _accumulate(acc_ref, jnp.sum(e * e, axis=0, keepdims=True), i == 0)

            @pl.when(i == last)
            def _():
                l_ref[...] = (0.5 / N) * jnp.sum(acc_ref[...], axis=1, keepdims=True)
        f_ref[...] = f.astype(f_ref.dtype)

    row = lambda width: pl.BlockSpec((tm, width), lambda i: (i, 0))
    in_specs = [row(K), pl.BlockSpec((K, N), lambda i: (0, 0)), row(N), pl.BlockSpec((1, N), lambda i: (0, 0))]
    out_specs = [row(N), row(N)]
    out_shape = [jax.ShapeDtypeStruct((S, N), F32), jax.ShapeDtypeStruct((S, N), _MXU)]
    args = [a, w, xres, gt]
    if with_loss:
        in_specs.append(row(N))
        args.append(target)
        out_specs.append(pl.BlockSpec((1, 1), lambda i: (0, 0)))
        out_shape.append(jax.ShapeDtypeStruct((1, 1), F32))
    return pl.pallas_call(
        body, name=name, grid=(S // tm,), in_specs=in_specs, out_specs=out_specs, out_shape=out_shape,
        scratch_shapes=[pltpu.VMEM((1, N), F32)] if with_loss else [],
    )(*args)


def _accumulate(ref, part, first):
    @pl.when(first)
    def _():
        ref[...] = part

    @pl.when(jnp.logical_not(first))
    def _():
        ref[...] += part


def _gate_nt_matmul(dx, f, gt, w, u, name):
    S, D = dx.shape
    N = w.shape[0]
    tm = _blk(S, ROW_BLOCK)
    with_u = u is not None

    def body(*refs):
        if with_u:
            dx_ref, f_ref, gt_ref, w_ref, u_ref, dz_ref, dgt_ref, res_ref = refs
        else:
            dx_ref, f_ref, gt_ref, w_ref, dz_ref, dgt_ref, res_ref = refs
        dxv = dx_ref[...]
        dz = (dxv * gt_ref[...]).astype(_MXU)
        dz_ref[...] = dz
        _accumulate(dgt_ref, jnp.sum(dxv * f_ref[...].astype(F32), axis=0, keepdims=True), pl.program_id(0) == 0)
        r = _nt(dz, w_ref[...])
        if with_u:
            r = r * (2.0 * jnp.maximum(u_ref[...].astype(F32), 0.0))
        res_ref[...] = r.astype(res_ref.dtype)

    row = lambda width: pl.BlockSpec((tm, width), lambda i: (i, 0))
    in_specs = [row(D), row(D), pl.BlockSpec((1, D), lambda i: (0, 0)), pl.BlockSpec((N, D), lambda i: (0, 0))]
    args = [dx, f, gt, w]
    if with_u:
        in_specs.append(row(N))
        args.append(u)
    return pl.pallas_call(
        body, name=name, grid=(S // tm,), in_specs=in_specs,
        out_specs=[row(D), pl.BlockSpec((1, D), lambda i: (0, 0)), row(N)],
        out_shape=[jax.ShapeDtypeStruct((S, D), _MXU), jax.ShapeDtypeStruct((1, D), F32),
                   jax.ShapeDtypeStruct((S, N), _MXU)],
    )(*args)


def _tn_matmul(a, b, by_col, relu2, name):
    S, Ka = a.shape
    Nb = b.shape[1]
    ts = _blk(S, 2 * ROW_BLOCK)
    half = NDEV // 2
    if by_col:
        R, C = Ka, Nb // NDEV
        a_spec = pl.BlockSpec((ts, Ka), lambda h, k: (k, 0))
        b_spec = pl.BlockSpec((ts, half * C), lambda h, k: (k, h))
    else:
        R, C = Ka // NDEV, Nb
        a_spec = pl.BlockSpec((ts, half * R), lambda h, k: (k, h))
        b_spec = pl.BlockSpec((ts, Nb), lambda h, k: (k, 0))

    def body(a_ref, b_ref, o_ref):
        av = a_ref[...]
        if relu2:
            af = jnp.maximum(av.astype(F32), 0.0)
            av = (af * af).astype(_MXU)
        p = _tn(av, b_ref[...])
        first = pl.program_id(1) == 0
        for d in range(half):
            part = p[:, d * C:(d + 1) * C] if by_col else p[d * R:(d + 1) * R, :]
            _accumulate(o_ref.at[d], part, first)

    return pl.pallas_call(
        body, name=name, grid=(NDEV // half, S // ts), in_specs=[a_spec, b_spec],
        out_specs=pl.BlockSpec((half, R, C), lambda h, k: (h, 0, 0)),
        out_shape=jax.ShapeDtypeStruct((NDEV, R, C), F32),
    )(a, b)


def _nt_ln_bwd(dy, w, x, g, sc, sh, dxres, name):
    S, D = x.shape
    N = w.shape[1]
    tm = _blk(S, ROW_BLOCK)

    def body(dy_ref, w_ref, x_ref, g_ref, sc_ref, sh_ref, dxr_ref, dx_ref, dsh_ref, dsc_ref, dg_ref):
        dh = _nt(dy_ref[...], w_ref[...])
        xv = x_ref[...]
        r = lax.rsqrt(jnp.mean(xv * xv, axis=-1, keepdims=True) + EPS)
        xhat = xv * r
        gv = g_ref[...]
        dn = dh * (1.0 + sc_ref[...])
        dxhat = dn * gv
        dxv = r * (dxhat - xhat * jnp.mean(dxhat * xhat, axis=-1, keepdims=True))
        dx_ref[...] = dxr_ref[...] + dxv
        first = pl.program_id(0) == 0
        _accumulate(dsh_ref, jnp.sum(dh, axis=0, keepdims=True), first)
        _accumulate(dsc_ref, jnp.sum(dh * (xhat * gv), axis=0, keepdims=True), first)
        _accumulate(dg_ref, jnp.sum(dn * xhat, axis=0, keepdims=True), first)

    row = lambda width: pl.BlockSpec((tm, width), lambda i: (i, 0))
    vec = pl.BlockSpec((1, D), lambda i: (0, 0))
    return pl.pallas_call(
        body, name=name, grid=(S // tm,),
        in_specs=[row(N), pl.BlockSpec((D, N), lambda i: (0, 0)), row(D), vec, vec, vec, row(D)],
        out_specs=[row(D), vec, vec, vec],
        out_shape=[jax.ShapeDtypeStruct((S, D), F32)] + [jax.ShapeDtypeStruct((1, D), F32)] * 3,
    )(dy, w, x, g, sc, sh, dxres)


def _split2(v):
    hi = v.astype(_MXU)
    mid = (v - hi.astype(F32)).astype(_MXU)
    return hi, mid


def _tri_sums(vs, tri2):
    T = vs[0].shape[0]
    out = []
    for j in range(len(vs) // 2):
        wide = [jnp.concatenate(_split2(vs[2 * j + e]), axis=1) for e in range(2)]
        for both in _per_head(_nn(jnp.concatenate(wide, axis=0), tri2), T):
            out.append((both[:, :T], both[:, T:]))
    return out


def _tri2(T, inclusive):
    j = lax.broadcasted_iota(jnp.int32, (2 * T, 2 * T), 0) % T
    s = lax.broadcasted_iota(jnp.int32, (2 * T, 2 * T), 1)
    keep = (j >= s) if inclusive else (j > s)
    return jnp.where((s >= T) | keep, 1.0, 0.0).astype(_MXU)


def _log_sigmoid(z):
    return jnp.minimum(z, 0.0) - jnp.log(1.0 + jnp.exp(-jnp.abs(z)))


def _per_head(tall, T):
    return [tall[h * T:(h + 1) * T] for h in range(tall.shape[0] // T)]


def _sb_blocks(q_tall, k2, strict, tri2, carry):
    T = k2[0].shape[0]
    zs = []
    for qt, kblk in zip(q_tall, k2):
        zs += _per_head(_nt(qt, kblk), T)
    lbs, l1s = [], []
    for z in zs:
        lb = _log_sigmoid(z)
        l1 = lb - z
        if strict is not None:
            l1 = jnp.where(strict, l1, 0.0)
        lbs.append(lb)
        l1s.append(l1)
    sums = _tri_sums(l1s, tri2)
    amps, new_carry = [], []
    for lb, (sfx, tot), c in zip(lbs, sums, carry):
        a = jnp.exp(lb + sfx + c)
        if strict is not None:
            a = jnp.where(strict, a, 0.0)
        amps.append(a)
        new_carry.append(c + tot)
    return lbs, amps, new_carry


def _sb_alive(carry):
    top = carry[0]
    for c in carry[1:]:
        top = jnp.maximum(top, c)
    return jnp.max(top) > SB_SKIP


def _skew_index():
    i = lax.broadcasted_iota(jnp.int32, (CA_T, SKEW_W + 1), 0)
    m = lax.broadcasted_iota(jnp.int32, (CA_T, SKEW_W + 1), 1)
    wrapped = i + m >= SKEW_W
    row = jnp.where(wrapped, i + 1, i)
    j = jnp.where(wrapped, i + m - SKEW_W, i + m)
    a = row // CHUNK
    jj = j - a * CHUNK
    inband = (jj >= 0) & (jj < BAND) & (j < CA_W) & (row < CA_T)
    idx = jnp.clip((row - a * CHUNK) + PAD - jj, -REL_CLIP, REL_CLIP) + REL_CLIP
    return inband, idx, wrapped


def _skew(tile):
    H = tile.shape[0]
    flat = jnp.pad(tile, ((0, 0), (0, 0), (0, SKEW_W - CA_W))).reshape(H, CA_T * SKEW_W)
    return jnp.pad(flat, ((0, 0), (0, CA_T))).reshape(H, CA_T, SKEW_W + 1)


def _unskew(view):
    H = view.shape[0]
    flat = view.reshape(H, CA_T * (SKEW_W + 1))[:, :CA_T * SKEW_W]
    return flat.reshape(H, CA_T, SKEW_W)[:, :, :CA_W]


def _ca_bias(rel_bias, name):
    H = rel_bias.shape[0]
    top = rel_bias[:, N_REL - 1:]
    by_offset = jnp.concatenate(
        [jnp.broadcast_to(top, (H, PAD - REL_CLIP + 1)), jnp.flip(rel_bias[:, :N_REL - 1], axis=1),
         jnp.broadcast_to(top, (H, SKEW_W + 1 - (PAD - REL_CLIP + 1) - (N_REL - 1)))], axis=1)

    def body(t_ref, o_ref):
        inband, _, wrapped = _skew_index()
        vals = jnp.where(wrapped, t_ref[0][:, 0:1], t_ref[0])
        o_ref[0] = jnp.where(inband, vals, NEG)

    view = pl.pallas_call(
        body, name=name, grid=(H,), in_specs=[pl.BlockSpec((1, 1, SKEW_W + 1), lambda h: (h, 0, 0))],
        out_specs=pl.BlockSpec((1, CA_T, SKEW_W + 1), lambda h: (h, 0, 0)),
        out_shape=jax.ShapeDtypeStruct((H, CA_T, SKEW_W + 1), F32),
    )(by_offset.reshape(H, 1, SKEW_W + 1))
    return _unskew(view)


def _ca_bias_bwd(dbias, name):
    H = dbias.shape[0]

    def body(d_ref, o_ref):
        inband, idx, _ = _skew_index()
        d = jnp.where(inband, d_ref[0], 0.0)
        clipped = idx == N_REL - 1
        by_offset = jnp.sum(jnp.where(clipped, 0.0, d), axis=0, keepdims=True)
        top = jnp.sum(jnp.sum(jnp.where(clipped, d, 0.0), axis=0, keepdims=True), axis=1, keepdims=True)
        lane = lax.broadcasted_iota(jnp.int32, (1, SKEW_W + 1), 1)
        o_ref[0] = jnp.where(lane == 0, top, by_offset)

    out = pl.pallas_call(
        body, name=name, grid=(H,), in_specs=[pl.BlockSpec((1, CA_T, SKEW_W + 1), lambda h: (h, 0, 0))],
        out_specs=pl.BlockSpec((1, 1, SKEW_W + 1), lambda h: (h, 0, 0)),
        out_shape=jax.ShapeDtypeStruct((H, 1, SKEW_W + 1), F32),
    )(_skew(dbias))[:, 0]
    first = PAD - REL_CLIP + 1
    return jnp.concatenate([jnp.flip(out[:, first:first + N_REL - 1], axis=1), out[:, 0:1]], axis=1)


def _low_lanes(rows):
    return lax.broadcasted_iota(jnp.int32, (rows, PAIR), 1) < HEAD_DIM


def _one_head(t2, low, first, scale=1.0):
    tf = t2.astype(F32) * scale
    return (jnp.where(low, tf, 0.0) if first else jnp.where(low, 0.0, tf)).astype(_MXU)


def _two_heads(t2, low, scale=1.0):
    return jnp.concatenate([_one_head(t2, low, True, scale), _one_head(t2, low, False, scale)], axis=0)


def _sb_fwd(proj, name, ex=None):
    S, W = proj.shape
    half = W // 6
    npair = half // PAIR
    T = _blk(S, SB_T)
    GP = _blk(npair, SB_PAIRS)
    GW = GP * PAIR
    nb = npair // GP

    def body(q_ref, k_ref, v_ref, o_ref, ox_ref):
        qi = pl.program_id(1)
        low = _low_lanes(T)
        q_tall = []
        for j in range(GP):
            q2 = q_ref[:, j * PAIR:(j + 1) * PAIR]
            q_tall.append(_two_heads(q2, low, HEAD_DIM ** -0.5))
        row = lax.broadcasted_iota(jnp.int32, (T, T), 0)
        col = lax.broadcasted_iota(jnp.int32, (T, T), 1)
        tri2 = _tri2(T, inclusive=False)

        def pairs(kb, carry, acc, fine, strict):
            rows = pl.ds(pl.multiple_of(kb * T, T), T)
            k2 = [k_ref[rows, j * PAIR:(j + 1) * PAIR] for j in range(GP)]
            v2 = [v_ref[rows, j * PAIR:(j + 1) * PAIR] for j in range(GP)]
            _, amps, carry = _sb_blocks(q_tall, k2, strict, tri2, carry)
            parts = [_split2(a) for a in amps]
            new_acc, new_fine = [], []
            for j in range(GP):
                tall = jnp.concatenate([parts[2 * j][0], parts[2 * j + 1][0], parts[2 * j][1], parts[2 * j + 1][1]],
                                       axis=0)
                hi0, hi1, mid0, mid1 = _per_head(_nn(tall, v2[j]), T)
                new_acc.append(acc[j] + jnp.where(low, hi0, hi1))
                new_fine.append(fine[j] + jnp.where(low, mid0, mid1))
            return tuple(carry), tuple(new_acc), tuple(new_fine)

        zero = (jnp.zeros((T, PAIR), F32),) * GP
        carry, acc, fine = pairs(qi, (jnp.zeros((T, T), F32),) * (2 * GP), zero, zero, col < row)

        def cond(st):
            kb, alive, _, _, _ = st
            return jnp.logical_and(kb >= 0, alive)

        def step(st):
            kb, _, carry, acc, fine = st
            carry, acc, fine = pairs(kb, carry, acc, fine, None)
            return kb - 1, _sb_alive(carry), carry, acc, fine

        _, _, _, acc, fine = lax.while_loop(cond, step, (qi - 1, _sb_alive(carry), carry, acc, fine))
        for j in range(GP):
            o_ref[:, j * PAIR:(j + 1) * PAIR] = acc[j].astype(o_ref.dtype)
            ox_ref[:, j * PAIR:(j + 1) * PAIR] = acc[j] + fine[j]

    blk = pl.BlockSpec((T, GW), lambda p, i: (i, p))
    return _call_hosted(
        body, name, (nb, S // T),
        [blk, pl.BlockSpec((S, GW), lambda p, i: (0, nb + p)), pl.BlockSpec((S, GW), lambda p, i: (0, 2 * nb + p))],
        [blk, blk], [jax.ShapeDtypeStruct((S, half), _MXU), jax.ShapeDtypeStruct((S, half), F32)],
        [], [proj, proj, proj], ex)


def _sb_bwd(proj, ox, dmixed, name, ex=None):
    S, W = proj.shape
    half = W // 6
    npair = half // PAIR
    T = _blk(S, SB_T)
    GP = _blk(npair, SB_PAIRS)
    GW = GP * PAIR
    nb = npair // GP
    last = S // T - 1
    scale = HEAD_DIM ** -0.5

    def body(q_ref, k_ref, v_ref, ox_ref, do_ref, dq_ref, dk_ref, dv_ref, dka_ref, dva_ref):
        qi = pl.program_id(1)

        @pl.when(qi == 0)
        def _():
            dka_ref[...] = jnp.zeros_like(dka_ref)
            dva_ref[...] = jnp.zeros_like(dva_ref)

        low = _low_lanes(T)
        q2, do2, q_tall, do_tall, deltas = [], [], [], [], []
        for j in range(GP):
            cols = slice(j * PAIR, (j + 1) * PAIR)
            q2.append(q_ref[:, cols])
            do2.append(do_ref[:, cols])
            q_tall.append(_two_heads(q2[j], low, scale))
            dobs = [_one_head(do2[j], low, True), _one_head(do2[j], low, False)]
            do_tall.append(jnp.concatenate(dobs, axis=0))
            for e in range(2):
                deltas.append(jnp.sum(dobs[e].astype(F32) * ox_ref[:, cols], axis=-1, keepdims=True))
        row = lax.broadcasted_iota(jnp.int32, (T, T), 0)
        col = lax.broadcasted_iota(jnp.int32, (T, T), 1)
        tri_ex = _tri2(T, inclusive=False)
        tri_in = _tri2(T, inclusive=True)

        def pairs(kb, carry, right, dq, strict):
            rows = pl.ds(pl.multiple_of(kb * T, T), T)
            k2 = [k_ref[rows, j * PAIR:(j + 1) * PAIR] for j in range(GP)]
            v2 = [v_ref[rows, j * PAIR:(j + 1) * PAIR] for j in range(GP)]
            nh = 2 * GP
            gs = []
            for j in range(GP):
                gs += _per_head(_nt(do_tall[j], v2[j]), T)
            lbs, amps, carry = _sb_blocks(q_tall, k2, strict, tri_ex, carry)
            ags = [a * gg for a, gg in zip(amps, gs)]
            sums = _tri_sums(ags, tri_in)
            dzbs = []
            for h in range(nh):
                left = deltas[h] - (sums[h][0] + right[h])
                beta = jnp.exp(lbs[h])
                dz = ags[h] - beta * (ags[h] + left)
                if strict is not None:
                    dz = jnp.where(strict, dz, 0.0)
                dzbs.append(dz.astype(_MXU))
            abs_ = [a.astype(_MXU) for a in amps]
            new_dq = []
            for j in range(GP):
                cols = slice(j * PAIR, (j + 1) * PAIR)
                dk0, dk1 = _per_head(_tn(jnp.concatenate(dzbs[2 * j:2 * j + 2], axis=1), q2[j]), T)
                dv0, dv1 = _per_head(_tn(jnp.concatenate(abs_[2 * j:2 * j + 2], axis=1), do2[j]), T)
                dq0, dq1 = _per_head(_nn(jnp.concatenate(dzbs[2 * j:2 * j + 2], axis=0), k2[j]), T)
                dka_ref[rows, cols] += jnp.where(low, dk0, dk1)
                dva_ref[rows, cols] += jnp.where(low, dv0, dv1)
                new_dq.append(dq[j] + jnp.where(low, dq0, dq1))
            right = tuple(right[h] + sums[h][1] for h in range(nh))
            return tuple(carry), right, tuple(new_dq)

        zero = (jnp.zeros((T, T), F32),) * (2 * GP)
        carry, right, dq = pairs(qi, zero, zero, (jnp.zeros((T, PAIR), F32),) * GP, col < row)

        def cond(st):
            kb, alive, _, _, _ = st
            return jnp.logical_and(kb >= 0, alive)

        def step(st):
            kb, _, carry, right, dq = st
            carry, right, dq = pairs(kb, carry, right, dq, None)
            return kb - 1, _sb_alive(carry), carry, right, dq

        _, _, _, _, dq = lax.while_loop(cond, step, (qi - 1, _sb_alive(carry), carry, right, dq))
        for j in range(GP):
            dq_ref[:, j * PAIR:(j + 1) * PAIR] = (dq[j] * scale).astype(dq_ref.dtype)

        @pl.when(qi == last)
        def _():
            dk_ref[...] = (dka_ref[...] * scale).astype(dk_ref.dtype)
            dv_ref[...] = dva_ref[...].astype(dv_ref.dtype)

    blk = pl.BlockSpec((T, GW), lambda p, i: (i, p))
    full = pl.BlockSpec((S, GW), lambda p, i: (0, p))
    return _call_hosted(
        body, name, (nb, S // T),
        [blk, pl.BlockSpec((S, GW), lambda p, i: (0, nb + p)), pl.BlockSpec((S, GW), lambda p, i: (0, 2 * nb + p)),
         blk, blk],
        [blk, full, full], [jax.ShapeDtypeStruct((S, half), _MXU)] * 3,
        [pltpu.VMEM((S, GW), F32), pltpu.VMEM((S, GW), F32)], [proj, proj, proj, ox, dmixed], ex)


def _pair_norm(t2, g2, low):
    tf = t2.astype(F32)
    sq = tf * tf
    both = jnp.sum(sq, axis=-1, keepdims=True)
    first = jnp.sum(jnp.where(low, sq, 0.0), axis=-1, keepdims=True)
    r = jnp.where(low, lax.rsqrt(first * (1.0 / HEAD_DIM) + EPS), lax.rsqrt((both - first) * (1.0 / HEAD_DIM) + EPS))
    hat = tf * r
    return hat * g2, hat, r


def _pair_norm_bwd(dn, hat, r, g2, low):
    dhat = dn * g2
    prod = dhat * hat
    both = jnp.sum(prod, axis=-1, keepdims=True)
    first = jnp.sum(jnp.where(low, prod, 0.0), axis=-1, keepdims=True)
    mean = jnp.where(low, first, both - first) * (1.0 / HEAD_DIM)
    return r * (dhat - hat * mean)


def _ca_fill(j, k_ref, v_ref, gk_ref, kn_ref, vp_ref):
    S = k_ref.shape[0]
    cols = slice(j * PAIR, (j + 1) * PAIR)
    kn, _, _ = _pair_norm(k_ref[:, cols], gk_ref[...], _low_lanes(S))
    kn_ref[j, 0:PAD, :] = jnp.zeros((PAD, PAIR), kn_ref.dtype)
    vp_ref[j, 0:PAD, :] = jnp.zeros((PAD, PAIR), vp_ref.dtype)
    kn_ref[j, PAD:PAD + S, :] = kn.astype(kn_ref.dtype)
    vp_ref[j, PAD:PAD + S, :] = v_ref[:, cols]


def _ca_scores(j, q_ref, b2_ref, gq_ref, kn_ref, qi, low):
    qn, qhat, r = _pair_norm(q_ref[:, j * PAIR:(j + 1) * PAIR], gq_ref[...], low)
    qn = qn * HEAD_DIM ** -0.5
    band = pl.ds(pl.multiple_of(qi * CA_T, CA_T), CA_W)
    key_pos = qi * CA_T - PAD + lax.broadcasted_iota(jnp.int32, (CA_T, CA_W), 1)
    both = _per_head(_nt(_two_heads(qn, low), kn_ref[j, band, :]), CA_T)
    scores = [jnp.where(key_pos >= 0, both[e] + b2_ref[2 * j + e], NEG) for e in range(2)]
    return scores, qn.astype(_MXU), qhat, r


def _softmax(s):
    e = jnp.exp(s - jnp.max(s, axis=-1, keepdims=True))
    return e * (1.0 / jnp.sum(e, axis=-1, keepdims=True))


def _ca_fwd(proj, bias2, gq2, gk2, name, ex=None):
    S, W = proj.shape
    half = W // 6
    npair = half // PAIR
    GP = _blk(npair, CA_PAIRS_FWD)
    GW = GP * PAIR
    nb = npair // GP

    def body(q_ref, k_ref, v_ref, b2_ref, gq_ref, gk_ref, o_ref, kn_ref, vp_ref):
        qi = pl.program_id(1)

        @pl.when(qi == 0)
        def _():
            for j in range(GP):
                _ca_fill(j, k_ref, v_ref, gk_ref, kn_ref, vp_ref)

        low = _low_lanes(CA_T)
        band = pl.ds(pl.multiple_of(qi * CA_T, CA_T), CA_W)
        scores = [_ca_scores(j, q_ref, b2_ref, gq_ref, kn_ref, qi, low)[0] for j in range(GP)]
        probs = [[_softmax(s).astype(_MXU) for s in pair] for pair in scores]
        for j in range(GP):
            outs = _per_head(_nn(jnp.concatenate(probs[j], axis=0), vp_ref[j, band, :]), CA_T)
            o_ref[:, j * PAIR:(j + 1) * PAIR] = jnp.where(low, outs[0], outs[1]).astype(o_ref.dtype)

    vec = pl.BlockSpec((1, PAIR), lambda p, i: (0, 0))
    return _call_hosted(
        body, name, (nb, S // CA_T),
        [pl.BlockSpec((CA_T, GW), lambda p, i: (i, 3 * nb + p)),
         pl.BlockSpec((S, GW), lambda p, i: (0, 4 * nb + p)), pl.BlockSpec((S, GW), lambda p, i: (0, 5 * nb + p)),
         pl.BlockSpec((2 * GP, CA_T, CA_W), lambda p, i: (p, 0, 0)), vec, vec],
        [pl.BlockSpec((CA_T, GW), lambda p, i: (i, p))], [jax.ShapeDtypeStruct((S, half), _MXU)],
        [pltpu.VMEM((GP, PAD + S, PAIR), _MXU), pltpu.VMEM((GP, PAD + S, PAIR), _MXU)],
        [proj, proj, proj, bias2, gq2, gk2], ex)


def _ca_bwd(proj, bias2, gq2, gk2, dmixed, name, ex=None):
    S, W = proj.shape
    half = W // 6
    npair = half // PAIR
    GP = _blk(npair, CA_PAIRS_BWD)
    GW = GP * PAIR
    nb = npair // GP
    scale = HEAD_DIM ** -0.5
    last = S // CA_T - 1

    def body(q_ref, k_ref, v_ref, b2_ref, gq_ref, gk_ref, do_ref,
             dq_ref, dk_ref, dv_ref, db_ref, dgq_ref, dgk_ref, kn_ref, vp_ref, dkn_ref, dvp_ref):
        p_id, qi = pl.program_id(0), pl.program_id(1)

        @pl.when(qi == 0)
        def _():
            for j in range(GP):
                _ca_fill(j, k_ref, v_ref, gk_ref, kn_ref, vp_ref)
            dkn_ref[...] = jnp.zeros_like(dkn_ref)
            dvp_ref[...] = jnp.zeros_like(dvp_ref)
            db_ref[...] = jnp.zeros_like(db_ref)

        @pl.when(jnp.logical_and(p_id == 0, qi == 0))
        def _():
            dgq_ref[...] = jnp.zeros_like(dgq_ref)
            dgk_ref[...] = jnp.zeros_like(dgk_ref)

        low = _low_lanes(CA_T)
        top_w = lax.broadcasted_iota(jnp.int32, (PAIR, CA_W), 0) < HEAD_DIM
        band = pl.ds(pl.multiple_of(qi * CA_T, CA_T), CA_W)
        pairs = [_ca_scores(j, q_ref, b2_ref, gq_ref, kn_ref, qi, low) for j in range(GP)]
        do2 = [do_ref[:, j * PAIR:(j + 1) * PAIR] for j in range(GP)]
        dps = [_per_head(_nt(_two_heads(do2[j], low), vp_ref[j, band, :]), CA_T) for j in range(GP)]
        probs, dsbs = [], []
        for j in range(GP):
            pj, dj = [], []
            for e in range(2):
                p = _softmax(pairs[j][0][e])
                ds = p * (dps[j][e] - jnp.sum(p * dps[j][e], axis=-1, keepdims=True))
                db_ref[2 * j + e] += ds
                pj.append(p.astype(_MXU))
                dj.append(ds.astype(_MXU))
            probs.append(pj)
            dsbs.append(dj)
        dgq = jnp.zeros((1, PAIR), F32)
        for j in range(GP):
            _, qn, qhat, r = pairs[j]
            dq_h = _per_head(_nn(jnp.concatenate(dsbs[j], axis=0), kn_ref[j, band, :]), CA_T)
            dk_t = _tn(qn, jnp.concatenate(dsbs[j], axis=1))
            dv_t = _tn(do2[j], jnp.concatenate(probs[j], axis=1))
            dkn_ref[j, :, band] += jnp.where(top_w, dk_t[:, :CA_W], dk_t[:, CA_W:])
            dvp_ref[j, :, band] += jnp.where(top_w, dv_t[:, :CA_W], dv_t[:, CA_W:])
            dqn = jnp.where(low, dq_h[0], dq_h[1]) * scale
            dgq = dgq + jnp.sum(dqn * qhat, axis=0, keepdims=True)
            dq_ref[:, j * PAIR:(j + 1) * PAIR] = _pair_norm_bwd(dqn, qhat, r, gq_ref[...], low).astype(dq_ref.dtype)
        dgq_ref[...] += dgq

        @pl.when(qi == last)
        def _():
            low_s = _low_lanes(S)
            for j in range(GP):
                cols = slice(j * PAIR, (j + 1) * PAIR)
                _, khat, rk = _pair_norm(k_ref[:, cols], gk_ref[...], low_s)
                dkn = dkn_ref[j, :, PAD:PAD + S].T
                dgk_ref[...] += jnp.sum(dkn * khat, axis=0, keepdims=True)
                dk_ref[:, cols] = _pair_norm_bwd(dkn, khat, rk, gk_ref[...], low_s).astype(dk_ref.dtype)
                dv_ref[:, cols] = dvp_ref[j, :, PAD:PAD + S].T.astype(dv_ref.dtype)

    vec = pl.BlockSpec((1, PAIR), lambda p, i: (0, 0))
    tile = pl.BlockSpec((2 * GP, CA_T, CA_W), lambda p, i: (p, 0, 0))
    full = pl.BlockSpec((S, GW), lambda p, i: (0, p))
    return _call_hosted(
        body, name, (nb, S // CA_T),
        [pl.BlockSpec((CA_T, GW), lambda p, i: (i, 3 * nb + p)),
         pl.BlockSpec((S, GW), lambda p, i: (0, 4 * nb + p)), pl.BlockSpec((S, GW), lambda p, i: (0, 5 * nb + p)),
         tile, vec, vec, pl.BlockSpec((CA_T, GW), lambda p, i: (i, nb + p))],
        [pl.BlockSpec((CA_T, GW), lambda p, i: (i, p)), full, full, tile, vec, vec],
        [jax.ShapeDtypeStruct((S, half), _MXU)] * 3
        + [jax.ShapeDtypeStruct(bias2.shape, F32), jax.ShapeDtypeStruct((1, PAIR), F32),
           jax.ShapeDtypeStruct((1, PAIR), F32)],
        [pltpu.VMEM((GP, PAD + S, PAIR), _MXU), pltpu.VMEM((GP, PAD + S, PAIR), _MXU),
         pltpu.VMEM((GP, PAIR, PAD + S), F32), pltpu.VMEM((GP, PAIR, PAD + S), F32)],
        [proj, proj, proj, bias2, gq2, gk2, dmixed], ex)


def _pack_small(parts):
    flat = jnp.concatenate([p.reshape(-1) for layer in parts for p in layer])
    n = flat.shape[0]
    n_pad = -(-n // 1024) * 1024
    return jnp.pad(flat, (0, n_pad - n)).reshape(1, n_pad)


def _unpack_small(flat, shapes):
    out, off = [], 0
    for layer in shapes:
        cur = []
        for shp in layer:
            size = 1
            for s in shp:
                size *= s
            cur.append(flat[off:off + size].reshape(shp))
            off += size
        out.append(cur)
    return out


def kernel(x, c, g_norm1, w_in, g_q, g_k, rel_bias, w_o, g_norm2, w1, w2, w_ada, b_ada, loss_target, m_g_norm1, m_w_in, m_g_q, m_g_k, m_rel_bias, m_w_o, m_g_norm2, m_w1, m_w2, m_w_ada, m_b_ada, v_g_norm1, v_w_in, v_g_q, v_g_k, v_rel_bias, v_w_o, v_g_norm2, v_w1, v_w2, v_w_ada, v_b_ada):
    L = w_in.shape[0]
    S, D = x.shape[1:]
    H2 = D // HEAD_DIM // 2
    Ca = w_ada.shape[2]
    xi, yi, ci = _pos()
    me = 4 * xi + 2 * yi + ci
    place = jnp.stack([2 * xi + yi, ci]).astype(jnp.int32)

    wire = lambda a: a.astype(_MXU)
    by_cols = lambda g: g.transpose(1, 0, 2).reshape(D, g.shape[0] * g.shape[2])
    b_cols = lax.dynamic_slice(b_ada, (0, me * Ca), (L, Ca))
    c_all, mod_all, first = _start(c, w_ada, b_cols, wire(w_in[0]), "start")
    c_all = c_all.reshape(NDEV, D)
    mod = lax.dynamic_index_in_dim(mod_all, me, axis=1, keepdims=False)
    mod = mod.reshape(NDEV, L, Ca).transpose(1, 0, 2).reshape(L, 6, 1, D)
    W_in = {0: by_cols(first)}
    W_o, W_1, W_2 = {}, {}, {}

    xs = [x[0]]
    saved = []
    for l in range(L):
        sh1, sc1, gt1, sh2, sc2, gt2 = [mod[l, i] for i in range(6)]
        gn1, gn2 = g_norm1[l:l + 1], g_norm2[l:l + 1]
        gq2, gk2 = jnp.tile(g_q[l:l + 1], (1, 2)), jnp.tile(g_k[l:l + 1], (1, 2))
        proj, h1 = _ln_mod_matmul(xs[-1], gn1, sc1, sh1, W_in[l], f"l{l}_proj")
        (o_sb, ox_sb), got = _sb_fwd(proj, f"l{l}_sb_fwd", _gather_exchange([wire(w1[l]), wire(w2[l])]))
        W_1[l], W_2[l] = by_cols(got[0]), got[1].reshape(4 * D, D)
        bias2 = _ca_bias(rel_bias[l], f"l{l}_ca_bias")
        nxt = [wire(w_in[l + 1])] if l + 1 < L else []
        (o_ca,), got = _ca_fwd(proj, bias2, gq2, gk2, f"l{l}_ca_fwd", _gather_exchange([wire(w_o[l])] + nxt))
        W_o[l] = got[0].reshape(D, D)
        if nxt:
            W_in[l + 1] = by_cols(got[1])
        mixed = jnp.concatenate([o_sb, o_ca], axis=1)
        x1, f1, u, h2 = _attn_out_mlp_in(mixed, W_o[l], xs[-1], gt1, gn2, sc2, sh2, W_1[l], f"l{l}_attn_out_mlp_in")
        x0 = xs[-1]
        if l + 1 < L:
            x2, f2 = _matmul_res_gate(u, W_2[l], x1, gt2, True, f"l{l}_mlp_out")
            xs.append(x2)
        else:
            dx, f2, loss_part = _matmul_res_gate(u, W_2[l], x1, gt2, True, f"l{l}_mlp_out", loss_target[0])
        saved.append(dict(x0=x0, h1=h1, proj=proj, ox_sb=ox_sb, bias2=bias2, mixed=mixed, f1=f1, x1=x1,
                          h2=h2, u=u, f2=f2))

    owns, recv_b = {}, {}
    ready = []
    small_parts = [None] * L

    def partials(keys, grads, recv_a):
        parts = []
        for key, g, r in zip(keys, grads, recv_a):
            owns[key], part = _rs_chip_partial(place, g, r, f"rs_partial_l{key[0]}_{key[1]}")
            parts.append(part)
        return parts

    for l in reversed(range(L)):
        sv = saved[l]
        sh1, sc1, gt1, sh2, sc2, gt2 = [mod[l, i] for i in range(6)]
        gn1, gn2 = g_norm1[l:l + 1], g_norm2[l:l + 1]
        gq2, gk2 = jnp.tile(g_q[l:l + 1], (1, 2)), jnp.tile(g_k[l:l + 1], (1, 2))
        dz2, dgt2, du = _gate_nt_matmul(dx, sv["f2"], gt2, W_2[l], sv["u"], f"l{l}_mlp_out_bwd")
        gw2 = _tn_matmul(sv["u"], dz2, False, True, f"l{l}_gw2")
        gw1 = _tn_matmul(sv["h2"], du, True, False, f"l{l}_gw1")
        dx, dsh2, dsc2, dgn2 = _nt_ln_bwd(du, W_1[l], sv["x1"], gn2, sc2, sh2, dx, f"l{l}_mlp_in_bwd")
        dz1, dgt1, dmixed = _gate_nt_matmul(dx, sv["f1"], gt1, W_o[l], None, f"l{l}_attn_out_bwd")
        gwo = _tn_matmul(sv["mixed"], dz1, False, False, f"l{l}_gwo")
        ready += [((l, 1), gwo), ((l, 2), gw1), ((l, 3), gw2)]
        keys, grads = [k for k, _ in ready], [g for _, g in ready]
        (dq_sb, dk_sb, dv_sb), recv_a = _sb_bwd(sv["proj"], sv["ox_sb"], dmixed, f"l{l}_sb_bwd",
                                                _sibling_exchange(grads))
        parts = partials(keys, grads, recv_a)
        (dq_ca, dk_ca, dv_ca, dbias2, dgq2, dgk2), got = _ca_bwd(sv["proj"], sv["bias2"], gq2, gk2, dmixed,
                                                                 f"l{l}_ca_bwd", _chip_exchange(parts))
        recv_b.update(zip(keys, got))
        dgq = dgq2[:, :HEAD_DIM] + dgq2[:, HEAD_DIM:]
        dgk = dgk2[:, :HEAD_DIM] + dgk2[:, HEAD_DIM:]
        drb = _ca_bias_bwd(dbias2, f"l{l}_ca_bias_bwd")
        dproj = jnp.concatenate([dq_sb, dk_sb, dv_sb, dq_ca, dk_ca, dv_ca], axis=1)
        gwin = _tn_matmul(sv["h1"], dproj, True, False, f"l{l}_gwin")
        ready = [((l, 0), gwin)]
        dx, dsh1, dsc1, dgn1 = _nt_ln_bwd(dproj, W_in[l], sv["x0"], gn1, sc1, sh1, dx, f"l{l}_proj_bwd")
        dmod = jnp.concatenate([dsh1, dsc1, dgt1, dsh2, dsc2, dgt2], axis=1)
        small_parts[l] = [dgn1, dgq, dgk, drb, dgn2, dmod]
    grad_x = dx[None]

    keys, grads = [k for k, _ in ready], [g for _, g in ready]
    parts = partials(keys, grads, _run_exchange(_sibling_exchange(grads), "rs_sibling_last"))
    recv_b.update(zip(keys, _run_exchange(_chip_exchange(parts), "rs_chips_last")))
    big_out = []
    for t, (w, m, v) in enumerate([(w_in, m_w_in, v_w_in), (w_o, m_w_o, v_w_o), (w1, m_w1, v_w1), (w2, m_w2, v_w2)]):
        big_out.append(_rs_sum_adamw([owns[(l, t)] for l in range(L)], [recv_b[(l, t)] for l in range(L)],
                                     w, m, v, f"adamw_big_{t}"))

    packed = _pack_small(small_parts)
    gathered_small = _all_gather_small(packed, "ag_small_grads")
    small_sum = _sum_devices(gathered_small, "sum_small_grads")
    shapes = [[(1, D), (1, HEAD_DIM), (1, HEAD_DIM), (H2, N_REL), (1, D), (1, 6 * D)]] * L
    names = ["g_norm1", "g_q", "g_k", "rel_bias", "g_norm2", "b_ada"]
    small_w = {"g_norm1": (g_norm1, m_g_norm1, v_g_norm1), "g_q": (g_q, m_g_q, v_g_q), "g_k": (g_k, m_g_k, v_g_k),
               "rel_bias": (rel_bias, m_rel_bias, v_rel_bias), "g_norm2": (g_norm2, m_g_norm2, v_g_norm2),
               "b_ada": (b_ada, m_b_ada, v_b_ada)}
    packs = [_pack_small([[small_w[n][k][l] for n in names] for l in range(L)]) for k in range(3)]
    n_pad = packed.shape[1]
    as_rows = lambda a: a.reshape(n_pad // 128, 128)
    sd, sm, sv_ = _adamw(as_rows(packs[0]), as_rows(small_sum), as_rows(packs[1]), as_rows(packs[2]), "adamw_small")
    small_out = {}
    for key, flat in [("grad", small_sum), ("delta", sd), ("m", sm), ("v", sv_)]:
        per_layer = _unpack_small(flat.reshape(-1), shapes)
        for i, n in enumerate(names):
            small_out[(key, n)] = jnp.stack([per_layer[l][i].reshape(small_w[n][0].shape[1:]) for l in range(L)])

    layer_len = 2 * D + 2 * HEAD_DIM + H2 * N_REL + 6 * D
    rows = gathered_small.reshape(NDEV, n_pad)
    dmod_all = jnp.stack([rows[:, l * layer_len + layer_len - 6 * D:(l + 1) * layer_len] for l in range(L)])
    dmod_cols = lax.dynamic_slice(dmod_all, (0, 0, me * Ca), (L, NDEV, Ca))
    dmod_cols = jnp.pad(dmod_cols, ((0, 0), (0, 128 - NDEV), (0, 0)))
    c_t = jnp.pad(c_all.T, ((0, 0), (0, 128 - NDEV)))
    g_ada = _w_ada_grad(c_t, dmod_cols, "w_ada_grad")
    flat2 = lambda a: a.reshape(L * D, Ca)
    ad, am, av = _adamw(flat2(w_ada), flat2(g_ada), flat2(m_w_ada), flat2(v_w_ada), "adamw_w_ada")
    ada_out = [g_ada] + [a.reshape(L, D, Ca) for a in (ad, am, av)]

    def leaf(kind):
        k = {"grad": 0, "delta": 1, "m": 2, "v": 3}[kind]
        return [small_out[(kind, "g_norm1")], big_out[0][k], small_out[(kind, "g_q")], small_out[(kind, "g_k")],
                small_out[(kind, "rel_bias")], big_out[1][k], small_out[(kind, "g_norm2")], big_out[2][k],
                big_out[3][k], ada_out[k], small_out[(kind, "b_ada")]]

    loss = lax.psum(loss_part[0, 0], ("x", "y", "c"))
    return (loss, grad_x, *leaf("grad"), *leaf("delta"), *leaf("m"), *leaf("v"))
```

```python
import functools

import jax
import jax.numpy as jnp
from jax import lax
from jax.experimental import pallas as pl
from jax.experimental.pallas import tpu as pltpu

F32 = jnp.float32
_MXU = jnp.bfloat16

HEAD_DIM = 64
CHUNK = 64
LEFT_CHUNKS = 8
PAD = LEFT_CHUNKS * CHUNK
BAND = PAD + CHUNK
REL_CLIP = 128
N_REL = 2 * REL_CLIP + 1
EPS = 1e-6
NEG = -1e30
NDEV = 8
SB_T = 128
CA_T = 2 * CHUNK
CA_W = CA_T + PAD
SB_SKIP = -104.0
PAIR = 2 * HEAD_DIM
SB_PAIRS = 4
CA_PAIRS_FWD = 4
CA_PAIRS_BWD = 2
ROW_BLOCK = 512
SKEW_W = CA_W + CA_T - 1

ADAM_LR, ADAM_B1, ADAM_B2, ADAM_EPS, ADAM_WD, ADAM_STEP = 0.001, 0.9, 0.999, 1e-08, 0.01, 10

MESH = pl.DeviceIdType.MESH
VMEM_SPEC = pl.BlockSpec(memory_space=pltpu.VMEM)
ANY_SPEC = pl.BlockSpec(memory_space=pl.ANY)


def _nn(a, b):
    return lax.dot_general(a, b, (((1,), (0,)), ((), ())), preferred_element_type=F32)


def _nt(a, b):
    return lax.dot_general(a, b, (((1,), (1,)), ((), ())), preferred_element_type=F32)


def _tn(a, b):
    return lax.dot_general(a, b, (((0,), (0,)), ((), ())), preferred_element_type=F32)


def _blk(n, pref):
    return pref if n % pref == 0 else n


def _pos():
    return lax.axis_index("x"), lax.axis_index("y"), lax.axis_index("c")


def _flip(v, bit):
    return 1 - v if bit else v


def _gather_small(x_ref, out_ref, send_sems, recv_sems):
    R, C = x_ref.shape
    x, y, c = _pos()
    me = 4 * x + 2 * y + c

    def peer(k):
        return (_flip(x, k & 4), _flip(y, k & 2), _flip(c, k & 1))

    def copy(k, slot):
        return pltpu.make_async_remote_copy(
            src_ref=x_ref, dst_ref=out_ref.at[slot], send_sem=send_sems.at[k - 1],
            recv_sem=recv_sems.at[k - 1], device_id=peer(k), device_id_type=MESH)

    out_ref[pl.ds(me, 1), :, :] = x_ref[...].reshape(1, R, C)
    sends = [copy(k, me) for k in range(1, NDEV)]
    for cp in sends:
        cp.start()
    for k in range(1, NDEV):
        px, py, pc = peer(k)
        copy(k, 4 * px + 2 * py + pc).wait_recv()
    for cp in sends:
        cp.wait_send()


def _all_gather_small(blk, name):
    return pl.pallas_call(
        lambda x_ref, out_ref, send_sems, recv_sems: _gather_small(x_ref, out_ref, send_sems, recv_sems), name=name,
        out_shape=jax.ShapeDtypeStruct((NDEV,) + blk.shape, blk.dtype),
        in_specs=[VMEM_SPEC], out_specs=VMEM_SPEC,
        scratch_shapes=[pltpu.SemaphoreType.DMA((NDEV - 1,)), pltpu.SemaphoreType.DMA((NDEV - 1,))],
    )(blk)


class _Exchange:
    def __init__(self, inputs, out_shapes, sems, start, finish, middle=None):
        self.inputs, self.out_shapes, self.sems = list(inputs), list(out_shapes), list(sems)
        self.start, self.middle, self.finish = start, middle, finish


def _run_exchange(ex, name):
    n_in, n_out = len(ex.inputs), len(ex.out_shapes)

    def body(*refs):
        ins, outs, sems = refs[:n_in], refs[n_in:n_in + n_out], refs[n_in + n_out:]
        ex.start(ins, outs, sems)
        if ex.middle is not None:
            ex.middle(ins, outs, sems)
        ex.finish(ins, outs, sems)

    return pl.pallas_call(
        body, name=name, out_shape=ex.out_shapes, in_specs=[ANY_SPEC] * n_in, out_specs=[ANY_SPEC] * n_out,
        scratch_shapes=ex.sems,
    )(*ex.inputs)


def _hosted(body, n_in, n_out, ex, step, steps):
    if ex is None:
        return body
    xi, xo = len(ex.inputs), len(ex.out_shapes)

    def wrapped(*refs):
        own_in, ex_in = refs[:n_in], refs[n_in:n_in + xi]
        rest = refs[n_in + xi:]
        own_out, ex_out = rest[:n_out], rest[n_out:n_out + xo]
        rest = rest[n_out + xo:]
        own_scratch, ex_sems = rest[:len(rest) - len(ex.sems)], rest[len(rest) - len(ex.sems):]
        t = step()
        pl.when(t == 0)(lambda: ex.start(ex_in, ex_out, ex_sems))
        body(*own_in, *own_out, *own_scratch)
        if ex.middle is not None:
            pl.when(t == (steps * 7) // 8)(lambda: ex.middle(ex_in, ex_out, ex_sems))
        pl.when(t == steps - 1)(lambda: ex.finish(ex_in, ex_out, ex_sems))

    return wrapped


def _call_hosted(body, name, grid, in_specs, out_specs, out_shape, scratch, args, ex):
    n_in, n_out = len(in_specs), len(out_specs)
    steps = 1
    for extent in grid:
        steps *= extent

    def step():
        t = pl.program_id(0)
        for axis in range(1, len(grid)):
            t = t * grid[axis] + pl.program_id(axis)
        return t

    if ex is not None:
        in_specs = in_specs + [ANY_SPEC] * len(ex.inputs)
        out_specs = out_specs + [ANY_SPEC] * len(ex.out_shapes)
        out_shape = out_shape + ex.out_shapes
        scratch = scratch + ex.sems
        args = args + ex.inputs
    outs = pl.pallas_call(
        _hosted(body, n_in, n_out, ex, step, steps), name=name, grid=grid, in_specs=in_specs, out_specs=out_specs,
        out_shape=out_shape, scratch_shapes=scratch,
    )(*args)
    return list(outs[:n_out]), list(outs[n_out:])


def _gather_exchange(shards):
    n = len(shards)

    def setup(ins, outs, sems):
        send_sems, recv_sems, local_sems = sems
        x, y, c = _pos()
        me, sibling = (x, y, c), (x, y, 1 - c)
        chips = [(1 - x, y), (x, 1 - y), (1 - x, 1 - y)]

        def copy(i, k, block, to, src=None):
            px, py, pc = block
            dst = outs[i].at[4 * px + 2 * py + pc]
            return pltpu.make_async_remote_copy(
                src_ref=dst if src is None else src, dst_ref=dst, send_sem=send_sems.at[7 * i + k],
                recv_sem=recv_sems.at[7 * i + k], device_id=to, device_id_type=MESH)

        def mine(i):
            return pltpu.make_async_copy(ins[i], outs[i].at[4 * x + 2 * y + c], local_sems.at[i])

        def first(i):
            return [copy(i, 0, me, sibling, src=ins[i])] + [
                copy(i, 1 + j, me, (*chip, c), src=ins[i]) for j, chip in enumerate(chips)]

        def passed(i, j):
            return copy(i, 4 + j, (*chips[j], c), sibling)

        return me, sibling, chips, c, copy, mine, first, passed

    def start(ins, outs, sems):
        _, _, _, _, _, mine, first, _ = setup(ins, outs, sems)
        for i in range(n):
            mine(i).start()
            for cp in first(i):
                cp.start()

    def middle(ins, outs, sems):
        me, _, chips, c, copy, _, _, passed = setup(ins, outs, sems)
        for j, chip in enumerate(chips):
            for i in range(n):
                copy(i, 1 + j, (*chip, c), me).wait_recv()
                passed(i, j).start()

    def finish(ins, outs, sems):
        me, sibling, chips, c, copy, mine, first, passed = setup(ins, outs, sems)
        for i in range(n):
            copy(i, 0, sibling, me).wait_recv()
            for j, chip in enumerate(chips):
                copy(i, 4 + j, (*chip, 1 - c), me).wait_recv()
        for i in range(n):
            for cp in first(i) + [passed(i, j) for j in range(3)]:
                cp.wait_send()
            mine(i).wait()

    return _Exchange(
        shards, [jax.ShapeDtypeStruct((NDEV,) + s.shape, s.dtype) for s in shards],
        [pltpu.SemaphoreType.DMA((7 * n,)), pltpu.SemaphoreType.DMA((7 * n,)), pltpu.SemaphoreType.DMA((n,))],
        start, finish, middle)


def _sibling_exchange(grads):
    n = len(grads)

    def copies(ins, outs, sems):
        send_sems, recv_sems = sems
        x, y, c = _pos()
        return [pltpu.make_async_remote_copy(
            src_ref=ins[i].at[2 * q + (1 - c)], dst_ref=outs[i].at[q], send_sem=send_sems.at[4 * i + q],
            recv_sem=recv_sems.at[4 * i + q], device_id=(x, y, 1 - c), device_id_type=MESH)
            for i in range(n) for q in range(4)]

    def start(ins, outs, sems):
        for cp in copies(ins, outs, sems):
            cp.start()

    def finish(ins, outs, sems):
        for cp in copies(ins, outs, sems):
            cp.wait()

    return _Exchange(
        grads, [jax.ShapeDtypeStruct((4,) + g.shape[1:], g.dtype) for g in grads],
        [pltpu.SemaphoreType.DMA((4 * n,)), pltpu.SemaphoreType.DMA((4 * n,))], start, finish)


def _chip_exchange(parts):
    n = len(parts)

    def copies(ins, outs, sems):
        send_sems, recv_sems = sems
        x, y, c = _pos()
        return [pltpu.make_async_remote_copy(
            src_ref=ins[i].at[j - 1], dst_ref=outs[i].at[j - 1], send_sem=send_sems.at[3 * i + j - 1],
            recv_sem=recv_sems.at[3 * i + j - 1], device_id=(_flip(x, j & 2), _flip(y, j & 1), c),
            device_id_type=MESH) for i in range(n) for j in range(1, 4)]

    def start(ins, outs, sems):
        for cp in copies(ins, outs, sems):
            cp.start()

    def finish(ins, outs, sems):
        for cp in copies(ins, outs, sems):
            cp.wait()

    return _Exchange(
        parts, [jax.ShapeDtypeStruct(p.shape, p.dtype) for p in parts],
        [pltpu.SemaphoreType.DMA((3 * n,)), pltpu.SemaphoreType.DMA((3 * n,))], start, finish)


def _rs_chip_partial(place, grad, recv, name):
    _, R, C = grad.shape
    tr = _blk(R, 256)

    def body(place_ref, *refs):
        g_refs, r_refs = refs[:4], refs[4:8]
        own_ref, out_ref = refs[8:]
        own_ref[...] = g_refs[0][0] + r_refs[0][0]
        for j in range(1, 4):
            out_ref[j - 1] = (g_refs[j][0] + r_refs[j][0]).astype(out_ref.dtype)

    def g_map(j):
        return lambda i, p: (2 * jnp.bitwise_xor(p[0], j) + p[1], i, 0)

    def r_map(j):
        return lambda i, p: (jnp.bitwise_xor(p[0], j), i, 0)

    grid_spec = pltpu.PrefetchScalarGridSpec(
        num_scalar_prefetch=1, grid=(R // tr,),
        in_specs=[pl.BlockSpec((1, tr, C), g_map(j)) for j in range(4)]
        + [pl.BlockSpec((1, tr, C), r_map(j)) for j in range(4)],
        out_specs=[pl.BlockSpec((tr, C), lambda i, p: (i, 0)), pl.BlockSpec((3, tr, C), lambda i, p: (0, i, 0))])
    return pl.pallas_call(
        body, name=name, grid_spec=grid_spec,
        out_shape=[jax.ShapeDtypeStruct((R, C), F32), jax.ShapeDtypeStruct((3, R, C), _MXU)],
    )(place, *([grad] * 4), *([recv] * 4))


def _adamw_math(w, g, m, v):
    m = ADAM_B1 * m + (1.0 - ADAM_B1) * g
    v = ADAM_B2 * v + (1.0 - ADAM_B2) * (g * g)
    m_hat = m / (1.0 - ADAM_B1 ** ADAM_STEP)
    v_hat = v / (1.0 - ADAM_B2 ** ADAM_STEP)
    delta = -ADAM_LR * (m_hat / (jnp.sqrt(v_hat) + ADAM_EPS) + ADAM_WD * w)
    return delta, m, v


def _adamw(w, g, m, v, name):
    R, C = w.shape
    tr = _blk(R, 256)

    def body(w_ref, g_ref, m_ref, v_ref, d_ref, nm_ref, nv_ref):
        d, nm, nv = _adamw_math(w_ref[...], g_ref[...], m_ref[...], v_ref[...])
        d_ref[...] = d
        nm_ref[...] = nm
        nv_ref[...] = nv

    spec = pl.BlockSpec((tr, C), lambda i: (i, 0))
    return pl.pallas_call(
        body, name=name, grid=(R // tr,), in_specs=[spec] * 4, out_specs=[spec] * 3,
        out_shape=[jax.ShapeDtypeStruct((R, C), F32)] * 3,
    )(w, g, m, v)


def _rs_sum_adamw(owns, recvs, w, m, v, name):
    L, R, C = w.shape
    tr = _blk(R, 256)
    nr = R // tr

    def body(o0, o1, r0, r1, w_ref, m_ref, v_ref, g_ref, d_ref, nm_ref, nv_ref):
        def step(o_ref, r_ref):
            g = o_ref[...]
            for j in range(3):
                g = g + r_ref[j].astype(F32)
            d, nm, nv = _adamw_math(w_ref[0], g, m_ref[0], v_ref[0])
            g_ref[0] = g
            d_ref[0] = d
            nm_ref[0] = nm
            nv_ref[0] = nv

        pl.when(pl.program_id(0) == 0)(lambda: step(o0, r0))
        pl.when(pl.program_id(0) == 1)(lambda: step(o1, r1))

    def hold(layer):
        if layer == 0:
            return lambda l, i: i * (1 - l) + (nr - 1) * l
        return lambda l, i: i * l

    own_spec = [pl.BlockSpec((tr, C), functools.partial(lambda l, i, f: (f(l, i), 0), f=hold(k))) for k in range(2)]
    recv_spec = [pl.BlockSpec((3, tr, C), functools.partial(lambda l, i, f: (0, f(l, i), 0), f=hold(k)))
                 for k in range(2)]
    lay = pl.BlockSpec((1, tr, C), lambda l, i: (l, i, 0))
    return pl.pallas_call(
        body, name=name, grid=(L, nr),
        in_specs=own_spec + recv_spec + [lay] * 3, out_specs=[lay] * 4,
        out_shape=[jax.ShapeDtypeStruct((L, R, C), F32)] * 4,
    )(owns[0], owns[1], recvs[0], recvs[1], w, m, v)


def _silu(x):
    return x / (1.0 + jnp.exp(-x))


def _start(c, w_ada, b_cols, w_first, name):
    L, D, Ca = w_ada.shape
    ex = _gather_exchange([w_first])

    def body(c_ref, w_ref, b_ref, first_ref, call_ref, mod_ref, gathered_ref, part_ref, c_send, c_recv, m_send,
             m_recv, *ex_sems):
        _gather_small(c_ref, call_ref, c_send, c_recv)
        ex.start([first_ref], [gathered_ref], ex_sems)
        act = _silu(call_ref[...].reshape(NDEV, D)).astype(_MXU)
        for l in range(L):
            part_ref[:, l * Ca:(l + 1) * Ca] = _nn(act, w_ref[l].astype(_MXU)) + b_ref[l:l + 1, :]
        _gather_small(part_ref, mod_ref, m_send, m_recv)
        ex.middle([first_ref], [gathered_ref], ex_sems)
        ex.finish([first_ref], [gathered_ref], ex_sems)

    pairs = [pltpu.SemaphoreType.DMA((NDEV - 1,))] * 4
    return pl.pallas_call(
        body, name=name,
        out_shape=[jax.ShapeDtypeStruct((NDEV, 1, D), F32), jax.ShapeDtypeStruct((NDEV, NDEV, L * Ca), F32)]
        + ex.out_shapes,
        in_specs=[VMEM_SPEC] * 3 + [ANY_SPEC], out_specs=[VMEM_SPEC, VMEM_SPEC, ANY_SPEC],
        scratch_shapes=[pltpu.VMEM((NDEV, L * Ca), F32)] + pairs + ex.sems,
    )(c, w_ada, b_cols, w_first)


def _w_ada_grad(c_t, dmod_cols, name):
    L, _, Ca = dmod_cols.shape
    D = c_t.shape[0]

    def body(c_ref, d_ref, o_ref):
        act = _silu(c_ref[...]).astype(_MXU)
        for l in range(L):
            o_ref[l] = _nn(act, d_ref[l].astype(_MXU))

    return pl.pallas_call(
        body, name=name, out_shape=jax.ShapeDtypeStruct((L, D, Ca), F32),
        in_specs=[VMEM_SPEC] * 2, out_specs=VMEM_SPEC,
    )(c_t, dmod_cols)


def _sum_devices(gathered, name):
    _, _, N = gathered.shape

    def body(x_ref, o_ref):
        acc = x_ref[0]
        for d in range(1, NDEV):
            acc = acc + x_ref[d]
        o_ref[...] = acc

    return pl.pallas_call(
        body, name=name, out_shape=jax.ShapeDtypeStruct((1, N), F32),
        in_specs=[VMEM_SPEC], out_specs=VMEM_SPEC,
    )(gathered)


def _ln_mod_matmul(x, g, sc, sh, w, name):
    S, D = x.shape
    N = w.shape[1]
    tm = _blk(S, ROW_BLOCK)

    def body(x_ref, g_ref, sc_ref, sh_ref, w_ref, o_ref, h_ref):
        xv = x_ref[...]
        r = lax.rsqrt(jnp.mean(xv * xv, axis=-1, keepdims=True) + EPS)
        hv = ((xv * r) * g_ref[...]) * (1.0 + sc_ref[...]) + sh_ref[...]
        hb = hv.astype(_MXU)
        h_ref[...] = hb
        o_ref[...] = _nn(hb, w_ref[...]).astype(o_ref.dtype)

    vec = pl.BlockSpec((1, D), lambda i: (0, 0))
    row = lambda width: pl.BlockSpec((tm, width), lambda i: (i, 0))
    return pl.pallas_call(
        body, name=name, grid=(S // tm,),
        in_specs=[row(D), vec, vec, vec, pl.BlockSpec((D, N), lambda i: (0, 0))],
        out_specs=[row(N), row(D)],
        out_shape=[jax.ShapeDtypeStruct((S, N), _MXU), jax.ShapeDtypeStruct((S, D), _MXU)],
    )(x, g, sc, sh, w)


def _attn_out_mlp_in(mixed, w_o, xres, gt, g, sc, sh, w1, name):
    S, D = xres.shape
    N = w1.shape[1]
    tm = _blk(S, ROW_BLOCK)

    def body(a_ref, wo_ref, x_ref, gt_ref, g_ref, sc_ref, sh_ref, w1_ref, x1_ref, f_ref, u_ref, h_ref):
        f = _nn(a_ref[...], wo_ref[...])
        f_ref[...] = f.astype(f_ref.dtype)
        xv = x_ref[...] + gt_ref[...] * f
        x1_ref[...] = xv
        r = lax.rsqrt(jnp.mean(xv * xv, axis=-1, keepdims=True) + EPS)
        hb = (((xv * r) * g_ref[...]) * (1.0 + sc_ref[...]) + sh_ref[...]).astype(_MXU)
        h_ref[...] = hb
        u_ref[...] = _nn(hb, w1_ref[...]).astype(u_ref.dtype)

    vec = pl.BlockSpec((1, D), lambda i: (0, 0))
    row = lambda width: pl.BlockSpec((tm, width), lambda i: (i, 0))
    whole = lambda arr: pl.BlockSpec(arr.shape, lambda i: (0, 0))
    return pl.pallas_call(
        body, name=name, grid=(S // tm,),
        in_specs=[row(D), whole(w_o), row(D), vec, vec, vec, vec, whole(w1)],
        out_specs=[row(D), row(D), row(N), row(D)],
        out_shape=[jax.ShapeDtypeStruct((S, D), F32), jax.ShapeDtypeStruct((S, D), _MXU),
                   jax.ShapeDtypeStruct((S, N), _MXU), jax.ShapeDtypeStruct((S, D), _MXU)],
    )(mixed, w_o, xres, gt, g, sc, sh, w1)


def _matmul_res_gate(a, w, xres, gt, relu2, name, target=None):
    S, K = a.shape
    N = w.shape[1]
    tm = _blk(S, ROW_BLOCK)
    last = S // tm - 1
    with_loss = target is not None

    def body(a_ref, w_ref, x_ref, gt_ref, *rest):
        av = a_ref[...]
        if relu2:
            af = jnp.maximum(av.astype(F32), 0.0)
            av = (af * af).astype(_MXU)
        f = _nn(av, w_ref[...])
        out = x_ref[...] + gt_ref[...] * f
        if not with_loss:
            o_ref, f_ref = rest
            o_ref[...] = out
        else:
            t_ref, o_ref, f_ref, l_ref, acc_ref = rest
            i = pl.program_id(0)
            e = out - t_ref[...]
            o_ref[...] = e * (1.0 / N)
            _accumulate(acc_ref, jnp.sum(e * e, axis=0, keepdims=True), i == 0)

            @pl.when(i == last)
            def _():
                l_ref[...] = (0.5 / N) * jnp.sum(acc_ref[...], axis=1, keepdims=True)
        f_ref[...] = f.astype(f_ref.dtype)

    row = lambda width: pl.BlockSpec((tm, width), lambda i: (i, 0))
    in_specs = [row(K), pl.BlockSpec((K, N), lambda i: (0, 0)), row(N), pl.BlockSpec((1, N), lambda i: (0, 0))]
    out_specs = [row(N), row(N)]
    out_shape = [jax.ShapeDtypeStruct((S, N), F32), jax.ShapeDtypeStruct((S, N), _MXU)]
    args = [a, w, xres, gt]
    if with_loss:
        in_specs.append(row(N))
        args.append(target)
        out_specs.append(pl.BlockSpec((1, 1), lambda i: (0, 0)))
        out_shape.append(jax.ShapeDtypeStruct((1, 1), F32))
    return pl.pallas_call(
        body, name=name, grid=(S // tm,), in_specs=in_specs, out_specs=out_specs, out_shape=out_shape,
        scratch_shapes=[pltpu.VMEM((1, N), F32)] if with_loss else [],
    )(*args)


def _accumulate(ref, part, first):
    @pl.when(first)
    def _():
        ref[...] = part

    @pl.when(jnp.logical_not(first))
    def _():
        ref[...] += part


def _gate_nt_matmul(dx, f, gt, w, u, name):
    S, D = dx.shape
    N = w.shape[0]
    tm = _blk(S, ROW_BLOCK)
    with_u = u is not None

    def body(*refs):
        if with_u:
            dx_ref, f_ref, gt_ref, w_ref, u_ref, dz_ref, dgt_ref, res_ref = refs
        else:
            dx_ref, f_ref, gt_ref, w_ref, dz_ref, dgt_ref, res_ref = refs
        dxv = dx_ref[...]
        dz = (dxv * gt_ref[...]).astype(_MXU)
        dz_ref[...] = dz
        _accumulate(dgt_ref, jnp.sum(dxv * f_ref[...].astype(F32), axis=0, keepdims=True), pl.program_id(0) == 0)
        r = _nt(dz, w_ref[...])
        if with_u:
            r = r * (2.0 * jnp.maximum(u_ref[...].astype(F32), 0.0))
        res_ref[...] = r.astype(res_ref.dtype)

    row = lambda width: pl.BlockSpec((tm, width), lambda i: (i, 0))
    in_specs = [row(D), row(D), pl.BlockSpec((1, D), lambda i: (0, 0)), pl.BlockSpec((N, D), lambda i: (0, 0))]
    args = [dx, f, gt, w]
    if with_u:
        in_specs.append(row(N))
        args.append(u)
    return pl.pallas_call(
        body, name=name, grid=(S // tm,), in_specs=in_specs,
        out_specs=[row(D), pl.BlockSpec((1, D), lambda i: (0, 0)), row(N)],
        out_shape=[jax.ShapeDtypeStruct((S, D), _MXU), jax.ShapeDtypeStruct((1, D), F32),
                   jax.ShapeDtypeStruct((S, N), _MXU)],
    )(*args)


def _mlp_in_attn_out_bwd(du, w1, x1, g, sc, sh, dxres, f, gt, w_o, name):
    S, D = x1.shape
    N = w1.shape[1]
    tm = _blk(S, ROW_BLOCK)

    def body(du_ref, w1_ref, x_ref, g_ref, sc_ref, sh_ref, dxr_ref, f_ref, gt_ref, wo_ref,
             dx_ref, dsh_ref, dsc_ref, dg_ref, dz_ref, dgt_ref, dm_ref):
        dh = _nt(du_ref[...], w1_ref[...])
        xv = x_ref[...]
        r = lax.rsqrt(jnp.mean(xv * xv, axis=-1, keepdims=True) + EPS)
        xhat = xv * r
        gv = g_ref[...]
        dn = dh * (1.0 + sc_ref[...])
        dxhat = dn * gv
        dxv = dxr_ref[...] + r * (dxhat - xhat * jnp.mean(dxhat * xhat, axis=-1, keepdims=True))
        dx_ref[...] = dxv
        first = pl.program_id(0) == 0
        _accumulate(dsh_ref, jnp.sum(dh, axis=0, keepdims=True), first)
        _accumulate(dsc_ref, jnp.sum(dh * (xhat * gv), axis=0, keepdims=True), first)
        _accumulate(dg_ref, jnp.sum(dn * xhat, axis=0, keepdims=True), first)
        dz = (dxv * gt_ref[...]).astype(_MXU)
        dz_ref[...] = dz
        _accumulate(dgt_ref, jnp.sum(dxv * f_ref[...].astype(F32), axis=0, keepdims=True), first)
        dm_ref[...] = _nt(dz, wo_ref[...]).astype(dm_ref.dtype)

    row = lambda width: pl.BlockSpec((tm, width), lambda i: (i, 0))
    vec = pl.BlockSpec((1, D), lambda i: (0, 0))
    whole = lambda arr: pl.BlockSpec(arr.shape, lambda i: (0, 0))
    return pl.pallas_call(
        body, name=name, grid=(S // tm,),
        in_specs=[row(N), whole(w1), row(D), vec, vec, vec, row(D), row(D), vec, whole(w_o)],
        out_specs=[row(D), vec, vec, vec, row(D), vec, row(D)],
        out_shape=[jax.ShapeDtypeStruct((S, D), F32)] + [jax.ShapeDtypeStruct((1, D), F32)] * 3
        + [jax.ShapeDtypeStruct((S, D), _MXU), jax.ShapeDtypeStruct((1, D), F32), jax.ShapeDtypeStruct((S, D), _MXU)],
    )(du, w1, x1, g, sc, sh, dxres, f, gt, w_o)


def _tn_matmul(a, b, by_col, relu2, name):
    S, Ka = a.shape
    Nb = b.shape[1]
    ts = _blk(S, 2 * ROW_BLOCK)
    half = NDEV // 2
    if by_col:
        R, C = Ka, Nb // NDEV
        a_spec = pl.BlockSpec((ts, Ka), lambda h, k: (k, 0))
        b_spec = pl.BlockSpec((ts, half * C), lambda h, k: (k, h))
    else:
        R, C = Ka // NDEV, Nb
        a_spec = pl.BlockSpec((ts, half * R), lambda h, k: (k, h))
        b_spec = pl.BlockSpec((ts, Nb), lambda h, k: (k, 0))

    def body(a_ref, b_ref, o_ref):
        av = a_ref[...]
        if relu2:
            af = jnp.maximum(av.astype(F32), 0.0)
            av = (af * af).astype(_MXU)
        p = _tn(av, b_ref[...])
        first = pl.program_id(1) == 0
        for d in range(half):
            part = p[:, d * C:(d + 1) * C] if by_col else p[d * R:(d + 1) * R, :]
            _accumulate(o_ref.at[d], part, first)

    return pl.pallas_call(
        body, name=name, grid=(NDEV // half, S // ts), in_specs=[a_spec, b_spec],
        out_specs=pl.BlockSpec((half, R, C), lambda h, k: (h, 0, 0)),
        out_shape=jax.ShapeDtypeStruct((NDEV, R, C), F32),
    )(a, b)


def _nt_ln_bwd(dy, w, x, g, sc, sh, dxres, name):
    S, D = x.shape
    N = w.shape[1]
    tm = _blk(S, ROW_BLOCK)

    def body(dy_ref, w_ref, x_ref, g_ref, sc_ref, sh_ref, dxr_ref, dx_ref, dsh_ref, dsc_ref, dg_ref):
        dh = _nt(dy_ref[...], w_ref[...])
        xv = x_ref[...]
        r = lax.rsqrt(jnp.mean(xv * xv, axis=-1, keepdims=True) + EPS)
        xhat = xv * r
        gv = g_ref[...]
        dn = dh * (1.0 + sc_ref[...])
        dxhat = dn * gv
        dxv = r * (dxhat - xhat * jnp.mean(dxhat * xhat, axis=-1, keepdims=True))
        dx_ref[...] = dxr_ref[...] + dxv
        first = pl.program_id(0) == 0
        _accumulate(dsh_ref, jnp.sum(dh, axis=0, keepdims=True), first)
        _accumulate(dsc_ref, jnp.sum(dh * (xhat * gv), axis=0, keepdims=True), first)
        _accumulate(dg_ref, jnp.sum(dn * xhat, axis=0, keepdims=True), first)

    row = lambda width: pl.BlockSpec((tm, width), lambda i: (i, 0))
    vec = pl.BlockSpec((1, D), lambda i: (0, 0))
    return pl.pallas_call(
        body, name=name, grid=(S // tm,),
        in_specs=[row(N), pl.BlockSpec((D, N), lambda i: (0, 0)), row(D), vec, vec, vec, row(D)],
        out_specs=[row(D), vec, vec, vec],
        out_shape=[jax.ShapeDtypeStruct((S, D), F32)] + [jax.ShapeDtypeStruct((1, D), F32)] * 3,
    )(dy, w, x, g, sc, sh, dxres)


def _split2(v):
    hi = v.astype(_MXU)
    mid = (v - hi.astype(F32)).astype(_MXU)
    return hi, mid


def _tri_sums(vs, tri2):
    T = vs[0].shape[0]
    out = []
    for j in range(len(vs) // 2):
        wide = [jnp.concatenate(_split2(vs[2 * j + e]), axis=1) for e in range(2)]
        for both in _per_head(_nn(jnp.concatenate(wide, axis=0), tri2), T):
            out.append((both[:, :T], both[:, T:]))
    return out


def _tri2(T, inclusive):
    j = lax.broadcasted_iota(jnp.int32, (2 * T, 2 * T), 0) % T
    s = lax.broadcasted_iota(jnp.int32, (2 * T, 2 * T), 1)
    keep = (j >= s) if inclusive else (j > s)
    return jnp.where((s >= T) | keep, 1.0, 0.0).astype(_MXU)


def _log_sigmoid(z):
    return jnp.minimum(z, 0.0) - jnp.log(1.0 + jnp.exp(-jnp.abs(z)))


def _per_head(tall, T):
    return [tall[h * T:(h + 1) * T] for h in range(tall.shape[0] // T)]


def _sb_blocks(q_tall, k2, strict, tri2, carry):
    T = k2[0].shape[0]
    zs = []
    for qt, kblk in zip(q_tall, k2):
        zs += _per_head(_nt(qt, kblk), T)
    lbs, l1s = [], []
    for z in zs:
        lb = _log_sigmoid(z)
        l1 = lb - z
        if strict is not None:
            l1 = jnp.where(strict, l1, 0.0)
        lbs.append(lb)
        l1s.append(l1)
    sums = _tri_sums(l1s, tri2)
    amps, new_carry = [], []
    for lb, (sfx, tot), c in zip(lbs, sums, carry):
        a = jnp.exp(lb + sfx + c)
        if strict is not None:
            a = jnp.where(strict, a, 0.0)
        amps.append(a)
        new_carry.append(c + tot)
    return lbs, amps, new_carry


def _sb_alive(carry):
    top = carry[0]
    for c in carry[1:]:
        top = jnp.maximum(top, c)
    return jnp.max(top) > SB_SKIP


def _skew_index():
    i = lax.broadcasted_iota(jnp.int32, (CA_T, SKEW_W + 1), 0)
    m = lax.broadcasted_iota(jnp.int32, (CA_T, SKEW_W + 1), 1)
    wrapped = i + m >= SKEW_W
    row = jnp.where(wrapped, i + 1, i)
    j = jnp.where(wrapped, i + m - SKEW_W, i + m)
    a = row // CHUNK
    jj = j - a * CHUNK
    inband = (jj >= 0) & (jj < BAND) & (j < CA_W) & (row < CA_T)
    idx = jnp.clip((row - a * CHUNK) + PAD - jj, -REL_CLIP, REL_CLIP) + REL_CLIP
    return inband, idx, wrapped


def _skew(tile):
    H = tile.shape[0]
    flat = jnp.pad(tile, ((0, 0), (0, 0), (0, SKEW_W - CA_W))).reshape(H, CA_T * SKEW_W)
    return jnp.pad(flat, ((0, 0), (0, CA_T))).reshape(H, CA_T, SKEW_W + 1)


def _unskew(view):
    H = view.shape[0]
    flat = view.reshape(H, CA_T * (SKEW_W + 1))[:, :CA_T * SKEW_W]
    return flat.reshape(H, CA_T, SKEW_W)[:, :, :CA_W]


def _ca_bias(rel_bias, name):
    H = rel_bias.shape[0]
    top = rel_bias[:, N_REL - 1:]
    by_offset = jnp.concatenate(
        [jnp.broadcast_to(top, (H, PAD - REL_CLIP + 1)), jnp.flip(rel_bias[:, :N_REL - 1], axis=1),
         jnp.broadcast_to(top, (H, SKEW_W + 1 - (PAD - REL_CLIP + 1) - (N_REL - 1)))], axis=1)

    def body(t_ref, o_ref):
        inband, _, wrapped = _skew_index()
        vals = jnp.where(wrapped, t_ref[0][:, 0:1], t_ref[0])
        o_ref[0] = jnp.where(inband, vals, NEG)

    view = pl.pallas_call(
        body, name=name, grid=(H,), in_specs=[pl.BlockSpec((1, 1, SKEW_W + 1), lambda h: (h, 0, 0))],
        out_specs=pl.BlockSpec((1, CA_T, SKEW_W + 1), lambda h: (h, 0, 0)),
        out_shape=jax.ShapeDtypeStruct((H, CA_T, SKEW_W + 1), F32),
    )(by_offset.reshape(H, 1, SKEW_W + 1))
    return _unskew(view)


def _ca_bias_bwd(dbias, name):
    H = dbias.shape[0]

    def body(d_ref, o_ref):
        inband, idx, _ = _skew_index()
        d = jnp.where(inband, d_ref[0], 0.0)
        clipped = idx == N_REL - 1
        by_offset = jnp.sum(jnp.where(clipped, 0.0, d), axis=0, keepdims=True)
        top = jnp.sum(jnp.sum(jnp.where(clipped, d, 0.0), axis=0, keepdims=True), axis=1, keepdims=True)
        lane = lax.broadcasted_iota(jnp.int32, (1, SKEW_W + 1), 1)
        o_ref[0] = jnp.where(lane == 0, top, by_offset)

    out = pl.pallas_call(
        body, name=name, grid=(H,), in_specs=[pl.BlockSpec((1, CA_T, SKEW_W + 1), lambda h: (h, 0, 0))],
        out_specs=pl.BlockSpec((1, 1, SKEW_W + 1), lambda h: (h, 0, 0)),
        out_shape=jax.ShapeDtypeStruct((H, 1, SKEW_W + 1), F32),
    )(_skew(dbias))[:, 0]
    first = PAD - REL_CLIP + 1
    return jnp.concatenate([jnp.flip(out[:, first:first + N_REL - 1], axis=1), out[:, 0:1]], axis=1)


def _low_lanes(rows):
    return lax.broadcasted_iota(jnp.int32, (rows, PAIR), 1) < HEAD_DIM


def _one_head(t2, low, first, scale=1.0):
    tf = t2.astype(F32) * scale
    return (jnp.where(low, tf, 0.0) if first else jnp.where(low, 0.0, tf)).astype(_MXU)


def _two_heads(t2, low, scale=1.0):
    return jnp.concatenate([_one_head(t2, low, True, scale), _one_head(t2, low, False, scale)], axis=0)


def _sb_fwd(proj, name, ex=None):
    S, W = proj.shape
    half = W // 6
    npair = half // PAIR
    T = _blk(S, SB_T)
    GP = _blk(npair, SB_PAIRS)
    GW = GP * PAIR
    nb = npair // GP

    def body(q_ref, k_ref, v_ref, o_ref, ox_ref):
        qi = pl.program_id(1)
        low = _low_lanes(T)
        q_tall = []
        for j in range(GP):
            q2 = q_ref[:, j * PAIR:(j + 1) * PAIR]
            q_tall.append(_two_heads(q2, low, HEAD_DIM ** -0.5))
        row = lax.broadcasted_iota(jnp.int32, (T, T), 0)
        col = lax.broadcasted_iota(jnp.int32, (T, T), 1)
        tri2 = _tri2(T, inclusive=False)

        def pairs(kb, carry, acc, fine, strict):
            rows = pl.ds(pl.multiple_of(kb * T, T), T)
            k2 = [k_ref[rows, j * PAIR:(j + 1) * PAIR] for j in range(GP)]
            v2 = [v_ref[rows, j * PAIR:(j + 1) * PAIR] for j in range(GP)]
            _, amps, carry = _sb_blocks(q_tall, k2, strict, tri2, carry)
            parts = [_split2(a) for a in amps]
            new_acc, new_fine = [], []
            for j in range(GP):
                tall = jnp.concatenate([parts[2 * j][0], parts[2 * j + 1][0], parts[2 * j][1], parts[2 * j + 1][1]],
                                       axis=0)
                hi0, hi1, mid0, mid1 = _per_head(_nn(tall, v2[j]), T)
                new_acc.append(acc[j] + jnp.where(low, hi0, hi1))
                new_fine.append(fine[j] + jnp.where(low, mid0, mid1))
            return tuple(carry), tuple(new_acc), tuple(new_fine)

        zero = (jnp.zeros((T, PAIR), F32),) * GP
        carry, acc, fine = pairs(qi, (jnp.zeros((T, T), F32),) * (2 * GP), zero, zero, col < row)

        def cond(st):
            kb, alive, _, _, _ = st
            return jnp.logical_and(kb >= 0, alive)

        def step(st):
            kb, _, carry, acc, fine = st
            carry, acc, fine = pairs(kb, carry, acc, fine, None)
            return kb - 1, _sb_alive(carry), carry, acc, fine

        _, _, _, acc, fine = lax.while_loop(cond, step, (qi - 1, _sb_alive(carry), carry, acc, fine))
        for j in range(GP):
            o_ref[:, j * PAIR:(j + 1) * PAIR] = acc[j].astype(o_ref.dtype)
            ox_ref[:, j * PAIR:(j + 1) * PAIR] = acc[j] + fine[j]

    blk = pl.BlockSpec((T, GW), lambda p, i: (i, p))
    return _call_hosted(
        body, name, (nb, S // T),
        [blk, pl.BlockSpec((S, GW), lambda p, i: (0, nb + p)), pl.BlockSpec((S, GW), lambda p, i: (0, 2 * nb + p))],
        [blk, blk], [jax.ShapeDtypeStruct((S, half), _MXU), jax.ShapeDtypeStruct((S, half), F32)],
        [], [proj, proj, proj], ex)


def _sb_bwd(proj, ox, dmixed, name, ex=None):
    S, W = proj.shape
    half = W // 6
    npair = half // PAIR
    T = _blk(S, SB_T)
    GP = _blk(npair, SB_PAIRS)
    GW = GP * PAIR
    nb = npair // GP
    last = S // T - 1
    scale = HEAD_DIM ** -0.5

    def body(q_ref, k_ref, v_ref, ox_ref, do_ref, dq_ref, dk_ref, dv_ref, dka_ref, dva_ref):
        qi = pl.program_id(1)

        @pl.when(qi == 0)
        def _():
            dka_ref[...] = jnp.zeros_like(dka_ref)
            dva_ref[...] = jnp.zeros_like(dva_ref)

        low = _low_lanes(T)
        q2, do2, q_tall, do_tall, deltas = [], [], [], [], []
        for j in range(GP):
            cols = slice(j * PAIR, (j + 1) * PAIR)
            q2.append(q_ref[:, cols])
            do2.append(do_ref[:, cols])
            q_tall.append(_two_heads(q2[j], low, scale))
            dobs = [_one_head(do2[j], low, True), _one_head(do2[j], low, False)]
            do_tall.append(jnp.concatenate(dobs, axis=0))
            for e in range(2):
                deltas.append(jnp.sum(dobs[e].astype(F32) * ox_ref[:, cols], axis=-1, keepdims=True))
        row = lax.broadcasted_iota(jnp.int32, (T, T), 0)
        col = lax.broadcasted_iota(jnp.int32, (T, T), 1)
        tri_ex = _tri2(T, inclusive=False)
        tri_in = _tri2(T, inclusive=True)

        def pairs(kb, carry, right, dq, strict):
            rows = pl.ds(pl.multiple_of(kb * T, T), T)
            k2 = [k_ref[rows, j * PAIR:(j + 1) * PAIR] for j in range(GP)]
            v2 = [v_ref[rows, j * PAIR:(j + 1) * PAIR] for j in range(GP)]
            nh = 2 * GP
            gs = []
            for j in range(GP):
                gs += _per_head(_nt(do_tall[j], v2[j]), T)
            lbs, amps, carry = _sb_blocks(q_tall, k2, strict, tri_ex, carry)
            ags = [a * gg for a, gg in zip(amps, gs)]
            sums = _tri_sums(ags, tri_in)
            dzbs = []
            for h in range(nh):
                left = deltas[h] - (sums[h][0] + right[h])
                beta = jnp.exp(lbs[h])
                dz = ags[h] - beta * (ags[h] + left)
                if strict is not None:
                    dz = jnp.where(strict, dz, 0.0)
                dzbs.append(dz.astype(_MXU))
            abs_ = [a.astype(_MXU) for a in amps]
            new_dq = []
            for j in range(GP):
                cols = slice(j * PAIR, (j + 1) * PAIR)
                dk0, dk1 = _per_head(_tn(jnp.concatenate(dzbs[2 * j:2 * j + 2], axis=1), q2[j]), T)
                dv0, dv1 = _per_head(_tn(jnp.concatenate(abs_[2 * j:2 * j + 2], axis=1), do2[j]), T)
                dq0, dq1 = _per_head(_nn(jnp.concatenate(dzbs[2 * j:2 * j + 2], axis=0), k2[j]), T)
                dka_ref[rows, cols] += jnp.where(low, dk0, dk1)
                dva_ref[rows, cols] += jnp.where(low, dv0, dv1)
                new_dq.append(dq[j] + jnp.where(low, dq0, dq1))
            right = tuple(right[h] + sums[h][1] for h in range(nh))
            return tuple(carry), right, tuple(new_dq)

        zero = (jnp.zeros((T, T), F32),) * (2 * GP)
        carry, right, dq = pairs(qi, zero, zero, (jnp.zeros((T, PAIR), F32),) * GP, col < row)

        def cond(st):
            kb, alive, _, _, _ = st
            return jnp.logical_and(kb >= 0, alive)

        def step(st):
            kb, _, carry, right, dq = st
            carry, right, dq = pairs(kb, carry, right, dq, None)
            return kb - 1, _sb_alive(carry), carry, right, dq

        _, _, _, _, dq = lax.while_loop(cond, step, (qi - 1, _sb_alive(carry), carry, right, dq))
        for j in range(GP):
            dq_ref[:, j * PAIR:(j + 1) * PAIR] = (dq[j] * scale).astype(dq_ref.dtype)

        @pl.when(qi == last)
        def _():
            dk_ref[...] = (dka_ref[...] * scale).astype(dk_ref.dtype)
            dv_ref[...] = dva_ref[...].astype(dv_ref.dtype)

    blk = pl.BlockSpec((T, GW), lambda p, i: (i, p))
    full = pl.BlockSpec((S, GW), lambda p, i: (0, p))
    return _call_hosted(
        body, name, (nb, S // T),
        [blk, pl.BlockSpec((S, GW), lambda p, i: (0, nb + p)), pl.BlockSpec((S, GW), lambda p, i: (0, 2 * nb + p)),
         blk, blk],
        [blk, full, full], [jax.ShapeDtypeStruct((S, half), _MXU)] * 3,
        [pltpu.VMEM((S, GW), F32), pltpu.VMEM((S, GW), F32)], [proj, proj, proj, ox, dmixed], ex)


def _pair_norm(t2, g2, low):
    tf = t2.astype(F32)
    sq = tf * tf
    both = jnp.sum(sq, axis=-1, keepdims=True)
    first = jnp.sum(jnp.where(low, sq, 0.0), axis=-1, keepdims=True)
    r = jnp.where(low, lax.rsqrt(first * (1.0 / HEAD_DIM) + EPS), lax.rsqrt((both - first) * (1.0 / HEAD_DIM) + EPS))
    hat = tf * r
    return hat * g2, hat, r


def _pair_norm_bwd(dn, hat, r, g2, low):
    dhat = dn * g2
    prod = dhat * hat
    both = jnp.sum(prod, axis=-1, keepdims=True)
    first = jnp.sum(jnp.where(low, prod, 0.0), axis=-1, keepdims=True)
    mean = jnp.where(low, first, both - first) * (1.0 / HEAD_DIM)
    return r * (dhat - hat * mean)


def _ca_fill(j, k_ref, v_ref, gk_ref, kn_ref, vp_ref):
    S = k_ref.shape[0]
    cols = slice(j * PAIR, (j + 1) * PAIR)
    kn, _, _ = _pair_norm(k_ref[:, cols], gk_ref[...], _low_lanes(S))
    kn_ref[j, 0:PAD, :] = jnp.zeros((PAD, PAIR), kn_ref.dtype)
    vp_ref[j, 0:PAD, :] = jnp.zeros((PAD, PAIR), vp_ref.dtype)
    kn_ref[j, PAD:PAD + S, :] = kn.astype(kn_ref.dtype)
    vp_ref[j, PAD:PAD + S, :] = v_ref[:, cols]


def _ca_scores(j, q_ref, b2_ref, gq_ref, kn_ref, qi, low):
    qn, qhat, r = _pair_norm(q_ref[:, j * PAIR:(j + 1) * PAIR], gq_ref[...], low)
    qn = qn * HEAD_DIM ** -0.5
    band = pl.ds(pl.multiple_of(qi * CA_T, CA_T), CA_W)
    key_pos = qi * CA_T - PAD + lax.broadcasted_iota(jnp.int32, (CA_T, CA_W), 1)
    both = _per_head(_nt(_two_heads(qn, low), kn_ref[j, band, :]), CA_T)
    scores = [jnp.where(key_pos >= 0, both[e] + b2_ref[2 * j + e], NEG) for e in range(2)]
    return scores, qn.astype(_MXU), qhat, r


def _softmax(s):
    e = jnp.exp(s - jnp.max(s, axis=-1, keepdims=True))
    return e * (1.0 / jnp.sum(e, axis=-1, keepdims=True))


def _ca_fwd(proj, bias2, gq2, gk2, name, ex=None):
    S, W = proj.shape
    half = W // 6
    npair = half // PAIR
    GP = _blk(npair, CA_PAIRS_FWD)
    GW = GP * PAIR
    nb = npair // GP

    def body(q_ref, k_ref, v_ref, b2_ref, gq_ref, gk_ref, o_ref, kn_ref, vp_ref):
        qi = pl.program_id(1)

        @pl.when(qi == 0)
        def _():
            for j in range(GP):
                _ca_fill(j, k_ref, v_ref, gk_ref, kn_ref, vp_ref)

        low = _low_lanes(CA_T)
        band = pl.ds(pl.multiple_of(qi * CA_T, CA_T), CA_W)
        scores = [_ca_scores(j, q_ref, b2_ref, gq_ref, kn_ref, qi, low)[0] for j in range(GP)]
        probs = [[_softmax(s).astype(_MXU) for s in pair] for pair in scores]
        for j in range(GP):
            outs = _per_head(_nn(jnp.concatenate(probs[j], axis=0), vp_ref[j, band, :]), CA_T)
            o_ref[:, j * PAIR:(j + 1) * PAIR] = jnp.where(low, outs[0], outs[1]).astype(o_ref.dtype)

    vec = pl.BlockSpec((1, PAIR), lambda p, i: (0, 0))
    return _call_hosted(
        body, name, (nb, S // CA_T),
        [pl.BlockSpec((CA_T, GW), lambda p, i: (i, 3 * nb + p)),
         pl.BlockSpec((S, GW), lambda p, i: (0, 4 * nb + p)), pl.BlockSpec((S, GW), lambda p, i: (0, 5 * nb + p)),
         pl.BlockSpec((2 * GP, CA_T, CA_W), lambda p, i: (p, 0, 0)), vec, vec],
        [pl.BlockSpec((CA_T, GW), lambda p, i: (i, p))], [jax.ShapeDtypeStruct((S, half), _MXU)],
        [pltpu.VMEM((GP, PAD + S, PAIR), _MXU), pltpu.VMEM((GP, PAD + S, PAIR), _MXU)],
        [proj, proj, proj, bias2, gq2, gk2], ex)


def _ca_bwd(proj, bias2, gq2, gk2, dmixed, name, ex=None):
    S, W = proj.shape
    half = W // 6
    npair = half // PAIR
    GP = _blk(npair, CA_PAIRS_BWD)
    GW = GP * PAIR
    nb = npair // GP
    scale = HEAD_DIM ** -0.5
    last = S // CA_T - 1

    def body(q_ref, k_ref, v_ref, b2_ref, gq_ref, gk_ref, do_ref,
             dq_ref, dk_ref, dv_ref, db_ref, dgq_ref, dgk_ref, kn_ref, vp_ref, dkn_ref, dvp_ref):
        p_id, qi = pl.program_id(0), pl.program_id(1)

        @pl.when(qi == 0)
        def _():
            for j in range(GP):
                _ca_fill(j, k_ref, v_ref, gk_ref, kn_ref, vp_ref)
            dkn_ref[...] = jnp.zeros_like(dkn_ref)
            dvp_ref[...] = jnp.zeros_like(dvp_ref)
            db_ref[...] = jnp.zeros_like(db_ref)

        @pl.when(jnp.logical_and(p_id == 0, qi == 0))
        def _():
            dgq_ref[...] = jnp.zeros_like(dgq_ref)
            dgk_ref[...] = jnp.zeros_like(dgk_ref)

        low = _low_lanes(CA_T)
        top_w = lax.broadcasted_iota(jnp.int32, (PAIR, CA_W), 0) < HEAD_DIM
        band = pl.ds(pl.multiple_of(qi * CA_T, CA_T), CA_W)
        pairs = [_ca_scores(j, q_ref, b2_ref, gq_ref, kn_ref, qi, low) for j in range(GP)]
        do2 = [do_ref[:, j * PAIR:(j + 1) * PAIR] for j in range(GP)]
        dps = [_per_head(_nt(_two_heads(do2[j], low), vp_ref[j, band, :]), CA_T) for j in range(GP)]
        probs, dsbs = [], []
        for j in range(GP):
            pj, dj = [], []
            for e in range(2):
                p = _softmax(pairs[j][0][e])
                ds = p * (dps[j][e] - jnp.sum(p * dps[j][e], axis=-1, keepdims=True))
                db_ref[2 * j + e] += ds
                pj.append(p.astype(_MXU))
                dj.append(ds.astype(_MXU))
            probs.append(pj)
            dsbs.append(dj)
        dgq = jnp.zeros((1, PAIR), F32)
        for j in range(GP):
            _, qn, qhat, r = pairs[j]
            dq_h = _per_head(_nn(jnp.concatenate(dsbs[j], axis=0), kn_ref[j, band, :]), CA_T)
            dk_t = _tn(qn, jnp.concatenate(dsbs[j], axis=1))
            dv_t = _tn(do2[j], jnp.concatenate(probs[j], axis=1))
            dkn_ref[j, :, band] += jnp.where(top_w, dk_t[:, :CA_W], dk_t[:, CA_W:])
            dvp_ref[j, :, band] += jnp.where(top_w, dv_t[:, :CA_W], dv_t[:, CA_W:])
            dqn = jnp.where(low, dq_h[0], dq_h[1]) * scale
            dgq = dgq + jnp.sum(dqn * qhat, axis=0, keepdims=True)
            dq_ref[:, j * PAIR:(j + 1) * PAIR] = _pair_norm_bwd(dqn, qhat, r, gq_ref[...], low).astype(dq_ref.dtype)
        dgq_ref[...] += dgq

        @pl.when(qi == last)
        def _():
            low_s = _low_lanes(S)
            for j in range(GP):
                cols = slice(j * PAIR, (j + 1) * PAIR)
                _, khat, rk = _pair_norm(k_ref[:, cols], gk_ref[...], low_s)
                dkn = dkn_ref[j, :, PAD:PAD + S].T
                dgk_ref[...] += jnp.sum(dkn * khat, axis=0, keepdims=True)
                dk_ref[:, cols] = _pair_norm_bwd(dkn, khat, rk, gk_ref[...], low_s).astype(dk_ref.dtype)
                dv_ref[:, cols] = dvp_ref[j, :, PAD:PAD + S].T.astype(dv_ref.dtype)

    vec = pl.BlockSpec((1, PAIR), lambda p, i: (0, 0))
    tile = pl.BlockSpec((2 * GP, CA_T, CA_W), lambda p, i: (p, 0, 0))
    full = pl.BlockSpec((S, GW), lambda p, i: (0, p))
    return _call_hosted(
        body, name, (nb, S // CA_T),
        [pl.BlockSpec((CA_T, GW), lambda p, i: (i, 3 * nb + p)),
         pl.BlockSpec((S, GW), lambda p, i: (0, 4 * nb + p)), pl.BlockSpec((S, GW), lambda p, i: (0, 5 * nb + p)),
         tile, vec, vec, pl.BlockSpec((CA_T, GW), lambda p, i: (i, nb + p))],
        [pl.BlockSpec((CA_T, GW), lambda p, i: (i, p)), full, full, tile, vec, vec],
        [jax.ShapeDtypeStruct((S, half), _MXU)] * 3
        + [jax.ShapeDtypeStruct(bias2.shape, F32), jax.ShapeDtypeStruct((1, PAIR), F32),
           jax.ShapeDtypeStruct((1, PAIR), F32)],
        [pltpu.VMEM((GP, PAD + S, PAIR), _MXU), pltpu.VMEM((GP, PAD + S, PAIR), _MXU),
         pltpu.VMEM((GP, PAIR, PAD + S), F32), pltpu.VMEM((GP, PAIR, PAD + S), F32)],
        [proj, proj, proj, bias2, gq2, gk2, dmixed], ex)


def _pack_small(parts):
    flat = jnp.concatenate([p.reshape(-1) for layer in parts for p in layer])
    n = flat.shape[0]
    n_pad = -(-n // 1024) * 1024
    return jnp.pad(flat, (0, n_pad - n)).reshape(1, n_pad)


def _unpack_small(flat, shapes):
    out, off = [], 0
    for layer in shapes:
        cur = []
        for shp in layer:
            size = 1
            for s in shp:
                size *= s
            cur.append(flat[off:off + size].reshape(shp))
            off += size
        out.append(cur)
    return out


def kernel(x, c, g_norm1, w_in, g_q, g_k, rel_bias, w_o, g_norm2, w1, w2, w_ada, b_ada, loss_target, m_g_norm1, m_w_in, m_g_q, m_g_k, m_rel_bias, m_w_o, m_g_norm2, m_w1, m_w2, m_w_ada, m_b_ada, v_g_norm1, v_w_in, v_g_q, v_g_k, v_rel_bias, v_w_o, v_g_norm2, v_w1, v_w2, v_w_ada, v_b_ada):
    L = w_in.shape[0]
    S, D = x.shape[1:]
    H2 = D // HEAD_DIM // 2
    Ca = w_ada.shape[2]
    xi, yi, ci = _pos()
    me = 4 * xi + 2 * yi + ci
    place = jnp.stack([2 * xi + yi, ci]).astype(jnp.int32)

    wire = lambda a: a.astype(_MXU)
    by_cols = lambda g: g.transpose(1, 0, 2).reshape(D, g.shape[0] * g.shape[2])
    b_cols = lax.dynamic_slice(b_ada, (0, me * Ca), (L, Ca))
    c_all, mod_all, first = _start(c, w_ada, b_cols, wire(w_in[0]), "start")
    c_all = c_all.reshape(NDEV, D)
    mod = lax.dynamic_index_in_dim(mod_all, me, axis=1, keepdims=False)
    mod = mod.reshape(NDEV, L, Ca).transpose(1, 0, 2).reshape(L, 6, 1, D)
    W_in = {0: by_cols(first)}
    W_o, W_1, W_2 = {}, {}, {}

    xs = [x[0]]
    saved = []
    for l in range(L):
        sh1, sc1, gt1, sh2, sc2, gt2 = [mod[l, i] for i in range(6)]
        gn1, gn2 = g_norm1[l:l + 1], g_norm2[l:l + 1]
        gq2, gk2 = jnp.tile(g_q[l:l + 1], (1, 2)), jnp.tile(g_k[l:l + 1], (1, 2))
        proj, h1 = _ln_mod_matmul(xs[-1], gn1, sc1, sh1, W_in[l], f"l{l}_proj")
        (o_sb, ox_sb), got = _sb_fwd(proj, f"l{l}_sb_fwd", _gather_exchange([wire(w1[l]), wire(w2[l])]))
        W_1[l], W_2[l] = by_cols(got[0]), got[1].reshape(4 * D, D)
        bias2 = _ca_bias(rel_bias[l], f"l{l}_ca_bias")
        nxt = [wire(w_in[l + 1])] if l + 1 < L else []
        (o_ca,), got = _ca_fwd(proj, bias2, gq2, gk2, f"l{l}_ca_fwd", _gather_exchange([wire(w_o[l])] + nxt))
        W_o[l] = got[0].reshape(D, D)
        if nxt:
            W_in[l + 1] = by_cols(got[1])
        mixed = jnp.concatenate([o_sb, o_ca], axis=1)
        x1, f1, u, h2 = _attn_out_mlp_in(mixed, W_o[l], xs[-1], gt1, gn2, sc2, sh2, W_1[l], f"l{l}_attn_out_mlp_in")
        x0 = xs[-1]
        if l + 1 < L:
            x2, f2 = _matmul_res_gate(u, W_2[l], x1, gt2, True, f"l{l}_mlp_out")
            xs.append(x2)
        else:
            dx, f2, loss_part = _matmul_res_gate(u, W_2[l], x1, gt2, True, f"l{l}_mlp_out", loss_target[0])
        saved.append(dict(x0=x0, h1=h1, proj=proj, ox_sb=ox_sb, bias2=bias2, mixed=mixed, f1=f1, x1=x1,
                          h2=h2, u=u, f2=f2))

    owns, recv_b = {}, {}
    ready = []
    small_parts = [None] * L

    def partials(keys, grads, recv_a):
        parts = []
        for key, g, r in zip(keys, grads, recv_a):
            owns[key], part = _rs_chip_partial(place, g, r, f"rs_partial_l{key[0]}_{key[1]}")
            parts.append(part)
        return parts

    for l in reversed(range(L)):
        sv = saved[l]
        sh1, sc1, gt1, sh2, sc2, gt2 = [mod[l, i] for i in range(6)]
        gn1, gn2 = g_norm1[l:l + 1], g_norm2[l:l + 1]
        gq2, gk2 = jnp.tile(g_q[l:l + 1], (1, 2)), jnp.tile(g_k[l:l + 1], (1, 2))
        dz2, dgt2, du = _gate_nt_matmul(dx, sv["f2"], gt2, W_2[l], sv["u"], f"l{l}_mlp_out_bwd")
        gw2 = _tn_matmul(sv["u"], dz2, False, True, f"l{l}_gw2")
        gw1 = _tn_matmul(sv["h2"], du, True, False, f"l{l}_gw1")
        dx, dsh2, dsc2, dgn2, dz1, dgt1, dmixed = _mlp_in_attn_out_bwd(
            du, W_1[l], sv["x1"], gn2, sc2, sh2, dx, sv["f1"], gt1, W_o[l], f"l{l}_mlp_in_attn_out_bwd")
        gwo = _tn_matmul(sv["mixed"], dz1, False, False, f"l{l}_gwo")
        ready += [((l, 1), gwo), ((l, 2), gw1), ((l, 3), gw2)]
        keys, grads = [k for k, _ in ready], [g for _, g in ready]
        (dq_sb, dk_sb, dv_sb), recv_a = _sb_bwd(sv["proj"], sv["ox_sb"], dmixed, f"l{l}_sb_bwd",
                                                _sibling_exchange(grads))
        parts = partials(keys, grads, recv_a)
        (dq_ca, dk_ca, dv_ca, dbias2, dgq2, dgk2), got = _ca_bwd(sv["proj"], sv["bias2"], gq2, gk2, dmixed,
                                                                 f"l{l}_ca_bwd", _chip_exchange(parts))
        recv_b.update(zip(keys, got))
        dgq = dgq2[:, :HEAD_DIM] + dgq2[:, HEAD_DIM:]
        dgk = dgk2[:, :HEAD_DIM] + dgk2[:, HEAD_DIM:]
        drb = _ca_bias_bwd(dbias2, f"l{l}_ca_bias_bwd")
        dproj = jnp.concatenate([dq_sb, dk_sb, dv_sb, dq_ca, dk_ca, dv_ca], axis=1)
        gwin = _tn_matmul(sv["h1"], dproj, True, False, f"l{l}_gwin")
        ready = [((l, 0), gwin)]
        dx, dsh1, dsc1, dgn1 = _nt_ln_bwd(dproj, W_in[l], sv["x0"], gn1, sc1, sh1, dx, f"l{l}_proj_bwd")
        dmod = jnp.concatenate([dsh1, dsc1, dgt1, dsh2, dsc2, dgt2], axis=1)
        small_parts[l] = [dgn1, dgq, dgk, drb, dgn2, dmod]
    grad_x = dx[None]

    keys, grads = [k for k, _ in ready], [g for _, g in ready]
    parts = partials(keys, grads, _run_exchange(_sibling_exchange(grads), "rs_sibling_last"))
    recv_b.update(zip(keys, _run_exchange(_chip_exchange(parts), "rs_chips_last")))
    big_out = []
    for t, (w, m, v) in enumerate([(w_in, m_w_in, v_w_in), (w_o, m_w_o, v_w_o), (w1, m_w1, v_w1), (w2, m_w2, v_w2)]):
        big_out.append(_rs_sum_adamw([owns[(l, t)] for l in range(L)], [recv_b[(l, t)] for l in range(L)],
                                     w, m, v, f"adamw_big_{t}"))

    packed = _pack_small(small_parts)
    gathered_small = _all_gather_small(packed, "ag_small_grads")
    small_sum = _sum_devices(gathered_small, "sum_small_grads")
    shapes = [[(1, D), (1, HEAD_DIM), (1, HEAD_DIM), (H2, N_REL), (1, D), (1, 6 * D)]] * L
    names = ["g_norm1", "g_q", "g_k", "rel_bias", "g_norm2", "b_ada"]
    small_w = {"g_norm1": (g_norm1, m_g_norm1, v_g_norm1), "g_q": (g_q, m_g_q, v_g_q), "g_k": (g_k, m_g_k, v_g_k),
               "rel_bias": (rel_bias, m_rel_bias, v_rel_bias), "g_norm2": (g_norm2, m_g_norm2, v_g_norm2),
               "b_ada": (b_ada, m_b_ada, v_b_ada)}
    packs = [_pack_small([[small_w[n][k][l] for n in names] for l in range(L)]) for k in range(3)]
    n_pad = packed.shape[1]
    as_rows = lambda a: a.reshape(n_pad // 128, 128)
    sd, sm, sv_ = _adamw(as_rows(packs[0]), as_rows(small_sum), as_rows(packs[1]), as_rows(packs[2]), "adamw_small")
    small_out = {}
    for key, flat in [("grad", small_sum), ("delta", sd), ("m", sm), ("v", sv_)]:
        per_layer = _unpack_small(flat.reshape(-1), shapes)
        for i, n in enumerate(names):
            small_out[(key, n)] = jnp.stack([per_layer[l][i].reshape(small_w[n][0].shape[1:]) for l in range(L)])

    layer_len = 2 * D + 2 * HEAD_DIM + H2 * N_REL + 6 * D
    rows = gathered_small.reshape(NDEV, n_pad)
    dmod_all = jnp.stack([rows[:, l * layer_len + layer_len - 6 * D:(l + 1) * layer_len] for l in range(L)])
    dmod_cols = lax.dynamic_slice(dmod_all, (0, 0, me * Ca), (L, NDEV, Ca))
    dmod_cols = jnp.pad(dmod_cols, ((0, 0), (0, 128 - NDEV), (0, 0)))
    c_t = jnp.pad(c_all.T, ((0, 0), (0, 128 - NDEV)))
    g_ada = _w_ada_grad(c_t, dmod_cols, "w_ada_grad")
    flat2 = lambda a: a.reshape(L * D, Ca)
    ad, am, av = _adamw(flat2(w_ada), flat2(g_ada), flat2(m_w_ada), flat2(v_w_ada), "adamw_w_ada")
    ada_out = [g_ada] + [a.reshape(L, D, Ca) for a in (ad, am, av)]

    def leaf(kind):
        k = {"grad": 0, "delta": 1, "m": 2, "v": 3}[kind]
        return [small_out[(kind, "g_norm1")], big_out[0][k], small_out[(kind, "g_q")], small_out[(kind, "g_k")],
                small_out[(kind, "rel_bias")], big_out[1][k], small_out[(kind, "g_norm2")], big_out[2][k],
                big_out[3][k], ada_out[k], small_out[(kind, "b_ada")]]

    loss = lax.psum(loss_part[0, 0], ("x", "y", "c"))
    return (loss, grad_x, *leaf("grad"), *leaf("delta"), *leaf("m"), *leaf("v"))
```

```python
import functools

import jax
import jax.numpy as jnp
from jax import lax
from jax.experimental import pallas as pl
from jax.experimental.pallas import tpu as pltpu

F32 = jnp.float32
_MXU = jnp.bfloat16

HEAD_DIM = 64
CHUNK = 64
LEFT_CHUNKS = 8
PAD = LEFT_CHUNKS * CHUNK
BAND = PAD + CHUNK
REL_CLIP = 128
N_REL = 2 * REL_CLIP + 1
EPS = 1e-6
NEG = -1e30
NDEV = 8
SB_T = 128
CA_T = 2 * CHUNK
CA_W = CA_T + PAD
SB_SKIP = -104.0
PAIR = 2 * HEAD_DIM
SB_PAIRS = 4
CA_PAIRS_FWD = 4
CA_PAIRS_BWD = 2
ROW_BLOCK = 512
SKEW_W = CA_W + CA_T - 1

ADAM_LR, ADAM_B1, ADAM_B2, ADAM_EPS, ADAM_WD, ADAM_STEP = 0.001, 0.9, 0.999, 1e-08, 0.01, 10

MESH = pl.DeviceIdType.MESH
VMEM_SPEC = pl.BlockSpec(memory_space=pltpu.VMEM)
ANY_SPEC = pl.BlockSpec(memory_space=pl.ANY)


def _nn(a, b):
    return lax.dot_general(a, b, (((1,), (0,)), ((), ())), preferred_element_type=F32)


def _nt(a, b):
    return lax.dot_general(a, b, (((1,), (1,)), ((), ())), preferred_element_type=F32)


def _tn(a, b):
    return lax.dot_general(a, b, (((0,), (0,)), ((), ())), preferred_element_type=F32)


def _blk(n, pref):
    return pref if n % pref == 0 else n


def _pos():
    return lax.axis_index("x"), lax.axis_index("y"), lax.axis_index("c")


def _flip(v, bit):
    return 1 - v if bit else v


def _gather_small(x_ref, out_ref, send_sems, recv_sems):
    R, C = x_ref.shape
    x, y, c = _pos()
    me = 4 * x + 2 * y + c

    def peer(k):
        return (_flip(x, k & 4), _flip(y, k & 2), _flip(c, k & 1))

    def copy(k, slot):
        return pltpu.make_async_remote_copy(
            src_ref=x_ref, dst_ref=out_ref.at[slot], send_sem=send_sems.at[k - 1],
            recv_sem=recv_sems.at[k - 1], device_id=peer(k), device_id_type=MESH)

    out_ref[pl.ds(me, 1), :, :] = x_ref[...].reshape(1, R, C)
    sends = [copy(k, me) for k in range(1, NDEV)]
    for cp in sends:
        cp.start()
    for k in range(1, NDEV):
        px, py, pc = peer(k)
        copy(k, 4 * px + 2 * py + pc).wait_recv()
    for cp in sends:
        cp.wait_send()


def _all_gather_small(blk, name):
    return pl.pallas_call(
        lambda x_ref, out_ref, send_sems, recv_sems: _gather_small(x_ref, out_ref, send_sems, recv_sems), name=name,
        out_shape=jax.ShapeDtypeStruct((NDEV,) + blk.shape, blk.dtype),
        in_specs=[VMEM_SPEC], out_specs=VMEM_SPEC,
        scratch_shapes=[pltpu.SemaphoreType.DMA((NDEV - 1,)), pltpu.SemaphoreType.DMA((NDEV - 1,))],
    )(blk)


class _Exchange:
    def __init__(self, inputs, out_shapes, sems, start, finish, middle=None):
        self.inputs, self.out_shapes, self.sems = list(inputs), list(out_shapes), list(sems)
        self.start, self.middle, self.finish = start, middle, finish


def _run_exchange(ex, name):
    n_in, n_out = len(ex.inputs), len(ex.out_shapes)

    def body(*refs):
        ins, outs, sems = refs[:n_in], refs[n_in:n_in + n_out], refs[n_in + n_out:]
        ex.start(ins, outs, sems)
        if ex.middle is not None:
            ex.middle(ins, outs, sems)
        ex.finish(ins, outs, sems)

    return pl.pallas_call(
        body, name=name, out_shape=ex.out_shapes, in_specs=[ANY_SPEC] * n_in, out_specs=[ANY_SPEC] * n_out,
        scratch_shapes=ex.sems,
    )(*ex.inputs)


def _hosted(body, n_in, n_out, ex, step, steps):
    if ex is None:
        return body
    xi, xo = len(ex.inputs), len(ex.out_shapes)

    def wrapped(*refs):
        own_in, ex_in = refs[:n_in], refs[n_in:n_in + xi]
        rest = refs[n_in + xi:]
        own_out, ex_out = rest[:n_out], rest[n_out:n_out + xo]
        rest = rest[n_out + xo:]
        own_scratch, ex_sems = rest[:len(rest) - len(ex.sems)], rest[len(rest) - len(ex.sems):]
        t = step()
        pl.when(t == 0)(lambda: ex.start(ex_in, ex_out, ex_sems))
        body(*own_in, *own_out, *own_scratch)
        if ex.middle is not None:
            pl.when(t == (steps * 7) // 8)(lambda: ex.middle(ex_in, ex_out, ex_sems))
        pl.when(t == steps - 1)(lambda: ex.finish(ex_in, ex_out, ex_sems))

    return wrapped


def _call_hosted(body, name, grid, in_specs, out_specs, out_shape, scratch, args, ex):
    n_in, n_out = len(in_specs), len(out_specs)
    steps = 1
    for extent in grid:
        steps *= extent

    def step():
        t = pl.program_id(0)
        for axis in range(1, len(grid)):
            t = t * grid[axis] + pl.program_id(axis)
        return t

    if ex is not None:
        in_specs = in_specs + [ANY_SPEC] * len(ex.inputs)
        out_specs = out_specs + [ANY_SPEC] * len(ex.out_shapes)
        out_shape = out_shape + ex.out_shapes
        scratch = scratch + ex.sems
        args = args + ex.inputs
    outs = pl.pallas_call(
        _hosted(body, n_in, n_out, ex, step, steps), name=name, grid=grid, in_specs=in_specs, out_specs=out_specs,
        out_shape=out_shape, scratch_shapes=scratch,
    )(*args)
    return list(outs[:n_out]), list(outs[n_out:])


def _gather_exchange(shards):
    n = len(shards)

    def setup(ins, outs, sems):
        send_sems, recv_sems, local_sems = sems
        x, y, c = _pos()
        me, sibling = (x, y, c), (x, y, 1 - c)
        chips = [(1 - x, y), (x, 1 - y), (1 - x, 1 - y)]

        def copy(i, k, block, to, src=None):
            px, py, pc = block
            dst = outs[i].at[4 * px + 2 * py + pc]
            return pltpu.make_async_remote_copy(
                src_ref=dst if src is None else src, dst_ref=dst, send_sem=send_sems.at[7 * i + k],
                recv_sem=recv_sems.at[7 * i + k], device_id=to, device_id_type=MESH)

        def mine(i):
            return pltpu.make_async_copy(ins[i], outs[i].at[4 * x + 2 * y + c], local_sems.at[i])

        def first(i):
            return [copy(i, 0, me, sibling, src=ins[i])] + [
                copy(i, 1 + j, me, (*chip, c), src=ins[i]) for j, chip in enumerate(chips)]

        def passed(i, j):
            return copy(i, 4 + j, (*chips[j], c), sibling)

        return me, sibling, chips, c, copy, mine, first, passed

    def start(ins, outs, sems):
        _, _, _, _, _, mine, first, _ = setup(ins, outs, sems)
        for i in range(n):
            mine(i).start()
            for cp in first(i):
                cp.start()

    def middle(ins, outs, sems):
        me, _, chips, c, copy, _, _, passed = setup(ins, outs, sems)
        for j, chip in enumerate(chips):
            for i in range(n):
                copy(i, 1 + j, (*chip, c), me).wait_recv()
                passed(i, j).start()

    def finish(ins, outs, sems):
        me, sibling, chips, c, copy, mine, first, passed = setup(ins, outs, sems)
        for i in range(n):
            copy(i, 0, sibling, me).wait_recv()
            for j, chip in enumerate(chips):
                copy(i, 4 + j, (*chip, 1 - c), me).wait_recv()
        for i in range(n):
            for cp in first(i) + [passed(i, j) for j in range(3)]:
                cp.wait_send()
            mine(i).wait()

    return _Exchange(
        shards, [jax.ShapeDtypeStruct((NDEV,) + s.shape, s.dtype) for s in shards],
        [pltpu.SemaphoreType.DMA((7 * n,)), pltpu.SemaphoreType.DMA((7 * n,)), pltpu.SemaphoreType.DMA((n,))],
        start, finish, middle)


def _sibling_exchange(grads):
    n = len(grads)

    def copies(ins, outs, sems):
        send_sems, recv_sems = sems
        x, y, c = _pos()
        return [pltpu.make_async_remote_copy(
            src_ref=ins[i].at[2 * q + (1 - c)], dst_ref=outs[i].at[q], send_sem=send_sems.at[4 * i + q],
            recv_sem=recv_sems.at[4 * i + q], device_id=(x, y, 1 - c), device_id_type=MESH)
            for i in range(n) for q in range(4)]

    def start(ins, outs, sems):
        for cp in copies(ins, outs, sems):
            cp.start()

    def finish(ins, outs, sems):
        for cp in copies(ins, outs, sems):
            cp.wait()

    return _Exchange(
        grads, [jax.ShapeDtypeStruct((4,) + g.shape[1:], g.dtype) for g in grads],
        [pltpu.SemaphoreType.DMA((4 * n,)), pltpu.SemaphoreType.DMA((4 * n,))], start, finish)


def _chip_exchange(parts):
    n = len(parts)

    def copies(ins, outs, sems):
        send_sems, recv_sems = sems
        x, y, c = _pos()
        return [pltpu.make_async_remote_copy(
            src_ref=ins[i].at[j - 1], dst_ref=outs[i].at[j - 1], send_sem=send_sems.at[3 * i + j - 1],
            recv_sem=recv_sems.at[3 * i + j - 1], device_id=(_flip(x, j & 2), _flip(y, j & 1), c),
            device_id_type=MESH) for i in range(n) for j in range(1, 4)]

    def start(ins, outs, sems):
        for cp in copies(ins, outs, sems):
            cp.start()

    def finish(ins, outs, sems):
        for cp in copies(ins, outs, sems):
            cp.wait()

    return _Exchange(
        parts, [jax.ShapeDtypeStruct(p.shape, p.dtype) for p in parts],
        [pltpu.SemaphoreType.DMA((3 * n,)), pltpu.SemaphoreType.DMA((3 * n,))], start, finish)


def _rs_chip_partial(place, grad, recv, name):
    _, R, C = grad.shape
    tr = _blk(R, 256)

    def body(place_ref, *refs):
        g_refs, r_refs = refs[:4], refs[4:8]
        own_ref, out_ref = refs[8:]
        own_ref[...] = g_refs[0][0] + r_refs[0][0]
        for j in range(1, 4):
            out_ref[j - 1] = (g_refs[j][0] + r_refs[j][0]).astype(out_ref.dtype)

    def g_map(j):
        return lambda i, p: (2 * jnp.bitwise_xor(p[0], j) + p[1], i, 0)

    def r_map(j):
        return lambda i, p: (jnp.bitwise_xor(p[0], j), i, 0)

    grid_spec = pltpu.PrefetchScalarGridSpec(
        num_scalar_prefetch=1, grid=(R // tr,),
        in_specs=[pl.BlockSpec((1, tr, C), g_map(j)) for j in range(4)]
        + [pl.BlockSpec((1, tr, C), r_map(j)) for j in range(4)],
        out_specs=[pl.BlockSpec((tr, C), lambda i, p: (i, 0)), pl.BlockSpec((3, tr, C), lambda i, p: (0, i, 0))])
    return pl.pallas_call(
        body, name=name, grid_spec=grid_spec,
        out_shape=[jax.ShapeDtypeStruct((R, C), F32), jax.ShapeDtypeStruct((3, R, C), _MXU)],
    )(place, *([grad] * 4), *([recv] * 4))


def _adamw_math(w, g, m, v):
    m = ADAM_B1 * m + (1.0 - ADAM_B1) * g
    v = ADAM_B2 * v + (1.0 - ADAM_B2) * (g * g)
    m_hat = m / (1.0 - ADAM_B1 ** ADAM_STEP)
    v_hat = v / (1.0 - ADAM_B2 ** ADAM_STEP)
    delta = -ADAM_LR * (m_hat / (jnp.sqrt(v_hat) + ADAM_EPS) + ADAM_WD * w)
    return delta, m, v


def _adamw(w, g, m, v, name):
    R, C = w.shape
    tr = _blk(R, 256)

    def body(w_ref, g_ref, m_ref, v_ref, d_ref, nm_ref, nv_ref):
        d, nm, nv = _adamw_math(w_ref[...], g_ref[...], m_ref[...], v_ref[...])
        d_ref[...] = d
        nm_ref[...] = nm
        nv_ref[...] = nv

    spec = pl.BlockSpec((tr, C), lambda i: (i, 0))
    return pl.pallas_call(
        body, name=name, grid=(R // tr,), in_specs=[spec] * 4, out_specs=[spec] * 3,
        out_shape=[jax.ShapeDtypeStruct((R, C), F32)] * 3,
    )(w, g, m, v)


def _rs_sum_adamw(owns, recvs, w, m, v, name):
    L, R, C = w.shape
    tr = _blk(R, 256)
    nr = R // tr

    def body(o0, o1, r0, r1, w_ref, m_ref, v_ref, g_ref, d_ref, nm_ref, nv_ref):
        def step(o_ref, r_ref):
            g = o_ref[...]
            for j in range(3):
                g = g + r_ref[j].astype(F32)
            d, nm, nv = _adamw_math(w_ref[0], g, m_ref[0], v_ref[0])
            g_ref[0] = g
            d_ref[0] = d
            nm_ref[0] = nm
            nv_ref[0] = nv

        pl.when(pl.program_id(0) == 0)(lambda: step(o0, r0))
        pl.when(pl.program_id(0) == 1)(lambda: step(o1, r1))

    def hold(layer):
        if layer == 0:
            return lambda l, i: i * (1 - l) + (nr - 1) * l
        return lambda l, i: i * l

    own_spec = [pl.BlockSpec((tr, C), functools.partial(lambda l, i, f: (f(l, i), 0), f=hold(k))) for k in range(2)]
    recv_spec = [pl.BlockSpec((3, tr, C), functools.partial(lambda l, i, f: (0, f(l, i), 0), f=hold(k)))
                 for k in range(2)]
    lay = pl.BlockSpec((1, tr, C), lambda l, i: (l, i, 0))
    return pl.pallas_call(
        body, name=name, grid=(L, nr),
        in_specs=own_spec + recv_spec + [lay] * 3, out_specs=[lay] * 4,
        out_shape=[jax.ShapeDtypeStruct((L, R, C), F32)] * 4,
    )(owns[0], owns[1], recvs[0], recvs[1], w, m, v)


def _silu(x):
    return x / (1.0 + jnp.exp(-x))


def _start(c, w_ada, b_cols, w_first, name):
    L, D, Ca = w_ada.shape
    ex = _gather_exchange([w_first])

    def body(c_ref, w_ref, b_ref, first_ref, call_ref, mod_ref, gathered_ref, part_ref, c_send, c_recv, m_send,
             m_recv, *ex_sems):
        _gather_small(c_ref, call_ref, c_send, c_recv)
        ex.start([first_ref], [gathered_ref], ex_sems)
        act = _silu(call_ref[...].reshape(NDEV, D)).astype(_MXU)
        for l in range(L):
            part_ref[:, l * Ca:(l + 1) * Ca] = _nn(act, w_ref[l].astype(_MXU)) + b_ref[l:l + 1, :]
        _gather_small(part_ref, mod_ref, m_send, m_recv)
        ex.middle([first_ref], [gathered_ref], ex_sems)
        ex.finish([first_ref], [gathered_ref], ex_sems)

    pairs = [pltpu.SemaphoreType.DMA((NDEV - 1,))] * 4
    return pl.pallas_call(
        body, name=name,
        out_shape=[jax.ShapeDtypeStruct((NDEV, 1, D), F32), jax.ShapeDtypeStruct((NDEV, NDEV, L * Ca), F32)]
        + ex.out_shapes,
        in_specs=[VMEM_SPEC] * 3 + [ANY_SPEC], out_specs=[VMEM_SPEC, VMEM_SPEC, ANY_SPEC],
        scratch_shapes=[pltpu.VMEM((NDEV, L * Ca), F32)] + pairs + ex.sems,
    )(c, w_ada, b_cols, w_first)


def _w_ada_grad(c_t, dmod_cols, name):
    L, _, Ca = dmod_cols.shape
    D = c_t.shape[0]

    def body(c_ref, d_ref, o_ref):
        act = _silu(c_ref[...]).astype(_MXU)
        for l in range(L):
            o_ref[l] = _nn(act, d_ref[l].astype(_MXU))

    return pl.pallas_call(
        body, name=name, out_shape=jax.ShapeDtypeStruct((L, D, Ca), F32),
        in_specs=[VMEM_SPEC] * 2, out_specs=VMEM_SPEC,
    )(c_t, dmod_cols)


def _sum_devices(gathered, name):
    _, _, N = gathered.shape

    def body(x_ref, o_ref):
        acc = x_ref[0]
        for d in range(1, NDEV):
            acc = acc + x_ref[d]
        o_ref[...] = acc

    return pl.pallas_call(
        body, name=name, out_shape=jax.ShapeDtypeStruct((1, N), F32),
        in_specs=[VMEM_SPEC], out_specs=VMEM_SPEC,
    )(gathered)


def _ln_mod_matmul(x, g, sc, sh, w, name):
    S, D = x.shape
    N = w.shape[1]
    tm = _blk(S, 2 * ROW_BLOCK)

    def body(x_ref, g_ref, sc_ref, sh_ref, w_ref, o_ref, h_ref):
        xv = x_ref[...]
        r = lax.rsqrt(jnp.mean(xv * xv, axis=-1, keepdims=True) + EPS)
        hv = ((xv * r) * g_ref[...]) * (1.0 + sc_ref[...]) + sh_ref[...]
        hb = hv.astype(_MXU)
        h_ref[...] = hb
        o_ref[...] = _nn(hb, w_ref[...]).astype(o_ref.dtype)

    vec = pl.BlockSpec((1, D), lambda i: (0, 0))
    row = lambda width: pl.BlockSpec((tm, width), lambda i: (i, 0))
    return pl.pallas_call(
        body, name=name, grid=(S // tm,),
        in_specs=[row(D), vec, vec, vec, pl.BlockSpec((D, N), lambda i: (0, 0), pipeline_mode=pl.Buffered(1))],
        out_specs=[row(N), row(D)],
        out_shape=[jax.ShapeDtypeStruct((S, N), _MXU), jax.ShapeDtypeStruct((S, D), _MXU)],
    )(x, g, sc, sh, w)


def _attn_out_mlp_in(mixed, w_o, xres, gt, g, sc, sh, w1, name):
    S, D = xres.shape
    N = w1.shape[1]
    tm = _blk(S, ROW_BLOCK)

    def body(a_ref, wo_ref, x_ref, gt_ref, g_ref, sc_ref, sh_ref, w1_ref, x1_ref, f_ref, u_ref, h_ref):
        f = _nn(a_ref[...], wo_ref[...])
        f_ref[...] = f.astype(f_ref.dtype)
        xv = x_ref[...] + gt_ref[...] * f
        x1_ref[...] = xv
        r = lax.rsqrt(jnp.mean(xv * xv, axis=-1, keepdims=True) + EPS)
        hb = (((xv * r) * g_ref[...]) * (1.0 + sc_ref[...]) + sh_ref[...]).astype(_MXU)
        h_ref[...] = hb
        u_ref[...] = _nn(hb, w1_ref[...]).astype(u_ref.dtype)

    vec = pl.BlockSpec((1, D), lambda i: (0, 0))
    row = lambda width: pl.BlockSpec((tm, width), lambda i: (i, 0))
    whole = lambda arr: pl.BlockSpec(arr.shape, lambda i: (0, 0))
    return pl.pallas_call(
        body, name=name, grid=(S // tm,),
        in_specs=[row(D), whole(w_o), row(D), vec, vec, vec, vec, whole(w1)],
        out_specs=[row(D), row(D), row(N), row(D)],
        out_shape=[jax.ShapeDtypeStruct((S, D), F32), jax.ShapeDtypeStruct((S, D), _MXU),
                   jax.ShapeDtypeStruct((S, N), _MXU), jax.ShapeDtypeStruct((S, D), _MXU)],
    )(mixed, w_o, xres, gt, g, sc, sh, w1)


def _matmul_res_gate(a, w, xres, gt, relu2, name, target=None):
    S, K = a.shape
    N = w.shape[1]
    tm = _blk(S, ROW_BLOCK)
    last = S // tm - 1
    with_loss = target is not None

    def body(a_ref, w_ref, x_ref, gt_ref, *rest):
        av = a_ref[...]
        if relu2:
            af = jnp.maximum(av.astype(F32), 0.0)
            av = (af * af).astype(_MXU)
        f = _nn(av, w_ref[...])
        out = x_ref[...] + gt_ref[...] * f
        if not with_loss:
            o_ref, f_ref = rest
            o_ref[...] = out
        else:
            t_ref, o_ref, f_ref, l_ref, acc_ref = rest
            i = pl.program_id(0)
            e = out - t_ref[...]
            o_ref[...] = e * (1.0 / N)
            _accumulate(acc_ref, jnp.sum(e * e, axis=0, keepdims=True), i == 0)

            @pl.when(i == last)
            def _():
                l_ref[...] = (0.5 / N) * jnp.sum(acc_ref[...], axis=1, keepdims=True)
        f_ref[...] = f.astype(f_ref.dtype)

    row = lambda width: pl.BlockSpec((tm, width), lambda i: (i, 0))
    in_specs = [row(K), pl.BlockSpec((K, N), lambda i: (0, 0)), row(N), pl.BlockSpec((1, N), lambda i: (0, 0))]
    out_specs = [row(N), row(N)]
    out_shape = [jax.ShapeDtypeStruct((S, N), F32), jax.ShapeDtypeStruct((S, N), _MXU)]
    args = [a, w, xres, gt]
    if with_loss:
        in_specs.append(row(N))
        args.append(target)
        out_specs.append(pl.BlockSpec((1, 1), lambda i: (0, 0)))
        out_shape.append(jax.ShapeDtypeStruct((1, 1), F32))
    return pl.pallas_call(
        body, name=name, grid=(S // tm,), in_specs=in_specs, out_specs=out_specs, out_shape=out_shape,
        scratch_shapes=[pltpu.VMEM((1, N), F32)] if with_loss else [],
    )(*args)


def _accumulate(ref, part, first):
    @pl.when(first)
    def _():
        ref[...] = part

    @pl.when(jnp.logical_not(first))
    def _():
        ref[...] += part


def _gate_nt_matmul(dx, f, gt, w, u, name):
    S, D = dx.shape
    N = w.shape[0]
    tm = _blk(S, ROW_BLOCK)
    with_u = u is not None

    def body(*refs):
        if with_u:
            dx_ref, f_ref, gt_ref, w_ref, u_ref, dz_ref, dgt_ref, res_ref = refs
        else:
            dx_ref, f_ref, gt_ref, w_ref, dz_ref, dgt_ref, res_ref = refs
        dxv = dx_ref[...]
        dz = (dxv * gt_ref[...]).astype(_MXU)
        dz_ref[...] = dz
        _accumulate(dgt_ref, jnp.sum(dxv * f_ref[...].astype(F32), axis=0, keepdims=True), pl.program_id(0) == 0)
        r = _nt(dz, w_ref[...])
        if with_u:
            r = r * (2.0 * jnp.maximum(u_ref[...].astype(F32), 0.0))
        res_ref[...] = r.astype(res_ref.dtype)

    row = lambda width: pl.BlockSpec((tm, width), lambda i: (i, 0))
    in_specs = [row(D), row(D), pl.BlockSpec((1, D), lambda i: (0, 0)), pl.BlockSpec((N, D), lambda i: (0, 0))]
    args = [dx, f, gt, w]
    if with_u:
        in_specs.append(row(N))
        args.append(u)
    return pl.pallas_call(
        body, name=name, grid=(S // tm,), in_specs=in_specs,
        out_specs=[row(D), pl.BlockSpec((1, D), lambda i: (0, 0)), row(N)],
        out_shape=[jax.ShapeDtypeStruct((S, D), _MXU), jax.ShapeDtypeStruct((1, D), F32),
                   jax.ShapeDtypeStruct((S, N), _MXU)],
    )(*args)


def _mlp_in_attn_out_bwd(du, w1, x1, g, sc, sh, dxres, f, gt, w_o, name):
    S, D = x1.shape
    N = w1.shape[1]
    tm = _blk(S, ROW_BLOCK)

    def body(du_ref, w1_ref, x_ref, g_ref, sc_ref, sh_ref, dxr_ref, f_ref, gt_ref, wo_ref,
             dx_ref, dsh_ref, dsc_ref, dg_ref, dz_ref, dgt_ref, dm_ref):
        dh = _nt(du_ref[...], w1_ref[...])
        xv = x_ref[...]
        r = lax.rsqrt(jnp.mean(xv * xv, axis=-1, keepdims=True) + EPS)
        xhat = xv * r
        gv = g_ref[...]
        dn = dh * (1.0 + sc_ref[...])
        dxhat = dn * gv
        dxv = dxr_ref[...] + r * (dxhat - xhat * jnp.mean(dxhat * xhat, axis=-1, keepdims=True))
        dx_ref[...] = dxv
        first = pl.program_id(0) == 0
        _accumulate(dsh_ref, jnp.sum(dh, axis=0, keepdims=True), first)
        _accumulate(dsc_ref, jnp.sum(dh * (xhat * gv), axis=0, keepdims=True), first)
        _accumulate(dg_ref, jnp.sum(dn * xhat, axis=0, keepdims=True), first)
        dz = (dxv * gt_ref[...]).astype(_MXU)
        dz_ref[...] = dz
        _accumulate(dgt_ref, jnp.sum(dxv * f_ref[...].astype(F32), axis=0, keepdims=True), first)
        dm_ref[...] = _nt(dz, wo_ref[...]).astype(dm_ref.dtype)

    row = lambda width: pl.BlockSpec((tm, width), lambda i: (i, 0))
    vec = pl.BlockSpec((1, D), lambda i: (0, 0))
    whole = lambda arr: pl.BlockSpec(arr.shape, lambda i: (0, 0))
    return pl.pallas_call(
        body, name=name, grid=(S // tm,),
        in_specs=[row(N), whole(w1), row(D), vec, vec, vec, row(D), row(D), vec, whole(w_o)],
        out_specs=[row(D), vec, vec, vec, row(D), vec, row(D)],
        out_shape=[jax.ShapeDtypeStruct((S, D), F32)] + [jax.ShapeDtypeStruct((1, D), F32)] * 3
        + [jax.ShapeDtypeStruct((S, D), _MXU), jax.ShapeDtypeStruct((1, D), F32), jax.ShapeDtypeStruct((S, D), _MXU)],
    )(du, w1, x1, g, sc, sh, dxres, f, gt, w_o)


def _tn_matmul(a, b, by_col, relu2, name):
    S, Ka = a.shape
    Nb = b.shape[1]
    ts = _blk(S, 2 * ROW_BLOCK)
    half = NDEV // 2
    if by_col:
        R, C = Ka, Nb // NDEV
        a_spec = pl.BlockSpec((ts, Ka), lambda h, k: (k, 0))
        b_spec = pl.BlockSpec((ts, half * C), lambda h, k: (k, h))
    else:
        R, C = Ka // NDEV, Nb
        a_spec = pl.BlockSpec((ts, half * R), lambda h, k: (k, h))
        b_spec = pl.BlockSpec((ts, Nb), lambda h, k: (k, 0))

    def body(a_ref, b_ref, o_ref):
        av = a_ref[...]
        if relu2:
            af = jnp.maximum(av.astype(F32), 0.0)
            av = (af * af).astype(_MXU)
        p = _tn(av, b_ref[...])
        first = pl.program_id(1) == 0
        for d in range(half):
            part = p[:, d * C:(d + 1) * C] if by_col else p[d * R:(d + 1) * R, :]
            _accumulate(o_ref.at[d], part, first)

    return pl.pallas_call(
        body, name=name, grid=(NDEV // half, S // ts), in_specs=[a_spec, b_spec],
        out_specs=pl.BlockSpec((half, R, C), lambda h, k: (h, 0, 0)),
        out_shape=jax.ShapeDtypeStruct((NDEV, R, C), F32),
    )(a, b)


def _nt_ln_bwd(dy, w, x, g, sc, sh, dxres, name):
    S, D = x.shape
    N = w.shape[1]
    tm = _blk(S, 2 * ROW_BLOCK)

    def body(dy_ref, w_ref, x_ref, g_ref, sc_ref, sh_ref, dxr_ref, dx_ref, dsh_ref, dsc_ref, dg_ref):
        dh = _nt(dy_ref[...], w_ref[...])
        xv = x_ref[...]
        r = lax.rsqrt(jnp.mean(xv * xv, axis=-1, keepdims=True) + EPS)
        xhat = xv * r
        gv = g_ref[...]
        dn = dh * (1.0 + sc_ref[...])
        dxhat = dn * gv
        dxv = r * (dxhat - xhat * jnp.mean(dxhat * xhat, axis=-1, keepdims=True))
        dx_ref[...] = dxr_ref[...] + dxv
        first = pl.program_id(0) == 0
        _accumulate(dsh_ref, jnp.sum(dh, axis=0, keepdims=True), first)
        _accumulate(dsc_ref, jnp.sum(dh * (xhat * gv), axis=0, keepdims=True), first)
        _accumulate(dg_ref, jnp.sum(dn * xhat, axis=0, keepdims=True), first)

    row = lambda width: pl.BlockSpec((tm, width), lambda i: (i, 0))
    vec = pl.BlockSpec((1, D), lambda i: (0, 0))
    return pl.pallas_call(
        body, name=name, grid=(S // tm,),
        in_specs=[row(N), pl.BlockSpec((D, N), lambda i: (0, 0), pipeline_mode=pl.Buffered(1)), row(D), vec, vec, vec,
                  row(D)],
        out_specs=[row(D), vec, vec, vec],
        out_shape=[jax.ShapeDtypeStruct((S, D), F32)] + [jax.ShapeDtypeStruct((1, D), F32)] * 3,
    )(dy, w, x, g, sc, sh, dxres)


def _split2(v):
    hi = v.astype(_MXU)
    mid = (v - hi.astype(F32)).astype(_MXU)
    return hi, mid


def _tri_sums(vs, tri2):
    T = vs[0].shape[0]
    out = []
    for j in range(len(vs) // 2):
        wide = [jnp.concatenate(_split2(vs[2 * j + e]), axis=1) for e in range(2)]
        for both in _per_head(_nn(jnp.concatenate(wide, axis=0), tri2), T):
            out.append((both[:, :T], both[:, T:]))
    return out


def _tri2(T, inclusive):
    j = lax.broadcasted_iota(jnp.int32, (2 * T, 2 * T), 0) % T
    s = lax.broadcasted_iota(jnp.int32, (2 * T, 2 * T), 1)
    keep = (j >= s) if inclusive else (j > s)
    return jnp.where((s >= T) | keep, 1.0, 0.0).astype(_MXU)


def _log_sigmoid(z):
    return jnp.minimum(z, 0.0) - jnp.log(1.0 + jnp.exp(-jnp.abs(z)))


def _per_head(tall, T):
    return [tall[h * T:(h + 1) * T] for h in range(tall.shape[0] // T)]


def _sb_blocks(q_tall, k2, strict, tri2, carry):
    T = k2[0].shape[0]
    zs = []
    for qt, kblk in zip(q_tall, k2):
        zs += _per_head(_nt(qt, kblk), T)
    lbs, l1s = [], []
    for z in zs:
        lb = _log_sigmoid(z)
        l1 = lb - z
        if strict is not None:
            l1 = jnp.where(strict, l1, 0.0)
        lbs.append(lb)
        l1s.append(l1)
    sums = _tri_sums(l1s, tri2)
    amps, new_carry = [], []
    for lb, (sfx, tot), c in zip(lbs, sums, carry):
        a = jnp.exp(lb + sfx + c)
        if strict is not None:
            a = jnp.where(strict, a, 0.0)
        amps.append(a)
        new_carry.append(c + tot)
    return lbs, amps, new_carry


def _sb_alive(carry):
    top = carry[0]
    for c in carry[1:]:
        top = jnp.maximum(top, c)
    return jnp.max(top) > SB_SKIP


def _skew_index():
    i = lax.broadcasted_iota(jnp.int32, (CA_T, SKEW_W + 1), 0)
    m = lax.broadcasted_iota(jnp.int32, (CA_T, SKEW_W + 1), 1)
    wrapped = i + m >= SKEW_W
    row = jnp.where(wrapped, i + 1, i)
    j = jnp.where(wrapped, i + m - SKEW_W, i + m)
    a = row // CHUNK
    jj = j - a * CHUNK
    inband = (jj >= 0) & (jj < BAND) & (j < CA_W) & (row < CA_T)
    idx = jnp.clip((row - a * CHUNK) + PAD - jj, -REL_CLIP, REL_CLIP) + REL_CLIP
    return inband, idx, wrapped


def _skew(tile):
    H = tile.shape[0]
    flat = jnp.pad(tile, ((0, 0), (0, 0), (0, SKEW_W - CA_W))).reshape(H, CA_T * SKEW_W)
    return jnp.pad(flat, ((0, 0), (0, CA_T))).reshape(H, CA_T, SKEW_W + 1)


def _unskew(view):
    H = view.shape[0]
    flat = view.reshape(H, CA_T * (SKEW_W + 1))[:, :CA_T * SKEW_W]
    return flat.reshape(H, CA_T, SKEW_W)[:, :, :CA_W]


def _ca_bias(rel_bias, name):
    H = rel_bias.shape[0]
    top = rel_bias[:, N_REL - 1:]
    by_offset = jnp.concatenate(
        [jnp.broadcast_to(top, (H, PAD - REL_CLIP + 1)), jnp.flip(rel_bias[:, :N_REL - 1], axis=1),
         jnp.broadcast_to(top, (H, SKEW_W + 1 - (PAD - REL_CLIP + 1) - (N_REL - 1)))], axis=1)

    def body(t_ref, o_ref):
        inband, _, wrapped = _skew_index()
        vals = jnp.where(wrapped, t_ref[0][:, 0:1], t_ref[0])
        o_ref[0] = jnp.where(inband, vals, NEG)

    view = pl.pallas_call(
        body, name=name, grid=(H,), in_specs=[pl.BlockSpec((1, 1, SKEW_W + 1), lambda h: (h, 0, 0))],
        out_specs=pl.BlockSpec((1, CA_T, SKEW_W + 1), lambda h: (h, 0, 0)),
        out_shape=jax.ShapeDtypeStruct((H, CA_T, SKEW_W + 1), F32),
    )(by_offset.reshape(H, 1, SKEW_W + 1))
    return _unskew(view)


def _ca_bias_bwd(dbias, name):
    H = dbias.shape[0]

    def body(d_ref, o_ref):
        inband, idx, _ = _skew_index()
        d = jnp.where(inband, d_ref[0], 0.0)
        clipped = idx == N_REL - 1
        by_offset = jnp.sum(jnp.where(clipped, 0.0, d), axis=0, keepdims=True)
        top = jnp.sum(jnp.sum(jnp.where(clipped, d, 0.0), axis=0, keepdims=True), axis=1, keepdims=True)
        lane = lax.broadcasted_iota(jnp.int32, (1, SKEW_W + 1), 1)
        o_ref[0] = jnp.where(lane == 0, top, by_offset)

    out = pl.pallas_call(
        body, name=name, grid=(H,), in_specs=[pl.BlockSpec((1, CA_T, SKEW_W + 1), lambda h: (h, 0, 0))],
        out_specs=pl.BlockSpec((1, 1, SKEW_W + 1), lambda h: (h, 0, 0)),
        out_shape=jax.ShapeDtypeStruct((H, 1, SKEW_W + 1), F32),
    )(_skew(dbias))[:, 0]
    first = PAD - REL_CLIP + 1
    return jnp.concatenate([jnp.flip(out[:, first:first + N_REL - 1], axis=1), out[:, 0:1]], axis=1)


def _low_lanes(rows):
    return lax.broadcasted_iota(jnp.int32, (rows, PAIR), 1) < HEAD_DIM


def _one_head(t2, low, first, scale=1.0):
    tf = t2.astype(F32) * scale
    return (jnp.where(low, tf, 0.0) if first else jnp.where(low, 0.0, tf)).astype(_MXU)


def _two_heads(t2, low, scale=1.0):
    return jnp.concatenate([_one_head(t2, low, True, scale), _one_head(t2, low, False, scale)], axis=0)


def _sb_fwd(proj, name, ex=None):
    S, W = proj.shape
    half = W // 6
    npair = half // PAIR
    T = _blk(S, SB_T)
    GP = _blk(npair, SB_PAIRS)
    GW = GP * PAIR
    nb = npair // GP

    def body(q_ref, k_ref, v_ref, o_ref, ox_ref):
        qi = pl.program_id(1)
        low = _low_lanes(T)
        q_tall = []
        for j in range(GP):
            q2 = q_ref[:, j * PAIR:(j + 1) * PAIR]
            q_tall.append(_two_heads(q2, low, HEAD_DIM ** -0.5))
        row = lax.broadcasted_iota(jnp.int32, (T, T), 0)
        col = lax.broadcasted_iota(jnp.int32, (T, T), 1)
        tri2 = _tri2(T, inclusive=False)

        def pairs(kb, carry, acc, fine, strict):
            rows = pl.ds(pl.multiple_of(kb * T, T), T)
            k2 = [k_ref[rows, j * PAIR:(j + 1) * PAIR] for j in range(GP)]
            v2 = [v_ref[rows, j * PAIR:(j + 1) * PAIR] for j in range(GP)]
            _, amps, carry = _sb_blocks(q_tall, k2, strict, tri2, carry)
            parts = [_split2(a) for a in amps]
            new_acc, new_fine = [], []
            for j in range(GP):
                tall = jnp.concatenate([parts[2 * j][0], parts[2 * j + 1][0], parts[2 * j][1], parts[2 * j + 1][1]],
                                       axis=0)
                hi0, hi1, mid0, mid1 = _per_head(_nn(tall, v2[j]), T)
                new_acc.append(acc[j] + jnp.where(low, hi0, hi1))
                new_fine.append(fine[j] + jnp.where(low, mid0, mid1))
            return tuple(carry), tuple(new_acc), tuple(new_fine)

        zero = (jnp.zeros((T, PAIR), F32),) * GP
        carry, acc, fine = pairs(qi, (jnp.zeros((T, T), F32),) * (2 * GP), zero, zero, col < row)

        def cond(st):
            kb, alive, _, _, _ = st
            return jnp.logical_and(kb >= 0, alive)

        def step(st):
            kb, _, carry, acc, fine = st
            carry, acc, fine = pairs(kb, carry, acc, fine, None)
            return kb - 1, _sb_alive(carry), carry, acc, fine

        _, _, _, acc, fine = lax.while_loop(cond, step, (qi - 1, _sb_alive(carry), carry, acc, fine))
        for j in range(GP):
            o_ref[:, j * PAIR:(j + 1) * PAIR] = acc[j].astype(o_ref.dtype)
            ox_ref[:, j * PAIR:(j + 1) * PAIR] = acc[j] + fine[j]

    blk = pl.BlockSpec((T, GW), lambda p, i: (i, p))
    return _call_hosted(
        body, name, (nb, S // T),
        [blk, pl.BlockSpec((S, GW), lambda p, i: (0, nb + p)), pl.BlockSpec((S, GW), lambda p, i: (0, 2 * nb + p))],
        [blk, blk], [jax.ShapeDtypeStruct((S, half), _MXU), jax.ShapeDtypeStruct((S, half), F32)],
        [], [proj, proj, proj], ex)


def _sb_bwd(proj, ox, dmixed, name, ex=None):
    S, W = proj.shape
    half = W // 6
    npair = half // PAIR
    T = _blk(S, SB_T)
    GP = _blk(npair, SB_PAIRS)
    GW = GP * PAIR
    nb = npair // GP
    last = S // T - 1
    scale = HEAD_DIM ** -0.5

    def body(q_ref, k_ref, v_ref, ox_ref, do_ref, dq_ref, dk_ref, dv_ref, dka_ref, dva_ref):
        qi = pl.program_id(1)

        @pl.when(qi == 0)
        def _():
            dka_ref[...] = jnp.zeros_like(dka_ref)
            dva_ref[...] = jnp.zeros_like(dva_ref)

        low = _low_lanes(T)
        q2, do2, q_tall, do_tall, deltas = [], [], [], [], []
        for j in range(GP):
            cols = slice(j * PAIR, (j + 1) * PAIR)
            q2.append(q_ref[:, cols])
            do2.append(do_ref[:, cols])
            q_tall.append(_two_heads(q2[j], low, scale))
            dobs = [_one_head(do2[j], low, True), _one_head(do2[j], low, False)]
            do_tall.append(jnp.concatenate(dobs, axis=0))
            for e in range(2):
                deltas.append(jnp.sum(dobs[e].astype(F32) * ox_ref[:, cols], axis=-1, keepdims=True))
        row = lax.broadcasted_iota(jnp.int32, (T, T), 0)
        col = lax.broadcasted_iota(jnp.int32, (T, T), 1)
        tri_ex = _tri2(T, inclusive=False)
        tri_in = _tri2(T, inclusive=True)

        def pairs(kb, carry, right, dq, strict):
            rows = pl.ds(pl.multiple_of(kb * T, T), T)
            k2 = [k_ref[rows, j * PAIR:(j + 1) * PAIR] for j in range(GP)]
            v2 = [v_ref[rows, j * PAIR:(j + 1) * PAIR] for j in range(GP)]
            nh = 2 * GP
            gs = []
            for j in range(GP):
                gs += _per_head(_nt(do_tall[j], v2[j]), T)
            lbs, amps, carry = _sb_blocks(q_tall, k2, strict, tri_ex, carry)
            ags = [a * gg for a, gg in zip(amps, gs)]
            sums = _tri_sums(ags, tri_in)
            dzbs = []
            for h in range(nh):
                left = deltas[h] - (sums[h][0] + right[h])
                beta = jnp.exp(lbs[h])
                dz = ags[h] - beta * (ags[h] + left)
                if strict is not None:
                    dz = jnp.where(strict, dz, 0.0)
                dzbs.append(dz.astype(_MXU))
            abs_ = [a.astype(_MXU) for a in amps]
            new_dq = []
            for j in range(GP):
                cols = slice(j * PAIR, (j + 1) * PAIR)
                dk0, dk1 = _per_head(_tn(jnp.concatenate(dzbs[2 * j:2 * j + 2], axis=1), q2[j]), T)
                dv0, dv1 = _per_head(_tn(jnp.concatenate(abs_[2 * j:2 * j + 2], axis=1), do2[j]), T)
                dq0, dq1 = _per_head(_nn(jnp.concatenate(dzbs[2 * j:2 * j + 2], axis=0), k2[j]), T)
                dka_ref[rows, cols] += jnp.where(low, dk0, dk1)
                dva_ref[rows, cols] += jnp.where(low, dv0, dv1)
                new_dq.append(dq[j] + jnp.where(low, dq0, dq1))
            right = tuple(right[h] + sums[h][1] for h in range(nh))
            return tuple(carry), right, tuple(new_dq)

        zero = (jnp.zeros((T, T), F32),) * (2 * GP)
        carry, right, dq = pairs(qi, zero, zero, (jnp.zeros((T, PAIR), F32),) * GP, col < row)

        def cond(st):
            kb, alive, _, _, _ = st
            return jnp.logical_and(kb >= 0, alive)

        def step(st):
            kb, _, carry, right, dq = st
            carry, right, dq = pairs(kb, carry, right, dq, None)
            return kb - 1, _sb_alive(carry), carry, right, dq

        _, _, _, _, dq = lax.while_loop(cond, step, (qi - 1, _sb_alive(carry), carry, right, dq))
        for j in range(GP):
            dq_ref[:, j * PAIR:(j + 1) * PAIR] = (dq[j] * scale).astype(dq_ref.dtype)

        @pl.when(qi == last)
        def _():
            dk_ref[...] = (dka_ref[...] * scale).astype(dk_ref.dtype)
            dv_ref[...] = dva_ref[...].astype(dv_ref.dtype)

    blk = pl.BlockSpec((T, GW), lambda p, i: (i, p))
    full = pl.BlockSpec((S, GW), lambda p, i: (0, p))
    return _call_hosted(
        body, name, (nb, S // T),
        [blk, pl.BlockSpec((S, GW), lambda p, i: (0, nb + p)), pl.BlockSpec((S, GW), lambda p, i: (0, 2 * nb + p)),
         blk, blk],
        [blk, full, full], [jax.ShapeDtypeStruct((S, half), _MXU)] * 3,
        [pltpu.VMEM((S, GW), F32), pltpu.VMEM((S, GW), F32)], [proj, proj, proj, ox, dmixed], ex)


def _pair_norm(t2, g2, low):
    tf = t2.astype(F32)
    sq = tf * tf
    both = jnp.sum(sq, axis=-1, keepdims=True)
    first = jnp.sum(jnp.where(low, sq, 0.0), axis=-1, keepdims=True)
    r = jnp.where(low, lax.rsqrt(first * (1.0 / HEAD_DIM) + EPS), lax.rsqrt((both - first) * (1.0 / HEAD_DIM) + EPS))
    hat = tf * r
    return hat * g2, hat, r


def _pair_norm_bwd(dn, hat, r, g2, low):
    dhat = dn * g2
    prod = dhat * hat
    both = jnp.sum(prod, axis=-1, keepdims=True)
    first = jnp.sum(jnp.where(low, prod, 0.0), axis=-1, keepdims=True)
    mean = jnp.where(low, first, both - first) * (1.0 / HEAD_DIM)
    return r * (dhat - hat * mean)


def _ca_fill(j, k_ref, v_ref, gk_ref, kn_ref, vp_ref):
    S = k_ref.shape[0]
    cols = slice(j * PAIR, (j + 1) * PAIR)
    kn, _, _ = _pair_norm(k_ref[:, cols], gk_ref[...], _low_lanes(S))
    kn_ref[j, 0:PAD, :] = jnp.zeros((PAD, PAIR), kn_ref.dtype)
    vp_ref[j, 0:PAD, :] = jnp.zeros((PAD, PAIR), vp_ref.dtype)
    kn_ref[j, PAD:PAD + S, :] = kn.astype(kn_ref.dtype)
    vp_ref[j, PAD:PAD + S, :] = v_ref[:, cols]


def _ca_scores(j, q_ref, b2_ref, gq_ref, kn_ref, qi, low):
    qn, qhat, r = _pair_norm(q_ref[:, j * PAIR:(j + 1) * PAIR], gq_ref[...], low)
    qn = qn * HEAD_DIM ** -0.5
    band = pl.ds(pl.multiple_of(qi * CA_T, CA_T), CA_W)
    key_pos = qi * CA_T - PAD + lax.broadcasted_iota(jnp.int32, (CA_T, CA_W), 1)
    both = _per_head(_nt(_two_heads(qn, low), kn_ref[j, band, :]), CA_T)
    scores = [jnp.where(key_pos >= 0, both[e] + b2_ref[2 * j + e], NEG) for e in range(2)]
    return scores, qn.astype(_MXU), qhat, r


def _softmax(s):
    e = jnp.exp(s - jnp.max(s, axis=-1, keepdims=True))
    return e * (1.0 / jnp.sum(e, axis=-1, keepdims=True))


def _ca_fwd(proj, bias2, gq2, gk2, name, ex=None):
    S, W = proj.shape
    half = W // 6
    npair = half // PAIR
    GP = _blk(npair, CA_PAIRS_FWD)
    GW = GP * PAIR
    nb = npair // GP

    def body(q_ref, k_ref, v_ref, b2_ref, gq_ref, gk_ref, o_ref, kn_ref, vp_ref):
        qi = pl.program_id(1)

        @pl.when(qi == 0)
        def _():
            for j in range(GP):
                _ca_fill(j, k_ref, v_ref, gk_ref, kn_ref, vp_ref)

        low = _low_lanes(CA_T)
        band = pl.ds(pl.multiple_of(qi * CA_T, CA_T), CA_W)
        scores = [_ca_scores(j, q_ref, b2_ref, gq_ref, kn_ref, qi, low)[0] for j in range(GP)]
        probs = [[_softmax(s).astype(_MXU) for s in pair] for pair in scores]
        for j in range(GP):
            outs = _per_head(_nn(jnp.concatenate(probs[j], axis=0), vp_ref[j, band, :]), CA_T)
            o_ref[:, j * PAIR:(j + 1) * PAIR] = jnp.where(low, outs[0], outs[1]).astype(o_ref.dtype)

    vec = pl.BlockSpec((1, PAIR), lambda p, i: (0, 0))
    return _call_hosted(
        body, name, (nb, S // CA_T),
        [pl.BlockSpec((CA_T, GW), lambda p, i: (i, 3 * nb + p)),
         pl.BlockSpec((S, GW), lambda p, i: (0, 4 * nb + p)), pl.BlockSpec((S, GW), lambda p, i: (0, 5 * nb + p)),
         pl.BlockSpec((2 * GP, CA_T, CA_W), lambda p, i: (p, 0, 0)), vec, vec],
        [pl.BlockSpec((CA_T, GW), lambda p, i: (i, p))], [jax.ShapeDtypeStruct((S, half), _MXU)],
        [pltpu.VMEM((GP, PAD + S, PAIR), _MXU), pltpu.VMEM((GP, PAD + S, PAIR), _MXU)],
        [proj, proj, proj, bias2, gq2, gk2], ex)


def _ca_bwd(proj, bias2, gq2, gk2, dmixed, name, ex=None):
    S, W = proj.shape
    half = W // 6
    npair = half // PAIR
    GP = _blk(npair, CA_PAIRS_BWD)
    GW = GP * PAIR
    nb = npair // GP
    scale = HEAD_DIM ** -0.5
    last = S // CA_T - 1

    def body(q_ref, k_ref, v_ref, b2_ref, gq_ref, gk_ref, do_ref,
             dq_ref, dk_ref, dv_ref, db_ref, dgq_ref, dgk_ref, kn_ref, vp_ref, dkn_ref, dvp_ref):
        p_id, qi = pl.program_id(0), pl.program_id(1)

        @pl.when(qi == 0)
        def _():
            for j in range(GP):
                _ca_fill(j, k_ref, v_ref, gk_ref, kn_ref, vp_ref)
            dkn_ref[...] = jnp.zeros_like(dkn_ref)
            dvp_ref[...] = jnp.zeros_like(dvp_ref)
            db_ref[...] = jnp.zeros_like(db_ref)

        @pl.when(jnp.logical_and(p_id == 0, qi == 0))
        def _():
            dgq_ref[...] = jnp.zeros_like(dgq_ref)
            dgk_ref[...] = jnp.zeros_like(dgk_ref)

        low = _low_lanes(CA_T)
        top_w = lax.broadcasted_iota(jnp.int32, (PAIR, CA_W), 0) < HEAD_DIM
        band = pl.ds(pl.multiple_of(qi * CA_T, CA_T), CA_W)
        pairs = [_ca_scores(j, q_ref, b2_ref, gq_ref, kn_ref, qi, low) for j in range(GP)]
        do2 = [do_ref[:, j * PAIR:(j + 1) * PAIR] for j in range(GP)]
        dps = [_per_head(_nt(_two_heads(do2[j], low), vp_ref[j, band, :]), CA_T) for j in range(GP)]
        probs, dsbs = [], []
        for j in range(GP):
            pj, dj = [], []
            for e in range(2):
                p = _softmax(pairs[j][0][e])
                ds = p * (dps[j][e] - jnp.sum(p * dps[j][e], axis=-1, keepdims=True))
                db_ref[2 * j + e] += ds
                pj.append(p.astype(_MXU))
                dj.append(ds.astype(_MXU))
            probs.append(pj)
            dsbs.append(dj)
        dgq = jnp.zeros((1, PAIR), F32)
        for j in range(GP):
            _, qn, qhat, r = pairs[j]
            dq_h = _per_head(_nn(jnp.concatenate(dsbs[j], axis=0), kn_ref[j, band, :]), CA_T)
            dk_t = _tn(qn, jnp.concatenate(dsbs[j], axis=1))
            dv_t = _tn(do2[j], jnp.concatenate(probs[j], axis=1))
            dkn_ref[j, :, band] += jnp.where(top_w, dk_t[:, :CA_W], dk_t[:, CA_W:])
            dvp_ref[j, :, band] += jnp.where(top_w, dv_t[:, :CA_W], dv_t[:, CA_W:])
            dqn = jnp.where(low, dq_h[0], dq_h[1]) * scale
            dgq = dgq + jnp.sum(dqn * qhat, axis=0, keepdims=True)
            dq_ref[:, j * PAIR:(j + 1) * PAIR] = _pair_norm_bwd(dqn, qhat, r, gq_ref[...], low).astype(dq_ref.dtype)
        dgq_ref[...] += dgq

        @pl.when(qi == last)
        def _():
            low_s = _low_lanes(S)
            for j in range(GP):
                cols = slice(j * PAIR, (j + 1) * PAIR)
                _, khat, rk = _pair_norm(k_ref[:, cols], gk_ref[...], low_s)
                dkn = dkn_ref[j, :, PAD:PAD + S].T
                dgk_ref[...] += jnp.sum(dkn * khat, axis=0, keepdims=True)
                dk_ref[:, cols] = _pair_norm_bwd(dkn, khat, rk, gk_ref[...], low_s).astype(dk_ref.dtype)
                dv_ref[:, cols] = dvp_ref[j, :, PAD:PAD + S].T.astype(dv_ref.dtype)

    vec = pl.BlockSpec((1, PAIR), lambda p, i: (0, 0))
    tile = pl.BlockSpec((2 * GP, CA_T, CA_W), lambda p, i: (p, 0, 0))
    full = pl.BlockSpec((S, GW), lambda p, i: (0, p))
    return _call_hosted(
        body, name, (nb, S // CA_T),
        [pl.BlockSpec((CA_T, GW), lambda p, i: (i, 3 * nb + p)),
         pl.BlockSpec((S, GW), lambda p, i: (0, 4 * nb + p)), pl.BlockSpec((S, GW), lambda p, i: (0, 5 * nb + p)),
         tile, vec, vec, pl.BlockSpec((CA_T, GW), lambda p, i: (i, nb + p))],
        [pl.BlockSpec((CA_T, GW), lambda p, i: (i, p)), full, full, tile, vec, vec],
        [jax.ShapeDtypeStruct((S, half), _MXU)] * 3
        + [jax.ShapeDtypeStruct(bias2.shape, F32), jax.ShapeDtypeStruct((1, PAIR), F32),
           jax.ShapeDtypeStruct((1, PAIR), F32)],
        [pltpu.VMEM((GP, PAD + S, PAIR), _MXU), pltpu.VMEM((GP, PAD + S, PAIR), _MXU),
         pltpu.VMEM((GP, PAIR, PAD + S), F32), pltpu.VMEM((GP, PAIR, PAD + S), F32)],
        [proj, proj, proj, bias2, gq2, gk2, dmixed], ex)


def _pack_small(parts):
    flat = jnp.concatenate([p.reshape(-1) for layer in parts for p in layer])
    n = flat.shape[0]
    n_pad = -(-n // 1024) * 1024
    return jnp.pad(flat, (0, n_pad - n)).reshape(1, n_pad)


def _unpack_small(flat, shapes):
    out, off = [], 0
    for layer in shapes:
        cur = []
        for shp in layer:
            size = 1
            for s in shp:
                size *= s
            cur.append(flat[off:off + size].reshape(shp))
            off += size
        out.append(cur)
    return out


def kernel(x, c, g_norm1, w_in, g_q, g_k, rel_bias, w_o, g_norm2, w1, w2, w_ada, b_ada, loss_target, m_g_norm1, m_w_in, m_g_q, m_g_k, m_rel_bias, m_w_o, m_g_norm2, m_w1, m_w2, m_w_ada, m_b_ada, v_g_norm1, v_w_in, v_g_q, v_g_k, v_rel_bias, v_w_o, v_g_norm2, v_w1, v_w2, v_w_ada, v_b_ada):
    L = w_in.shape[0]
    S, D = x.shape[1:]
    H2 = D // HEAD_DIM // 2
    Ca = w_ada.shape[2]
    xi, yi, ci = _pos()
    me = 4 * xi + 2 * yi + ci
    place = jnp.stack([2 * xi + yi, ci]).astype(jnp.int32)

    wire = lambda a: a.astype(_MXU)
    by_cols = lambda g: g.transpose(1, 0, 2).reshape(D, g.shape[0] * g.shape[2])
    b_cols = lax.dynamic_slice(b_ada, (0, me * Ca), (L, Ca))
    c_all, mod_all, first = _start(c, w_ada, b_cols, wire(w_in[0]), "start")
    c_all = c_all.reshape(NDEV, D)
    mod = lax.dynamic_index_in_dim(mod_all, me, axis=1, keepdims=False)
    mod = mod.reshape(NDEV, L, Ca).transpose(1, 0, 2).reshape(L, 6, 1, D)
    W_in = {0: by_cols(first)}
    W_o, W_1, W_2 = {}, {}, {}

    xs = [x[0]]
    saved = []
    for l in range(L):
        sh1, sc1, gt1, sh2, sc2, gt2 = [mod[l, i] for i in range(6)]
        gn1, gn2 = g_norm1[l:l + 1], g_norm2[l:l + 1]
        gq2, gk2 = jnp.tile(g_q[l:l + 1], (1, 2)), jnp.tile(g_k[l:l + 1], (1, 2))
        proj, h1 = _ln_mod_matmul(xs[-1], gn1, sc1, sh1, W_in[l], f"l{l}_proj")
        (o_sb, ox_sb), got = _sb_fwd(proj, f"l{l}_sb_fwd", _gather_exchange([wire(w1[l]), wire(w2[l])]))
        W_1[l], W_2[l] = by_cols(got[0]), got[1].reshape(4 * D, D)
        bias2 = _ca_bias(rel_bias[l], f"l{l}_ca_bias")
        nxt = [wire(w_in[l + 1])] if l + 1 < L else []
        (o_ca,), got = _ca_fwd(proj, bias2, gq2, gk2, f"l{l}_ca_fwd", _gather_exchange([wire(w_o[l])] + nxt))
        W_o[l] = got[0].reshape(D, D)
        if nxt:
            W_in[l + 1] = by_cols(got[1])
        mixed = jnp.concatenate([o_sb, o_ca], axis=1)
        x1, f1, u, h2 = _attn_out_mlp_in(mixed, W_o[l], xs[-1], gt1, gn2, sc2, sh2, W_1[l], f"l{l}_attn_out_mlp_in")
        x0 = xs[-1]
        if l + 1 < L:
            x2, f2 = _matmul_res_gate(u, W_2[l], x1, gt2, True, f"l{l}_mlp_out")
            xs.append(x2)
        else:
            dx, f2, loss_part = _matmul_res_gate(u, W_2[l], x1, gt2, True, f"l{l}_mlp_out", loss_target[0])
        saved.append(dict(x0=x0, h1=h1, proj=proj, ox_sb=ox_sb, bias2=bias2, mixed=mixed, f1=f1, x1=x1,
                          h2=h2, u=u, f2=f2))

    owns, recv_b = {}, {}
    ready = []
    small_parts = [None] * L

    def partials(keys, grads, recv_a):
        parts = []
        for key, g, r in zip(keys, grads, recv_a):
            owns[key], part = _rs_chip_partial(place, g, r, f"rs_partial_l{key[0]}_{key[1]}")
            parts.append(part)
        return parts

    for l in reversed(range(L)):
        sv = saved[l]
        sh1, sc1, gt1, sh2, sc2, gt2 = [mod[l, i] for i in range(6)]
        gn1, gn2 = g_norm1[l:l + 1], g_norm2[l:l + 1]
        gq2, gk2 = jnp.tile(g_q[l:l + 1], (1, 2)), jnp.tile(g_k[l:l + 1], (1, 2))
        dz2, dgt2, du = _gate_nt_matmul(dx, sv["f2"], gt2, W_2[l], sv["u"], f"l{l}_mlp_out_bwd")
        gw2 = _tn_matmul(sv["u"], dz2, False, True, f"l{l}_gw2")
        gw1 = _tn_matmul(sv["h2"], du, True, False, f"l{l}_gw1")
        dx, dsh2, dsc2, dgn2, dz1, dgt1, dmixed = _mlp_in_attn_out_bwd(
            du, W_1[l], sv["x1"], gn2, sc2, sh2, dx, sv["f1"], gt1, W_o[l], f"l{l}_mlp_in_attn_out_bwd")
        gwo = _tn_matmul(sv["mixed"], dz1, False, False, f"l{l}_gwo")
        ready += [((l, 1), gwo), ((l, 2), gw1), ((l, 3), gw2)]
        keys, grads = [k for k, _ in ready], [g for _, g in ready]
        (dq_sb, dk_sb, dv_sb), recv_a = _sb_bwd(sv["proj"], sv["ox_sb"], dmixed, f"l{l}_sb_bwd",
                                                _sibling_exchange(grads))
        parts = partials(keys, grads, recv_a)
        (dq_ca, dk_ca, dv_ca, dbias2, dgq2, dgk2), got = _ca_bwd(sv["proj"], sv["bias2"], gq2, gk2, dmixed,
                                                                 f"l{l}_ca_bwd", _chip_exchange(parts))
        recv_b.update(zip(keys, got))
        dgq = dgq2[:, :HEAD_DIM] + dgq2[:, HEAD_DIM:]
        dgk = dgk2[:, :HEAD_DIM] + dgk2[:, HEAD_DIM:]
        drb = _ca_bias_bwd(dbias2, f"l{l}_ca_bias_bwd")
        dproj = jnp.concatenate([dq_sb, dk_sb, dv_sb, dq_ca, dk_ca, dv_ca], axis=1)
        gwin = _tn_matmul(sv["h1"], dproj, True, False, f"l{l}_gwin")
        ready = [((l, 0), gwin)]
        dx, dsh1, dsc1, dgn1 = _nt_ln_bwd(dproj, W_in[l], sv["x0"], gn1, sc1, sh1, dx, f"l{l}_proj_bwd")
        dmod = jnp.concatenate([dsh1, dsc1, dgt1, dsh2, dsc2, dgt2], axis=1)
        small_parts[l] = [dgn1, dgq, dgk, drb, dgn2, dmod]
    grad_x = dx[None]

    keys, grads = [k for k, _ in ready], [g for _, g in ready]
    parts = partials(keys, grads, _run_exchange(_sibling_exchange(grads), "rs_sibling_last"))
    recv_b.update(zip(keys, _run_exchange(_chip_exchange(parts), "rs_chips_last")))
    big_out = []
    for t, (w, m, v) in enumerate([(w_in, m_w_in, v_w_in), (w_o, m_w_o, v_w_o), (w1, m_w1, v_w1), (w2, m_w2, v_w2)]):
        big_out.append(_rs_sum_adamw([owns[(l, t)] for l in range(L)], [recv_b[(l, t)] for l in range(L)],
                                     w, m, v, f"adamw_big_{t}"))

    packed = _pack_small(small_parts)
    gathered_small = _all_gather_small(packed, "ag_small_grads")
    small_sum = _sum_devices(gathered_small, "sum_small_grads")
    shapes = [[(1, D), (1, HEAD_DIM), (1, HEAD_DIM), (H2, N_REL), (1, D), (1, 6 * D)]] * L
    names = ["g_norm1", "g_q", "g_k", "rel_bias", "g_norm2", "b_ada"]
    small_w = {"g_norm1": (g_norm1, m_g_norm1, v_g_norm1), "g_q": (g_q, m_g_q, v_g_q), "g_k": (g_k, m_g_k, v_g_k),
               "rel_bias": (rel_bias, m_rel_bias, v_rel_bias), "g_norm2": (g_norm2, m_g_norm2, v_g_norm2),
               "b_ada": (b_ada, m_b_ada, v_b_ada)}
    packs = [_pack_small([[small_w[n][k][l] for n in names] for l in range(L)]) for k in range(3)]
    n_pad = packed.shape[1]
    as_rows = lambda a: a.reshape(n_pad // 128, 128)
    sd, sm, sv_ = _adamw(as_rows(packs[0]), as_rows(small_sum), as_rows(packs[1]), as_rows(packs[2]), "adamw_small")
    small_out = {}
    for key, flat in [("grad", small_sum), ("delta", sd), ("m", sm), ("v", sv_)]:
        per_layer = _unpack_small(flat.reshape(-1), shapes)
        for i, n in enumerate(names):
            small_out[(key, n)] = jnp.stack([per_layer[l][i].reshape(small_w[n][0].shape[1:]) for l in range(L)])

    layer_len = 2 * D + 2 * HEAD_DIM + H2 * N_REL + 6 * D
    rows = gathered_small.reshape(NDEV, n_pad)
    dmod_all = jnp.stack([rows[:, l * layer_len + layer_len - 6 * D:(l + 1) * layer_len] for l in range(L)])
    dmod_cols = lax.dynamic_slice(dmod_all, (0, 0, me * Ca), (L, NDEV, Ca))
    dmod_cols = jnp.pad(dmod_cols, ((0, 0), (0, 128 - NDEV), (0, 0)))
    c_t = jnp.pad(c_all.T, ((0, 0), (0, 128 - NDEV)))
    g_ada = _w_ada_grad(c_t, dmod_cols, "w_ada_grad")
    flat2 = lambda a: a.reshape(L * D, Ca)
    ad, am, av = _adamw(flat2(w_ada), flat2(g_ada), flat2(m_w_ada), flat2(v_w_ada), "adamw_w_ada")
    ada_out = [g_ada] + [a.reshape(L, D, Ca) for a in (ad, am, av)]

    def leaf(kind):
        k = {"grad": 0, "delta": 1, "m": 2, "v": 3}[kind]
        return [small_out[(kind, "g_norm1")], big_out[0][k], small_out[(kind, "g_q")], small_out[(kind, "g_k")],
                small_out[(kind, "rel_bias")], big_out[1][k], small_out[(kind, "g_norm2")], big_out[2][k],
                big_out[3][k], ada_out[k], small_out[(kind, "b_ada")]]

    loss = lax.psum(loss_part[0, 0], ("x", "y", "c"))
    return (loss, grad_x, *leaf("grad"), *leaf("delta"), *leaf("m"), *leaf("v"))
```
